```python
import math
import jax, jax.numpy as jnp
from jax import lax
import numpy as np

D_MODEL = 1024
BATCH = 8
SEQ = 2048
DEPTH = 4

PLE_DIM = 256
Q_BLOCK = 128
NORM_EPS = 1e-6

MLA_HEADS = 8
MLA_NOPE = 64
MLA_ROPE = 32
MLA_V = 64
MLA_Q_LORA = 384
MLA_KV_LORA = 256
MLA_WIDTH = MLA_HEADS * MLA_V
ROPE_THETA = 10000.0

SSM_HEADS = 16
SSM_HEAD_DIM = 64
SSM_INNER = SSM_HEADS * SSM_HEAD_DIM
SSM_GROUPS = 2
SSM_STATE = 128
SSM_CONV = 4
SSM_CHUNK = 128
SSM_CONV_DIM = SSM_INNER + 2 * SSM_GROUPS * SSM_STATE

DSA_HEADS = 8
DSA_HEAD_DIM = 64
DSA_WIDTH = DSA_HEADS * DSA_HEAD_DIM
IDX_HEADS = 8
IDX_DIM = 64
TOPK_MAX = 256

REL_BUCKETS = 32
REL_MAX_DIST = 128

N_BRANCHES = 3
SPLIT_SIZES = (
    MLA_Q_LORA,
    MLA_KV_LORA,
    MLA_ROPE,
    MLA_WIDTH,
    SSM_INNER,
    SSM_CONV_DIM,
    SSM_HEADS,
    DSA_WIDTH,
    DSA_HEAD_DIM,
    DSA_HEAD_DIM,
    IDX_HEADS * IDX_DIM,
    IDX_DIM,
    IDX_HEADS,
    DSA_WIDTH,
    N_BRANCHES * D_MODEL,
)
IN_TOTAL = sum(SPLIT_SIZES)

kernel_name = 'hybrid_mla_ssd_dsa_gated_merge'


def rms_norm(x, g):
    xf = x.astype(jnp.float32)
    y = xf * lax.rsqrt(jnp.mean(xf * xf, axis=-1, keepdims=True) + NORM_EPS)
    return (y * g.astype(jnp.float32)).astype(x.dtype)


def apply_rope(x, cos, sin):
    xf = x.astype(jnp.float32)
    x1, x2 = jnp.split(xf, 2, axis=-1)
    return jnp.concatenate([x1 * cos - x2 * sin, x2 * cos + x1 * sin], axis=-1).astype(x.dtype)


def t5_bucket(rel):
    n = jnp.maximum(rel, 0)
    exact = REL_BUCKETS // 2
    log_ratio = jnp.log(jnp.maximum(n, exact).astype(jnp.float32) / exact) / math.log(REL_MAX_DIST / exact)
    large = jnp.minimum(exact + (log_ratio * (REL_BUCKETS - exact)).astype(jnp.int32), REL_BUCKETS - 1)
    return jnp.where(n < exact, n, large)


def mla_branch(c_q, c_kv, k_rope, q_norm, w_uq, kv_norm, w_ukv, cos, sin):
    b, s, _ = c_q.shape
    q = (rms_norm(c_q, q_norm) @ w_uq).reshape(b, s, MLA_HEADS, MLA_NOPE + MLA_ROPE)
    q_nope = q[..., :MLA_NOPE]
    q_rope = apply_rope(q[..., MLA_NOPE:], cos[:, :, None], sin[:, :, None])
    kv = (rms_norm(c_kv, kv_norm) @ w_ukv).reshape(b, s, MLA_HEADS, MLA_NOPE + MLA_V)
    k_nope, v = kv[..., :MLA_NOPE], kv[..., MLA_NOPE:]
    k_rope = apply_rope(k_rope, cos, sin)
    scale = (MLA_NOPE + MLA_ROPE) ** -0.5
    outs = []
    for blk in range(s // Q_BLOCK):
        qs, qe = blk * Q_BLOCK, (blk + 1) * Q_BLOCK
        logits = (jnp.einsum('bqhd,bkhd->bhqk', q_nope[:, qs:qe], k_nope[:, :qe])
                  + jnp.einsum('bqhd,bkd->bhqk', q_rope[:, qs:qe], k_rope[:, :qe])).astype(jnp.float32) * scale
        causal = jnp.arange(qs, qe)[:, None] >= jnp.arange(qe)[None, :]
        probs = jax.nn.softmax(jnp.where(causal, logits, -jnp.inf), axis=-1).astype(v.dtype)
        outs.append(jnp.einsum('bhqk,bkhd->bqhd', probs, v[:, :qe]))
    return jnp.concatenate(outs, axis=1).reshape(b, s, MLA_WIDTH)


def ssd_chunked(x, dt, a, bm, cm, d_skip):
    b, s, h, pdim = x.shape
    nc, q = s // SSM_CHUNK, SSM_CHUNK
    hg = SSM_HEADS // SSM_GROUPS
    xc = x.reshape(b, nc, q, SSM_GROUPS, hg, pdim)
    dtc = dt.reshape(b, nc, q, SSM_GROUPS, hg)
    xdt = xc * dtc[..., None]
    bc = bm.reshape(b, nc, q, SSM_GROUPS, SSM_STATE)
    cc = cm.reshape(b, nc, q, SSM_GROUPS, SSM_STATE)
    cum = jnp.cumsum(jnp.moveaxis(dtc * a.reshape(SSM_GROUPS, hg), 2, -1), axis=-1)
    tril = jnp.tril(jnp.ones((q, q), dtype=bool))
    decay_in = jnp.exp(jnp.where(tril, cum[..., :, None] - cum[..., None, :], -jnp.inf))
    cb = jnp.einsum('bclgn,bcsgn->bcgls', cc, bc)
    y_diag = jnp.einsum('bcghls,bcsghp->bclghp', cb[:, :, :, None] * decay_in, xdt)
    decay_to_end = jnp.exp(cum[..., -1:] - cum)
    chunk_states = jnp.einsum('bcsgn,bcghs,bcsghp->bcghpn', bc, decay_to_end, xdt)
    chunk_decay = jnp.exp(cum[..., -1])

    def step(state, inp):
        st, dec = inp
        return state * dec[..., None, None] + st, state

    init = jnp.zeros((b, SSM_GROUPS, hg, pdim, SSM_STATE), x.dtype)
    _, prev = lax.scan(step, init, (jnp.moveaxis(chunk_states, 1, 0), jnp.moveaxis(chunk_decay, 1, 0)))
    prev = jnp.moveaxis(prev, 0, 1)
    y_off = jnp.einsum('bclgn,bcghpn,bcghl->bclghp', cc, prev, jnp.exp(cum))
    y = y_diag + y_off + xc * d_skip.reshape(SSM_GROUPS, hg)[..., None]
    return y.reshape(b, s, h * pdim)


def ssd_branch(z, xbc, dt_raw, conv_w, conv_b, dt_bias, a_log, d_skip, ssm_norm):
    b, s, _ = xbc.shape
    xbc = lax.conv_general_dilated(xbc, conv_w.astype(xbc.dtype)[:, None, :], window_strides=(1,),
                                   padding=[(SSM_CONV - 1, 0)], dimension_numbers=('NWC', 'WIO', 'NWC'),
                                   feature_group_count=SSM_CONV_DIM)
    xbc = jax.nn.silu(xbc + conv_b)
    xs, bm, cm = jnp.split(xbc, [SSM_INNER, SSM_INNER + SSM_GROUPS * SSM_STATE], axis=-1)
    dt = jax.nn.softplus(dt_raw.astype(jnp.float32) + dt_bias.astype(jnp.float32))
    a = -jnp.exp(a_log.astype(jnp.float32))
    y = ssd_chunked(xs.reshape(b, s, SSM_HEADS, SSM_HEAD_DIM).astype(jnp.float32), dt, a,
                    bm.reshape(b, s, SSM_GROUPS, SSM_STATE).astype(jnp.float32),
                    cm.reshape(b, s, SSM_GROUPS, SSM_STATE).astype(jnp.float32),
                    d_skip.astype(jnp.float32)).astype(z.dtype)
    return rms_norm(y * jax.nn.silu(z), ssm_norm)


def dsa_branch(q_c, k_c, v_c, q_idx, k_idx, w_idx, positions, rel_bias):
    b, s, _ = q_c.shape
    n_sel = min(TOPK_MAX, s // 4)
    q_c = q_c.reshape(b, s, DSA_HEADS, DSA_HEAD_DIM)
    q_idx = q_idx.reshape(b, s, IDX_HEADS, IDX_DIM)
    gather = jax.vmap(lambda arr, ids: arr[ids])
    key_ids = jnp.arange(s)
    scale = DSA_HEAD_DIM ** -0.5
    outs = []
    for blk in range(s // Q_BLOCK):
        qs, qe = blk * Q_BLOCK, (blk + 1) * Q_BLOCK
        q_ids = jnp.arange(qs, qe)
        rel = jax.nn.relu(jnp.einsum('bqhd,bkd->bqhk', q_idx[:, qs:qe], k_idx).astype(jnp.float32))
        index_score = jnp.einsum('bqh,bqhk->bqk', w_idx[:, qs:qe].astype(jnp.float32), rel)
        index_score = jnp.where(key_ids[None, None, :] <= q_ids[None, :, None], index_score, -jnp.inf)
        _, sel = lax.top_k(index_score, n_sel)
        k_sel = gather(k_c, sel)
        v_sel = gather(v_c, sel)
        pos_sel = gather(positions, sel)
        logits = jnp.einsum('bqhd,bqkd->bhqk', q_c[:, qs:qe], k_sel).astype(jnp.float32) * scale
        bucket = t5_bucket(positions[:, qs:qe, None] - pos_sel)
        logits = logits + jnp.moveaxis(rel_bias[bucket].astype(jnp.float32), -1, 1)
        valid = sel <= q_ids[None, :, None]
        probs = jax.nn.softmax(jnp.where(valid[:, None], logits, -jnp.inf), axis=-1).astype(v_c.dtype)
        outs.append(jnp.einsum('bhqk,bqkd->bqhd', probs, v_sel))
    return jnp.concatenate(outs, axis=1).reshape(b, s, DSA_WIDTH)


def setup_inputs(seed: int = 0) -> dict:
    key = jax.random.key(seed)
    ks = jax.random.split(key, 24)
    f32 = jnp.float32

    def nrm(k, shape, scale):
        return jax.random.normal(k, shape, f32) * scale

    def gain(k, shape):
        return 1.0 + 0.1 * jax.random.normal(k, shape, f32)

    dt0 = jnp.exp(jax.random.uniform(ks[12], (DEPTH, SSM_HEADS), f32) * (math.log(0.1) - math.log(0.001)) + math.log(0.001))
    return {
        'x': jax.random.normal(ks[0], (BATCH, SEQ, D_MODEL), f32),
        'p': jax.random.normal(ks[1], (DEPTH, BATCH, SEQ, PLE_DIM), f32),
        'positions': (jnp.arange(SEQ, dtype=jnp.int32)[None, :]
                      + jax.random.randint(ks[2], (BATCH, 1), 0, 1024, dtype=jnp.int32)),
        'norm_g': gain(ks[3], (DEPTH, D_MODEL)),
        'w_in': nrm(ks[4], (DEPTH, D_MODEL, IN_TOTAL), D_MODEL ** -0.5),
        'mla_q_norm': gain(ks[5], (DEPTH, MLA_Q_LORA)),
        'w_uq': nrm(ks[6], (DEPTH, MLA_Q_LORA, MLA_HEADS * (MLA_NOPE + MLA_ROPE)), MLA_Q_LORA ** -0.5),
        'mla_kv_norm': gain(ks[7], (DEPTH, MLA_KV_LORA)),
        'w_ukv': nrm(ks[8], (DEPTH, MLA_KV_LORA, MLA_HEADS * (MLA_NOPE + MLA_V)), MLA_KV_LORA ** -0.5),
        'conv_w': nrm(ks[9], (DEPTH, SSM_CONV, SSM_CONV_DIM), SSM_CONV ** -0.5),
        'conv_b': nrm(ks[10], (DEPTH, SSM_CONV_DIM), 0.01),
        'dt_bias': dt0 + jnp.log(-jnp.expm1(-dt0)),
        'a_log': jnp.log(jax.random.uniform(ks[11], (DEPTH, SSM_HEADS), f32, minval=1.0, maxval=16.0)),
        'd_skip': gain(ks[13], (DEPTH, SSM_HEADS)),
        'ssm_norm': gain(ks[14], (DEPTH, SSM_INNER)),
        'w_br_a': nrm(ks[15], (DEPTH, MLA_WIDTH, D_MODEL), MLA_WIDTH ** -0.5),
        'w_br_b': nrm(ks[16], (DEPTH, SSM_INNER, D_MODEL), SSM_INNER ** -0.5),
        'w_br_c': nrm(ks[17], (DEPTH, DSA_WIDTH, D_MODEL), DSA_WIDTH ** -0.5),
        'w_out': nrm(ks[18], (DEPTH, D_MODEL, D_MODEL), D_MODEL ** -0.5),
        'rel_bias': nrm(ks[19], (REL_BUCKETS, DSA_HEADS), 0.5),
        'w_ple': nrm(ks[20], (DEPTH, PLE_DIM, D_MODEL), PLE_DIM ** -0.5),
        'w_ple_gate': nrm(ks[21], (DEPTH, D_MODEL, D_MODEL), D_MODEL ** -0.5),
        'final_norm': gain(ks[22], (D_MODEL,)),
    }


def reference(x, p, positions, norm_g, w_in, mla_q_norm, w_uq, mla_kv_norm, w_ukv, conv_w, conv_b,
              dt_bias, a_log, d_skip, ssm_norm, w_br_a, w_br_b, w_br_c, w_out, rel_bias, w_ple,
              w_ple_gate, final_norm):
    inv_freq = 1.0 / (ROPE_THETA ** (jnp.arange(0, MLA_ROPE, 2, dtype=jnp.float32) / MLA_ROPE))
    ang = positions.astype(jnp.float32)[..., None] * inv_freq
    cos, sin = jnp.cos(ang), jnp.sin(ang)
    split_at = np.cumsum(SPLIT_SIZES)[:-1].tolist()
    for i in range(DEPTH):
        h = rms_norm(x, norm_g[i])
        (c_q, c_kv, k_rope, gate_a, z, xbc, dt_raw, q_c, k_c, v_c, q_idx, k_idx, w_idx, gate_c,
         merge_logits) = jnp.split(h @ w_in[i], split_at, axis=-1)
        o_a = mla_branch(c_q, c_kv, k_rope, mla_q_norm[i], w_uq[i], mla_kv_norm[i], w_ukv[i], cos, sin) * jax.nn.silu(gate_a)
        o_b = ssd_branch(z, xbc, dt_raw, conv_w[i], conv_b[i], dt_bias[i], a_log[i], d_skip[i], ssm_norm[i])
        o_c = dsa_branch(q_c, k_c, v_c, q_idx, k_idx, w_idx, positions, rel_bias) * jax.nn.silu(gate_c)
        g_a, g_b, g_c = jnp.split(jax.nn.sigmoid(merge_logits), N_BRANCHES, axis=-1)
        merged = g_a * (o_a @ w_br_a[i]) + g_b * (o_b @ w_br_b[i]) + g_c * (o_c @ w_br_c[i])
        x = x + merged @ w_out[i]
        x = x + jax.nn.sigmoid(x @ w_ple_gate[i]) * (p[i] @ w_ple[i])
    return rms_norm(x, final_norm)
```

```python
import functools
import math

import jax
import jax.numpy as jnp
from jax import lax
from jax.experimental import pallas as pl
from jax.experimental.pallas import tpu as pltpu

F32 = jnp.float32
BF16 = jnp.bfloat16
I32 = jnp.int32

D_MODEL = 1024
PLE_DIM = 256
NORM_EPS = 1e-6

MLA_HEADS = 8
MLA_NOPE = 64
MLA_ROPE = 32
MLA_V = 64
MLA_Q_LORA = 384
MLA_KV_LORA = 256
MLA_WIDTH = MLA_HEADS * MLA_V
ROPE_THETA = 10000.0
MLA_QK_PAD = 128

SSM_HEADS = 16
SSM_HEAD_DIM = 64
SSM_INNER = SSM_HEADS * SSM_HEAD_DIM
SSM_GROUPS = 2
SSM_STATE = 128
SSM_CONV = 4
SSM_CHUNK = 128
SSM_CONV_DIM = SSM_INNER + 2 * SSM_GROUPS * SSM_STATE
SSM_HEADS_PER_GROUP = SSM_HEADS // SSM_GROUPS

DSA_HEADS = 8
DSA_HEAD_DIM = 64
DSA_WIDTH = DSA_HEADS * DSA_HEAD_DIM
IDX_HEADS = 8
IDX_DIM = 64
TOPK_MAX = 256
DSA_BLOCK = 128

REL_BUCKETS = 32
REL_MAX_DIST = 128
N_BRANCHES = 3

LANES = 128
SUBLANES = 8
VMEM_LIMIT_CAP = 56 * 1024 * 1024
INT_MIN = -(2 ** 31)
NEG_BIG = -1e30

SPLIT_SIZES = (
    MLA_Q_LORA, MLA_KV_LORA, MLA_ROPE, MLA_WIDTH, SSM_INNER, SSM_CONV_DIM, SSM_HEADS, DSA_WIDTH,
    DSA_HEAD_DIM, DSA_HEAD_DIM, IDX_HEADS * IDX_DIM, IDX_DIM, IDX_HEADS, DSA_WIDTH,
    N_BRANCHES * D_MODEL,
)
SPLIT_NAMES = ("c_q", "c_kv", "k_rope", "gate_a", "z", "xbc", "dt", "q_c", "k_c", "v_c", "q_idx",
               "k_idx", "w_idx", "gate_c", "merge")


def _split_bounds():
    out, off = {}, 0
    for name, size in zip(SPLIT_NAMES, SPLIT_SIZES, strict=True):
        out[name] = (off, off + size)
        off += size
    return out


SPLIT = _split_bounds()


def _t5_large_thresholds():
    exact = REL_BUCKETS // 2
    thr = []
    for j in range(1, REL_BUCKETS - exact):
        thr.append(int(math.ceil(exact * (REL_MAX_DIST / exact) ** (j / (REL_BUCKETS - exact)) - 1e-9)))
    return tuple(thr)


T5_EXACT = REL_BUCKETS // 2
T5_LARGE_THR = _t5_large_thresholds()
T5_FAR = T5_LARGE_THR[-1]


def _compiler_params(semantics, vmem_bytes):
    limit = int(min(VMEM_LIMIT_CAP, max(32 * 1024 * 1024, vmem_bytes)))
    return pltpu.CompilerParams(dimension_semantics=semantics, vmem_limit_bytes=limit)


def _nbytes(shape, dtype):
    return math.prod(shape) * jnp.dtype(dtype).itemsize


def _sigmoid(x):
    return 1.0 / (1.0 + jnp.exp(-x))


def _rms(x, g):
    return x * lax.rsqrt(jnp.mean(x * x, axis=-1, keepdims=True) + NORM_EPS) * g


def _norm_kernel(x_ref, g_ref, o_ref):
    o_ref[...] = _rms(x_ref[...], g_ref[...]).astype(o_ref.dtype)


def _norm(x2d, g, out_dtype, tm=512):
    t, d = x2d.shape
    return pl.pallas_call(
        _norm_kernel,
        grid=(t // tm,),
        in_specs=[pl.BlockSpec((tm, d), lambda i: (i, 0)), pl.BlockSpec((1, d), lambda i: (0, 0))],
        out_specs=pl.BlockSpec((tm, d), lambda i: (i, 0)),
        out_shape=jax.ShapeDtypeStruct((t, d), out_dtype),
        compiler_params=_compiler_params(("parallel",), 4 * _nbytes((tm, d), F32)),
        name="rms_norm",
    )(x2d, g.reshape(1, d))


def _proj_kernel(h_ref, w_ref, o_ref, *, act):
    y = jnp.dot(h_ref[...], w_ref[...], preferred_element_type=F32)
    if act == "silu":
        y = y * _sigmoid(y)
    elif act == "sigmoid":
        y = _sigmoid(y)
    o_ref[...] = y.astype(o_ref.dtype)


def _proj(h2d, w, out_dtype, act=None, tm=512, tn=None, name="proj"):
    t, k = h2d.shape
    n = w.shape[1]
    if tn is None:
        tn = n
        for cand in (1024, 768, 512, 896):
            if n > cand and n % cand == 0:
                tn = cand
                break
    vmem = 2 * (_nbytes((tm, k), BF16) + _nbytes((k, tn), BF16) + _nbytes((tm, tn), out_dtype)) + _nbytes((tm, tn), F32) * 2
    return pl.pallas_call(
        functools.partial(_proj_kernel, act=act),
        grid=(n // tn, t // tm),
        in_specs=[pl.BlockSpec((tm, k), lambda j, i: (i, 0)), pl.BlockSpec((k, tn), lambda j, i: (0, j))],
        out_specs=pl.BlockSpec((tm, tn), lambda j, i: (i, j)),
        out_shape=jax.ShapeDtypeStruct((t, n), out_dtype),
        compiler_params=_compiler_params(("parallel", "parallel"), vmem),
        name=name,
    )(h2d, w)


def _proj_t_kernel(h_ref, wt_ref, o_ref):
    y = lax.dot_general(wt_ref[...], h_ref[0], (((1,), (1,)), ((), ())), preferred_element_type=F32)
    o_ref[0] = y.astype(o_ref.dtype)


def _proj_t(h3d, wt, out_dtype, tm=512, name="proj_t"):
    b, s, k = h3d.shape
    n = wt.shape[0]
    vmem = 2 * (_nbytes((tm, k), BF16) + _nbytes((n, k), BF16) + _nbytes((n, tm), out_dtype)) + _nbytes((n, tm), F32) * 2
    return pl.pallas_call(
        _proj_t_kernel,
        grid=(b, s // tm),
        in_specs=[pl.BlockSpec((1, tm, k), lambda bi, i: (bi, i, 0)), pl.BlockSpec((n, k), lambda bi, i: (0, 0))],
        out_specs=pl.BlockSpec((1, n, tm), lambda bi, i: (bi, 0, i)),
        out_shape=jax.ShapeDtypeStruct((b, n, s), out_dtype),
        compiler_params=_compiler_params(("parallel", "parallel"), vmem),
        name=name,
    )(h3d, wt)


def _mla_prep_kernel(a_ref, m1_ref, m2_ref, qn_ref, kvn_ref, wq1_ref, wq2_ref, wk_ref, wv_ref, q_ref, k_ref, v_ref):
    a = a_ref[...]
    c_q = a[:, :MLA_Q_LORA]
    c_kv = a[:, MLA_Q_LORA:MLA_Q_LORA + MLA_KV_LORA]
    kr1 = a[:, MLA_Q_LORA + MLA_KV_LORA:MLA_Q_LORA + MLA_KV_LORA + LANES]
    kr2 = a[:, MLA_Q_LORA + MLA_KV_LORA + LANES:]
    m1 = m1_ref[...]
    m2 = m2_ref[...]
    cqn = _rms(c_q, qn_ref[...]).astype(BF16)
    ckvn = _rms(c_kv, kvn_ref[...]).astype(BF16)
    qa = jnp.dot(cqn, wq1_ref[...], preferred_element_type=F32)
    qb = jnp.dot(cqn, wq2_ref[...], preferred_element_type=F32)
    kn = jnp.dot(ckvn, wk_ref[...], preferred_element_type=F32)
    kr = kr1 * m1 + kr2 * m2
    scale = (MLA_NOPE + MLA_ROPE) ** -0.5
    for h in range(MLA_HEADS):
        sl = slice(h * MLA_QK_PAD, (h + 1) * MLA_QK_PAD)
        q_ref[:, sl] = ((qa[:, sl] * m1 + qb[:, sl] * m2) * scale).astype(q_ref.dtype)
        k_ref[:, sl] = (kn[:, sl] + kr).astype(k_ref.dtype)
    v_ref[...] = jnp.dot(ckvn, wv_ref[...], preferred_element_type=F32).astype(v_ref.dtype)


def _mla_prep(a2d, m1, m2, qn, kvn, wq1, wq2, wk, wv, tm=256):
    t, wa = a2d.shape
    hq = MLA_HEADS * MLA_QK_PAD
    full = lambda shape: pl.BlockSpec(shape, lambda i: (0, 0))
    row = lambda width: pl.BlockSpec((tm, width), lambda i: (i, 0))
    vmem = (2 * (_nbytes((tm, wa), F32) + 2 * _nbytes((tm, LANES), F32) + 2 * _nbytes((MLA_Q_LORA, hq), BF16)
                 + _nbytes((MLA_KV_LORA, hq), BF16) + _nbytes((MLA_KV_LORA, MLA_WIDTH), BF16)
                 + 2 * _nbytes((tm, hq), BF16) + _nbytes((tm, MLA_WIDTH), BF16)) + 4 * _nbytes((tm, hq), F32))
    return pl.pallas_call(
        _mla_prep_kernel,
        grid=(t // tm,),
        in_specs=[row(wa), row(LANES), row(LANES), full((1, MLA_Q_LORA)), full((1, MLA_KV_LORA)),
                  full((MLA_Q_LORA, hq)), full((MLA_Q_LORA, hq)), full((MLA_KV_LORA, hq)), full((MLA_KV_LORA, MLA_WIDTH))],
        out_specs=[row(hq), row(hq), row(MLA_WIDTH)],
        out_shape=[jax.ShapeDtypeStruct((t, hq), BF16), jax.ShapeDtypeStruct((t, hq), BF16),
                   jax.ShapeDtypeStruct((t, MLA_WIDTH), BF16)],
        compiler_params=_compiler_params(("parallel",), vmem),
        name="mla_prep",
    )(a2d, m1, m2, qn.reshape(1, -1), kvn.reshape(1, -1), wq1, wq2, wk, wv)


MLA_HEADS_PER_STEP = 2


def _mla_attn_kernel(q_ref, k_ref, v_ref, g_ref, o_ref, *, tq):
    qi = pl.program_id(2)
    row = lax.broadcasted_iota(I32, (tq, tq), 0)
    col = lax.broadcasted_iota(I32, (tq, tq), 1)
    outs = []
    for hh in range(MLA_HEADS_PER_STEP):
        q = q_ref[0, :, hh * MLA_QK_PAD:(hh + 1) * MLA_QK_PAD]

        def step(j, carry, hh=hh, q=q, diagonal=False):
            m, l, acc = carry
            ks = pl.multiple_of(j * tq, tq)
            k = k_ref[0, pl.ds(ks, tq), hh * MLA_QK_PAD:(hh + 1) * MLA_QK_PAD]
            v = v_ref[0, pl.ds(ks, tq), hh * MLA_V:(hh + 1) * MLA_V]
            s = lax.dot_general(q, k, (((1,), (1,)), ((), ())), preferred_element_type=F32)
            if diagonal:
                s = jnp.where(col <= row, s, -jnp.inf)
            m_new = jnp.maximum(m, jnp.max(s, axis=-1, keepdims=True))
            alpha = jnp.exp(m - m_new)
            p = jnp.exp(s - m_new)
            l_new = alpha * l + jnp.sum(p, axis=-1, keepdims=True)
            acc_new = alpha * acc + jnp.dot(p.astype(BF16), v, preferred_element_type=F32)
            return m_new, l_new, acc_new

        init = (jnp.full((tq, 1), -jnp.inf, F32), jnp.zeros((tq, 1), F32), jnp.zeros((tq, MLA_V), F32))
        carry = lax.fori_loop(0, qi, step, init)
        _, l, acc = step(qi, carry, diagonal=True)
        outs.append(acc / l)
    o = jnp.concatenate(outs, axis=-1) * g_ref[0]
    o_ref[0] = o.astype(o_ref.dtype)


def _mla_attn(q, k, v, gate, tq=256):
    b, s, _ = q.shape
    hp = MLA_HEADS_PER_STEP
    vmem = 2 * (_nbytes((tq, hp * MLA_QK_PAD), BF16) + _nbytes((s, hp * MLA_QK_PAD), BF16) + _nbytes((s, hp * MLA_V), BF16)
                + _nbytes((tq, hp * MLA_V), F32) + _nbytes((tq, hp * MLA_V), BF16)) + 8 * _nbytes((tq, tq), F32)
    return pl.pallas_call(
        functools.partial(_mla_attn_kernel, tq=tq),
        grid=(b, MLA_HEADS // hp, s // tq),
        in_specs=[pl.BlockSpec((1, tq, hp * MLA_QK_PAD), lambda bi, h, i: (bi, i, h)),
                  pl.BlockSpec((1, s, hp * MLA_QK_PAD), lambda bi, h, i: (bi, 0, h)),
                  pl.BlockSpec((1, s, hp * MLA_V), lambda bi, h, i: (bi, 0, h)),
                  pl.BlockSpec((1, tq, hp * MLA_V), lambda bi, h, i: (bi, i, h))],
        out_specs=pl.BlockSpec((1, tq, hp * MLA_V), lambda bi, h, i: (bi, i, h)),
        out_shape=jax.ShapeDtypeStruct((b, s, MLA_WIDTH), BF16),
        compiler_params=_compiler_params(("parallel", "parallel", "arbitrary"), vmem),
        name="mla_attn",
    )(q, k, v, gate)


def _ssd_kernel(xbc_ref, zs_ref, sm_ref, cw_ref, cb_ref, dtb_ref, alog_ref, dsk_ref, nrm_ref, o_ref,
                xwin_ref, state_ref, y_ref):
    c = pl.program_id(1)
    q = SSM_CHUNK
    tail = SUBLANES

    @pl.when(c == 0)
    def _():
        xwin_ref[0:tail, :] = jnp.zeros((tail, SSM_CONV_DIM), F32)
        state_ref[...] = jnp.zeros(state_ref.shape, F32)

    x = xbc_ref[0]
    xwin_ref[tail:tail + q, :] = x
    acc = cb_ref[...] + cw_ref[SSM_CONV - 1:SSM_CONV, :] * x
    for j in range(1, SSM_CONV):
        acc = acc + cw_ref[SSM_CONV - 1 - j:SSM_CONV - j, :] * xwin_ref[tail - j:tail - j + q, :]
    xwin_ref[0:tail, :] = x[q - tail:, :]
    xc = acc * _sigmoid(acc)

    xs = xc[:, :SSM_INNER]
    bm = xc[:, SSM_INNER:SSM_INNER + SSM_GROUPS * SSM_STATE]
    cm = xc[:, SSM_INNER + SSM_GROUPS * SSM_STATE:]

    pre = sm_ref[0] + dtb_ref[...]
    dt = jnp.maximum(pre, 0.0) + jnp.log1p(jnp.exp(-jnp.abs(pre)))
    a = -jnp.exp(alog_ref[...])
    row = lax.broadcasted_iota(I32, (q, q), 0)
    col = lax.broadcasted_iota(I32, (q, q), 1)
    lower = row >= col
    cum = jnp.dot(jnp.where(lower, 1.0, 0.0).astype(F32), dt * a, preferred_element_type=F32,
                  precision=lax.Precision.HIGHEST)
    cum_t = cum.T
    cum_last = cum[q - 1:q, :]
    e_cum = jnp.exp(cum)
    decay_to_end = jnp.exp(cum_last - cum)
    chunk_decay = jnp.exp(cum_last)
    dsk = dsk_ref[...]

    for g in range(SSM_GROUPS):
        bg = bm[:, g * SSM_STATE:(g + 1) * SSM_STATE].astype(BF16)
        cg = cm[:, g * SSM_STATE:(g + 1) * SSM_STATE].astype(BF16)
        cb = lax.dot_general(cg, bg, (((1,), (1,)), ((), ())), preferred_element_type=F32)
        for hh in range(SSM_HEADS_PER_GROUP):
            h = g * SSM_HEADS_PER_GROUP + hh
            xh = xs[:, h * SSM_HEAD_DIM:(h + 1) * SSM_HEAD_DIM]
            xdt = xh * dt[:, h:h + 1]
            diff = cum[:, h:h + 1] - cum_t[h:h + 1, :]
            decay_in = jnp.exp(jnp.where(lower, diff, -jnp.inf))
            y_diag = jnp.dot((cb * decay_in).astype(BF16), xdt.astype(BF16), preferred_element_type=F32)
            st = state_ref[h]
            y_off = lax.dot_general(cg, st.astype(BF16), (((1,), (1,)), ((), ())),
                                    preferred_element_type=F32) * e_cum[:, h:h + 1]
            xw_t = (xdt * decay_to_end[:, h:h + 1]).T.astype(BF16)
            chunk_state = jnp.dot(xw_t, bg, preferred_element_type=F32)
            state_ref[h] = st * chunk_decay[:, h:h + 1] + chunk_state
            y_ref[:, h * SSM_HEAD_DIM:(h + 1) * SSM_HEAD_DIM] = y_diag + y_off + xh * dsk[:, h:h + 1]

    yg = y_ref[...] * zs_ref[0]
    o_ref[0] = _rms(yg, nrm_ref[...]).astype(o_ref.dtype)


def _ssd(xbc, zs, small, conv_w, conv_b, dt_bias, a_log, d_skip, ssm_norm):
    b, s, _ = xbc.shape
    q = SSM_CHUNK
    pad = lambda v: jnp.pad(v.astype(F32), (0, LANES - v.shape[0])).reshape(1, LANES)
    full = lambda shape: pl.BlockSpec(shape, lambda bi, c: (0,) * len(shape))
    blk = lambda width: pl.BlockSpec((1, q, width), lambda bi, c: (bi, c, 0))
    vmem = (2 * (_nbytes((q, SSM_CONV_DIM), F32) + _nbytes((q, SSM_INNER), F32) + _nbytes((q, LANES), F32)
                 + _nbytes((q, SSM_INNER), BF16)) + _nbytes((q + SUBLANES, SSM_CONV_DIM), F32)
            + _nbytes((SSM_HEADS, SSM_HEAD_DIM, SSM_STATE), F32) + 8 * _nbytes((q, SSM_CONV_DIM), F32))
    return pl.pallas_call(
        _ssd_kernel,
        grid=(b, s // q),
        in_specs=[blk(SSM_CONV_DIM), blk(SSM_INNER), blk(LANES), full((SSM_CONV, SSM_CONV_DIM)), full((1, SSM_CONV_DIM)),
                  full((1, LANES)), full((1, LANES)), full((1, LANES)), full((1, SSM_INNER))],
        out_specs=blk(SSM_INNER),
        out_shape=jax.ShapeDtypeStruct((b, s, SSM_INNER), BF16),
        scratch_shapes=[pltpu.VMEM((q + SUBLANES, SSM_CONV_DIM), F32),
                        pltpu.VMEM((SSM_HEADS, SSM_HEAD_DIM, SSM_STATE), F32),
                        pltpu.VMEM((q, SSM_INNER), F32)],
        compiler_params=_compiler_params(("parallel", "arbitrary"), vmem),
        name="ssd",
    )(xbc, zs, small, conv_w.astype(F32), conv_b.reshape(1, -1).astype(F32), pad(dt_bias), pad(a_log), pad(d_skip),
      ssm_norm.reshape(1, -1).astype(F32))


def _t5_bucket_of(n):
    large = jnp.full(n.shape, T5_EXACT, I32)
    for thr in T5_LARGE_THR:
        large = large + jnp.where(n >= thr, 1, 0)
    return jnp.where(n < T5_EXACT, n, large)


def _dsa_kernel(tbl_ref, qct_ref, qit_ref, kc_ref, ki_ref, vw_ref, wq_ref, posk_ref, posq_ref, gate_ref, o_ref,
                key_ref, acc_ref, m_ref, l_ref, tie_ref, *, n_sel):
    blk = DSA_BLOCK
    qi = pl.program_id(1)
    n_tiles = qi + 1
    krow = lax.broadcasted_iota(I32, (blk, blk), 0)
    qcol = lax.broadcasted_iota(I32, (blk, blk), 1)
    w = wq_ref[0, DSA_HEAD_DIM:DSA_HEAD_DIM + IDX_HEADS, :]
    posq = posq_ref[0]

    def tile_start(kt):
        return pl.multiple_of(kt * blk, blk)

    def score_tile(kt, carry):
        ks = tile_start(kt)
        kidx = ki_ref[0, pl.ds(ks, blk), :]
        sc = jnp.zeros((blk, blk), F32)
        for h in range(IDX_HEADS):
            r = jnp.dot(kidx, qit_ref[0, h * IDX_DIM:(h + 1) * IDX_DIM, :], preferred_element_type=F32)
            sc = sc + w[h:h + 1, :] * jnp.maximum(r, 0.0)
        sc = jnp.where(sc == 0.0, 0.0, sc)
        bits = lax.bitcast_convert_type(sc, I32)
        key = bits ^ ((bits >> 31) & 0x7FFFFFFF)
        causal = (ks + krow) <= (qi * blk + qcol)
        key_ref[pl.ds(ks, blk), :] = jnp.where(causal, key, INT_MIN)
        return carry

    lax.fori_loop(0, n_tiles, score_tile, 0)

    def count_where(pred):
        def body(kt, cnt):
            return cnt + jnp.where(pred(key_ref[pl.ds(tile_start(kt), blk), :]), 1, 0)
        cnt = lax.fori_loop(0, n_tiles, body, jnp.zeros((blk, blk), I32))
        return jnp.sum(cnt, axis=0, keepdims=True)

    def bit_step(i, cand):
        trial = cand | lax.shift_left(jnp.int32(1), 31 - i)
        thr = trial ^ INT_MIN
        tot = count_where(lambda k: k >= thr)
        return jnp.where(tot >= n_sel, trial, cand)

    cand = lax.fori_loop(0, 32, bit_step, jnp.zeros((1, blk), I32))
    tau = cand ^ INT_MIN
    need = n_sel - count_where(lambda k: k > tau)

    m_ref[...] = jnp.full(m_ref.shape, NEG_BIG, F32)
    l_ref[...] = jnp.zeros(l_ref.shape, F32)
    acc_ref[...] = jnp.zeros(acc_ref.shape, F32)
    tie_ref[...] = jnp.zeros(tie_ref.shape, I32)
    lower_incl = jnp.where(krow >= qcol, 1.0, 0.0).astype(BF16)

    def attend_tile(kt, general_bias):
        ks = tile_start(kt)
        key = key_ref[pl.ds(ks, blk), :]
        eq = key == tau
        prefix = jnp.dot(lower_incl, jnp.where(eq, 1.0, 0.0).astype(BF16), preferred_element_type=F32)
        tie_before = tie_ref[0:1, :]
        tie_rank = tie_before + prefix.astype(I32)
        causal = (ks + krow) <= (qi * blk + qcol)
        sel = ((key > tau) | (eq & (tie_rank <= need))) & causal
        tie_ref[0:1, :] = tie_before + prefix[blk - 1:blk, :].astype(I32)

        if general_bias:
            posk = posk_ref[0, pl.ds(ks, blk), :]
            bucket = _t5_bucket_of(jnp.maximum(posq - posk, 0))
        kc = kc_ref[0, pl.ds(ks, blk), :]
        vt = vw_ref[0, 0:DSA_HEAD_DIM, pl.ds(ks, blk)].astype(BF16)
        for h in range(DSA_HEADS):
            s = jnp.dot(kc, qct_ref[0, h * DSA_HEAD_DIM:(h + 1) * DSA_HEAD_DIM, :], preferred_element_type=F32)
            if general_bias:
                bias = jnp.full((blk, blk), tbl_ref[0, h], F32)
                for bkt in range(1, REL_BUCKETS):
                    bias = jnp.where(bucket == bkt, tbl_ref[bkt, h], bias)
                s = s + bias
            else:
                s = s + tbl_ref[REL_BUCKETS - 1, h]
            s = jnp.where(sel, s, NEG_BIG)
            m_old = m_ref[h:h + 1, :]
            m_new = jnp.maximum(m_old, jnp.max(s, axis=0, keepdims=True))
            alpha = jnp.exp(m_old - m_new)
            p = jnp.exp(s - m_new)
            l_ref[h:h + 1, :] = alpha * l_ref[h:h + 1, :] + jnp.sum(p, axis=0, keepdims=True)
            hs = slice(h * DSA_HEAD_DIM, (h + 1) * DSA_HEAD_DIM)
            acc_ref[hs, :] = alpha * acc_ref[hs, :] + jnp.dot(vt, p.astype(BF16), preferred_element_type=F32)
            m_ref[h:h + 1, :] = m_new

    def attend(kt, carry):
        posk = posk_ref[0, pl.ds(tile_start(kt), blk), :]
        far = jnp.min((posq - posk).astype(F32)) >= T5_FAR

        @pl.when(far)
        def _():
            attend_tile(kt, general_bias=False)

        @pl.when(jnp.logical_not(far))
        def _():
            attend_tile(kt, general_bias=True)

        return carry

    lax.fori_loop(0, n_tiles, attend, 0)

    for h in range(DSA_HEADS):
        hs = slice(h * DSA_HEAD_DIM, (h + 1) * DSA_HEAD_DIM)
        acc_ref[hs, :] = acc_ref[hs, :] / l_ref[h:h + 1, :]
    o_ref[0] = (acc_ref[...].T * gate_ref[0]).astype(o_ref.dtype)


def _dsa(rel_bias, qct, qit, kc, ki, vw, pos_col, pos_row, gate):
    b, s, _ = kc.shape
    blk = DSA_BLOCK
    n_sel = min(TOPK_MAX, s // 4)
    vw_rows = vw.shape[1]
    vmem = (2 * (2 * _nbytes((DSA_WIDTH, blk), BF16) + 2 * _nbytes((s, LANES), BF16) + _nbytes((vw_rows, s), F32)
                 + _nbytes((vw_rows, blk), F32) + _nbytes((s, LANES), I32) + _nbytes((blk, DSA_WIDTH), F32)
                 + _nbytes((blk, DSA_WIDTH), BF16)) + _nbytes((s, blk), I32) + _nbytes((DSA_WIDTH, blk), F32)
            + 64 * _nbytes((blk, blk), F32))
    return pl.pallas_call(
        functools.partial(_dsa_kernel, n_sel=n_sel),
        grid=(b, s // blk),
        in_specs=[pl.BlockSpec(memory_space=pltpu.SMEM),
                  pl.BlockSpec((1, DSA_WIDTH, blk), lambda bi, i: (bi, 0, i)),
                  pl.BlockSpec((1, DSA_WIDTH, blk), lambda bi, i: (bi, 0, i)),
                  pl.BlockSpec((1, s, DSA_HEAD_DIM), lambda bi, i: (bi, 0, 0)),
                  pl.BlockSpec((1, s, IDX_DIM), lambda bi, i: (bi, 0, 0)),
                  pl.BlockSpec((1, vw_rows, s), lambda bi, i: (bi, 0, 0)),
                  pl.BlockSpec((1, vw_rows, blk), lambda bi, i: (bi, 0, i)),
                  pl.BlockSpec((1, s, 1), lambda bi, i: (bi, 0, 0)),
                  pl.BlockSpec((1, 1, blk), lambda bi, i: (bi, 0, i)),
                  pl.BlockSpec((1, blk, DSA_WIDTH), lambda bi, i: (bi, i, 0))],
        out_specs=pl.BlockSpec((1, blk, DSA_WIDTH), lambda bi, i: (bi, i, 0)),
        out_shape=jax.ShapeDtypeStruct((b, s, DSA_WIDTH), BF16),
        scratch_shapes=[pltpu.VMEM((s, blk), I32), pltpu.VMEM((DSA_WIDTH, blk), F32),
                        pltpu.VMEM((DSA_HEADS, blk), F32), pltpu.VMEM((DSA_HEADS, blk), F32),
                        pltpu.VMEM((SUBLANES, blk), I32)],
        compiler_params=_compiler_params(("parallel", "arbitrary"), vmem),
        name="dsa",
    )(rel_bias, qct, qit, kc, ki, vw, vw, pos_col, pos_row, gate)


def _merge_kernel(oa_ref, ob_ref, oc_ref, g_ref, x_ref, p_ref, wa_ref, wb_ref, wc_ref, wo_ref, wpg_ref, wple_ref,
                  gn_ref, xo_ref, ho_ref):
    d = D_MODEL
    ya = jnp.dot(oa_ref[...], wa_ref[...], preferred_element_type=F32)
    yb = jnp.dot(ob_ref[...], wb_ref[...], preferred_element_type=F32)
    yc = jnp.dot(oc_ref[...], wc_ref[...], preferred_element_type=F32)
    merged = g_ref[:, 0:d] * ya + g_ref[:, d:2 * d] * yb + g_ref[:, 2 * d:3 * d] * yc
    x1 = x_ref[...] + jnp.dot(merged.astype(BF16), wo_ref[...], preferred_element_type=F32)
    ple_gate = _sigmoid(jnp.dot(x1.astype(BF16), wpg_ref[...], preferred_element_type=F32))
    x2 = x1 + ple_gate * jnp.dot(p_ref[...].astype(BF16), wple_ref[...], preferred_element_type=F32)
    xo_ref[...] = x2
    ho_ref[...] = _rms(x2, gn_ref[...]).astype(ho_ref.dtype)


def _merge(oa, ob, oc, gates, x, p, wa, wb, wc, wo, wpg, wple, g_next, h_dtype, tm=256):
    t, d = x.shape
    full = lambda a: pl.BlockSpec(a.shape, lambda i: (0, 0))
    row = lambda a: pl.BlockSpec((tm, a.shape[1]), lambda i: (i, 0))
    weights = (wa, wb, wc, wo, wpg, wple)
    acts = (oa, ob, oc, gates, x, p)
    vmem = (2 * (sum(_nbytes(a.shape, a.dtype) for a in weights) + sum(_nbytes((tm, a.shape[1]), a.dtype) for a in acts)
                 + 2 * _nbytes((tm, d), F32)) + 8 * _nbytes((tm, d), F32))
    return pl.pallas_call(
        _merge_kernel,
        grid=(t // tm,),
        in_specs=[row(a) for a in acts] + [full(a) for a in weights] + [pl.BlockSpec((1, d), lambda i: (0, 0))],
        out_specs=[pl.BlockSpec((tm, d), lambda i: (i, 0)), pl.BlockSpec((tm, d), lambda i: (i, 0))],
        out_shape=[jax.ShapeDtypeStruct((t, d), F32), jax.ShapeDtypeStruct((t, d), h_dtype)],
        compiler_params=_compiler_params(("parallel",), vmem),
        name="merge",
    )(*acts, *weights, g_next.reshape(1, d))


def _rotate_half_cols(w):
    half = w.shape[-1] // 2
    return jnp.concatenate([-w[..., half:], w[..., :half]], axis=-1)


def _prep_weights(w_in, w_uq, w_ukv):
    depth = w_in.shape[0]
    seg = lambda name: w_in[:, :, SPLIT[name][0]:SPLIT[name][1]]
    z = lambda *shape: jnp.zeros((depth,) + shape, w_in.dtype)
    d = D_MODEL
    kr = seg("k_rope")
    rope_lo = MLA_NOPE
    rope_pad = MLA_QK_PAD - MLA_NOPE - MLA_ROPE
    w = {}
    w["mla_in"] = jnp.concatenate(
        [seg("c_q"), seg("c_kv"), z(d, rope_lo), kr, z(d, rope_pad), z(d, rope_lo), _rotate_half_cols(kr), z(d, rope_pad)], axis=-1)
    w["gate_a"] = seg("gate_a")
    w["z"] = seg("z")
    w["xbc"] = seg("xbc")
    w["small"] = jnp.concatenate([seg("dt"), z(d, LANES - SSM_HEADS)], axis=-1)
    w["k_c"] = seg("k_c")
    w["k_idx"] = seg("k_idx")
    w["gate_c"] = seg("gate_c")
    w["merge"] = seg("merge")
    w["q_c_t"] = jnp.swapaxes(seg("q_c") * (DSA_HEAD_DIM ** -0.5), 1, 2)
    w["q_idx_t"] = jnp.swapaxes(seg("q_idx"), 1, 2)
    w["vw_t"] = jnp.swapaxes(jnp.concatenate([seg("v_c"), seg("w_idx")], axis=-1), 1, 2)
    w = {k: v.astype(BF16) for k, v in w.items()}

    uq = w_uq.reshape(depth, MLA_Q_LORA, MLA_HEADS, MLA_NOPE + MLA_ROPE)
    nope, rope = uq[..., :MLA_NOPE], uq[..., MLA_NOPE:]
    zq = lambda width: jnp.zeros((depth, MLA_Q_LORA, MLA_HEADS, width), w_uq.dtype)
    hq = MLA_HEADS * MLA_QK_PAD
    w["uq1"] = jnp.concatenate([nope, rope, zq(rope_pad)], axis=-1).reshape(depth, MLA_Q_LORA, hq).astype(BF16)
    w["uq2"] = jnp.concatenate([zq(MLA_NOPE), _rotate_half_cols(rope), zq(rope_pad)], axis=-1).reshape(depth, MLA_Q_LORA, hq).astype(BF16)
    ukv = w_ukv.reshape(depth, MLA_KV_LORA, MLA_HEADS, MLA_NOPE + MLA_V)
    zk = jnp.zeros((depth, MLA_KV_LORA, MLA_HEADS, MLA_QK_PAD - MLA_NOPE), w_ukv.dtype)
    w["uk"] = jnp.concatenate([ukv[..., :MLA_NOPE], zk], axis=-1).reshape(depth, MLA_KV_LORA, hq).astype(BF16)
    w["uv"] = ukv[..., MLA_NOPE:].reshape(depth, MLA_KV_LORA, MLA_WIDTH).astype(BF16)
    return w


def _rope_multipliers(positions):
    b, s = positions.shape
    inv_freq = 1.0 / (ROPE_THETA ** (jnp.arange(0, MLA_ROPE, 2, dtype=F32) / MLA_ROPE))
    ang = positions.astype(F32)[..., None] * inv_freq
    cos, sin = jnp.cos(ang), jnp.sin(ang)
    pad = MLA_QK_PAD - MLA_NOPE - MLA_ROPE
    m1 = jnp.concatenate([jnp.ones((b, s, MLA_NOPE), F32), cos, cos, jnp.zeros((b, s, pad), F32)], axis=-1)
    m2 = jnp.concatenate([jnp.zeros((b, s, MLA_NOPE), F32), sin, sin, jnp.zeros((b, s, pad), F32)], axis=-1)
    return m1.reshape(b * s, MLA_QK_PAD), m2.reshape(b * s, MLA_QK_PAD)


def kernel(x, p, positions, norm_g, w_in, mla_q_norm, w_uq, mla_kv_norm, w_ukv, conv_w, conv_b, dt_bias, a_log, d_skip, ssm_norm, w_br_a, w_br_b, w_br_c, w_out, rel_bias, w_ple, w_ple_gate, final_norm):
    b, s, d = x.shape
    depth = w_in.shape[0]
    t = b * s
    w = _prep_weights(w_in, w_uq, w_ukv)
    m1, m2 = _rope_multipliers(positions)
    pos_col = positions.astype(I32).reshape(b, s, 1)
    pos_row = positions.astype(I32).reshape(b, 1, s)
    wbr_a, wbr_b, wbr_c = w_br_a.astype(BF16), w_br_b.astype(BF16), w_br_c.astype(BF16)
    wo, wpg, wple = w_out.astype(BF16), w_ple_gate.astype(BF16), w_ple.astype(BF16)

    x2d = x.reshape(t, d)
    h = _norm(x2d, norm_g[0], BF16)
    for i in range(depth):
        h3d = h.reshape(b, s, d)
        a_in = _proj(h, w["mla_in"][i], F32, name="proj_mla_in")
        gate_a = _proj(h, w["gate_a"][i], F32, act="silu", name="proj_gate_a")
        zs = _proj(h, w["z"][i], F32, act="silu", name="proj_z")
        xbc = _proj(h, w["xbc"][i], F32, name="proj_xbc")
        small = _proj(h, w["small"][i], F32, name="proj_small")
        k_c = _proj(h, w["k_c"][i], BF16, name="proj_k_c")
        k_idx = _proj(h, w["k_idx"][i], BF16, name="proj_k_idx")
        gate_c = _proj(h, w["gate_c"][i], F32, act="silu", name="proj_gate_c")
        gates = _proj(h, w["merge"][i], F32, act="sigmoid", name="proj_merge")
        qct = _proj_t(h3d, w["q_c_t"][i], BF16, name="proj_q_c_t")
        qit = _proj_t(h3d, w["q_idx_t"][i], BF16, name="proj_q_idx_t")
        vw = _proj_t(h3d, w["vw_t"][i], F32, name="proj_vw_t")

        q, k, v = _mla_prep(a_in, m1, m2, mla_q_norm[i], mla_kv_norm[i], w["uq1"][i], w["uq2"][i], w["uk"][i], w["uv"][i])
        o_a = _mla_attn(q.reshape(b, s, -1), k.reshape(b, s, -1), v.reshape(b, s, -1), gate_a.reshape(b, s, -1))
        o_b = _ssd(xbc.reshape(b, s, -1), zs.reshape(b, s, -1), small.reshape(b, s, -1), conv_w[i], conv_b[i],
                   dt_bias[i], a_log[i], d_skip[i], ssm_norm[i])
        o_c = _dsa(rel_bias.astype(F32), qct, qit, k_c.reshape(b, s, -1), k_idx.reshape(b, s, -1), vw, pos_col, pos_row,
                   gate_c.reshape(b, s, -1))

        last = i == depth - 1
        g_next = final_norm if last else norm_g[i + 1]
        x2d, h = _merge(o_a.reshape(t, -1), o_b.reshape(t, -1), o_c.reshape(t, -1), gates, x2d, p[i].reshape(t, -1),
                        wbr_a[i], wbr_b[i], wbr_c[i], wo[i], wpg[i], wple[i], g_next, F32 if last else BF16)
    return h.reshape(b, s, d)
```

```python
import functools
import math

import jax
import jax.numpy as jnp
from jax import lax
from jax.experimental import pallas as pl
from jax.experimental.pallas import tpu as pltpu

F32 = jnp.float32
BF16 = jnp.bfloat16
I32 = jnp.int32

D_MODEL = 1024
PLE_DIM = 256
NORM_EPS = 1e-6

MLA_HEADS = 8
MLA_NOPE = 64
MLA_ROPE = 32
MLA_V = 64
MLA_Q_LORA = 384
MLA_KV_LORA = 256
MLA_WIDTH = MLA_HEADS * MLA_V
ROPE_THETA = 10000.0
MLA_QK_PAD = 128
MLA_LOGITS_AHEAD = 8
MLA_Q_SCALE =(MLA_NOPE + MLA_ROPE) ** -0.5 * math.log2(math.e)

SSM_HEADS = 16
SSM_HEAD_DIM = 64
SSM_INNER = SSM_HEADS * SSM_HEAD_DIM
SSM_GROUPS = 2
SSM_STATE = 128
SSM_CONV = 4
SSM_CHUNK = 128
SSM_CONV_DIM = SSM_INNER + 2 * SSM_GROUPS * SSM_STATE
SSM_HEADS_PER_GROUP = SSM_HEADS // SSM_GROUPS

DSA_HEADS = 8
DSA_HEAD_DIM = 64
DSA_WIDTH = DSA_HEADS * DSA_HEAD_DIM
IDX_HEADS = 8
IDX_DIM = 64
TOPK_MAX = 256
DSA_KEY_TILE = 128

REL_BUCKETS = 32
REL_MAX_DIST = 128
N_BRANCHES = 3

LANES = 128
SUBLANES = 8
VMEM_LIMIT_CAP = 56 * 1024 * 1024
INT_MIN = -(2 ** 31)
NEG_BIG = -1e30

SPLIT_SIZES = (
    MLA_Q_LORA, MLA_KV_LORA, MLA_ROPE, MLA_WIDTH, SSM_INNER, SSM_CONV_DIM, SSM_HEADS, DSA_WIDTH,
    DSA_HEAD_DIM, DSA_HEAD_DIM, IDX_HEADS * IDX_DIM, IDX_DIM, IDX_HEADS, DSA_WIDTH,
    N_BRANCHES * D_MODEL,
)
SPLIT_NAMES = ("c_q", "c_kv", "k_rope", "gate_a", "z", "xbc", "dt", "q_c", "k_c", "v_c", "q_idx",
               "k_idx", "w_idx", "gate_c", "merge")


def _split_bounds():
    out, off = {}, 0
    for name, size in zip(SPLIT_NAMES, SPLIT_SIZES, strict=True):
        out[name] = (off, off + size)
        off += size
    return out


SPLIT = _split_bounds()


def _t5_large_thresholds():
    exact = REL_BUCKETS // 2
    thr = []
    for j in range(1, REL_BUCKETS - exact):
        thr.append(int(math.ceil(exact * (REL_MAX_DIST / exact) ** (j / (REL_BUCKETS - exact)) - 1e-9)))
    return tuple(thr)


T5_EXACT = REL_BUCKETS // 2
T5_LARGE_THR = _t5_large_thresholds()
T5_FAR = T5_LARGE_THR[-1]


def _compiler_params(semantics, vmem_bytes):
    limit = int(min(VMEM_LIMIT_CAP, max(32 * 1024 * 1024, vmem_bytes)))
    return pltpu.CompilerParams(dimension_semantics=semantics, vmem_limit_bytes=limit)


def _nbytes(shape, dtype):
    return math.prod(shape) * jnp.dtype(dtype).itemsize


def _sigmoid(x):
    return 1.0 / (1.0 + jnp.exp(-x))


def _rms(x, g):
    return x * lax.rsqrt(jnp.mean(x * x, axis=-1, keepdims=True) + NORM_EPS) * g


def _norm_kernel(x_ref, g_ref, o_ref):
    o_ref[...] = _rms(x_ref[...], g_ref[...]).astype(o_ref.dtype)


def _norm(x2d, g, out_dtype, tm=512):
    t, d = x2d.shape
    return pl.pallas_call(
        _norm_kernel,
        grid=(t // tm,),
        in_specs=[pl.BlockSpec((tm, d), lambda i: (i, 0)), pl.BlockSpec((1, d), lambda i: (0, 0))],
        out_specs=pl.BlockSpec((tm, d), lambda i: (i, 0)),
        out_shape=jax.ShapeDtypeStruct((t, d), out_dtype),
        compiler_params=_compiler_params(("parallel",), 4 * _nbytes((tm, d), F32)),
        name="rms_norm",
    )(x2d, g.reshape(1, d))


def _proj_kernel(h_ref, w_ref, o_ref, *, act):
    y = jnp.dot(h_ref[...], w_ref[...], preferred_element_type=F32)
    if act == "silu":
        y = y * _sigmoid(y)
    elif act == "sigmoid":
        y = _sigmoid(y)
    o_ref[...] = y.astype(o_ref.dtype)


def _proj(h2d, w, out_dtype, act=None, tm=512, tn=None, name="proj"):
    t, k = h2d.shape
    n = w.shape[1]
    if tn is None:
        tn = n
        for cand in (1024, 768, 512, 896):
            if n > cand and n % cand == 0:
                tn = cand
                break
    vmem = 2 * (_nbytes((tm, k), BF16) + _nbytes((k, tn), BF16) + _nbytes((tm, tn), out_dtype)) + _nbytes((tm, tn), F32) * 2
    return pl.pallas_call(
        functools.partial(_proj_kernel, act=act),
        grid=(n // tn, t // tm),
        in_specs=[pl.BlockSpec((tm, k), lambda j, i: (i, 0)), pl.BlockSpec((k, tn), lambda j, i: (0, j))],
        out_specs=pl.BlockSpec((tm, tn), lambda j, i: (i, j)),
        out_shape=jax.ShapeDtypeStruct((t, n), out_dtype),
        compiler_params=_compiler_params(("parallel", "parallel"), vmem),
        name=name,
    )(h2d, w)


def _proj_t_kernel(h_ref, wt_ref, o_ref):
    y = lax.dot_general(wt_ref[...], h_ref[0], (((1,), (1,)), ((), ())), preferred_element_type=F32)
    o_ref[0] = y.astype(o_ref.dtype)


def _proj_t(h3d, wt, out_dtype, tm=512, name="proj_t"):
    b, s, k = h3d.shape
    n = wt.shape[0]
    vmem = 2 * (_nbytes((tm, k), BF16) + _nbytes((n, k), BF16) + _nbytes((n, tm), out_dtype)) + _nbytes((n, tm), F32) * 2
    return pl.pallas_call(
        _proj_t_kernel,
        grid=(b, s // tm),
        in_specs=[pl.BlockSpec((1, tm, k), lambda bi, i: (bi, i, 0)), pl.BlockSpec((n, k), lambda bi, i: (0, 0))],
        out_specs=pl.BlockSpec((1, n, tm), lambda bi, i: (bi, 0, i)),
        out_shape=jax.ShapeDtypeStruct((b, n, s), out_dtype),
        compiler_params=_compiler_params(("parallel", "parallel"), vmem),
        name=name,
    )(h3d, wt)


def _mla_prep_kernel(a_ref, m1_ref, m2_ref, m1t_ref, m2t_ref, qn_ref, kvn_ref, wq1t_ref, wq2t_ref, wk_ref, wvt_ref,
                     qt_ref, k_ref, vt_ref):
    a = a_ref[0]
    c_q = a[:, :MLA_Q_LORA]
    c_kv = a[:, MLA_Q_LORA:MLA_Q_LORA + MLA_KV_LORA]
    kr1 = a[:, MLA_Q_LORA + MLA_KV_LORA:MLA_Q_LORA + MLA_KV_LORA + LANES]
    kr2 = a[:, MLA_Q_LORA + MLA_KV_LORA + LANES:]
    cqn = _rms(c_q, qn_ref[...]).astype(BF16)
    ckvn = _rms(c_kv, kvn_ref[...]).astype(BF16)
    nt = (((1,), (1,)), ((), ()))
    qa_t = lax.dot_general(wq1t_ref[...], cqn, nt, preferred_element_type=F32)
    qb_t = lax.dot_general(wq2t_ref[...], cqn, nt, preferred_element_type=F32)
    kn = jnp.dot(ckvn, wk_ref[...], preferred_element_type=F32)
    kr = kr1 * m1_ref[0] + kr2 * m2_ref[0]
    m1t = m1t_ref[0] * MLA_Q_SCALE
    m2t = m2t_ref[0] * MLA_Q_SCALE
    for h in range(MLA_HEADS):
        sl = slice(h * MLA_QK_PAD, (h + 1) * MLA_QK_PAD)
        qt_ref[0, sl, :] = (qa_t[sl, :] * m1t + qb_t[sl, :] * m2t).astype(qt_ref.dtype)
        k_ref[0, :, sl] = (kn[:, sl] + kr).astype(k_ref.dtype)
    vt_ref[0] = lax.dot_general(wvt_ref[...], ckvn, nt, preferred_element_type=F32).astype(vt_ref.dtype)


def _mla_prep(a, m1, m2, m1t, m2t, qn, kvn, wq1t, wq2t, wk, wvt, tm=256):
    b, s, wa = a.shape
    hq = MLA_HEADS * MLA_QK_PAD
    full = lambda shape: pl.BlockSpec(shape, lambda bi, i: (0, 0))
    row = lambda width: pl.BlockSpec((1, tm, width), lambda bi, i: (bi, i, 0))
    col = lambda height: pl.BlockSpec((1, height, tm), lambda bi, i: (bi, 0, i))
    vmem = (2 * (_nbytes((tm, wa), F32) + 4 * _nbytes((tm, LANES), F32) + 2 * _nbytes((MLA_Q_LORA, hq), BF16)
                 + _nbytes((MLA_KV_LORA, hq), BF16) + _nbytes((MLA_KV_LORA, MLA_WIDTH), BF16)
                 + 2 * _nbytes((tm, hq), BF16) + _nbytes((tm, MLA_WIDTH), BF16)) + 4 * _nbytes((tm, hq), F32))
    return pl.pallas_call(
        _mla_prep_kernel,
        grid=(b, s // tm),
        in_specs=[row(wa), row(LANES), row(LANES), col(LANES), col(LANES), full((1, MLA_Q_LORA)), full((1, MLA_KV_LORA)),
                  full((hq, MLA_Q_LORA)), full((hq, MLA_Q_LORA)), full((MLA_KV_LORA, hq)), full((MLA_WIDTH, MLA_KV_LORA))],
        out_specs=[col(hq), row(hq), col(MLA_WIDTH)],
        out_shape=[jax.ShapeDtypeStruct((b, hq, s), BF16), jax.ShapeDtypeStruct((b, s, hq), BF16),
                   jax.ShapeDtypeStruct((b, MLA_WIDTH, s), BF16)],
        compiler_params=_compiler_params(("parallel", "parallel"), vmem),
        name="mla_prep",
    )(a, m1, m2, m1t, m2t, qn.reshape(1, -1), kvn.reshape(1, -1), wq1t, wq2t, wk, wvt)


def _mla_attn_kernel(qt_ref, k_ref, vt_ref, g_ref, o_ref, m_ref, l_ref, acc_ref, *, tq, tk):
    qi = pl.program_id(1)
    krow = lax.broadcasted_iota(I32, (tk, tq), 0)
    qcol = lax.broadcasted_iota(I32, (tk, tq), 1)
    tiles_per_q = tq // tk
    m_ref[...] = jnp.full(m_ref.shape, -jnp.inf, F32)
    l_ref[...] = jnp.zeros(l_ref.shape, F32)
    acc_ref[...] = jnp.zeros(acc_ref.shape, F32)

    def tile(j, diagonal):
        ks = pl.multiple_of(j * tk, tk)
        m_all = m_ref[...]
        l_all = l_ref[...]
        if diagonal:
            causal = ks + krow <= qi * tq + qcol

        def logits(h):
            k = k_ref[0, pl.ds(ks, tk), h * MLA_QK_PAD:(h + 1) * MLA_QK_PAD]
            return jnp.dot(k, qt_ref[0, h * MLA_QK_PAD:(h + 1) * MLA_QK_PAD, :], preferred_element_type=F32)

        ahead = MLA_LOGITS_AHEAD
        pending = [logits(h) for h in range(ahead)]
        for h in range(MLA_HEADS):
            s = pending.pop(0)
            if h + ahead < MLA_HEADS:
                pending.append(logits(h + ahead))
            if diagonal:
                s = jnp.where(causal, s, -jnp.inf)
            m_old = m_all[h:h + 1, :]
            m_new = jnp.maximum(m_old, jnp.max(s, axis=0, keepdims=True))
            alpha = jnp.exp2(m_old - m_new)
            p = jnp.exp2(s - m_new)
            l_ref[h:h + 1, :] = alpha * l_all[h:h + 1, :] + jnp.sum(p, axis=0, keepdims=True)
            m_ref[h:h + 1, :] = m_new
            hs = slice(h * MLA_V, (h + 1) * MLA_V)
            pv = jnp.dot(vt_ref[0, hs, pl.ds(ks, tk)], p.astype(BF16), preferred_element_type=F32)
            acc_ref[hs, :] = alpha * acc_ref[hs, :] + pv

    def full_tile(j, carry):
        tile(j, diagonal=False)
        return carry

    lax.fori_loop(0, qi * tiles_per_q, full_tile, 0)
    for d in range(tiles_per_q):
        tile(qi * tiles_per_q + d, diagonal=True)
    for h in range(MLA_HEADS):
        hs = slice(h * MLA_V, (h + 1) * MLA_V)
        acc_ref[hs, :] = acc_ref[hs, :] / l_ref[h:h + 1, :]
    o_ref[0] = (acc_ref[...].T * g_ref[0]).astype(o_ref.dtype)


def _mla_attn(qt, k, vt, gate, tq=256, tk=128):
    b, s, hq = k.shape
    vmem = 2 * (_nbytes((hq, tq), BF16) + _nbytes((s, hq), BF16) + _nbytes((MLA_WIDTH, s), BF16)
                + _nbytes((tq, MLA_WIDTH), F32) + _nbytes((tq, MLA_WIDTH), BF16)) + _nbytes((MLA_WIDTH, tq), F32) + 16 * _nbytes((tk, tq), F32)
    return pl.pallas_call(
        functools.partial(_mla_attn_kernel, tq=tq, tk=tk),
        grid=(b, s // tq),
        in_specs=[pl.BlockSpec((1, hq, tq), lambda bi, i: (bi, 0, i)),
                  pl.BlockSpec((1, s, hq), lambda bi, i: (bi, 0, 0)),
                  pl.BlockSpec((1, MLA_WIDTH, s), lambda bi, i: (bi, 0, 0)),
                  pl.BlockSpec((1, tq, MLA_WIDTH), lambda bi, i: (bi, i, 0))],
        out_specs=pl.BlockSpec((1, tq, MLA_WIDTH), lambda bi, i: (bi, i, 0)),
        out_shape=jax.ShapeDtypeStruct((b, s, MLA_WIDTH), BF16),
        scratch_shapes=[pltpu.VMEM((MLA_HEADS, tq), F32), pltpu.VMEM((MLA_HEADS, tq), F32),
                        pltpu.VMEM((MLA_WIDTH, tq), F32)],
        compiler_params=_compiler_params(("parallel", "arbitrary"), vmem),
        name="mla_attn",
    )(qt, k, vt, gate)


def _ssd_kernel(xbc_ref, zs_ref, sm_ref, cw_ref, cb_ref, dtb_ref, alog_ref, dsk_ref, nrm_ref, o_ref,
                xwin_ref, state_ref, y_ref):
    c = pl.program_id(1)
    q = SSM_CHUNK
    tail = SUBLANES

    @pl.when(c == 0)
    def _():
        xwin_ref[0:tail, :] = jnp.zeros((tail, SSM_CONV_DIM), F32)
        state_ref[...] = jnp.zeros(state_ref.shape, F32)

    x = xbc_ref[0]
    xwin_ref[tail:tail + q, :] = x
    acc = cb_ref[...] + cw_ref[SSM_CONV - 1:SSM_CONV, :] * x
    for j in range(1, SSM_CONV):
        acc = acc + cw_ref[SSM_CONV - 1 - j:SSM_CONV - j, :] * xwin_ref[tail - j:tail - j + q, :]
    xwin_ref[0:tail, :] = x[q - tail:, :]
    xc = acc * _sigmoid(acc)

    xs = xc[:, :SSM_INNER]
    bm = xc[:, SSM_INNER:SSM_INNER + SSM_GROUPS * SSM_STATE]
    cm = xc[:, SSM_INNER + SSM_GROUPS * SSM_STATE:]

    pre = sm_ref[0] + dtb_ref[...]
    dt = jnp.maximum(pre, 0.0) + jnp.log1p(jnp.exp(-jnp.abs(pre)))
    a = -jnp.exp(alog_ref[...])
    row = lax.broadcasted_iota(I32, (q, q), 0)
    col = lax.broadcasted_iota(I32, (q, q), 1)
    lower = row >= col
    cum = jnp.dot(jnp.where(lower, 1.0, 0.0).astype(F32), dt * a, preferred_element_type=F32,
                  precision=lax.Precision.HIGHEST)
    cum_t = cum.T
    cum_last = cum[q - 1:q, :]
    e_cum = jnp.exp(cum)
    decay_to_end = jnp.exp(cum_last - cum)
    chunk_decay = jnp.exp(cum_last)
    dsk = dsk_ref[...]

    for g in range(SSM_GROUPS):
        bg = bm[:, g * SSM_STATE:(g + 1) * SSM_STATE].astype(BF16)
        cg = cm[:, g * SSM_STATE:(g + 1) * SSM_STATE].astype(BF16)
        cb = lax.dot_general(cg, bg, (((1,), (1,)), ((), ())), preferred_element_type=F32)
        for hh in range(SSM_HEADS_PER_GROUP):
            h = g * SSM_HEADS_PER_GROUP + hh
            xh = xs[:, h * SSM_HEAD_DIM:(h + 1) * SSM_HEAD_DIM]
            xdt = xh * dt[:, h:h + 1]
            diff = cum[:, h:h + 1] - cum_t[h:h + 1, :]
            decay_in = jnp.exp(jnp.where(lower, diff, -jnp.inf))
            y_diag = jnp.dot((cb * decay_in).astype(BF16), xdt.astype(BF16), preferred_element_type=F32)
            st = state_ref[h]
            y_off = lax.dot_general(cg, st.astype(BF16), (((1,), (1,)), ((), ())),
                                    preferred_element_type=F32) * e_cum[:, h:h + 1]
            xw_t = (xdt * decay_to_end[:, h:h + 1]).T.astype(BF16)
            chunk_state = jnp.dot(xw_t, bg, preferred_element_type=F32)
            state_ref[h] = st * chunk_decay[:, h:h + 1] + chunk_state
            y_ref[:, h * SSM_HEAD_DIM:(h + 1) * SSM_HEAD_DIM] = y_diag + y_off + xh * dsk[:, h:h + 1]

    yg = y_ref[...] * zs_ref[0]
    o_ref[0] = _rms(yg, nrm_ref[...]).astype(o_ref.dtype)


def _ssd(xbc, zs, small, conv_w, conv_b, dt_bias, a_log, d_skip, ssm_norm):
    b, s, _ = xbc.shape
    q = SSM_CHUNK
    pad = lambda v: jnp.pad(v.astype(F32), (0, LANES - v.shape[0])).reshape(1, LANES)
    full = lambda shape: pl.BlockSpec(shape, lambda bi, c: (0,) * len(shape))
    blk = lambda width: pl.BlockSpec((1, q, width), lambda bi, c: (bi, c, 0))
    vmem = (2 * (_nbytes((q, SSM_CONV_DIM), F32) + _nbytes((q, SSM_INNER), F32) + _nbytes((q, LANES), F32)
                 + _nbytes((q, SSM_INNER), BF16)) + _nbytes((q + SUBLANES, SSM_CONV_DIM), F32)
            + _nbytes((SSM_HEADS, SSM_HEAD_DIM, SSM_STATE), F32) + 8 * _nbytes((q, SSM_CONV_DIM), F32))
    return pl.pallas_call(
        _ssd_kernel,
        grid=(b, s // q),
        in_specs=[blk(SSM_CONV_DIM), blk(SSM_INNER), blk(LANES), full((SSM_CONV, SSM_CONV_DIM)), full((1, SSM_CONV_DIM)),
                  full((1, LANES)), full((1, LANES)), full((1, LANES)), full((1, SSM_INNER))],
        out_specs=blk(SSM_INNER),
        out_shape=jax.ShapeDtypeStruct((b, s, SSM_INNER), BF16),
        scratch_shapes=[pltpu.VMEM((q + SUBLANES, SSM_CONV_DIM), F32),
                        pltpu.VMEM((SSM_HEADS, SSM_HEAD_DIM, SSM_STATE), F32),
                        pltpu.VMEM((q, SSM_INNER), F32)],
        compiler_params=_compiler_params(("parallel", "arbitrary"), vmem),
        name="ssd",
    )(xbc, zs, small, conv_w.astype(F32), conv_b.reshape(1, -1).astype(F32), pad(dt_bias), pad(a_log), pad(d_skip),
      ssm_norm.reshape(1, -1).astype(F32))


def _t5_bucket_of(n):
    large = jnp.full(n.shape, T5_EXACT, I32)
    for thr in T5_LARGE_THR:
        large = large + jnp.where(n >= thr, 1, 0)
    return jnp.where(n < T5_EXACT, n, large)


def _dsa_kernel(tbl_ref, qmin_ref, kmax_ref, tblt_ref, qct_ref, qit_ref, kc_ref, ki_ref, vw_ref, wq_ref, posk_ref, posq_ref, gate_ref, o_ref,
                key_ref, acc_ref, m_ref, l_ref, tie_ref, *, n_sel, tq):
    tk = DSA_KEY_TILE
    qi = pl.program_id(1)
    n_tiles = (qi + 1) * (tq // tk)
    krow = lax.broadcasted_iota(I32, (tk, tq), 0)
    qcol = lax.broadcasted_iota(I32, (tk, tq), 1)
    w = wq_ref[0, DSA_HEAD_DIM:DSA_HEAD_DIM + IDX_HEADS, :]
    posq = posq_ref[0]

    def tile_start(kt):
        return pl.multiple_of(kt * tk, tk)

    def causal_of(ks):
        return (ks + krow) <= (qi * tq + qcol)

    def score_tile(kt, carry):
        ks = tile_start(kt)
        kidx = ki_ref[0, pl.ds(ks, tk), :]
        rel = [jnp.dot(kidx, qit_ref[0, h * IDX_DIM:(h + 1) * IDX_DIM, :], preferred_element_type=F32)
               for h in range(IDX_HEADS)]
        sc = jnp.zeros((tk, tq), F32)
        for h in range(IDX_HEADS):
            sc = sc + w[h:h + 1, :] * jnp.maximum(rel[h], 0.0)
        sc = jnp.where(sc == 0.0, 0.0, sc)
        bits = lax.bitcast_convert_type(sc, I32)
        key = bits ^ ((bits >> 31) & 0x7FFFFFFF)
        key_ref[pl.ds(ks, tk), :] = jnp.where(causal_of(ks), key, INT_MIN)
        return carry

    lax.fori_loop(0, n_tiles, score_tile, 0)

    def count_where(pred):
        def body(kt, cnt):
            return cnt + jnp.where(pred(key_ref[pl.ds(tile_start(kt), tk), :]), 1, 0)
        cnt = lax.fori_loop(0, n_tiles, body, jnp.zeros((tk, tq), I32))
        return jnp.sum(cnt, axis=0, keepdims=True)

    def bit_step(i, cand):
        trial = cand | lax.shift_left(jnp.int32(1), 31 - i)
        thr = trial ^ INT_MIN
        tot = count_where(lambda k: k >= thr)
        return jnp.where(tot >= n_sel, trial, cand)

    cand = lax.fori_loop(0, 32, bit_step, jnp.zeros((1, tq), I32))
    tau = cand ^ INT_MIN
    need = n_sel - count_where(lambda k: k > tau)

    m_ref[...] = jnp.full(m_ref.shape, NEG_BIG, F32)
    l_ref[...] = jnp.zeros(l_ref.shape, F32)
    acc_ref[...] = jnp.zeros(acc_ref.shape, F32)
    tie_ref[...] = jnp.zeros(tie_ref.shape, I32)
    lower_incl = jnp.where(lax.broadcasted_iota(I32, (tk, tk), 0) >= lax.broadcasted_iota(I32, (tk, tk), 1),
                           1.0, 0.0).astype(BF16)

    n_ge = count_where(lambda k: k >= tau)
    splits = jnp.where((n_ge > n_sel) & (tau != INT_MIN), 1.0, 0.0)
    any_split = jnp.max(splits) > 0.0

    def attend_tile(kt, general_bias, tie_split):
        ks = tile_start(kt)
        key = key_ref[pl.ds(ks, tk), :]
        if tie_split:
            eq = key == tau
            prefix = jnp.dot(lower_incl, jnp.where(eq, 1.0, 0.0).astype(BF16), preferred_element_type=F32)
            tie_before = tie_ref[0:1, :]
            tie_rank = tie_before + prefix.astype(I32)
            sel = ((key > tau) | (eq & (tie_rank <= need))) & causal_of(ks)
            tie_ref[0:1, :] = tie_before + prefix[tk - 1:tk, :].astype(I32)
        else:
            sel = (key >= tau) & causal_of(ks)

        if general_bias:
            posk = posk_ref[0, pl.ds(ks, tk), :]
            bucket = _t5_bucket_of(jnp.maximum(posq - posk, 0))
        kc = kc_ref[0, pl.ds(ks, tk), :]
        vt = vw_ref[0, 0:DSA_HEAD_DIM, pl.ds(ks, tk)].astype(BF16)
        m_all = m_ref[...]
        l_all = l_ref[...]
        logits = [jnp.dot(kc, qct_ref[0, h * DSA_HEAD_DIM:(h + 1) * DSA_HEAD_DIM, :], preferred_element_type=F32)
                  for h in range(DSA_HEADS)]
        for h in range(DSA_HEADS):
            s = logits[h]
            if general_bias:
                table = jnp.broadcast_to(tblt_ref[h:h + 1, :], (tk, LANES))
                s = s + jnp.concatenate(
                    [jnp.take_along_axis(table, bucket[:, c * LANES:(c + 1) * LANES], axis=1) for c in range(tq // LANES)],
                    axis=1)
            else:
                s = s + tbl_ref[REL_BUCKETS - 1, h]
            s = jnp.where(sel, s, NEG_BIG)
            m_old = m_all[h:h + 1, :]
            m_new = jnp.maximum(m_old, jnp.max(s, axis=0, keepdims=True))
            alpha = jnp.exp(m_old - m_new)
            p = jnp.exp(s - m_new)
            l_ref[h:h + 1, :] = alpha * l_all[h:h + 1, :] + jnp.sum(p, axis=0, keepdims=True)
            m_ref[h:h + 1, :] = m_new
            hs = slice(h * DSA_HEAD_DIM, (h + 1) * DSA_HEAD_DIM)
            acc_ref[hs, :] = alpha * acc_ref[hs, :] + jnp.dot(vt, p.astype(BF16), preferred_element_type=F32)

    bi = pl.program_id(0)

    def attend(kt, carry):
        far = (qmin_ref[bi, qi] - kmax_ref[bi, kt]) >= T5_FAR
        fast = jnp.logical_and(far, jnp.logical_not(any_split))
        plain = jnp.logical_and(jnp.logical_not(far), jnp.logical_not(any_split))

        @pl.when(fast)
        def _():
            attend_tile(kt, general_bias=False, tie_split=False)

        @pl.when(plain)
        def _():
            attend_tile(kt, general_bias=True, tie_split=False)

        @pl.when(any_split)
        def _():
            attend_tile(kt, general_bias=True, tie_split=True)

        return carry

    lax.fori_loop(0, n_tiles, attend, 0)

    for h in range(DSA_HEADS):
        hs = slice(h * DSA_HEAD_DIM, (h + 1) * DSA_HEAD_DIM)
        acc_ref[hs, :] = acc_ref[hs, :] / l_ref[h:h + 1, :]
    o_ref[0] = (acc_ref[...].T * gate_ref[0]).astype(o_ref.dtype)


def _dsa(rel_bias, qct, qit, kc, ki, vw, pos_col, pos_row, gate, tq=256):
    b, s, _ = kc.shape
    n_sel = min(TOPK_MAX, s // 4)
    vw_rows = vw.shape[1]
    tbl = rel_bias.astype(F32)
    tbl_t = jnp.pad(tbl.T, ((0, 0), (0, LANES - REL_BUCKETS)))
    q_min = jnp.min(pos_row.reshape(b, s // tq, tq), axis=-1)
    k_max = jnp.max(pos_row.reshape(b, s // DSA_KEY_TILE, DSA_KEY_TILE), axis=-1)
    vmem = (2 * (2 * _nbytes((DSA_WIDTH, tq), BF16) + 2 * _nbytes((s, LANES), BF16) + _nbytes((vw_rows, s), F32)
                 + _nbytes((vw_rows, tq), F32) + _nbytes((s, LANES), I32) + _nbytes((tq, DSA_WIDTH), F32)
                 + _nbytes((tq, DSA_WIDTH), BF16)) + _nbytes((s, tq), I32) + _nbytes((DSA_WIDTH, tq), F32)
            + 64 * _nbytes((DSA_KEY_TILE, tq), F32))
    return pl.pallas_call(
        functools.partial(_dsa_kernel, n_sel=n_sel, tq=tq),
        grid=(b, s // tq),
        in_specs=[pl.BlockSpec(memory_space=pltpu.SMEM), pl.BlockSpec(memory_space=pltpu.SMEM),
                  pl.BlockSpec(memory_space=pltpu.SMEM),
                  pl.BlockSpec((DSA_HEADS, LANES), lambda bi, i: (0, 0)),
                  pl.BlockSpec((1, DSA_WIDTH, tq), lambda bi, i: (bi, 0, i)),
                  pl.BlockSpec((1, DSA_WIDTH, tq), lambda bi, i: (bi, 0, i)),
                  pl.BlockSpec((1, s, DSA_HEAD_DIM), lambda bi, i: (bi, 0, 0)),
                  pl.BlockSpec((1, s, IDX_DIM), lambda bi, i: (bi, 0, 0)),
                  pl.BlockSpec((1, vw_rows, s), lambda bi, i: (bi, 0, 0)),
                  pl.BlockSpec((1, vw_rows, tq), lambda bi, i: (bi, 0, i)),
                  pl.BlockSpec((1, s, 1), lambda bi, i: (bi, 0, 0)),
                  pl.BlockSpec((1, 1, tq), lambda bi, i: (bi, 0, i)),
                  pl.BlockSpec((1, tq, DSA_WIDTH), lambda bi, i: (bi, i, 0))],
        out_specs=pl.BlockSpec((1, tq, DSA_WIDTH), lambda bi, i: (bi, i, 0)),
        out_shape=jax.ShapeDtypeStruct((b, s, DSA_WIDTH), BF16),
        scratch_shapes=[pltpu.VMEM((s, tq), I32), pltpu.VMEM((DSA_WIDTH, tq), F32),
                        pltpu.VMEM((DSA_HEADS, tq), F32), pltpu.VMEM((DSA_HEADS, tq), F32),
                        pltpu.VMEM((SUBLANES, tq), I32)],
        compiler_params=_compiler_params(("parallel", "arbitrary"), vmem),
        name="dsa",
    )(tbl, q_min, k_max, tbl_t, qct, qit, kc, ki, vw, vw, pos_col, pos_row, gate)


def _merge_kernel(oa_ref, ob_ref, oc_ref, g_ref, x_ref, p_ref, wa_ref, wb_ref, wc_ref, wo_ref, wpg_ref, wple_ref,
                  gn_ref, xo_ref, ho_ref):
    d = D_MODEL
    ya = jnp.dot(oa_ref[...], wa_ref[...], preferred_element_type=F32)
    yb = jnp.dot(ob_ref[...], wb_ref[...], preferred_element_type=F32)
    yc = jnp.dot(oc_ref[...], wc_ref[...], preferred_element_type=F32)
    merged = g_ref[:, 0:d] * ya + g_ref[:, d:2 * d] * yb + g_ref[:, 2 * d:3 * d] * yc
    x1 = x_ref[...] + jnp.dot(merged.astype(BF16), wo_ref[...], preferred_element_type=F32)
    ple_gate = _sigmoid(jnp.dot(x1.astype(BF16), wpg_ref[...], preferred_element_type=F32))
    x2 = x1 + ple_gate * jnp.dot(p_ref[...].astype(BF16), wple_ref[...], preferred_element_type=F32)
    xo_ref[...] = x2
    ho_ref[...] = _rms(x2, gn_ref[...]).astype(ho_ref.dtype)


def _merge(oa, ob, oc, gates, x, p, wa, wb, wc, wo, wpg, wple, g_next, h_dtype, tm=256):
    t, d = x.shape
    full = lambda a: pl.BlockSpec(a.shape, lambda i: (0, 0))
    row = lambda a: pl.BlockSpec((tm, a.shape[1]), lambda i: (i, 0))
    weights = (wa, wb, wc, wo, wpg, wple)
    acts = (oa, ob, oc, gates, x, p)
    vmem = (2 * (sum(_nbytes(a.shape, a.dtype) for a in weights) + sum(_nbytes((tm, a.shape[1]), a.dtype) for a in acts)
                 + 2 * _nbytes((tm, d), F32)) + 8 * _nbytes((tm, d), F32))
    return pl.pallas_call(
        _merge_kernel,
        grid=(t // tm,),
        in_specs=[row(a) for a in acts] + [full(a) for a in weights] + [pl.BlockSpec((1, d), lambda i: (0, 0))],
        out_specs=[pl.BlockSpec((tm, d), lambda i: (i, 0)), pl.BlockSpec((tm, d), lambda i: (i, 0))],
        out_shape=[jax.ShapeDtypeStruct((t, d), F32), jax.ShapeDtypeStruct((t, d), h_dtype)],
        compiler_params=_compiler_params(("parallel",), vmem),
        name="merge",
    )(*acts, *weights, g_next.reshape(1, d))


def _rotate_half_cols(w):
    half = w.shape[-1] // 2
    return jnp.concatenate([-w[..., half:], w[..., :half]], axis=-1)


def _prep_weights(w_in, w_uq, w_ukv):
    depth = w_in.shape[0]
    seg = lambda name: w_in[:, :, SPLIT[name][0]:SPLIT[name][1]]
    z = lambda *shape: jnp.zeros((depth,) + shape, w_in.dtype)
    d = D_MODEL
    kr = seg("k_rope")
    rope_lo = MLA_NOPE
    rope_pad = MLA_QK_PAD - MLA_NOPE - MLA_ROPE
    w = {}
    w["mla_in"] = jnp.concatenate(
        [seg("c_q"), seg("c_kv"), z(d, rope_lo), kr, z(d, rope_pad), z(d, rope_lo), _rotate_half_cols(kr), z(d, rope_pad)], axis=-1)
    w["gate_a"] = seg("gate_a")
    w["z"] = seg("z")
    w["xbc"] = seg("xbc")
    w["small"] = jnp.concatenate([seg("dt"), z(d, LANES - SSM_HEADS)], axis=-1)
    w["k_c"] = seg("k_c")
    w["k_idx"] = seg("k_idx")
    w["gate_c"] = seg("gate_c")
    w["merge"] = seg("merge")
    w["q_c_t"] = jnp.swapaxes(seg("q_c") * (DSA_HEAD_DIM ** -0.5), 1, 2)
    w["q_idx_t"] = jnp.swapaxes(seg("q_idx"), 1, 2)
    w["vw_t"] = jnp.swapaxes(jnp.concatenate([seg("v_c"), seg("w_idx")], axis=-1), 1, 2)
    w = {k: v.astype(BF16) for k, v in w.items()}

    uq = w_uq.reshape(depth, MLA_Q_LORA, MLA_HEADS, MLA_NOPE + MLA_ROPE)
    nope, rope = uq[..., :MLA_NOPE], uq[..., MLA_NOPE:]
    zq = lambda width: jnp.zeros((depth, MLA_Q_LORA, MLA_HEADS, width), w_uq.dtype)
    hq = MLA_HEADS * MLA_QK_PAD
    uq1 = jnp.concatenate([nope, rope, zq(rope_pad)], axis=-1).reshape(depth, MLA_Q_LORA, hq)
    uq2 = jnp.concatenate([zq(MLA_NOPE), _rotate_half_cols(rope), zq(rope_pad)], axis=-1).reshape(depth, MLA_Q_LORA, hq)
    w["uq1_t"] = jnp.swapaxes(uq1, 1, 2).astype(BF16)
    w["uq2_t"] = jnp.swapaxes(uq2, 1, 2).astype(BF16)
    ukv = w_ukv.reshape(depth, MLA_KV_LORA, MLA_HEADS, MLA_NOPE + MLA_V)
    zk = jnp.zeros((depth, MLA_KV_LORA, MLA_HEADS, MLA_QK_PAD - MLA_NOPE), w_ukv.dtype)
    w["uk"] = jnp.concatenate([ukv[..., :MLA_NOPE], zk], axis=-1).reshape(depth, MLA_KV_LORA, hq).astype(BF16)
    w["uv_t"] = jnp.swapaxes(ukv[..., MLA_NOPE:].reshape(depth, MLA_KV_LORA, MLA_WIDTH), 1, 2).astype(BF16)
    return w


def _rope_multipliers(positions):
    b, s = positions.shape
    inv_freq = 1.0 / (ROPE_THETA ** (jnp.arange(0, MLA_ROPE, 2, dtype=F32) / MLA_ROPE))
    ang = positions.astype(F32)[..., None] * inv_freq
    cos, sin = jnp.cos(ang), jnp.sin(ang)
    pad = MLA_QK_PAD - MLA_NOPE - MLA_ROPE
    m1 = jnp.concatenate([jnp.ones((b, s, MLA_NOPE), F32), cos, cos, jnp.zeros((b, s, pad), F32)], axis=-1)
    m2 = jnp.concatenate([jnp.zeros((b, s, MLA_NOPE), F32), sin, sin, jnp.zeros((b, s, pad), F32)], axis=-1)
    return m1, m2, jnp.swapaxes(m1, 1, 2), jnp.swapaxes(m2, 1, 2)


def kernel(x, p, positions, norm_g, w_in, mla_q_norm, w_uq, mla_kv_norm, w_ukv, conv_w, conv_b, dt_bias, a_log, d_skip, ssm_norm, w_br_a, w_br_b, w_br_c, w_out, rel_bias, w_ple, w_ple_gate, final_norm):
    b, s, d = x.shape
    depth = w_in.shape[0]
    t = b * s
    w = _prep_weights(w_in, w_uq, w_ukv)
    m1, m2, m1t, m2t = _rope_multipliers(positions)
    pos_col = positions.astype(I32).reshape(b, s, 1)
    pos_row = positions.astype(I32).reshape(b, 1, s)
    wbr_a, wbr_b, wbr_c = w_br_a.astype(BF16), w_br_b.astype(BF16), w_br_c.astype(BF16)
    wo, wpg, wple = w_out.astype(BF16), w_ple_gate.astype(BF16), w_ple.astype(BF16)

    x2d = x.reshape(t, d)
    h = _norm(x2d, norm_g[0], BF16)
    for i in range(depth):
        h3d = h.reshape(b, s, d)
        a_in = _proj(h, w["mla_in"][i], F32, name="proj_mla_in")
        gate_a = _proj(h, w["gate_a"][i], F32, act="silu", name="proj_gate_a")
        zs = _proj(h, w["z"][i], F32, act="silu", name="proj_z")
        xbc = _proj(h, w["xbc"][i], F32, name="proj_xbc")
        small = _proj(h, w["small"][i], F32, name="proj_small")
        k_c = _proj(h, w["k_c"][i], BF16, name="proj_k_c")
        k_idx = _proj(h, w["k_idx"][i], BF16, name="proj_k_idx")
        gate_c = _proj(h, w["gate_c"][i], F32, act="silu", name="proj_gate_c")
        gates = _proj(h, w["merge"][i], F32, act="sigmoid", name="proj_merge")
        qct = _proj_t(h3d, w["q_c_t"][i], BF16, name="proj_q_c_t")
        qit = _proj_t(h3d, w["q_idx_t"][i], BF16, name="proj_q_idx_t")
        vw = _proj_t(h3d, w["vw_t"][i], F32, name="proj_vw_t")

        qt, k, vt = _mla_prep(a_in.reshape(b, s, -1), m1, m2, m1t, m2t, mla_q_norm[i], mla_kv_norm[i],
                              w["uq1_t"][i], w["uq2_t"][i], w["uk"][i], w["uv_t"][i])
        o_a = _mla_attn(qt, k, vt, gate_a.reshape(b, s, -1))
        o_b = _ssd(xbc.reshape(b, s, -1), zs.reshape(b, s, -1), small.reshape(b, s, -1), conv_w[i], conv_b[i],
                   dt_bias[i], a_log[i], d_skip[i], ssm_norm[i])
        o_c = _dsa(rel_bias.astype(F32), qct, qit, k_c.reshape(b, s, -1), k_idx.reshape(b, s, -1), vw, pos_col, pos_row,
                   gate_c.reshape(b, s, -1))

        last = i == depth - 1
        g_next = final_norm if last else norm_g[i + 1]
        x2d, h = _merge(o_a.reshape(t, -1), o_b.reshape(t, -1), o_c.reshape(t, -1), gates, x2d, p[i].reshape(t, -1),
                        wbr_a[i], wbr_b[i], wbr_c[i], wo[i], wpg[i], wple[i], g_next, F32 if last else BF16)
    return h.reshape(b, s, d)
```

```python
import functools
import math

import jax
import jax.numpy as jnp
from jax import lax
from jax.experimental import pallas as pl
from jax.experimental.pallas import tpu as pltpu

F32 = jnp.float32
BF16 = jnp.bfloat16
I32 = jnp.int32

D_MODEL = 1024
PLE_DIM = 256
NORM_EPS = 1e-6

MLA_HEADS = 8
MLA_NOPE = 64
MLA_ROPE = 32
MLA_V = 64
MLA_Q_LORA = 384
MLA_KV_LORA = 256
MLA_WIDTH = MLA_HEADS * MLA_V
ROPE_THETA = 10000.0
MLA_QK_PAD = 128
MLA_LOGITS_AHEAD = 8
LOG2E = math.log2(math.e)
MLA_Q_SCALE = (MLA_NOPE + MLA_ROPE) ** -0.5 * LOG2E

SSM_HEADS = 16
SSM_HEAD_DIM = 64
SSM_INNER = SSM_HEADS * SSM_HEAD_DIM
SSM_GROUPS = 2
SSM_STATE = 128
SSM_CONV = 4
SSM_CHUNK = 128
SSM_CONV_DIM = SSM_INNER + 2 * SSM_GROUPS * SSM_STATE
SSM_HEADS_PER_GROUP = SSM_HEADS // SSM_GROUPS

DSA_HEADS = 8
DSA_HEAD_DIM = 64
DSA_WIDTH = DSA_HEADS * DSA_HEAD_DIM
IDX_HEADS = 8
IDX_DIM = 64
TOPK_MAX = 256
DSA_KEY_TILE = 128
DSA_Q_BLOCK = 256

REL_BUCKETS = 32
REL_MAX_DIST = 128
N_BRANCHES = 3

LANES = 128
SUBLANES = 8
VMEM_LIMIT_CAP = 56 * 1024 * 1024
INT_MIN = -(2 ** 31)
NEG_BIG = -1e30

SPLIT_SIZES = (
    MLA_Q_LORA, MLA_KV_LORA, MLA_ROPE, MLA_WIDTH, SSM_INNER, SSM_CONV_DIM, SSM_HEADS, DSA_WIDTH,
    DSA_HEAD_DIM, DSA_HEAD_DIM, IDX_HEADS * IDX_DIM, IDX_DIM, IDX_HEADS, DSA_WIDTH,
    N_BRANCHES * D_MODEL,
)
SPLIT_NAMES = ("c_q", "c_kv", "k_rope", "gate_a", "z", "xbc", "dt", "q_c", "k_c", "v_c", "q_idx",
               "k_idx", "w_idx", "gate_c", "merge")


def _split_bounds():
    out, off = {}, 0
    for name, size in zip(SPLIT_NAMES, SPLIT_SIZES, strict=True):
        out[name] = (off, off + size)
        off += size
    return out


SPLIT = _split_bounds()


def _t5_large_thresholds():
    exact = REL_BUCKETS // 2
    thr = []
    for j in range(1, REL_BUCKETS - exact):
        thr.append(int(math.ceil(exact * (REL_MAX_DIST / exact) ** (j / (REL_BUCKETS - exact)) - 1e-9)))
    return tuple(thr)


T5_EXACT = REL_BUCKETS // 2
T5_LARGE_THR = _t5_large_thresholds()
T5_FAR = T5_LARGE_THR[-1]


def _compiler_params(semantics, vmem_bytes):
    limit = int(min(VMEM_LIMIT_CAP, max(32 * 1024 * 1024, vmem_bytes)))
    return pltpu.CompilerParams(dimension_semantics=semantics, vmem_limit_bytes=limit)


def _nbytes(shape, dtype):
    return math.prod(shape) * jnp.dtype(dtype).itemsize


def _sigmoid(x):
    return 1.0 / (1.0 + jnp.exp(-x))


def _rms(x, g):
    return x * lax.rsqrt(jnp.mean(x * x, axis=-1, keepdims=True) + NORM_EPS) * g


def _norm_kernel(x_ref, g_ref, o_ref):
    o_ref[...] = _rms(x_ref[...], g_ref[...]).astype(o_ref.dtype)


def _norm(x2d, g, out_dtype, tm=512):
    t, d = x2d.shape
    return pl.pallas_call(
        _norm_kernel,
        grid=(t // tm,),
        in_specs=[pl.BlockSpec((tm, d), lambda i: (i, 0)), pl.BlockSpec((1, d), lambda i: (0, 0))],
        out_specs=pl.BlockSpec((tm, d), lambda i: (i, 0)),
        out_shape=jax.ShapeDtypeStruct((t, d), out_dtype),
        compiler_params=_compiler_params(("parallel",), 4 * _nbytes((tm, d), F32)),
        name="rms_norm",
    )(x2d, g.reshape(1, d))


def _proj_kernel(h_ref, w_ref, o_ref, *, act):
    y = jnp.dot(h_ref[...], w_ref[...], preferred_element_type=F32)
    if act == "silu":
        y = y * _sigmoid(y)
    elif act == "sigmoid":
        y = _sigmoid(y)
    o_ref[...] = y.astype(o_ref.dtype)


def _proj(h2d, w, out_dtype, act=None, tm=512, tn=None, name="proj"):
    t, k = h2d.shape
    n = w.shape[1]
    if tn is None:
        tn = n
        for cand in (1024, 768, 512, 896):
            if n > cand and n % cand == 0:
                tn = cand
                break
    vmem = 2 * (_nbytes((tm, k), BF16) + _nbytes((k, tn), BF16) + _nbytes((tm, tn), out_dtype)) + _nbytes((tm, tn), F32) * 2
    return pl.pallas_call(
        functools.partial(_proj_kernel, act=act),
        grid=(n // tn, t // tm),
        in_specs=[pl.BlockSpec((tm, k), lambda j, i: (i, 0)), pl.BlockSpec((k, tn), lambda j, i: (0, j))],
        out_specs=pl.BlockSpec((tm, tn), lambda j, i: (i, j)),
        out_shape=jax.ShapeDtypeStruct((t, n), out_dtype),
        compiler_params=_compiler_params(("parallel", "parallel"), vmem),
        name=name,
    )(h2d, w)


def _proj_t_kernel(h_ref, wt_ref, o_ref):
    y = lax.dot_general(wt_ref[...], h_ref[0], (((1,), (1,)), ((), ())), preferred_element_type=F32)
    o_ref[0] = y.astype(o_ref.dtype)


def _proj_t(h3d, wt, out_dtype, tm=512, name="proj_t"):
    b, s, k = h3d.shape
    n = wt.shape[0]
    vmem = 2 * (_nbytes((tm, k), BF16) + _nbytes((n, k), BF16) + _nbytes((n, tm), out_dtype)) + _nbytes((n, tm), F32) * 2
    return pl.pallas_call(
        _proj_t_kernel,
        grid=(b, s // tm),
        in_specs=[pl.BlockSpec((1, tm, k), lambda bi, i: (bi, i, 0)), pl.BlockSpec((n, k), lambda bi, i: (0, 0))],
        out_specs=pl.BlockSpec((1, n, tm), lambda bi, i: (bi, 0, i)),
        out_shape=jax.ShapeDtypeStruct((b, n, s), out_dtype),
        compiler_params=_compiler_params(("parallel", "parallel"), vmem),
        name=name,
    )(h3d, wt)


def _proj_wide_kernel(h_ref, wt_ref, o_ref, *, heads, dim):
    y = lax.dot_general(wt_ref[...], h_ref[0], (((1,), (1,)), ((), ())), preferred_element_type=F32)
    tq = y.shape[1]
    for h in range(heads):
        o_ref[0, 0, :, h * tq:(h + 1) * tq] = y[h * dim:(h + 1) * dim, :].astype(o_ref.dtype)


def _proj_wide(h3d, wt, heads, dim, tq, name):
    b, s, k = h3d.shape
    vmem = 2 * (_nbytes((tq, k), BF16) + _nbytes((heads * dim, k), BF16) + _nbytes((dim, heads * tq), BF16)) + 2 * _nbytes((heads * dim, tq), F32)
    return pl.pallas_call(
        functools.partial(_proj_wide_kernel, heads=heads, dim=dim),
        grid=(b, s // tq),
        in_specs=[pl.BlockSpec((1, tq, k), lambda bi, i: (bi, i, 0)), pl.BlockSpec((heads * dim, k), lambda bi, i: (0, 0))],
        out_specs=pl.BlockSpec((1, 1, dim, heads * tq), lambda bi, i: (bi, i, 0, 0)),
        out_shape=jax.ShapeDtypeStruct((b, s // tq, dim, heads * tq), BF16),
        compiler_params=_compiler_params(("parallel", "parallel"), vmem),
        name=name,
    )(h3d, wt)


def _mla_prep_kernel(a_ref, m1_ref, m2_ref, m1t_ref, m2t_ref, qn_ref, kvn_ref, wq1t_ref, wq2t_ref, wk_ref, wvt_ref,
                     qt_ref, k_ref, vt_ref):
    a = a_ref[0]
    c_q = a[:, :MLA_Q_LORA]
    c_kv = a[:, MLA_Q_LORA:MLA_Q_LORA + MLA_KV_LORA]
    kr1 = a[:, MLA_Q_LORA + MLA_KV_LORA:MLA_Q_LORA + MLA_KV_LORA + LANES]
    kr2 = a[:, MLA_Q_LORA + MLA_KV_LORA + LANES:]
    cqn = _rms(c_q, qn_ref[...]).astype(BF16)
    ckvn = _rms(c_kv, kvn_ref[...]).astype(BF16)
    nt = (((1,), (1,)), ((), ()))
    qa_t = lax.dot_general(wq1t_ref[...], cqn, nt, preferred_element_type=F32)
    qb_t = lax.dot_general(wq2t_ref[...], cqn, nt, preferred_element_type=F32)
    kn = jnp.dot(ckvn, wk_ref[...], preferred_element_type=F32)
    kr = kr1 * m1_ref[0] + kr2 * m2_ref[0]
    m1t = m1t_ref[0] * MLA_Q_SCALE
    m2t = m2t_ref[0] * MLA_Q_SCALE
    for h in range(MLA_HEADS):
        sl = slice(h * MLA_QK_PAD, (h + 1) * MLA_QK_PAD)
        qt_ref[0, sl, :] = (qa_t[sl, :] * m1t + qb_t[sl, :] * m2t).astype(qt_ref.dtype)
        k_ref[0, :, sl] = (kn[:, sl] + kr).astype(k_ref.dtype)
    vt_ref[0] = lax.dot_general(wvt_ref[...], ckvn, nt, preferred_element_type=F32).astype(vt_ref.dtype)


def _mla_prep(a, m1, m2, m1t, m2t, qn, kvn, wq1t, wq2t, wk, wvt, tm=256):
    b, s, wa = a.shape
    hq = MLA_HEADS * MLA_QK_PAD
    full = lambda shape: pl.BlockSpec(shape, lambda bi, i: (0, 0))
    row = lambda width: pl.BlockSpec((1, tm, width), lambda bi, i: (bi, i, 0))
    col = lambda height: pl.BlockSpec((1, height, tm), lambda bi, i: (bi, 0, i))
    vmem = (2 * (_nbytes((tm, wa), F32) + 4 * _nbytes((tm, LANES), F32) + 2 * _nbytes((MLA_Q_LORA, hq), BF16)
                 + _nbytes((MLA_KV_LORA, hq), BF16) + _nbytes((MLA_KV_LORA, MLA_WIDTH), BF16)
                 + 2 * _nbytes((tm, hq), BF16) + _nbytes((tm, MLA_WIDTH), BF16)) + 4 * _nbytes((tm, hq), F32))
    return pl.pallas_call(
        _mla_prep_kernel,
        grid=(b, s // tm),
        in_specs=[row(wa), row(LANES), row(LANES), col(LANES), col(LANES), full((1, MLA_Q_LORA)), full((1, MLA_KV_LORA)),
                  full((hq, MLA_Q_LORA)), full((hq, MLA_Q_LORA)), full((MLA_KV_LORA, hq)), full((MLA_WIDTH, MLA_KV_LORA))],
        out_specs=[col(hq), row(hq), col(MLA_WIDTH)],
        out_shape=[jax.ShapeDtypeStruct((b, hq, s), BF16), jax.ShapeDtypeStruct((b, s, hq), BF16),
                   jax.ShapeDtypeStruct((b, MLA_WIDTH, s), BF16)],
        compiler_params=_compiler_params(("parallel", "parallel"), vmem),
        name="mla_prep",
    )(a, m1, m2, m1t, m2t, qn.reshape(1, -1), kvn.reshape(1, -1), wq1t, wq2t, wk, wvt)


def _mla_attn_kernel(qt_ref, k_ref, vt_ref, g_ref, o_ref, m_ref, l_ref, acc_ref, *, tq, tk):
    qi = pl.program_id(1)
    krow = lax.broadcasted_iota(I32, (tk, tq), 0)
    qcol = lax.broadcasted_iota(I32, (tk, tq), 1)
    tiles_per_q = tq // tk
    m_ref[...] = jnp.full(m_ref.shape, -jnp.inf, F32)
    l_ref[...] = jnp.zeros(l_ref.shape, F32)
    acc_ref[...] = jnp.zeros(acc_ref.shape, F32)

    def tile(j, diagonal):
        ks = pl.multiple_of(j * tk, tk)
        m_all = m_ref[...]
        l_all = l_ref[...]
        if diagonal:
            causal = ks + krow <= qi * tq + qcol

        def logits(h):
            k = k_ref[0, pl.ds(ks, tk), h * MLA_QK_PAD:(h + 1) * MLA_QK_PAD]
            return jnp.dot(k, qt_ref[0, h * MLA_QK_PAD:(h + 1) * MLA_QK_PAD, :], preferred_element_type=F32)

        ahead = MLA_LOGITS_AHEAD
        pending = [logits(h) for h in range(ahead)]
        for h in range(MLA_HEADS):
            s = pending.pop(0)
            if h + ahead < MLA_HEADS:
                pending.append(logits(h + ahead))
            if diagonal:
                s = jnp.where(causal, s, -jnp.inf)
            m_old = m_all[h:h + 1, :]
            m_new = jnp.maximum(m_old, jnp.max(s, axis=0, keepdims=True))
            alpha = jnp.exp2(m_old - m_new)
            p = jnp.exp2(s - m_new)
            l_ref[h:h + 1, :] = alpha * l_all[h:h + 1, :] + jnp.sum(p, axis=0, keepdims=True)
            m_ref[h:h + 1, :] = m_new
            hs = slice(h * MLA_V, (h + 1) * MLA_V)
            pv = jnp.dot(vt_ref[0, hs, pl.ds(ks, tk)], p.astype(BF16), preferred_element_type=F32)
            acc_ref[hs, :] = alpha * acc_ref[hs, :] + pv

    def full_tile(j, carry):
        tile(j, diagonal=False)
        return carry

    lax.fori_loop(0, qi * tiles_per_q, full_tile, 0)
    for d in range(tiles_per_q):
        tile(qi * tiles_per_q + d, diagonal=True)
    for h in range(MLA_HEADS):
        hs = slice(h * MLA_V, (h + 1) * MLA_V)
        acc_ref[hs, :] = acc_ref[hs, :] / l_ref[h:h + 1, :]
    o_ref[0] = (acc_ref[...].T * g_ref[0]).astype(o_ref.dtype)


def _mla_attn(qt, k, vt, gate, tq=256, tk=128):
    b, s, hq = k.shape
    vmem = 2 * (_nbytes((hq, tq), BF16) + _nbytes((s, hq), BF16) + _nbytes((MLA_WIDTH, s), BF16)
                + _nbytes((tq, MLA_WIDTH), F32) + _nbytes((tq, MLA_WIDTH), BF16)) + _nbytes((MLA_WIDTH, tq), F32) + 16 * _nbytes((tk, tq), F32)
    return pl.pallas_call(
        functools.partial(_mla_attn_kernel, tq=tq, tk=tk),
        grid=(b, s // tq),
        in_specs=[pl.BlockSpec((1, hq, tq), lambda bi, i: (bi, 0, i)),
                  pl.BlockSpec((1, s, hq), lambda bi, i: (bi, 0, 0)),
                  pl.BlockSpec((1, MLA_WIDTH, s), lambda bi, i: (bi, 0, 0)),
                  pl.BlockSpec((1, tq, MLA_WIDTH), lambda bi, i: (bi, i, 0))],
        out_specs=pl.BlockSpec((1, tq, MLA_WIDTH), lambda bi, i: (bi, i, 0)),
        out_shape=jax.ShapeDtypeStruct((b, s, MLA_WIDTH), BF16),
        scratch_shapes=[pltpu.VMEM((MLA_HEADS, tq), F32), pltpu.VMEM((MLA_HEADS, tq), F32),
                        pltpu.VMEM((MLA_WIDTH, tq), F32)],
        compiler_params=_compiler_params(("parallel", "arbitrary"), vmem),
        name="mla_attn",
    )(qt, k, vt, gate)


def _ssd_kernel(xbc_ref, zs_ref, sm_ref, cw_ref, cb_ref, dtb_ref, alog_ref, dsk_ref, nrm_ref, o_ref,
                xwin_ref, state_ref, y_ref):
    c = pl.program_id(1)
    q = SSM_CHUNK
    tail = SUBLANES

    @pl.when(c == 0)
    def _():
        xwin_ref[0:tail, :] = jnp.zeros((tail, SSM_CONV_DIM), F32)
        state_ref[...] = jnp.zeros(state_ref.shape, F32)

    x = xbc_ref[0]
    xwin_ref[tail:tail + q, :] = x
    acc = cb_ref[...] + cw_ref[SSM_CONV - 1:SSM_CONV, :] * x
    for j in range(1, SSM_CONV):
        acc = acc + cw_ref[SSM_CONV - 1 - j:SSM_CONV - j, :] * xwin_ref[tail - j:tail - j + q, :]
    xwin_ref[0:tail, :] = x[q - tail:, :]
    xc = acc * _sigmoid(acc)

    xs = xc[:, :SSM_INNER]
    bm = xc[:, SSM_INNER:SSM_INNER + SSM_GROUPS * SSM_STATE]
    cm = xc[:, SSM_INNER + SSM_GROUPS * SSM_STATE:]

    pre = sm_ref[0] + dtb_ref[...]
    dt = jnp.maximum(pre, 0.0) + jnp.log1p(jnp.exp(-jnp.abs(pre)))
    a = -jnp.exp(alog_ref[...])
    row = lax.broadcasted_iota(I32, (q, q), 0)
    col = lax.broadcasted_iota(I32, (q, q), 1)
    lower = row >= col
    cum = jnp.dot(jnp.where(lower, 1.0, 0.0).astype(F32), dt * a, preferred_element_type=F32,
                  precision=lax.Precision.HIGHEST)
    cum_t = cum.T
    cum_last = cum[q - 1:q, :]
    e_cum = jnp.exp(cum)
    decay_to_end = jnp.exp(cum_last - cum)
    chunk_decay = jnp.exp(cum_last)
    dsk = dsk_ref[...]

    for g in range(SSM_GROUPS):
        bg = bm[:, g * SSM_STATE:(g + 1) * SSM_STATE].astype(BF16)
        cg = cm[:, g * SSM_STATE:(g + 1) * SSM_STATE].astype(BF16)
        cb = lax.dot_general(cg, bg, (((1,), (1,)), ((), ())), preferred_element_type=F32)
        for hh in range(SSM_HEADS_PER_GROUP):
            h = g * SSM_HEADS_PER_GROUP + hh
            xh = xs[:, h * SSM_HEAD_DIM:(h + 1) * SSM_HEAD_DIM]
            xdt = xh * dt[:, h:h + 1]
            diff = cum[:, h:h + 1] - cum_t[h:h + 1, :]
            decay_in = jnp.exp(jnp.where(lower, diff, -jnp.inf))
            y_diag = jnp.dot((cb * decay_in).astype(BF16), xdt.astype(BF16), preferred_element_type=F32)
            st = state_ref[h]
            y_off = lax.dot_general(cg, st.astype(BF16), (((1,), (1,)), ((), ())),
                                    preferred_element_type=F32) * e_cum[:, h:h + 1]
            xw_t = (xdt * decay_to_end[:, h:h + 1]).T.astype(BF16)
            chunk_state = jnp.dot(xw_t, bg, preferred_element_type=F32)
            state_ref[h] = st * chunk_decay[:, h:h + 1] + chunk_state
            y_ref[:, h * SSM_HEAD_DIM:(h + 1) * SSM_HEAD_DIM] = y_diag + y_off + xh * dsk[:, h:h + 1]

    yg = y_ref[...] * zs_ref[0]
    o_ref[0] = _rms(yg, nrm_ref[...]).astype(o_ref.dtype)


def _ssd(xbc, zs, small, conv_w, conv_b, dt_bias, a_log, d_skip, ssm_norm):
    b, s, _ = xbc.shape
    q = SSM_CHUNK
    pad = lambda v: jnp.pad(v.astype(F32), (0, LANES - v.shape[0])).reshape(1, LANES)
    full = lambda shape: pl.BlockSpec(shape, lambda bi, c: (0,) * len(shape))
    blk = lambda width: pl.BlockSpec((1, q, width), lambda bi, c: (bi, c, 0))
    vmem = (2 * (_nbytes((q, SSM_CONV_DIM), F32) + _nbytes((q, SSM_INNER), F32) + _nbytes((q, LANES), F32)
                 + _nbytes((q, SSM_INNER), BF16)) + _nbytes((q + SUBLANES, SSM_CONV_DIM), F32)
            + _nbytes((SSM_HEADS, SSM_HEAD_DIM, SSM_STATE), F32) + 8 * _nbytes((q, SSM_CONV_DIM), F32))
    return pl.pallas_call(
        _ssd_kernel,
        grid=(b, s // q),
        in_specs=[blk(SSM_CONV_DIM), blk(SSM_INNER), blk(LANES), full((SSM_CONV, SSM_CONV_DIM)), full((1, SSM_CONV_DIM)),
                  full((1, LANES)), full((1, LANES)), full((1, LANES)), full((1, SSM_INNER))],
        out_specs=blk(SSM_INNER),
        out_shape=jax.ShapeDtypeStruct((b, s, SSM_INNER), BF16),
        scratch_shapes=[pltpu.VMEM((q + SUBLANES, SSM_CONV_DIM), F32),
                        pltpu.VMEM((SSM_HEADS, SSM_HEAD_DIM, SSM_STATE), F32),
                        pltpu.VMEM((q, SSM_INNER), F32)],
        compiler_params=_compiler_params(("parallel", "arbitrary"), vmem),
        name="ssd",
    )(xbc, zs, small, conv_w.astype(F32), conv_b.reshape(1, -1).astype(F32), pad(dt_bias), pad(a_log), pad(d_skip),
      ssm_norm.reshape(1, -1).astype(F32))


def _t5_bucket_of(n):
    large = jnp.full(n.shape, T5_EXACT, I32)
    for thr in T5_LARGE_THR:
        large = large + jnp.where(n >= thr, 1, 0)
    return jnp.where(n < T5_EXACT, n, large)


def _dsa_kernel(tbl_ref, qmin_ref, kmax_ref, tblt_ref, qcw_ref, qiw_ref, kc_ref, ki_ref, vw_ref, wq_ref, posk_ref, posq_ref, gate_ref, o_ref,
                key_ref, s0_ref, s1_ref, p_ref, alpha_ref, acc_ref, ot_ref, m_ref, l_ref, tie_ref, *, n_sel, tq):
    tk = DSA_KEY_TILE
    qi = pl.program_id(1)
    n_tiles = (qi + 1) * (tq // tk)
    krow = lax.broadcasted_iota(I32, (tk, tq), 0)
    qcol = lax.broadcasted_iota(I32, (tk, tq), 1)
    w = wq_ref[0, DSA_HEAD_DIM:DSA_HEAD_DIM + IDX_HEADS, :]
    posq = posq_ref[0]

    def tile_start(kt):
        return pl.multiple_of(kt * tk, tk)

    def causal_of(ks):
        return (ks + krow) <= (qi * tq + qcol)

    def head_lanes(h):
        return slice(h * tq, (h + 1) * tq)

    staged = (s0_ref, s1_ref)

    def stage(k_ref, q_ref, kt, dst_ref):
        dst_ref[...] = jnp.dot(k_ref[0, pl.ds(tile_start(kt), tk), :], q_ref[0, 0], preferred_element_type=F32)

    def tile_pairs(tile_fn):
        def pair(pi, carry):
            for half in range(2):
                tile_fn(2 * pi + half, staged[half], staged[1 - half])
            return carry
        lax.fori_loop(0, n_tiles // 2, pair, 0)

    stage(ki_ref, qiw_ref, 0, staged[0])

    def score_tile(kt, cur_ref, next_ref):
        ks = tile_start(kt)
        stage(ki_ref, qiw_ref, jnp.minimum(kt + 1, n_tiles - 1), next_ref)
        sc = jnp.zeros((tk, tq), F32)
        for h in range(IDX_HEADS):
            sc = sc + w[h:h + 1, :] * jnp.maximum(cur_ref[:, head_lanes(h)], 0.0)
        sc = jnp.where(sc == 0.0, 0.0, sc)
        bits = lax.bitcast_convert_type(sc, I32)
        key = bits ^ ((bits >> 31) & 0x7FFFFFFF)
        key_ref[pl.ds(ks, tk), :] = jnp.where(causal_of(ks), key, INT_MIN)

    tile_pairs(score_tile)

    def count_where(pred):
        def body(kt, cnt):
            return cnt + jnp.where(pred(key_ref[pl.ds(tile_start(kt), tk), :]), 1, 0)
        cnt = lax.fori_loop(0, n_tiles, body, jnp.zeros((tk, tq), I32))
        return jnp.sum(cnt, axis=0, keepdims=True)

    def bit_step(i, cand):
        trial = cand | lax.shift_left(jnp.int32(1), 31 - i)
        thr = trial ^ INT_MIN
        tot = count_where(lambda k: k >= thr)
        return jnp.where(tot >= n_sel, trial, cand)

    cand = lax.fori_loop(0, 32, bit_step, jnp.zeros((1, tq), I32))
    tau = cand ^ INT_MIN
    need = n_sel - count_where(lambda k: k > tau)

    m_ref[...] = jnp.full(m_ref.shape, NEG_BIG, F32)
    l_ref[...] = jnp.zeros(l_ref.shape, F32)
    acc_ref[...] = jnp.zeros(acc_ref.shape, F32)
    tie_ref[...] = jnp.zeros(tie_ref.shape, I32)
    lower_incl = jnp.where(lax.broadcasted_iota(I32, (tk, tk), 0) >= lax.broadcasted_iota(I32, (tk, tk), 1),
                           1.0, 0.0).astype(BF16)

    n_ge = count_where(lambda k: k >= tau)
    splits = jnp.where((n_ge > n_sel) & (tau != INT_MIN), 1.0, 0.0)
    any_split = jnp.max(splits) > 0.0

    def attend_tile(kt, cur_ref, next_ref, general_bias, tie_split):
        ks = tile_start(kt)
        key = key_ref[pl.ds(ks, tk), :]
        if tie_split:
            eq = key == tau
            prefix = jnp.dot(lower_incl, jnp.where(eq, 1.0, 0.0).astype(BF16), preferred_element_type=F32)
            tie_before = tie_ref[0:1, :]
            tie_rank = tie_before + prefix.astype(I32)
            sel = ((key > tau) | (eq & (tie_rank <= need))) & causal_of(ks)
            tie_ref[0:1, :] = tie_before + prefix[tk - 1:tk, :].astype(I32)
        else:
            sel = (key >= tau) & causal_of(ks)

        if general_bias:
            posk = posk_ref[0, pl.ds(ks, tk), :]
            bucket = _t5_bucket_of(jnp.maximum(posq - posk, 0))
        stage(kc_ref, qcw_ref, jnp.minimum(kt + 1, n_tiles - 1), next_ref)
        m_all = m_ref[0:1, :]
        l_all = l_ref[0:1, :]
        for h in range(DSA_HEADS):
            hl = head_lanes(h)
            s = cur_ref[:, hl]
            if general_bias:
                table = jnp.broadcast_to(tblt_ref[h:h + 1, :], (tk, LANES))
                s = s + jnp.concatenate(
                    [jnp.take_along_axis(table, bucket[:, c * LANES:(c + 1) * LANES], axis=1) for c in range(tq // LANES)],
                    axis=1)
            else:
                s = s + tbl_ref[REL_BUCKETS - 1, h]
            s = jnp.where(sel, s, NEG_BIG)
            m_old = m_all[:, hl]
            m_new = jnp.maximum(m_old, jnp.max(s, axis=0, keepdims=True))
            alpha = jnp.exp2(m_old - m_new)
            p = jnp.exp2(s - m_new)
            l_ref[0:1, hl] = alpha * l_all[:, hl] + jnp.sum(p, axis=0, keepdims=True)
            m_ref[0:1, hl] = m_new
            alpha_ref[0:1, hl] = alpha
            p_ref[:, hl] = p.astype(BF16)
        vt = vw_ref[0, 0:DSA_HEAD_DIM, pl.ds(ks, tk)].astype(BF16)
        pv = jnp.dot(vt, p_ref[...], preferred_element_type=F32)
        acc_ref[...] = alpha_ref[0:1, :] * acc_ref[...] + pv

    bi = pl.program_id(0)
    stage(kc_ref, qcw_ref, 0, staged[0])

    def attend(kt, cur_ref, next_ref):
        far = (qmin_ref[bi, qi] - kmax_ref[bi, kt]) >= T5_FAR
        fast = jnp.logical_and(far, jnp.logical_not(any_split))
        plain = jnp.logical_and(jnp.logical_not(far), jnp.logical_not(any_split))

        @pl.when(fast)
        def _():
            attend_tile(kt, cur_ref, next_ref, general_bias=False, tie_split=False)

        @pl.when(plain)
        def _():
            attend_tile(kt, cur_ref, next_ref, general_bias=True, tie_split=False)

        @pl.when(any_split)
        def _():
            attend_tile(kt, cur_ref, next_ref, general_bias=True, tie_split=True)

    tile_pairs(attend)

    for h in range(DSA_HEADS):
        hl = head_lanes(h)
        ot_ref[h * DSA_HEAD_DIM:(h + 1) * DSA_HEAD_DIM, :] = acc_ref[:, hl] / l_ref[0:1, hl]
    o_ref[0] = (ot_ref[...].T * gate_ref[0]).astype(o_ref.dtype)


def _dsa(rel_bias, qcw, qiw, kc, ki, vw, pos_col, pos_row, gate, tq):
    b, s, _ = kc.shape
    n_sel = min(TOPK_MAX, s // 4)
    vw_rows = vw.shape[1]
    hw = DSA_HEADS * tq
    tbl = rel_bias.astype(F32) * LOG2E
    tbl_t = jnp.pad(tbl.T, ((0, 0), (0, LANES - REL_BUCKETS)))
    q_min = jnp.min(pos_row.reshape(b, s // tq, tq), axis=-1)
    k_max = jnp.max(pos_row.reshape(b, s // DSA_KEY_TILE, DSA_KEY_TILE), axis=-1)
    tk = DSA_KEY_TILE
    vmem = (2 * (2 * _nbytes((DSA_HEAD_DIM, hw), BF16) + 2 * _nbytes((s, LANES), BF16) + _nbytes((vw_rows, s), F32)
                 + _nbytes((vw_rows, tq), F32) + _nbytes((s, LANES), I32) + _nbytes((tq, DSA_WIDTH), F32)
                 + _nbytes((tq, DSA_WIDTH), BF16)) + _nbytes((s, tq), I32) + 2 * _nbytes((tk, hw), F32)
            + _nbytes((tk, hw), BF16) + _nbytes((DSA_HEAD_DIM, hw), F32) + _nbytes((DSA_WIDTH, tq), F32)
            + 32 * _nbytes((tk, tq), F32))
    return pl.pallas_call(
        functools.partial(_dsa_kernel, n_sel=n_sel, tq=tq),
        grid=(b, s // tq),
        in_specs=[pl.BlockSpec(memory_space=pltpu.SMEM), pl.BlockSpec(memory_space=pltpu.SMEM),
                  pl.BlockSpec(memory_space=pltpu.SMEM),
                  pl.BlockSpec((DSA_HEADS, LANES), lambda bi, i: (0, 0)),
                  pl.BlockSpec((1, 1, DSA_HEAD_DIM, hw), lambda bi, i: (bi, i, 0, 0)),
                  pl.BlockSpec((1, 1, IDX_DIM, hw), lambda bi, i: (bi, i, 0, 0)),
                  pl.BlockSpec((1, s, DSA_HEAD_DIM), lambda bi, i: (bi, 0, 0)),
                  pl.BlockSpec((1, s, IDX_DIM), lambda bi, i: (bi, 0, 0)),
                  pl.BlockSpec((1, vw_rows, s), lambda bi, i: (bi, 0, 0)),
                  pl.BlockSpec((1, vw_rows, tq), lambda bi, i: (bi, 0, i)),
                  pl.BlockSpec((1, s, 1), lambda bi, i: (bi, 0, 0)),
                  pl.BlockSpec((1, 1, tq), lambda bi, i: (bi, 0, i)),
                  pl.BlockSpec((1, tq, DSA_WIDTH), lambda bi, i: (bi, i, 0))],
        out_specs=pl.BlockSpec((1, tq, DSA_WIDTH), lambda bi, i: (bi, i, 0)),
        out_shape=jax.ShapeDtypeStruct((b, s, DSA_WIDTH), BF16),
        scratch_shapes=[pltpu.VMEM((s, tq), I32),
                        pltpu.VMEM((tk, hw), F32), pltpu.VMEM((tk, hw), F32),
                        pltpu.VMEM((tk, hw), BF16),
                        pltpu.VMEM((SUBLANES, hw), F32),
                        pltpu.VMEM((DSA_HEAD_DIM, hw), F32),
                        pltpu.VMEM((DSA_WIDTH, tq), F32),
                        pltpu.VMEM((SUBLANES, hw), F32), pltpu.VMEM((SUBLANES, hw), F32),
                        pltpu.VMEM((SUBLANES, tq), I32)],
        compiler_params=_compiler_params(("parallel", "arbitrary"), vmem),
        name="dsa",
    )(tbl, q_min, k_max, tbl_t, qcw, qiw, kc, ki, vw, vw, pos_col, pos_row, gate)


def _merge_kernel(oa_ref, ob_ref, oc_ref, g_ref, x_ref, p_ref, wa_ref, wb_ref, wc_ref, wo_ref, wpg_ref, wple_ref,
                  gn_ref, xo_ref, ho_ref):
    d = D_MODEL
    ya = jnp.dot(oa_ref[...], wa_ref[...], preferred_element_type=F32)
    yb = jnp.dot(ob_ref[...], wb_ref[...], preferred_element_type=F32)
    yc = jnp.dot(oc_ref[...], wc_ref[...], preferred_element_type=F32)
    merged = g_ref[:, 0:d] * ya + g_ref[:, d:2 * d] * yb + g_ref[:, 2 * d:3 * d] * yc
    x1 = x_ref[...] + jnp.dot(merged.astype(BF16), wo_ref[...], preferred_element_type=F32)
    ple_gate = _sigmoid(jnp.dot(x1.astype(BF16), wpg_ref[...], preferred_element_type=F32))
    x2 = x1 + ple_gate * jnp.dot(p_ref[...].astype(BF16), wple_ref[...], preferred_element_type=F32)
    xo_ref[...] = x2
    ho_ref[...] = _rms(x2, gn_ref[...]).astype(ho_ref.dtype)


def _merge(oa, ob, oc, gates, x, p, wa, wb, wc, wo, wpg, wple, g_next, h_dtype, tm=256):
    t, d = x.shape
    full = lambda a: pl.BlockSpec(a.shape, lambda i: (0, 0))
    row = lambda a: pl.BlockSpec((tm, a.shape[1]), lambda i: (i, 0))
    weights = (wa, wb, wc, wo, wpg, wple)
    acts = (oa, ob, oc, gates, x, p)
    vmem = (2 * (sum(_nbytes(a.shape, a.dtype) for a in weights) + sum(_nbytes((tm, a.shape[1]), a.dtype) for a in acts)
                 + 2 * _nbytes((tm, d), F32)) + 8 * _nbytes((tm, d), F32))
    return pl.pallas_call(
        _merge_kernel,
        grid=(t // tm,),
        in_specs=[row(a) for a in acts] + [full(a) for a in weights] + [pl.BlockSpec((1, d), lambda i: (0, 0))],
        out_specs=[pl.BlockSpec((tm, d), lambda i: (i, 0)), pl.BlockSpec((tm, d), lambda i: (i, 0))],
        out_shape=[jax.ShapeDtypeStruct((t, d), F32), jax.ShapeDtypeStruct((t, d), h_dtype)],
        compiler_params=_compiler_params(("parallel",), vmem),
        name="merge",
    )(*acts, *weights, g_next.reshape(1, d))


def _rotate_half_cols(w):
    half = w.shape[-1] // 2
    return jnp.concatenate([-w[..., half:], w[..., :half]], axis=-1)


def _prep_weights(w_in, w_uq, w_ukv):
    depth = w_in.shape[0]
    seg = lambda name: w_in[:, :, SPLIT[name][0]:SPLIT[name][1]]
    z = lambda *shape: jnp.zeros((depth,) + shape, w_in.dtype)
    d = D_MODEL
    kr = seg("k_rope")
    rope_lo = MLA_NOPE
    rope_pad = MLA_QK_PAD - MLA_NOPE - MLA_ROPE
    w = {}
    w["mla_in"] = jnp.concatenate(
        [seg("c_q"), seg("c_kv"), z(d, rope_lo), kr, z(d, rope_pad), z(d, rope_lo), _rotate_half_cols(kr), z(d, rope_pad)], axis=-1)
    w["gate_a"] = seg("gate_a")
    w["z"] = seg("z")
    w["xbc"] = seg("xbc")
    w["small"] = jnp.concatenate([seg("dt"), z(d, LANES - SSM_HEADS)], axis=-1)
    w["k_c"] = seg("k_c")
    w["k_idx"] = seg("k_idx")
    w["gate_c"] = seg("gate_c")
    w["merge"] = seg("merge")
    w["q_c_t"] = jnp.swapaxes(seg("q_c") * (DSA_HEAD_DIM ** -0.5 * LOG2E), 1, 2)
    w["q_idx_t"] = jnp.swapaxes(seg("q_idx"), 1, 2)
    w["vw_t"] = jnp.swapaxes(jnp.concatenate([seg("v_c"), seg("w_idx")], axis=-1), 1, 2)
    w = {k: v.astype(BF16) for k, v in w.items()}

    uq = w_uq.reshape(depth, MLA_Q_LORA, MLA_HEADS, MLA_NOPE + MLA_ROPE)
    nope, rope = uq[..., :MLA_NOPE], uq[..., MLA_NOPE:]
    zq = lambda width: jnp.zeros((depth, MLA_Q_LORA, MLA_HEADS, width), w_uq.dtype)
    hq = MLA_HEADS * MLA_QK_PAD
    uq1 = jnp.concatenate([nope, rope, zq(rope_pad)], axis=-1).reshape(depth, MLA_Q_LORA, hq)
    uq2 = jnp.concatenate([zq(MLA_NOPE), _rotate_half_cols(rope), zq(rope_pad)], axis=-1).reshape(depth, MLA_Q_LORA, hq)
    w["uq1_t"] = jnp.swapaxes(uq1, 1, 2).astype(BF16)
    w["uq2_t"] = jnp.swapaxes(uq2, 1, 2).astype(BF16)
    ukv = w_ukv.reshape(depth, MLA_KV_LORA, MLA_HEADS, MLA_NOPE + MLA_V)
    zk = jnp.zeros((depth, MLA_KV_LORA, MLA_HEADS, MLA_QK_PAD - MLA_NOPE), w_ukv.dtype)
    w["uk"] = jnp.concatenate([ukv[..., :MLA_NOPE], zk], axis=-1).reshape(depth, MLA_KV_LORA, hq).astype(BF16)
    w["uv_t"] = jnp.swapaxes(ukv[..., MLA_NOPE:].reshape(depth, MLA_KV_LORA, MLA_WIDTH), 1, 2).astype(BF16)
    return w


def _rope_multipliers(positions):
    b, s = positions.shape
    inv_freq = 1.0 / (ROPE_THETA ** (jnp.arange(0, MLA_ROPE, 2, dtype=F32) / MLA_ROPE))
    ang = positions.astype(F32)[..., None] * inv_freq
    cos, sin = jnp.cos(ang), jnp.sin(ang)
    pad = MLA_QK_PAD - MLA_NOPE - MLA_ROPE
    m1 = jnp.concatenate([jnp.ones((b, s, MLA_NOPE), F32), cos, cos, jnp.zeros((b, s, pad), F32)], axis=-1)
    m2 = jnp.concatenate([jnp.zeros((b, s, MLA_NOPE), F32), sin, sin, jnp.zeros((b, s, pad), F32)], axis=-1)
    return m1, m2, jnp.swapaxes(m1, 1, 2), jnp.swapaxes(m2, 1, 2)


def kernel(x, p, positions, norm_g, w_in, mla_q_norm, w_uq, mla_kv_norm, w_ukv, conv_w, conv_b, dt_bias, a_log, d_skip, ssm_norm, w_br_a, w_br_b, w_br_c, w_out, rel_bias, w_ple, w_ple_gate, final_norm):
    b, s, d = x.shape
    depth = w_in.shape[0]
    t = b * s
    w = _prep_weights(w_in, w_uq, w_ukv)
    m1, m2, m1t, m2t = _rope_multipliers(positions)
    pos_col = positions.astype(I32).reshape(b, s, 1)
    pos_row = positions.astype(I32).reshape(b, 1, s)
    wbr_a, wbr_b, wbr_c = w_br_a.astype(BF16), w_br_b.astype(BF16), w_br_c.astype(BF16)
    wo, wpg, wple = w_out.astype(BF16), w_ple_gate.astype(BF16), w_ple.astype(BF16)

    x2d = x.reshape(t, d)
    h = _norm(x2d, norm_g[0], BF16)
    for i in range(depth):
        h3d = h.reshape(b, s, d)
        a_in = _proj(h, w["mla_in"][i], F32, name="proj_mla_in")
        gate_a = _proj(h, w["gate_a"][i], F32, act="silu", name="proj_gate_a")
        zs = _proj(h, w["z"][i], F32, act="silu", name="proj_z")
        xbc = _proj(h, w["xbc"][i], F32, name="proj_xbc")
        small = _proj(h, w["small"][i], F32, name="proj_small")
        k_c = _proj(h, w["k_c"][i], BF16, name="proj_k_c")
        k_idx = _proj(h, w["k_idx"][i], BF16, name="proj_k_idx")
        gate_c = _proj(h, w["gate_c"][i], F32, act="silu", name="proj_gate_c")
        gates = _proj(h, w["merge"][i], F32, act="sigmoid", name="proj_merge")
        qcw = _proj_wide(h3d, w["q_c_t"][i], DSA_HEADS, DSA_HEAD_DIM, DSA_Q_BLOCK, name="proj_q_c_wide")
        qiw = _proj_wide(h3d, w["q_idx_t"][i], IDX_HEADS, IDX_DIM, DSA_Q_BLOCK, name="proj_q_idx_wide")
        vw = _proj_t(h3d, w["vw_t"][i], F32, name="proj_vw_t")

        qt, k, vt = _mla_prep(a_in.reshape(b, s, -1), m1, m2, m1t, m2t, mla_q_norm[i], mla_kv_norm[i],
                              w["uq1_t"][i], w["uq2_t"][i], w["uk"][i], w["uv_t"][i])
        o_a = _mla_attn(qt, k, vt, gate_a.reshape(b, s, -1))
        o_b = _ssd(xbc.reshape(b, s, -1), zs.reshape(b, s, -1), small.reshape(b, s, -1), conv_w[i], conv_b[i],
                   dt_bias[i], a_log[i], d_skip[i], ssm_norm[i])
        o_c = _dsa(rel_bias, qcw, qiw, k_c.reshape(b, s, -1), k_idx.reshape(b, s, -1), vw, pos_col, pos_row,
                   gate_c.reshape(b, s, -1), DSA_Q_BLOCK)

        last = i == depth - 1
        g_next = final_norm if last else norm_g[i + 1]
        x2d, h = _merge(o_a.reshape(t, -1), o_b.reshape(t, -1), o_c.reshape(t, -1), gates, x2d, p[i].reshape(t, -1),
                        wbr_a[i], wbr_b[i], wbr_c[i], wo[i], wpg[i], wple[i], g_next, F32 if last else BF16)
    return h.reshape(b, s, d)
```

```python
import functools
import math

import jax
import jax.numpy as jnp
from jax import lax
from jax.experimental import pallas as pl
from jax.experimental.pallas import tpu as pltpu

F32 = jnp.float32
BF16 = jnp.bfloat16
I32 = jnp.int32
I16 = jnp.int16
HALF16 = 1 << 15

D_MODEL = 1024
PLE_DIM = 256
NORM_EPS = 1e-6

MLA_HEADS = 8
MLA_NOPE = 64
MLA_ROPE = 32
MLA_V = 64
MLA_Q_LORA = 384
MLA_KV_LORA = 256
MLA_WIDTH = MLA_HEADS * MLA_V
ROPE_THETA = 10000.0
MLA_QK_PAD = 128
LOG2E = math.log2(math.e)
MLA_Q_SCALE = (MLA_NOPE + MLA_ROPE) ** -0.5 * LOG2E

SSM_HEADS = 16
SSM_HEAD_DIM = 64
SSM_INNER = SSM_HEADS * SSM_HEAD_DIM
SSM_GROUPS = 2
SSM_STATE = 128
SSM_CONV = 4
SSM_CHUNK = 128
SSM_CONV_DIM = SSM_INNER + 2 * SSM_GROUPS * SSM_STATE
SSM_HEADS_PER_GROUP = SSM_HEADS // SSM_GROUPS

DSA_HEADS = 8
DSA_HEAD_DIM = 64
DSA_WIDTH = DSA_HEADS * DSA_HEAD_DIM
IDX_HEADS = 8
IDX_DIM = 64
TOPK_MAX = 256
DSA_KEY_TILE = 128
DSA_Q_BLOCK = 256

REL_BUCKETS = 32
REL_MAX_DIST = 128
N_BRANCHES = 3

LANES = 128
SUBLANES = 8
VMEM_LIMIT_CAP = 56 * 1024 * 1024
INT_MIN = -(2 ** 31)
NEG_BIG = -1e30

SPLIT_SIZES = (
    MLA_Q_LORA, MLA_KV_LORA, MLA_ROPE, MLA_WIDTH, SSM_INNER, SSM_CONV_DIM, SSM_HEADS, DSA_WIDTH,
    DSA_HEAD_DIM, DSA_HEAD_DIM, IDX_HEADS * IDX_DIM, IDX_DIM, IDX_HEADS, DSA_WIDTH,
    N_BRANCHES * D_MODEL,
)
SPLIT_NAMES = ("c_q", "c_kv", "k_rope", "gate_a", "z", "xbc", "dt", "q_c", "k_c", "v_c", "q_idx",
               "k_idx", "w_idx", "gate_c", "merge")


def _split_bounds():
    out, off = {}, 0
    for name, size in zip(SPLIT_NAMES, SPLIT_SIZES, strict=True):
        out[name] = (off, off + size)
        off += size
    return out


SPLIT = _split_bounds()


def _t5_large_thresholds():
    exact = REL_BUCKETS // 2
    thr = []
    for j in range(1, REL_BUCKETS - exact):
        thr.append(int(math.ceil(exact * (REL_MAX_DIST / exact) ** (j / (REL_BUCKETS - exact)) - 1e-9)))
    return tuple(thr)


T5_EXACT = REL_BUCKETS // 2
T5_LARGE_THR = _t5_large_thresholds()
T5_FAR = T5_LARGE_THR[-1]


def _compiler_params(semantics, vmem_bytes):
    limit = int(min(VMEM_LIMIT_CAP, max(32 * 1024 * 1024, vmem_bytes)))
    return pltpu.CompilerParams(dimension_semantics=semantics, vmem_limit_bytes=limit)


def _nbytes(shape, dtype):
    return math.prod(shape) * jnp.dtype(dtype).itemsize


def _sigmoid(x):
    return 1.0 / (1.0 + jnp.exp(-x))


def _rms(x, g):
    return x * lax.rsqrt(jnp.mean(x * x, axis=-1, keepdims=True) + NORM_EPS) * g


def _norm_kernel(x_ref, g_ref, o_ref):
    o_ref[...] = _rms(x_ref[...], g_ref[...]).astype(o_ref.dtype)


def _norm(x2d, g, out_dtype, tm=512):
    t, d = x2d.shape
    return pl.pallas_call(
        _norm_kernel,
        grid=(t // tm,),
        in_specs=[pl.BlockSpec((tm, d), lambda i: (i, 0)), pl.BlockSpec((1, d), lambda i: (0, 0))],
        out_specs=pl.BlockSpec((tm, d), lambda i: (i, 0)),
        out_shape=jax.ShapeDtypeStruct((t, d), out_dtype),
        compiler_params=_compiler_params(("parallel",), 4 * _nbytes((tm, d), F32)),
        name="rms_norm",
    )(x2d, g.reshape(1, d))


def _proj_kernel(h_ref, w_ref, o_ref, *, act):
    y = jnp.dot(h_ref[...], w_ref[...], preferred_element_type=F32)
    if act == "silu":
        y = y * _sigmoid(y)
    elif act == "sigmoid":
        y = _sigmoid(y)
    o_ref[...] = y.astype(o_ref.dtype)


def _proj(h2d, w, out_dtype, act=None, tm=512, tn=None, name="proj"):
    t, k = h2d.shape
    n = w.shape[1]
    if tn is None:
        tn = n
        for cand in (1024, 768, 512, 896):
            if n > cand and n % cand == 0:
                tn = cand
                break
    vmem = 2 * (_nbytes((tm, k), BF16) + _nbytes((k, tn), BF16) + _nbytes((tm, tn), out_dtype)) + _nbytes((tm, tn), F32) * 2
    return pl.pallas_call(
        functools.partial(_proj_kernel, act=act),
        grid=(n // tn, t // tm),
        in_specs=[pl.BlockSpec((tm, k), lambda j, i: (i, 0)), pl.BlockSpec((k, tn), lambda j, i: (0, j))],
        out_specs=pl.BlockSpec((tm, tn), lambda j, i: (i, j)),
        out_shape=jax.ShapeDtypeStruct((t, n), out_dtype),
        compiler_params=_compiler_params(("parallel", "parallel"), vmem),
        name=name,
    )(h2d, w)


def _proj_t_kernel(h_ref, wt_ref, o_ref):
    y = lax.dot_general(wt_ref[...], h_ref[0], (((1,), (1,)), ((), ())), preferred_element_type=F32)
    o_ref[0] = y.astype(o_ref.dtype)


def _proj_t(h3d, wt, out_dtype, tm=512, name="proj_t"):
    b, s, k = h3d.shape
    n = wt.shape[0]
    vmem = 2 * (_nbytes((tm, k), BF16) + _nbytes((n, k), BF16) + _nbytes((n, tm), out_dtype)) + _nbytes((n, tm), F32) * 2
    return pl.pallas_call(
        _proj_t_kernel,
        grid=(b, s // tm),
        in_specs=[pl.BlockSpec((1, tm, k), lambda bi, i: (bi, i, 0)), pl.BlockSpec((n, k), lambda bi, i: (0, 0))],
        out_specs=pl.BlockSpec((1, n, tm), lambda bi, i: (bi, 0, i)),
        out_shape=jax.ShapeDtypeStruct((b, n, s), out_dtype),
        compiler_params=_compiler_params(("parallel", "parallel"), vmem),
        name=name,
    )(h3d, wt)


def _proj_wide_kernel(h_ref, wt_ref, o_ref, *, heads, dim):
    y = lax.dot_general(wt_ref[...], h_ref[0], (((1,), (1,)), ((), ())), preferred_element_type=F32)
    tq = y.shape[1]
    for h in range(heads):
        o_ref[0, 0, :, h * tq:(h + 1) * tq] = y[h * dim:(h + 1) * dim, :].astype(o_ref.dtype)


def _proj_wide(h3d, wt, heads, dim, tq, name):
    b, s, k = h3d.shape
    vmem = 2 * (_nbytes((tq, k), BF16) + _nbytes((heads * dim, k), BF16) + _nbytes((dim, heads * tq), BF16)) + 2 * _nbytes((heads * dim, tq), F32)
    return pl.pallas_call(
        functools.partial(_proj_wide_kernel, heads=heads, dim=dim),
        grid=(b, s // tq),
        in_specs=[pl.BlockSpec((1, tq, k), lambda bi, i: (bi, i, 0)), pl.BlockSpec((heads * dim, k), lambda bi, i: (0, 0))],
        out_specs=pl.BlockSpec((1, 1, dim, heads * tq), lambda bi, i: (bi, i, 0, 0)),
        out_shape=jax.ShapeDtypeStruct((b, s // tq, dim, heads * tq), BF16),
        compiler_params=_compiler_params(("parallel", "parallel"), vmem),
        name=name,
    )(h3d, wt)


def _mla_prep_kernel(a_ref, m1_ref, m2_ref, m1t_ref, m2t_ref, qn_ref, kvn_ref, wq1t_ref, wq2t_ref, wk_ref, wvt_ref,
                     qt_ref, k_ref, vt_ref):
    a = a_ref[0]
    c_q = a[:, :MLA_Q_LORA]
    c_kv = a[:, MLA_Q_LORA:MLA_Q_LORA + MLA_KV_LORA]
    kr1 = a[:, MLA_Q_LORA + MLA_KV_LORA:MLA_Q_LORA + MLA_KV_LORA + LANES]
    kr2 = a[:, MLA_Q_LORA + MLA_KV_LORA + LANES:]
    cqn = _rms(c_q, qn_ref[...]).astype(BF16)
    ckvn = _rms(c_kv, kvn_ref[...]).astype(BF16)
    nt = (((1,), (1,)), ((), ()))
    qa_t = lax.dot_general(wq1t_ref[...], cqn, nt, preferred_element_type=F32)
    qb_t = lax.dot_general(wq2t_ref[...], cqn, nt, preferred_element_type=F32)
    kn = jnp.dot(ckvn, wk_ref[...], preferred_element_type=F32)
    kr = kr1 * m1_ref[0] + kr2 * m2_ref[0]
    m1t = m1t_ref[0] * MLA_Q_SCALE
    m2t = m2t_ref[0] * MLA_Q_SCALE
    for h in range(MLA_HEADS):
        sl = slice(h * MLA_QK_PAD, (h + 1) * MLA_QK_PAD)
        qt_ref[0, sl, :] = (qa_t[sl, :] * m1t + qb_t[sl, :] * m2t).astype(qt_ref.dtype)
        k_ref[0, :, sl] = (kn[:, sl] + kr).astype(k_ref.dtype)
    vt_ref[0] = lax.dot_general(wvt_ref[...], ckvn, nt, preferred_element_type=F32).astype(vt_ref.dtype)


def _mla_prep(a, m1, m2, m1t, m2t, qn, kvn, wq1t, wq2t, wk, wvt, tm=256):
    b, s, wa = a.shape
    hq = MLA_HEADS * MLA_QK_PAD
    full = lambda shape: pl.BlockSpec(shape, lambda bi, i: (0, 0))
    row = lambda width: pl.BlockSpec((1, tm, width), lambda bi, i: (bi, i, 0))
    col = lambda height: pl.BlockSpec((1, height, tm), lambda bi, i: (bi, 0, i))
    vmem = (2 * (_nbytes((tm, wa), F32) + 4 * _nbytes((tm, LANES), F32) + 2 * _nbytes((MLA_Q_LORA, hq), BF16)
                 + _nbytes((MLA_KV_LORA, hq), BF16) + _nbytes((MLA_KV_LORA, MLA_WIDTH), BF16)
                 + 2 * _nbytes((tm, hq), BF16) + _nbytes((tm, MLA_WIDTH), BF16)) + 4 * _nbytes((tm, hq), F32))
    return pl.pallas_call(
        _mla_prep_kernel,
        grid=(b, s // tm),
        in_specs=[row(wa), row(LANES), row(LANES), col(LANES), col(LANES), full((1, MLA_Q_LORA)), full((1, MLA_KV_LORA)),
                  full((hq, MLA_Q_LORA)), full((hq, MLA_Q_LORA)), full((MLA_KV_LORA, hq)), full((MLA_WIDTH, MLA_KV_LORA))],
        out_specs=[col(hq), row(hq), col(MLA_WIDTH)],
        out_shape=[jax.ShapeDtypeStruct((b, hq, s), BF16), jax.ShapeDtypeStruct((b, s, hq), BF16),
                   jax.ShapeDtypeStruct((b, MLA_WIDTH, s), BF16)],
        compiler_params=_compiler_params(("parallel", "parallel"), vmem),
        name="mla_prep",
    )(a, m1, m2, m1t, m2t, qn.reshape(1, -1), kvn.reshape(1, -1), wq1t, wq2t, wk, wvt)


def _mla_attn_kernel(qt_ref, k_ref, vt_ref, g_ref, o_ref, m_ref, l_ref, acc_ref, s0_ref, s1_ref, *, tq, tk):
    assert tq == 2 * tk
    qi = pl.program_id(1)
    krow = lax.broadcasted_iota(I32, (tk, tq), 0)
    qcol = lax.broadcasted_iota(I32, (tk, tq), 1)
    tiles_per_q = tq // tk
    m_ref[...] = jnp.full(m_ref.shape, -jnp.inf, F32)
    l_ref[...] = jnp.zeros(l_ref.shape, F32)
    acc_ref[...] = jnp.zeros(acc_ref.shape, F32)

    staged = (s0_ref, s1_ref)
    n_tiles = (qi + 1) * tiles_per_q

    def head_lanes(h):
        return slice(h * tq, (h + 1) * tq)

    def stage(j, dst_ref):
        ks = pl.multiple_of(j * tk, tk)
        for h in range(MLA_HEADS):
            k = k_ref[0, pl.ds(ks, tk), h * MLA_QK_PAD:(h + 1) * MLA_QK_PAD]
            dst_ref[:, head_lanes(h)] = jnp.dot(k, qt_ref[0, h * MLA_QK_PAD:(h + 1) * MLA_QK_PAD, :],
                                                preferred_element_type=F32)

    def tile(j, cur_ref, next_ref, diagonal):
        ks = pl.multiple_of(j * tk, tk)
        stage(jnp.minimum(j + 1, n_tiles - 1), next_ref)
        m_all = m_ref[...]
        l_all = l_ref[...]
        if diagonal:
            causal = ks + krow <= qi * tq + qcol
        for h in range(MLA_HEADS):
            s = cur_ref[:, head_lanes(h)]
            if diagonal:
                s = jnp.where(causal, s, -jnp.inf)
            m_old = m_all[h:h + 1, :]
            m_new = jnp.maximum(m_old, jnp.max(s, axis=0, keepdims=True))
            alpha = jnp.exp2(m_old - m_new)
            p = jnp.exp2(s - m_new)
            l_ref[h:h + 1, :] = alpha * l_all[h:h + 1, :] + jnp.sum(p, axis=0, keepdims=True)
            m_ref[h:h + 1, :] = m_new
            hs = slice(h * MLA_V, (h + 1) * MLA_V)
            pv = jnp.dot(vt_ref[0, hs, pl.ds(ks, tk)], p.astype(BF16), preferred_element_type=F32)
            acc_ref[hs, :] = alpha * acc_ref[hs, :] + pv

    stage(0, staged[0])

    def full_pair(pi, carry):
        for half in range(2):
            tile(2 * pi + half, staged[half], staged[1 - half], diagonal=False)
        return carry

    lax.fori_loop(0, qi, full_pair, 0)
    for half in range(2):
        tile(2 * qi + half, staged[half], staged[1 - half], diagonal=True)
    for h in range(MLA_HEADS):
        hs = slice(h * MLA_V, (h + 1) * MLA_V)
        acc_ref[hs, :] = acc_ref[hs, :] / l_ref[h:h + 1, :]
    o_ref[0] = (acc_ref[...].T * g_ref[0]).astype(o_ref.dtype)


def _mla_attn(qt, k, vt, gate, tq=256, tk=128):
    b, s, hq = k.shape
    vmem = 2 * (_nbytes((hq, tq), BF16) + _nbytes((s, hq), BF16) + _nbytes((MLA_WIDTH, s), BF16)
                + _nbytes((tq, MLA_WIDTH), F32) + _nbytes((tq, MLA_WIDTH), BF16)) + _nbytes((MLA_WIDTH, tq), F32) + 16 * _nbytes((tk, tq), F32)
    return pl.pallas_call(
        functools.partial(_mla_attn_kernel, tq=tq, tk=tk),
        grid=(b, s // tq),
        in_specs=[pl.BlockSpec((1, hq, tq), lambda bi, i: (bi, 0, i)),
                  pl.BlockSpec((1, s, hq), lambda bi, i: (bi, 0, 0)),
                  pl.BlockSpec((1, MLA_WIDTH, s), lambda bi, i: (bi, 0, 0)),
                  pl.BlockSpec((1, tq, MLA_WIDTH), lambda bi, i: (bi, i, 0))],
        out_specs=pl.BlockSpec((1, tq, MLA_WIDTH), lambda bi, i: (bi, i, 0)),
        out_shape=jax.ShapeDtypeStruct((b, s, MLA_WIDTH), BF16),
        scratch_shapes=[pltpu.VMEM((MLA_HEADS, tq), F32), pltpu.VMEM((MLA_HEADS, tq), F32),
                        pltpu.VMEM((MLA_WIDTH, tq), F32),
                        pltpu.VMEM((tk, MLA_HEADS * tq), F32), pltpu.VMEM((tk, MLA_HEADS * tq), F32)],
        compiler_params=_compiler_params(("parallel", "arbitrary"), vmem),
        name="mla_attn",
    )(qt, k, vt, gate)


def _ssd_kernel(xbc_ref, zs_ref, sm_ref, cw_ref, cb_ref, dtb_ref, alog_ref, dsk_ref, nrm_ref, o_ref,
                xwin_ref, state_ref, y_ref):
    c = pl.program_id(1)
    q = SSM_CHUNK
    tail = SUBLANES

    @pl.when(c == 0)
    def _():
        xwin_ref[0:tail, :] = jnp.zeros((tail, SSM_CONV_DIM), F32)
        state_ref[...] = jnp.zeros(state_ref.shape, F32)

    x = xbc_ref[0]
    xwin_ref[tail:tail + q, :] = x
    acc = cb_ref[...] + cw_ref[SSM_CONV - 1:SSM_CONV, :] * x
    for j in range(1, SSM_CONV):
        acc = acc + cw_ref[SSM_CONV - 1 - j:SSM_CONV - j, :] * xwin_ref[tail - j:tail - j + q, :]
    xwin_ref[0:tail, :] = x[q - tail:, :]
    xc = acc * _sigmoid(acc)

    xs = xc[:, :SSM_INNER]
    bm = xc[:, SSM_INNER:SSM_INNER + SSM_GROUPS * SSM_STATE]
    cm = xc[:, SSM_INNER + SSM_GROUPS * SSM_STATE:]

    pre = sm_ref[0] + dtb_ref[...]
    dt = jnp.maximum(pre, 0.0) + jnp.log1p(jnp.exp(-jnp.abs(pre)))
    a = -jnp.exp(alog_ref[...])
    row = lax.broadcasted_iota(I32, (q, q), 0)
    col = lax.broadcasted_iota(I32, (q, q), 1)
    lower = row >= col
    cum = jnp.dot(jnp.where(lower, 1.0, 0.0).astype(F32), dt * a, preferred_element_type=F32,
                  precision=lax.Precision.HIGHEST)
    cum_t = cum.T
    cum_last = cum[q - 1:q, :]
    e_cum = jnp.exp(cum)
    decay_to_end = jnp.exp(cum_last - cum)
    chunk_decay = jnp.exp(cum_last)
    dsk = dsk_ref[...]

    for g in range(SSM_GROUPS):
        bg = bm[:, g * SSM_STATE:(g + 1) * SSM_STATE].astype(BF16)
        cg = cm[:, g * SSM_STATE:(g + 1) * SSM_STATE].astype(BF16)
        cb = lax.dot_general(cg, bg, (((1,), (1,)), ((), ())), preferred_element_type=F32)
        for hh in range(SSM_HEADS_PER_GROUP):
            h = g * SSM_HEADS_PER_GROUP + hh
            xh = xs[:, h * SSM_HEAD_DIM:(h + 1) * SSM_HEAD_DIM]
            xdt = xh * dt[:, h:h + 1]
            diff = cum[:, h:h + 1] - cum_t[h:h + 1, :]
            decay_in = jnp.exp(jnp.where(lower, diff, -jnp.inf))
            y_diag = jnp.dot((cb * decay_in).astype(BF16), xdt.astype(BF16), preferred_element_type=F32)
            st = state_ref[h]
            y_off = lax.dot_general(cg, st.astype(BF16), (((1,), (1,)), ((), ())),
                                    preferred_element_type=F32) * e_cum[:, h:h + 1]
            xw_t = (xdt * decay_to_end[:, h:h + 1]).T.astype(BF16)
            chunk_state = jnp.dot(xw_t, bg, preferred_element_type=F32)
            state_ref[h] = st * chunk_decay[:, h:h + 1] + chunk_state
            y_ref[:, h * SSM_HEAD_DIM:(h + 1) * SSM_HEAD_DIM] = y_diag + y_off + xh * dsk[:, h:h + 1]

    yg = y_ref[...] * zs_ref[0]
    o_ref[0] = _rms(yg, nrm_ref[...]).astype(o_ref.dtype)


def _ssd(xbc, zs, small, conv_w, conv_b, dt_bias, a_log, d_skip, ssm_norm):
    b, s, _ = xbc.shape
    q = SSM_CHUNK
    pad = lambda v: jnp.pad(v.astype(F32), (0, LANES - v.shape[0])).reshape(1, LANES)
    full = lambda shape: pl.BlockSpec(shape, lambda bi, c: (0,) * len(shape))
    blk = lambda width: pl.BlockSpec((1, q, width), lambda bi, c: (bi, c, 0))
    vmem = (2 * (_nbytes((q, SSM_CONV_DIM), F32) + _nbytes((q, SSM_INNER), F32) + _nbytes((q, LANES), F32)
                 + _nbytes((q, SSM_INNER), BF16)) + _nbytes((q + SUBLANES, SSM_CONV_DIM), F32)
            + _nbytes((SSM_HEADS, SSM_HEAD_DIM, SSM_STATE), F32) + 8 * _nbytes((q, SSM_CONV_DIM), F32))
    return pl.pallas_call(
        _ssd_kernel,
        grid=(b, s // q),
        in_specs=[blk(SSM_CONV_DIM), blk(SSM_INNER), blk(LANES), full((SSM_CONV, SSM_CONV_DIM)), full((1, SSM_CONV_DIM)),
                  full((1, LANES)), full((1, LANES)), full((1, LANES)), full((1, SSM_INNER))],
        out_specs=blk(SSM_INNER),
        out_shape=jax.ShapeDtypeStruct((b, s, SSM_INNER), BF16),
        scratch_shapes=[pltpu.VMEM((q + SUBLANES, SSM_CONV_DIM), F32),
                        pltpu.VMEM((SSM_HEADS, SSM_HEAD_DIM, SSM_STATE), F32),
                        pltpu.VMEM((q, SSM_INNER), F32)],
        compiler_params=_compiler_params(("parallel", "arbitrary"), vmem),
        name="ssd",
    )(xbc, zs, small, conv_w.astype(F32), conv_b.reshape(1, -1).astype(F32), pad(dt_bias), pad(a_log), pad(d_skip),
      ssm_norm.reshape(1, -1).astype(F32))


def _t5_bucket_of(n):
    large = jnp.full(n.shape, T5_EXACT, I32)
    for thr in T5_LARGE_THR:
        large = large + jnp.where(n >= thr, 1, 0)
    return jnp.where(n < T5_EXACT, n, large)


def _dsa_kernel(tbl_ref, qmin_ref, kmax_ref, tblt_ref, qcw_ref, qiw_ref, kc_ref, ki_ref, vw_ref, wq_ref, posk_ref, posq_ref, gate_ref, o_ref,
                key_ref, hi_ref, lo_ref, s0_ref, s1_ref, p_ref, alpha_ref, acc_ref, ot_ref, m_ref, l_ref, tie_ref, *, n_sel, tq):
    tk = DSA_KEY_TILE
    qi = pl.program_id(1)
    n_tiles = (qi + 1) * (tq // tk)
    krow = lax.broadcasted_iota(I32, (tk, tq), 0)
    qcol = lax.broadcasted_iota(I32, (tk, tq), 1)
    w = wq_ref[0, DSA_HEAD_DIM:DSA_HEAD_DIM + IDX_HEADS, :]
    posq = posq_ref[0]

    def tile_start(kt):
        return pl.multiple_of(kt * tk, tk)

    def causal_of(ks):
        return (ks + krow) <= (qi * tq + qcol)

    def head_lanes(h):
        return slice(h * tq, (h + 1) * tq)

    staged = (s0_ref, s1_ref)

    def stage(k_ref, q_ref, kt, dst_ref):
        dst_ref[...] = jnp.dot(k_ref[0, pl.ds(tile_start(kt), tk), :], q_ref[0, 0], preferred_element_type=F32)

    def tile_pairs(tile_fn):
        def pair(pi, carry):
            for half in range(2):
                tile_fn(2 * pi + half, staged[half], staged[1 - half])
            return carry
        lax.fori_loop(0, n_tiles // 2, pair, 0)

    stage(ki_ref, qiw_ref, 0, staged[0])

    def score_tile(kt, cur_ref, next_ref):
        ks = tile_start(kt)
        stage(ki_ref, qiw_ref, jnp.minimum(kt + 1, n_tiles - 1), next_ref)
        sc = jnp.zeros((tk, tq), F32)
        for h in range(IDX_HEADS):
            sc = sc + w[h:h + 1, :] * jnp.maximum(cur_ref[:, head_lanes(h)], 0.0)
        sc = jnp.where(sc == 0.0, 0.0, sc)
        bits = lax.bitcast_convert_type(sc, I32)
        key = bits ^ ((bits >> 31) & 0x7FFFFFFF)
        key = jnp.where(causal_of(ks), key, INT_MIN)
        key_ref[pl.ds(ks, tk), :] = key
        hi_ref[pl.ds(ks, tk), :] = (key >> 16).astype(I16)
        lo_ref[pl.ds(ks, tk), :] = ((key & 0xFFFF) - HALF16).astype(I16)

    tile_pairs(score_tile)

    def count16(half_ref, pred):
        def body(kt, cnt):
            return cnt + jnp.where(pred(half_ref[pl.ds(tile_start(kt), tk), :]), jnp.int16(1), jnp.int16(0))
        cnt = lax.fori_loop(0, n_tiles, body, jnp.zeros((tk, tq), I16))
        return jnp.sum(cnt.astype(I32), axis=0, keepdims=True)

    def nth_largest16(half_ref, rank):
        def bit_step(i, cand):
            trial = cand | lax.shift_left(jnp.int32(1), 15 - i)
            thr = (trial - HALF16).astype(I16)
            tot = count16(half_ref, lambda v: v >= thr)
            return jnp.where(tot >= rank, trial, cand)
        return lax.fori_loop(0, 16, bit_step, jnp.zeros((1, tq), I32)) - HALF16

    tau_hi = nth_largest16(hi_ref, n_sel)
    tau_hi16 = tau_hi.astype(I16)
    above_hi = count16(hi_ref, lambda v: v > tau_hi16)

    def mask_low(kt, carry):
        rows = pl.ds(tile_start(kt), tk)
        lo_ref[rows, :] = jnp.where(hi_ref[rows, :] == tau_hi16, lo_ref[rows, :], jnp.int16(-HALF16))
        return carry

    lax.fori_loop(0, n_tiles, mask_low, 0)
    tau_lo = nth_largest16(lo_ref, n_sel - above_hi)
    tau_lo16 = tau_lo.astype(I16)
    tau = lax.shift_left(tau_hi, 16) | (tau_lo + HALF16)
    n_gt = above_hi + count16(lo_ref, lambda v: v > tau_lo16)
    n_ge = above_hi + count16(lo_ref, lambda v: v >= tau_lo16)
    need = n_sel - n_gt

    m_ref[...] = jnp.full(m_ref.shape, NEG_BIG, F32)
    l_ref[...] = jnp.zeros(l_ref.shape, F32)
    acc_ref[...] = jnp.zeros(acc_ref.shape, F32)
    tie_ref[...] = jnp.zeros(tie_ref.shape, I32)
    lower_incl = jnp.where(lax.broadcasted_iota(I32, (tk, tk), 0) >= lax.broadcasted_iota(I32, (tk, tk), 1),
                           1.0, 0.0).astype(BF16)

    splits =jnp.where((n_ge > n_sel) & (tau != INT_MIN), 1.0, 0.0)
    any_split = jnp.max(splits) > 0.0

    def attend_tile(kt, cur_ref, next_ref, general_bias, tie_split):
        ks = tile_start(kt)
        key = key_ref[pl.ds(ks, tk), :]
        if tie_split:
            eq = key == tau
            prefix = jnp.dot(lower_incl, jnp.where(eq, 1.0, 0.0).astype(BF16), preferred_element_type=F32)
            tie_before = tie_ref[0:1, :]
            tie_rank = tie_before + prefix.astype(I32)
            sel = ((key > tau) | (eq & (tie_rank <= need))) & causal_of(ks)
            tie_ref[0:1, :] = tie_before + prefix[tk - 1:tk, :].astype(I32)
        else:
            sel = (key >= tau) & causal_of(ks)

        if general_bias:
            posk = posk_ref[0, pl.ds(ks, tk), :]
            bucket = _t5_bucket_of(jnp.maximum(posq - posk, 0))
        stage(kc_ref, qcw_ref, jnp.minimum(kt + 1, n_tiles - 1), next_ref)
        m_all = m_ref[0:1, :]
        l_all = l_ref[0:1, :]
        for h in range(DSA_HEADS):
            hl = head_lanes(h)
            s = cur_ref[:, hl]
            if general_bias:
                table = jnp.broadcast_to(tblt_ref[h:h + 1, :], (tk, LANES))
                s = s + jnp.concatenate(
                    [jnp.take_along_axis(table, bucket[:, c * LANES:(c + 1) * LANES], axis=1) for c in range(tq // LANES)],
                    axis=1)
            else:
                s = s + tbl_ref[REL_BUCKETS - 1, h]
            s = jnp.where(sel, s, NEG_BIG)
            m_old = m_all[:, hl]
            m_new = jnp.maximum(m_old, jnp.max(s, axis=0, keepdims=True))
            alpha = jnp.exp2(m_old - m_new)
            p = jnp.exp2(s - m_new)
            l_ref[0:1, hl] = alpha * l_all[:, hl] + jnp.sum(p, axis=0, keepdims=True)
            m_ref[0:1, hl] = m_new
            alpha_ref[0:1, hl] = alpha
            p_ref[:, hl] = p.astype(BF16)
        vt = vw_ref[0, 0:DSA_HEAD_DIM, pl.ds(ks, tk)].astype(BF16)
        pv = jnp.dot(vt, p_ref[...], preferred_element_type=F32)
        acc_ref[...] = alpha_ref[0:1, :] * acc_ref[...] + pv

    bi = pl.program_id(0)
    stage(kc_ref, qcw_ref, 0, staged[0])

    def attend(kt, cur_ref, next_ref):
        far = (qmin_ref[bi, qi] - kmax_ref[bi, kt]) >= T5_FAR
        fast = jnp.logical_and(far, jnp.logical_not(any_split))
        plain = jnp.logical_and(jnp.logical_not(far), jnp.logical_not(any_split))

        @pl.when(fast)
        def _():
            attend_tile(kt, cur_ref, next_ref, general_bias=False, tie_split=False)

        @pl.when(plain)
        def _():
            attend_tile(kt, cur_ref, next_ref, general_bias=True, tie_split=False)

        @pl.when(any_split)
        def _():
            attend_tile(kt, cur_ref, next_ref, general_bias=True, tie_split=True)

    tile_pairs(attend)

    for h in range(DSA_HEADS):
        hl = head_lanes(h)
        ot_ref[h * DSA_HEAD_DIM:(h + 1) * DSA_HEAD_DIM, :] = acc_ref[:, hl] / l_ref[0:1, hl]
    o_ref[0] = (ot_ref[...].T * gate_ref[0]).astype(o_ref.dtype)


def _dsa(rel_bias, qcw, qiw, kc, ki, vw, pos_col, pos_row, gate, tq):
    b, s, _ = kc.shape
    n_sel = min(TOPK_MAX, s // 4)
    vw_rows = vw.shape[1]
    hw = DSA_HEADS * tq
    tbl = rel_bias.astype(F32) * LOG2E
    tbl_t = jnp.pad(tbl.T, ((0, 0), (0, LANES - REL_BUCKETS)))
    q_min = jnp.min(pos_row.reshape(b, s // tq, tq), axis=-1)
    k_max = jnp.max(pos_row.reshape(b, s // DSA_KEY_TILE, DSA_KEY_TILE), axis=-1)
    tk = DSA_KEY_TILE
    vmem = (2 * (2 * _nbytes((DSA_HEAD_DIM, hw), BF16) + 2 * _nbytes((s, LANES), BF16) + _nbytes((vw_rows, s), F32)
                 + _nbytes((vw_rows, tq), F32) + _nbytes((s, LANES), I32) + _nbytes((tq, DSA_WIDTH), F32)
                 + _nbytes((tq, DSA_WIDTH), BF16)) + _nbytes((s, tq), I32) + 2 * _nbytes((tk, hw), F32)
            + _nbytes((tk, hw), BF16) + _nbytes((DSA_HEAD_DIM, hw), F32) + _nbytes((DSA_WIDTH, tq), F32)
            + 32 * _nbytes((tk, tq), F32))
    return pl.pallas_call(
        functools.partial(_dsa_kernel, n_sel=n_sel, tq=tq),
        grid=(b, s // tq),
        in_specs=[pl.BlockSpec(memory_space=pltpu.SMEM), pl.BlockSpec(memory_space=pltpu.SMEM),
                  pl.BlockSpec(memory_space=pltpu.SMEM),
                  pl.BlockSpec((DSA_HEADS, LANES), lambda bi, i: (0, 0)),
                  pl.BlockSpec((1, 1, DSA_HEAD_DIM, hw), lambda bi, i: (bi, i, 0, 0)),
                  pl.BlockSpec((1, 1, IDX_DIM, hw), lambda bi, i: (bi, i, 0, 0)),
                  pl.BlockSpec((1, s, DSA_HEAD_DIM), lambda bi, i: (bi, 0, 0)),
                  pl.BlockSpec((1, s, IDX_DIM), lambda bi, i: (bi, 0, 0)),
                  pl.BlockSpec((1, vw_rows, s), lambda bi, i: (bi, 0, 0)),
                  pl.BlockSpec((1, vw_rows, tq), lambda bi, i: (bi, 0, i)),
                  pl.BlockSpec((1, s, 1), lambda bi, i: (bi, 0, 0)),
                  pl.BlockSpec((1, 1, tq), lambda bi, i: (bi, 0, i)),
                  pl.BlockSpec((1, tq, DSA_WIDTH), lambda bi, i: (bi, i, 0))],
        out_specs=pl.BlockSpec((1, tq, DSA_WIDTH), lambda bi, i: (bi, i, 0)),
        out_shape=jax.ShapeDtypeStruct((b, s, DSA_WIDTH), BF16),
        scratch_shapes=[pltpu.VMEM((s, tq), I32),
                        pltpu.VMEM((s, tq), I16), pltpu.VMEM((s, tq), I16),
                        pltpu.VMEM((tk, hw), F32), pltpu.VMEM((tk, hw), F32),
                        pltpu.VMEM((tk, hw), BF16),
                        pltpu.VMEM((SUBLANES, hw), F32),
                        pltpu.VMEM((DSA_HEAD_DIM, hw), F32),
                        pltpu.VMEM((DSA_WIDTH, tq), F32),
                        pltpu.VMEM((SUBLANES, hw), F32), pltpu.VMEM((SUBLANES, hw), F32),
                        pltpu.VMEM((SUBLANES, tq), I32)],
        compiler_params=_compiler_params(("parallel", "arbitrary"), vmem),
        name="dsa",
    )(tbl, q_min, k_max, tbl_t, qcw, qiw, kc, ki, vw, vw, pos_col, pos_row, gate)


def _merge_kernel(oa_ref, ob_ref, oc_ref, g_ref, x_ref, p_ref, wa_ref, wb_ref, wc_ref, wo_ref, wpg_ref, wple_ref,
                  gn_ref, xo_ref, ho_ref):
    d = D_MODEL
    ya = jnp.dot(oa_ref[...], wa_ref[...], preferred_element_type=F32)
    yb = jnp.dot(ob_ref[...], wb_ref[...], preferred_element_type=F32)
    yc = jnp.dot(oc_ref[...], wc_ref[...], preferred_element_type=F32)
    merged = g_ref[:, 0:d] * ya + g_ref[:, d:2 * d] * yb + g_ref[:, 2 * d:3 * d] * yc
    x1 = x_ref[...] + jnp.dot(merged.astype(BF16), wo_ref[...], preferred_element_type=F32)
    ple_gate = _sigmoid(jnp.dot(x1.astype(BF16), wpg_ref[...], preferred_element_type=F32))
    x2 = x1 + ple_gate * jnp.dot(p_ref[...].astype(BF16), wple_ref[...], preferred_element_type=F32)
    xo_ref[...] = x2
    ho_ref[...] = _rms(x2, gn_ref[...]).astype(ho_ref.dtype)


def _merge(oa, ob, oc, gates, x, p, wa, wb, wc, wo, wpg, wple, g_next, h_dtype, tm=256):
    t, d = x.shape
    full = lambda a: pl.BlockSpec(a.shape, lambda i: (0, 0))
    row = lambda a: pl.BlockSpec((tm, a.shape[1]), lambda i: (i, 0))
    weights = (wa, wb, wc, wo, wpg, wple)
    acts = (oa, ob, oc, gates, x, p)
    vmem = (2 * (sum(_nbytes(a.shape, a.dtype) for a in weights) + sum(_nbytes((tm, a.shape[1]), a.dtype) for a in acts)
                 + 2 * _nbytes((tm, d), F32)) + 8 * _nbytes((tm, d), F32))
    return pl.pallas_call(
        _merge_kernel,
        grid=(t // tm,),
        in_specs=[row(a) for a in acts] + [full(a) for a in weights] + [pl.BlockSpec((1, d), lambda i: (0, 0))],
        out_specs=[pl.BlockSpec((tm, d), lambda i: (i, 0)), pl.BlockSpec((tm, d), lambda i: (i, 0))],
        out_shape=[jax.ShapeDtypeStruct((t, d), F32), jax.ShapeDtypeStruct((t, d), h_dtype)],
        compiler_params=_compiler_params(("parallel",), vmem),
        name="merge",
    )(*acts, *weights, g_next.reshape(1, d))


def _rotate_half_cols(w):
    half = w.shape[-1] // 2
    return jnp.concatenate([-w[..., half:], w[..., :half]], axis=-1)


def _prep_weights(w_in, w_uq, w_ukv):
    depth = w_in.shape[0]
    seg = lambda name: w_in[:, :, SPLIT[name][0]:SPLIT[name][1]]
    z = lambda *shape: jnp.zeros((depth,) + shape, w_in.dtype)
    d = D_MODEL
    kr = seg("k_rope")
    rope_lo = MLA_NOPE
    rope_pad = MLA_QK_PAD - MLA_NOPE - MLA_ROPE
    w = {}
    w["mla_in"] = jnp.concatenate(
        [seg("c_q"), seg("c_kv"), z(d, rope_lo), kr, z(d, rope_pad), z(d, rope_lo), _rotate_half_cols(kr), z(d, rope_pad)], axis=-1)
    w["gate_a"] = seg("gate_a")
    w["z"] = seg("z")
    w["xbc"] = seg("xbc")
    w["small"] = jnp.concatenate([seg("dt"), z(d, LANES - SSM_HEADS)], axis=-1)
    w["k_c"] = seg("k_c")
    w["k_idx"] = seg("k_idx")
    w["gate_c"] = seg("gate_c")
    w["merge"] = seg("merge")
    w["q_c_t"] = jnp.swapaxes(seg("q_c") * (DSA_HEAD_DIM ** -0.5 * LOG2E), 1, 2)
    w["q_idx_t"] = jnp.swapaxes(seg("q_idx"), 1, 2)
    w["vw_t"] = jnp.swapaxes(jnp.concatenate([seg("v_c"), seg("w_idx")], axis=-1), 1, 2)
    w = {k: v.astype(BF16) for k, v in w.items()}

    uq = w_uq.reshape(depth, MLA_Q_LORA, MLA_HEADS, MLA_NOPE + MLA_ROPE)
    nope, rope = uq[..., :MLA_NOPE], uq[..., MLA_NOPE:]
    zq = lambda width: jnp.zeros((depth, MLA_Q_LORA, MLA_HEADS, width), w_uq.dtype)
    hq = MLA_HEADS * MLA_QK_PAD
    uq1 = jnp.concatenate([nope, rope, zq(rope_pad)], axis=-1).reshape(depth, MLA_Q_LORA, hq)
    uq2 = jnp.concatenate([zq(MLA_NOPE), _rotate_half_cols(rope), zq(rope_pad)], axis=-1).reshape(depth, MLA_Q_LORA, hq)
    w["uq1_t"] = jnp.swapaxes(uq1, 1, 2).astype(BF16)
    w["uq2_t"] = jnp.swapaxes(uq2, 1, 2).astype(BF16)
    ukv = w_ukv.reshape(depth, MLA_KV_LORA, MLA_HEADS, MLA_NOPE + MLA_V)
    zk = jnp.zeros((depth, MLA_KV_LORA, MLA_HEADS, MLA_QK_PAD - MLA_NOPE), w_ukv.dtype)
    w["uk"] = jnp.concatenate([ukv[..., :MLA_NOPE], zk], axis=-1).reshape(depth, MLA_KV_LORA, hq).astype(BF16)
    w["uv_t"] = jnp.swapaxes(ukv[..., MLA_NOPE:].reshape(depth, MLA_KV_LORA, MLA_WIDTH), 1, 2).astype(BF16)
    return w


def _rope_multipliers(positions):
    b, s = positions.shape
    inv_freq = 1.0 / (ROPE_THETA ** (jnp.arange(0, MLA_ROPE, 2, dtype=F32) / MLA_ROPE))
    ang = positions.astype(F32)[..., None] * inv_freq
    cos, sin = jnp.cos(ang), jnp.sin(ang)
    pad = MLA_QK_PAD - MLA_NOPE - MLA_ROPE
    m1 = jnp.concatenate([jnp.ones((b, s, MLA_NOPE), F32), cos, cos, jnp.zeros((b, s, pad), F32)], axis=-1)
    m2 = jnp.concatenate([jnp.zeros((b, s, MLA_NOPE), F32), sin, sin, jnp.zeros((b, s, pad), F32)], axis=-1)
    return m1, m2, jnp.swapaxes(m1, 1, 2), jnp.swapaxes(m2, 1, 2)


def kernel(x, p, positions, norm_g, w_in, mla_q_norm, w_uq, mla_kv_norm, w_ukv, conv_w, conv_b, dt_bias, a_log, d_skip, ssm_norm, w_br_a, w_br_b, w_br_c, w_out, rel_bias, w_ple, w_ple_gate, final_norm):
    b, s, d = x.shape
    depth = w_in.shape[0]
    t = b * s
    w = _prep_weights(w_in, w_uq, w_ukv)
    m1, m2, m1t, m2t = _rope_multipliers(positions)
    pos_col = positions.astype(I32).reshape(b, s, 1)
    pos_row = positions.astype(I32).reshape(b, 1, s)
    wbr_a, wbr_b, wbr_c = w_br_a.astype(BF16), w_br_b.astype(BF16), w_br_c.astype(BF16)
    wo, wpg, wple = w_out.astype(BF16), w_ple_gate.astype(BF16), w_ple.astype(BF16)

    x2d = x.reshape(t, d)
    h = _norm(x2d, norm_g[0], BF16)
    for i in range(depth):
        h3d = h.reshape(b, s, d)
        a_in = _proj(h, w["mla_in"][i], F32, name="proj_mla_in")
        gate_a = _proj(h, w["gate_a"][i], F32, act="silu", name="proj_gate_a")
        zs = _proj(h, w["z"][i], F32, act="silu", name="proj_z")
        xbc = _proj(h, w["xbc"][i], F32, name="proj_xbc")
        small = _proj(h, w["small"][i], F32, name="proj_small")
        k_c = _proj(h, w["k_c"][i], BF16, name="proj_k_c")
        k_idx = _proj(h, w["k_idx"][i], BF16, name="proj_k_idx")
        gate_c = _proj(h, w["gate_c"][i], F32, act="silu", name="proj_gate_c")
        gates = _proj(h, w["merge"][i], F32, act="sigmoid", name="proj_merge")
        qcw = _proj_wide(h3d, w["q_c_t"][i], DSA_HEADS, DSA_HEAD_DIM, DSA_Q_BLOCK, name="proj_q_c_wide")
        qiw = _proj_wide(h3d, w["q_idx_t"][i], IDX_HEADS, IDX_DIM, DSA_Q_BLOCK, name="proj_q_idx_wide")
        vw = _proj_t(h3d, w["vw_t"][i], F32, name="proj_vw_t")

        qt, k, vt = _mla_prep(a_in.reshape(b, s, -1), m1, m2, m1t, m2t, mla_q_norm[i], mla_kv_norm[i],
                              w["uq1_t"][i], w["uq2_t"][i], w["uk"][i], w["uv_t"][i])
        o_a = _mla_attn(qt, k, vt, gate_a.reshape(b, s, -1))
        o_b = _ssd(xbc.reshape(b, s, -1), zs.reshape(b, s, -1), small.reshape(b, s, -1), conv_w[i], conv_b[i],
                   dt_bias[i], a_log[i], d_skip[i], ssm_norm[i])
        o_c = _dsa(rel_bias, qcw, qiw, k_c.reshape(b, s, -1), k_idx.reshape(b, s, -1), vw, pos_col, pos_row,
                   gate_c.reshape(b, s, -1), DSA_Q_BLOCK)

        last = i == depth - 1
        g_next = final_norm if last else norm_g[i + 1]
        x2d, h = _merge(o_a.reshape(t, -1), o_b.reshape(t, -1), o_c.reshape(t, -1), gates, x2d, p[i].reshape(t, -1),
                        wbr_a[i], wbr_b[i], wbr_c[i], wo[i], wpg[i], wple[i], g_next, F32 if last else BF16)
    return h.reshape(b, s, d)
```

```python
import functools
import math

import jax
import jax.numpy as jnp
from jax import lax
from jax.experimental import pallas as pl
from jax.experimental.pallas import tpu as pltpu

F32 = jnp.float32
BF16 = jnp.bfloat16
I32 = jnp.int32
I16 = jnp.int16
HALF16 = 1 << 15

D_MODEL = 1024
PLE_DIM = 256
NORM_EPS = 1e-6

MLA_HEADS = 8
MLA_NOPE = 64
MLA_ROPE = 32
MLA_V = 64
MLA_Q_LORA = 384
MLA_KV_LORA = 256
MLA_WIDTH = MLA_HEADS * MLA_V
ROPE_THETA = 10000.0
MLA_QK_PAD = 128
LOG2E = math.log2(math.e)
MLA_Q_SCALE = (MLA_NOPE + MLA_ROPE) ** -0.5 * LOG2E

SSM_HEADS = 16
SSM_HEAD_DIM = 64
SSM_INNER = SSM_HEADS * SSM_HEAD_DIM
SSM_GROUPS = 2
SSM_STATE = 128
SSM_CONV = 4
SSM_CHUNK = 128
SSM_CONV_DIM = SSM_INNER + 2 * SSM_GROUPS * SSM_STATE
SSM_HEADS_PER_GROUP = SSM_HEADS // SSM_GROUPS

DSA_HEADS = 8
DSA_HEAD_DIM = 64
DSA_WIDTH = DSA_HEADS * DSA_HEAD_DIM
IDX_HEADS = 8
IDX_DIM = 64
TOPK_MAX = 256
DSA_KEY_TILE = 128
DSA_Q_BLOCK = 256

REL_BUCKETS = 32
REL_MAX_DIST = 128
N_BRANCHES = 3

LANES = 128
SUBLANES = 8
VMEM_LIMIT_CAP = 56 * 1024 * 1024
INT_MIN = -(2 ** 31)
NEG_BIG = -1e30

SPLIT_SIZES = (
    MLA_Q_LORA, MLA_KV_LORA, MLA_ROPE, MLA_WIDTH, SSM_INNER, SSM_CONV_DIM, SSM_HEADS, DSA_WIDTH,
    DSA_HEAD_DIM, DSA_HEAD_DIM, IDX_HEADS * IDX_DIM, IDX_DIM, IDX_HEADS, DSA_WIDTH,
    N_BRANCHES * D_MODEL,
)
SPLIT_NAMES = ("c_q", "c_kv", "k_rope", "gate_a", "z", "xbc", "dt", "q_c", "k_c", "v_c", "q_idx",
               "k_idx", "w_idx", "gate_c", "merge")


def _split_bounds():
    out, off = {}, 0
    for name, size in zip(SPLIT_NAMES, SPLIT_SIZES, strict=True):
        out[name] = (off, off + size)
        off += size
    return out


SPLIT = _split_bounds()


def _t5_large_thresholds():
    exact = REL_BUCKETS // 2
    thr = []
    for j in range(1, REL_BUCKETS - exact):
        thr.append(int(math.ceil(exact * (REL_MAX_DIST / exact) ** (j / (REL_BUCKETS - exact)) - 1e-9)))
    return tuple(thr)


T5_EXACT = REL_BUCKETS // 2
T5_LARGE_THR = _t5_large_thresholds()
T5_FAR = T5_LARGE_THR[-1]


def _compiler_params(semantics, vmem_bytes):
    limit = int(min(VMEM_LIMIT_CAP, max(32 * 1024 * 1024, vmem_bytes)))
    return pltpu.CompilerParams(dimension_semantics=semantics, vmem_limit_bytes=limit)


def _nbytes(shape, dtype):
    return math.prod(shape) * jnp.dtype(dtype).itemsize


def _sigmoid(x):
    return 1.0 / (1.0 + jnp.exp(-x))


def _rms(x, g):
    return x * lax.rsqrt(jnp.mean(x * x, axis=-1, keepdims=True) + NORM_EPS) * g


def _norm_kernel(x_ref, g_ref, o_ref):
    o_ref[...] = _rms(x_ref[...], g_ref[...]).astype(o_ref.dtype)


def _norm(x2d, g, out_dtype, tm=512):
    t, d = x2d.shape
    return pl.pallas_call(
        _norm_kernel,
        grid=(t // tm,),
        in_specs=[pl.BlockSpec((tm, d), lambda i: (i, 0)), pl.BlockSpec((1, d), lambda i: (0, 0))],
        out_specs=pl.BlockSpec((tm, d), lambda i: (i, 0)),
        out_shape=jax.ShapeDtypeStruct((t, d), out_dtype),
        compiler_params=_compiler_params(("parallel",), 4 * _nbytes((tm, d), F32)),
        name="rms_norm",
    )(x2d, g.reshape(1, d))


PROJ_SEGMENTS = (
    ("mla_in", MLA_Q_LORA + MLA_KV_LORA + 2 * LANES, None, F32),
    ("small", LANES, None, F32),
    ("gate_a", MLA_WIDTH, "silu", BF16),
    ("gate_c", DSA_WIDTH, "silu", BF16),
    ("z", SSM_INNER, "silu", BF16),
    ("xbc", SSM_CONV_DIM, None, BF16),
    ("merge", N_BRANCHES * D_MODEL, "sigmoid", BF16),
    ("k_c", DSA_HEAD_DIM, None, BF16),
    ("k_idx", IDX_DIM, None, BF16),
)
PROJ_CHUNK = 512
PROJ_T_ROWS = (DSA_WIDTH, IDX_HEADS * IDX_DIM, DSA_HEAD_DIM + IDX_HEADS)


def _proj_all_kernel(h_ref, w_ref, wt_ref, *out_refs):
    h = h_ref[0]
    n_seg = len(PROJ_SEGMENTS)
    off = 0
    for (_, width, act, _), o_ref in zip(PROJ_SEGMENTS, out_refs[:n_seg], strict=True):
        for c in range(0, width, PROJ_CHUNK):
            cw = min(PROJ_CHUNK, width - c)
            y = jnp.dot(h, w_ref[:, off + c:off + c + cw], preferred_element_type=F32)
            if act == "silu":
                y = y * _sigmoid(y)
            elif act == "sigmoid":
                y = _sigmoid(y)
            o_ref[0, :, c:c + cw] = y.astype(o_ref.dtype)
        off += width
    qcw_ref, qiw_ref, vw_ref = out_refs[n_seg:]
    nt = (((1,), (1,)), ((), ()))
    tq = h.shape[0]
    row = 0
    for o_ref, heads, dim in ((qcw_ref, DSA_HEADS, DSA_HEAD_DIM), (qiw_ref, IDX_HEADS, IDX_DIM)):
        y = lax.dot_general(wt_ref[row:row + heads * dim, :], h, nt, preferred_element_type=F32)
        for hd in range(heads):
            o_ref[0, 0, :, hd * tq:(hd + 1) * tq] = y[hd * dim:(hd + 1) * dim, :].astype(o_ref.dtype)
        row += heads * dim
    vw_ref[0] = lax.dot_general(wt_ref[row:row + PROJ_T_ROWS[2], :], h, nt, preferred_element_type=F32)


def _proj_all(h3d, w, wt, tq):
    b, s, k = h3d.shape
    n_all = sum(seg[1] for seg in PROJ_SEGMENTS)
    vw_rows = PROJ_T_ROWS[2]
    out_shapes = [jax.ShapeDtypeStruct((b, s, width), dt) for _, width, _, dt in PROJ_SEGMENTS]
    out_specs = [pl.BlockSpec((1, tq, width), lambda bi, i: (bi, i, 0)) for _, width, _, _ in PROJ_SEGMENTS]
    for heads, dim in ((DSA_HEADS, DSA_HEAD_DIM), (IDX_HEADS, IDX_DIM)):
        out_shapes.append(jax.ShapeDtypeStruct((b, s // tq, dim, heads * tq), BF16))
        out_specs.append(pl.BlockSpec((1, 1, dim, heads * tq), lambda bi, i: (bi, i, 0, 0)))
    out_shapes.append(jax.ShapeDtypeStruct((b, vw_rows, s), F32))
    out_specs.append(pl.BlockSpec((1, vw_rows, tq), lambda bi, i: (bi, 0, i)))
    resident = pl.Buffered(1)
    vmem = (_nbytes((k, n_all), BF16) + _nbytes((sum(PROJ_T_ROWS), k), BF16) + 2 * _nbytes((tq, k), BF16)
            + 2 * sum(_nbytes((tq, width), dt) for _, width, _, dt in PROJ_SEGMENTS)
            + 2 * (2 * _nbytes((DSA_HEAD_DIM, DSA_HEADS * tq), BF16) + _nbytes((vw_rows, tq), F32))
            + 8 * _nbytes((tq, PROJ_CHUNK), F32))
    outs = pl.pallas_call(
        _proj_all_kernel,
        grid=(b, s // tq),
        in_specs=[pl.BlockSpec((1, tq, k), lambda bi, i: (bi, i, 0)),
                  pl.BlockSpec((k, n_all), lambda bi, i: (0, 0), pipeline_mode=resident),
                  pl.BlockSpec((sum(PROJ_T_ROWS), k), lambda bi, i: (0, 0), pipeline_mode=resident)],
        out_specs=out_specs,
        out_shape=out_shapes,
        compiler_params=_compiler_params(("parallel", "parallel"), vmem),
        name="proj_all",
    )(h3d, w, wt)
    names = [seg[0] for seg in PROJ_SEGMENTS] + ["q_c_wide", "q_idx_wide", "vw_t"]
    return dict(zip(names, outs, strict=True))


def _mla_prep_kernel(a_ref, m1_ref, m2_ref, m1t_ref, m2t_ref, qn_ref, kvn_ref, wq1t_ref, wq2t_ref, wk_ref, wvt_ref,
                     qt_ref, k_ref, vt_ref):
    a = a_ref[0]
    c_q = a[:, :MLA_Q_LORA]
    c_kv = a[:, MLA_Q_LORA:MLA_Q_LORA + MLA_KV_LORA]
    kr1 = a[:, MLA_Q_LORA + MLA_KV_LORA:MLA_Q_LORA + MLA_KV_LORA + LANES]
    kr2 = a[:, MLA_Q_LORA + MLA_KV_LORA + LANES:]
    cqn = _rms(c_q, qn_ref[...]).astype(BF16)
    ckvn = _rms(c_kv, kvn_ref[...]).astype(BF16)
    nt = (((1,), (1,)), ((), ()))
    qa_t = lax.dot_general(wq1t_ref[...], cqn, nt, preferred_element_type=F32)
    qb_t = lax.dot_general(wq2t_ref[...], cqn, nt, preferred_element_type=F32)
    kn = jnp.dot(ckvn, wk_ref[...], preferred_element_type=F32)
    kr = kr1 * m1_ref[0] + kr2 * m2_ref[0]
    m1t = m1t_ref[0] * MLA_Q_SCALE
    m2t = m2t_ref[0] * MLA_Q_SCALE
    for h in range(MLA_HEADS):
        sl = slice(h * MLA_QK_PAD, (h + 1) * MLA_QK_PAD)
        qt_ref[0, sl, :] = (qa_t[sl, :] * m1t + qb_t[sl, :] * m2t).astype(qt_ref.dtype)
        k_ref[0, :, sl] = (kn[:, sl] + kr).astype(k_ref.dtype)
    vt_ref[0] = lax.dot_general(wvt_ref[...], ckvn, nt, preferred_element_type=F32).astype(vt_ref.dtype)


def _mla_prep(a, m1, m2, m1t, m2t, qn, kvn, wq1t, wq2t, wk, wvt, tm=256):
    b, s, wa = a.shape
    hq = MLA_HEADS * MLA_QK_PAD
    full = lambda shape: pl.BlockSpec(shape, lambda bi, i: (0, 0))
    row = lambda width: pl.BlockSpec((1, tm, width), lambda bi, i: (bi, i, 0))
    col = lambda height: pl.BlockSpec((1, height, tm), lambda bi, i: (bi, 0, i))
    vmem = (2 * (_nbytes((tm, wa), F32) + 4 * _nbytes((tm, LANES), F32) + 2 * _nbytes((MLA_Q_LORA, hq), BF16)
                 + _nbytes((MLA_KV_LORA, hq), BF16) + _nbytes((MLA_KV_LORA, MLA_WIDTH), BF16)
                 + 2 * _nbytes((tm, hq), BF16) + _nbytes((tm, MLA_WIDTH), BF16)) + 4 * _nbytes((tm, hq), F32))
    return pl.pallas_call(
        _mla_prep_kernel,
        grid=(b, s // tm),
        in_specs=[row(wa), row(LANES), row(LANES), col(LANES), col(LANES), full((1, MLA_Q_LORA)), full((1, MLA_KV_LORA)),
                  full((hq, MLA_Q_LORA)), full((hq, MLA_Q_LORA)), full((MLA_KV_LORA, hq)), full((MLA_WIDTH, MLA_KV_LORA))],
        out_specs=[col(hq), row(hq), col(MLA_WIDTH)],
        out_shape=[jax.ShapeDtypeStruct((b, hq, s), BF16), jax.ShapeDtypeStruct((b, s, hq), BF16),
                   jax.ShapeDtypeStruct((b, MLA_WIDTH, s), BF16)],
        compiler_params=_compiler_params(("parallel", "parallel"), vmem),
        name="mla_prep",
    )(a, m1, m2, m1t, m2t, qn.reshape(1, -1), kvn.reshape(1, -1), wq1t, wq2t, wk, wvt)


def _mla_attn_kernel(qt_ref, k_ref, vt_ref, g_ref, o_ref, m_ref, l_ref, acc_ref, s0_ref, s1_ref, *, tq, tk):
    assert tq == 2 * tk
    qi = pl.program_id(1)
    krow = lax.broadcasted_iota(I32, (tk, tq), 0)
    qcol = lax.broadcasted_iota(I32, (tk, tq), 1)
    tiles_per_q = tq // tk
    m_ref[...] = jnp.full(m_ref.shape, -jnp.inf, F32)
    l_ref[...] = jnp.zeros(l_ref.shape, F32)
    acc_ref[...] = jnp.zeros(acc_ref.shape, F32)

    staged = (s0_ref, s1_ref)
    n_tiles = (qi + 1) * tiles_per_q

    def head_lanes(h):
        return slice(h * tq, (h + 1) * tq)

    def stage(j, dst_ref):
        ks = pl.multiple_of(j * tk, tk)
        for h in range(MLA_HEADS):
            k = k_ref[0, pl.ds(ks, tk), h * MLA_QK_PAD:(h + 1) * MLA_QK_PAD]
            dst_ref[:, head_lanes(h)] = jnp.dot(k, qt_ref[0, h * MLA_QK_PAD:(h + 1) * MLA_QK_PAD, :],
                                                preferred_element_type=F32)

    def tile(j, cur_ref, next_ref, diagonal):
        ks = pl.multiple_of(j * tk, tk)
        stage(jnp.minimum(j + 1, n_tiles - 1), next_ref)
        m_all = m_ref[...]
        l_all = l_ref[...]
        if diagonal:
            causal = ks + krow <= qi * tq + qcol
        for h in range(MLA_HEADS):
            s = cur_ref[:, head_lanes(h)]
            if diagonal:
                s = jnp.where(causal, s, -jnp.inf)
            m_old = m_all[h:h + 1, :]
            m_new = jnp.maximum(m_old, jnp.max(s, axis=0, keepdims=True))
            alpha = jnp.exp2(m_old - m_new)
            p = jnp.exp2(s - m_new)
            l_ref[h:h + 1, :] = alpha * l_all[h:h + 1, :] + jnp.sum(p, axis=0, keepdims=True)
            m_ref[h:h + 1, :] = m_new
            hs = slice(h * MLA_V, (h + 1) * MLA_V)
            pv = jnp.dot(vt_ref[0, hs, pl.ds(ks, tk)], p.astype(BF16), preferred_element_type=F32)
            acc_ref[hs, :] = alpha * acc_ref[hs, :] + pv

    stage(0, staged[0])

    def full_pair(pi, carry):
        for half in range(2):
            tile(2 * pi + half, staged[half], staged[1 - half], diagonal=False)
        return carry

    lax.fori_loop(0, qi, full_pair, 0)
    for half in range(2):
        tile(2 * qi + half, staged[half], staged[1 - half], diagonal=True)
    for h in range(MLA_HEADS):
        hs = slice(h * MLA_V, (h + 1) * MLA_V)
        acc_ref[hs, :] = acc_ref[hs, :] / l_ref[h:h + 1, :]
    o_ref[0] = (acc_ref[...].T * g_ref[0]).astype(o_ref.dtype)


def _mla_attn(qt, k, vt, gate, tq=256, tk=128):
    b, s, hq = k.shape
    vmem = 2 * (_nbytes((hq, tq), BF16) + _nbytes((s, hq), BF16) + _nbytes((MLA_WIDTH, s), BF16)
                + _nbytes((tq, MLA_WIDTH), F32) + _nbytes((tq, MLA_WIDTH), BF16)) + _nbytes((MLA_WIDTH, tq), F32) + 16 * _nbytes((tk, tq), F32)
    return pl.pallas_call(
        functools.partial(_mla_attn_kernel, tq=tq, tk=tk),
        grid=(b, s // tq),
        in_specs=[pl.BlockSpec((1, hq, tq), lambda bi, i: (bi, 0, i)),
                  pl.BlockSpec((1, s, hq), lambda bi, i: (bi, 0, 0)),
                  pl.BlockSpec((1, MLA_WIDTH, s), lambda bi, i: (bi, 0, 0)),
                  pl.BlockSpec((1, tq, MLA_WIDTH), lambda bi, i: (bi, i, 0))],
        out_specs=pl.BlockSpec((1, tq, MLA_WIDTH), lambda bi, i: (bi, i, 0)),
        out_shape=jax.ShapeDtypeStruct((b, s, MLA_WIDTH), BF16),
        scratch_shapes=[pltpu.VMEM((MLA_HEADS, tq), F32), pltpu.VMEM((MLA_HEADS, tq), F32),
                        pltpu.VMEM((MLA_WIDTH, tq), F32),
                        pltpu.VMEM((tk, MLA_HEADS * tq), F32), pltpu.VMEM((tk, MLA_HEADS * tq), F32)],
        compiler_params=_compiler_params(("parallel", "arbitrary"), vmem),
        name="mla_attn",
    )(qt, k, vt, gate)


def _ssd_kernel(xbc_ref, zs_ref, sm_ref, cw_ref, cb_ref, dtb_ref, alog_ref, dsk_ref, nrm_ref, o_ref,
                xwin_ref, state_ref, y_ref):
    c = pl.program_id(1)
    q = SSM_CHUNK
    tail = SUBLANES

    @pl.when(c == 0)
    def _():
        xwin_ref[0:tail, :] = jnp.zeros((tail, SSM_CONV_DIM), F32)
        state_ref[...] = jnp.zeros(state_ref.shape, F32)

    x = xbc_ref[0].astype(F32)
    xwin_ref[tail:tail + q, :] = x
    acc = cb_ref[...] + cw_ref[SSM_CONV - 1:SSM_CONV, :] * x
    for j in range(1, SSM_CONV):
        acc = acc + cw_ref[SSM_CONV - 1 - j:SSM_CONV - j, :] * xwin_ref[tail - j:tail - j + q, :]
    xwin_ref[0:tail, :] = x[q - tail:, :]
    xc = acc * _sigmoid(acc)

    xs = xc[:, :SSM_INNER]
    bm = xc[:, SSM_INNER:SSM_INNER + SSM_GROUPS * SSM_STATE]
    cm = xc[:, SSM_INNER + SSM_GROUPS * SSM_STATE:]

    pre = sm_ref[0] + dtb_ref[...]
    dt = jnp.maximum(pre, 0.0) + jnp.log1p(jnp.exp(-jnp.abs(pre)))
    a = -jnp.exp(alog_ref[...])
    row = lax.broadcasted_iota(I32, (q, q), 0)
    col = lax.broadcasted_iota(I32, (q, q), 1)
    lower = row >= col
    cum = jnp.dot(jnp.where(lower, 1.0, 0.0).astype(F32), dt * a, preferred_element_type=F32,
                  precision=lax.Precision.HIGHEST)
    cum_t = cum.T
    cum_last = cum[q - 1:q, :]
    e_cum = jnp.exp(cum)
    decay_to_end = jnp.exp(cum_last - cum)
    chunk_decay = jnp.exp(cum_last)
    dsk = dsk_ref[...]

    for g in range(SSM_GROUPS):
        bg = bm[:, g * SSM_STATE:(g + 1) * SSM_STATE].astype(BF16)
        cg = cm[:, g * SSM_STATE:(g + 1) * SSM_STATE].astype(BF16)
        cb = lax.dot_general(cg, bg, (((1,), (1,)), ((), ())), preferred_element_type=F32)
        for hh in range(SSM_HEADS_PER_GROUP):
            h = g * SSM_HEADS_PER_GROUP + hh
            xh = xs[:, h * SSM_HEAD_DIM:(h + 1) * SSM_HEAD_DIM]
            xdt = xh * dt[:, h:h + 1]
            diff = cum[:, h:h + 1] - cum_t[h:h + 1, :]
            decay_in = jnp.exp(jnp.where(lower, diff, -jnp.inf))
            y_diag = jnp.dot((cb * decay_in).astype(BF16), xdt.astype(BF16), preferred_element_type=F32)
            st = state_ref[h]
            y_off = lax.dot_general(cg, st.astype(BF16), (((1,), (1,)), ((), ())),
                                    preferred_element_type=F32) * e_cum[:, h:h + 1]
            xw_t = (xdt * decay_to_end[:, h:h + 1]).T.astype(BF16)
            chunk_state = jnp.dot(xw_t, bg, preferred_element_type=F32)
            state_ref[h] = st * chunk_decay[:, h:h + 1] + chunk_state
            y_ref[:, h * SSM_HEAD_DIM:(h + 1) * SSM_HEAD_DIM] = y_diag + y_off + xh * dsk[:, h:h + 1]

    yg = y_ref[...] * zs_ref[0]
    o_ref[0] = _rms(yg, nrm_ref[...]).astype(o_ref.dtype)


def _ssd(xbc, zs, small, conv_w, conv_b, dt_bias, a_log, d_skip, ssm_norm):
    b, s, _ = xbc.shape
    q = SSM_CHUNK
    pad = lambda v: jnp.pad(v.astype(F32), (0, LANES - v.shape[0])).reshape(1, LANES)
    full = lambda shape: pl.BlockSpec(shape, lambda bi, c: (0,) * len(shape))
    blk = lambda width: pl.BlockSpec((1, q, width), lambda bi, c: (bi, c, 0))
    vmem = (2 * (_nbytes((q, SSM_CONV_DIM), F32) + _nbytes((q, SSM_INNER), F32) + _nbytes((q, LANES), F32)
                 + _nbytes((q, SSM_INNER), BF16)) + _nbytes((q + SUBLANES, SSM_CONV_DIM), F32)
            + _nbytes((SSM_HEADS, SSM_HEAD_DIM, SSM_STATE), F32) + 8 * _nbytes((q, SSM_CONV_DIM), F32))
    return pl.pallas_call(
        _ssd_kernel,
        grid=(b, s // q),
        in_specs=[blk(SSM_CONV_DIM), blk(SSM_INNER), blk(LANES), full((SSM_CONV, SSM_CONV_DIM)), full((1, SSM_CONV_DIM)),
                  full((1, LANES)), full((1, LANES)), full((1, LANES)), full((1, SSM_INNER))],
        out_specs=blk(SSM_INNER),
        out_shape=jax.ShapeDtypeStruct((b, s, SSM_INNER), BF16),
        scratch_shapes=[pltpu.VMEM((q + SUBLANES, SSM_CONV_DIM), F32),
                        pltpu.VMEM((SSM_HEADS, SSM_HEAD_DIM, SSM_STATE), F32),
                        pltpu.VMEM((q, SSM_INNER), F32)],
        compiler_params=_compiler_params(("parallel", "arbitrary"), vmem),
        name="ssd",
    )(xbc, zs, small, conv_w.astype(F32), conv_b.reshape(1, -1).astype(F32), pad(dt_bias), pad(a_log), pad(d_skip),
      ssm_norm.reshape(1, -1).astype(F32))


def _t5_bucket_of(n):
    large = jnp.full(n.shape, T5_EXACT, I32)
    for thr in T5_LARGE_THR:
        large = large + jnp.where(n >= thr, 1, 0)
    return jnp.where(n < T5_EXACT, n, large)


def _dsa_kernel(tbl_ref, qmin_ref, kmax_ref, tblt_ref, qcw_ref, qiw_ref, kc_ref, ki_ref, vw_ref, wq_ref, posk_ref, posq_ref, gate_ref, o_ref,
                key_ref, hi_ref, lo_ref, s0_ref, s1_ref, p_ref, alpha_ref, acc_ref, ot_ref, m_ref, l_ref, tie_ref, *, n_sel, tq):
    tk = DSA_KEY_TILE
    qi = pl.program_id(1)
    n_tiles = (qi + 1) * (tq // tk)
    krow = lax.broadcasted_iota(I32, (tk, tq), 0)
    qcol = lax.broadcasted_iota(I32, (tk, tq), 1)
    w = wq_ref[0, DSA_HEAD_DIM:DSA_HEAD_DIM + IDX_HEADS, :]
    posq = posq_ref[0]

    def tile_start(kt):
        return pl.multiple_of(kt * tk, tk)

    def causal_of(ks):
        return (ks + krow) <= (qi * tq + qcol)

    def head_lanes(h):
        return slice(h * tq, (h + 1) * tq)

    staged = (s0_ref, s1_ref)

    def stage(k_ref, q_ref, kt, dst_ref):
        dst_ref[...] = jnp.dot(k_ref[0, pl.ds(tile_start(kt), tk), :], q_ref[0, 0], preferred_element_type=F32)

    def tile_pairs(tile_fn):
        def pair(pi, carry):
            for half in range(2):
                tile_fn(2 * pi + half, staged[half], staged[1 - half])
            return carry
        lax.fori_loop(0, n_tiles // 2, pair, 0)

    stage(ki_ref, qiw_ref, 0, staged[0])

    def score_tile(kt, cur_ref, next_ref):
        ks = tile_start(kt)
        stage(ki_ref, qiw_ref, jnp.minimum(kt + 1, n_tiles - 1), next_ref)
        sc = jnp.zeros((tk, tq), F32)
        for h in range(IDX_HEADS):
            sc = sc + w[h:h + 1, :] * jnp.maximum(cur_ref[:, head_lanes(h)], 0.0)
        sc = jnp.where(sc == 0.0, 0.0, sc)
        bits = lax.bitcast_convert_type(sc, I32)
        key = bits ^ ((bits >> 31) & 0x7FFFFFFF)
        key = jnp.where(causal_of(ks), key, INT_MIN)
        key_ref[pl.ds(ks, tk), :] = key
        hi_ref[pl.ds(ks, tk), :] = (key >> 16).astype(I16)
        lo_ref[pl.ds(ks, tk), :] = ((key & 0xFFFF) - HALF16).astype(I16)

    tile_pairs(score_tile)

    def count16(half_ref, pred):
        def body(kt, cnt):
            return cnt + jnp.where(pred(half_ref[pl.ds(tile_start(kt), tk), :]), jnp.int16(1), jnp.int16(0))
        cnt = lax.fori_loop(0, n_tiles, body, jnp.zeros((tk, tq), I16))
        return jnp.sum(cnt.astype(I32), axis=0, keepdims=True)

    def nth_largest16(half_ref, rank):
        def bit_step(i, cand):
            trial = cand | lax.shift_left(jnp.int32(1), 15 - i)
            thr = (trial - HALF16).astype(I16)
            tot = count16(half_ref, lambda v: v >= thr)
            return jnp.where(tot >= rank, trial, cand)
        return lax.fori_loop(0, 16, bit_step, jnp.zeros((1, tq), I32)) - HALF16

    tau_hi = nth_largest16(hi_ref, n_sel)
    tau_hi16 = tau_hi.astype(I16)
    above_hi = count16(hi_ref, lambda v: v > tau_hi16)

    def mask_low(kt, carry):
        rows = pl.ds(tile_start(kt), tk)
        lo_ref[rows, :] = jnp.where(hi_ref[rows, :] == tau_hi16, lo_ref[rows, :], jnp.int16(-HALF16))
        return carry

    lax.fori_loop(0, n_tiles, mask_low, 0)
    tau_lo = nth_largest16(lo_ref, n_sel - above_hi)
    tau_lo16 = tau_lo.astype(I16)
    tau = lax.shift_left(tau_hi, 16) | (tau_lo + HALF16)
    n_gt = above_hi + count16(lo_ref, lambda v: v > tau_lo16)
    n_ge = above_hi + count16(lo_ref, lambda v: v >= tau_lo16)
    need = n_sel - n_gt

    m_ref[...] = jnp.full(m_ref.shape, NEG_BIG, F32)
    l_ref[...] = jnp.zeros(l_ref.shape, F32)
    acc_ref[...] = jnp.zeros(acc_ref.shape, F32)
    tie_ref[...] = jnp.zeros(tie_ref.shape, I32)
    lower_incl = jnp.where(lax.broadcasted_iota(I32, (tk, tk), 0) >= lax.broadcasted_iota(I32, (tk, tk), 1),
                           1.0, 0.0).astype(BF16)

    splits =jnp.where((n_ge > n_sel) & (tau != INT_MIN), 1.0, 0.0)
    any_split = jnp.max(splits) > 0.0

    def attend_tile(kt, cur_ref, next_ref, general_bias, tie_split):
        ks = tile_start(kt)
        key = key_ref[pl.ds(ks, tk), :]
        if tie_split:
            eq = key == tau
            prefix = jnp.dot(lower_incl, jnp.where(eq, 1.0, 0.0).astype(BF16), preferred_element_type=F32)
            tie_before = tie_ref[0:1, :]
            tie_rank = tie_before + prefix.astype(I32)
            sel = ((key > tau) | (eq & (tie_rank <= need))) & causal_of(ks)
            tie_ref[0:1, :] = tie_before + prefix[tk - 1:tk, :].astype(I32)
        else:
            sel = (key >= tau) & causal_of(ks)

        if general_bias:
            posk = posk_ref[0, pl.ds(ks, tk), :]
            bucket = _t5_bucket_of(jnp.maximum(posq - posk, 0))
        stage(kc_ref, qcw_ref, jnp.minimum(kt + 1, n_tiles - 1), next_ref)
        m_all = m_ref[0:1, :]
        l_all = l_ref[0:1, :]
        for h in range(DSA_HEADS):
            hl = head_lanes(h)
            s = cur_ref[:, hl]
            if general_bias:
                table = jnp.broadcast_to(tblt_ref[h:h + 1, :], (tk, LANES))
                s = s + jnp.concatenate(
                    [jnp.take_along_axis(table, bucket[:, c * LANES:(c + 1) * LANES], axis=1) for c in range(tq // LANES)],
                    axis=1)
            else:
                s = s + tbl_ref[REL_BUCKETS - 1, h]
            s = jnp.where(sel, s, NEG_BIG)
            m_old = m_all[:, hl]
            m_new = jnp.maximum(m_old, jnp.max(s, axis=0, keepdims=True))
            alpha = jnp.exp2(m_old - m_new)
            p = jnp.exp2(s - m_new)
            l_ref[0:1, hl] = alpha * l_all[:, hl] + jnp.sum(p, axis=0, keepdims=True)
            m_ref[0:1, hl] = m_new
            alpha_ref[0:1, hl] = alpha
            p_ref[:, hl] = p.astype(BF16)
        vt = vw_ref[0, 0:DSA_HEAD_DIM, pl.ds(ks, tk)].astype(BF16)
        pv = jnp.dot(vt, p_ref[...], preferred_element_type=F32)
        acc_ref[...] = alpha_ref[0:1, :] * acc_ref[...] + pv

    bi = pl.program_id(0)
    stage(kc_ref, qcw_ref, 0, staged[0])

    def attend(kt, cur_ref, next_ref):
        far = (qmin_ref[bi, qi] - kmax_ref[bi, kt]) >= T5_FAR
        fast = jnp.logical_and(far, jnp.logical_not(any_split))
        plain = jnp.logical_and(jnp.logical_not(far), jnp.logical_not(any_split))

        @pl.when(fast)
        def _():
            attend_tile(kt, cur_ref, next_ref, general_bias=False, tie_split=False)

        @pl.when(plain)
        def _():
            attend_tile(kt, cur_ref, next_ref, general_bias=True, tie_split=False)

        @pl.when(any_split)
        def _():
            attend_tile(kt, cur_ref, next_ref, general_bias=True, tie_split=True)

    tile_pairs(attend)

    for h in range(DSA_HEADS):
        hl = head_lanes(h)
        ot_ref[h * DSA_HEAD_DIM:(h + 1) * DSA_HEAD_DIM, :] = acc_ref[:, hl] / l_ref[0:1, hl]
    o_ref[0] = (ot_ref[...].T * gate_ref[0]).astype(o_ref.dtype)


def _dsa(rel_bias, qcw, qiw, kc, ki, vw, pos_col, pos_row, gate, tq):
    b, s, _ = kc.shape
    n_sel = min(TOPK_MAX, s // 4)
    vw_rows = vw.shape[1]
    hw = DSA_HEADS * tq
    tbl = rel_bias.astype(F32) * LOG2E
    tbl_t = jnp.pad(tbl.T, ((0, 0), (0, LANES - REL_BUCKETS)))
    q_min = jnp.min(pos_row.reshape(b, s // tq, tq), axis=-1)
    k_max = jnp.max(pos_row.reshape(b, s // DSA_KEY_TILE, DSA_KEY_TILE), axis=-1)
    tk = DSA_KEY_TILE
    vmem = (2 * (2 * _nbytes((DSA_HEAD_DIM, hw), BF16) + 2 * _nbytes((s, LANES), BF16) + _nbytes((vw_rows, s), F32)
                 + _nbytes((vw_rows, tq), F32) + _nbytes((s, LANES), I32) + _nbytes((tq, DSA_WIDTH), F32)
                 + _nbytes((tq, DSA_WIDTH), BF16)) + _nbytes((s, tq), I32) + 2 * _nbytes((tk, hw), F32)
            + _nbytes((tk, hw), BF16) + _nbytes((DSA_HEAD_DIM, hw), F32) + _nbytes((DSA_WIDTH, tq), F32)
            + 32 * _nbytes((tk, tq), F32))
    return pl.pallas_call(
        functools.partial(_dsa_kernel, n_sel=n_sel, tq=tq),
        grid=(b, s // tq),
        in_specs=[pl.BlockSpec(memory_space=pltpu.SMEM), pl.BlockSpec(memory_space=pltpu.SMEM),
                  pl.BlockSpec(memory_space=pltpu.SMEM),
                  pl.BlockSpec((DSA_HEADS, LANES), lambda bi, i: (0, 0)),
                  pl.BlockSpec((1, 1, DSA_HEAD_DIM, hw), lambda bi, i: (bi, i, 0, 0)),
                  pl.BlockSpec((1, 1, IDX_DIM, hw), lambda bi, i: (bi, i, 0, 0)),
                  pl.BlockSpec((1, s, DSA_HEAD_DIM), lambda bi, i: (bi, 0, 0)),
                  pl.BlockSpec((1, s, IDX_DIM), lambda bi, i: (bi, 0, 0)),
                  pl.BlockSpec((1, vw_rows, s), lambda bi, i: (bi, 0, 0)),
                  pl.BlockSpec((1, vw_rows, tq), lambda bi, i: (bi, 0, i)),
                  pl.BlockSpec((1, s, 1), lambda bi, i: (bi, 0, 0)),
                  pl.BlockSpec((1, 1, tq), lambda bi, i: (bi, 0, i)),
                  pl.BlockSpec((1, tq, DSA_WIDTH), lambda bi, i: (bi, i, 0))],
        out_specs=pl.BlockSpec((1, tq, DSA_WIDTH), lambda bi, i: (bi, i, 0)),
        out_shape=jax.ShapeDtypeStruct((b, s, DSA_WIDTH), BF16),
        scratch_shapes=[pltpu.VMEM((s, tq), I32),
                        pltpu.VMEM((s, tq), I16), pltpu.VMEM((s, tq), I16),
                        pltpu.VMEM((tk, hw), F32), pltpu.VMEM((tk, hw), F32),
                        pltpu.VMEM((tk, hw), BF16),
                        pltpu.VMEM((SUBLANES, hw), F32),
                        pltpu.VMEM((DSA_HEAD_DIM, hw), F32),
                        pltpu.VMEM((DSA_WIDTH, tq), F32),
                        pltpu.VMEM((SUBLANES, hw), F32), pltpu.VMEM((SUBLANES, hw), F32),
                        pltpu.VMEM((SUBLANES, tq), I32)],
        compiler_params=_compiler_params(("parallel", "arbitrary"), vmem),
        name="dsa",
    )(tbl, q_min, k_max, tbl_t, qcw, qiw, kc, ki, vw, vw, pos_col, pos_row, gate)


def _merge_kernel(oa_ref, ob_ref, oc_ref, g_ref, x_ref, p_ref, wa_ref, wb_ref, wc_ref, wo_ref, wpg_ref, wple_ref,
                  gn_ref, xo_ref, ho_ref):
    d = D_MODEL
    ya = jnp.dot(oa_ref[...], wa_ref[...], preferred_element_type=F32)
    yb = jnp.dot(ob_ref[...], wb_ref[...], preferred_element_type=F32)
    yc = jnp.dot(oc_ref[...], wc_ref[...], preferred_element_type=F32)
    merged = g_ref[:, 0:d] * ya + g_ref[:, d:2 * d] * yb + g_ref[:, 2 * d:3 * d] * yc
    x1 = x_ref[...] + jnp.dot(merged.astype(BF16), wo_ref[...], preferred_element_type=F32)
    ple_gate = _sigmoid(jnp.dot(x1.astype(BF16), wpg_ref[...], preferred_element_type=F32))
    x2 = x1 + ple_gate * jnp.dot(p_ref[...].astype(BF16), wple_ref[...], preferred_element_type=F32)
    xo_ref[...] = x2
    ho_ref[...] = _rms(x2, gn_ref[...]).astype(ho_ref.dtype)


def _merge(oa, ob, oc, gates, x, p, wa, wb, wc, wo, wpg, wple, g_next, h_dtype, tm=256):
    t, d = x.shape
    full = lambda a: pl.BlockSpec(a.shape, lambda i: (0, 0))
    row = lambda a: pl.BlockSpec((tm, a.shape[1]), lambda i: (i, 0))
    weights = (wa, wb, wc, wo, wpg, wple)
    acts = (oa, ob, oc, gates, x, p)
    vmem = (2 * (sum(_nbytes(a.shape, a.dtype) for a in weights) + sum(_nbytes((tm, a.shape[1]), a.dtype) for a in acts)
                 + 2 * _nbytes((tm, d), F32)) + 8 * _nbytes((tm, d), F32))
    return pl.pallas_call(
        _merge_kernel,
        grid=(t // tm,),
        in_specs=[row(a) for a in acts] + [full(a) for a in weights] + [pl.BlockSpec((1, d), lambda i: (0, 0))],
        out_specs=[pl.BlockSpec((tm, d), lambda i: (i, 0)), pl.BlockSpec((tm, d), lambda i: (i, 0))],
        out_shape=[jax.ShapeDtypeStruct((t, d), F32), jax.ShapeDtypeStruct((t, d), h_dtype)],
        compiler_params=_compiler_params(("parallel",), vmem),
        name="merge",
    )(*acts, *weights, g_next.reshape(1, d))


def _rotate_half_cols(w):
    half = w.shape[-1] // 2
    return jnp.concatenate([-w[..., half:], w[..., :half]], axis=-1)


def _prep_weights(w_in, w_uq, w_ukv):
    depth = w_in.shape[0]
    seg = lambda name: w_in[:, :, SPLIT[name][0]:SPLIT[name][1]]
    z = lambda *shape: jnp.zeros((depth,) + shape, w_in.dtype)
    d = D_MODEL
    kr = seg("k_rope")
    rope_lo = MLA_NOPE
    rope_pad = MLA_QK_PAD - MLA_NOPE - MLA_ROPE
    w = {}
    w["mla_in"] = jnp.concatenate(
        [seg("c_q"), seg("c_kv"), z(d, rope_lo), kr, z(d, rope_pad), z(d, rope_lo), _rotate_half_cols(kr), z(d, rope_pad)], axis=-1)
    w["gate_a"] = seg("gate_a")
    w["z"] = seg("z")
    w["xbc"] = seg("xbc")
    w["small"] = jnp.concatenate([seg("dt"), z(d, LANES - SSM_HEADS)], axis=-1)
    w["k_c"] = seg("k_c")
    w["k_idx"] = seg("k_idx")
    w["gate_c"] = seg("gate_c")
    w["merge"] = seg("merge")
    w["q_c_t"] = jnp.swapaxes(seg("q_c") * (DSA_HEAD_DIM ** -0.5 * LOG2E), 1, 2)
    w["q_idx_t"] = jnp.swapaxes(seg("q_idx"), 1, 2)
    w["vw_t"] = jnp.swapaxes(jnp.concatenate([seg("v_c"), seg("w_idx")], axis=-1), 1, 2)
    w = {
        "proj": jnp.concatenate([w[seg[0]] for seg in PROJ_SEGMENTS], axis=-1).astype(BF16),
        "proj_t": jnp.concatenate([w["q_c_t"], w["q_idx_t"], w["vw_t"]], axis=1).astype(BF16),
    }

    uq =w_uq.reshape(depth, MLA_Q_LORA, MLA_HEADS, MLA_NOPE + MLA_ROPE)
    nope, rope = uq[..., :MLA_NOPE], uq[..., MLA_NOPE:]
    zq = lambda width: jnp.zeros((depth, MLA_Q_LORA, MLA_HEADS, width), w_uq.dtype)
    hq = MLA_HEADS * MLA_QK_PAD
    uq1 = jnp.concatenate([nope, rope, zq(rope_pad)], axis=-1).reshape(depth, MLA_Q_LORA, hq)
    uq2 = jnp.concatenate([zq(MLA_NOPE), _rotate_half_cols(rope), zq(rope_pad)], axis=-1).reshape(depth, MLA_Q_LORA, hq)
    w["uq1_t"] = jnp.swapaxes(uq1, 1, 2).astype(BF16)
    w["uq2_t"] = jnp.swapaxes(uq2, 1, 2).astype(BF16)
    ukv = w_ukv.reshape(depth, MLA_KV_LORA, MLA_HEADS, MLA_NOPE + MLA_V)
    zk = jnp.zeros((depth, MLA_KV_LORA, MLA_HEADS, MLA_QK_PAD - MLA_NOPE), w_ukv.dtype)
    w["uk"] = jnp.concatenate([ukv[..., :MLA_NOPE], zk], axis=-1).reshape(depth, MLA_KV_LORA, hq).astype(BF16)
    w["uv_t"] = jnp.swapaxes(ukv[..., MLA_NOPE:].reshape(depth, MLA_KV_LORA, MLA_WIDTH), 1, 2).astype(BF16)
    return w


def _rope_multipliers(positions):
    b, s = positions.shape
    inv_freq = 1.0 / (ROPE_THETA ** (jnp.arange(0, MLA_ROPE, 2, dtype=F32) / MLA_ROPE))
    ang = positions.astype(F32)[..., None] * inv_freq
    cos, sin = jnp.cos(ang), jnp.sin(ang)
    pad = MLA_QK_PAD - MLA_NOPE - MLA_ROPE
    m1 = jnp.concatenate([jnp.ones((b, s, MLA_NOPE), F32), cos, cos, jnp.zeros((b, s, pad), F32)], axis=-1)
    m2 = jnp.concatenate([jnp.zeros((b, s, MLA_NOPE), F32), sin, sin, jnp.zeros((b, s, pad), F32)], axis=-1)
    return m1, m2, jnp.swapaxes(m1, 1, 2), jnp.swapaxes(m2, 1, 2)


def kernel(x, p, positions, norm_g, w_in, mla_q_norm, w_uq, mla_kv_norm, w_ukv, conv_w, conv_b, dt_bias, a_log, d_skip, ssm_norm, w_br_a, w_br_b, w_br_c, w_out, rel_bias, w_ple, w_ple_gate, final_norm):
    b, s, d = x.shape
    depth = w_in.shape[0]
    t = b * s
    w = _prep_weights(w_in, w_uq, w_ukv)
    m1, m2, m1t, m2t = _rope_multipliers(positions)
    pos_col = positions.astype(I32).reshape(b, s, 1)
    pos_row = positions.astype(I32).reshape(b, 1, s)
    wbr_a, wbr_b, wbr_c = w_br_a.astype(BF16), w_br_b.astype(BF16), w_br_c.astype(BF16)
    wo, wpg, wple = w_out.astype(BF16), w_ple_gate.astype(BF16), w_ple.astype(BF16)

    x2d = x.reshape(t, d)
    h = _norm(x2d, norm_g[0], BF16)
    for i in range(depth):
        pr = _proj_all(h.reshape(b, s, d), w["proj"][i], w["proj_t"][i], DSA_Q_BLOCK)

        qt, k, vt = _mla_prep(pr["mla_in"], m1, m2, m1t, m2t, mla_q_norm[i], mla_kv_norm[i],
                              w["uq1_t"][i], w["uq2_t"][i], w["uk"][i], w["uv_t"][i])
        o_a = _mla_attn(qt, k, vt, pr["gate_a"])
        o_b = _ssd(pr["xbc"], pr["z"], pr["small"], conv_w[i], conv_b[i], dt_bias[i], a_log[i], d_skip[i], ssm_norm[i])
        o_c = _dsa(rel_bias, pr["q_c_wide"], pr["q_idx_wide"], pr["k_c"], pr["k_idx"], pr["vw_t"], pos_col, pos_row,
                   pr["gate_c"], DSA_Q_BLOCK)

        last = i == depth - 1
        g_next = final_norm if last else norm_g[i + 1]
        x2d, h = _merge(o_a.reshape(t, -1), o_b.reshape(t, -1), o_c.reshape(t, -1), pr["merge"].reshape(t, -1), x2d, p[i].reshape(t, -1),
                        wbr_a[i], wbr_b[i], wbr_c[i], wo[i], wpg[i], wple[i], g_next, F32 if last else BF16)
    return h.reshape(b, s, d)
```

```python
import functools
import math

import jax
import jax.numpy as jnp
from jax import lax
from jax.experimental import pallas as pl
from jax.experimental.pallas import tpu as pltpu

F32 = jnp.float32
BF16 = jnp.bfloat16
I32 = jnp.int32
I16 = jnp.int16
HALF16 = 1 << 15

D_MODEL = 1024
PLE_DIM = 256
NORM_EPS = 1e-6

MLA_HEADS = 8
MLA_NOPE = 64
MLA_ROPE = 32
MLA_V = 64
MLA_Q_LORA = 384
MLA_KV_LORA = 256
MLA_WIDTH = MLA_HEADS * MLA_V
ROPE_THETA = 10000.0
MLA_QK_PAD = 128
LOG2E = math.log2(math.e)
MLA_Q_SCALE = (MLA_NOPE + MLA_ROPE) ** -0.5 * LOG2E

SSM_HEADS = 16
SSM_HEAD_DIM = 64
SSM_INNER = SSM_HEADS * SSM_HEAD_DIM
SSM_GROUPS = 2
SSM_STATE = 128
SSM_CONV = 4
SSM_CHUNK = 128
SSM_CONV_DIM = SSM_INNER + 2 * SSM_GROUPS * SSM_STATE
SSM_HEADS_PER_GROUP = SSM_HEADS // SSM_GROUPS
SSM_CONV_TAIL = 16

DSA_HEADS = 8
DSA_HEAD_DIM = 64
DSA_WIDTH = DSA_HEADS * DSA_HEAD_DIM
IDX_HEADS = 8
IDX_DIM = 64
TOPK_MAX = 256
DSA_KEY_TILE = 128
DSA_Q_BLOCK = 256

REL_BUCKETS = 32
REL_MAX_DIST = 128
N_BRANCHES = 3

LANES = 128
SUBLANES = 8
VMEM_LIMIT_CAP = 56 * 1024 * 1024
INT_MIN = -(2 ** 31)
NEG_BIG = -1e30

SPLIT_SIZES = (
    MLA_Q_LORA, MLA_KV_LORA, MLA_ROPE, MLA_WIDTH, SSM_INNER, SSM_CONV_DIM, SSM_HEADS, DSA_WIDTH,
    DSA_HEAD_DIM, DSA_HEAD_DIM, IDX_HEADS * IDX_DIM, IDX_DIM, IDX_HEADS, DSA_WIDTH,
    N_BRANCHES * D_MODEL,
)
SPLIT_NAMES = ("c_q", "c_kv", "k_rope", "gate_a", "z", "xbc", "dt", "q_c", "k_c", "v_c", "q_idx",
               "k_idx", "w_idx", "gate_c", "merge")


def _split_bounds():
    out, off = {}, 0
    for name, size in zip(SPLIT_NAMES, SPLIT_SIZES, strict=True):
        out[name] = (off, off + size)
        off += size
    return out


SPLIT = _split_bounds()


def _t5_large_thresholds():
    exact = REL_BUCKETS // 2
    thr = []
    for j in range(1, REL_BUCKETS - exact):
        thr.append(int(math.ceil(exact * (REL_MAX_DIST / exact) ** (j / (REL_BUCKETS - exact)) - 1e-9)))
    return tuple(thr)


T5_EXACT = REL_BUCKETS // 2
T5_LARGE_THR = _t5_large_thresholds()
T5_FAR = T5_LARGE_THR[-1]


def _compiler_params(semantics, vmem_bytes):
    limit = int(min(VMEM_LIMIT_CAP, max(32 * 1024 * 1024, vmem_bytes)))
    return pltpu.CompilerParams(dimension_semantics=semantics, vmem_limit_bytes=limit)


def _nbytes(shape, dtype):
    return math.prod(shape) * jnp.dtype(dtype).itemsize


def _sigmoid(x):
    return 1.0 / (1.0 + jnp.exp(-x))


def _rms(x, g):
    return x * lax.rsqrt(jnp.mean(x * x, axis=-1, keepdims=True) + NORM_EPS) * g


def _norm_kernel(x_ref, g_ref, o_ref):
    o_ref[...] = _rms(x_ref[...], g_ref[...]).astype(o_ref.dtype)


def _norm(x2d, g, out_dtype, tm=512):
    t, d = x2d.shape
    return pl.pallas_call(
        _norm_kernel,
        grid=(t // tm,),
        in_specs=[pl.BlockSpec((tm, d), lambda i: (i, 0)), pl.BlockSpec((1, d), lambda i: (0, 0))],
        out_specs=pl.BlockSpec((tm, d), lambda i: (i, 0)),
        out_shape=jax.ShapeDtypeStruct((t, d), out_dtype),
        compiler_params=_compiler_params(("parallel",), 4 * _nbytes((tm, d), F32)),
        name="rms_norm",
    )(x2d, g.reshape(1, d))


PROJ_SEGMENTS = (
    ("mla_in", MLA_Q_LORA + MLA_KV_LORA + 2 * LANES, None, F32),
    ("small", LANES, None, F32),
    ("gate_a", MLA_WIDTH, "silu", BF16),
    ("gate_c", DSA_WIDTH, "silu", BF16),
    ("z", SSM_INNER, "silu", BF16),
    ("xbc", SSM_CONV_DIM, None, BF16),
    ("merge", N_BRANCHES * D_MODEL, "sigmoid", BF16),
    ("k_c", DSA_HEAD_DIM, None, BF16),
    ("k_idx", IDX_DIM, None, BF16),
)
PROJ_CHUNK = 512
PROJ_T_ROWS = (DSA_WIDTH, IDX_HEADS * IDX_DIM, DSA_HEAD_DIM + IDX_HEADS)


def _proj_all_kernel(h_ref, w_ref, wt_ref, *out_refs):
    h = h_ref[0]
    n_seg = len(PROJ_SEGMENTS)
    off = 0
    for (_, width, act, _), o_ref in zip(PROJ_SEGMENTS, out_refs[:n_seg], strict=True):
        for c in range(0, width, PROJ_CHUNK):
            cw = min(PROJ_CHUNK, width - c)
            y = jnp.dot(h, w_ref[:, off + c:off + c + cw], preferred_element_type=F32)
            if act == "silu":
                y = y * _sigmoid(y)
            elif act == "sigmoid":
                y = _sigmoid(y)
            o_ref[0, :, c:c + cw] = y.astype(o_ref.dtype)
        off += width
    qcw_ref, qiw_ref, vw_ref = out_refs[n_seg:]
    nt = (((1,), (1,)), ((), ()))
    tq = h.shape[0]
    row = 0
    for o_ref, heads, dim in ((qcw_ref, DSA_HEADS, DSA_HEAD_DIM), (qiw_ref, IDX_HEADS, IDX_DIM)):
        y = lax.dot_general(wt_ref[row:row + heads * dim, :], h, nt, preferred_element_type=F32)
        for hd in range(heads):
            o_ref[0, 0, :, hd * tq:(hd + 1) * tq] = y[hd * dim:(hd + 1) * dim, :].astype(o_ref.dtype)
        row += heads * dim
    vw_ref[0] = lax.dot_general(wt_ref[row:row + PROJ_T_ROWS[2], :], h, nt, preferred_element_type=F32)


def _proj_all(h3d, w, wt, tq):
    b, s, k = h3d.shape
    n_all = sum(seg[1] for seg in PROJ_SEGMENTS)
    vw_rows = PROJ_T_ROWS[2]
    out_shapes = [jax.ShapeDtypeStruct((b, s, width), dt) for _, width, _, dt in PROJ_SEGMENTS]
    out_specs = [pl.BlockSpec((1, tq, width), lambda bi, i: (bi, i, 0)) for _, width, _, _ in PROJ_SEGMENTS]
    for heads, dim in ((DSA_HEADS, DSA_HEAD_DIM), (IDX_HEADS, IDX_DIM)):
        out_shapes.append(jax.ShapeDtypeStruct((b, s // tq, dim, heads * tq), BF16))
        out_specs.append(pl.BlockSpec((1, 1, dim, heads * tq), lambda bi, i: (bi, i, 0, 0)))
    out_shapes.append(jax.ShapeDtypeStruct((b, vw_rows, s), F32))
    out_specs.append(pl.BlockSpec((1, vw_rows, tq), lambda bi, i: (bi, 0, i)))
    resident = pl.Buffered(1)
    vmem = (_nbytes((k, n_all), BF16) + _nbytes((sum(PROJ_T_ROWS), k), BF16) + 2 * _nbytes((tq, k), BF16)
            + 2 * sum(_nbytes((tq, width), dt) for _, width, _, dt in PROJ_SEGMENTS)
            + 2 * (2 * _nbytes((DSA_HEAD_DIM, DSA_HEADS * tq), BF16) + _nbytes((vw_rows, tq), F32))
            + 8 * _nbytes((tq, PROJ_CHUNK), F32))
    outs = pl.pallas_call(
        _proj_all_kernel,
        grid=(b, s // tq),
        in_specs=[pl.BlockSpec((1, tq, k), lambda bi, i: (bi, i, 0)),
                  pl.BlockSpec((k, n_all), lambda bi, i: (0, 0), pipeline_mode=resident),
                  pl.BlockSpec((sum(PROJ_T_ROWS), k), lambda bi, i: (0, 0), pipeline_mode=resident)],
        out_specs=out_specs,
        out_shape=out_shapes,
        compiler_params=_compiler_params(("parallel", "parallel"), vmem),
        name="proj_all",
    )(h3d, w, wt)
    names = [seg[0] for seg in PROJ_SEGMENTS] + ["q_c_wide", "q_idx_wide", "vw_t"]
    return dict(zip(names, outs, strict=True))


def _mla_prep_kernel(a_ref, m1_ref, m2_ref, m1t_ref, m2t_ref, qn_ref, kvn_ref, wq1t_ref, wq2t_ref, wk_ref, wvt_ref,
                     qt_ref, k_ref, vt_ref):
    a = a_ref[0]
    c_q = a[:, :MLA_Q_LORA]
    c_kv = a[:, MLA_Q_LORA:MLA_Q_LORA + MLA_KV_LORA]
    kr1 = a[:, MLA_Q_LORA + MLA_KV_LORA:MLA_Q_LORA + MLA_KV_LORA + LANES]
    kr2 = a[:, MLA_Q_LORA + MLA_KV_LORA + LANES:]
    cqn = _rms(c_q, qn_ref[...]).astype(BF16)
    ckvn = _rms(c_kv, kvn_ref[...]).astype(BF16)
    nt = (((1,), (1,)), ((), ()))
    qa_t = lax.dot_general(wq1t_ref[...], cqn, nt, preferred_element_type=F32)
    qb_t = lax.dot_general(wq2t_ref[...], cqn, nt, preferred_element_type=F32)
    kn = jnp.dot(ckvn, wk_ref[...], preferred_element_type=F32)
    kr = kr1 * m1_ref[0] + kr2 * m2_ref[0]
    m1t = m1t_ref[0] * MLA_Q_SCALE
    m2t = m2t_ref[0] * MLA_Q_SCALE
    for h in range(MLA_HEADS):
        sl = slice(h * MLA_QK_PAD, (h + 1) * MLA_QK_PAD)
        qt_ref[0, sl, :] = (qa_t[sl, :] * m1t + qb_t[sl, :] * m2t).astype(qt_ref.dtype)
        k_ref[0, :, sl] = (kn[:, sl] + kr).astype(k_ref.dtype)
    vt_ref[0] = lax.dot_general(wvt_ref[...], ckvn, nt, preferred_element_type=F32).astype(vt_ref.dtype)


def _mla_prep(a, m1, m2, m1t, m2t, qn, kvn, wq1t, wq2t, wk, wvt, tm=256):
    b, s, wa = a.shape
    hq = MLA_HEADS * MLA_QK_PAD
    full = lambda shape: pl.BlockSpec(shape, lambda bi, i: (0, 0))
    row = lambda width: pl.BlockSpec((1, tm, width), lambda bi, i: (bi, i, 0))
    col = lambda height: pl.BlockSpec((1, height, tm), lambda bi, i: (bi, 0, i))
    vmem = (2 * (_nbytes((tm, wa), F32) + 4 * _nbytes((tm, LANES), F32) + 2 * _nbytes((MLA_Q_LORA, hq), BF16)
                 + _nbytes((MLA_KV_LORA, hq), BF16) + _nbytes((MLA_KV_LORA, MLA_WIDTH), BF16)
                 + 2 * _nbytes((tm, hq), BF16) + _nbytes((tm, MLA_WIDTH), BF16)) + 4 * _nbytes((tm, hq), F32))
    return pl.pallas_call(
        _mla_prep_kernel,
        grid=(b, s // tm),
        in_specs=[row(wa), row(LANES), row(LANES), col(LANES), col(LANES), full((1, MLA_Q_LORA)), full((1, MLA_KV_LORA)),
                  full((hq, MLA_Q_LORA)), full((hq, MLA_Q_LORA)), full((MLA_KV_LORA, hq)), full((MLA_WIDTH, MLA_KV_LORA))],
        out_specs=[col(hq), row(hq), col(MLA_WIDTH)],
        out_shape=[jax.ShapeDtypeStruct((b, hq, s), BF16), jax.ShapeDtypeStruct((b, s, hq), BF16),
                   jax.ShapeDtypeStruct((b, MLA_WIDTH, s), BF16)],
        compiler_params=_compiler_params(("parallel", "parallel"), vmem),
        name="mla_prep",
    )(a, m1, m2, m1t, m2t, qn.reshape(1, -1), kvn.reshape(1, -1), wq1t, wq2t, wk, wvt)


def _mla_attn_kernel(qt_ref, k_ref, vt_ref, g_ref, o_ref, m_ref, l_ref, acc_ref, s0_ref, s1_ref, *, tq, tk):
    assert tq == 2 * tk
    qi = pl.program_id(1)
    krow = lax.broadcasted_iota(I32, (tk, tq), 0)
    qcol = lax.broadcasted_iota(I32, (tk, tq), 1)
    tiles_per_q = tq // tk
    m_ref[...] = jnp.full(m_ref.shape, -jnp.inf, F32)
    l_ref[...] = jnp.zeros(l_ref.shape, F32)
    acc_ref[...] = jnp.zeros(acc_ref.shape, F32)

    staged = (s0_ref, s1_ref)
    n_tiles = (qi + 1) * tiles_per_q

    def head_lanes(h):
        return slice(h * tq, (h + 1) * tq)

    def stage(j, dst_ref):
        ks = pl.multiple_of(j * tk, tk)
        for h in range(MLA_HEADS):
            k = k_ref[0, pl.ds(ks, tk), h * MLA_QK_PAD:(h + 1) * MLA_QK_PAD]
            dst_ref[:, head_lanes(h)] = jnp.dot(k, qt_ref[0, h * MLA_QK_PAD:(h + 1) * MLA_QK_PAD, :],
                                                preferred_element_type=F32)

    def tile(j, cur_ref, next_ref, diagonal):
        ks = pl.multiple_of(j * tk, tk)
        stage(jnp.minimum(j + 1, n_tiles - 1), next_ref)
        m_all = m_ref[...]
        l_all = l_ref[...]
        if diagonal:
            causal = ks + krow <= qi * tq + qcol
        for h in range(MLA_HEADS):
            s = cur_ref[:, head_lanes(h)]
            if diagonal:
                s = jnp.where(causal, s, -jnp.inf)
            m_old = m_all[h:h + 1, :]
            m_new = jnp.maximum(m_old, jnp.max(s, axis=0, keepdims=True))
            alpha = jnp.exp2(m_old - m_new)
            p = jnp.exp2(s - m_new)
            l_ref[h:h + 1, :] = alpha * l_all[h:h + 1, :] + jnp.sum(p, axis=0, keepdims=True)
            m_ref[h:h + 1, :] = m_new
            hs = slice(h * MLA_V, (h + 1) * MLA_V)
            pv = jnp.dot(vt_ref[0, hs, pl.ds(ks, tk)], p.astype(BF16), preferred_element_type=F32)
            acc_ref[hs, :] = alpha * acc_ref[hs, :] + pv

    stage(0, staged[0])

    def full_pair(pi, carry):
        for half in range(2):
            tile(2 * pi + half, staged[half], staged[1 - half], diagonal=False)
        return carry

    lax.fori_loop(0, qi, full_pair, 0)
    for half in range(2):
        tile(2 * qi + half, staged[half], staged[1 - half], diagonal=True)
    for h in range(MLA_HEADS):
        hs = slice(h * MLA_V, (h + 1) * MLA_V)
        acc_ref[hs, :] = acc_ref[hs, :] / l_ref[h:h + 1, :]
    o_ref[0] = (acc_ref[...].T * g_ref[0]).astype(o_ref.dtype)


def _mla_attn(qt, k, vt, gate, tq=256, tk=128):
    b, s, hq = k.shape
    vmem = 2 * (_nbytes((hq, tq), BF16) + _nbytes((s, hq), BF16) + _nbytes((MLA_WIDTH, s), BF16)
                + _nbytes((tq, MLA_WIDTH), F32) + _nbytes((tq, MLA_WIDTH), BF16)) + _nbytes((MLA_WIDTH, tq), F32) + 16 * _nbytes((tk, tq), F32)
    return pl.pallas_call(
        functools.partial(_mla_attn_kernel, tq=tq, tk=tk),
        grid=(b, s // tq),
        in_specs=[pl.BlockSpec((1, hq, tq), lambda bi, i: (bi, 0, i)),
                  pl.BlockSpec((1, s, hq), lambda bi, i: (bi, 0, 0)),
                  pl.BlockSpec((1, MLA_WIDTH, s), lambda bi, i: (bi, 0, 0)),
                  pl.BlockSpec((1, tq, MLA_WIDTH), lambda bi, i: (bi, i, 0))],
        out_specs=pl.BlockSpec((1, tq, MLA_WIDTH), lambda bi, i: (bi, i, 0)),
        out_shape=jax.ShapeDtypeStruct((b, s, MLA_WIDTH), BF16),
        scratch_shapes=[pltpu.VMEM((MLA_HEADS, tq), F32), pltpu.VMEM((MLA_HEADS, tq), F32),
                        pltpu.VMEM((MLA_WIDTH, tq), F32),
                        pltpu.VMEM((tk, MLA_HEADS * tq), F32), pltpu.VMEM((tk, MLA_HEADS * tq), F32)],
        compiler_params=_compiler_params(("parallel", "arbitrary"), vmem),
        name="mla_attn",
    )(qt, k, vt, gate)


def _ssd_kernel(xbc_ref, xprev_ref, zs_ref, sm_ref, cw_ref, cb_ref, dtb_ref, alog_ref, dskf_ref, nrm_ref, exp16_ref, exp32_ref,
                o_ref, state_ref, y_ref):
    c = pl.program_id(1)
    q = SSM_CHUNK

    @pl.when(c == 0)
    def _():
        state_ref[...] = jnp.zeros(state_ref.shape, F32)

    x_in = xbc_ref[0]
    tail = xprev_ref.shape[1]
    prev = jnp.where(c > 0, xprev_ref[0].astype(F32), 0.0).astype(x_in.dtype)
    win = jnp.concatenate([prev, x_in], axis=0)
    x = x_in.astype(F32)
    wrow = lax.broadcasted_iota(I32, (q, tail + q), 0)
    wcol = lax.broadcasted_iota(I32, (q, tail + q), 1)
    acc = cb_ref[...] + cw_ref[SSM_CONV - 1:SSM_CONV, :] * x
    for j in range(1, SSM_CONV):
        pick = jnp.where(wcol == wrow + (tail - j), 1.0, 0.0).astype(win.dtype)
        acc = acc + cw_ref[SSM_CONV - 1 - j:SSM_CONV - j, :] * jnp.dot(pick, win, preferred_element_type=F32)
    xc = acc * _sigmoid(acc)

    xs = xc[:, :SSM_INNER]
    bm = xc[:, SSM_INNER:SSM_INNER + SSM_GROUPS * SSM_STATE]
    cm = xc[:, SSM_INNER + SSM_GROUPS * SSM_STATE:]

    pre = sm_ref[0] + dtb_ref[...]
    dt = jnp.maximum(pre, 0.0) + jnp.log1p(jnp.exp(-jnp.abs(pre)))
    a = -jnp.exp(alog_ref[...])
    row = lax.broadcasted_iota(I32, (q, q), 0)
    col = lax.broadcasted_iota(I32, (q, q), 1)
    lower = row >= col
    cum = jnp.dot(jnp.where(lower, 1.0, 0.0).astype(F32), dt * a, preferred_element_type=F32,
                  precision=lax.Precision.HIGHEST)
    cum_t = cum.T
    cum_last = cum[q - 1:q, :]
    per_head = jnp.concatenate([dt, jnp.exp(cum_last - cum), jnp.exp(cum)], axis=0).astype(BF16)
    spread = jnp.dot(per_head, exp16_ref[...], preferred_element_type=F32)
    dt_full, to_end_full, e_cum_full = spread[0:q], spread[q:2 * q], spread[2 * q:3 * q]
    chunk_decay_full = jnp.dot(jnp.broadcast_to(jnp.exp(cum_last), (SUBLANES, LANES)), exp32_ref[...],
                               preferred_element_type=F32, precision=lax.Precision.HIGHEST)[0:1]
    xdt = xs * dt_full
    xw = (xdt * to_end_full).astype(BF16)
    first_head = lax.broadcasted_iota(I32, (q, LANES), 1) < SSM_HEAD_DIM
    group_w = SSM_HEADS_PER_GROUP * SSM_HEAD_DIM
    heads_per_tile = LANES // SSM_HEAD_DIM

    for g in range(SSM_GROUPS):
        bg32 = bm[:, g * SSM_STATE:(g + 1) * SSM_STATE]
        bg = bg32.astype(BF16)
        cg = cm[:, g * SSM_STATE:(g + 1) * SSM_STATE].astype(BF16)
        cb = lax.dot_general(cg, bg, (((1,), (1,)), ((), ())), preferred_element_type=F32)
        gl = slice(g * group_w, (g + 1) * group_w)
        st = state_ref[g]
        y_off = jnp.dot(cg, st.astype(BF16), preferred_element_type=F32)
        state_ref[g] = st * chunk_decay_full[:, gl] + jnp.dot(bg32.T.astype(BF16), xw[:, gl], preferred_element_type=F32)
        for tile in range(group_w // LANES):
            h0 = g * SSM_HEADS_PER_GROUP + tile * heads_per_tile
            tl = slice(h0 * SSM_HEAD_DIM, h0 * SSM_HEAD_DIM + LANES)
            g_both = []
            for h in range(h0, h0 + heads_per_tile):
                diff = cum[:, h:h + 1] - cum_t[h:h + 1, :]
                decay_in = jnp.exp(jnp.where(lower, diff, -jnp.inf))
                g_both.append((cb * decay_in).astype(BF16))
            xdt_tile = xdt[:, tl]
            stacked = jnp.concatenate([jnp.where(first_head, xdt_tile, 0.0).astype(BF16),
                                       jnp.where(first_head, 0.0, xdt_tile).astype(BF16)], axis=0)
            y_diag = jnp.dot(jnp.concatenate(g_both, axis=1), stacked, preferred_element_type=F32)
            y = y_diag + y_off[:, tile * LANES:(tile + 1) * LANES] * e_cum_full[:, tl] + xs[:, tl] * dskf_ref[:, tl]
            y_ref[:, tl] = y * zs_ref[0, :, tl]

    o_ref[0] = _rms(y_ref[...], nrm_ref[...]).astype(o_ref.dtype)


def _ssd(xbc, zs, small, conv_w, conv_b, dt_bias, a_log, d_skip, ssm_norm):
    b, s, _ = xbc.shape
    q = SSM_CHUNK
    pad = lambda v: jnp.pad(v.astype(F32), (0, LANES - v.shape[0])).reshape(1, LANES)
    full = lambda shape: pl.BlockSpec(shape, lambda bi, c: (0,) * len(shape))
    blk = lambda width: pl.BlockSpec((1, q, width), lambda bi, c: (bi, c, 0))
    expand = (jnp.arange(LANES)[:, None] == jnp.arange(SSM_INNER)[None, :] // SSM_HEAD_DIM).astype(F32)
    d_skip_full = jnp.repeat(d_skip.astype(F32), SSM_HEAD_DIM).reshape(1, SSM_INNER)
    group_w = SSM_HEADS_PER_GROUP * SSM_HEAD_DIM
    vmem = (2 * (_nbytes((q, SSM_CONV_DIM), xbc.dtype) + _nbytes((q, SSM_INNER), zs.dtype) + _nbytes((q, LANES), F32)
                 + _nbytes((q, SSM_INNER), BF16) + _nbytes((LANES, SSM_INNER), BF16) + _nbytes((LANES, SSM_INNER), F32))
            + _nbytes((2 * q, SSM_CONV_DIM), xbc.dtype) + _nbytes((SSM_GROUPS, SSM_STATE, group_w), F32)
            + _nbytes((q, SSM_INNER), F32) + 10 * _nbytes((q, SSM_CONV_DIM), F32))
    return pl.pallas_call(
        _ssd_kernel,
        grid=(b, s // q),
        in_specs=[blk(SSM_CONV_DIM),
                  pl.BlockSpec((1, SSM_CONV_TAIL, SSM_CONV_DIM),
                               lambda bi, c: (bi, jnp.maximum(c * (q // SSM_CONV_TAIL) - 1, 0), 0)),
                  blk(SSM_INNER), blk(LANES), full((SSM_CONV, SSM_CONV_DIM)), full((1, SSM_CONV_DIM)),
                  full((1, LANES)), full((1, LANES)), full((1, SSM_INNER)), full((1, SSM_INNER)),
                  full((LANES, SSM_INNER)), full((LANES, SSM_INNER))],
        out_specs=blk(SSM_INNER),
        out_shape=jax.ShapeDtypeStruct((b, s, SSM_INNER), BF16),
        scratch_shapes=[pltpu.VMEM((SSM_GROUPS, SSM_STATE, group_w), F32),
                        pltpu.VMEM((q, SSM_INNER), F32)],
        compiler_params=_compiler_params(("parallel", "arbitrary"), vmem),
        name="ssd",
    )(xbc, xbc, zs, small, conv_w.astype(F32), conv_b.reshape(1, -1).astype(F32), pad(dt_bias), pad(a_log), d_skip_full,
      ssm_norm.reshape(1, -1).astype(F32), expand.astype(BF16), expand)


def _t5_bucket_of(n):
    large = jnp.full(n.shape, T5_EXACT, I32)
    for thr in T5_LARGE_THR:
        large = large + jnp.where(n >= thr, 1, 0)
    return jnp.where(n < T5_EXACT, n, large)


def _dsa_kernel(tbl_ref, qmin_ref, kmax_ref, tblt_ref, qcw_ref, qiw_ref, kc_ref, ki_ref, vw_ref, wq_ref, posk_ref, posq_ref, gate_ref, o_ref,
                key_ref, hi_ref, lo_ref, s0_ref, s1_ref, p_ref, alpha_ref, acc_ref, ot_ref, m_ref, l_ref, tie_ref, *, n_sel, tq):
    tk = DSA_KEY_TILE
    qi = pl.program_id(1)
    n_tiles = (qi + 1) * (tq // tk)
    krow = lax.broadcasted_iota(I32, (tk, tq), 0)
    qcol = lax.broadcasted_iota(I32, (tk, tq), 1)
    w = wq_ref[0, DSA_HEAD_DIM:DSA_HEAD_DIM + IDX_HEADS, :]
    posq = posq_ref[0]

    def tile_start(kt):
        return pl.multiple_of(kt * tk, tk)

    def causal_of(ks):
        return (ks + krow) <= (qi * tq + qcol)

    def head_lanes(h):
        return slice(h * tq, (h + 1) * tq)

    staged = (s0_ref, s1_ref)

    def stage(k_ref, q_ref, kt, dst_ref):
        dst_ref[...] = jnp.dot(k_ref[0, pl.ds(tile_start(kt), tk), :], q_ref[0, 0], preferred_element_type=F32)

    def tile_pairs(tile_fn):
        def pair(pi, carry):
            for half in range(2):
                tile_fn(2 * pi + half, staged[half], staged[1 - half])
            return carry
        lax.fori_loop(0, n_tiles // 2, pair, 0)

    stage(ki_ref, qiw_ref, 0, staged[0])

    def score_tile(kt, cur_ref, next_ref):
        ks = tile_start(kt)
        stage(ki_ref, qiw_ref, jnp.minimum(kt + 1, n_tiles - 1), next_ref)
        sc = jnp.zeros((tk, tq), F32)
        for h in range(IDX_HEADS):
            sc = sc + w[h:h + 1, :] * jnp.maximum(cur_ref[:, head_lanes(h)], 0.0)
        sc = jnp.where(sc == 0.0, 0.0, sc)
        bits = lax.bitcast_convert_type(sc, I32)
        key = bits ^ ((bits >> 31) & 0x7FFFFFFF)
        key = jnp.where(causal_of(ks), key, INT_MIN)
        key_ref[pl.ds(ks, tk), :] = key
        hi_ref[pl.ds(ks, tk), :] = (key >> 16).astype(I16)
        lo_ref[pl.ds(ks, tk), :] = ((key & 0xFFFF) - HALF16).astype(I16)

    tile_pairs(score_tile)

    def count16(half_ref, pred):
        def body(kt, cnt):
            return cnt + jnp.where(pred(half_ref[pl.ds(tile_start(kt), tk), :]), jnp.int16(1), jnp.int16(0))
        cnt = lax.fori_loop(0, n_tiles, body, jnp.zeros((tk, tq), I16))
        return jnp.sum(cnt.astype(I32), axis=0, keepdims=True)

    def nth_largest16(half_ref, rank):
        def bit_step(i, cand):
            trial = cand | lax.shift_left(jnp.int32(1), 15 - i)
            thr = (trial - HALF16).astype(I16)
            tot = count16(half_ref, lambda v: v >= thr)
            return jnp.where(tot >= rank, trial, cand)
        return lax.fori_loop(0, 16, bit_step, jnp.zeros((1, tq), I32)) - HALF16

    tau_hi = nth_largest16(hi_ref, n_sel)
    tau_hi16 = tau_hi.astype(I16)
    above_hi = count16(hi_ref, lambda v: v > tau_hi16)

    def mask_low(kt, carry):
        rows = pl.ds(tile_start(kt), tk)
        lo_ref[rows, :] = jnp.where(hi_ref[rows, :] == tau_hi16, lo_ref[rows, :], jnp.int16(-HALF16))
        return carry

    lax.fori_loop(0, n_tiles, mask_low, 0)
    tau_lo = nth_largest16(lo_ref, n_sel - above_hi)
    tau_lo16 = tau_lo.astype(I16)
    tau = lax.shift_left(tau_hi, 16) | (tau_lo + HALF16)
    n_gt = above_hi + count16(lo_ref, lambda v: v > tau_lo16)
    n_ge = above_hi + count16(lo_ref, lambda v: v >= tau_lo16)
    need = n_sel - n_gt

    m_ref[...] = jnp.full(m_ref.shape, NEG_BIG, F32)
    l_ref[...] = jnp.zeros(l_ref.shape, F32)
    acc_ref[...] = jnp.zeros(acc_ref.shape, F32)
    tie_ref[...] = jnp.zeros(tie_ref.shape, I32)
    lower_incl = jnp.where(lax.broadcasted_iota(I32, (tk, tk), 0) >= lax.broadcasted_iota(I32, (tk, tk), 1),
                           1.0, 0.0).astype(BF16)

    splits =jnp.where((n_ge > n_sel) & (tau != INT_MIN), 1.0, 0.0)
    any_split = jnp.max(splits) > 0.0

    def attend_tile(kt, cur_ref, next_ref, general_bias, tie_split):
        ks = tile_start(kt)
        key = key_ref[pl.ds(ks, tk), :]
        if tie_split:
            eq = key == tau
            prefix = jnp.dot(lower_incl, jnp.where(eq, 1.0, 0.0).astype(BF16), preferred_element_type=F32)
            tie_before = tie_ref[0:1, :]
            tie_rank = tie_before + prefix.astype(I32)
            sel = ((key > tau) | (eq & (tie_rank <= need))) & causal_of(ks)
            tie_ref[0:1, :] = tie_before + prefix[tk - 1:tk, :].astype(I32)
        else:
            sel = (key >= tau) & causal_of(ks)

        if general_bias:
            posk = posk_ref[0, pl.ds(ks, tk), :]
            bucket = _t5_bucket_of(jnp.maximum(posq - posk, 0))
        stage(kc_ref, qcw_ref, jnp.minimum(kt + 1, n_tiles - 1), next_ref)
        m_all = m_ref[0:1, :]
        l_all = l_ref[0:1, :]
        for h in range(DSA_HEADS):
            hl = head_lanes(h)
            s = cur_ref[:, hl]
            if general_bias:
                table = jnp.broadcast_to(tblt_ref[h:h + 1, :], (tk, LANES))
                s = s + jnp.concatenate(
                    [jnp.take_along_axis(table, bucket[:, c * LANES:(c + 1) * LANES], axis=1) for c in range(tq // LANES)],
                    axis=1)
            else:
                s = s + tbl_ref[REL_BUCKETS - 1, h]
            s = jnp.where(sel, s, NEG_BIG)
            m_old = m_all[:, hl]
            m_new = jnp.maximum(m_old, jnp.max(s, axis=0, keepdims=True))
            alpha = jnp.exp2(m_old - m_new)
            p = jnp.exp2(s - m_new)
            l_ref[0:1, hl] = alpha * l_all[:, hl] + jnp.sum(p, axis=0, keepdims=True)
            m_ref[0:1, hl] = m_new
            alpha_ref[0:1, hl] = alpha
            p_ref[:, hl] = p.astype(BF16)
        vt = vw_ref[0, 0:DSA_HEAD_DIM, pl.ds(ks, tk)].astype(BF16)
        pv = jnp.dot(vt, p_ref[...], preferred_element_type=F32)
        acc_ref[...] = alpha_ref[0:1, :] * acc_ref[...] + pv

    bi = pl.program_id(0)
    stage(kc_ref, qcw_ref, 0, staged[0])

    def attend(kt, cur_ref, next_ref):
        far = (qmin_ref[bi, qi] - kmax_ref[bi, kt]) >= T5_FAR
        fast = jnp.logical_and(far, jnp.logical_not(any_split))
        plain = jnp.logical_and(jnp.logical_not(far), jnp.logical_not(any_split))

        @pl.when(fast)
        def _():
            attend_tile(kt, cur_ref, next_ref, general_bias=False, tie_split=False)

        @pl.when(plain)
        def _():
            attend_tile(kt, cur_ref, next_ref, general_bias=True, tie_split=False)

        @pl.when(any_split)
        def _():
            attend_tile(kt, cur_ref, next_ref, general_bias=True, tie_split=True)

    tile_pairs(attend)

    for h in range(DSA_HEADS):
        hl = head_lanes(h)
        ot_ref[h * DSA_HEAD_DIM:(h + 1) * DSA_HEAD_DIM, :] = acc_ref[:, hl] / l_ref[0:1, hl]
    o_ref[0] = (ot_ref[...].T * gate_ref[0]).astype(o_ref.dtype)


def _dsa(rel_bias, qcw, qiw, kc, ki, vw, pos_col, pos_row, gate, tq):
    b, s, _ = kc.shape
    n_sel = min(TOPK_MAX, s // 4)
    vw_rows = vw.shape[1]
    hw = DSA_HEADS * tq
    tbl = rel_bias.astype(F32) * LOG2E
    tbl_t = jnp.pad(tbl.T, ((0, 0), (0, LANES - REL_BUCKETS)))
    q_min = jnp.min(pos_row.reshape(b, s // tq, tq), axis=-1)
    k_max = jnp.max(pos_row.reshape(b, s // DSA_KEY_TILE, DSA_KEY_TILE), axis=-1)
    tk = DSA_KEY_TILE
    vmem = (2 * (2 * _nbytes((DSA_HEAD_DIM, hw), BF16) + 2 * _nbytes((s, LANES), BF16) + _nbytes((vw_rows, s), F32)
                 + _nbytes((vw_rows, tq), F32) + _nbytes((s, LANES), I32) + _nbytes((tq, DSA_WIDTH), F32)
                 + _nbytes((tq, DSA_WIDTH), BF16)) + _nbytes((s, tq), I32) + 2 * _nbytes((tk, hw), F32)
            + _nbytes((tk, hw), BF16) + _nbytes((DSA_HEAD_DIM, hw), F32) + _nbytes((DSA_WIDTH, tq), F32)
            + 32 * _nbytes((tk, tq), F32))
    return pl.pallas_call(
        functools.partial(_dsa_kernel, n_sel=n_sel, tq=tq),
        grid=(b, s // tq),
        in_specs=[pl.BlockSpec(memory_space=pltpu.SMEM), pl.BlockSpec(memory_space=pltpu.SMEM),
                  pl.BlockSpec(memory_space=pltpu.SMEM),
                  pl.BlockSpec((DSA_HEADS, LANES), lambda bi, i: (0, 0)),
                  pl.BlockSpec((1, 1, DSA_HEAD_DIM, hw), lambda bi, i: (bi, i, 0, 0)),
                  pl.BlockSpec((1, 1, IDX_DIM, hw), lambda bi, i: (bi, i, 0, 0)),
                  pl.BlockSpec((1, s, DSA_HEAD_DIM), lambda bi, i: (bi, 0, 0)),
                  pl.BlockSpec((1, s, IDX_DIM), lambda bi, i: (bi, 0, 0)),
                  pl.BlockSpec((1, vw_rows, s), lambda bi, i: (bi, 0, 0)),
                  pl.BlockSpec((1, vw_rows, tq), lambda bi, i: (bi, 0, i)),
                  pl.BlockSpec((1, s, 1), lambda bi, i: (bi, 0, 0)),
                  pl.BlockSpec((1, 1, tq), lambda bi, i: (bi, 0, i)),
                  pl.BlockSpec((1, tq, DSA_WIDTH), lambda bi, i: (bi, i, 0))],
        out_specs=pl.BlockSpec((1, tq, DSA_WIDTH), lambda bi, i: (bi, i, 0)),
        out_shape=jax.ShapeDtypeStruct((b, s, DSA_WIDTH), BF16),
        scratch_shapes=[pltpu.VMEM((s, tq), I32),
                        pltpu.VMEM((s, tq), I16), pltpu.VMEM((s, tq), I16),
                        pltpu.VMEM((tk, hw), F32), pltpu.VMEM((tk, hw), F32),
                        pltpu.VMEM((tk, hw), BF16),
                        pltpu.VMEM((SUBLANES, hw), F32),
                        pltpu.VMEM((DSA_HEAD_DIM, hw), F32),
                        pltpu.VMEM((DSA_WIDTH, tq), F32),
                        pltpu.VMEM((SUBLANES, hw), F32), pltpu.VMEM((SUBLANES, hw), F32),
                        pltpu.VMEM((SUBLANES, tq), I32)],
        compiler_params=_compiler_params(("parallel", "arbitrary"), vmem),
        name="dsa",
    )(tbl, q_min, k_max, tbl_t, qcw, qiw, kc, ki, vw, vw, pos_col, pos_row, gate)


def _merge_kernel(oa_ref, ob_ref, oc_ref, g_ref, x_ref, p_ref, wa_ref, wb_ref, wc_ref, wo_ref, wpg_ref, wple_ref,
                  gn_ref, xo_ref, ho_ref):
    d = D_MODEL
    ya = jnp.dot(oa_ref[...], wa_ref[...], preferred_element_type=F32)
    yb = jnp.dot(ob_ref[...], wb_ref[...], preferred_element_type=F32)
    yc = jnp.dot(oc_ref[...], wc_ref[...], preferred_element_type=F32)
    merged = g_ref[:, 0:d] * ya + g_ref[:, d:2 * d] * yb + g_ref[:, 2 * d:3 * d] * yc
    x1 = x_ref[...] + jnp.dot(merged.astype(BF16), wo_ref[...], preferred_element_type=F32)
    ple_gate = _sigmoid(jnp.dot(x1.astype(BF16), wpg_ref[...], preferred_element_type=F32))
    x2 = x1 + ple_gate * jnp.dot(p_ref[...].astype(BF16), wple_ref[...], preferred_element_type=F32)
    xo_ref[...] = x2
    ho_ref[...] = _rms(x2, gn_ref[...]).astype(ho_ref.dtype)


def _merge(oa, ob, oc, gates, x, p, wa, wb, wc, wo, wpg, wple, g_next, h_dtype, tm=256):
    t, d = x.shape
    full = lambda a: pl.BlockSpec(a.shape, lambda i: (0, 0))
    row = lambda a: pl.BlockSpec((tm, a.shape[1]), lambda i: (i, 0))
    weights = (wa, wb, wc, wo, wpg, wple)
    acts = (oa, ob, oc, gates, x, p)
    vmem = (2 * (sum(_nbytes(a.shape, a.dtype) for a in weights) + sum(_nbytes((tm, a.shape[1]), a.dtype) for a in acts)
                 + 2 * _nbytes((tm, d), F32)) + 8 * _nbytes((tm, d), F32))
    return pl.pallas_call(
        _merge_kernel,
        grid=(t // tm,),
        in_specs=[row(a) for a in acts] + [full(a) for a in weights] + [pl.BlockSpec((1, d), lambda i: (0, 0))],
        out_specs=[pl.BlockSpec((tm, d), lambda i: (i, 0)), pl.BlockSpec((tm, d), lambda i: (i, 0))],
        out_shape=[jax.ShapeDtypeStruct((t, d), F32), jax.ShapeDtypeStruct((t, d), h_dtype)],
        compiler_params=_compiler_params(("parallel",), vmem),
        name="merge",
    )(*acts, *weights, g_next.reshape(1, d))


def _rotate_half_cols(w):
    half = w.shape[-1] // 2
    return jnp.concatenate([-w[..., half:], w[..., :half]], axis=-1)


def _prep_weights(w_in, w_uq, w_ukv):
    depth = w_in.shape[0]
    seg = lambda name: w_in[:, :, SPLIT[name][0]:SPLIT[name][1]]
    z = lambda *shape: jnp.zeros((depth,) + shape, w_in.dtype)
    d = D_MODEL
    kr = seg("k_rope")
    rope_lo = MLA_NOPE
    rope_pad = MLA_QK_PAD - MLA_NOPE - MLA_ROPE
    w = {}
    w["mla_in"] = jnp.concatenate(
        [seg("c_q"), seg("c_kv"), z(d, rope_lo), kr, z(d, rope_pad), z(d, rope_lo), _rotate_half_cols(kr), z(d, rope_pad)], axis=-1)
    w["gate_a"] = seg("gate_a")
    w["z"] = seg("z")
    w["xbc"] = seg("xbc")
    w["small"] = jnp.concatenate([seg("dt"), z(d, LANES - SSM_HEADS)], axis=-1)
    w["k_c"] = seg("k_c")
    w["k_idx"] = seg("k_idx")
    w["gate_c"] = seg("gate_c")
    w["merge"] = seg("merge")
    w["q_c_t"] = jnp.swapaxes(seg("q_c") * (DSA_HEAD_DIM ** -0.5 * LOG2E), 1, 2)
    w["q_idx_t"] = jnp.swapaxes(seg("q_idx"), 1, 2)
    w["vw_t"] = jnp.swapaxes(jnp.concatenate([seg("v_c"), seg("w_idx")], axis=-1), 1, 2)
    w = {
        "proj": jnp.concatenate([w[seg[0]] for seg in PROJ_SEGMENTS], axis=-1).astype(BF16),
        "proj_t": jnp.concatenate([w["q_c_t"], w["q_idx_t"], w["vw_t"]], axis=1).astype(BF16),
    }

    uq =w_uq.reshape(depth, MLA_Q_LORA, MLA_HEADS, MLA_NOPE + MLA_ROPE)
    nope, rope = uq[..., :MLA_NOPE], uq[..., MLA_NOPE:]
    zq = lambda width: jnp.zeros((depth, MLA_Q_LORA, MLA_HEADS, width), w_uq.dtype)
    hq = MLA_HEADS * MLA_QK_PAD
    uq1 = jnp.concatenate([nope, rope, zq(rope_pad)], axis=-1).reshape(depth, MLA_Q_LORA, hq)
    uq2 = jnp.concatenate([zq(MLA_NOPE), _rotate_half_cols(rope), zq(rope_pad)], axis=-1).reshape(depth, MLA_Q_LORA, hq)
    w["uq1_t"] = jnp.swapaxes(uq1, 1, 2).astype(BF16)
    w["uq2_t"] = jnp.swapaxes(uq2, 1, 2).astype(BF16)
    ukv = w_ukv.reshape(depth, MLA_KV_LORA, MLA_HEADS, MLA_NOPE + MLA_V)
    zk = jnp.zeros((depth, MLA_KV_LORA, MLA_HEADS, MLA_QK_PAD - MLA_NOPE), w_ukv.dtype)
    w["uk"] = jnp.concatenate([ukv[..., :MLA_NOPE], zk], axis=-1).reshape(depth, MLA_KV_LORA, hq).astype(BF16)
    w["uv_t"] = jnp.swapaxes(ukv[..., MLA_NOPE:].reshape(depth, MLA_KV_LORA, MLA_WIDTH), 1, 2).astype(BF16)
    return w


def _rope_multipliers(positions):
    b, s = positions.shape
    inv_freq = 1.0 / (ROPE_THETA ** (jnp.arange(0, MLA_ROPE, 2, dtype=F32) / MLA_ROPE))
    ang = positions.astype(F32)[..., None] * inv_freq
    cos, sin = jnp.cos(ang), jnp.sin(ang)
    pad = MLA_QK_PAD - MLA_NOPE - MLA_ROPE
    m1 = jnp.concatenate([jnp.ones((b, s, MLA_NOPE), F32), cos, cos, jnp.zeros((b, s, pad), F32)], axis=-1)
    m2 = jnp.concatenate([jnp.zeros((b, s, MLA_NOPE), F32), sin, sin, jnp.zeros((b, s, pad), F32)], axis=-1)
    return m1, m2, jnp.swapaxes(m1, 1, 2), jnp.swapaxes(m2, 1, 2)


def kernel(x, p, positions, norm_g, w_in, mla_q_norm, w_uq, mla_kv_norm, w_ukv, conv_w, conv_b, dt_bias, a_log, d_skip, ssm_norm, w_br_a, w_br_b, w_br_c, w_out, rel_bias, w_ple, w_ple_gate, final_norm):
    b, s, d = x.shape
    depth = w_in.shape[0]
    t = b * s
    w = _prep_weights(w_in, w_uq, w_ukv)
    m1, m2, m1t, m2t = _rope_multipliers(positions)
    pos_col = positions.astype(I32).reshape(b, s, 1)
    pos_row = positions.astype(I32).reshape(b, 1, s)
    wbr_a, wbr_b, wbr_c = w_br_a.astype(BF16), w_br_b.astype(BF16), w_br_c.astype(BF16)
    wo, wpg, wple = w_out.astype(BF16), w_ple_gate.astype(BF16), w_ple.astype(BF16)

    x2d = x.reshape(t, d)
    h = _norm(x2d, norm_g[0], BF16)
    for i in range(depth):
        pr = _proj_all(h.reshape(b, s, d), w["proj"][i], w["proj_t"][i], DSA_Q_BLOCK)

        qt, k, vt = _mla_prep(pr["mla_in"], m1, m2, m1t, m2t, mla_q_norm[i], mla_kv_norm[i],
                              w["uq1_t"][i], w["uq2_t"][i], w["uk"][i], w["uv_t"][i])
        o_a = _mla_attn(qt, k, vt, pr["gate_a"])
        o_b = _ssd(pr["xbc"], pr["z"], pr["small"], conv_w[i], conv_b[i], dt_bias[i], a_log[i], d_skip[i], ssm_norm[i])
        o_c = _dsa(rel_bias, pr["q_c_wide"], pr["q_idx_wide"], pr["k_c"], pr["k_idx"], pr["vw_t"], pos_col, pos_row,
                   pr["gate_c"], DSA_Q_BLOCK)

        last = i == depth - 1
        g_next = final_norm if last else norm_g[i + 1]
        x2d, h = _merge(o_a.reshape(t, -1), o_b.reshape(t, -1), o_c.reshape(t, -1), pr["merge"].reshape(t, -1), x2d, p[i].reshape(t, -1),
                        wbr_a[i], wbr_b[i], wbr_c[i], wo[i], wpg[i], wple[i], g_next, F32 if last else BF16)
    return h.reshape(b, s, d)
```

```python
import functools
import math

import jax
import jax.numpy as jnp
from jax import lax
from jax.experimental import pallas as pl
from jax.experimental.pallas import tpu as pltpu

F32 = jnp.float32
BF16 = jnp.bfloat16
I32 = jnp.int32
I16 = jnp.int16
HALF16 = 1 << 15

D_MODEL = 1024
PLE_DIM = 256
NORM_EPS = 1e-6

MLA_HEADS = 8
MLA_NOPE = 64
MLA_ROPE = 32
MLA_V = 64
MLA_Q_LORA = 384
MLA_KV_LORA = 256
MLA_WIDTH = MLA_HEADS * MLA_V
ROPE_THETA = 10000.0
MLA_QK_PAD = 128
LOG2E = math.log2(math.e)
MLA_Q_SCALE = (MLA_NOPE + MLA_ROPE) ** -0.5 * LOG2E

SSM_HEADS = 16
SSM_HEAD_DIM = 64
SSM_INNER = SSM_HEADS * SSM_HEAD_DIM
SSM_GROUPS = 2
SSM_STATE = 128
SSM_CONV = 4
SSM_CHUNK = 128
SSM_CONV_DIM = SSM_INNER + 2 * SSM_GROUPS * SSM_STATE
SSM_HEADS_PER_GROUP = SSM_HEADS // SSM_GROUPS
ONES_ROWS = 16
MLA_V1_ROWS = MLA_V + ONES_ROWS
SSM_CONV_TAIL = 16

DSA_HEADS = 8
DSA_HEAD_DIM = 64
DSA_WIDTH = DSA_HEADS * DSA_HEAD_DIM
IDX_HEADS = 8
IDX_DIM = 64
TOPK_MAX = 256
DSA_KEY_TILE = 128
DSA_Q_BLOCK = 256

REL_BUCKETS = 32
REL_MAX_DIST = 128
N_BRANCHES = 3

LANES = 128
SUBLANES = 8
VMEM_LIMIT_CAP = 56 * 1024 * 1024
INT_MIN = -(2 ** 31)
NEG_BIG = -1e30

SPLIT_SIZES = (
    MLA_Q_LORA, MLA_KV_LORA, MLA_ROPE, MLA_WIDTH, SSM_INNER, SSM_CONV_DIM, SSM_HEADS, DSA_WIDTH,
    DSA_HEAD_DIM, DSA_HEAD_DIM, IDX_HEADS * IDX_DIM, IDX_DIM, IDX_HEADS, DSA_WIDTH,
    N_BRANCHES * D_MODEL,
)
SPLIT_NAMES = ("c_q", "c_kv", "k_rope", "gate_a", "z", "xbc", "dt", "q_c", "k_c", "v_c", "q_idx",
               "k_idx", "w_idx", "gate_c", "merge")


def _split_bounds():
    out, off = {}, 0
    for name, size in zip(SPLIT_NAMES, SPLIT_SIZES, strict=True):
        out[name] = (off, off + size)
        off += size
    return out


SPLIT = _split_bounds()


def _t5_large_thresholds():
    exact = REL_BUCKETS // 2
    thr = []
    for j in range(1, REL_BUCKETS - exact):
        thr.append(int(math.ceil(exact * (REL_MAX_DIST / exact) ** (j / (REL_BUCKETS - exact)) - 1e-9)))
    return tuple(thr)


T5_EXACT = REL_BUCKETS // 2
T5_LARGE_THR = _t5_large_thresholds()
T5_FAR = T5_LARGE_THR[-1]


def _compiler_params(semantics, vmem_bytes):
    limit = int(min(VMEM_LIMIT_CAP, max(32 * 1024 * 1024, vmem_bytes)))
    return pltpu.CompilerParams(dimension_semantics=semantics, vmem_limit_bytes=limit)


def _nbytes(shape, dtype):
    return math.prod(shape) * jnp.dtype(dtype).itemsize


def _sigmoid(x):
    return 1.0 / (1.0 + jnp.exp(-x))


def _rms(x, g):
    return x * lax.rsqrt(jnp.mean(x * x, axis=-1, keepdims=True) + NORM_EPS) * g


def _norm_kernel(x_ref, g_ref, o_ref):
    o_ref[...] = _rms(x_ref[...], g_ref[...]).astype(o_ref.dtype)


def _norm(x2d, g, out_dtype, tm=512):
    t, d = x2d.shape
    return pl.pallas_call(
        _norm_kernel,
        grid=(t // tm,),
        in_specs=[pl.BlockSpec((tm, d), lambda i: (i, 0)), pl.BlockSpec((1, d), lambda i: (0, 0))],
        out_specs=pl.BlockSpec((tm, d), lambda i: (i, 0)),
        out_shape=jax.ShapeDtypeStruct((t, d), out_dtype),
        compiler_params=_compiler_params(("parallel",), 4 * _nbytes((tm, d), F32)),
        name="rms_norm",
    )(x2d, g.reshape(1, d))


PROJ_SEGMENTS = (
    ("mla_in", MLA_Q_LORA + MLA_KV_LORA + 2 * LANES, None, F32),
    ("small", LANES, None, F32),
    ("gate_a", MLA_WIDTH, "silu", BF16),
    ("gate_c", DSA_WIDTH, "silu", BF16),
    ("z", SSM_INNER, "silu", BF16),
    ("xbc", SSM_CONV_DIM, None, BF16),
    ("merge", N_BRANCHES * D_MODEL, "sigmoid", BF16),
    ("k_c", DSA_HEAD_DIM, None, BF16),
    ("k_idx", IDX_DIM, None, BF16),
)
PROJ_CHUNK = 512
PROJ_T_ROWS = (DSA_WIDTH, IDX_HEADS * IDX_DIM, DSA_HEAD_DIM + IDX_HEADS)


def _proj_all_kernel(h_ref, w_ref, wt_ref, *out_refs):
    h = h_ref[0]
    n_seg = len(PROJ_SEGMENTS)
    off = 0
    for (_, width, act, _), o_ref in zip(PROJ_SEGMENTS, out_refs[:n_seg], strict=True):
        for c in range(0, width, PROJ_CHUNK):
            cw = min(PROJ_CHUNK, width - c)
            y = jnp.dot(h, w_ref[:, off + c:off + c + cw], preferred_element_type=F32)
            if act == "silu":
                y = y * _sigmoid(y)
            elif act == "sigmoid":
                y = _sigmoid(y)
            o_ref[0, :, c:c + cw] = y.astype(o_ref.dtype)
        off += width
    qcw_ref, qiw_ref, v1_ref, wi_ref = out_refs[n_seg:]
    nt = (((1,), (1,)), ((), ()))
    tq = h.shape[0]
    row = 0
    for o_ref, heads, dim in ((qcw_ref, DSA_HEADS, DSA_HEAD_DIM), (qiw_ref, IDX_HEADS, IDX_DIM)):
        y = lax.dot_general(wt_ref[row:row + heads * dim, :], h, nt, preferred_element_type=F32)
        for hd in range(heads):
            o_ref[0, 0, :, hd * tq:(hd + 1) * tq] = y[hd * dim:(hd + 1) * dim, :].astype(o_ref.dtype)
        row += heads * dim
    vw = lax.dot_general(wt_ref[row:row + PROJ_T_ROWS[2], :], h, nt, preferred_element_type=F32)
    v1_ref[0, 0:DSA_HEAD_DIM, :] = vw[0:DSA_HEAD_DIM, :].astype(v1_ref.dtype)
    v1_ref[0, DSA_HEAD_DIM:, :] = jnp.ones((ONES_ROWS, tq), v1_ref.dtype)
    wi_ref[0] = vw[DSA_HEAD_DIM:, :]


def _proj_all(h3d, w, wt, tq):
    b, s, k = h3d.shape
    n_all = sum(seg[1] for seg in PROJ_SEGMENTS)
    v1_rows = DSA_HEAD_DIM + ONES_ROWS
    out_shapes = [jax.ShapeDtypeStruct((b, s, width), dt) for _, width, _, dt in PROJ_SEGMENTS]
    out_specs = [pl.BlockSpec((1, tq, width), lambda bi, i: (bi, i, 0)) for _, width, _, _ in PROJ_SEGMENTS]
    for heads, dim in ((DSA_HEADS, DSA_HEAD_DIM), (IDX_HEADS, IDX_DIM)):
        out_shapes.append(jax.ShapeDtypeStruct((b, s // tq, dim, heads * tq), BF16))
        out_specs.append(pl.BlockSpec((1, 1, dim, heads * tq), lambda bi, i: (bi, i, 0, 0)))
    for rows, dt in ((v1_rows, BF16), (IDX_HEADS, F32)):
        out_shapes.append(jax.ShapeDtypeStruct((b, rows, s), dt))
        out_specs.append(pl.BlockSpec((1, rows, tq), lambda bi, i: (bi, 0, i)))
    resident = pl.Buffered(1)
    vmem = (_nbytes((k, n_all), BF16) + _nbytes((sum(PROJ_T_ROWS), k), BF16) + 2 * _nbytes((tq, k), BF16)
            + 2 * sum(_nbytes((tq, width), dt) for _, width, _, dt in PROJ_SEGMENTS)
            + 2 * (2 * _nbytes((DSA_HEAD_DIM, DSA_HEADS * tq), BF16) + _nbytes((v1_rows + IDX_HEADS, tq), F32))
            + 8 * _nbytes((tq, PROJ_CHUNK), F32))
    outs = pl.pallas_call(
        _proj_all_kernel,
        grid=(b, s // tq),
        in_specs=[pl.BlockSpec((1, tq, k), lambda bi, i: (bi, i, 0)),
                  pl.BlockSpec((k, n_all), lambda bi, i: (0, 0), pipeline_mode=resident),
                  pl.BlockSpec((sum(PROJ_T_ROWS), k), lambda bi, i: (0, 0), pipeline_mode=resident)],
        out_specs=out_specs,
        out_shape=out_shapes,
        compiler_params=_compiler_params(("parallel", "parallel"), vmem),
        name="proj_all",
    )(h3d, w, wt)
    names = [seg[0] for seg in PROJ_SEGMENTS] + ["q_c_wide", "q_idx_wide", "v_t1", "w_idx_t"]
    return dict(zip(names, outs, strict=True))


def _mla_prep_kernel(a_ref, m1_ref, m2_ref, m1t_ref, m2t_ref, qn_ref, kvn_ref, wq1t_ref, wq2t_ref, wk_ref, wvt_ref,
                     qt_ref, k_ref, vt_ref):
    a = a_ref[0]
    c_q = a[:, :MLA_Q_LORA]
    c_kv = a[:, MLA_Q_LORA:MLA_Q_LORA + MLA_KV_LORA]
    kr1 = a[:, MLA_Q_LORA + MLA_KV_LORA:MLA_Q_LORA + MLA_KV_LORA + LANES]
    kr2 = a[:, MLA_Q_LORA + MLA_KV_LORA + LANES:]
    cqn = _rms(c_q, qn_ref[...]).astype(BF16)
    ckvn = _rms(c_kv, kvn_ref[...]).astype(BF16)
    nt = (((1,), (1,)), ((), ()))
    qa_t = lax.dot_general(wq1t_ref[...], cqn, nt, preferred_element_type=F32)
    qb_t = lax.dot_general(wq2t_ref[...], cqn, nt, preferred_element_type=F32)
    kn = jnp.dot(ckvn, wk_ref[...], preferred_element_type=F32)
    kr = kr1 * m1_ref[0] + kr2 * m2_ref[0]
    m1t = m1t_ref[0] * MLA_Q_SCALE
    m2t = m2t_ref[0] * MLA_Q_SCALE
    for h in range(MLA_HEADS):
        sl = slice(h * MLA_QK_PAD, (h + 1) * MLA_QK_PAD)
        qt_ref[0, sl, :] = (qa_t[sl, :] * m1t + qb_t[sl, :] * m2t).astype(qt_ref.dtype)
        k_ref[0, :, sl] = (kn[:, sl] + kr).astype(k_ref.dtype)
    v_t = lax.dot_general(wvt_ref[...], ckvn, nt, preferred_element_type=F32)
    ones = jnp.ones((ONES_ROWS, v_t.shape[1]), vt_ref.dtype)
    for h in range(MLA_HEADS):
        base = h * MLA_V1_ROWS
        vt_ref[0, base:base + MLA_V, :] = v_t[h * MLA_V:(h + 1) * MLA_V, :].astype(vt_ref.dtype)
        vt_ref[0, base + MLA_V:base + MLA_V1_ROWS, :] = ones


def _mla_prep(a, m1, m2, m1t, m2t, qn, kvn, wq1t, wq2t, wk, wvt, tm=256):
    b, s, wa = a.shape
    hq = MLA_HEADS * MLA_QK_PAD
    full = lambda shape: pl.BlockSpec(shape, lambda bi, i: (0, 0))
    row = lambda width: pl.BlockSpec((1, tm, width), lambda bi, i: (bi, i, 0))
    col = lambda height: pl.BlockSpec((1, height, tm), lambda bi, i: (bi, 0, i))
    vmem = (2 * (_nbytes((tm, wa), F32) + 4 * _nbytes((tm, LANES), F32) + 2 * _nbytes((MLA_Q_LORA, hq), BF16)
                 + _nbytes((MLA_KV_LORA, hq), BF16) + _nbytes((MLA_KV_LORA, MLA_WIDTH), BF16)
                 + 2 * _nbytes((tm, hq), BF16) + _nbytes((tm, MLA_WIDTH), BF16)) + 4 * _nbytes((tm, hq), F32))
    return pl.pallas_call(
        _mla_prep_kernel,
        grid=(b, s // tm),
        in_specs=[row(wa), row(LANES), row(LANES), col(LANES), col(LANES), full((1, MLA_Q_LORA)), full((1, MLA_KV_LORA)),
                  full((hq, MLA_Q_LORA)), full((hq, MLA_Q_LORA)), full((MLA_KV_LORA, hq)), full((MLA_WIDTH, MLA_KV_LORA))],
        out_specs=[col(hq), row(hq), col(MLA_HEADS * MLA_V1_ROWS)],
        out_shape=[jax.ShapeDtypeStruct((b, hq, s), BF16), jax.ShapeDtypeStruct((b, s, hq), BF16),
                   jax.ShapeDtypeStruct((b, MLA_HEADS * MLA_V1_ROWS, s), BF16)],
        compiler_params=_compiler_params(("parallel", "parallel"), vmem),
        name="mla_prep",
    )(a, m1, m2, m1t, m2t, qn.reshape(1, -1), kvn.reshape(1, -1), wq1t, wq2t, wk, wvt)


def _mla_attn_kernel(qt_ref, k_ref, vt_ref, g_ref, o_ref, m_ref, l_ref, acc_ref, s0_ref, s1_ref, *, tq, tk):
    assert tq == 2 * tk
    qi = pl.program_id(1)
    krow = lax.broadcasted_iota(I32, (tk, tq), 0)
    qcol = lax.broadcasted_iota(I32, (tk, tq), 1)
    tiles_per_q = tq // tk
    m_ref[...] = jnp.full(m_ref.shape, -jnp.inf, F32)
    l_ref[...] = jnp.zeros(l_ref.shape, F32)
    acc_ref[...] = jnp.zeros(acc_ref.shape, F32)

    staged = (s0_ref, s1_ref)
    n_tiles = (qi + 1) * tiles_per_q

    def head_lanes(h):
        return slice(h * tq, (h + 1) * tq)

    def stage(j, dst_ref):
        ks = pl.multiple_of(j * tk, tk)
        for h in range(MLA_HEADS):
            k = k_ref[0, pl.ds(ks, tk), h * MLA_QK_PAD:(h + 1) * MLA_QK_PAD]
            dst_ref[:, head_lanes(h)] = jnp.dot(k, qt_ref[0, h * MLA_QK_PAD:(h + 1) * MLA_QK_PAD, :],
                                                preferred_element_type=F32)

    def tile(j, cur_ref, next_ref, diagonal):
        ks = pl.multiple_of(j * tk, tk)
        stage(jnp.minimum(j + 1, n_tiles - 1), next_ref)
        m_all = m_ref[...]
        l_all = l_ref[...]
        if diagonal:
            causal = ks + krow <= qi * tq + qcol
        for h in range(MLA_HEADS):
            s = cur_ref[:, head_lanes(h)]
            if diagonal:
                s = jnp.where(causal, s, -jnp.inf)
            m_old = m_all[h:h + 1, :]
            m_new = jnp.maximum(m_old, jnp.max(s, axis=0, keepdims=True))
            alpha = jnp.exp2(m_old - m_new)
            p = jnp.exp2(s - m_new).astype(BF16)
            m_ref[h:h + 1, :] = m_new
            pv = jnp.dot(vt_ref[0, h * MLA_V1_ROWS:(h + 1) * MLA_V1_ROWS, pl.ds(ks, tk)], p, preferred_element_type=F32)
            hs = slice(h * MLA_V, (h + 1) * MLA_V)
            acc_ref[hs, :] = alpha * acc_ref[hs, :] + pv[0:MLA_V, :]
            l_ref[h:h + 1, :] = alpha * l_all[h:h + 1, :] + pv[MLA_V:MLA_V + 1, :]

    stage(0, staged[0])

    def full_pair(pi, carry):
        for half in range(2):
            tile(2 * pi + half, staged[half], staged[1 - half], diagonal=False)
        return carry

    lax.fori_loop(0, qi, full_pair, 0)
    for half in range(2):
        tile(2 * qi + half, staged[half], staged[1 - half], diagonal=True)
    for h in range(MLA_HEADS):
        hs = slice(h * MLA_V, (h + 1) * MLA_V)
        acc_ref[hs, :] = acc_ref[hs, :] / l_ref[h:h + 1, :]
    o_ref[0] = (acc_ref[...].T * g_ref[0]).astype(o_ref.dtype)


def _mla_attn(qt, k, vt, gate, tq=256, tk=128):
    b, s, hq = k.shape
    vmem = 2 * (_nbytes((hq, tq), BF16) + _nbytes((s, hq), BF16) + _nbytes((MLA_WIDTH, s), BF16)
                + _nbytes((tq, MLA_WIDTH), F32) + _nbytes((tq, MLA_WIDTH), BF16)) + _nbytes((MLA_WIDTH, tq), F32) + 16 * _nbytes((tk, tq), F32)
    return pl.pallas_call(
        functools.partial(_mla_attn_kernel, tq=tq, tk=tk),
        grid=(b, s // tq),
        in_specs=[pl.BlockSpec((1, hq, tq), lambda bi, i: (bi, 0, i)),
                  pl.BlockSpec((1, s, hq), lambda bi, i: (bi, 0, 0)),
                  pl.BlockSpec((1, MLA_HEADS * MLA_V1_ROWS, s), lambda bi, i: (bi, 0, 0)),
                  pl.BlockSpec((1, tq, MLA_WIDTH), lambda bi, i: (bi, i, 0))],
        out_specs=pl.BlockSpec((1, tq, MLA_WIDTH), lambda bi, i: (bi, i, 0)),
        out_shape=jax.ShapeDtypeStruct((b, s, MLA_WIDTH), BF16),
        scratch_shapes=[pltpu.VMEM((MLA_HEADS, tq), F32), pltpu.VMEM((MLA_HEADS, tq), F32),
                        pltpu.VMEM((MLA_WIDTH, tq), F32),
                        pltpu.VMEM((tk, MLA_HEADS * tq), F32), pltpu.VMEM((tk, MLA_HEADS * tq), F32)],
        compiler_params=_compiler_params(("parallel", "arbitrary"), vmem),
        name="mla_attn",
    )(qt, k, vt, gate)


def _ssd_kernel(xbc_ref, xprev_ref, zs_ref, sm_ref, cw_ref, cb_ref, dtb_ref, alog_ref, dskf_ref, nrm_ref, exp16_ref, exp32_ref,
                o_ref, state_ref, y_ref):
    c = pl.program_id(1)
    q = SSM_CHUNK

    @pl.when(c == 0)
    def _():
        state_ref[...] = jnp.zeros(state_ref.shape, F32)

    x_in = xbc_ref[0]
    tail = xprev_ref.shape[1]
    prev = jnp.where(c > 0, xprev_ref[0].astype(F32), 0.0).astype(x_in.dtype)
    win = jnp.concatenate([prev, x_in], axis=0)
    x = x_in.astype(F32)
    wrow = lax.broadcasted_iota(I32, (q, tail + q), 0)
    wcol = lax.broadcasted_iota(I32, (q, tail + q), 1)
    acc = cb_ref[...] + cw_ref[SSM_CONV - 1:SSM_CONV, :] * x
    for j in range(1, SSM_CONV):
        pick = jnp.where(wcol == wrow + (tail - j), 1.0, 0.0).astype(win.dtype)
        acc = acc + cw_ref[SSM_CONV - 1 - j:SSM_CONV - j, :] * jnp.dot(pick, win, preferred_element_type=F32)
    xc = acc * _sigmoid(acc)

    xs = xc[:, :SSM_INNER]
    bm = xc[:, SSM_INNER:SSM_INNER + SSM_GROUPS * SSM_STATE]
    cm = xc[:, SSM_INNER + SSM_GROUPS * SSM_STATE:]

    pre = sm_ref[0] + dtb_ref[...]
    dt = jnp.maximum(pre, 0.0) + jnp.log1p(jnp.exp(-jnp.abs(pre)))
    a = -jnp.exp(alog_ref[...])
    row = lax.broadcasted_iota(I32, (q, q), 0)
    col = lax.broadcasted_iota(I32, (q, q), 1)
    lower = row >= col
    cum = jnp.dot(jnp.where(lower, 1.0, 0.0).astype(F32), dt * a, preferred_element_type=F32,
                  precision=lax.Precision.HIGHEST)
    cum_t = cum.T
    cum_last = cum[q - 1:q, :]
    per_head = jnp.concatenate([dt, jnp.exp(cum_last - cum), jnp.exp(cum)], axis=0).astype(BF16)
    spread = jnp.dot(per_head, exp16_ref[...], preferred_element_type=F32)
    dt_full, to_end_full, e_cum_full = spread[0:q], spread[q:2 * q], spread[2 * q:3 * q]
    chunk_decay_full = jnp.dot(jnp.broadcast_to(jnp.exp(cum_last), (SUBLANES, LANES)), exp32_ref[...],
                               preferred_element_type=F32, precision=lax.Precision.HIGHEST)[0:1]
    xdt = xs * dt_full
    xw = (xdt * to_end_full).astype(BF16)
    first_head = lax.broadcasted_iota(I32, (q, LANES), 1) < SSM_HEAD_DIM
    group_w = SSM_HEADS_PER_GROUP * SSM_HEAD_DIM
    heads_per_tile = LANES // SSM_HEAD_DIM

    for g in range(SSM_GROUPS):
        bg32 = bm[:, g * SSM_STATE:(g + 1) * SSM_STATE]
        bg = bg32.astype(BF16)
        cg = cm[:, g * SSM_STATE:(g + 1) * SSM_STATE].astype(BF16)
        cb = lax.dot_general(cg, bg, (((1,), (1,)), ((), ())), preferred_element_type=F32)
        gl = slice(g * group_w, (g + 1) * group_w)
        st = state_ref[g]
        y_off = jnp.dot(cg, st.astype(BF16), preferred_element_type=F32)
        state_ref[g] = st * chunk_decay_full[:, gl] + jnp.dot(bg32.T.astype(BF16), xw[:, gl], preferred_element_type=F32)
        for tile in range(group_w // LANES):
            h0 = g * SSM_HEADS_PER_GROUP + tile * heads_per_tile
            tl = slice(h0 * SSM_HEAD_DIM, h0 * SSM_HEAD_DIM + LANES)
            g_both = []
            for h in range(h0, h0 + heads_per_tile):
                diff = cum[:, h:h + 1] - cum_t[h:h + 1, :]
                decay_in = jnp.exp(jnp.where(lower, diff, -jnp.inf))
                g_both.append((cb * decay_in).astype(BF16))
            xdt_tile = xdt[:, tl]
            stacked = jnp.concatenate([jnp.where(first_head, xdt_tile, 0.0).astype(BF16),
                                       jnp.where(first_head, 0.0, xdt_tile).astype(BF16)], axis=0)
            y_diag = jnp.dot(jnp.concatenate(g_both, axis=1), stacked, preferred_element_type=F32)
            y = y_diag + y_off[:, tile * LANES:(tile + 1) * LANES] * e_cum_full[:, tl] + xs[:, tl] * dskf_ref[:, tl]
            y_ref[:, tl] = y * zs_ref[0, :, tl]

    o_ref[0] = _rms(y_ref[...], nrm_ref[...]).astype(o_ref.dtype)


def _ssd(xbc, zs, small, conv_w, conv_b, dt_bias, a_log, d_skip, ssm_norm):
    b, s, _ = xbc.shape
    q = SSM_CHUNK
    pad = lambda v: jnp.pad(v.astype(F32), (0, LANES - v.shape[0])).reshape(1, LANES)
    full = lambda shape: pl.BlockSpec(shape, lambda bi, c: (0,) * len(shape))
    blk = lambda width: pl.BlockSpec((1, q, width), lambda bi, c: (bi, c, 0))
    expand = (jnp.arange(LANES)[:, None] == jnp.arange(SSM_INNER)[None, :] // SSM_HEAD_DIM).astype(F32)
    d_skip_full = jnp.repeat(d_skip.astype(F32), SSM_HEAD_DIM).reshape(1, SSM_INNER)
    group_w = SSM_HEADS_PER_GROUP * SSM_HEAD_DIM
    vmem = (2 * (_nbytes((q, SSM_CONV_DIM), xbc.dtype) + _nbytes((q, SSM_INNER), zs.dtype) + _nbytes((q, LANES), F32)
                 + _nbytes((q, SSM_INNER), BF16) + _nbytes((LANES, SSM_INNER), BF16) + _nbytes((LANES, SSM_INNER), F32))
            + _nbytes((2 * q, SSM_CONV_DIM), xbc.dtype) + _nbytes((SSM_GROUPS, SSM_STATE, group_w), F32)
            + _nbytes((q, SSM_INNER), F32) + 10 * _nbytes((q, SSM_CONV_DIM), F32))
    return pl.pallas_call(
        _ssd_kernel,
        grid=(b, s // q),
        in_specs=[blk(SSM_CONV_DIM),
                  pl.BlockSpec((1, SSM_CONV_TAIL, SSM_CONV_DIM),
                               lambda bi, c: (bi, jnp.maximum(c * (q // SSM_CONV_TAIL) - 1, 0), 0)),
                  blk(SSM_INNER), blk(LANES), full((SSM_CONV, SSM_CONV_DIM)), full((1, SSM_CONV_DIM)),
                  full((1, LANES)), full((1, LANES)), full((1, SSM_INNER)), full((1, SSM_INNER)),
                  full((LANES, SSM_INNER)), full((LANES, SSM_INNER))],
        out_specs=blk(SSM_INNER),
        out_shape=jax.ShapeDtypeStruct((b, s, SSM_INNER), BF16),
        scratch_shapes=[pltpu.VMEM((SSM_GROUPS, SSM_STATE, group_w), F32),
                        pltpu.VMEM((q, SSM_INNER), F32)],
        compiler_params=_compiler_params(("parallel", "arbitrary"), vmem),
        name="ssd",
    )(xbc, xbc, zs, small, conv_w.astype(F32), conv_b.reshape(1, -1).astype(F32), pad(dt_bias), pad(a_log), d_skip_full,
      ssm_norm.reshape(1, -1).astype(F32), expand.astype(BF16), expand)


def _t5_bucket_of(n):
    steps_per_octave = (REL_BUCKETS - T5_EXACT) / math.log2(REL_MAX_DIST / T5_EXACT)
    x = jnp.maximum(n, T5_EXACT).astype(F32) * (1.0 / T5_EXACT)
    large = T5_EXACT + jnp.floor(jnp.log2(x) * steps_per_octave).astype(I32)
    return jnp.where(n <= T5_EXACT, n, jnp.minimum(large, REL_BUCKETS - 1))


def _dsa_kernel(qmin_ref, kmax_ref, tblt_ref, qcw_ref, qiw_ref, kc_ref, ki_ref, v1_ref, wi_ref, posk_ref, posq_ref, gate_ref, o_ref,
                key_ref, hi_ref, lo_ref, s0_ref, s1_ref, bias_ref, p_ref, alpha_ref, acc_ref, ot_ref, m_ref, l_ref, tie_ref, *, n_sel, tq):
    tk = DSA_KEY_TILE
    qi = pl.program_id(1)
    n_tiles = (qi + 1) * (tq // tk)
    krow = lax.broadcasted_iota(I32, (tk, tq), 0)
    qcol = lax.broadcasted_iota(I32, (tk, tq), 1)
    w = wi_ref[0]
    posq = posq_ref[0]

    def tile_start(kt):
        return pl.multiple_of(kt * tk, tk)

    def causal_of(ks):
        return (ks + krow) <= (qi * tq + qcol)

    def head_lanes(h):
        return slice(h * tq, (h + 1) * tq)

    staged = (s0_ref, s1_ref)

    def stage(k_ref, q_ref, kt, dst_ref):
        dst_ref[...] = jnp.dot(k_ref[0, pl.ds(tile_start(kt), tk), :], q_ref[0, 0], preferred_element_type=F32)

    def tile_pairs(tile_fn):
        def pair(pi, carry):
            for half in range(2):
                tile_fn(2 * pi + half, staged[half], staged[1 - half])
            return carry
        lax.fori_loop(0, n_tiles // 2, pair, 0)

    stage(ki_ref, qiw_ref, 0, staged[0])

    def score_tile(kt, cur_ref, next_ref):
        ks = tile_start(kt)
        stage(ki_ref, qiw_ref, jnp.minimum(kt + 1, n_tiles - 1), next_ref)
        sc = jnp.zeros((tk, tq), F32)
        for h in range(IDX_HEADS):
            sc = sc + w[h:h + 1, :] * jnp.maximum(cur_ref[:, head_lanes(h)], 0.0)
        sc = jnp.where(sc == 0.0, 0.0, sc)
        bits = lax.bitcast_convert_type(sc, I32)
        key = bits ^ ((bits >> 31) & 0x7FFFFFFF)
        key = jnp.where(causal_of(ks), key, INT_MIN)
        key_ref[pl.ds(ks, tk), :] = key
        hi_ref[pl.ds(ks, tk), :] = (key >> 16).astype(I16)
        lo_ref[pl.ds(ks, tk), :] = ((key & 0xFFFF) - HALF16).astype(I16)

    tile_pairs(score_tile)

    def count16(half_ref, pred):
        def body(kt, cnt):
            return cnt + jnp.where(pred(half_ref[pl.ds(tile_start(kt), tk), :]), jnp.int16(1), jnp.int16(0))
        cnt = lax.fori_loop(0, n_tiles, body, jnp.zeros((tk, tq), I16))
        rows = 2 * SUBLANES
        parts = [cnt[r:r + rows, :] for r in range(0, tk, rows)]
        while len(parts) > 1:
            parts = [a + b for a, b in zip(parts[0::2], parts[1::2], strict=True)]
        return jnp.sum(parts[0].astype(I32), axis=0, keepdims=True)

    def nth_largest16(half_ref, rank):
        def bit_step(i, cand):
            trial = cand | lax.shift_left(jnp.int32(1), 15 - i)
            thr = (trial - HALF16).astype(I16)
            tot = count16(half_ref, lambda v: v >= thr)
            return jnp.where(tot >= rank, trial, cand)
        return lax.fori_loop(0, 16, bit_step, jnp.zeros((1, tq), I32)) - HALF16

    tau_hi = nth_largest16(hi_ref, n_sel)
    tau_hi16 = tau_hi.astype(I16)
    above_hi = count16(hi_ref, lambda v: v > tau_hi16)

    def mask_low(kt, carry):
        rows = pl.ds(tile_start(kt), tk)
        lo_ref[rows, :] = jnp.where(hi_ref[rows, :] == tau_hi16, lo_ref[rows, :], jnp.int16(-HALF16))
        return carry

    lax.fori_loop(0, n_tiles, mask_low, 0)
    tau_lo = nth_largest16(lo_ref, n_sel - above_hi)
    tau_lo16 = tau_lo.astype(I16)
    tau = lax.shift_left(tau_hi, 16) | (tau_lo + HALF16)
    n_gt = above_hi + count16(lo_ref, lambda v: v > tau_lo16)
    n_ge = above_hi + count16(lo_ref, lambda v: v >= tau_lo16)
    need = n_sel - n_gt

    m_ref[...] = jnp.full(m_ref.shape, NEG_BIG, F32)
    l_ref[...] = jnp.zeros(l_ref.shape, F32)
    acc_ref[...] = jnp.zeros(acc_ref.shape, F32)
    tie_ref[...] = jnp.zeros(tie_ref.shape, I32)
    lower_incl = jnp.where(lax.broadcasted_iota(I32, (tk, tk), 0) >= lax.broadcasted_iota(I32, (tk, tk), 1),
                           1.0, 0.0).astype(BF16)

    splits =jnp.where((n_ge > n_sel) & (tau != INT_MIN), 1.0, 0.0)
    any_split = jnp.max(splits) > 0.0

    def attend_tile(kt, cur_ref, next_ref, general_bias, tie_split):
        ks = tile_start(kt)
        key = key_ref[pl.ds(ks, tk), :]
        if tie_split:
            eq = key == tau
            prefix = jnp.dot(lower_incl, jnp.where(eq, 1.0, 0.0).astype(BF16), preferred_element_type=F32)
            tie_before = tie_ref[0:1, :]
            tie_rank = tie_before + prefix.astype(I32)
            sel = ((key > tau) | (eq & (tie_rank <= need))) & causal_of(ks)
            tie_ref[0:1, :] = tie_before + prefix[tk - 1:tk, :].astype(I32)
        else:
            sel = (key >= tau) & causal_of(ks)

        if general_bias:
            posk = posk_ref[0, pl.ds(ks, tk), :]
            bucket = _t5_bucket_of(jnp.maximum(posq - posk, 0))
            for h in range(DSA_HEADS):
                table = jnp.broadcast_to(tblt_ref[h:h + 1, :], (tk, LANES))
                for c in range(tq // LANES):
                    bias_ref[:, h * tq + c * LANES:h * tq + (c + 1) * LANES] = jnp.take_along_axis(
                        table, bucket[:, c * LANES:(c + 1) * LANES], axis=1)
        stage(kc_ref, qcw_ref, jnp.minimum(kt + 1, n_tiles - 1), next_ref)
        m_all = m_ref[0:1, :]
        for h in range(DSA_HEADS):
            hl = head_lanes(h)
            s = cur_ref[:, hl]
            if general_bias:
                s = s + bias_ref[:, hl]
            s = jnp.where(sel, s, NEG_BIG)
            m_old = m_all[:, hl]
            m_new = jnp.maximum(m_old, jnp.max(s, axis=0, keepdims=True))
            m_ref[0:1, hl] = m_new
            alpha_ref[0:1, hl] = jnp.exp2(m_old - m_new)
            p_ref[:, hl] = jnp.exp2(s - m_new).astype(BF16)
        pv = jnp.dot(v1_ref[0, :, pl.ds(ks, tk)], p_ref[...], preferred_element_type=F32)
        alpha = alpha_ref[0:1, :]
        acc_ref[...] = alpha * acc_ref[...] + pv[0:DSA_HEAD_DIM, :]
        l_ref[0:1, :] = alpha * l_ref[0:1, :] + pv[DSA_HEAD_DIM:DSA_HEAD_DIM + 1, :]

    bi = pl.program_id(0)
    stage(kc_ref, qcw_ref, 0, staged[0])

    def attend(kt, cur_ref, next_ref):
        far = (qmin_ref[bi, qi] - kmax_ref[bi, kt]) >= T5_FAR
        fast = jnp.logical_and(far, jnp.logical_not(any_split))
        plain = jnp.logical_and(jnp.logical_not(far), jnp.logical_not(any_split))

        @pl.when(fast)
        def _():
            attend_tile(kt, cur_ref, next_ref, general_bias=False, tie_split=False)

        @pl.when(plain)
        def _():
            attend_tile(kt, cur_ref, next_ref, general_bias=True, tie_split=False)

        @pl.when(any_split)
        def _():
            attend_tile(kt, cur_ref, next_ref, general_bias=True, tie_split=True)

    tile_pairs(attend)

    for h in range(DSA_HEADS):
        hl = head_lanes(h)
        ot_ref[h * DSA_HEAD_DIM:(h + 1) * DSA_HEAD_DIM, :] = acc_ref[:, hl] / l_ref[0:1, hl]
    o_ref[0] = (ot_ref[...].T * gate_ref[0]).astype(o_ref.dtype)


def _dsa(rel_bias, qcw, qiw, kc, ki, v1, wi, pos_col, pos_row, gate, tq):
    b, s, _ = kc.shape
    n_sel = min(TOPK_MAX, s // 4)
    vw_rows = v1.shape[1]
    hw = DSA_HEADS * tq
    tbl = rel_bias.astype(F32) * LOG2E
    tbl = tbl - tbl[REL_BUCKETS - 1:REL_BUCKETS, :]
    tbl_t = jnp.pad(tbl.T, ((0, 0), (0, LANES - REL_BUCKETS)))
    q_min = jnp.min(pos_row.reshape(b, s // tq, tq), axis=-1)
    k_max = jnp.max(pos_row.reshape(b, s // DSA_KEY_TILE, DSA_KEY_TILE), axis=-1)
    tk = DSA_KEY_TILE
    vmem = (2 * (2 * _nbytes((DSA_HEAD_DIM, hw), BF16) + 2 * _nbytes((s, LANES), BF16) + _nbytes((vw_rows, s), F32)
                 + _nbytes((vw_rows, tq), F32) + _nbytes((s, LANES), I32) + _nbytes((tq, DSA_WIDTH), F32)
                 + _nbytes((tq, DSA_WIDTH), BF16)) + _nbytes((s, tq), I32) + 2 * _nbytes((tk, hw), F32)
            + _nbytes((tk, hw), BF16) + _nbytes((DSA_HEAD_DIM, hw), F32) + _nbytes((DSA_WIDTH, tq), F32)
            + 32 * _nbytes((tk, tq), F32))
    return pl.pallas_call(
        functools.partial(_dsa_kernel, n_sel=n_sel, tq=tq),
        grid=(b, s // tq),
        in_specs=[pl.BlockSpec(memory_space=pltpu.SMEM), pl.BlockSpec(memory_space=pltpu.SMEM),
                  pl.BlockSpec((DSA_HEADS, LANES), lambda bi, i: (0, 0)),
                  pl.BlockSpec((1, 1, DSA_HEAD_DIM, hw), lambda bi, i: (bi, i, 0, 0)),
                  pl.BlockSpec((1, 1, IDX_DIM, hw), lambda bi, i: (bi, i, 0, 0)),
                  pl.BlockSpec((1, s, DSA_HEAD_DIM), lambda bi, i: (bi, 0, 0)),
                  pl.BlockSpec((1, s, IDX_DIM), lambda bi, i: (bi, 0, 0)),
                  pl.BlockSpec((1, vw_rows, s), lambda bi, i: (bi, 0, 0)),
                  pl.BlockSpec((1, IDX_HEADS, tq), lambda bi, i: (bi, 0, i)),
                  pl.BlockSpec((1, s, 1), lambda bi, i: (bi, 0, 0)),
                  pl.BlockSpec((1, 1, tq), lambda bi, i: (bi, 0, i)),
                  pl.BlockSpec((1, tq, DSA_WIDTH), lambda bi, i: (bi, i, 0))],
        out_specs=pl.BlockSpec((1, tq, DSA_WIDTH), lambda bi, i: (bi, i, 0)),
        out_shape=jax.ShapeDtypeStruct((b, s, DSA_WIDTH), BF16),
        scratch_shapes=[pltpu.VMEM((s, tq), I32),
                        pltpu.VMEM((s, tq), I16), pltpu.VMEM((s, tq), I16),
                        pltpu.VMEM((tk, hw), F32), pltpu.VMEM((tk, hw), F32),
                        pltpu.VMEM((tk, hw), F32),
                        pltpu.VMEM((tk, hw), BF16),
                        pltpu.VMEM((SUBLANES, hw), F32),
                        pltpu.VMEM((DSA_HEAD_DIM, hw), F32),
                        pltpu.VMEM((DSA_WIDTH, tq), F32),
                        pltpu.VMEM((SUBLANES, hw), F32), pltpu.VMEM((SUBLANES, hw), F32),
                        pltpu.VMEM((SUBLANES, tq), I32)],
        compiler_params=_compiler_params(("parallel", "arbitrary"), vmem),
        name="dsa",
    )(q_min, k_max, tbl_t, qcw, qiw, kc, ki, v1, wi, pos_col, pos_row, gate)


def _merge_kernel(oa_ref, ob_ref, oc_ref, g_ref, x_ref, p_ref, wa_ref, wb_ref, wc_ref, wo_ref, wpg_ref, wple_ref,
                  gn_ref, xo_ref, ho_ref):
    d = D_MODEL
    ya = jnp.dot(oa_ref[...], wa_ref[...], preferred_element_type=F32)
    yb = jnp.dot(ob_ref[...], wb_ref[...], preferred_element_type=F32)
    yc = jnp.dot(oc_ref[...], wc_ref[...], preferred_element_type=F32)
    merged = g_ref[:, 0:d] * ya + g_ref[:, d:2 * d] * yb + g_ref[:, 2 * d:3 * d] * yc
    x1 = x_ref[...] + jnp.dot(merged.astype(BF16), wo_ref[...], preferred_element_type=F32)
    ple_gate = _sigmoid(jnp.dot(x1.astype(BF16), wpg_ref[...], preferred_element_type=F32))
    x2 = x1 + ple_gate * jnp.dot(p_ref[...].astype(BF16), wple_ref[...], preferred_element_type=F32)
    xo_ref[...] = x2
    ho_ref[...] = _rms(x2, gn_ref[...]).astype(ho_ref.dtype)


def _merge(oa, ob, oc, gates, x, p, wa, wb, wc, wo, wpg, wple, g_next, h_dtype, tm=256):
    t, d = x.shape
    full = lambda a: pl.BlockSpec(a.shape, lambda i: (0, 0))
    row = lambda a: pl.BlockSpec((tm, a.shape[1]), lambda i: (i, 0))
    weights = (wa, wb, wc, wo, wpg, wple)
    acts = (oa, ob, oc, gates, x, p)
    vmem = (2 * (sum(_nbytes(a.shape, a.dtype) for a in weights) + sum(_nbytes((tm, a.shape[1]), a.dtype) for a in acts)
                 + 2 * _nbytes((tm, d), F32)) + 8 * _nbytes((tm, d), F32))
    return pl.pallas_call(
        _merge_kernel,
        grid=(t // tm,),
        in_specs=[row(a) for a in acts] + [full(a) for a in weights] + [pl.BlockSpec((1, d), lambda i: (0, 0))],
        out_specs=[pl.BlockSpec((tm, d), lambda i: (i, 0)), pl.BlockSpec((tm, d), lambda i: (i, 0))],
        out_shape=[jax.ShapeDtypeStruct((t, d), F32), jax.ShapeDtypeStruct((t, d), h_dtype)],
        compiler_params=_compiler_params(("parallel",), vmem),
        name="merge",
    )(*acts, *weights, g_next.reshape(1, d))


def _rotate_half_cols(w):
    half = w.shape[-1] // 2
    return jnp.concatenate([-w[..., half:], w[..., :half]], axis=-1)


def _prep_weights(w_in, w_uq, w_ukv):
    depth = w_in.shape[0]
    seg = lambda name: w_in[:, :, SPLIT[name][0]:SPLIT[name][1]]
    z = lambda *shape: jnp.zeros((depth,) + shape, w_in.dtype)
    d = D_MODEL
    kr = seg("k_rope")
    rope_lo = MLA_NOPE
    rope_pad = MLA_QK_PAD - MLA_NOPE - MLA_ROPE
    w = {}
    w["mla_in"] = jnp.concatenate(
        [seg("c_q"), seg("c_kv"), z(d, rope_lo), kr, z(d, rope_pad), z(d, rope_lo), _rotate_half_cols(kr), z(d, rope_pad)], axis=-1)
    w["gate_a"] = seg("gate_a")
    w["z"] = seg("z")
    w["xbc"] = seg("xbc")
    w["small"] = jnp.concatenate([seg("dt"), z(d, LANES - SSM_HEADS)], axis=-1)
    w["k_c"] = seg("k_c")
    w["k_idx"] = seg("k_idx")
    w["gate_c"] = seg("gate_c")
    w["merge"] = seg("merge")
    w["q_c_t"] = jnp.swapaxes(seg("q_c") * (DSA_HEAD_DIM ** -0.5 * LOG2E), 1, 2)
    w["q_idx_t"] = jnp.swapaxes(seg("q_idx"), 1, 2)
    w["vw_t"] = jnp.swapaxes(jnp.concatenate([seg("v_c"), seg("w_idx")], axis=-1), 1, 2)
    w = {
        "proj": jnp.concatenate([w[seg[0]] for seg in PROJ_SEGMENTS], axis=-1).astype(BF16),
        "proj_t": jnp.concatenate([w["q_c_t"], w["q_idx_t"], w["vw_t"]], axis=1).astype(BF16),
    }

    uq =w_uq.reshape(depth, MLA_Q_LORA, MLA_HEADS, MLA_NOPE + MLA_ROPE)
    nope, rope = uq[..., :MLA_NOPE], uq[..., MLA_NOPE:]
    zq = lambda width: jnp.zeros((depth, MLA_Q_LORA, MLA_HEADS, width), w_uq.dtype)
    hq = MLA_HEADS * MLA_QK_PAD
    uq1 = jnp.concatenate([nope, rope, zq(rope_pad)], axis=-1).reshape(depth, MLA_Q_LORA, hq)
    uq2 = jnp.concatenate([zq(MLA_NOPE), _rotate_half_cols(rope), zq(rope_pad)], axis=-1).reshape(depth, MLA_Q_LORA, hq)
    w["uq1_t"] = jnp.swapaxes(uq1, 1, 2).astype(BF16)
    w["uq2_t"] = jnp.swapaxes(uq2, 1, 2).astype(BF16)
    ukv = w_ukv.reshape(depth, MLA_KV_LORA, MLA_HEADS, MLA_NOPE + MLA_V)
    zk = jnp.zeros((depth, MLA_KV_LORA, MLA_HEADS, MLA_QK_PAD - MLA_NOPE), w_ukv.dtype)
    w["uk"] = jnp.concatenate([ukv[..., :MLA_NOPE], zk], axis=-1).reshape(depth, MLA_KV_LORA, hq).astype(BF16)
    w["uv_t"] = jnp.swapaxes(ukv[..., MLA_NOPE:].reshape(depth, MLA_KV_LORA, MLA_WIDTH), 1, 2).astype(BF16)
    return w


def _rope_multipliers(positions):
    b, s = positions.shape
    inv_freq = 1.0 / (ROPE_THETA ** (jnp.arange(0, MLA_ROPE, 2, dtype=F32) / MLA_ROPE))
    ang = positions.astype(F32)[..., None] * inv_freq
    cos, sin = jnp.cos(ang), jnp.sin(ang)
    pad = MLA_QK_PAD - MLA_NOPE - MLA_ROPE
    m1 = jnp.concatenate([jnp.ones((b, s, MLA_NOPE), F32), cos, cos, jnp.zeros((b, s, pad), F32)], axis=-1)
    m2 = jnp.concatenate([jnp.zeros((b, s, MLA_NOPE), F32), sin, sin, jnp.zeros((b, s, pad), F32)], axis=-1)
    return m1, m2, jnp.swapaxes(m1, 1, 2), jnp.swapaxes(m2, 1, 2)


def kernel(x, p, positions, norm_g, w_in, mla_q_norm, w_uq, mla_kv_norm, w_ukv, conv_w, conv_b, dt_bias, a_log, d_skip, ssm_norm, w_br_a, w_br_b, w_br_c, w_out, rel_bias, w_ple, w_ple_gate, final_norm):
    b, s, d = x.shape
    depth = w_in.shape[0]
    t = b * s
    w = _prep_weights(w_in, w_uq, w_ukv)
    m1, m2, m1t, m2t = _rope_multipliers(positions)
    pos_col = positions.astype(I32).reshape(b, s, 1)
    pos_row = positions.astype(I32).reshape(b, 1, s)
    wbr_a, wbr_b, wbr_c = w_br_a.astype(BF16), w_br_b.astype(BF16), w_br_c.astype(BF16)
    wo, wpg, wple = w_out.astype(BF16), w_ple_gate.astype(BF16), w_ple.astype(BF16)

    x2d = x.reshape(t, d)
    h = _norm(x2d, norm_g[0], BF16)
    for i in range(depth):
        pr = _proj_all(h.reshape(b, s, d), w["proj"][i], w["proj_t"][i], DSA_Q_BLOCK)

        qt, k, vt = _mla_prep(pr["mla_in"], m1, m2, m1t, m2t, mla_q_norm[i], mla_kv_norm[i],
                              w["uq1_t"][i], w["uq2_t"][i], w["uk"][i], w["uv_t"][i])
        o_a = _mla_attn(qt, k, vt, pr["gate_a"])
        o_b = _ssd(pr["xbc"], pr["z"], pr["small"], conv_w[i], conv_b[i], dt_bias[i], a_log[i], d_skip[i], ssm_norm[i])
        o_c = _dsa(rel_bias, pr["q_c_wide"], pr["q_idx_wide"], pr["k_c"], pr["k_idx"], pr["v_t1"], pr["w_idx_t"], pos_col, pos_row,
                   pr["gate_c"], DSA_Q_BLOCK)

        last = i == depth - 1
        g_next = final_norm if last else norm_g[i + 1]
        x2d, h = _merge(o_a.reshape(t, -1), o_b.reshape(t, -1), o_c.reshape(t, -1), pr["merge"].reshape(t, -1), x2d, p[i].reshape(t, -1),
                        wbr_a[i], wbr_b[i], wbr_c[i], wo[i], wpg[i], wple[i], g_next, F32 if last else BF16)
    return h.reshape(b, s, d)
```

```python
import functools
import math

import jax
import jax.numpy as jnp
from jax import lax
from jax.experimental import pallas as pl
from jax.experimental.pallas import tpu as pltpu

F32 = jnp.float32
BF16 = jnp.bfloat16
I32 = jnp.int32
I16 = jnp.int16
HALF16 = 1 << 15

D_MODEL = 1024
PLE_DIM = 256
NORM_EPS = 1e-6

MLA_HEADS = 8
MLA_NOPE = 64
MLA_ROPE = 32
MLA_V = 64
MLA_Q_LORA = 384
MLA_KV_LORA = 256
MLA_WIDTH = MLA_HEADS * MLA_V
ROPE_THETA = 10000.0
MLA_QK_PAD = 128
LOG2E = math.log2(math.e)
MLA_Q_SCALE = (MLA_NOPE + MLA_ROPE) ** -0.5 * LOG2E

SSM_HEADS = 16
SSM_HEAD_DIM = 64
SSM_INNER = SSM_HEADS * SSM_HEAD_DIM
SSM_GROUPS = 2
SSM_STATE = 128
SSM_CONV = 4
SSM_CHUNK = 128
SSM_CONV_DIM = SSM_INNER + 2 * SSM_GROUPS * SSM_STATE
SSM_HEADS_PER_GROUP = SSM_HEADS // SSM_GROUPS
ONES_ROWS = 16
MLA_V1_ROWS = MLA_V + ONES_ROWS
SSM_CONV_TAIL = 16

DSA_HEADS = 8
DSA_HEAD_DIM = 64
DSA_WIDTH = DSA_HEADS * DSA_HEAD_DIM
IDX_HEADS = 8
IDX_DIM = 64
TOPK_MAX = 256
DSA_KEY_TILE = 128
DSA_Q_BLOCK = 256

REL_BUCKETS = 32
REL_MAX_DIST = 128
N_BRANCHES = 3

LANES = 128
SUBLANES = 8
VMEM_LIMIT_CAP = 56 * 1024 * 1024
INT_MIN = -(2 ** 31)
NEG_BIG = -1e30

SPLIT_SIZES = (
    MLA_Q_LORA, MLA_KV_LORA, MLA_ROPE, MLA_WIDTH, SSM_INNER, SSM_CONV_DIM, SSM_HEADS, DSA_WIDTH,
    DSA_HEAD_DIM, DSA_HEAD_DIM, IDX_HEADS * IDX_DIM, IDX_DIM, IDX_HEADS, DSA_WIDTH,
    N_BRANCHES * D_MODEL,
)
SPLIT_NAMES = ("c_q", "c_kv", "k_rope", "gate_a", "z", "xbc", "dt", "q_c", "k_c", "v_c", "q_idx",
               "k_idx", "w_idx", "gate_c", "merge")


def _split_bounds():
    out, off = {}, 0
    for name, size in zip(SPLIT_NAMES, SPLIT_SIZES, strict=True):
        out[name] = (off, off + size)
        off += size
    return out


SPLIT = _split_bounds()


def _t5_large_thresholds():
    exact = REL_BUCKETS // 2
    thr = []
    for j in range(1, REL_BUCKETS - exact):
        thr.append(int(math.ceil(exact * (REL_MAX_DIST / exact) ** (j / (REL_BUCKETS - exact)) - 1e-9)))
    return tuple(thr)


T5_EXACT = REL_BUCKETS // 2
T5_LARGE_THR = _t5_large_thresholds()
T5_FAR = T5_LARGE_THR[-1]


def _compiler_params(semantics, vmem_bytes):
    limit = int(min(VMEM_LIMIT_CAP, max(32 * 1024 * 1024, vmem_bytes)))
    return pltpu.CompilerParams(dimension_semantics=semantics, vmem_limit_bytes=limit)


def _nbytes(shape, dtype):
    return math.prod(shape) * jnp.dtype(dtype).itemsize


def _sigmoid(x):
    return 1.0 / (1.0 + jnp.exp(-x))


def _rms(x, g):
    return x * lax.rsqrt(jnp.mean(x * x, axis=-1, keepdims=True) + NORM_EPS) * g


def _norm_kernel(x_ref, g_ref, o_ref):
    o_ref[...] = _rms(x_ref[...], g_ref[...]).astype(o_ref.dtype)


def _norm(x2d, g, out_dtype, tm=512):
    t, d = x2d.shape
    return pl.pallas_call(
        _norm_kernel,
        grid=(t // tm,),
        in_specs=[pl.BlockSpec((tm, d), lambda i: (i, 0)), pl.BlockSpec((1, d), lambda i: (0, 0))],
        out_specs=pl.BlockSpec((tm, d), lambda i: (i, 0)),
        out_shape=jax.ShapeDtypeStruct((t, d), out_dtype),
        compiler_params=_compiler_params(("parallel",), 4 * _nbytes((tm, d), F32)),
        name="rms_norm",
    )(x2d, g.reshape(1, d))


MLA_IN_WIDTH = MLA_Q_LORA + MLA_KV_LORA + 2 * LANES
PROJ_SEGMENTS = (
    ("small", LANES, None, F32),
    ("gate_a", MLA_WIDTH, "silu", BF16),
    ("gate_c", DSA_WIDTH, "silu", BF16),
    ("z", SSM_INNER, "silu", BF16),
    ("xbc", SSM_CONV_DIM, None, BF16),
    ("merge", N_BRANCHES * D_MODEL, "sigmoid", BF16),
    ("k_c", DSA_HEAD_DIM, None, BF16),
    ("k_idx", IDX_DIM, None, BF16),
)
PROJ_CHUNK = 512
PROJ_T_ROWS = (DSA_WIDTH, IDX_HEADS * IDX_DIM, DSA_HEAD_DIM + IDX_HEADS)


def _proj_all_kernel(h_ref, w_ref, wt_ref, m1_ref, m2_ref, m1t_ref, m2t_ref, qn_ref, kvn_ref, wq1t_ref, wq2t_ref, wk_ref,
                     wvt_ref, *out_refs):
    h = h_ref[0]
    n_seg = len(PROJ_SEGMENTS)
    a = jnp.dot(h, w_ref[0, :, 0:MLA_IN_WIDTH], preferred_element_type=F32)
    _mla_prep(a, m1_ref, m2_ref, m1t_ref, m2t_ref, qn_ref, kvn_ref, wq1t_ref, wq2t_ref, wk_ref, wvt_ref, *out_refs[n_seg + 4:])
    off = MLA_IN_WIDTH
    for (_, width, act, _), o_ref in zip(PROJ_SEGMENTS, out_refs[:n_seg], strict=True):
        for c in range(0, width, PROJ_CHUNK):
            cw = min(PROJ_CHUNK, width - c)
            y = jnp.dot(h, w_ref[0, :, off + c:off + c + cw], preferred_element_type=F32)
            if act == "silu":
                y = y * _sigmoid(y)
            elif act == "sigmoid":
                y = _sigmoid(y)
            o_ref[0, :, c:c + cw] = y.astype(o_ref.dtype)
        off += width
    qcw_ref, qiw_ref, v1_ref, wi_ref = out_refs[n_seg:n_seg + 4]
    nt = (((1,), (1,)), ((), ()))
    tq = h.shape[0]
    row = 0
    for o_ref, heads, dim in ((qcw_ref, DSA_HEADS, DSA_HEAD_DIM), (qiw_ref, IDX_HEADS, IDX_DIM)):
        y = lax.dot_general(wt_ref[0, row:row + heads * dim, :], h, nt, preferred_element_type=F32)
        for hd in range(heads):
            o_ref[0, 0, :, hd * tq:(hd + 1) * tq] = y[hd * dim:(hd + 1) * dim, :].astype(o_ref.dtype)
        row += heads * dim
    vw = lax.dot_general(wt_ref[0, row:row + PROJ_T_ROWS[2], :], h, nt, preferred_element_type=F32)
    v1_ref[0, 0:DSA_HEAD_DIM, :] = vw[0:DSA_HEAD_DIM, :].astype(v1_ref.dtype)
    v1_ref[0, DSA_HEAD_DIM:, :] = jnp.ones((ONES_ROWS, tq), v1_ref.dtype)
    wi_ref[0] = vw[DSA_HEAD_DIM:, :]


def _proj_all(h3d, layer, w, wt, rope, qn, kvn, wq1t, wq2t, wk, wvt, tq):
    b, s, k = h3d.shape
    n_all = MLA_IN_WIDTH + sum(seg[1] for seg in PROJ_SEGMENTS)
    hq = MLA_HEADS * MLA_QK_PAD
    v1_rows = DSA_HEAD_DIM + ONES_ROWS
    out_shapes = [jax.ShapeDtypeStruct((b, s, width), dt) for _, width, _, dt in PROJ_SEGMENTS]
    out_specs = [pl.BlockSpec((1, tq, width), lambda bi, i: (bi, i, 0)) for _, width, _, _ in PROJ_SEGMENTS]
    for heads, dim in ((DSA_HEADS, DSA_HEAD_DIM), (IDX_HEADS, IDX_DIM)):
        out_shapes.append(jax.ShapeDtypeStruct((b, s // tq, dim, heads * tq), BF16))
        out_specs.append(pl.BlockSpec((1, 1, dim, heads * tq), lambda bi, i: (bi, i, 0, 0)))
    for rows, dt in ((v1_rows, BF16), (IDX_HEADS, F32)):
        out_shapes.append(jax.ShapeDtypeStruct((b, rows, s), dt))
        out_specs.append(pl.BlockSpec((1, rows, tq), lambda bi, i: (bi, 0, i)))
    for rows in (hq, None, MLA_HEADS * MLA_V1_ROWS):
        if rows is None:
            out_shapes.append(jax.ShapeDtypeStruct((b, s, hq), BF16))
            out_specs.append(pl.BlockSpec((1, tq, hq), lambda bi, i: (bi, i, 0)))
        else:
            out_shapes.append(jax.ShapeDtypeStruct((b, rows, s), BF16))
            out_specs.append(pl.BlockSpec((1, rows, tq), lambda bi, i: (bi, 0, i)))
    resident = pl.Buffered(1)
    weights = (w, wt, qn, kvn, wq1t, wq2t, wk, wvt)
    layer_spec = lambda a: pl.BlockSpec((1,) + a.shape[1:], lambda bi, i: (layer,) + (0,) * (a.ndim - 1), pipeline_mode=resident)
    m1, m2, m1t, m2t = rope
    vmem = (sum(_nbytes(a.shape[1:], a.dtype) for a in weights) + 2 * _nbytes((tq, k), BF16) + 8 * _nbytes((tq, LANES), F32)
            + 2 * sum(_nbytes((tq, width), dt) for _, width, _, dt in PROJ_SEGMENTS)
            + 2 * (2 * _nbytes((DSA_HEAD_DIM, DSA_HEADS * tq), BF16) + _nbytes((v1_rows + IDX_HEADS, tq), F32))
            + 2 * (2 * _nbytes((tq, hq), BF16) + _nbytes((MLA_HEADS * MLA_V1_ROWS, tq), BF16))
            + 8 * _nbytes((tq, PROJ_CHUNK), F32) + 6 * _nbytes((tq, hq), F32))
    outs = pl.pallas_call(
        _proj_all_kernel,
        grid=(b, s // tq),
        in_specs=[pl.BlockSpec((1, tq, k), lambda bi, i: (bi, i, 0)), layer_spec(w), layer_spec(wt),
                  pl.BlockSpec((1, tq, LANES), lambda bi, i: (bi, i, 0)), pl.BlockSpec((1, tq, LANES), lambda bi, i: (bi, i, 0)),
                  pl.BlockSpec((1, LANES, tq), lambda bi, i: (bi, 0, i)), pl.BlockSpec((1, LANES, tq), lambda bi, i: (bi, 0, i))]
        + [layer_spec(a) for a in weights[2:]],
        out_specs=out_specs,
        out_shape=out_shapes,
        compiler_params=_compiler_params(("parallel", "parallel"), vmem),
        name="proj_all",
    )(h3d, w, wt, m1, m2, m1t, m2t, *weights[2:])
    names = [seg[0] for seg in PROJ_SEGMENTS] + ["q_c_wide", "q_idx_wide", "v_t1", "w_idx_t", "mla_q_t", "mla_k", "mla_v_t1"]
    return dict(zip(names, outs, strict=True))


def _mla_prep(a, m1_ref, m2_ref, m1t_ref, m2t_ref, qn_ref, kvn_ref, wq1t_ref, wq2t_ref, wk_ref, wvt_ref, qt_ref, k_ref, vt_ref):
    c_q = a[:, :MLA_Q_LORA]
    c_kv = a[:, MLA_Q_LORA:MLA_Q_LORA + MLA_KV_LORA]
    kr1 = a[:, MLA_Q_LORA + MLA_KV_LORA:MLA_Q_LORA + MLA_KV_LORA + LANES]
    kr2 = a[:, MLA_Q_LORA + MLA_KV_LORA + LANES:]
    cqn = _rms(c_q, qn_ref[0]).astype(BF16)
    ckvn = _rms(c_kv, kvn_ref[0]).astype(BF16)
    nt = (((1,), (1,)), ((), ()))
    qa_t = lax.dot_general(wq1t_ref[0], cqn, nt, preferred_element_type=F32)
    qb_t = lax.dot_general(wq2t_ref[0], cqn, nt, preferred_element_type=F32)
    kn = jnp.dot(ckvn, wk_ref[0], preferred_element_type=F32)
    kr = kr1 * m1_ref[0] + kr2 * m2_ref[0]
    m1t = m1t_ref[0] * MLA_Q_SCALE
    m2t = m2t_ref[0] * MLA_Q_SCALE
    for h in range(MLA_HEADS):
        sl = slice(h * MLA_QK_PAD, (h + 1) * MLA_QK_PAD)
        qt_ref[0, sl, :] = (qa_t[sl, :] * m1t + qb_t[sl, :] * m2t).astype(qt_ref.dtype)
        k_ref[0, :, sl] = (kn[:, sl] + kr).astype(k_ref.dtype)
    v_t = lax.dot_general(wvt_ref[0], ckvn, nt, preferred_element_type=F32)
    ones = jnp.ones((ONES_ROWS, v_t.shape[1]), vt_ref.dtype)
    for h in range(MLA_HEADS):
        base = h * MLA_V1_ROWS
        vt_ref[0, base:base + MLA_V, :] = v_t[h * MLA_V:(h + 1) * MLA_V, :].astype(vt_ref.dtype)
        vt_ref[0, base + MLA_V:base + MLA_V1_ROWS, :] = ones


def _mla_attn_kernel(qt_ref, k_ref, vt_ref, g_ref, o_ref, m_ref, l_ref, acc_ref, s0_ref, s1_ref, *, tq, tk):
    assert tq == 2 * tk
    qi = pl.program_id(1)
    krow = lax.broadcasted_iota(I32, (tk, tq), 0)
    qcol = lax.broadcasted_iota(I32, (tk, tq), 1)
    tiles_per_q = tq // tk
    m_ref[...] = jnp.full(m_ref.shape, -jnp.inf, F32)
    l_ref[...] = jnp.zeros(l_ref.shape, F32)
    acc_ref[...] = jnp.zeros(acc_ref.shape, F32)

    staged = (s0_ref, s1_ref)
    n_tiles = (qi + 1) * tiles_per_q

    def head_lanes(h):
        return slice(h * tq, (h + 1) * tq)

    def stage(j, dst_ref):
        ks = pl.multiple_of(j * tk, tk)
        for h in range(MLA_HEADS):
            k = k_ref[0, pl.ds(ks, tk), h * MLA_QK_PAD:(h + 1) * MLA_QK_PAD]
            dst_ref[:, head_lanes(h)] = jnp.dot(k, qt_ref[0, h * MLA_QK_PAD:(h + 1) * MLA_QK_PAD, :],
                                                preferred_element_type=F32)

    def tile(j, cur_ref, next_ref, diagonal):
        ks = pl.multiple_of(j * tk, tk)
        stage(jnp.minimum(j + 1, n_tiles - 1), next_ref)
        m_all = m_ref[...]
        l_all = l_ref[...]
        if diagonal:
            causal = ks + krow <= qi * tq + qcol
        for h in range(MLA_HEADS):
            s = cur_ref[:, head_lanes(h)]
            if diagonal:
                s = jnp.where(causal, s, -jnp.inf)
            m_old = m_all[h:h + 1, :]
            m_new = jnp.maximum(m_old, jnp.max(s, axis=0, keepdims=True))
            alpha = jnp.exp2(m_old - m_new)
            p = jnp.exp2(s - m_new).astype(BF16)
            m_ref[h:h + 1, :] = m_new
            pv = jnp.dot(vt_ref[0, h * MLA_V1_ROWS:(h + 1) * MLA_V1_ROWS, pl.ds(ks, tk)], p, preferred_element_type=F32)
            hs = slice(h * MLA_V, (h + 1) * MLA_V)
            acc_ref[hs, :] = alpha * acc_ref[hs, :] + pv[0:MLA_V, :]
            l_ref[h:h + 1, :] = alpha * l_all[h:h + 1, :] + pv[MLA_V:MLA_V + 1, :]

    stage(0, staged[0])

    def full_pair(pi, carry):
        for half in range(2):
            tile(2 * pi + half, staged[half], staged[1 - half], diagonal=False)
        return carry

    lax.fori_loop(0, qi, full_pair, 0)
    for half in range(2):
        tile(2 * qi + half, staged[half], staged[1 - half], diagonal=True)
    for h in range(MLA_HEADS):
        hs = slice(h * MLA_V, (h + 1) * MLA_V)
        acc_ref[hs, :] = acc_ref[hs, :] / l_ref[h:h + 1, :]
    o_ref[0] = (acc_ref[...].T * g_ref[0]).astype(o_ref.dtype)


def _mla_attn(qt, k, vt, gate, tq=256, tk=128):
    b, s, hq = k.shape
    vmem = 2 * (_nbytes((hq, tq), BF16) + _nbytes((s, hq), BF16) + _nbytes((MLA_WIDTH, s), BF16)
                + _nbytes((tq, MLA_WIDTH), F32) + _nbytes((tq, MLA_WIDTH), BF16)) + _nbytes((MLA_WIDTH, tq), F32) + 16 * _nbytes((tk, tq), F32)
    return pl.pallas_call(
        functools.partial(_mla_attn_kernel, tq=tq, tk=tk),
        grid=(b, s // tq),
        in_specs=[pl.BlockSpec((1, hq, tq), lambda bi, i: (bi, 0, i)),
                  pl.BlockSpec((1, s, hq), lambda bi, i: (bi, 0, 0)),
                  pl.BlockSpec((1, MLA_HEADS * MLA_V1_ROWS, s), lambda bi, i: (bi, 0, 0)),
                  pl.BlockSpec((1, tq, MLA_WIDTH), lambda bi, i: (bi, i, 0))],
        out_specs=pl.BlockSpec((1, tq, MLA_WIDTH), lambda bi, i: (bi, i, 0)),
        out_shape=jax.ShapeDtypeStruct((b, s, MLA_WIDTH), BF16),
        scratch_shapes=[pltpu.VMEM((MLA_HEADS, tq), F32), pltpu.VMEM((MLA_HEADS, tq), F32),
                        pltpu.VMEM((MLA_WIDTH, tq), F32),
                        pltpu.VMEM((tk, MLA_HEADS * tq), F32), pltpu.VMEM((tk, MLA_HEADS * tq), F32)],
        compiler_params=_compiler_params(("parallel", "arbitrary"), vmem),
        name="mla_attn",
    )(qt, k, vt, gate)


def _ssd_kernel(xbc_ref, xprev_ref, zs_ref, sm_ref, cw_ref, cb_ref, dtb_ref, alog_ref, dskf_ref, nrm_ref, exp16_ref, exp32_ref,
                o_ref, state_ref, y_ref):
    c = pl.program_id(1)
    q = SSM_CHUNK

    @pl.when(c == 0)
    def _():
        state_ref[...] = jnp.zeros(state_ref.shape, F32)

    x_in = xbc_ref[0]
    tail = xprev_ref.shape[1]
    prev = jnp.where(c > 0, xprev_ref[0].astype(F32), 0.0).astype(x_in.dtype)
    win = jnp.concatenate([prev, x_in], axis=0)
    x = x_in.astype(F32)
    wrow = lax.broadcasted_iota(I32, (q, tail + q), 0)
    wcol = lax.broadcasted_iota(I32, (q, tail + q), 1)
    acc = cb_ref[...] + cw_ref[SSM_CONV - 1:SSM_CONV, :] * x
    for j in range(1, SSM_CONV):
        pick = jnp.where(wcol == wrow + (tail - j), 1.0, 0.0).astype(win.dtype)
        acc = acc + cw_ref[SSM_CONV - 1 - j:SSM_CONV - j, :] * jnp.dot(pick, win, preferred_element_type=F32)
    xc = acc * _sigmoid(acc)

    xs = xc[:, :SSM_INNER]
    bm = xc[:, SSM_INNER:SSM_INNER + SSM_GROUPS * SSM_STATE]
    cm = xc[:, SSM_INNER + SSM_GROUPS * SSM_STATE:]

    pre = sm_ref[0] + dtb_ref[...]
    dt = jnp.maximum(pre, 0.0) + jnp.log1p(jnp.exp(-jnp.abs(pre)))
    a = -jnp.exp(alog_ref[...])
    row = lax.broadcasted_iota(I32, (q, q), 0)
    col = lax.broadcasted_iota(I32, (q, q), 1)
    lower = row >= col
    cum = jnp.dot(jnp.where(lower, 1.0, 0.0).astype(F32), dt * a, preferred_element_type=F32,
                  precision=lax.Precision.HIGHEST)
    cum_t = cum.T
    cum_last = cum[q - 1:q, :]
    per_head = jnp.concatenate([dt, jnp.exp(cum_last - cum), jnp.exp(cum)], axis=0).astype(BF16)
    spread = jnp.dot(per_head, exp16_ref[...], preferred_element_type=F32)
    dt_full, to_end_full, e_cum_full = spread[0:q], spread[q:2 * q], spread[2 * q:3 * q]
    chunk_decay_full = jnp.dot(jnp.broadcast_to(jnp.exp(cum_last), (SUBLANES, LANES)), exp32_ref[...],
                               preferred_element_type=F32, precision=lax.Precision.HIGHEST)[0:1]
    xdt = xs * dt_full
    xw = (xdt * to_end_full).astype(BF16)
    first_head = lax.broadcasted_iota(I32, (q, LANES), 1) < SSM_HEAD_DIM
    group_w = SSM_HEADS_PER_GROUP * SSM_HEAD_DIM
    heads_per_tile = LANES // SSM_HEAD_DIM

    for g in range(SSM_GROUPS):
        bg32 = bm[:, g * SSM_STATE:(g + 1) * SSM_STATE]
        bg = bg32.astype(BF16)
        cg = cm[:, g * SSM_STATE:(g + 1) * SSM_STATE].astype(BF16)
        cb = lax.dot_general(cg, bg, (((1,), (1,)), ((), ())), preferred_element_type=F32)
        gl = slice(g * group_w, (g + 1) * group_w)
        st = state_ref[g]
        y_off = jnp.dot(cg, st.astype(BF16), preferred_element_type=F32)
        state_ref[g] = st * chunk_decay_full[:, gl] + jnp.dot(bg32.T.astype(BF16), xw[:, gl], preferred_element_type=F32)
        for tile in range(group_w // LANES):
            h0 = g * SSM_HEADS_PER_GROUP + tile * heads_per_tile
            tl = slice(h0 * SSM_HEAD_DIM, h0 * SSM_HEAD_DIM + LANES)
            g_both = []
            for h in range(h0, h0 + heads_per_tile):
                diff = cum[:, h:h + 1] - cum_t[h:h + 1, :]
                decay_in = jnp.exp(jnp.where(lower, diff, -jnp.inf))
                g_both.append((cb * decay_in).astype(BF16))
            xdt_tile = xdt[:, tl]
            stacked = jnp.concatenate([jnp.where(first_head, xdt_tile, 0.0).astype(BF16),
                                       jnp.where(first_head, 0.0, xdt_tile).astype(BF16)], axis=0)
            y_diag = jnp.dot(jnp.concatenate(g_both, axis=1), stacked, preferred_element_type=F32)
            y = y_diag + y_off[:, tile * LANES:(tile + 1) * LANES] * e_cum_full[:, tl] + xs[:, tl] * dskf_ref[:, tl]
            y_ref[:, tl] = y * zs_ref[0, :, tl]

    o_ref[0] = _rms(y_ref[...], nrm_ref[...]).astype(o_ref.dtype)


def _ssd(xbc, zs, small, conv_w, conv_b, dt_bias, a_log, d_skip, ssm_norm):
    b, s, _ = xbc.shape
    q = SSM_CHUNK
    pad = lambda v: jnp.pad(v.astype(F32), (0, LANES - v.shape[0])).reshape(1, LANES)
    full = lambda shape: pl.BlockSpec(shape, lambda bi, c: (0,) * len(shape))
    blk = lambda width: pl.BlockSpec((1, q, width), lambda bi, c: (bi, c, 0))
    expand = (jnp.arange(LANES)[:, None] == jnp.arange(SSM_INNER)[None, :] // SSM_HEAD_DIM).astype(F32)
    d_skip_full = jnp.repeat(d_skip.astype(F32), SSM_HEAD_DIM).reshape(1, SSM_INNER)
    group_w = SSM_HEADS_PER_GROUP * SSM_HEAD_DIM
    vmem = (2 * (_nbytes((q, SSM_CONV_DIM), xbc.dtype) + _nbytes((q, SSM_INNER), zs.dtype) + _nbytes((q, LANES), F32)
                 + _nbytes((q, SSM_INNER), BF16) + _nbytes((LANES, SSM_INNER), BF16) + _nbytes((LANES, SSM_INNER), F32))
            + _nbytes((2 * q, SSM_CONV_DIM), xbc.dtype) + _nbytes((SSM_GROUPS, SSM_STATE, group_w), F32)
            + _nbytes((q, SSM_INNER), F32) + 10 * _nbytes((q, SSM_CONV_DIM), F32))
    return pl.pallas_call(
        _ssd_kernel,
        grid=(b, s // q),
        in_specs=[blk(SSM_CONV_DIM),
                  pl.BlockSpec((1, SSM_CONV_TAIL, SSM_CONV_DIM),
                               lambda bi, c: (bi, jnp.maximum(c * (q // SSM_CONV_TAIL) - 1, 0), 0)),
                  blk(SSM_INNER), blk(LANES), full((SSM_CONV, SSM_CONV_DIM)), full((1, SSM_CONV_DIM)),
                  full((1, LANES)), full((1, LANES)), full((1, SSM_INNER)), full((1, SSM_INNER)),
                  full((LANES, SSM_INNER)), full((LANES, SSM_INNER))],
        out_specs=blk(SSM_INNER),
        out_shape=jax.ShapeDtypeStruct((b, s, SSM_INNER), BF16),
        scratch_shapes=[pltpu.VMEM((SSM_GROUPS, SSM_STATE, group_w), F32),
                        pltpu.VMEM((q, SSM_INNER), F32)],
        compiler_params=_compiler_params(("parallel", "arbitrary"), vmem),
        name="ssd",
    )(xbc, xbc, zs, small, conv_w.astype(F32), conv_b.reshape(1, -1).astype(F32), pad(dt_bias), pad(a_log), d_skip_full,
      ssm_norm.reshape(1, -1).astype(F32), expand.astype(BF16), expand)


def _t5_bucket_of(n):
    steps_per_octave = (REL_BUCKETS - T5_EXACT) / math.log2(REL_MAX_DIST / T5_EXACT)
    x = jnp.maximum(n, T5_EXACT).astype(F32) * (1.0 / T5_EXACT)
    large = T5_EXACT + jnp.floor(jnp.log2(x) * steps_per_octave).astype(I32)
    return jnp.where(n <= T5_EXACT, n, jnp.minimum(large, REL_BUCKETS - 1))


def _dsa_kernel(qmin_ref, kmax_ref, tblt_ref, qcw_ref, qiw_ref, kc_ref, ki_ref, v1_ref, wi_ref, posk_ref, posq_ref, gate_ref, o_ref,
                key_ref, hi_ref, lo_ref, s0_ref, s1_ref, bias_ref, p_ref, alpha_ref, acc_ref, ot_ref, m_ref, l_ref, tie_ref, *, n_sel, tq):
    tk = DSA_KEY_TILE
    qi = pl.program_id(1)
    n_tiles = (qi + 1) * (tq // tk)
    krow = lax.broadcasted_iota(I32, (tk, tq), 0)
    qcol = lax.broadcasted_iota(I32, (tk, tq), 1)
    w = wi_ref[0]
    posq = posq_ref[0]

    def tile_start(kt):
        return pl.multiple_of(kt * tk, tk)

    def causal_of(ks):
        return (ks + krow) <= (qi * tq + qcol)

    def head_lanes(h):
        return slice(h * tq, (h + 1) * tq)

    staged = (s0_ref, s1_ref)

    def stage(k_ref, q_ref, kt, dst_ref):
        dst_ref[...] = jnp.dot(k_ref[0, pl.ds(tile_start(kt), tk), :], q_ref[0, 0], preferred_element_type=F32)

    def tile_pairs(tile_fn):
        def pair(pi, carry):
            for half in range(2):
                tile_fn(2 * pi + half, staged[half], staged[1 - half])
            return carry
        lax.fori_loop(0, n_tiles // 2, pair, 0)

    stage(ki_ref, qiw_ref, 0, staged[0])

    def score_tile(kt, cur_ref, next_ref):
        ks = tile_start(kt)
        stage(ki_ref, qiw_ref, jnp.minimum(kt + 1, n_tiles - 1), next_ref)
        sc = jnp.zeros((tk, tq), F32)
        for h in range(IDX_HEADS):
            sc = sc + w[h:h + 1, :] * jnp.maximum(cur_ref[:, head_lanes(h)], 0.0)
        sc = jnp.where(sc == 0.0, 0.0, sc)
        bits = lax.bitcast_convert_type(sc, I32)
        key = bits ^ ((bits >> 31) & 0x7FFFFFFF)
        key = jnp.where(causal_of(ks), key, INT_MIN)
        key_ref[pl.ds(ks, tk), :] = key
        hi_ref[pl.ds(ks, tk), :] = (key >> 16).astype(I16)
        lo_ref[pl.ds(ks, tk), :] = ((key & 0xFFFF) - HALF16).astype(I16)

    tile_pairs(score_tile)

    def count16(half_ref, pred):
        def body(kt, cnt):
            return cnt + jnp.where(pred(half_ref[pl.ds(tile_start(kt), tk), :]), jnp.int16(1), jnp.int16(0))
        cnt = lax.fori_loop(0, n_tiles, body, jnp.zeros((tk, tq), I16))
        rows = 2 * SUBLANES
        parts = [cnt[r:r + rows, :] for r in range(0, tk, rows)]
        while len(parts) > 1:
            parts = [a + b for a, b in zip(parts[0::2], parts[1::2], strict=True)]
        return jnp.sum(parts[0].astype(I32), axis=0, keepdims=True)

    def nth_largest16(half_ref, rank):
        def bit_step(i, cand):
            trial = cand | lax.shift_left(jnp.int32(1), 15 - i)
            thr = (trial - HALF16).astype(I16)
            tot = count16(half_ref, lambda v: v >= thr)
            return jnp.where(tot >= rank, trial, cand)
        return lax.fori_loop(0, 16, bit_step, jnp.zeros((1, tq), I32)) - HALF16

    tau_hi = nth_largest16(hi_ref, n_sel)
    tau_hi16 = tau_hi.astype(I16)
    above_hi = count16(hi_ref, lambda v: v > tau_hi16)

    def mask_low(kt, carry):
        rows = pl.ds(tile_start(kt), tk)
        lo_ref[rows, :] = jnp.where(hi_ref[rows, :] == tau_hi16, lo_ref[rows, :], jnp.int16(-HALF16))
        return carry

    lax.fori_loop(0, n_tiles, mask_low, 0)
    tau_lo = nth_largest16(lo_ref, n_sel - above_hi)
    tau_lo16 = tau_lo.astype(I16)
    tau = lax.shift_left(tau_hi, 16) | (tau_lo + HALF16)
    n_gt = above_hi + count16(lo_ref, lambda v: v > tau_lo16)
    n_ge = above_hi + count16(lo_ref, lambda v: v >= tau_lo16)
    need = n_sel - n_gt

    m_ref[...] = jnp.full(m_ref.shape, NEG_BIG, F32)
    l_ref[...] = jnp.zeros(l_ref.shape, F32)
    acc_ref[...] = jnp.zeros(acc_ref.shape, F32)
    tie_ref[...] = jnp.zeros(tie_ref.shape, I32)
    lower_incl = jnp.where(lax.broadcasted_iota(I32, (tk, tk), 0) >= lax.broadcasted_iota(I32, (tk, tk), 1),
                           1.0, 0.0).astype(BF16)

    splits =jnp.where((n_ge > n_sel) & (tau != INT_MIN), 1.0, 0.0)
    any_split = jnp.max(splits) > 0.0

    def attend_tile(kt, cur_ref, next_ref, general_bias, tie_split):
        ks = tile_start(kt)
        key = key_ref[pl.ds(ks, tk), :]
        if tie_split:
            eq = key == tau
            prefix = jnp.dot(lower_incl, jnp.where(eq, 1.0, 0.0).astype(BF16), preferred_element_type=F32)
            tie_before = tie_ref[0:1, :]
            tie_rank = tie_before + prefix.astype(I32)
            sel = ((key > tau) | (eq & (tie_rank <= need))) & causal_of(ks)
            tie_ref[0:1, :] = tie_before + prefix[tk - 1:tk, :].astype(I32)
        else:
            sel = (key >= tau) & causal_of(ks)

        if general_bias:
            posk = posk_ref[0, pl.ds(ks, tk), :]
            bucket = _t5_bucket_of(jnp.maximum(posq - posk, 0))
            for h in range(DSA_HEADS):
                table = jnp.broadcast_to(tblt_ref[h:h + 1, :], (tk, LANES))
                for c in range(tq // LANES):
                    bias_ref[:, h * tq + c * LANES:h * tq + (c + 1) * LANES] = jnp.take_along_axis(
                        table, bucket[:, c * LANES:(c + 1) * LANES], axis=1)
        stage(kc_ref, qcw_ref, jnp.minimum(kt + 1, n_tiles - 1), next_ref)
        m_all = m_ref[0:1, :]
        for h in range(DSA_HEADS):
            hl = head_lanes(h)
            s = cur_ref[:, hl]
            if general_bias:
                s = s + bias_ref[:, hl]
            s = jnp.where(sel, s, NEG_BIG)
            m_old = m_all[:, hl]
            m_new = jnp.maximum(m_old, jnp.max(s, axis=0, keepdims=True))
            m_ref[0:1, hl] = m_new
            alpha_ref[0:1, hl] = jnp.exp2(m_old - m_new)
            p_ref[:, hl] = jnp.exp2(s - m_new).astype(BF16)
        pv = jnp.dot(v1_ref[0, :, pl.ds(ks, tk)], p_ref[...], preferred_element_type=F32)
        alpha = alpha_ref[0:1, :]
        acc_ref[...] = alpha * acc_ref[...] + pv[0:DSA_HEAD_DIM, :]
        l_ref[0:1, :] = alpha * l_ref[0:1, :] + pv[DSA_HEAD_DIM:DSA_HEAD_DIM + 1, :]

    bi = pl.program_id(0)
    stage(kc_ref, qcw_ref, 0, staged[0])

    def attend(kt, cur_ref, next_ref):
        far = (qmin_ref[bi, qi] - kmax_ref[bi, kt]) >= T5_FAR
        fast = jnp.logical_and(far, jnp.logical_not(any_split))
        plain = jnp.logical_and(jnp.logical_not(far), jnp.logical_not(any_split))

        @pl.when(fast)
        def _():
            attend_tile(kt, cur_ref, next_ref, general_bias=False, tie_split=False)

        @pl.when(plain)
        def _():
            attend_tile(kt, cur_ref, next_ref, general_bias=True, tie_split=False)

        @pl.when(any_split)
        def _():
            attend_tile(kt, cur_ref, next_ref, general_bias=True, tie_split=True)

    tile_pairs(attend)

    for h in range(DSA_HEADS):
        hl = head_lanes(h)
        ot_ref[h * DSA_HEAD_DIM:(h + 1) * DSA_HEAD_DIM, :] = acc_ref[:, hl] / l_ref[0:1, hl]
    o_ref[0] = (ot_ref[...].T * gate_ref[0]).astype(o_ref.dtype)


def _dsa(rel_bias, qcw, qiw, kc, ki, v1, wi, pos_col, pos_row, gate, tq):
    b, s, _ = kc.shape
    n_sel = min(TOPK_MAX, s // 4)
    vw_rows = v1.shape[1]
    hw = DSA_HEADS * tq
    tbl = rel_bias.astype(F32) * LOG2E
    tbl = tbl - tbl[REL_BUCKETS - 1:REL_BUCKETS, :]
    tbl_t = jnp.pad(tbl.T, ((0, 0), (0, LANES - REL_BUCKETS)))
    q_min = jnp.min(pos_row.reshape(b, s // tq, tq), axis=-1)
    k_max = jnp.max(pos_row.reshape(b, s // DSA_KEY_TILE, DSA_KEY_TILE), axis=-1)
    tk = DSA_KEY_TILE
    vmem = (2 * (2 * _nbytes((DSA_HEAD_DIM, hw), BF16) + 2 * _nbytes((s, LANES), BF16) + _nbytes((vw_rows, s), F32)
                 + _nbytes((vw_rows, tq), F32) + _nbytes((s, LANES), I32) + _nbytes((tq, DSA_WIDTH), F32)
                 + _nbytes((tq, DSA_WIDTH), BF16)) + _nbytes((s, tq), I32) + 2 * _nbytes((tk, hw), F32)
            + _nbytes((tk, hw), BF16) + _nbytes((DSA_HEAD_DIM, hw), F32) + _nbytes((DSA_WIDTH, tq), F32)
            + 32 * _nbytes((tk, tq), F32))
    return pl.pallas_call(
        functools.partial(_dsa_kernel, n_sel=n_sel, tq=tq),
        grid=(b, s // tq),
        in_specs=[pl.BlockSpec(memory_space=pltpu.SMEM), pl.BlockSpec(memory_space=pltpu.SMEM),
                  pl.BlockSpec((DSA_HEADS, LANES), lambda bi, i: (0, 0)),
                  pl.BlockSpec((1, 1, DSA_HEAD_DIM, hw), lambda bi, i: (bi, i, 0, 0)),
                  pl.BlockSpec((1, 1, IDX_DIM, hw), lambda bi, i: (bi, i, 0, 0)),
                  pl.BlockSpec((1, s, DSA_HEAD_DIM), lambda bi, i: (bi, 0, 0)),
                  pl.BlockSpec((1, s, IDX_DIM), lambda bi, i: (bi, 0, 0)),
                  pl.BlockSpec((1, vw_rows, s), lambda bi, i: (bi, 0, 0)),
                  pl.BlockSpec((1, IDX_HEADS, tq), lambda bi, i: (bi, 0, i)),
                  pl.BlockSpec((1, s, 1), lambda bi, i: (bi, 0, 0)),
                  pl.BlockSpec((1, 1, tq), lambda bi, i: (bi, 0, i)),
                  pl.BlockSpec((1, tq, DSA_WIDTH), lambda bi, i: (bi, i, 0))],
        out_specs=pl.BlockSpec((1, tq, DSA_WIDTH), lambda bi, i: (bi, i, 0)),
        out_shape=jax.ShapeDtypeStruct((b, s, DSA_WIDTH), BF16),
        scratch_shapes=[pltpu.VMEM((s, tq), I32),
                        pltpu.VMEM((s, tq), I16), pltpu.VMEM((s, tq), I16),
                        pltpu.VMEM((tk, hw), F32), pltpu.VMEM((tk, hw), F32),
                        pltpu.VMEM((tk, hw), F32),
                        pltpu.VMEM((tk, hw), BF16),
                        pltpu.VMEM((SUBLANES, hw), F32),
                        pltpu.VMEM((DSA_HEAD_DIM, hw), F32),
                        pltpu.VMEM((DSA_WIDTH, tq), F32),
                        pltpu.VMEM((SUBLANES, hw), F32), pltpu.VMEM((SUBLANES, hw), F32),
                        pltpu.VMEM((SUBLANES, tq), I32)],
        compiler_params=_compiler_params(("parallel", "arbitrary"), vmem),
        name="dsa",
    )(q_min, k_max, tbl_t, qcw, qiw, kc, ki, v1, wi, pos_col, pos_row, gate)


def _merge_kernel(oa_ref, ob_ref, oc_ref, g_ref, x_ref, p_ref, wa_ref, wb_ref, wc_ref, wo_ref, wpg_ref, wple_ref,
                  gn_ref, xo_ref, ho_ref):
    d = D_MODEL
    ya = jnp.dot(oa_ref[...], wa_ref[0], preferred_element_type=F32)
    yb = jnp.dot(ob_ref[...], wb_ref[0], preferred_element_type=F32)
    yc = jnp.dot(oc_ref[...], wc_ref[0], preferred_element_type=F32)
    merged = g_ref[:, 0:d] * ya + g_ref[:, d:2 * d] * yb + g_ref[:, 2 * d:3 * d] * yc
    x1 = x_ref[...] + jnp.dot(merged.astype(BF16), wo_ref[0], preferred_element_type=F32)
    ple_gate = _sigmoid(jnp.dot(x1.astype(BF16), wpg_ref[0], preferred_element_type=F32))
    x2 = x1 + ple_gate * jnp.dot(p_ref[0].astype(BF16), wple_ref[0], preferred_element_type=F32)
    xo_ref[...] = x2
    ho_ref[...] = _rms(x2, gn_ref[...]).astype(ho_ref.dtype)


def _merge(oa, ob, oc, gates, x, p, layer, wa, wb, wc, wo, wpg, wple, g_next, h_dtype, tm=256):
    t, d = x.shape
    full = lambda a: pl.BlockSpec((1,) + a.shape[1:], lambda i: (layer, 0, 0), pipeline_mode=pl.Buffered(1))
    row = lambda a: pl.BlockSpec((tm, a.shape[1]), lambda i: (i, 0))
    weights = (wa, wb, wc, wo, wpg, wple)
    acts = (oa, ob, oc, gates, x)
    vmem = (sum(_nbytes(a.shape[1:], a.dtype) for a in weights)
            + 2 * (sum(_nbytes((tm, a.shape[1]), a.dtype) for a in acts) + _nbytes((tm, p.shape[2]), p.dtype)
                   + 2 * _nbytes((tm, d), F32)) + 8 * _nbytes((tm, d), F32))
    return pl.pallas_call(
        _merge_kernel,
        grid=(t // tm,),
        in_specs=[row(a) for a in acts] + [pl.BlockSpec((1, tm, p.shape[2]), lambda i: (layer, i, 0))]
        + [full(a) for a in weights] + [pl.BlockSpec((1, d), lambda i: (0, 0))],
        out_specs=[pl.BlockSpec((tm, d), lambda i: (i, 0)), pl.BlockSpec((tm, d), lambda i: (i, 0))],
        out_shape=[jax.ShapeDtypeStruct((t, d), F32), jax.ShapeDtypeStruct((t, d), h_dtype)],
        compiler_params=_compiler_params(("parallel",), vmem),
        name="merge",
    )(*acts, p, *weights, g_next.reshape(1, d))


def _rotate_half_cols(w):
    half = w.shape[-1] // 2
    return jnp.concatenate([-w[..., half:], w[..., :half]], axis=-1)


def _prep_weights(w_in, w_uq, w_ukv):
    depth = w_in.shape[0]
    seg = lambda name: w_in[:, :, SPLIT[name][0]:SPLIT[name][1]]
    z = lambda *shape: jnp.zeros((depth,) + shape, w_in.dtype)
    d = D_MODEL
    kr = seg("k_rope")
    rope_lo = MLA_NOPE
    rope_pad = MLA_QK_PAD - MLA_NOPE - MLA_ROPE
    w = {}
    w["mla_in"] = jnp.concatenate(
        [seg("c_q"), seg("c_kv"), z(d, rope_lo), kr, z(d, rope_pad), z(d, rope_lo), _rotate_half_cols(kr), z(d, rope_pad)], axis=-1)
    w["gate_a"] = seg("gate_a")
    w["z"] = seg("z")
    w["xbc"] = seg("xbc")
    w["small"] = jnp.concatenate([seg("dt"), z(d, LANES - SSM_HEADS)], axis=-1)
    w["k_c"] = seg("k_c")
    w["k_idx"] = seg("k_idx")
    w["gate_c"] = seg("gate_c")
    w["merge"] = seg("merge")
    w["q_c_t"] = jnp.swapaxes(seg("q_c") * (DSA_HEAD_DIM ** -0.5 * LOG2E), 1, 2)
    w["q_idx_t"] = jnp.swapaxes(seg("q_idx"), 1, 2)
    w["vw_t"] = jnp.swapaxes(jnp.concatenate([seg("v_c"), seg("w_idx")], axis=-1), 1, 2)
    w = {
        "proj": jnp.concatenate([w["mla_in"]] + [w[seg[0]] for seg in PROJ_SEGMENTS], axis=-1).astype(BF16),
        "proj_t": jnp.concatenate([w["q_c_t"], w["q_idx_t"], w["vw_t"]], axis=1).astype(BF16),
    }

    uq =w_uq.reshape(depth, MLA_Q_LORA, MLA_HEADS, MLA_NOPE + MLA_ROPE)
    nope, rope = uq[..., :MLA_NOPE], uq[..., MLA_NOPE:]
    zq = lambda width: jnp.zeros((depth, MLA_Q_LORA, MLA_HEADS, width), w_uq.dtype)
    hq = MLA_HEADS * MLA_QK_PAD
    uq1 = jnp.concatenate([nope, rope, zq(rope_pad)], axis=-1).reshape(depth, MLA_Q_LORA, hq)
    uq2 = jnp.concatenate([zq(MLA_NOPE), _rotate_half_cols(rope), zq(rope_pad)], axis=-1).reshape(depth, MLA_Q_LORA, hq)
    w["uq1_t"] = jnp.swapaxes(uq1, 1, 2).astype(BF16)
    w["uq2_t"] = jnp.swapaxes(uq2, 1, 2).astype(BF16)
    ukv = w_ukv.reshape(depth, MLA_KV_LORA, MLA_HEADS, MLA_NOPE + MLA_V)
    zk = jnp.zeros((depth, MLA_KV_LORA, MLA_HEADS, MLA_QK_PAD - MLA_NOPE), w_ukv.dtype)
    w["uk"] = jnp.concatenate([ukv[..., :MLA_NOPE], zk], axis=-1).reshape(depth, MLA_KV_LORA, hq).astype(BF16)
    w["uv_t"] = jnp.swapaxes(ukv[..., MLA_NOPE:].reshape(depth, MLA_KV_LORA, MLA_WIDTH), 1, 2).astype(BF16)
    return w


def _rope_multipliers(positions):
    b, s = positions.shape
    inv_freq = 1.0 / (ROPE_THETA ** (jnp.arange(0, MLA_ROPE, 2, dtype=F32) / MLA_ROPE))
    ang = positions.astype(F32)[..., None] * inv_freq
    cos, sin = jnp.cos(ang), jnp.sin(ang)
    pad = MLA_QK_PAD - MLA_NOPE - MLA_ROPE
    m1 = jnp.concatenate([jnp.ones((b, s, MLA_NOPE), F32), cos, cos, jnp.zeros((b, s, pad), F32)], axis=-1)
    m2 = jnp.concatenate([jnp.zeros((b, s, MLA_NOPE), F32), sin, sin, jnp.zeros((b, s, pad), F32)], axis=-1)
    return m1, m2, jnp.swapaxes(m1, 1, 2), jnp.swapaxes(m2, 1, 2)


def kernel(x, p, positions, norm_g, w_in, mla_q_norm, w_uq, mla_kv_norm, w_ukv, conv_w, conv_b, dt_bias, a_log, d_skip, ssm_norm, w_br_a, w_br_b, w_br_c, w_out, rel_bias, w_ple, w_ple_gate, final_norm):
    b, s, d = x.shape
    depth = w_in.shape[0]
    t = b * s
    w = _prep_weights(w_in, w_uq, w_ukv)
    m1, m2, m1t, m2t = _rope_multipliers(positions)
    pos_col = positions.astype(I32).reshape(b, s, 1)
    pos_row = positions.astype(I32).reshape(b, 1, s)
    wbr_a, wbr_b, wbr_c = w_br_a.astype(BF16), w_br_b.astype(BF16), w_br_c.astype(BF16)
    wo, wpg, wple = w_out.astype(BF16), w_ple_gate.astype(BF16), w_ple.astype(BF16)

    q_norm = mla_q_norm.astype(F32).reshape(depth, 1, -1)
    kv_norm = mla_kv_norm.astype(F32).reshape(depth, 1, -1)
    p3d = p.reshape(depth, t, -1)

    x2d = x.reshape(t, d)
    h = _norm(x2d, norm_g[0], BF16)
    for i in range(depth):
        pr = _proj_all(h.reshape(b, s, d), i, w["proj"], w["proj_t"], (m1, m2, m1t, m2t), q_norm, kv_norm,
                       w["uq1_t"], w["uq2_t"], w["uk"], w["uv_t"], DSA_Q_BLOCK)
        o_a = _mla_attn(pr["mla_q_t"], pr["mla_k"], pr["mla_v_t1"], pr["gate_a"])
        o_b = _ssd(pr["xbc"], pr["z"], pr["small"], conv_w[i], conv_b[i], dt_bias[i], a_log[i], d_skip[i], ssm_norm[i])
        o_c = _dsa(rel_bias, pr["q_c_wide"], pr["q_idx_wide"], pr["k_c"], pr["k_idx"], pr["v_t1"], pr["w_idx_t"], pos_col, pos_row,
                   pr["gate_c"], DSA_Q_BLOCK)

        last = i == depth - 1
        g_next = final_norm if last else norm_g[i + 1]
        x2d, h = _merge(o_a.reshape(t, -1), o_b.reshape(t, -1), o_c.reshape(t, -1), pr["merge"].reshape(t, -1), x2d, p3d,
                        i, wbr_a, wbr_b, wbr_c, wo, wpg, wple, g_next, F32 if last else BF16)
    return h.reshape(b, s, d)
```

```python
import functools
import math

import jax
import jax.numpy as jnp
from jax import lax
from jax.experimental import pallas as pl
from jax.experimental.pallas import tpu as pltpu

F32 = jnp.float32
BF16 = jnp.bfloat16
I32 = jnp.int32
I16 = jnp.int16
HALF16 = 1 << 15

D_MODEL = 1024
PLE_DIM = 256
NORM_EPS = 1e-6

MLA_HEADS = 8
MLA_NOPE = 64
MLA_ROPE = 32
MLA_V = 64
MLA_Q_LORA = 384
MLA_KV_LORA = 256
MLA_WIDTH = MLA_HEADS * MLA_V
ROPE_THETA = 10000.0
MLA_QK_PAD = 128
LOG2E = math.log2(math.e)
MLA_Q_SCALE = (MLA_NOPE + MLA_ROPE) ** -0.5 * LOG2E

SSM_HEADS = 16
SSM_HEAD_DIM = 64
SSM_INNER = SSM_HEADS * SSM_HEAD_DIM
SSM_GROUPS = 2
SSM_STATE = 128
SSM_CONV = 4
SSM_CHUNK = 128
SSM_CONV_DIM = SSM_INNER + 2 * SSM_GROUPS * SSM_STATE
SSM_HEADS_PER_GROUP = SSM_HEADS // SSM_GROUPS
ONES_ROWS = 16
MLA_V1_ROWS = MLA_V + ONES_ROWS
SSM_CONV_TAIL = 16

DSA_HEADS = 8
DSA_HEAD_DIM = 64
DSA_WIDTH = DSA_HEADS * DSA_HEAD_DIM
IDX_HEADS = 8
IDX_DIM = 64
TOPK_MAX = 256
DSA_KEY_TILE = 128
DSA_Q_BLOCK = 256

REL_BUCKETS = 32
REL_MAX_DIST = 128
N_BRANCHES = 3

LANES = 128
SUBLANES = 8
VMEM_LIMIT_CAP = 56 * 1024 * 1024
INT_MIN = -(2 ** 31)
NEG_BIG = -1e30

SPLIT_SIZES = (
    MLA_Q_LORA, MLA_KV_LORA, MLA_ROPE, MLA_WIDTH, SSM_INNER, SSM_CONV_DIM, SSM_HEADS, DSA_WIDTH,
    DSA_HEAD_DIM, DSA_HEAD_DIM, IDX_HEADS * IDX_DIM, IDX_DIM, IDX_HEADS, DSA_WIDTH,
    N_BRANCHES * D_MODEL,
)
SPLIT_NAMES = ("c_q", "c_kv", "k_rope", "gate_a", "z", "xbc", "dt", "q_c", "k_c", "v_c", "q_idx",
               "k_idx", "w_idx", "gate_c", "merge")


def _split_bounds():
    out, off = {}, 0
    for name, size in zip(SPLIT_NAMES, SPLIT_SIZES, strict=True):
        out[name] = (off, off + size)
        off += size
    return out


SPLIT = _split_bounds()


def _t5_large_thresholds():
    exact = REL_BUCKETS // 2
    thr = []
    for j in range(1, REL_BUCKETS - exact):
        thr.append(int(math.ceil(exact * (REL_MAX_DIST / exact) ** (j / (REL_BUCKETS - exact)) - 1e-9)))
    return tuple(thr)


T5_EXACT = REL_BUCKETS // 2
T5_LARGE_THR = _t5_large_thresholds()
T5_FAR = T5_LARGE_THR[-1]


def _compiler_params(semantics, vmem_bytes):
    limit = int(min(VMEM_LIMIT_CAP, max(32 * 1024 * 1024, vmem_bytes)))
    return pltpu.CompilerParams(dimension_semantics=semantics, vmem_limit_bytes=limit)


def _nbytes(shape, dtype):
    return math.prod(shape) * jnp.dtype(dtype).itemsize


def _sigmoid(x):
    return 1.0 / (1.0 + jnp.exp(-x))


def _rms(x, g):
    return x * lax.rsqrt(jnp.mean(x * x, axis=-1, keepdims=True) + NORM_EPS) * g


def _norm_kernel(x_ref, g_ref, o_ref):
    o_ref[...] = _rms(x_ref[...], g_ref[...]).astype(o_ref.dtype)


def _norm(x2d, g, out_dtype, tm=512):
    t, d = x2d.shape
    return pl.pallas_call(
        _norm_kernel,
        grid=(t // tm,),
        in_specs=[pl.BlockSpec((tm, d), lambda i: (i, 0)), pl.BlockSpec((1, d), lambda i: (0, 0))],
        out_specs=pl.BlockSpec((tm, d), lambda i: (i, 0)),
        out_shape=jax.ShapeDtypeStruct((t, d), out_dtype),
        compiler_params=_compiler_params(("parallel",), 4 * _nbytes((tm, d), F32)),
        name="rms_norm",
    )(x2d, g.reshape(1, d))


MLA_IN_WIDTH = MLA_Q_LORA + MLA_KV_LORA + 2 * LANES
PROJ_SEGMENTS = (
    ("small", LANES, None, F32),
    ("gate_a", MLA_WIDTH, "silu", BF16),
    ("gate_c", DSA_WIDTH, "silu", BF16),
    ("z", SSM_INNER, "silu", BF16),
    ("xbc", SSM_CONV_DIM, None, BF16),
    ("merge", N_BRANCHES * D_MODEL, "sigmoid", BF16),
    ("k_c", DSA_HEAD_DIM, None, BF16),
    ("k_idx", IDX_DIM, None, BF16),
)
PROJ_CHUNK = 512
PROJ_T_ROWS = (DSA_WIDTH, IDX_HEADS * IDX_DIM, DSA_HEAD_DIM + IDX_HEADS)


def _proj_all_kernel(h_ref, w_ref, wt_ref, m1_ref, m2_ref, m1t_ref, m2t_ref, qn_ref, kvn_ref, wq1t_ref, wq2t_ref, wk_ref,
                     wvt_ref, *out_refs):
    h = h_ref[0]
    n_seg = len(PROJ_SEGMENTS)
    a = jnp.dot(h, w_ref[0, :, 0:MLA_IN_WIDTH], preferred_element_type=F32)
    _mla_prep(a, m1_ref, m2_ref, m1t_ref, m2t_ref, qn_ref, kvn_ref, wq1t_ref, wq2t_ref, wk_ref, wvt_ref, *out_refs[n_seg + 4:])
    off = MLA_IN_WIDTH
    for (_, width, act, _), o_ref in zip(PROJ_SEGMENTS, out_refs[:n_seg], strict=True):
        for c in range(0, width, PROJ_CHUNK):
            cw = min(PROJ_CHUNK, width - c)
            y = jnp.dot(h, w_ref[0, :, off + c:off + c + cw], preferred_element_type=F32)
            if act == "silu":
                y = y * _sigmoid(y)
            elif act == "sigmoid":
                y = _sigmoid(y)
            o_ref[0, :, c:c + cw] = y.astype(o_ref.dtype)
        off += width
    qcw_ref, qiw_ref, v1_ref, wi_ref = out_refs[n_seg:n_seg + 4]
    nt = (((1,), (1,)), ((), ()))
    tq = h.shape[0]
    row = 0
    for o_ref, heads, dim in ((qcw_ref, DSA_HEADS, DSA_HEAD_DIM), (qiw_ref, IDX_HEADS, IDX_DIM)):
        y = lax.dot_general(wt_ref[0, row:row + heads * dim, :], h, nt, preferred_element_type=F32)
        for hd in range(heads):
            o_ref[0, 0, :, hd * tq:(hd + 1) * tq] = y[hd * dim:(hd + 1) * dim, :].astype(o_ref.dtype)
        row += heads * dim
    vw = lax.dot_general(wt_ref[0, row:row + PROJ_T_ROWS[2], :], h, nt, preferred_element_type=F32)
    v1_ref[0, 0:DSA_HEAD_DIM, :] = vw[0:DSA_HEAD_DIM, :].astype(v1_ref.dtype)
    v1_ref[0, DSA_HEAD_DIM:, :] = jnp.ones((ONES_ROWS, tq), v1_ref.dtype)
    wi_ref[0] = vw[DSA_HEAD_DIM:, :]


def _proj_all(h3d, layer, w, wt, rope, qn, kvn, wq1t, wq2t, wk, wvt, tq):
    b, s, k = h3d.shape
    n_all = MLA_IN_WIDTH + sum(seg[1] for seg in PROJ_SEGMENTS)
    hq = MLA_HEADS * MLA_QK_PAD
    v1_rows = DSA_HEAD_DIM + ONES_ROWS
    out_shapes = [jax.ShapeDtypeStruct((b, s, width), dt) for _, width, _, dt in PROJ_SEGMENTS]
    out_specs = [pl.BlockSpec((1, tq, width), lambda bi, i: (bi, i, 0)) for _, width, _, _ in PROJ_SEGMENTS]
    for heads, dim in ((DSA_HEADS, DSA_HEAD_DIM), (IDX_HEADS, IDX_DIM)):
        out_shapes.append(jax.ShapeDtypeStruct((b, s // tq, dim, heads * tq), BF16))
        out_specs.append(pl.BlockSpec((1, 1, dim, heads * tq), lambda bi, i: (bi, i, 0, 0)))
    for rows, dt in ((v1_rows, BF16), (IDX_HEADS, F32)):
        out_shapes.append(jax.ShapeDtypeStruct((b, rows, s), dt))
        out_specs.append(pl.BlockSpec((1, rows, tq), lambda bi, i: (bi, 0, i)))
    for rows in (hq, None, MLA_HEADS * MLA_V1_ROWS):
        if rows is None:
            out_shapes.append(jax.ShapeDtypeStruct((b, s, hq), BF16))
            out_specs.append(pl.BlockSpec((1, tq, hq), lambda bi, i: (bi, i, 0)))
        else:
            out_shapes.append(jax.ShapeDtypeStruct((b, rows, s), BF16))
            out_specs.append(pl.BlockSpec((1, rows, tq), lambda bi, i: (bi, 0, i)))
    resident = pl.Buffered(1)
    weights = (w, wt, qn, kvn, wq1t, wq2t, wk, wvt)
    layer_spec = lambda a: pl.BlockSpec((1,) + a.shape[1:], lambda bi, i: (layer,) + (0,) * (a.ndim - 1), pipeline_mode=resident)
    m1, m2, m1t, m2t = rope
    vmem = (sum(_nbytes(a.shape[1:], a.dtype) for a in weights) + 2 * _nbytes((tq, k), BF16) + 8 * _nbytes((tq, LANES), F32)
            + 2 * sum(_nbytes((tq, width), dt) for _, width, _, dt in PROJ_SEGMENTS)
            + 2 * (2 * _nbytes((DSA_HEAD_DIM, DSA_HEADS * tq), BF16) + _nbytes((v1_rows + IDX_HEADS, tq), F32))
            + 2 * (2 * _nbytes((tq, hq), BF16) + _nbytes((MLA_HEADS * MLA_V1_ROWS, tq), BF16))
            + 8 * _nbytes((tq, PROJ_CHUNK), F32) + 6 * _nbytes((tq, hq), F32))
    outs = pl.pallas_call(
        _proj_all_kernel,
        grid=(b, s // tq),
        in_specs=[pl.BlockSpec((1, tq, k), lambda bi, i: (bi, i, 0)), layer_spec(w), layer_spec(wt),
                  pl.BlockSpec((1, tq, LANES), lambda bi, i: (bi, i, 0)), pl.BlockSpec((1, tq, LANES), lambda bi, i: (bi, i, 0)),
                  pl.BlockSpec((1, LANES, tq), lambda bi, i: (bi, 0, i)), pl.BlockSpec((1, LANES, tq), lambda bi, i: (bi, 0, i))]
        + [layer_spec(a) for a in weights[2:]],
        out_specs=out_specs,
        out_shape=out_shapes,
        compiler_params=_compiler_params(("parallel", "parallel"), vmem),
        name="proj_all",
    )(h3d, w, wt, m1, m2, m1t, m2t, *weights[2:])
    names = [seg[0] for seg in PROJ_SEGMENTS] + ["q_c_wide", "q_idx_wide", "v_t1", "w_idx_t", "mla_q_t", "mla_k", "mla_v_t1"]
    return dict(zip(names, outs, strict=True))


def _mla_prep(a, m1_ref, m2_ref, m1t_ref, m2t_ref, qn_ref, kvn_ref, wq1t_ref, wq2t_ref, wk_ref, wvt_ref, qt_ref, k_ref, vt_ref):
    c_q = a[:, :MLA_Q_LORA]
    c_kv = a[:, MLA_Q_LORA:MLA_Q_LORA + MLA_KV_LORA]
    kr1 = a[:, MLA_Q_LORA + MLA_KV_LORA:MLA_Q_LORA + MLA_KV_LORA + LANES]
    kr2 = a[:, MLA_Q_LORA + MLA_KV_LORA + LANES:]
    cqn = _rms(c_q, qn_ref[0]).astype(BF16)
    ckvn = _rms(c_kv, kvn_ref[0]).astype(BF16)
    nt = (((1,), (1,)), ((), ()))
    qa_t = lax.dot_general(wq1t_ref[0], cqn, nt, preferred_element_type=F32)
    qb_t = lax.dot_general(wq2t_ref[0], cqn, nt, preferred_element_type=F32)
    kn = jnp.dot(ckvn, wk_ref[0], preferred_element_type=F32)
    kr = kr1 * m1_ref[0] + kr2 * m2_ref[0]
    m1t = m1t_ref[0] * MLA_Q_SCALE
    m2t = m2t_ref[0] * MLA_Q_SCALE
    for h in range(MLA_HEADS):
        sl = slice(h * MLA_QK_PAD, (h + 1) * MLA_QK_PAD)
        qt_ref[0, sl, :] = (qa_t[sl, :] * m1t + qb_t[sl, :] * m2t).astype(qt_ref.dtype)
        k_ref[0, :, sl] = (kn[:, sl] + kr).astype(k_ref.dtype)
    v_t = lax.dot_general(wvt_ref[0], ckvn, nt, preferred_element_type=F32)
    ones = jnp.ones((ONES_ROWS, v_t.shape[1]), vt_ref.dtype)
    for h in range(MLA_HEADS):
        base = h * MLA_V1_ROWS
        vt_ref[0, base:base + MLA_V, :] = v_t[h * MLA_V:(h + 1) * MLA_V, :].astype(vt_ref.dtype)
        vt_ref[0, base + MLA_V:base + MLA_V1_ROWS, :] = ones


def _mla_attn_kernel(qt_ref, k_ref, vt_ref, g_ref, o_ref, m_ref, l_ref, acc_ref, s0_ref, s1_ref, *, tq, tk):
    assert tq == 2 * tk
    qi = pl.program_id(1)
    krow = lax.broadcasted_iota(I32, (tk, tq), 0)
    qcol = lax.broadcasted_iota(I32, (tk, tq), 1)
    tiles_per_q = tq // tk
    m_ref[...] = jnp.full(m_ref.shape, -jnp.inf, F32)
    l_ref[...] = jnp.zeros(l_ref.shape, F32)
    acc_ref[...] = jnp.zeros(acc_ref.shape, F32)

    staged = (s0_ref, s1_ref)
    n_tiles = (qi + 1) * tiles_per_q

    def head_lanes(h):
        return slice(h * tq, (h + 1) * tq)

    def stage(j, dst_ref):
        ks = pl.multiple_of(j * tk, tk)
        for h in range(MLA_HEADS):
            k = k_ref[0, pl.ds(ks, tk), h * MLA_QK_PAD:(h + 1) * MLA_QK_PAD]
            dst_ref[:, head_lanes(h)] = jnp.dot(k, qt_ref[0, h * MLA_QK_PAD:(h + 1) * MLA_QK_PAD, :],
                                                preferred_element_type=F32)

    def tile(j, cur_ref, next_ref, diagonal):
        ks = pl.multiple_of(j * tk, tk)
        stage(jnp.minimum(j + 1, n_tiles - 1), next_ref)
        m_all = m_ref[...]
        l_all = l_ref[...]
        if diagonal:
            causal = ks + krow <= qi * tq + qcol
        for h in range(MLA_HEADS):
            s = cur_ref[:, head_lanes(h)]
            if diagonal:
                s = jnp.where(causal, s, -jnp.inf)
            m_old = m_all[h:h + 1, :]
            m_new = jnp.maximum(m_old, jnp.max(s, axis=0, keepdims=True))
            alpha = jnp.exp2(m_old - m_new)
            p = jnp.exp2(s - m_new).astype(BF16)
            m_ref[h:h + 1, :] = m_new
            pv = jnp.dot(vt_ref[0, h * MLA_V1_ROWS:(h + 1) * MLA_V1_ROWS, pl.ds(ks, tk)], p, preferred_element_type=F32)
            hs = slice(h * MLA_V, (h + 1) * MLA_V)
            acc_ref[hs, :] = alpha * acc_ref[hs, :] + pv[0:MLA_V, :]
            l_ref[h:h + 1, :] = alpha * l_all[h:h + 1, :] + pv[MLA_V:MLA_V + 1, :]

    stage(0, staged[0])

    def full_pair(pi, carry):
        for half in range(2):
            tile(2 * pi + half, staged[half], staged[1 - half], diagonal=False)
        return carry

    lax.fori_loop(0, qi, full_pair, 0)
    for half in range(2):
        tile(2 * qi + half, staged[half], staged[1 - half], diagonal=True)
    for h in range(MLA_HEADS):
        hs = slice(h * MLA_V, (h + 1) * MLA_V)
        acc_ref[hs, :] = acc_ref[hs, :] / l_ref[h:h + 1, :]
    o_ref[0] = (acc_ref[...].T * g_ref[0]).astype(o_ref.dtype)


def _mla_attn(qt, k, vt, gate, tq=256, tk=128):
    b, s, hq = k.shape
    vmem = 2 * (_nbytes((hq, tq), BF16) + _nbytes((s, hq), BF16) + _nbytes((MLA_WIDTH, s), BF16)
                + _nbytes((tq, MLA_WIDTH), F32) + _nbytes((tq, MLA_WIDTH), BF16)) + _nbytes((MLA_WIDTH, tq), F32) + 16 * _nbytes((tk, tq), F32)
    return pl.pallas_call(
        functools.partial(_mla_attn_kernel, tq=tq, tk=tk),
        grid=(b, s // tq),
        in_specs=[pl.BlockSpec((1, hq, tq), lambda bi, i: (bi, 0, i)),
                  pl.BlockSpec((1, s, hq), lambda bi, i: (bi, 0, 0)),
                  pl.BlockSpec((1, MLA_HEADS * MLA_V1_ROWS, s), lambda bi, i: (bi, 0, 0)),
                  pl.BlockSpec((1, tq, MLA_WIDTH), lambda bi, i: (bi, i, 0))],
        out_specs=pl.BlockSpec((1, tq, MLA_WIDTH), lambda bi, i: (bi, i, 0)),
        out_shape=jax.ShapeDtypeStruct((b, s, MLA_WIDTH), BF16),
        scratch_shapes=[pltpu.VMEM((MLA_HEADS, tq), F32), pltpu.VMEM((MLA_HEADS, tq), F32),
                        pltpu.VMEM((MLA_WIDTH, tq), F32),
                        pltpu.VMEM((tk, MLA_HEADS * tq), F32), pltpu.VMEM((tk, MLA_HEADS * tq), F32)],
        compiler_params=_compiler_params(("parallel", "arbitrary"), vmem),
        name="mla_attn",
    )(qt, k, vt, gate)


def _ssd_kernel(xbc_ref, xprev_ref, zs_ref, sm_ref, cw_ref, cb_ref, dtb_ref, alog_ref, dskf_ref, nrm_ref, exp16_ref, exp32_ref,
                o_ref, state_ref, y_ref):
    c = pl.program_id(1)
    q = SSM_CHUNK

    @pl.when(c == 0)
    def _():
        state_ref[...] = jnp.zeros(state_ref.shape, F32)

    x_in = xbc_ref[0]
    tail = xprev_ref.shape[1]
    prev = jnp.where(c > 0, xprev_ref[0].astype(F32), 0.0).astype(x_in.dtype)
    win = jnp.concatenate([prev, x_in], axis=0)
    x = x_in.astype(F32)
    wrow = lax.broadcasted_iota(I32, (q, tail + q), 0)
    wcol = lax.broadcasted_iota(I32, (q, tail + q), 1)
    acc = cb_ref[...] + cw_ref[SSM_CONV - 1:SSM_CONV, :] * x
    for j in range(1, SSM_CONV):
        pick = jnp.where(wcol == wrow + (tail - j), 1.0, 0.0).astype(win.dtype)
        acc = acc + cw_ref[SSM_CONV - 1 - j:SSM_CONV - j, :] * jnp.dot(pick, win, preferred_element_type=F32)
    half = 0.5 * acc
    xc = half + half * jnp.tanh(half)

    xs = xc[:, :SSM_INNER]
    bm = xc[:, SSM_INNER:SSM_INNER + SSM_GROUPS * SSM_STATE]
    cm = xc[:, SSM_INNER + SSM_GROUPS * SSM_STATE:]

    pre = sm_ref[0] + dtb_ref[...]
    dt = jnp.maximum(pre, 0.0) + jnp.log1p(jnp.exp(-jnp.abs(pre)))
    a = -jnp.exp(alog_ref[...])
    row = lax.broadcasted_iota(I32, (q, q), 0)
    col = lax.broadcasted_iota(I32, (q, q), 1)
    lower = row >= col
    cum = jnp.dot(jnp.where(lower, 1.0, 0.0).astype(F32), dt * a, preferred_element_type=F32,
                  precision=lax.Precision.HIGHEST)
    cum_t = cum.T
    cum_last = cum[q - 1:q, :]
    per_head = jnp.concatenate([dt, jnp.exp(cum_last - cum), jnp.exp(cum)], axis=0).astype(BF16)
    spread = jnp.dot(per_head, exp16_ref[...], preferred_element_type=F32)
    dt_full, to_end_full, e_cum_full = spread[0:q], spread[q:2 * q], spread[2 * q:3 * q]
    chunk_decay_full = jnp.dot(jnp.broadcast_to(jnp.exp(cum_last), (SUBLANES, LANES)), exp32_ref[...],
                               preferred_element_type=F32, precision=lax.Precision.HIGHEST)[0:1]
    xdt = xs * dt_full
    xw = (xdt * to_end_full).astype(BF16)
    first_head = lax.broadcasted_iota(I32, (q, LANES), 1) < SSM_HEAD_DIM
    group_w = SSM_HEADS_PER_GROUP * SSM_HEAD_DIM
    heads_per_tile = LANES // SSM_HEAD_DIM

    for g in range(SSM_GROUPS):
        bg32 = bm[:, g * SSM_STATE:(g + 1) * SSM_STATE]
        bg = bg32.astype(BF16)
        cg = cm[:, g * SSM_STATE:(g + 1) * SSM_STATE].astype(BF16)
        cb = lax.dot_general(cg, bg, (((1,), (1,)), ((), ())), preferred_element_type=F32)
        gl = slice(g * group_w, (g + 1) * group_w)
        st = state_ref[g]
        y_off = jnp.dot(cg, st.astype(BF16), preferred_element_type=F32)
        state_ref[g] = st * chunk_decay_full[:, gl] + jnp.dot(bg32.T.astype(BF16), xw[:, gl], preferred_element_type=F32)
        for tile in range(group_w // LANES):
            h0 = g * SSM_HEADS_PER_GROUP + tile * heads_per_tile
            tl = slice(h0 * SSM_HEAD_DIM, h0 * SSM_HEAD_DIM + LANES)
            g_both = []
            for h in range(h0, h0 + heads_per_tile):
                diff = cum[:, h:h + 1] - cum_t[h:h + 1, :]
                decay_in = jnp.exp(jnp.where(lower, diff, -jnp.inf))
                g_both.append((cb * decay_in).astype(BF16))
            xdt_tile = xdt[:, tl]
            stacked = jnp.concatenate([jnp.where(first_head, xdt_tile, 0.0).astype(BF16),
                                       jnp.where(first_head, 0.0, xdt_tile).astype(BF16)], axis=0)
            y_diag = jnp.dot(jnp.concatenate(g_both, axis=1), stacked, preferred_element_type=F32)
            y = y_diag + y_off[:, tile * LANES:(tile + 1) * LANES] * e_cum_full[:, tl] + xs[:, tl] * dskf_ref[:, tl]
            y_ref[:, tl] = y * zs_ref[0, :, tl]

    o_ref[0] = _rms(y_ref[...], nrm_ref[...]).astype(o_ref.dtype)


def _ssd(xbc, zs, small, conv_w, conv_b, dt_bias, a_log, d_skip, ssm_norm):
    b, s, _ = xbc.shape
    q = SSM_CHUNK
    pad = lambda v: jnp.pad(v.astype(F32), (0, LANES - v.shape[0])).reshape(1, LANES)
    full = lambda shape: pl.BlockSpec(shape, lambda bi, c: (0,) * len(shape))
    blk = lambda width: pl.BlockSpec((1, q, width), lambda bi, c: (bi, c, 0))
    expand = (jnp.arange(LANES)[:, None] == jnp.arange(SSM_INNER)[None, :] // SSM_HEAD_DIM).astype(F32)
    d_skip_full = jnp.repeat(d_skip.astype(F32), SSM_HEAD_DIM).reshape(1, SSM_INNER)
    group_w = SSM_HEADS_PER_GROUP * SSM_HEAD_DIM
    vmem = (2 * (_nbytes((q, SSM_CONV_DIM), xbc.dtype) + _nbytes((q, SSM_INNER), zs.dtype) + _nbytes((q, LANES), F32)
                 + _nbytes((q, SSM_INNER), BF16) + _nbytes((LANES, SSM_INNER), BF16) + _nbytes((LANES, SSM_INNER), F32))
            + _nbytes((2 * q, SSM_CONV_DIM), xbc.dtype) + _nbytes((SSM_GROUPS, SSM_STATE, group_w), F32)
            + _nbytes((q, SSM_INNER), F32) + 10 * _nbytes((q, SSM_CONV_DIM), F32))
    return pl.pallas_call(
        _ssd_kernel,
        grid=(b, s // q),
        in_specs=[blk(SSM_CONV_DIM),
                  pl.BlockSpec((1, SSM_CONV_TAIL, SSM_CONV_DIM),
                               lambda bi, c: (bi, jnp.maximum(c * (q // SSM_CONV_TAIL) - 1, 0), 0)),
                  blk(SSM_INNER), blk(LANES), full((SSM_CONV, SSM_CONV_DIM)), full((1, SSM_CONV_DIM)),
                  full((1, LANES)), full((1, LANES)), full((1, SSM_INNER)), full((1, SSM_INNER)),
                  full((LANES, SSM_INNER)), full((LANES, SSM_INNER))],
        out_specs=blk(SSM_INNER),
        out_shape=jax.ShapeDtypeStruct((b, s, SSM_INNER), BF16),
        scratch_shapes=[pltpu.VMEM((SSM_GROUPS, SSM_STATE, group_w), F32),
                        pltpu.VMEM((q, SSM_INNER), F32)],
        compiler_params=_compiler_params(("parallel", "arbitrary"), vmem),
        name="ssd",
    )(xbc, xbc, zs, small, conv_w.astype(F32), conv_b.reshape(1, -1).astype(F32), pad(dt_bias), pad(a_log), d_skip_full,
      ssm_norm.reshape(1, -1).astype(F32), expand.astype(BF16), expand)


def _t5_bucket_of(n):
    steps_per_octave = (REL_BUCKETS - T5_EXACT) / math.log2(REL_MAX_DIST / T5_EXACT)
    x = jnp.maximum(n, T5_EXACT).astype(F32) * (1.0 / T5_EXACT)
    large = T5_EXACT + jnp.floor(jnp.log2(x) * steps_per_octave).astype(I32)
    return jnp.where(n <= T5_EXACT, n, jnp.minimum(large, REL_BUCKETS - 1))


def _dsa_kernel(qmin_ref, kmax_ref, tblt_ref, qcw_ref, qiw_ref, kc_ref, ki_ref, v1_ref, wi_ref, posk_ref, posq_ref, gate_ref, o_ref,
                key_ref, hi_ref, lo_ref, s0_ref, s1_ref, bias_ref, acc_ref, ot_ref, m_ref, l_ref, tie_ref, *, n_sel, tq):
    tk = DSA_KEY_TILE
    qi = pl.program_id(1)
    n_tiles = (qi + 1) * (tq // tk)
    krow = lax.broadcasted_iota(I32, (tk, tq), 0)
    qcol = lax.broadcasted_iota(I32, (tk, tq), 1)
    w = wi_ref[0]
    posq = posq_ref[0]

    def tile_start(kt):
        return pl.multiple_of(kt * tk, tk)

    def causal_of(ks):
        return (ks + krow) <= (qi * tq + qcol)

    def head_lanes(h):
        return slice(h * tq, (h + 1) * tq)

    staged = (s0_ref, s1_ref)

    def stage(k_ref, q_ref, kt, dst_ref):
        dst_ref[...] = jnp.dot(k_ref[0, pl.ds(tile_start(kt), tk), :], q_ref[0, 0], preferred_element_type=F32)

    def tile_pairs(tile_fn):
        def pair(pi, carry):
            for half in range(2):
                tile_fn(2 * pi + half, staged[half], staged[1 - half])
            return carry
        lax.fori_loop(0, n_tiles // 2, pair, 0)

    stage(ki_ref, qiw_ref, 0, staged[0])

    def score_tile(kt, cur_ref, next_ref):
        ks = tile_start(kt)
        stage(ki_ref, qiw_ref, jnp.minimum(kt + 1, n_tiles - 1), next_ref)
        sc = jnp.zeros((tk, tq), F32)
        for h in range(IDX_HEADS):
            sc = sc + w[h:h + 1, :] * jnp.maximum(cur_ref[:, head_lanes(h)], 0.0)
        sc = jnp.where(sc == 0.0, 0.0, sc)
        bits = lax.bitcast_convert_type(sc, I32)
        key = bits ^ ((bits >> 31) & 0x7FFFFFFF)
        key = jnp.where(causal_of(ks), key, INT_MIN)
        key_ref[pl.ds(ks, tk), :] = key
        hi_ref[pl.ds(ks, tk), :] = (key >> 16).astype(I16)
        lo_ref[pl.ds(ks, tk), :] = ((key & 0xFFFF) - HALF16).astype(I16)

    tile_pairs(score_tile)

    def count16(half_ref, pred):
        def body(pi, cnt):
            for kt in (2 * pi, 2 * pi + 1):
                cnt = cnt + jnp.where(pred(half_ref[pl.ds(tile_start(kt), tk), :]), jnp.int16(1), jnp.int16(0))
            return cnt
        cnt = lax.fori_loop(0, n_tiles // 2, body, jnp.zeros((tk, tq), I16))
        rows = 2 * SUBLANES
        parts = [cnt[r:r + rows, :] for r in range(0, tk, rows)]
        while len(parts) > 1:
            parts = [a + b for a, b in zip(parts[0::2], parts[1::2], strict=True)]
        return jnp.sum(parts[0].astype(I32), axis=0, keepdims=True)

    def nth_largest16(half_ref, rank):
        def bit_step(i, cand):
            trial = cand | lax.shift_left(jnp.int32(1), 15 - i)
            thr = (trial - HALF16).astype(I16)
            tot = count16(half_ref, lambda v: v >= thr)
            return jnp.where(tot >= rank, trial, cand)
        return lax.fori_loop(0, 16, bit_step, jnp.zeros((1, tq), I32)) - HALF16

    tau_hi = nth_largest16(hi_ref, n_sel)
    tau_hi16 = tau_hi.astype(I16)
    above_hi = count16(hi_ref, lambda v: v > tau_hi16)

    def mask_low(kt, carry):
        rows = pl.ds(tile_start(kt), tk)
        lo_ref[rows, :] = jnp.where(hi_ref[rows, :] == tau_hi16, lo_ref[rows, :], jnp.int16(-HALF16))
        return carry

    lax.fori_loop(0, n_tiles, mask_low, 0)
    tau_lo = nth_largest16(lo_ref, n_sel - above_hi)
    tau_lo16 = tau_lo.astype(I16)
    tau = lax.shift_left(tau_hi, 16) | (tau_lo + HALF16)
    n_gt = above_hi + count16(lo_ref, lambda v: v > tau_lo16)
    n_ge = above_hi + count16(lo_ref, lambda v: v >= tau_lo16)
    need = n_sel - n_gt

    m_ref[...] = jnp.full(m_ref.shape, NEG_BIG, F32)
    l_ref[...] = jnp.zeros(l_ref.shape, F32)
    acc_ref[...] = jnp.zeros(acc_ref.shape, F32)
    tie_ref[...] = jnp.zeros(tie_ref.shape, I32)
    lower_incl = jnp.where(lax.broadcasted_iota(I32, (tk, tk), 0) >= lax.broadcasted_iota(I32, (tk, tk), 1),
                           1.0, 0.0).astype(BF16)

    splits =jnp.where((n_ge > n_sel) & (tau != INT_MIN), 1.0, 0.0)
    any_split = jnp.max(splits) > 0.0

    def attend_tile(kt, cur_ref, next_ref, general_bias, tie_split):
        ks = tile_start(kt)
        key = key_ref[pl.ds(ks, tk), :]
        if tie_split:
            eq = key == tau
            prefix = jnp.dot(lower_incl, jnp.where(eq, 1.0, 0.0).astype(BF16), preferred_element_type=F32)
            tie_before = tie_ref[0:1, :]
            tie_rank = tie_before + prefix.astype(I32)
            sel = ((key > tau) | (eq & (tie_rank <= need))) & causal_of(ks)
            tie_ref[0:1, :] = tie_before + prefix[tk - 1:tk, :].astype(I32)
        else:
            sel = (key >= tau) & causal_of(ks)

        if general_bias:
            posk = posk_ref[0, pl.ds(ks, tk), :]
            bucket = _t5_bucket_of(jnp.maximum(posq - posk, 0))
            for h in range(DSA_HEADS):
                table = jnp.broadcast_to(tblt_ref[h:h + 1, :], (tk, LANES))
                for c in range(tq // LANES):
                    bias_ref[:, h * tq + c * LANES:h * tq + (c + 1) * LANES] = jnp.take_along_axis(
                        table, bucket[:, c * LANES:(c + 1) * LANES], axis=1)
        stage(kc_ref, qcw_ref, jnp.minimum(kt + 1, n_tiles - 1), next_ref)
        m_all = m_ref[0:1, :]
        l_all = l_ref[0:1, :]
        v1 = v1_ref[0, :, pl.ds(ks, tk)]
        for h in range(DSA_HEADS):
            hl = head_lanes(h)
            s = cur_ref[:, hl]
            if general_bias:
                s = s + bias_ref[:, hl]
            s = jnp.where(sel, s, NEG_BIG)
            m_old = m_all[:, hl]
            m_new = jnp.maximum(m_old, jnp.max(s, axis=0, keepdims=True))
            m_ref[0:1, hl] = m_new
            alpha = jnp.exp2(m_old - m_new)
            pv = jnp.dot(v1, jnp.exp2(s - m_new).astype(BF16), preferred_element_type=F32)
            acc_ref[:, hl] = alpha * acc_ref[:, hl] + pv[0:DSA_HEAD_DIM, :]
            l_ref[0:1, hl] = alpha * l_all[:, hl] + pv[DSA_HEAD_DIM:DSA_HEAD_DIM + 1, :]

    bi = pl.program_id(0)
    stage(kc_ref, qcw_ref, 0, staged[0])

    def attend(kt, cur_ref, next_ref):
        far = (qmin_ref[bi, qi] - kmax_ref[bi, kt]) >= T5_FAR
        fast = jnp.logical_and(far, jnp.logical_not(any_split))
        plain = jnp.logical_and(jnp.logical_not(far), jnp.logical_not(any_split))

        @pl.when(fast)
        def _():
            attend_tile(kt, cur_ref, next_ref, general_bias=False, tie_split=False)

        @pl.when(plain)
        def _():
            attend_tile(kt, cur_ref, next_ref, general_bias=True, tie_split=False)

        @pl.when(any_split)
        def _():
            attend_tile(kt, cur_ref, next_ref, general_bias=True, tie_split=True)

    tile_pairs(attend)

    for h in range(DSA_HEADS):
        hl = head_lanes(h)
        ot_ref[h * DSA_HEAD_DIM:(h + 1) * DSA_HEAD_DIM, :] = acc_ref[:, hl] / l_ref[0:1, hl]
    o_ref[0] = (ot_ref[...].T * gate_ref[0]).astype(o_ref.dtype)


def _dsa(rel_bias, qcw, qiw, kc, ki, v1, wi, pos_col, pos_row, gate, tq):
    b, s, _ = kc.shape
    n_sel = min(TOPK_MAX, s // 4)
    vw_rows = v1.shape[1]
    hw = DSA_HEADS * tq
    tbl = rel_bias.astype(F32) * LOG2E
    tbl = tbl - tbl[REL_BUCKETS - 1:REL_BUCKETS, :]
    tbl_t = jnp.pad(tbl.T, ((0, 0), (0, LANES - REL_BUCKETS)))
    q_min = jnp.min(pos_row.reshape(b, s // tq, tq), axis=-1)
    k_max = jnp.max(pos_row.reshape(b, s // DSA_KEY_TILE, DSA_KEY_TILE), axis=-1)
    tk = DSA_KEY_TILE
    vmem = (2 * (2 * _nbytes((DSA_HEAD_DIM, hw), BF16) + 2 * _nbytes((s, LANES), BF16) + _nbytes((vw_rows, s), F32)
                 + _nbytes((vw_rows, tq), F32) + _nbytes((s, LANES), I32) + _nbytes((tq, DSA_WIDTH), F32)
                 + _nbytes((tq, DSA_WIDTH), BF16)) + _nbytes((s, tq), I32) + 2 * _nbytes((tk, hw), F32)
            + _nbytes((tk, hw), BF16) + _nbytes((DSA_HEAD_DIM, hw), F32) + _nbytes((DSA_WIDTH, tq), F32)
            + 32 * _nbytes((tk, tq), F32))
    return pl.pallas_call(
        functools.partial(_dsa_kernel, n_sel=n_sel, tq=tq),
        grid=(b, s // tq),
        in_specs=[pl.BlockSpec(memory_space=pltpu.SMEM), pl.BlockSpec(memory_space=pltpu.SMEM),
                  pl.BlockSpec((DSA_HEADS, LANES), lambda bi, i: (0, 0)),
                  pl.BlockSpec((1, 1, DSA_HEAD_DIM, hw), lambda bi, i: (bi, i, 0, 0)),
                  pl.BlockSpec((1, 1, IDX_DIM, hw), lambda bi, i: (bi, i, 0, 0)),
                  pl.BlockSpec((1, s, DSA_HEAD_DIM), lambda bi, i: (bi, 0, 0)),
                  pl.BlockSpec((1, s, IDX_DIM), lambda bi, i: (bi, 0, 0)),
                  pl.BlockSpec((1, vw_rows, s), lambda bi, i: (bi, 0, 0)),
                  pl.BlockSpec((1, IDX_HEADS, tq), lambda bi, i: (bi, 0, i)),
                  pl.BlockSpec((1, s, 1), lambda bi, i: (bi, 0, 0)),
                  pl.BlockSpec((1, 1, tq), lambda bi, i: (bi, 0, i)),
                  pl.BlockSpec((1, tq, DSA_WIDTH), lambda bi, i: (bi, i, 0))],
        out_specs=pl.BlockSpec((1, tq, DSA_WIDTH), lambda bi, i: (bi, i, 0)),
        out_shape=jax.ShapeDtypeStruct((b, s, DSA_WIDTH), BF16),
        scratch_shapes=[pltpu.VMEM((s, tq), I32),
                        pltpu.VMEM((s, tq), I16), pltpu.VMEM((s, tq), I16),
                        pltpu.VMEM((tk, hw), F32), pltpu.VMEM((tk, hw), F32),
                        pltpu.VMEM((tk, hw), F32),
                        pltpu.VMEM((DSA_HEAD_DIM, hw), F32),
                        pltpu.VMEM((DSA_WIDTH, tq), F32),
                        pltpu.VMEM((SUBLANES, hw), F32), pltpu.VMEM((SUBLANES, hw), F32),
                        pltpu.VMEM((SUBLANES, tq), I32)],
        compiler_params=_compiler_params(("parallel", "arbitrary"), vmem),
        name="dsa",
    )(q_min, k_max, tbl_t, qcw, qiw, kc, ki, v1, wi, pos_col, pos_row, gate)


def _merge_kernel(oa_ref, ob_ref, oc_ref, g_ref, x_ref, p_ref, wa_ref, wb_ref, wc_ref, wo_ref, wpg_ref, wple_ref,
                  gn_ref, xo_ref, ho_ref):
    d = D_MODEL
    ya = jnp.dot(oa_ref[...], wa_ref[0], preferred_element_type=F32)
    yb = jnp.dot(ob_ref[...], wb_ref[0], preferred_element_type=F32)
    yc = jnp.dot(oc_ref[...], wc_ref[0], preferred_element_type=F32)
    merged = g_ref[:, 0:d] * ya + g_ref[:, d:2 * d] * yb + g_ref[:, 2 * d:3 * d] * yc
    x1 = x_ref[...] + jnp.dot(merged.astype(BF16), wo_ref[0], preferred_element_type=F32)
    ple_gate = _sigmoid(jnp.dot(x1.astype(BF16), wpg_ref[0], preferred_element_type=F32))
    x2 = x1 + ple_gate * jnp.dot(p_ref[0].astype(BF16), wple_ref[0], preferred_element_type=F32)
    xo_ref[...] = x2
    ho_ref[...] = _rms(x2, gn_ref[...]).astype(ho_ref.dtype)


def _merge(oa, ob, oc, gates, x, p, layer, wa, wb, wc, wo, wpg, wple, g_next, h_dtype, tm=256):
    t, d = x.shape
    full = lambda a: pl.BlockSpec((1,) + a.shape[1:], lambda i: (layer, 0, 0), pipeline_mode=pl.Buffered(1))
    row = lambda a: pl.BlockSpec((tm, a.shape[1]), lambda i: (i, 0))
    weights = (wa, wb, wc, wo, wpg, wple)
    acts = (oa, ob, oc, gates, x)
    vmem = (sum(_nbytes(a.shape[1:], a.dtype) for a in weights)
            + 2 * (sum(_nbytes((tm, a.shape[1]), a.dtype) for a in acts) + _nbytes((tm, p.shape[2]), p.dtype)
                   + 2 * _nbytes((tm, d), F32)) + 8 * _nbytes((tm, d), F32))
    return pl.pallas_call(
        _merge_kernel,
        grid=(t // tm,),
        in_specs=[row(a) for a in acts] + [pl.BlockSpec((1, tm, p.shape[2]), lambda i: (layer, i, 0))]
        + [full(a) for a in weights] + [pl.BlockSpec((1, d), lambda i: (0, 0))],
        out_specs=[pl.BlockSpec((tm, d), lambda i: (i, 0)), pl.BlockSpec((tm, d), lambda i: (i, 0))],
        out_shape=[jax.ShapeDtypeStruct((t, d), F32), jax.ShapeDtypeStruct((t, d), h_dtype)],
        compiler_params=_compiler_params(("parallel",), vmem),
        name="merge",
    )(*acts, p, *weights, g_next.reshape(1, d))


def _rotate_half_cols(w):
    half = w.shape[-1] // 2
    return jnp.concatenate([-w[..., half:], w[..., :half]], axis=-1)


def _prep_weights(w_in, w_uq, w_ukv):
    depth = w_in.shape[0]
    seg = lambda name: w_in[:, :, SPLIT[name][0]:SPLIT[name][1]]
    z = lambda *shape: jnp.zeros((depth,) + shape, w_in.dtype)
    d = D_MODEL
    kr = seg("k_rope")
    rope_lo = MLA_NOPE
    rope_pad = MLA_QK_PAD - MLA_NOPE - MLA_ROPE
    w = {}
    w["mla_in"] = jnp.concatenate(
        [seg("c_q"), seg("c_kv"), z(d, rope_lo), kr, z(d, rope_pad), z(d, rope_lo), _rotate_half_cols(kr), z(d, rope_pad)], axis=-1)
    w["gate_a"] = seg("gate_a")
    w["z"] = seg("z")
    w["xbc"] = seg("xbc")
    w["small"] = jnp.concatenate([seg("dt"), z(d, LANES - SSM_HEADS)], axis=-1)
    w["k_c"] = seg("k_c")
    w["k_idx"] = seg("k_idx")
    w["gate_c"] = seg("gate_c")
    w["merge"] = seg("merge")
    w["q_c_t"] = jnp.swapaxes(seg("q_c") * (DSA_HEAD_DIM ** -0.5 * LOG2E), 1, 2)
    w["q_idx_t"] = jnp.swapaxes(seg("q_idx"), 1, 2)
    w["vw_t"] = jnp.swapaxes(jnp.concatenate([seg("v_c"), seg("w_idx")], axis=-1), 1, 2)
    w = {
        "proj": jnp.concatenate([w["mla_in"]] + [w[seg[0]] for seg in PROJ_SEGMENTS], axis=-1).astype(BF16),
        "proj_t": jnp.concatenate([w["q_c_t"], w["q_idx_t"], w["vw_t"]], axis=1).astype(BF16),
    }

    uq =w_uq.reshape(depth, MLA_Q_LORA, MLA_HEADS, MLA_NOPE + MLA_ROPE)
    nope, rope = uq[..., :MLA_NOPE], uq[..., MLA_NOPE:]
    zq = lambda width: jnp.zeros((depth, MLA_Q_LORA, MLA_HEADS, width), w_uq.dtype)
    hq = MLA_HEADS * MLA_QK_PAD
    uq1 = jnp.concatenate([nope, rope, zq(rope_pad)], axis=-1).reshape(depth, MLA_Q_LORA, hq)
    uq2 = jnp.concatenate([zq(MLA_NOPE), _rotate_half_cols(rope), zq(rope_pad)], axis=-1).reshape(depth, MLA_Q_LORA, hq)
    w["uq1_t"] = jnp.swapaxes(uq1, 1, 2).astype(BF16)
    w["uq2_t"] = jnp.swapaxes(uq2, 1, 2).astype(BF16)
    ukv = w_ukv.reshape(depth, MLA_KV_LORA, MLA_HEADS, MLA_NOPE + MLA_V)
    zk = jnp.zeros((depth, MLA_KV_LORA, MLA_HEADS, MLA_QK_PAD - MLA_NOPE), w_ukv.dtype)
    w["uk"] = jnp.concatenate([ukv[..., :MLA_NOPE], zk], axis=-1).reshape(depth, MLA_KV_LORA, hq).astype(BF16)
    w["uv_t"] = jnp.swapaxes(ukv[..., MLA_NOPE:].reshape(depth, MLA_KV_LORA, MLA_WIDTH), 1, 2).astype(BF16)
    return w


def _rope_multipliers(positions):
    b, s = positions.shape
    inv_freq = 1.0 / (ROPE_THETA ** (jnp.arange(0, MLA_ROPE, 2, dtype=F32) / MLA_ROPE))
    ang = positions.astype(F32)[..., None] * inv_freq
    cos, sin = jnp.cos(ang), jnp.sin(ang)
    pad = MLA_QK_PAD - MLA_NOPE - MLA_ROPE
    m1 = jnp.concatenate([jnp.ones((b, s, MLA_NOPE), F32), cos, cos, jnp.zeros((b, s, pad), F32)], axis=-1)
    m2 = jnp.concatenate([jnp.zeros((b, s, MLA_NOPE), F32), sin, sin, jnp.zeros((b, s, pad), F32)], axis=-1)
    return m1, m2, jnp.swapaxes(m1, 1, 2), jnp.swapaxes(m2, 1, 2)


def kernel(x, p, positions, norm_g, w_in, mla_q_norm, w_uq, mla_kv_norm, w_ukv, conv_w, conv_b, dt_bias, a_log, d_skip, ssm_norm, w_br_a, w_br_b, w_br_c, w_out, rel_bias, w_ple, w_ple_gate, final_norm):
    b, s, d = x.shape
    depth = w_in.shape[0]
    t = b * s
    w = _prep_weights(w_in, w_uq, w_ukv)
    m1, m2, m1t, m2t = _rope_multipliers(positions)
    pos_col = positions.astype(I32).reshape(b, s, 1)
    pos_row = positions.astype(I32).reshape(b, 1, s)
    wbr_a, wbr_b, wbr_c = w_br_a.astype(BF16), w_br_b.astype(BF16), w_br_c.astype(BF16)
    wo, wpg, wple = w_out.astype(BF16), w_ple_gate.astype(BF16), w_ple.astype(BF16)

    q_norm = mla_q_norm.astype(F32).reshape(depth, 1, -1)
    kv_norm = mla_kv_norm.astype(F32).reshape(depth, 1, -1)
    p3d = p.reshape(depth, t, -1)

    x2d = x.reshape(t, d)
    h = _norm(x2d, norm_g[0], BF16)
    for i in range(depth):
        pr = _proj_all(h.reshape(b, s, d), i, w["proj"], w["proj_t"], (m1, m2, m1t, m2t), q_norm, kv_norm,
                       w["uq1_t"], w["uq2_t"], w["uk"], w["uv_t"], DSA_Q_BLOCK)
        o_a = _mla_attn(pr["mla_q_t"], pr["mla_k"], pr["mla_v_t1"], pr["gate_a"])
        o_b = _ssd(pr["xbc"], pr["z"], pr["small"], conv_w[i], conv_b[i], dt_bias[i], a_log[i], d_skip[i], ssm_norm[i])
        o_c = _dsa(rel_bias, pr["q_c_wide"], pr["q_idx_wide"], pr["k_c"], pr["k_idx"], pr["v_t1"], pr["w_idx_t"], pos_col, pos_row,
                   pr["gate_c"], DSA_Q_BLOCK)

        last = i == depth - 1
        g_next = final_norm if last else norm_g[i + 1]
        x2d, h = _merge(o_a.reshape(t, -1), o_b.reshape(t, -1), o_c.reshape(t, -1), pr["merge"].reshape(t, -1), x2d, p3d,
                        i, wbr_a, wbr_b, wbr_c, wo, wpg, wple, g_next, F32 if last else BF16)
    return h.reshape(b, s, d)
```

```python
import functools
import math

import jax
import jax.numpy as jnp
from jax import lax
from jax.experimental import pallas as pl
from jax.experimental.pallas import tpu as pltpu

F32 = jnp.float32
BF16 = jnp.bfloat16
I32 = jnp.int32
I16 = jnp.int16
HALF16 = 1 << 15

D_MODEL = 1024
PLE_DIM = 256
NORM_EPS = 1e-6

MLA_HEADS = 8
MLA_NOPE = 64
MLA_ROPE = 32
MLA_V = 64
MLA_Q_LORA = 384
MLA_KV_LORA = 256
MLA_WIDTH = MLA_HEADS * MLA_V
ROPE_THETA = 10000.0
MLA_QK_PAD = 128
LOG2E = math.log2(math.e)
MLA_Q_SCALE = (MLA_NOPE + MLA_ROPE) ** -0.5 * LOG2E

SSM_HEADS = 16
SSM_HEAD_DIM = 64
SSM_INNER = SSM_HEADS * SSM_HEAD_DIM
SSM_GROUPS = 2
SSM_STATE = 128
SSM_CONV = 4
SSM_CHUNK = 128
SSM_CONV_DIM = SSM_INNER + 2 * SSM_GROUPS * SSM_STATE
SSM_HEADS_PER_GROUP = SSM_HEADS // SSM_GROUPS
ONES_ROWS = 16
MLA_V1_ROWS = MLA_V + ONES_ROWS
SSM_CONV_TAIL = 16

DSA_HEADS = 8
DSA_HEAD_DIM = 64
DSA_WIDTH = DSA_HEADS * DSA_HEAD_DIM
IDX_HEADS = 8
IDX_DIM = 64
TOPK_MAX = 256
DSA_KEY_TILE = 128
DSA_Q_BLOCK = 256

REL_BUCKETS = 32
REL_MAX_DIST = 128
N_BRANCHES = 3

LANES = 128
SUBLANES = 8
VMEM_LIMIT_CAP = 56 * 1024 * 1024
INT_MIN = -(2 ** 31)
NEG_BIG = -1e30

SPLIT_SIZES = (
    MLA_Q_LORA, MLA_KV_LORA, MLA_ROPE, MLA_WIDTH, SSM_INNER, SSM_CONV_DIM, SSM_HEADS, DSA_WIDTH,
    DSA_HEAD_DIM, DSA_HEAD_DIM, IDX_HEADS * IDX_DIM, IDX_DIM, IDX_HEADS, DSA_WIDTH,
    N_BRANCHES * D_MODEL,
)
SPLIT_NAMES = ("c_q", "c_kv", "k_rope", "gate_a", "z", "xbc", "dt", "q_c", "k_c", "v_c", "q_idx",
               "k_idx", "w_idx", "gate_c", "merge")


def _split_bounds():
    out, off = {}, 0
    for name, size in zip(SPLIT_NAMES, SPLIT_SIZES, strict=True):
        out[name] = (off, off + size)
        off += size
    return out


SPLIT = _split_bounds()


def _t5_large_thresholds():
    exact = REL_BUCKETS // 2
    thr = []
    for j in range(1, REL_BUCKETS - exact):
        thr.append(int(math.ceil(exact * (REL_MAX_DIST / exact) ** (j / (REL_BUCKETS - exact)) - 1e-9)))
    return tuple(thr)


T5_EXACT = REL_BUCKETS // 2
T5_LARGE_THR = _t5_large_thresholds()
T5_FAR = T5_LARGE_THR[-1]


def _compiler_params(semantics, vmem_bytes):
    limit = int(min(VMEM_LIMIT_CAP, max(32 * 1024 * 1024, vmem_bytes)))
    return pltpu.CompilerParams(dimension_semantics=semantics, vmem_limit_bytes=limit)


def _nbytes(shape, dtype):
    return math.prod(shape) * jnp.dtype(dtype).itemsize


def _sigmoid(x):
    return 1.0 / (1.0 + jnp.exp(-x))


def _rms(x, g):
    return x * lax.rsqrt(jnp.mean(x * x, axis=-1, keepdims=True) + NORM_EPS) * g


def _norm_kernel(x_ref, g_ref, o_ref):
    o_ref[...] = _rms(x_ref[...], g_ref[...]).astype(o_ref.dtype)


def _norm(x2d, g, out_dtype, tm=512):
    t, d = x2d.shape
    return pl.pallas_call(
        _norm_kernel,
        grid=(t // tm,),
        in_specs=[pl.BlockSpec((tm, d), lambda i: (i, 0)), pl.BlockSpec((1, d), lambda i: (0, 0))],
        out_specs=pl.BlockSpec((tm, d), lambda i: (i, 0)),
        out_shape=jax.ShapeDtypeStruct((t, d), out_dtype),
        compiler_params=_compiler_params(("parallel",), 4 * _nbytes((tm, d), F32)),
        name="rms_norm",
    )(x2d, g.reshape(1, d))


MLA_IN_WIDTH = MLA_Q_LORA + MLA_KV_LORA + 2 * LANES
PROJ_SEGMENTS = (
    ("small", LANES, None, F32),
    ("gate_a", MLA_WIDTH, "silu", BF16),
    ("gate_c", DSA_WIDTH, "silu", BF16),
    ("z", SSM_INNER, "silu", BF16),
    ("xbc", SSM_CONV_DIM, None, BF16),
    ("merge", N_BRANCHES * D_MODEL, "sigmoid", BF16),
    ("k_c", DSA_HEAD_DIM, None, BF16),
    ("k_idx", IDX_DIM, None, BF16),
)
PROJ_CHUNK = 512
PROJ_T_ROWS = (DSA_WIDTH, IDX_HEADS * IDX_DIM, DSA_HEAD_DIM + IDX_HEADS)


def _proj_all_kernel(h_ref, w_ref, wt_ref, m1_ref, m2_ref, m1t_ref, m2t_ref, qn_ref, kvn_ref, wq1t_ref, wq2t_ref, wk_ref,
                     wvt_ref, *out_refs):
    h = h_ref[0]
    n_seg = len(PROJ_SEGMENTS)
    a = jnp.dot(h, w_ref[0, :, 0:MLA_IN_WIDTH], preferred_element_type=F32)
    _mla_prep(a, m1_ref, m2_ref, m1t_ref, m2t_ref, qn_ref, kvn_ref, wq1t_ref, wq2t_ref, wk_ref, wvt_ref, *out_refs[n_seg + 4:])
    off = MLA_IN_WIDTH
    for (_, width, act, _), o_ref in zip(PROJ_SEGMENTS, out_refs[:n_seg], strict=True):
        for c in range(0, width, PROJ_CHUNK):
            cw = min(PROJ_CHUNK, width - c)
            y = jnp.dot(h, w_ref[0, :, off + c:off + c + cw], preferred_element_type=F32)
            if act == "silu":
                y = y * _sigmoid(y)
            elif act == "sigmoid":
                y = _sigmoid(y)
            o_ref[0, :, c:c + cw] = y.astype(o_ref.dtype)
        off += width
    qcw_ref, qiw_ref, v1_ref, wi_ref = out_refs[n_seg:n_seg + 4]
    nt = (((1,), (1,)), ((), ()))
    tq = h.shape[0]
    row = 0
    for o_ref, heads, dim in ((qcw_ref, DSA_HEADS, DSA_HEAD_DIM), (qiw_ref, IDX_HEADS, IDX_DIM)):
        y = lax.dot_general(wt_ref[0, row:row + heads * dim, :], h, nt, preferred_element_type=F32)
        for hd in range(heads):
            o_ref[0, 0, :, hd * tq:(hd + 1) * tq] = y[hd * dim:(hd + 1) * dim, :].astype(o_ref.dtype)
        row += heads * dim
    vw = lax.dot_general(wt_ref[0, row:row + PROJ_T_ROWS[2], :], h, nt, preferred_element_type=F32)
    v1_ref[0, 0:DSA_HEAD_DIM, :] = vw[0:DSA_HEAD_DIM, :].astype(v1_ref.dtype)
    v1_ref[0, DSA_HEAD_DIM:, :] = jnp.ones((ONES_ROWS, tq), v1_ref.dtype)
    wi_ref[0] = vw[DSA_HEAD_DIM:, :]


def _proj_all(h3d, layer, w, wt, rope, qn, kvn, wq1t, wq2t, wk, wvt, tq):
    b, s, k = h3d.shape
    n_all = MLA_IN_WIDTH + sum(seg[1] for seg in PROJ_SEGMENTS)
    hq = MLA_HEADS * MLA_QK_PAD
    v1_rows = DSA_HEAD_DIM + ONES_ROWS
    out_shapes = [jax.ShapeDtypeStruct((b, s, width), dt) for _, width, _, dt in PROJ_SEGMENTS]
    out_specs = [pl.BlockSpec((1, tq, width), lambda bi, i: (bi, i, 0)) for _, width, _, _ in PROJ_SEGMENTS]
    for heads, dim in ((DSA_HEADS, DSA_HEAD_DIM), (IDX_HEADS, IDX_DIM)):
        out_shapes.append(jax.ShapeDtypeStruct((b, s // tq, dim, heads * tq), BF16))
        out_specs.append(pl.BlockSpec((1, 1, dim, heads * tq), lambda bi, i: (bi, i, 0, 0)))
    for rows, dt in ((v1_rows, BF16), (IDX_HEADS, F32)):
        out_shapes.append(jax.ShapeDtypeStruct((b, rows, s), dt))
        out_specs.append(pl.BlockSpec((1, rows, tq), lambda bi, i: (bi, 0, i)))
    for rows in (hq, None, MLA_HEADS * MLA_V1_ROWS):
        if rows is None:
            out_shapes.append(jax.ShapeDtypeStruct((b, s, hq), BF16))
            out_specs.append(pl.BlockSpec((1, tq, hq), lambda bi, i: (bi, i, 0)))
        else:
            out_shapes.append(jax.ShapeDtypeStruct((b, rows, s), BF16))
            out_specs.append(pl.BlockSpec((1, rows, tq), lambda bi, i: (bi, 0, i)))
    resident = pl.Buffered(1)
    weights = (w, wt, qn, kvn, wq1t, wq2t, wk, wvt)
    layer_spec = lambda a: pl.BlockSpec((1,) + a.shape[1:], lambda bi, i: (layer,) + (0,) * (a.ndim - 1), pipeline_mode=resident)
    m1, m2, m1t, m2t = rope
    vmem = (sum(_nbytes(a.shape[1:], a.dtype) for a in weights) + 2 * _nbytes((tq, k), BF16) + 8 * _nbytes((tq, LANES), F32)
            + 2 * sum(_nbytes((tq, width), dt) for _, width, _, dt in PROJ_SEGMENTS)
            + 2 * (2 * _nbytes((DSA_HEAD_DIM, DSA_HEADS * tq), BF16) + _nbytes((v1_rows + IDX_HEADS, tq), F32))
            + 2 * (2 * _nbytes((tq, hq), BF16) + _nbytes((MLA_HEADS * MLA_V1_ROWS, tq), BF16))
            + 8 * _nbytes((tq, PROJ_CHUNK), F32) + 6 * _nbytes((tq, hq), F32))
    outs = pl.pallas_call(
        _proj_all_kernel,
        grid=(b, s // tq),
        in_specs=[pl.BlockSpec((1, tq, k), lambda bi, i: (bi, i, 0)), layer_spec(w), layer_spec(wt),
                  pl.BlockSpec((1, tq, LANES), lambda bi, i: (bi, i, 0)), pl.BlockSpec((1, tq, LANES), lambda bi, i: (bi, i, 0)),
                  pl.BlockSpec((1, LANES, tq), lambda bi, i: (bi, 0, i)), pl.BlockSpec((1, LANES, tq), lambda bi, i: (bi, 0, i))]
        + [layer_spec(a) for a in weights[2:]],
        out_specs=out_specs,
        out_shape=out_shapes,
        compiler_params=_compiler_params(("parallel", "parallel"), vmem),
        name="proj_all",
    )(h3d, w, wt, m1, m2, m1t, m2t, *weights[2:])
    names = [seg[0] for seg in PROJ_SEGMENTS] + ["q_c_wide", "q_idx_wide", "v_t1", "w_idx_t", "mla_q_t", "mla_k", "mla_v_t1"]
    return dict(zip(names, outs, strict=True))


def _mla_prep(a, m1_ref, m2_ref, m1t_ref, m2t_ref, qn_ref, kvn_ref, wq1t_ref, wq2t_ref, wk_ref, wvt_ref, qt_ref, k_ref, vt_ref):
    c_q = a[:, :MLA_Q_LORA]
    c_kv = a[:, MLA_Q_LORA:MLA_Q_LORA + MLA_KV_LORA]
    kr1 = a[:, MLA_Q_LORA + MLA_KV_LORA:MLA_Q_LORA + MLA_KV_LORA + LANES]
    kr2 = a[:, MLA_Q_LORA + MLA_KV_LORA + LANES:]
    cqn = _rms(c_q, qn_ref[0]).astype(BF16)
    ckvn = _rms(c_kv, kvn_ref[0]).astype(BF16)
    nt = (((1,), (1,)), ((), ()))
    qa_t = lax.dot_general(wq1t_ref[0], cqn, nt, preferred_element_type=F32)
    qb_t = lax.dot_general(wq2t_ref[0], cqn, nt, preferred_element_type=F32)
    kn = jnp.dot(ckvn, wk_ref[0], preferred_element_type=F32)
    kr = kr1 * m1_ref[0] + kr2 * m2_ref[0]
    m1t = m1t_ref[0] * MLA_Q_SCALE
    m2t = m2t_ref[0] * MLA_Q_SCALE
    for h in range(MLA_HEADS):
        sl = slice(h * MLA_QK_PAD, (h + 1) * MLA_QK_PAD)
        qt_ref[0, sl, :] = (qa_t[sl, :] * m1t + qb_t[sl, :] * m2t).astype(qt_ref.dtype)
        k_ref[0, :, sl] = (kn[:, sl] + kr).astype(k_ref.dtype)
    v_t = lax.dot_general(wvt_ref[0], ckvn, nt, preferred_element_type=F32)
    ones = jnp.ones((ONES_ROWS, v_t.shape[1]), vt_ref.dtype)
    for h in range(MLA_HEADS):
        base = h * MLA_V1_ROWS
        vt_ref[0, base:base + MLA_V, :] = v_t[h * MLA_V:(h + 1) * MLA_V, :].astype(vt_ref.dtype)
        vt_ref[0, base + MLA_V:base + MLA_V1_ROWS, :] = ones


def _mla_attn_kernel(qt_ref, k_ref, vt_ref, g_ref, o_ref, m_ref, l_ref, acc_ref, s0_ref, s1_ref, *, tq, tk):
    assert tq == 2 * tk
    qi = pl.program_id(1)
    krow = lax.broadcasted_iota(I32, (tk, tq), 0)
    qcol = lax.broadcasted_iota(I32, (tk, tq), 1)
    tiles_per_q = tq // tk
    m_ref[...] = jnp.full(m_ref.shape, -jnp.inf, F32)
    l_ref[...] = jnp.zeros(l_ref.shape, F32)
    acc_ref[...] = jnp.zeros(acc_ref.shape, F32)

    staged = (s0_ref, s1_ref)
    n_tiles = (qi + 1) * tiles_per_q

    def head_lanes(h):
        return slice(h * tq, (h + 1) * tq)

    def stage(j, dst_ref):
        ks = pl.multiple_of(j * tk, tk)
        for h in range(MLA_HEADS):
            k = k_ref[0, pl.ds(ks, tk), h * MLA_QK_PAD:(h + 1) * MLA_QK_PAD]
            dst_ref[:, head_lanes(h)] = jnp.dot(k, qt_ref[0, h * MLA_QK_PAD:(h + 1) * MLA_QK_PAD, :],
                                                preferred_element_type=F32)

    def tile(j, cur_ref, next_ref, diagonal):
        ks = pl.multiple_of(j * tk, tk)
        stage(jnp.minimum(j + 1, n_tiles - 1), next_ref)
        m_all = m_ref[...]
        l_all = l_ref[...]
        if diagonal:
            causal = ks + krow <= qi * tq + qcol
        for h in range(MLA_HEADS):
            s = cur_ref[:, head_lanes(h)]
            if diagonal:
                s = jnp.where(causal, s, -jnp.inf)
            m_old = m_all[h:h + 1, :]
            m_new = jnp.maximum(m_old, jnp.max(s, axis=0, keepdims=True))
            alpha = jnp.exp2(m_old - m_new)
            p = jnp.exp2(s - m_new).astype(BF16)
            m_ref[h:h + 1, :] = m_new
            pv = jnp.dot(vt_ref[0, h * MLA_V1_ROWS:(h + 1) * MLA_V1_ROWS, pl.ds(ks, tk)], p, preferred_element_type=F32)
            hs = slice(h * MLA_V, (h + 1) * MLA_V)
            acc_ref[hs, :] = alpha * acc_ref[hs, :] + pv[0:MLA_V, :]
            l_ref[h:h + 1, :] = alpha * l_all[h:h + 1, :] + pv[MLA_V:MLA_V + 1, :]

    stage(0, staged[0])

    def full_pair(pi, carry):
        for half in range(2):
            tile(2 * pi + half, staged[half], staged[1 - half], diagonal=False)
        return carry

    lax.fori_loop(0, qi, full_pair, 0)
    for half in range(2):
        tile(2 * qi + half, staged[half], staged[1 - half], diagonal=True)
    for h in range(MLA_HEADS):
        hs = slice(h * MLA_V, (h + 1) * MLA_V)
        acc_ref[hs, :] = acc_ref[hs, :] / l_ref[h:h + 1, :]
    o_ref[0] = (acc_ref[...].T * g_ref[0]).astype(o_ref.dtype)


def _mla_attn(qt, k, vt, gate, tq=256, tk=128):
    b, s, hq = k.shape
    vmem = 2 * (_nbytes((hq, tq), BF16) + _nbytes((s, hq), BF16) + _nbytes((MLA_WIDTH, s), BF16)
                + _nbytes((tq, MLA_WIDTH), F32) + _nbytes((tq, MLA_WIDTH), BF16)) + _nbytes((MLA_WIDTH, tq), F32) + 16 * _nbytes((tk, tq), F32)
    return pl.pallas_call(
        functools.partial(_mla_attn_kernel, tq=tq, tk=tk),
        grid=(b, s // tq),
        in_specs=[pl.BlockSpec((1, hq, tq), lambda bi, i: (bi, 0, i)),
                  pl.BlockSpec((1, s, hq), lambda bi, i: (bi, 0, 0)),
                  pl.BlockSpec((1, MLA_HEADS * MLA_V1_ROWS, s), lambda bi, i: (bi, 0, 0)),
                  pl.BlockSpec((1, tq, MLA_WIDTH), lambda bi, i: (bi, i, 0))],
        out_specs=pl.BlockSpec((1, tq, MLA_WIDTH), lambda bi, i: (bi, i, 0)),
        out_shape=jax.ShapeDtypeStruct((b, s, MLA_WIDTH), BF16),
        scratch_shapes=[pltpu.VMEM((MLA_HEADS, tq), F32), pltpu.VMEM((MLA_HEADS, tq), F32),
                        pltpu.VMEM((MLA_WIDTH, tq), F32),
                        pltpu.VMEM((tk, MLA_HEADS * tq), F32), pltpu.VMEM((tk, MLA_HEADS * tq), F32)],
        compiler_params=_compiler_params(("parallel", "arbitrary"), vmem),
        name="mla_attn",
    )(qt, k, vt, gate)


def _ssd_kernel(xbc_ref, xprev_ref, zs_ref, sm_ref, cw_ref, cb_ref, dtb_ref, alog_ref, dskf_ref, nrm_ref, exp16_ref, exp32_ref,
                o_ref, state_ref, y_ref):
    c = pl.program_id(1)
    q = SSM_CHUNK

    @pl.when(c == 0)
    def _():
        state_ref[...] = jnp.zeros(state_ref.shape, F32)

    x_in = xbc_ref[0]
    tail = xprev_ref.shape[1]
    prev = jnp.where(c > 0, xprev_ref[0].astype(F32), 0.0).astype(x_in.dtype)
    win = jnp.concatenate([prev, x_in], axis=0)
    x = x_in.astype(F32)
    wrow = lax.broadcasted_iota(I32, (q, tail + q), 0)
    wcol = lax.broadcasted_iota(I32, (q, tail + q), 1)
    acc = cb_ref[...] + cw_ref[SSM_CONV - 1:SSM_CONV, :] * x
    for j in range(1, SSM_CONV):
        pick = jnp.where(wcol == wrow + (tail - j), 1.0, 0.0).astype(win.dtype)
        acc = acc + cw_ref[SSM_CONV - 1 - j:SSM_CONV - j, :] * jnp.dot(pick, win, preferred_element_type=F32)
    half = 0.5 * acc
    xc = half + half * jnp.tanh(half)

    xs = xc[:, :SSM_INNER]
    bm = xc[:, SSM_INNER:SSM_INNER + SSM_GROUPS * SSM_STATE]
    cm = xc[:, SSM_INNER + SSM_GROUPS * SSM_STATE:]

    pre = sm_ref[0] + dtb_ref[...]
    dt = jnp.maximum(pre, 0.0) + jnp.log1p(jnp.exp(-jnp.abs(pre)))
    a = -jnp.exp(alog_ref[...])
    row = lax.broadcasted_iota(I32, (q, q), 0)
    col = lax.broadcasted_iota(I32, (q, q), 1)
    lower = row >= col
    cum = jnp.dot(jnp.where(lower, 1.0, 0.0).astype(F32), dt * a, preferred_element_type=F32,
                  precision=lax.Precision.HIGHEST)
    cum_t = cum.T
    cum_last = cum[q - 1:q, :]
    per_head = jnp.concatenate([dt, jnp.exp(cum_last - cum), jnp.exp(cum)], axis=0).astype(BF16)
    spread = jnp.dot(per_head, exp16_ref[...], preferred_element_type=F32)
    dt_full, to_end_full, e_cum_full = spread[0:q], spread[q:2 * q], spread[2 * q:3 * q]
    chunk_decay_full = jnp.dot(jnp.broadcast_to(jnp.exp(cum_last), (SUBLANES, LANES)), exp32_ref[...],
                               preferred_element_type=F32, precision=lax.Precision.HIGHEST)[0:1]
    xdt = xs * dt_full
    xw = (xdt * to_end_full).astype(BF16)
    first_head = lax.broadcasted_iota(I32, (q, LANES), 1) < SSM_HEAD_DIM
    group_w = SSM_HEADS_PER_GROUP * SSM_HEAD_DIM
    heads_per_tile = LANES // SSM_HEAD_DIM

    for g in range(SSM_GROUPS):
        bg32 = bm[:, g * SSM_STATE:(g + 1) * SSM_STATE]
        bg = bg32.astype(BF16)
        cg = cm[:, g * SSM_STATE:(g + 1) * SSM_STATE].astype(BF16)
        cb = lax.dot_general(cg, bg, (((1,), (1,)), ((), ())), preferred_element_type=F32)
        gl = slice(g * group_w, (g + 1) * group_w)
        st = state_ref[g]
        y_off = jnp.dot(cg, st.astype(BF16), preferred_element_type=F32)
        state_ref[g] = st * chunk_decay_full[:, gl] + jnp.dot(bg32.T.astype(BF16), xw[:, gl], preferred_element_type=F32)
        for tile in range(group_w // LANES):
            h0 = g * SSM_HEADS_PER_GROUP + tile * heads_per_tile
            tl = slice(h0 * SSM_HEAD_DIM, h0 * SSM_HEAD_DIM + LANES)
            g_both = []
            for h in range(h0, h0 + heads_per_tile):
                diff = cum[:, h:h + 1] - cum_t[h:h + 1, :]
                decay_in = jnp.exp(jnp.where(lower, diff, -jnp.inf))
                g_both.append((cb * decay_in).astype(BF16))
            xdt_tile = xdt[:, tl]
            stacked = jnp.concatenate([jnp.where(first_head, xdt_tile, 0.0).astype(BF16),
                                       jnp.where(first_head, 0.0, xdt_tile).astype(BF16)], axis=0)
            y_diag = jnp.dot(jnp.concatenate(g_both, axis=1), stacked, preferred_element_type=F32)
            y = y_diag + y_off[:, tile * LANES:(tile + 1) * LANES] * e_cum_full[:, tl] + xs[:, tl] * dskf_ref[:, tl]
            y_ref[:, tl] = y * zs_ref[0, :, tl]

    o_ref[0] = _rms(y_ref[...], nrm_ref[...]).astype(o_ref.dtype)


def _ssd(xbc, zs, small, conv_w, conv_b, dt_bias, a_log, d_skip, ssm_norm):
    b, s, _ = xbc.shape
    q = SSM_CHUNK
    pad = lambda v: jnp.pad(v.astype(F32), (0, LANES - v.shape[0])).reshape(1, LANES)
    full = lambda shape: pl.BlockSpec(shape, lambda bi, c: (0,) * len(shape))
    blk = lambda width: pl.BlockSpec((1, q, width), lambda bi, c: (bi, c, 0))
    expand = (jnp.arange(LANES)[:, None] == jnp.arange(SSM_INNER)[None, :] // SSM_HEAD_DIM).astype(F32)
    d_skip_full = jnp.repeat(d_skip.astype(F32), SSM_HEAD_DIM).reshape(1, SSM_INNER)
    group_w = SSM_HEADS_PER_GROUP * SSM_HEAD_DIM
    vmem = (2 * (_nbytes((q, SSM_CONV_DIM), xbc.dtype) + _nbytes((q, SSM_INNER), zs.dtype) + _nbytes((q, LANES), F32)
                 + _nbytes((q, SSM_INNER), BF16) + _nbytes((LANES, SSM_INNER), BF16) + _nbytes((LANES, SSM_INNER), F32))
            + _nbytes((2 * q, SSM_CONV_DIM), xbc.dtype) + _nbytes((SSM_GROUPS, SSM_STATE, group_w), F32)
            + _nbytes((q, SSM_INNER), F32) + 10 * _nbytes((q, SSM_CONV_DIM), F32))
    return pl.pallas_call(
        _ssd_kernel,
        grid=(b, s // q),
        in_specs=[blk(SSM_CONV_DIM),
                  pl.BlockSpec((1, SSM_CONV_TAIL, SSM_CONV_DIM),
                               lambda bi, c: (bi, jnp.maximum(c * (q // SSM_CONV_TAIL) - 1, 0), 0)),
                  blk(SSM_INNER), blk(LANES), full((SSM_CONV, SSM_CONV_DIM)), full((1, SSM_CONV_DIM)),
                  full((1, LANES)), full((1, LANES)), full((1, SSM_INNER)), full((1, SSM_INNER)),
                  full((LANES, SSM_INNER)), full((LANES, SSM_INNER))],
        out_specs=blk(SSM_INNER),
        out_shape=jax.ShapeDtypeStruct((b, s, SSM_INNER), BF16),
        scratch_shapes=[pltpu.VMEM((SSM_GROUPS, SSM_STATE, group_w), F32),
                        pltpu.VMEM((q, SSM_INNER), F32)],
        compiler_params=_compiler_params(("parallel", "arbitrary"), vmem),
        name="ssd",
    )(xbc, xbc, zs, small, conv_w.astype(F32), conv_b.reshape(1, -1).astype(F32), pad(dt_bias), pad(a_log), d_skip_full,
      ssm_norm.reshape(1, -1).astype(F32), expand.astype(BF16), expand)


def _t5_bucket_of(n):
    steps_per_octave = (REL_BUCKETS - T5_EXACT) / math.log2(REL_MAX_DIST / T5_EXACT)
    x = jnp.maximum(n, T5_EXACT).astype(F32) * (1.0 / T5_EXACT)
    large = T5_EXACT + jnp.floor(jnp.log2(x) * steps_per_octave).astype(I32)
    return jnp.where(n <= T5_EXACT, n, jnp.minimum(large, REL_BUCKETS - 1))


def _dsa_kernel(qmin_ref, kmax_ref, tblt_ref, qcw_ref, qiw_ref, kc_ref, ki_ref, v1_ref, wi_ref, posk_ref, posq_ref, gate_ref, o_ref,
                key_ref, hi_ref, lo_ref, s0_ref, s1_ref, bias_ref, acc_ref, ot_ref, m_ref, l_ref, tie_ref, *, n_sel, tq):
    tk = DSA_KEY_TILE
    qi = pl.program_id(1)
    n_tiles = (qi + 1) * (tq // tk)
    krow = lax.broadcasted_iota(I32, (tk, tq), 0)
    qcol = lax.broadcasted_iota(I32, (tk, tq), 1)
    w = wi_ref[0]
    posq = posq_ref[0]

    def tile_start(kt):
        return pl.multiple_of(kt * tk, tk)

    def causal_of(ks):
        return (ks + krow) <= (qi * tq + qcol)

    def head_lanes(h):
        return slice(h * tq, (h + 1) * tq)

    staged = (s0_ref, s1_ref)

    def stage(k_ref, q_ref, kt, dst_ref):
        dst_ref[...] = jnp.dot(k_ref[0, pl.ds(tile_start(kt), tk), :], q_ref[0, 0], preferred_element_type=F32)

    def tile_pairs(tile_fn, peel_last):
        def pair(pi, last_pair):
            for half in range(2):
                tile_fn(2 * pi + half, staged[half], staged[1 - half], last_pair, last_pair and half == 1)

        def body(pi, carry):
            pair(pi, False)
            return carry

        if peel_last:
            lax.fori_loop(0, n_tiles // 2 - 1, body, 0)
            pair(n_tiles // 2 - 1, True)
        else:
            lax.fori_loop(0, n_tiles // 2, body, 0)

    stage(ki_ref, qiw_ref, 0, staged[0])

    def score_tile(kt, cur_ref, next_ref, last_pair, last_tile):
        ks = tile_start(kt)
        if last_tile:
            stage(kc_ref, qcw_ref, 0, next_ref)
        else:
            stage(ki_ref, qiw_ref, kt + 1, next_ref)
        sc = jnp.zeros((tk, tq), F32)
        for h in range(IDX_HEADS):
            sc = sc + w[h:h + 1, :] * jnp.maximum(cur_ref[:, head_lanes(h)], 0.0)
        sc = jnp.where(sc == 0.0, 0.0, sc)
        bits = lax.bitcast_convert_type(sc, I32)
        key = bits ^ ((bits >> 31) & 0x7FFFFFFF)
        if last_pair:
            key = jnp.where(causal_of(ks), key, INT_MIN)
        key_ref[pl.ds(ks, tk), :] = key
        hi_ref[pl.ds(ks, tk), :] = (key >> 16).astype(I16)
        lo_ref[pl.ds(ks, tk), :] = ((key & 0xFFFF) - HALF16).astype(I16)

    tile_pairs(score_tile, peel_last=True)

    def count16(half_ref, pred):
        def body(pi, cnt):
            for kt in (2 * pi, 2 * pi + 1):
                cnt = cnt + jnp.where(pred(half_ref[pl.ds(tile_start(kt), tk), :]), jnp.int16(1), jnp.int16(0))
            return cnt
        cnt = lax.fori_loop(0, n_tiles // 2, body, jnp.zeros((tk, tq), I16))
        rows = 2 * SUBLANES
        parts = [cnt[r:r + rows, :] for r in range(0, tk, rows)]
        while len(parts) > 1:
            parts = [a + b for a, b in zip(parts[0::2], parts[1::2], strict=True)]
        return jnp.sum(parts[0].astype(I32), axis=0, keepdims=True)

    def nth_largest16(half_ref, rank):
        def bit_step(i, cand):
            trial = cand | lax.shift_left(jnp.int32(1), 15 - i)
            thr = (trial - HALF16).astype(I16)
            tot = count16(half_ref, lambda v: v >= thr)
            return jnp.where(tot >= rank, trial, cand)
        return lax.fori_loop(0, 16, bit_step, jnp.zeros((1, tq), I32)) - HALF16

    tau_hi = nth_largest16(hi_ref, n_sel)
    tau_hi16 = tau_hi.astype(I16)
    above_hi = count16(hi_ref, lambda v: v > tau_hi16)

    def mask_low(kt, carry):
        rows = pl.ds(tile_start(kt), tk)
        lo_ref[rows, :] = jnp.where(hi_ref[rows, :] == tau_hi16, lo_ref[rows, :], jnp.int16(-HALF16))
        return carry

    lax.fori_loop(0, n_tiles, mask_low, 0)
    tau_lo = nth_largest16(lo_ref, n_sel - above_hi)
    tau_lo16 = tau_lo.astype(I16)
    tau = lax.shift_left(tau_hi, 16) | (tau_lo + HALF16)
    n_gt = above_hi + count16(lo_ref, lambda v: v > tau_lo16)
    n_ge = above_hi + count16(lo_ref, lambda v: v >= tau_lo16)
    need = n_sel - n_gt

    m_ref[...] = jnp.full(m_ref.shape, NEG_BIG, F32)
    l_ref[...] = jnp.zeros(l_ref.shape, F32)
    acc_ref[...] = jnp.zeros(acc_ref.shape, F32)
    tie_ref[...] = jnp.zeros(tie_ref.shape, I32)
    lower_incl = jnp.where(lax.broadcasted_iota(I32, (tk, tk), 0) >= lax.broadcasted_iota(I32, (tk, tk), 1),
                           1.0, 0.0).astype(BF16)

    splits =jnp.where((n_ge > n_sel) & (tau != INT_MIN), 1.0, 0.0)
    any_split = jnp.max(splits) > 0.0

    def attend_tile(kt, cur_ref, next_ref, general_bias, tie_split):
        ks = tile_start(kt)
        key = key_ref[pl.ds(ks, tk), :]
        if tie_split:
            eq = key == tau
            prefix = jnp.dot(lower_incl, jnp.where(eq, 1.0, 0.0).astype(BF16), preferred_element_type=F32)
            tie_before = tie_ref[0:1, :]
            tie_rank = tie_before + prefix.astype(I32)
            sel = ((key > tau) | (eq & (tie_rank <= need))) & causal_of(ks)
            tie_ref[0:1, :] = tie_before + prefix[tk - 1:tk, :].astype(I32)
        else:
            sel = (key >= tau) & causal_of(ks)

        if general_bias:
            posk = posk_ref[0, pl.ds(ks, tk), :]
            bucket = _t5_bucket_of(jnp.maximum(posq - posk, 0))
            for h in range(DSA_HEADS):
                table = jnp.broadcast_to(tblt_ref[h:h + 1, :], (tk, LANES))
                for c in range(tq // LANES):
                    bias_ref[:, h * tq + c * LANES:h * tq + (c + 1) * LANES] = jnp.take_along_axis(
                        table, bucket[:, c * LANES:(c + 1) * LANES], axis=1)
        stage(kc_ref, qcw_ref, jnp.minimum(kt + 1, n_tiles - 1), next_ref)
        m_all = m_ref[0:1, :]
        l_all = l_ref[0:1, :]
        v1 = v1_ref[0, :, pl.ds(ks, tk)]
        for h in range(DSA_HEADS):
            hl = head_lanes(h)
            s = cur_ref[:, hl]
            if general_bias:
                s = s + bias_ref[:, hl]
            s = jnp.where(sel, s, NEG_BIG)
            m_old = m_all[:, hl]
            m_new = jnp.maximum(m_old, jnp.max(s, axis=0, keepdims=True))
            m_ref[0:1, hl] = m_new
            alpha = jnp.exp2(m_old - m_new)
            pv = jnp.dot(v1, jnp.exp2(s - m_new).astype(BF16), preferred_element_type=F32)
            acc_ref[:, hl] = alpha * acc_ref[:, hl] + pv[0:DSA_HEAD_DIM, :]
            l_ref[0:1, hl] = alpha * l_all[:, hl] + pv[DSA_HEAD_DIM:DSA_HEAD_DIM + 1, :]

    bi = pl.program_id(0)

    def attend(kt, cur_ref, next_ref, last_pair, last_tile):
        far = (qmin_ref[bi, qi] - kmax_ref[bi, kt]) >= T5_FAR
        fast = jnp.logical_and(far, jnp.logical_not(any_split))
        plain = jnp.logical_and(jnp.logical_not(far), jnp.logical_not(any_split))

        @pl.when(fast)
        def _():
            attend_tile(kt, cur_ref, next_ref, general_bias=False, tie_split=False)

        @pl.when(plain)
        def _():
            attend_tile(kt, cur_ref, next_ref, general_bias=True, tie_split=False)

        @pl.when(any_split)
        def _():
            attend_tile(kt, cur_ref, next_ref, general_bias=True, tie_split=True)

    tile_pairs(attend, peel_last=False)

    for h in range(DSA_HEADS):
        hl = head_lanes(h)
        ot_ref[h * DSA_HEAD_DIM:(h + 1) * DSA_HEAD_DIM, :] = acc_ref[:, hl] / l_ref[0:1, hl]
    o_ref[0] = (ot_ref[...].T * gate_ref[0]).astype(o_ref.dtype)


def _dsa(rel_bias, qcw, qiw, kc, ki, v1, wi, pos_col, pos_row, gate, tq):
    b, s, _ = kc.shape
    n_sel = min(TOPK_MAX, s // 4)
    vw_rows = v1.shape[1]
    hw = DSA_HEADS * tq
    tbl = rel_bias.astype(F32) * LOG2E
    tbl = tbl - tbl[REL_BUCKETS - 1:REL_BUCKETS, :]
    tbl_t = jnp.pad(tbl.T, ((0, 0), (0, LANES - REL_BUCKETS)))
    q_min = jnp.min(pos_row.reshape(b, s // tq, tq), axis=-1)
    k_max = jnp.max(pos_row.reshape(b, s // DSA_KEY_TILE, DSA_KEY_TILE), axis=-1)
    tk = DSA_KEY_TILE
    vmem = (2 * (2 * _nbytes((DSA_HEAD_DIM, hw), BF16) + 2 * _nbytes((s, LANES), BF16) + _nbytes((vw_rows, s), F32)
                 + _nbytes((vw_rows, tq), F32) + _nbytes((s, LANES), I32) + _nbytes((tq, DSA_WIDTH), F32)
                 + _nbytes((tq, DSA_WIDTH), BF16)) + _nbytes((s, tq), I32) + 2 * _nbytes((tk, hw), F32)
            + _nbytes((tk, hw), BF16) + _nbytes((DSA_HEAD_DIM, hw), F32) + _nbytes((DSA_WIDTH, tq), F32)
            + 32 * _nbytes((tk, tq), F32))
    return pl.pallas_call(
        functools.partial(_dsa_kernel, n_sel=n_sel, tq=tq),
        grid=(b, s // tq),
        in_specs=[pl.BlockSpec(memory_space=pltpu.SMEM), pl.BlockSpec(memory_space=pltpu.SMEM),
                  pl.BlockSpec((DSA_HEADS, LANES), lambda bi, i: (0, 0)),
                  pl.BlockSpec((1, 1, DSA_HEAD_DIM, hw), lambda bi, i: (bi, i, 0, 0)),
                  pl.BlockSpec((1, 1, IDX_DIM, hw), lambda bi, i: (bi, i, 0, 0)),
                  pl.BlockSpec((1, s, DSA_HEAD_DIM), lambda bi, i: (bi, 0, 0)),
                  pl.BlockSpec((1, s, IDX_DIM), lambda bi, i: (bi, 0, 0)),
                  pl.BlockSpec((1, vw_rows, s), lambda bi, i: (bi, 0, 0)),
                  pl.BlockSpec((1, IDX_HEADS, tq), lambda bi, i: (bi, 0, i)),
                  pl.BlockSpec((1, s, 1), lambda bi, i: (bi, 0, 0)),
                  pl.BlockSpec((1, 1, tq), lambda bi, i: (bi, 0, i)),
                  pl.BlockSpec((1, tq, DSA_WIDTH), lambda bi, i: (bi, i, 0))],
        out_specs=pl.BlockSpec((1, tq, DSA_WIDTH), lambda bi, i: (bi, i, 0)),
        out_shape=jax.ShapeDtypeStruct((b, s, DSA_WIDTH), BF16),
        scratch_shapes=[pltpu.VMEM((s, tq), I32),
                        pltpu.VMEM((s, tq), I16), pltpu.VMEM((s, tq), I16),
                        pltpu.VMEM((tk, hw), F32), pltpu.VMEM((tk, hw), F32),
                        pltpu.VMEM((tk, hw), F32),
                        pltpu.VMEM((DSA_HEAD_DIM, hw), F32),
                        pltpu.VMEM((DSA_WIDTH, tq), F32),
                        pltpu.VMEM((SUBLANES, hw), F32), pltpu.VMEM((SUBLANES, hw), F32),
                        pltpu.VMEM((SUBLANES, tq), I32)],
        compiler_params=_compiler_params(("parallel", "arbitrary"), vmem),
        name="dsa",
    )(q_min, k_max, tbl_t, qcw, qiw, kc, ki, v1, wi, pos_col, pos_row, gate)


def _merge_kernel(oa_ref, ob_ref, oc_ref, g_ref, x_ref, p_ref, wa_ref, wb_ref, wc_ref, wo_ref, wpg_ref, wple_ref,
                  gn_ref, xo_ref, ho_ref):
    d = D_MODEL
    ya = jnp.dot(oa_ref[...], wa_ref[0], preferred_element_type=F32)
    yb = jnp.dot(ob_ref[...], wb_ref[0], preferred_element_type=F32)
    yc = jnp.dot(oc_ref[...], wc_ref[0], preferred_element_type=F32)
    merged = g_ref[:, 0:d] * ya + g_ref[:, d:2 * d] * yb + g_ref[:, 2 * d:3 * d] * yc
    x1 = x_ref[...] + jnp.dot(merged.astype(BF16), wo_ref[0], preferred_element_type=F32)
    ple_gate = _sigmoid(jnp.dot(x1.astype(BF16), wpg_ref[0], preferred_element_type=F32))
    x2 = x1 + ple_gate * jnp.dot(p_ref[0].astype(BF16), wple_ref[0], preferred_element_type=F32)
    xo_ref[...] = x2
    ho_ref[...] = _rms(x2, gn_ref[...]).astype(ho_ref.dtype)


def _merge(oa, ob, oc, gates, x, p, layer, wa, wb, wc, wo, wpg, wple, g_next, h_dtype, tm=512):
    t, d = x.shape
    full = lambda a: pl.BlockSpec((1,) + a.shape[1:], lambda i: (layer, 0, 0), pipeline_mode=pl.Buffered(1))
    row = lambda a: pl.BlockSpec((tm, a.shape[1]), lambda i: (i, 0))
    weights = (wa, wb, wc, wo, wpg, wple)
    acts = (oa, ob, oc, gates, x)
    vmem = (sum(_nbytes(a.shape[1:], a.dtype) for a in weights)
            + 2 * (sum(_nbytes((tm, a.shape[1]), a.dtype) for a in acts) + _nbytes((tm, p.shape[2]), p.dtype)
                   + 2 * _nbytes((tm, d), F32)) + 8 * _nbytes((tm, d), F32))
    return pl.pallas_call(
        _merge_kernel,
        grid=(t // tm,),
        in_specs=[row(a) for a in acts] + [pl.BlockSpec((1, tm, p.shape[2]), lambda i: (layer, i, 0))]
        + [full(a) for a in weights] + [pl.BlockSpec((1, d), lambda i: (0, 0))],
        out_specs=[pl.BlockSpec((tm, d), lambda i: (i, 0)), pl.BlockSpec((tm, d), lambda i: (i, 0))],
        out_shape=[jax.ShapeDtypeStruct((t, d), F32), jax.ShapeDtypeStruct((t, d), h_dtype)],
        compiler_params=_compiler_params(("parallel",), vmem),
        name="merge",
    )(*acts, p, *weights, g_next.reshape(1, d))


def _rotate_half_cols(w):
    half = w.shape[-1] // 2
    return jnp.concatenate([-w[..., half:], w[..., :half]], axis=-1)


def _prep_weights(w_in, w_uq, w_ukv):
    depth = w_in.shape[0]
    seg = lambda name: w_in[:, :, SPLIT[name][0]:SPLIT[name][1]]
    z = lambda *shape: jnp.zeros((depth,) + shape, w_in.dtype)
    d = D_MODEL
    kr = seg("k_rope")
    rope_lo = MLA_NOPE
    rope_pad = MLA_QK_PAD - MLA_NOPE - MLA_ROPE
    w = {}
    w["mla_in"] = jnp.concatenate(
        [seg("c_q"), seg("c_kv"), z(d, rope_lo), kr, z(d, rope_pad), z(d, rope_lo), _rotate_half_cols(kr), z(d, rope_pad)], axis=-1)
    w["gate_a"] = seg("gate_a")
    w["z"] = seg("z")
    w["xbc"] = seg("xbc")
    w["small"] = jnp.concatenate([seg("dt"), z(d, LANES - SSM_HEADS)], axis=-1)
    w["k_c"] = seg("k_c")
    w["k_idx"] = seg("k_idx")
    w["gate_c"] = seg("gate_c")
    w["merge"] = seg("merge")
    w["q_c_t"] = jnp.swapaxes(seg("q_c") * (DSA_HEAD_DIM ** -0.5 * LOG2E), 1, 2)
    w["q_idx_t"] = jnp.swapaxes(seg("q_idx"), 1, 2)
    w["vw_t"] = jnp.swapaxes(jnp.concatenate([seg("v_c"), seg("w_idx")], axis=-1), 1, 2)
    w = {
        "proj": jnp.concatenate([w["mla_in"]] + [w[seg[0]] for seg in PROJ_SEGMENTS], axis=-1).astype(BF16),
        "proj_t": jnp.concatenate([w["q_c_t"], w["q_idx_t"], w["vw_t"]], axis=1).astype(BF16),
    }

    uq =w_uq.reshape(depth, MLA_Q_LORA, MLA_HEADS, MLA_NOPE + MLA_ROPE)
    nope, rope = uq[..., :MLA_NOPE], uq[..., MLA_NOPE:]
    zq = lambda width: jnp.zeros((depth, MLA_Q_LORA, MLA_HEADS, width), w_uq.dtype)
    hq = MLA_HEADS * MLA_QK_PAD
    uq1 = jnp.concatenate([nope, rope, zq(rope_pad)], axis=-1).reshape(depth, MLA_Q_LORA, hq)
    uq2 = jnp.concatenate([zq(MLA_NOPE), _rotate_half_cols(rope), zq(rope_pad)], axis=-1).reshape(depth, MLA_Q_LORA, hq)
    w["uq1_t"] = jnp.swapaxes(uq1, 1, 2).astype(BF16)
    w["uq2_t"] = jnp.swapaxes(uq2, 1, 2).astype(BF16)
    ukv = w_ukv.reshape(depth, MLA_KV_LORA, MLA_HEADS, MLA_NOPE + MLA_V)
    zk = jnp.zeros((depth, MLA_KV_LORA, MLA_HEADS, MLA_QK_PAD - MLA_NOPE), w_ukv.dtype)
    w["uk"] = jnp.concatenate([ukv[..., :MLA_NOPE], zk], axis=-1).reshape(depth, MLA_KV_LORA, hq).astype(BF16)
    w["uv_t"] = jnp.swapaxes(ukv[..., MLA_NOPE:].reshape(depth, MLA_KV_LORA, MLA_WIDTH), 1, 2).astype(BF16)
    return w


def _rope_multipliers(positions):
    b, s = positions.shape
    inv_freq = 1.0 / (ROPE_THETA ** (jnp.arange(0, MLA_ROPE, 2, dtype=F32) / MLA_ROPE))
    ang = positions.astype(F32)[..., None] * inv_freq
    cos, sin = jnp.cos(ang), jnp.sin(ang)
    pad = MLA_QK_PAD - MLA_NOPE - MLA_ROPE
    m1 = jnp.concatenate([jnp.ones((b, s, MLA_NOPE), F32), cos, cos, jnp.zeros((b, s, pad), F32)], axis=-1)
    m2 = jnp.concatenate([jnp.zeros((b, s, MLA_NOPE), F32), sin, sin, jnp.zeros((b, s, pad), F32)], axis=-1)
    return m1, m2, jnp.swapaxes(m1, 1, 2), jnp.swapaxes(m2, 1, 2)


def kernel(x, p, positions, norm_g, w_in, mla_q_norm, w_uq, mla_kv_norm, w_ukv, conv_w, conv_b, dt_bias, a_log, d_skip, ssm_norm, w_br_a, w_br_b, w_br_c, w_out, rel_bias, w_ple, w_ple_gate, final_norm):
    b, s, d = x.shape
    depth = w_in.shape[0]
    t = b * s
    w = _prep_weights(w_in, w_uq, w_ukv)
    m1, m2, m1t, m2t = _rope_multipliers(positions)
    pos_col = positions.astype(I32).reshape(b, s, 1)
    pos_row = positions.astype(I32).reshape(b, 1, s)
    wbr_a, wbr_b, wbr_c = w_br_a.astype(BF16), w_br_b.astype(BF16), w_br_c.astype(BF16)
    wo, wpg, wple = w_out.astype(BF16), w_ple_gate.astype(BF16), w_ple.astype(BF16)

    q_norm = mla_q_norm.astype(F32).reshape(depth, 1, -1)
    kv_norm = mla_kv_norm.astype(F32).reshape(depth, 1, -1)
    p3d = p.reshape(depth, t, -1)

    x2d = x.reshape(t, d)
    h = _norm(x2d, norm_g[0], BF16)
    for i in range(depth):
        pr = _proj_all(h.reshape(b, s, d), i, w["proj"], w["proj_t"], (m1, m2, m1t, m2t), q_norm, kv_norm,
                       w["uq1_t"], w["uq2_t"], w["uk"], w["uv_t"], DSA_Q_BLOCK)
        o_a = _mla_attn(pr["mla_q_t"], pr["mla_k"], pr["mla_v_t1"], pr["gate_a"])
        o_b = _ssd(pr["xbc"], pr["z"], pr["small"], conv_w[i], conv_b[i], dt_bias[i], a_log[i], d_skip[i], ssm_norm[i])
        o_c = _dsa(rel_bias, pr["q_c_wide"], pr["q_idx_wide"], pr["k_c"], pr["k_idx"], pr["v_t1"], pr["w_idx_t"], pos_col, pos_row,
                   pr["gate_c"], DSA_Q_BLOCK)

        last = i == depth - 1
        g_next = final_norm if last else norm_g[i + 1]
        x2d, h = _merge(o_a.reshape(t, -1), o_b.reshape(t, -1), o_c.reshape(t, -1), pr["merge"].reshape(t, -1), x2d, p3d,
                        i, wbr_a, wbr_b, wbr_c, wo, wpg, wple, g_next, F32 if last else BF16)
    return h.reshape(b, s, d)
```

```python
import functools
import math

import jax
import jax.numpy as jnp
from jax import lax
from jax.experimental import pallas as pl
from jax.experimental.pallas import tpu as pltpu

F32 = jnp.float32
BF16 = jnp.bfloat16
I32 = jnp.int32
I16 = jnp.int16
HALF16 = 1 << 15

D_MODEL = 1024
NORM_EPS = 1e-6

MLA_HEADS = 8
MLA_NOPE = 64
MLA_ROPE = 32
MLA_V = 64
MLA_Q_LORA = 384
MLA_KV_LORA = 256
MLA_WIDTH = MLA_HEADS * MLA_V
ROPE_THETA = 10000.0
MLA_QK_PAD = 128
LOG2E = math.log2(math.e)
MLA_Q_SCALE = (MLA_NOPE + MLA_ROPE) ** -0.5 * LOG2E

SSM_HEADS = 16
SSM_HEAD_DIM = 64
SSM_INNER = SSM_HEADS * SSM_HEAD_DIM
SSM_GROUPS = 2
SSM_STATE = 128
SSM_CONV = 4
SSM_CHUNK = 128
SSM_CONV_DIM = SSM_INNER + 2 * SSM_GROUPS * SSM_STATE
SSM_HEADS_PER_GROUP = SSM_HEADS // SSM_GROUPS
ONES_ROWS = 16
MLA_V1_ROWS = MLA_V + ONES_ROWS
SSM_CONV_TAIL = 16

DSA_HEADS = 8
DSA_HEAD_DIM = 64
DSA_WIDTH = DSA_HEADS * DSA_HEAD_DIM
IDX_HEADS = 8
IDX_DIM = 64
TOPK_MAX = 256
DSA_KEY_TILE = 128
DSA_Q_BLOCK = 256

REL_BUCKETS = 32
REL_MAX_DIST = 128
N_BRANCHES = 3

LANES = 128
SUBLANES = 8
VMEM_LIMIT_CAP = 56 * 1024 * 1024
VMEM_LIMIT_FLOOR = 32 * 1024 * 1024
INT_MIN = -(2 ** 31)
NEG_BIG = -1e30

SPLIT_SIZES = (
    MLA_Q_LORA, MLA_KV_LORA, MLA_ROPE, MLA_WIDTH, SSM_INNER, SSM_CONV_DIM, SSM_HEADS, DSA_WIDTH,
    DSA_HEAD_DIM, DSA_HEAD_DIM, IDX_HEADS * IDX_DIM, IDX_DIM, IDX_HEADS, DSA_WIDTH,
    N_BRANCHES * D_MODEL,
)
SPLIT_NAMES = ("c_q", "c_kv", "k_rope", "gate_a", "z", "xbc", "dt", "q_c", "k_c", "v_c", "q_idx",
               "k_idx", "w_idx", "gate_c", "merge")


def _split_bounds():
    out, off = {}, 0
    for name, size in zip(SPLIT_NAMES, SPLIT_SIZES, strict=True):
        out[name] = (off, off + size)
        off += size
    return out


SPLIT = _split_bounds()


def _t5_large_thresholds():
    exact = REL_BUCKETS // 2
    thr = []
    for j in range(1, REL_BUCKETS - exact):
        thr.append(int(math.ceil(exact * (REL_MAX_DIST / exact) ** (j / (REL_BUCKETS - exact)) - 1e-9)))
    return tuple(thr)


T5_EXACT = REL_BUCKETS // 2
T5_LARGE_THR = _t5_large_thresholds()
T5_FAR = T5_LARGE_THR[-1]


def _compiler_params(semantics, vmem_bytes):
    limit = int(min(VMEM_LIMIT_CAP, max(VMEM_LIMIT_FLOOR, vmem_bytes)))
    return pltpu.CompilerParams(dimension_semantics=semantics, vmem_limit_bytes=limit)


def _nbytes(shape, dtype):
    return math.prod(shape) * jnp.dtype(dtype).itemsize


def _sigmoid(x):
    return 1.0 / (1.0 + jnp.exp(-x))


def _rms(x, g):
    return x * lax.rsqrt(jnp.mean(x * x, axis=-1, keepdims=True) + NORM_EPS) * g


def _norm_kernel(x_ref, g_ref, o_ref):
    o_ref[0] = _rms(x_ref[0], g_ref[...]).astype(o_ref.dtype)


def _norm(x, g, out_dtype, tm=512):
    b, s, d = x.shape
    return pl.pallas_call(
        _norm_kernel,
        grid=(b, s // tm),
        in_specs=[pl.BlockSpec((1, tm, d), lambda bi, i: (bi, i, 0)), pl.BlockSpec((1, d), lambda bi, i: (0, 0))],
        out_specs=pl.BlockSpec((1, tm, d), lambda bi, i: (bi, i, 0)),
        out_shape=jax.ShapeDtypeStruct((b, s, d), out_dtype),
        compiler_params=_compiler_params(("parallel", "parallel"), 4 * _nbytes((tm, d), F32)),
        name="rms_norm",
    )(x, g.reshape(1, d))


MLA_IN_WIDTH = MLA_Q_LORA + MLA_KV_LORA + 2 * LANES
PROJ_SEGMENTS = (
    ("small", LANES, None, F32),
    ("gate_a", MLA_WIDTH, "silu", BF16),
    ("gate_c", DSA_WIDTH, "silu", BF16),
    ("z", SSM_INNER, "silu", BF16),
    ("xbc", SSM_CONV_DIM, None, BF16),
    ("merge", N_BRANCHES * D_MODEL, "sigmoid", BF16),
    ("k_c", DSA_HEAD_DIM, None, BF16),
    ("k_idx", IDX_DIM, None, BF16),
)
PROJ_CHUNK = 512
PROJ_T_ROWS = (DSA_WIDTH, IDX_HEADS * IDX_DIM, DSA_HEAD_DIM + IDX_HEADS)


def _proj_all_kernel(h_ref, w_ref, wt_ref, m1_ref, m2_ref, m1t_ref, m2t_ref, qn_ref, kvn_ref, wq1t_ref, wq2t_ref, wk_ref,
                     wvt_ref, *out_refs):
    h = h_ref[0]
    n_seg = len(PROJ_SEGMENTS)
    a = jnp.dot(h, w_ref[0, :, 0:MLA_IN_WIDTH], preferred_element_type=F32)
    _mla_prep(a, m1_ref, m2_ref, m1t_ref, m2t_ref, qn_ref, kvn_ref, wq1t_ref, wq2t_ref, wk_ref, wvt_ref, *out_refs[n_seg + 4:])
    off = MLA_IN_WIDTH
    for (_, width, act, _), o_ref in zip(PROJ_SEGMENTS, out_refs[:n_seg], strict=True):
        for c in range(0, width, PROJ_CHUNK):
            cw = min(PROJ_CHUNK, width - c)
            y = jnp.dot(h, w_ref[0, :, off + c:off + c + cw], preferred_element_type=F32)
            if act == "silu":
                y = y * _sigmoid(y)
            elif act == "sigmoid":
                y = _sigmoid(y)
            o_ref[0, :, c:c + cw] = y.astype(o_ref.dtype)
        off += width
    qcw_ref, qiw_ref, v1_ref, wi_ref = out_refs[n_seg:n_seg + 4]
    nt = (((1,), (1,)), ((), ()))
    tq = h.shape[0]
    row = 0
    for o_ref, heads, dim in ((qcw_ref, DSA_HEADS, DSA_HEAD_DIM), (qiw_ref, IDX_HEADS, IDX_DIM)):
        y = lax.dot_general(wt_ref[0, row:row + heads * dim, :], h, nt, preferred_element_type=F32)
        for hd in range(heads):
            o_ref[0, 0, :, hd * tq:(hd + 1) * tq] = y[hd * dim:(hd + 1) * dim, :].astype(o_ref.dtype)
        row += heads * dim
    vw = lax.dot_general(wt_ref[0, row:row + PROJ_T_ROWS[2], :], h, nt, preferred_element_type=F32)
    v1_ref[0, 0:DSA_HEAD_DIM, :] = vw[0:DSA_HEAD_DIM, :].astype(v1_ref.dtype)
    v1_ref[0, DSA_HEAD_DIM:, :] = jnp.ones((ONES_ROWS, tq), v1_ref.dtype)
    wi_ref[0] = vw[DSA_HEAD_DIM:, :]


def _proj_all(h3d, layer, w, wt, rope, qn, kvn, wq1t, wq2t, wk, wvt, tq):
    b, s, k = h3d.shape
    n_all = MLA_IN_WIDTH + sum(seg[1] for seg in PROJ_SEGMENTS)
    hq = MLA_HEADS * MLA_QK_PAD
    v1_rows = DSA_HEAD_DIM + ONES_ROWS
    out_shapes = [jax.ShapeDtypeStruct((b, s, width), dt) for _, width, _, dt in PROJ_SEGMENTS]
    out_specs = [pl.BlockSpec((1, tq, width), lambda bi, i: (bi, i, 0)) for _, width, _, _ in PROJ_SEGMENTS]
    for heads, dim in ((DSA_HEADS, DSA_HEAD_DIM), (IDX_HEADS, IDX_DIM)):
        out_shapes.append(jax.ShapeDtypeStruct((b, s // tq, dim, heads * tq), BF16))
        out_specs.append(pl.BlockSpec((1, 1, dim, heads * tq), lambda bi, i: (bi, i, 0, 0)))
    for rows, dt in ((v1_rows, BF16), (IDX_HEADS, F32)):
        out_shapes.append(jax.ShapeDtypeStruct((b, rows, s), dt))
        out_specs.append(pl.BlockSpec((1, rows, tq), lambda bi, i: (bi, 0, i)))
    for rows in (hq, None, MLA_HEADS * MLA_V1_ROWS):
        if rows is None:
            out_shapes.append(jax.ShapeDtypeStruct((b, s, hq), BF16))
            out_specs.append(pl.BlockSpec((1, tq, hq), lambda bi, i: (bi, i, 0)))
        else:
            out_shapes.append(jax.ShapeDtypeStruct((b, rows, s), BF16))
            out_specs.append(pl.BlockSpec((1, rows, tq), lambda bi, i: (bi, 0, i)))
    resident = pl.Buffered(1)
    weights = (w, wt, qn, kvn, wq1t, wq2t, wk, wvt)
    layer_spec = lambda a: pl.BlockSpec((1,) + a.shape[1:], lambda bi, i: (layer,) + (0,) * (a.ndim - 1), pipeline_mode=resident)
    m1, m2, m1t, m2t = rope
    vmem = (sum(_nbytes(a.shape[1:], a.dtype) for a in weights) + 2 * _nbytes((tq, k), BF16) + 8 * _nbytes((tq, LANES), F32)
            + 2 * sum(_nbytes((tq, width), dt) for _, width, _, dt in PROJ_SEGMENTS)
            + 2 * (2 * _nbytes((DSA_HEAD_DIM, DSA_HEADS * tq), BF16) + _nbytes((v1_rows + IDX_HEADS, tq), F32))
            + 2 * (2 * _nbytes((tq, hq), BF16) + _nbytes((MLA_HEADS * MLA_V1_ROWS, tq), BF16))
            + 8 * _nbytes((tq, PROJ_CHUNK), F32) + 6 * _nbytes((tq, hq), F32))
    outs = pl.pallas_call(
        _proj_all_kernel,
        grid=(b, s // tq),
        in_specs=[pl.BlockSpec((1, tq, k), lambda bi, i: (bi, i, 0)), layer_spec(w), layer_spec(wt),
                  pl.BlockSpec((1, tq, LANES), lambda bi, i: (bi, i, 0)), pl.BlockSpec((1, tq, LANES), lambda bi, i: (bi, i, 0)),
                  pl.BlockSpec((1, LANES, tq), lambda bi, i: (bi, 0, i)), pl.BlockSpec((1, LANES, tq), lambda bi, i: (bi, 0, i))]
        + [layer_spec(a) for a in weights[2:]],
        out_specs=out_specs,
        out_shape=out_shapes,
        compiler_params=_compiler_params(("parallel", "parallel"), vmem),
        name="proj_all",
    )(h3d, w, wt, m1, m2, m1t, m2t, *weights[2:])
    names = [seg[0] for seg in PROJ_SEGMENTS] + ["q_c_wide", "q_idx_wide", "v_t1", "w_idx_t", "mla_q_t", "mla_k", "mla_v_t1"]
    return dict(zip(names, outs, strict=True))


def _mla_prep(a, m1_ref, m2_ref, m1t_ref, m2t_ref, qn_ref, kvn_ref, wq1t_ref, wq2t_ref, wk_ref, wvt_ref, qt_ref, k_ref, vt_ref):
    c_q = a[:, :MLA_Q_LORA]
    c_kv = a[:, MLA_Q_LORA:MLA_Q_LORA + MLA_KV_LORA]
    kr1 = a[:, MLA_Q_LORA + MLA_KV_LORA:MLA_Q_LORA + MLA_KV_LORA + LANES]
    kr2 = a[:, MLA_Q_LORA + MLA_KV_LORA + LANES:]
    cqn = _rms(c_q, qn_ref[0]).astype(BF16)
    ckvn = _rms(c_kv, kvn_ref[0]).astype(BF16)
    nt = (((1,), (1,)), ((), ()))
    qa_t = lax.dot_general(wq1t_ref[0], cqn, nt, preferred_element_type=F32)
    qb_t = lax.dot_general(wq2t_ref[0], cqn, nt, preferred_element_type=F32)
    kn = jnp.dot(ckvn, wk_ref[0], preferred_element_type=F32)
    kr = kr1 * m1_ref[0] + kr2 * m2_ref[0]
    rope = slice(MLA_NOPE, MLA_NOPE + MLA_ROPE)
    cos_t = m1t_ref[0, rope, :] * MLA_Q_SCALE
    sin_t = m2t_ref[0, rope, :] * MLA_Q_SCALE
    pad_rows = jnp.zeros((MLA_QK_PAD - MLA_NOPE - MLA_ROPE, qa_t.shape[1]), qt_ref.dtype)
    for h in range(MLA_HEADS):
        base = h * MLA_QK_PAD
        sl = slice(base, base + MLA_QK_PAD)
        qt_ref[0, base:base + MLA_NOPE, :] = (qa_t[base:base + MLA_NOPE, :] * MLA_Q_SCALE).astype(qt_ref.dtype)
        qt_ref[0, base + MLA_NOPE:base + MLA_NOPE + MLA_ROPE, :] = (
            qa_t[base + MLA_NOPE:base + MLA_NOPE + MLA_ROPE, :] * cos_t
            + qb_t[h * MLA_ROPE:(h + 1) * MLA_ROPE, :] * sin_t).astype(qt_ref.dtype)
        qt_ref[0, base + MLA_NOPE + MLA_ROPE:base + MLA_QK_PAD, :] = pad_rows
        k_ref[0, :, sl] = (kn[:, sl] + kr).astype(k_ref.dtype)
    v_t = lax.dot_general(wvt_ref[0], ckvn, nt, preferred_element_type=F32)
    ones = jnp.ones((ONES_ROWS, v_t.shape[1]), vt_ref.dtype)
    for h in range(MLA_HEADS):
        base = h * MLA_V1_ROWS
        vt_ref[0, base:base + MLA_V, :] = v_t[h * MLA_V:(h + 1) * MLA_V, :].astype(vt_ref.dtype)
        vt_ref[0, base + MLA_V:base + MLA_V1_ROWS, :] = ones


def _mla_attn_kernel(qt_ref, k_ref, vt_ref, g_ref, o_ref, m_ref, l_ref, acc_ref, s0_ref, s1_ref, *, tq, tk):
    assert tq == 2 * tk
    qi = pl.program_id(1)
    krow = lax.broadcasted_iota(I32, (tk, tq), 0)
    qcol = lax.broadcasted_iota(I32, (tk, tq), 1)
    tiles_per_q = tq // tk
    m_ref[...] = jnp.full(m_ref.shape, -jnp.inf, F32)
    l_ref[...] = jnp.zeros(l_ref.shape, F32)
    acc_ref[...] = jnp.zeros(acc_ref.shape, F32)

    staged = (s0_ref, s1_ref)
    n_tiles = (qi + 1) * tiles_per_q

    def head_lanes(h):
        return slice(h * tq, (h + 1) * tq)

    def stage(j, dst_ref):
        ks = pl.multiple_of(j * tk, tk)
        for h in range(MLA_HEADS):
            k = k_ref[0, pl.ds(ks, tk), h * MLA_QK_PAD:(h + 1) * MLA_QK_PAD]
            dst_ref[:, head_lanes(h)] = jnp.dot(k, qt_ref[0, h * MLA_QK_PAD:(h + 1) * MLA_QK_PAD, :],
                                                preferred_element_type=F32)

    def tile(j, cur_ref, next_ref, diagonal):
        ks = pl.multiple_of(j * tk, tk)
        stage(jnp.minimum(j + 1, n_tiles - 1), next_ref)
        m_all = m_ref[...]
        l_all = l_ref[...]
        if diagonal:
            causal = ks + krow <= qi * tq + qcol
        for h in range(MLA_HEADS):
            s = cur_ref[:, head_lanes(h)]
            if diagonal:
                s = jnp.where(causal, s, -jnp.inf)
            m_old = m_all[h:h + 1, :]
            m_new = jnp.maximum(m_old, jnp.max(s, axis=0, keepdims=True))
            alpha = jnp.exp2(m_old - m_new)
            p = jnp.exp2(s - m_new).astype(BF16)
            m_ref[h:h + 1, :] = m_new
            pv = jnp.dot(vt_ref[0, h * MLA_V1_ROWS:(h + 1) * MLA_V1_ROWS, pl.ds(ks, tk)], p, preferred_element_type=F32)
            hs = slice(h * MLA_V, (h + 1) * MLA_V)
            acc_ref[hs, :] = alpha * acc_ref[hs, :] + pv[0:MLA_V, :]
            l_ref[h:h + 1, :] = alpha * l_all[h:h + 1, :] + pv[MLA_V:MLA_V + 1, :]

    stage(0, staged[0])

    def full_pair(pi, carry):
        for half in range(2):
            tile(2 * pi + half, staged[half], staged[1 - half], diagonal=False)
        return carry

    lax.fori_loop(0, qi, full_pair, 0)
    for half in range(2):
        tile(2 * qi + half, staged[half], staged[1 - half], diagonal=True)
    for h in range(MLA_HEADS):
        hs = slice(h * MLA_V, (h + 1) * MLA_V)
        acc_ref[hs, :] = acc_ref[hs, :] / l_ref[h:h + 1, :]
    o_ref[0] = (acc_ref[...].T * g_ref[0]).astype(o_ref.dtype)


def _mla_attn(qt, k, vt, gate, tq=256, tk=128):
    b, s, hq = k.shape
    vmem = 2 * (_nbytes((hq, tq), BF16) + _nbytes((s, hq), BF16) + _nbytes((MLA_WIDTH, s), BF16)
                + _nbytes((tq, MLA_WIDTH), F32) + _nbytes((tq, MLA_WIDTH), BF16)) + _nbytes((MLA_WIDTH, tq), F32) + 16 * _nbytes((tk, tq), F32)
    return pl.pallas_call(
        functools.partial(_mla_attn_kernel, tq=tq, tk=tk),
        grid=(b, s // tq),
        in_specs=[pl.BlockSpec((1, hq, tq), lambda bi, i: (bi, 0, i)),
                  pl.BlockSpec((1, s, hq), lambda bi, i: (bi, 0, 0)),
                  pl.BlockSpec((1, MLA_HEADS * MLA_V1_ROWS, s), lambda bi, i: (bi, 0, 0)),
                  pl.BlockSpec((1, tq, MLA_WIDTH), lambda bi, i: (bi, i, 0))],
        out_specs=pl.BlockSpec((1, tq, MLA_WIDTH), lambda bi, i: (bi, i, 0)),
        out_shape=jax.ShapeDtypeStruct((b, s, MLA_WIDTH), BF16),
        scratch_shapes=[pltpu.VMEM((MLA_HEADS, tq), F32), pltpu.VMEM((MLA_HEADS, tq), F32),
                        pltpu.VMEM((MLA_WIDTH, tq), F32),
                        pltpu.VMEM((tk, MLA_HEADS * tq), F32), pltpu.VMEM((tk, MLA_HEADS * tq), F32)],
        compiler_params=_compiler_params(("parallel", "arbitrary"), vmem),
        name="mla_attn",
    )(qt, k, vt, gate)


def _ssd_kernel(xbc_ref, xprev_ref, zs_ref, sm_ref, cw_ref, cb_ref, dtb_ref, alog_ref, dskf_ref, nrm_ref, exp16_ref, exp32_ref,
                o_ref, state_ref, y_ref):
    c = pl.program_id(1)
    q = SSM_CHUNK

    @pl.when(c == 0)
    def _():
        state_ref[...] = jnp.zeros(state_ref.shape, F32)

    x_in = xbc_ref[0]
    tail = xprev_ref.shape[1]
    prev = jnp.where(c > 0, xprev_ref[0].astype(F32), 0.0).astype(x_in.dtype)
    win = jnp.concatenate([prev, x_in], axis=0)
    x = x_in.astype(F32)
    wrow = lax.broadcasted_iota(I32, (q, tail + q), 0)
    wcol = lax.broadcasted_iota(I32, (q, tail + q), 1)
    acc = cb_ref[...] + cw_ref[SSM_CONV - 1:SSM_CONV, :] * x
    for j in range(1, SSM_CONV):
        pick = jnp.where(wcol == wrow + (tail - j), 1.0, 0.0).astype(win.dtype)
        acc = acc + cw_ref[SSM_CONV - 1 - j:SSM_CONV - j, :] * jnp.dot(pick, win, preferred_element_type=F32)
    half = 0.5 * acc
    xc = half + half * jnp.tanh(half)

    xs = xc[:, :SSM_INNER]
    bm = xc[:, SSM_INNER:SSM_INNER + SSM_GROUPS * SSM_STATE]
    cm = xc[:, SSM_INNER + SSM_GROUPS * SSM_STATE:]

    pre = sm_ref[0] + dtb_ref[...]
    dt = jnp.maximum(pre, 0.0) + jnp.log1p(jnp.exp(-jnp.abs(pre)))
    a = -jnp.exp(alog_ref[...])
    row = lax.broadcasted_iota(I32, (q, q), 0)
    col = lax.broadcasted_iota(I32, (q, q), 1)
    lower = row >= col
    cum = jnp.dot(jnp.where(lower, 1.0, 0.0).astype(F32), dt * a, preferred_element_type=F32,
                  precision=lax.Precision.HIGHEST)
    cum_t = cum.T
    cum_last = cum[q - 1:q, :]
    per_head = jnp.concatenate([dt, jnp.exp(cum_last - cum), jnp.exp(cum)], axis=0).astype(BF16)
    spread = jnp.dot(per_head, exp16_ref[...], preferred_element_type=F32)
    dt_full, to_end_full, e_cum_full = spread[0:q], spread[q:2 * q], spread[2 * q:3 * q]
    chunk_decay_full = jnp.dot(jnp.broadcast_to(jnp.exp(cum_last), (SUBLANES, LANES)), exp32_ref[...],
                               preferred_element_type=F32, precision=lax.Precision.HIGHEST)[0:1]
    xdt = xs * dt_full
    xw = (xdt * to_end_full).astype(BF16)
    first_head = lax.broadcasted_iota(I32, (q, LANES), 1) < SSM_HEAD_DIM
    group_w = SSM_HEADS_PER_GROUP * SSM_HEAD_DIM
    heads_per_tile = LANES // SSM_HEAD_DIM

    for g in range(SSM_GROUPS):
        bg32 = bm[:, g * SSM_STATE:(g + 1) * SSM_STATE]
        bg = bg32.astype(BF16)
        cg = cm[:, g * SSM_STATE:(g + 1) * SSM_STATE].astype(BF16)
        cb = lax.dot_general(cg, bg, (((1,), (1,)), ((), ())), preferred_element_type=F32)
        gl = slice(g * group_w, (g + 1) * group_w)
        st = state_ref[g]
        y_off = jnp.dot(cg, st.astype(BF16), preferred_element_type=F32)
        state_ref[g] = st * chunk_decay_full[:, gl] + jnp.dot(bg32.T.astype(BF16), xw[:, gl], preferred_element_type=F32)
        for tile in range(group_w // LANES):
            h0 = g * SSM_HEADS_PER_GROUP + tile * heads_per_tile
            tl = slice(h0 * SSM_HEAD_DIM, h0 * SSM_HEAD_DIM + LANES)
            g_both = []
            for h in range(h0, h0 + heads_per_tile):
                diff = cum[:, h:h + 1] - cum_t[h:h + 1, :]
                decay_in = jnp.exp(jnp.where(lower, diff, -jnp.inf))
                g_both.append((cb * decay_in).astype(BF16))
            xdt_tile = xdt[:, tl]
            stacked = jnp.concatenate([jnp.where(first_head, xdt_tile, 0.0).astype(BF16),
                                       jnp.where(first_head, 0.0, xdt_tile).astype(BF16)], axis=0)
            y_diag = jnp.dot(jnp.concatenate(g_both, axis=1), stacked, preferred_element_type=F32)
            y = y_diag + y_off[:, tile * LANES:(tile + 1) * LANES] * e_cum_full[:, tl] + xs[:, tl] * dskf_ref[:, tl]
            y_ref[:, tl] = y * zs_ref[0, :, tl]

    o_ref[0] = _rms(y_ref[...], nrm_ref[...]).astype(o_ref.dtype)


def _ssd(xbc, zs, small, conv_w, conv_b, dt_bias, a_log, d_skip, ssm_norm):
    b, s, _ = xbc.shape
    q = SSM_CHUNK
    pad = lambda v: jnp.pad(v.astype(F32), (0, LANES - v.shape[0])).reshape(1, LANES)
    full = lambda shape: pl.BlockSpec(shape, lambda bi, c: (0,) * len(shape))
    blk = lambda width: pl.BlockSpec((1, q, width), lambda bi, c: (bi, c, 0))
    expand = (jnp.arange(LANES)[:, None] == jnp.arange(SSM_INNER)[None, :] // SSM_HEAD_DIM).astype(F32)
    d_skip_full = jnp.repeat(d_skip.astype(F32), SSM_HEAD_DIM).reshape(1, SSM_INNER)
    group_w = SSM_HEADS_PER_GROUP * SSM_HEAD_DIM
    vmem = (2 * (_nbytes((q, SSM_CONV_DIM), xbc.dtype) + _nbytes((q, SSM_INNER), zs.dtype) + _nbytes((q, LANES), F32)
                 + _nbytes((q, SSM_INNER), BF16) + _nbytes((LANES, SSM_INNER), BF16) + _nbytes((LANES, SSM_INNER), F32))
            + _nbytes((2 * q, SSM_CONV_DIM), xbc.dtype) + _nbytes((SSM_GROUPS, SSM_STATE, group_w), F32)
            + _nbytes((q, SSM_INNER), F32) + 10 * _nbytes((q, SSM_CONV_DIM), F32))
    return pl.pallas_call(
        _ssd_kernel,
        grid=(b, s // q),
        in_specs=[blk(SSM_CONV_DIM),
                  pl.BlockSpec((1, SSM_CONV_TAIL, SSM_CONV_DIM),
                               lambda bi, c: (bi, jnp.maximum(c * (q // SSM_CONV_TAIL) - 1, 0), 0)),
                  blk(SSM_INNER), blk(LANES), full((SSM_CONV, SSM_CONV_DIM)), full((1, SSM_CONV_DIM)),
                  full((1, LANES)), full((1, LANES)), full((1, SSM_INNER)), full((1, SSM_INNER)),
                  full((LANES, SSM_INNER)), full((LANES, SSM_INNER))],
        out_specs=blk(SSM_INNER),
        out_shape=jax.ShapeDtypeStruct((b, s, SSM_INNER), BF16),
        scratch_shapes=[pltpu.VMEM((SSM_GROUPS, SSM_STATE, group_w), F32),
                        pltpu.VMEM((q, SSM_INNER), F32)],
        compiler_params=_compiler_params(("parallel", "arbitrary"), vmem),
        name="ssd",
    )(xbc, xbc, zs, small, conv_w.astype(F32), conv_b.reshape(1, -1).astype(F32), pad(dt_bias), pad(a_log), d_skip_full,
      ssm_norm.reshape(1, -1).astype(F32), expand.astype(BF16), expand)


def _t5_bucket_of(n):
    steps_per_octave = (REL_BUCKETS - T5_EXACT) / math.log2(REL_MAX_DIST / T5_EXACT)
    x = jnp.maximum(n, T5_EXACT).astype(F32) * (1.0 / T5_EXACT)
    large = T5_EXACT + jnp.floor(jnp.log2(x) * steps_per_octave).astype(I32)
    return jnp.where(n <= T5_EXACT, n, jnp.minimum(large, REL_BUCKETS - 1))


def _dsa_kernel(qmin_ref, kmax_ref, tblt_ref, qcw_ref, qiw_ref, kc_ref, ki_ref, v1_ref, wi_ref, posk_ref, posq_ref, gate_ref, o_ref,
                key_ref, hi_ref, lo_ref, s0_ref, s1_ref, bias_ref, acc_ref, ot_ref, m_ref, l_ref, tie_ref, *, n_sel, tq):
    tk = DSA_KEY_TILE
    qi = pl.program_id(1)
    n_tiles = (qi + 1) * (tq // tk)
    krow = lax.broadcasted_iota(I32, (tk, tq), 0)
    qcol = lax.broadcasted_iota(I32, (tk, tq), 1)
    w = wi_ref[0]
    posq = posq_ref[0]

    def tile_start(kt):
        return pl.multiple_of(kt * tk, tk)

    def causal_of(ks):
        return (ks + krow) <= (qi * tq + qcol)

    def head_lanes(h):
        return slice(h * tq, (h + 1) * tq)

    staged = (s0_ref, s1_ref)

    def stage(k_ref, q_ref, kt, dst_ref):
        dst_ref[...] = jnp.dot(k_ref[0, pl.ds(tile_start(kt), tk), :], q_ref[0, 0], preferred_element_type=F32)

    def tile_pairs(tile_fn, peel_last):
        def pair(pi, last_pair):
            for half in range(2):
                tile_fn(2 * pi + half, staged[half], staged[1 - half], last_pair, last_pair and half == 1)

        def body(pi, carry):
            pair(pi, False)
            return carry

        if peel_last:
            lax.fori_loop(0, n_tiles // 2 - 1, body, 0)
            pair(n_tiles // 2 - 1, True)
        else:
            lax.fori_loop(0, n_tiles // 2, body, 0)

    stage(ki_ref, qiw_ref, 0, staged[0])

    def score_tile(kt, cur_ref, next_ref, last_pair, last_tile):
        ks = tile_start(kt)
        if last_tile:
            stage(kc_ref, qcw_ref, 0, next_ref)
        else:
            stage(ki_ref, qiw_ref, kt + 1, next_ref)
        sc = jnp.zeros((tk, tq), F32)
        for h in range(IDX_HEADS):
            sc = sc + w[h:h + 1, :] * jnp.maximum(cur_ref[:, head_lanes(h)], 0.0)
        sc = jnp.where(sc == 0.0, 0.0, sc)
        bits = lax.bitcast_convert_type(sc, I32)
        key = bits ^ ((bits >> 31) & 0x7FFFFFFF)
        if last_pair:
            key = jnp.where(causal_of(ks), key, INT_MIN)
        key_ref[pl.ds(ks, tk), :] = key
        hi_ref[pl.ds(ks, tk), :] = (key >> 16).astype(I16)
        lo_ref[pl.ds(ks, tk), :] = ((key & 0xFFFF) - HALF16).astype(I16)

    tile_pairs(score_tile, peel_last=True)

    def count16(half_ref, pred):
        def body(pi, cnt):
            for kt in (2 * pi, 2 * pi + 1):
                cnt = cnt + jnp.where(pred(half_ref[pl.ds(tile_start(kt), tk), :]), jnp.int16(1), jnp.int16(0))
            return cnt
        cnt = lax.fori_loop(0, n_tiles // 2, body, jnp.zeros((tk, tq), I16))
        rows = 2 * SUBLANES
        parts = [cnt[r:r + rows, :] for r in range(0, tk, rows)]
        while len(parts) > 1:
            parts = [a + b for a, b in zip(parts[0::2], parts[1::2], strict=True)]
        return jnp.sum(parts[0].astype(I32), axis=0, keepdims=True)

    def nth_largest16(half_ref, rank):
        def bit_step(i, cand):
            trial = cand | lax.shift_left(jnp.int32(1), 15 - i)
            thr = (trial - HALF16).astype(I16)
            tot = count16(half_ref, lambda v: v >= thr)
            return jnp.where(tot >= rank, trial, cand)
        return lax.fori_loop(0, 16, bit_step, jnp.zeros((1, tq), I32)) - HALF16

    tau_hi = nth_largest16(hi_ref, n_sel)
    tau_hi16 = tau_hi.astype(I16)
    above_hi = count16(hi_ref, lambda v: v > tau_hi16)

    def mask_low(kt, carry):
        rows = pl.ds(tile_start(kt), tk)
        lo_ref[rows, :] = jnp.where(hi_ref[rows, :] == tau_hi16, lo_ref[rows, :], jnp.int16(-HALF16))
        return carry

    lax.fori_loop(0, n_tiles, mask_low, 0)
    tau_lo = nth_largest16(lo_ref, n_sel - above_hi)
    tau_lo16 = tau_lo.astype(I16)
    tau = lax.shift_left(tau_hi, 16) | (tau_lo + HALF16)
    n_gt = above_hi + count16(lo_ref, lambda v: v > tau_lo16)
    n_ge = above_hi + count16(lo_ref, lambda v: v >= tau_lo16)
    need = n_sel - n_gt

    m_ref[...] = jnp.full(m_ref.shape, NEG_BIG, F32)
    l_ref[...] = jnp.zeros(l_ref.shape, F32)
    acc_ref[...] = jnp.zeros(acc_ref.shape, F32)
    tie_ref[...] = jnp.zeros(tie_ref.shape, I32)
    lower_incl = jnp.where(lax.broadcasted_iota(I32, (tk, tk), 0) >= lax.broadcasted_iota(I32, (tk, tk), 1),
                           1.0, 0.0).astype(BF16)

    splits =jnp.where((n_ge > n_sel) & (tau != INT_MIN), 1.0, 0.0)
    any_split = jnp.max(splits) > 0.0

    def attend_tile(kt, cur_ref, next_ref, general_bias, tie_split):
        ks = tile_start(kt)
        key = key_ref[pl.ds(ks, tk), :]
        if tie_split:
            eq = key == tau
            prefix = jnp.dot(lower_incl, jnp.where(eq, 1.0, 0.0).astype(BF16), preferred_element_type=F32)
            tie_before = tie_ref[0:1, :]
            tie_rank = tie_before + prefix.astype(I32)
            sel = ((key > tau) | (eq & (tie_rank <= need))) & causal_of(ks)
            tie_ref[0:1, :] = tie_before + prefix[tk - 1:tk, :].astype(I32)
        else:
            sel = (key >= tau) & causal_of(ks)

        if general_bias:
            posk = posk_ref[0, pl.ds(ks, tk), :]
            bucket = _t5_bucket_of(jnp.maximum(posq - posk, 0))
            for h in range(DSA_HEADS):
                table = jnp.broadcast_to(tblt_ref[h:h + 1, :], (tk, LANES))
                for c in range(tq // LANES):
                    bias_ref[:, h * tq + c * LANES:h * tq + (c + 1) * LANES] = jnp.take_along_axis(
                        table, bucket[:, c * LANES:(c + 1) * LANES], axis=1)
        stage(kc_ref, qcw_ref, jnp.minimum(kt + 1, n_tiles - 1), next_ref)
        m_all = m_ref[0:1, :]
        l_all = l_ref[0:1, :]
        v1 = v1_ref[0, :, pl.ds(ks, tk)]
        for h in range(DSA_HEADS):
            hl = head_lanes(h)
            s = cur_ref[:, hl]
            if general_bias:
                s = s + bias_ref[:, hl]
            s = jnp.where(sel, s, NEG_BIG)
            m_old = m_all[:, hl]
            m_new = jnp.maximum(m_old, jnp.max(s, axis=0, keepdims=True))
            m_ref[0:1, hl] = m_new
            alpha = jnp.exp2(m_old - m_new)
            pv = jnp.dot(v1, jnp.exp2(s - m_new).astype(BF16), preferred_element_type=F32)
            acc_ref[:, hl] = alpha * acc_ref[:, hl] + pv[0:DSA_HEAD_DIM, :]
            l_ref[0:1, hl] = alpha * l_all[:, hl] + pv[DSA_HEAD_DIM:DSA_HEAD_DIM + 1, :]

    bi = pl.program_id(0)

    def attend(kt, cur_ref, next_ref, last_pair, last_tile):
        far = (qmin_ref[bi, qi] - kmax_ref[bi, kt]) >= T5_FAR
        fast = jnp.logical_and(far, jnp.logical_not(any_split))
        plain = jnp.logical_and(jnp.logical_not(far), jnp.logical_not(any_split))

        @pl.when(fast)
        def _():
            attend_tile(kt, cur_ref, next_ref, general_bias=False, tie_split=False)

        @pl.when(plain)
        def _():
            attend_tile(kt, cur_ref, next_ref, general_bias=True, tie_split=False)

        @pl.when(any_split)
        def _():
            attend_tile(kt, cur_ref, next_ref, general_bias=True, tie_split=True)

    tile_pairs(attend, peel_last=False)

    for h in range(DSA_HEADS):
        hl = head_lanes(h)
        ot_ref[h * DSA_HEAD_DIM:(h + 1) * DSA_HEAD_DIM, :] = acc_ref[:, hl] / l_ref[0:1, hl]
    o_ref[0] = (ot_ref[...].T * gate_ref[0]).astype(o_ref.dtype)


def _dsa(rel_bias, qcw, qiw, kc, ki, v1, wi, pos_col, pos_row, gate, tq):
    b, s, _ = kc.shape
    n_sel = min(TOPK_MAX, s // 4)
    vw_rows = v1.shape[1]
    hw = DSA_HEADS * tq
    tbl = rel_bias.astype(F32) * LOG2E
    tbl = tbl - tbl[REL_BUCKETS - 1:REL_BUCKETS, :]
    tbl_t = jnp.pad(tbl.T, ((0, 0), (0, LANES - REL_BUCKETS)))
    q_min = jnp.min(pos_row.reshape(b, s // tq, tq), axis=-1)
    k_max = jnp.max(pos_row.reshape(b, s // DSA_KEY_TILE, DSA_KEY_TILE), axis=-1)
    tk = DSA_KEY_TILE
    vmem = (2 * (2 * _nbytes((DSA_HEAD_DIM, hw), BF16) + 2 * _nbytes((s, LANES), BF16) + _nbytes((vw_rows, s), F32)
                 + _nbytes((vw_rows, tq), F32) + _nbytes((s, LANES), I32) + _nbytes((tq, DSA_WIDTH), F32)
                 + _nbytes((tq, DSA_WIDTH), BF16)) + _nbytes((s, tq), I32) + 2 * _nbytes((tk, hw), F32)
            + _nbytes((tk, hw), BF16) + _nbytes((DSA_HEAD_DIM, hw), F32) + _nbytes((DSA_WIDTH, tq), F32)
            + 32 * _nbytes((tk, tq), F32))
    return pl.pallas_call(
        functools.partial(_dsa_kernel, n_sel=n_sel, tq=tq),
        grid=(b, s // tq),
        in_specs=[pl.BlockSpec(memory_space=pltpu.SMEM), pl.BlockSpec(memory_space=pltpu.SMEM),
                  pl.BlockSpec((DSA_HEADS, LANES), lambda bi, i: (0, 0)),
                  pl.BlockSpec((1, 1, DSA_HEAD_DIM, hw), lambda bi, i: (bi, i, 0, 0)),
                  pl.BlockSpec((1, 1, IDX_DIM, hw), lambda bi, i: (bi, i, 0, 0)),
                  pl.BlockSpec((1, s, DSA_HEAD_DIM), lambda bi, i: (bi, 0, 0)),
                  pl.BlockSpec((1, s, IDX_DIM), lambda bi, i: (bi, 0, 0)),
                  pl.BlockSpec((1, vw_rows, s), lambda bi, i: (bi, 0, 0)),
                  pl.BlockSpec((1, IDX_HEADS, tq), lambda bi, i: (bi, 0, i)),
                  pl.BlockSpec((1, s, 1), lambda bi, i: (bi, 0, 0)),
                  pl.BlockSpec((1, 1, tq), lambda bi, i: (bi, 0, i)),
                  pl.BlockSpec((1, tq, DSA_WIDTH), lambda bi, i: (bi, i, 0))],
        out_specs=pl.BlockSpec((1, tq, DSA_WIDTH), lambda bi, i: (bi, i, 0)),
        out_shape=jax.ShapeDtypeStruct((b, s, DSA_WIDTH), BF16),
        scratch_shapes=[pltpu.VMEM((s, tq), I32),
                        pltpu.VMEM((s, tq), I16), pltpu.VMEM((s, tq), I16),
                        pltpu.VMEM((tk, hw), F32), pltpu.VMEM((tk, hw), F32),
                        pltpu.VMEM((tk, hw), F32),
                        pltpu.VMEM((DSA_HEAD_DIM, hw), F32),
                        pltpu.VMEM((DSA_WIDTH, tq), F32),
                        pltpu.VMEM((SUBLANES, hw), F32), pltpu.VMEM((SUBLANES, hw), F32),
                        pltpu.VMEM((SUBLANES, tq), I32)],
        compiler_params=_compiler_params(("parallel", "arbitrary"), vmem),
        name="dsa",
    )(q_min, k_max, tbl_t, qcw, qiw, kc, ki, v1, wi, pos_col, pos_row, gate)


def _merge_kernel(oa_ref, ob_ref, oc_ref, g_ref, x_ref, p_ref, wa_ref, wb_ref, wc_ref, wo_ref, wpg_ref, wple_ref,
                  gn_ref, xo_ref, ho_ref):
    d = D_MODEL
    ya = jnp.dot(oa_ref[0], wa_ref[0], preferred_element_type=F32)
    yb = jnp.dot(ob_ref[0], wb_ref[0], preferred_element_type=F32)
    yc = jnp.dot(oc_ref[0], wc_ref[0], preferred_element_type=F32)
    merged = g_ref[0, :, 0:d] * ya + g_ref[0, :, d:2 * d] * yb + g_ref[0, :, 2 * d:3 * d] * yc
    x1 = x_ref[0] + jnp.dot(merged.astype(BF16), wo_ref[0], preferred_element_type=F32)
    ple_gate = _sigmoid(jnp.dot(x1.astype(BF16), wpg_ref[0], preferred_element_type=F32))
    x2 = x1 + ple_gate * jnp.dot(p_ref[0, 0].astype(BF16), wple_ref[0], preferred_element_type=F32)
    xo_ref[0] = x2
    ho_ref[0] = _rms(x2, gn_ref[...]).astype(ho_ref.dtype)


def _merge(oa, ob, oc, gates, x, p, layer, wa, wb, wc, wo, wpg, wple, g_next, h_dtype, tm=512):
    b, s, d = x.shape
    full = lambda a: pl.BlockSpec((1,) + a.shape[1:], lambda bi, i: (layer, 0, 0), pipeline_mode=pl.Buffered(1))
    row = lambda a: pl.BlockSpec((1, tm, a.shape[2]), lambda bi, i: (bi, i, 0))
    weights = (wa, wb, wc, wo, wpg, wple)
    acts = (oa, ob, oc, gates, x)
    vmem = (sum(_nbytes(a.shape[1:], a.dtype) for a in weights)
            + 2 * (sum(_nbytes((tm, a.shape[2]), a.dtype) for a in acts) + _nbytes((tm, p.shape[3]), p.dtype)
                   + 2 * _nbytes((tm, d), F32)) + 8 * _nbytes((tm, d), F32))
    return pl.pallas_call(
        _merge_kernel,
        grid=(b, s // tm),
        in_specs=[row(a) for a in acts] + [pl.BlockSpec((1, 1, tm, p.shape[3]), lambda bi, i: (layer, bi, i, 0))]
        + [full(a) for a in weights] + [pl.BlockSpec((1, d), lambda bi, i: (0, 0))],
        out_specs=[pl.BlockSpec((1, tm, d), lambda bi, i: (bi, i, 0)), pl.BlockSpec((1, tm, d), lambda bi, i: (bi, i, 0))],
        out_shape=[jax.ShapeDtypeStruct((b, s, d), F32), jax.ShapeDtypeStruct((b, s, d), h_dtype)],
        compiler_params=_compiler_params(("parallel", "parallel"), vmem),
        name="merge",
    )(*acts, p, *weights, g_next.reshape(1, d))


def _rotate_half_cols(w):
    half = w.shape[-1] // 2
    return jnp.concatenate([-w[..., half:], w[..., :half]], axis=-1)


def _prep_weights(w_in, w_uq, w_ukv):
    depth = w_in.shape[0]
    seg = lambda name: w_in[:, :, SPLIT[name][0]:SPLIT[name][1]]
    z = lambda *shape: jnp.zeros((depth,) + shape, w_in.dtype)
    d = D_MODEL
    kr = seg("k_rope")
    rope_lo = MLA_NOPE
    rope_pad = MLA_QK_PAD - MLA_NOPE - MLA_ROPE
    w = {}
    w["mla_in"] = jnp.concatenate(
        [seg("c_q"), seg("c_kv"), z(d, rope_lo), kr, z(d, rope_pad), z(d, rope_lo), _rotate_half_cols(kr), z(d, rope_pad)], axis=-1)
    w["gate_a"] = seg("gate_a")
    w["z"] = seg("z")
    w["xbc"] = seg("xbc")
    w["small"] = jnp.concatenate([seg("dt"), z(d, LANES - SSM_HEADS)], axis=-1)
    w["k_c"] = seg("k_c")
    w["k_idx"] = seg("k_idx")
    w["gate_c"] = seg("gate_c")
    w["merge"] = seg("merge")
    w["q_c_t"] = jnp.swapaxes(seg("q_c") * (DSA_HEAD_DIM ** -0.5 * LOG2E), 1, 2)
    w["q_idx_t"] = jnp.swapaxes(seg("q_idx"), 1, 2)
    w["vw_t"] = jnp.swapaxes(jnp.concatenate([seg("v_c"), seg("w_idx")], axis=-1), 1, 2)
    w = {
        "proj": jnp.concatenate([w["mla_in"]] + [w[seg[0]] for seg in PROJ_SEGMENTS], axis=-1).astype(BF16),
        "proj_t": jnp.concatenate([w["q_c_t"], w["q_idx_t"], w["vw_t"]], axis=1).astype(BF16),
    }

    uq =w_uq.reshape(depth, MLA_Q_LORA, MLA_HEADS, MLA_NOPE + MLA_ROPE)
    nope, rope = uq[..., :MLA_NOPE], uq[..., MLA_NOPE:]
    zq = lambda width: jnp.zeros((depth, MLA_Q_LORA, MLA_HEADS, width), w_uq.dtype)
    hq = MLA_HEADS * MLA_QK_PAD
    uq1 = jnp.concatenate([nope, rope, zq(rope_pad)], axis=-1).reshape(depth, MLA_Q_LORA, hq)
    uq2 = _rotate_half_cols(rope).reshape(depth, MLA_Q_LORA, MLA_HEADS * MLA_ROPE)
    w["uq1_t"] = jnp.swapaxes(uq1, 1, 2).astype(BF16)
    w["uq2_t"] = jnp.swapaxes(uq2, 1, 2).astype(BF16)
    ukv = w_ukv.reshape(depth, MLA_KV_LORA, MLA_HEADS, MLA_NOPE + MLA_V)
    zk = jnp.zeros((depth, MLA_KV_LORA, MLA_HEADS, MLA_QK_PAD - MLA_NOPE), w_ukv.dtype)
    w["uk"] = jnp.concatenate([ukv[..., :MLA_NOPE], zk], axis=-1).reshape(depth, MLA_KV_LORA, hq).astype(BF16)
    w["uv_t"] = jnp.swapaxes(ukv[..., MLA_NOPE:].reshape(depth, MLA_KV_LORA, MLA_WIDTH), 1, 2).astype(BF16)
    return w


def _rope_multipliers(positions):
    b, s = positions.shape
    inv_freq = 1.0 / (ROPE_THETA ** (jnp.arange(0, MLA_ROPE, 2, dtype=F32) / MLA_ROPE))
    ang = positions.astype(F32)[..., None] * inv_freq
    cos, sin = jnp.cos(ang), jnp.sin(ang)
    pad = MLA_QK_PAD - MLA_NOPE - MLA_ROPE
    m1 = jnp.concatenate([jnp.ones((b, s, MLA_NOPE), F32), cos, cos, jnp.zeros((b, s, pad), F32)], axis=-1)
    m2 = jnp.concatenate([jnp.zeros((b, s, MLA_NOPE), F32), sin, sin, jnp.zeros((b, s, pad), F32)], axis=-1)
    return m1, m2, jnp.swapaxes(m1, 1, 2), jnp.swapaxes(m2, 1, 2)


def kernel(x, p, positions, norm_g, w_in, mla_q_norm, w_uq, mla_kv_norm, w_ukv, conv_w, conv_b, dt_bias, a_log, d_skip, ssm_norm, w_br_a, w_br_b, w_br_c, w_out, rel_bias, w_ple, w_ple_gate, final_norm):
    b, s, d = x.shape
    depth = w_in.shape[0]
    w = _prep_weights(w_in, w_uq, w_ukv)
    m1, m2, m1t, m2t = _rope_multipliers(positions)
    pos_col = positions.astype(I32).reshape(b, s, 1)
    pos_row = positions.astype(I32).reshape(b, 1, s)
    wbr_a, wbr_b, wbr_c = w_br_a.astype(BF16), w_br_b.astype(BF16), w_br_c.astype(BF16)
    wo, wpg, wple = w_out.astype(BF16), w_ple_gate.astype(BF16), w_ple.astype(BF16)

    q_norm = mla_q_norm.astype(F32).reshape(depth, 1, -1)
    kv_norm = mla_kv_norm.astype(F32).reshape(depth, 1, -1)

    h = _norm(x, norm_g[0], BF16)
    for i in range(depth):
        pr = _proj_all(h, i, w["proj"], w["proj_t"], (m1, m2, m1t, m2t), q_norm, kv_norm,
                       w["uq1_t"], w["uq2_t"], w["uk"], w["uv_t"], DSA_Q_BLOCK)
        o_a = _mla_attn(pr["mla_q_t"], pr["mla_k"], pr["mla_v_t1"], pr["gate_a"])
        o_b = _ssd(pr["xbc"], pr["z"], pr["small"], conv_w[i], conv_b[i], dt_bias[i], a_log[i], d_skip[i], ssm_norm[i])
        o_c = _dsa(rel_bias, pr["q_c_wide"], pr["q_idx_wide"], pr["k_c"], pr["k_idx"], pr["v_t1"], pr["w_idx_t"], pos_col, pos_row,
                   pr["gate_c"], DSA_Q_BLOCK)

        last = i == depth - 1
        g_next = final_norm if last else norm_g[i + 1]
        x, h = _merge(o_a, o_b, o_c, pr["merge"], x, p, i, wbr_a, wbr_b, wbr_c, wo, wpg, wple, g_next,
                      F32 if last else BF16)
    return h
```

```python
import functools
import math

import jax
import jax.numpy as jnp
from jax import lax
from jax.experimental import pallas as pl
from jax.experimental.pallas import tpu as pltpu

F32 = jnp.float32
BF16 = jnp.bfloat16
I32 = jnp.int32
I16 = jnp.int16
HALF16 = 1 << 15

D_MODEL = 1024
NORM_EPS = 1e-6

MLA_HEADS = 8
MLA_NOPE = 64
MLA_ROPE = 32
MLA_V = 64
MLA_Q_LORA = 384
MLA_KV_LORA = 256
MLA_WIDTH = MLA_HEADS * MLA_V
ROPE_THETA = 10000.0
MLA_QK_PAD = 128
LOG2E = math.log2(math.e)
MLA_Q_SCALE = (MLA_NOPE + MLA_ROPE) ** -0.5 * LOG2E

SSM_HEADS = 16
SSM_HEAD_DIM = 64
SSM_INNER = SSM_HEADS * SSM_HEAD_DIM
SSM_GROUPS = 2
SSM_STATE = 128
SSM_CONV = 4
SSM_CHUNK = 128
SSM_CONV_DIM = SSM_INNER + 2 * SSM_GROUPS * SSM_STATE
SSM_HEADS_PER_GROUP = SSM_HEADS // SSM_GROUPS
ONES_ROWS = 16
MLA_V1_ROWS = MLA_V + ONES_ROWS
SSM_CONV_TAIL = 16

DSA_HEADS = 8
DSA_HEAD_DIM = 64
DSA_WIDTH = DSA_HEADS * DSA_HEAD_DIM
IDX_HEADS = 8
IDX_DIM = 64
TOPK_MAX = 256
DSA_KEY_TILE = 128
DSA_Q_BLOCK = 256

REL_BUCKETS = 32
REL_MAX_DIST = 128
N_BRANCHES = 3

LANES = 128
SUBLANES = 8
VMEM_LIMIT_CAP = 56 * 1024 * 1024
VMEM_LIMIT_FLOOR = 32 * 1024 * 1024
INT_MIN = -(2 ** 31)
NEG_BIG = -1e30

SPLIT_SIZES = (
    MLA_Q_LORA, MLA_KV_LORA, MLA_ROPE, MLA_WIDTH, SSM_INNER, SSM_CONV_DIM, SSM_HEADS, DSA_WIDTH,
    DSA_HEAD_DIM, DSA_HEAD_DIM, IDX_HEADS * IDX_DIM, IDX_DIM, IDX_HEADS, DSA_WIDTH,
    N_BRANCHES * D_MODEL,
)
SPLIT_NAMES = ("c_q", "c_kv", "k_rope", "gate_a", "z", "xbc", "dt", "q_c", "k_c", "v_c", "q_idx",
               "k_idx", "w_idx", "gate_c", "merge")


def _split_bounds():
    out, off = {}, 0
    for name, size in zip(SPLIT_NAMES, SPLIT_SIZES, strict=True):
        out[name] = (off, off + size)
        off += size
    return out


SPLIT = _split_bounds()


def _t5_large_thresholds():
    exact = REL_BUCKETS // 2
    thr = []
    for j in range(1, REL_BUCKETS - exact):
        thr.append(int(math.ceil(exact * (REL_MAX_DIST / exact) ** (j / (REL_BUCKETS - exact)) - 1e-9)))
    return tuple(thr)


T5_EXACT = REL_BUCKETS // 2
T5_LARGE_THR = _t5_large_thresholds()
T5_FAR = T5_LARGE_THR[-1]


def _compiler_params(semantics, vmem_bytes):
    limit = int(min(VMEM_LIMIT_CAP, max(VMEM_LIMIT_FLOOR, vmem_bytes)))
    return pltpu.CompilerParams(dimension_semantics=semantics, vmem_limit_bytes=limit)


def _nbytes(shape, dtype):
    return math.prod(shape) * jnp.dtype(dtype).itemsize


def _sigmoid(x):
    return 1.0 / (1.0 + jnp.exp(-x))


def _rms(x, g):
    return x * lax.rsqrt(jnp.mean(x * x, axis=-1, keepdims=True) + NORM_EPS) * g


def _norm_kernel(x_ref, g_ref, o_ref):
    o_ref[0] = _rms(x_ref[0], g_ref[...]).astype(o_ref.dtype)


def _norm(x, g, out_dtype, tm=512):
    b, s, d = x.shape
    return pl.pallas_call(
        _norm_kernel,
        grid=(b, s // tm),
        in_specs=[pl.BlockSpec((1, tm, d), lambda bi, i: (bi, i, 0)), pl.BlockSpec((1, d), lambda bi, i: (0, 0))],
        out_specs=pl.BlockSpec((1, tm, d), lambda bi, i: (bi, i, 0)),
        out_shape=jax.ShapeDtypeStruct((b, s, d), out_dtype),
        compiler_params=_compiler_params(("parallel", "parallel"), 4 * _nbytes((tm, d), F32)),
        name="rms_norm",
    )(x, g.reshape(1, d))


MLA_IN_WIDTH = MLA_Q_LORA + MLA_KV_LORA + 2 * LANES
PROJ_SEGMENTS = (
    ("small", LANES, None, F32),
    ("gate_a", MLA_WIDTH, "silu", BF16),
    ("gate_c", DSA_WIDTH, "silu", BF16),
    ("z", SSM_INNER, "silu", BF16),
    ("xbc", SSM_CONV_DIM, None, BF16),
    ("merge", N_BRANCHES * D_MODEL, "sigmoid", BF16),
    ("k_c", DSA_HEAD_DIM, None, BF16),
    ("k_idx", IDX_DIM, None, BF16),
)
PROJ_CHUNK = 512
PROJ_T_ROWS = (DSA_WIDTH, IDX_HEADS * IDX_DIM, DSA_HEAD_DIM + IDX_HEADS)


def _proj_all_kernel(h_ref, w_ref, wt_ref, m1_ref, m2_ref, m1t_ref, m2t_ref, qn_ref, kvn_ref, wq1t_ref, wq2t_ref, wk_ref,
                     wvt_ref, *out_refs):
    h = h_ref[0]
    n_seg = len(PROJ_SEGMENTS)
    a = jnp.dot(h, w_ref[0, :, 0:MLA_IN_WIDTH], preferred_element_type=F32)
    _mla_prep(a, m1_ref, m2_ref, m1t_ref, m2t_ref, qn_ref, kvn_ref, wq1t_ref, wq2t_ref, wk_ref, wvt_ref, *out_refs[n_seg + 4:])
    off = MLA_IN_WIDTH
    for (_, width, act, _), o_ref in zip(PROJ_SEGMENTS, out_refs[:n_seg], strict=True):
        for c in range(0, width, PROJ_CHUNK):
            cw = min(PROJ_CHUNK, width - c)
            y = jnp.dot(h, w_ref[0, :, off + c:off + c + cw], preferred_element_type=F32)
            if act == "silu":
                y = y * _sigmoid(y)
            elif act == "sigmoid":
                y = _sigmoid(y)
            o_ref[0, :, c:c + cw] = y.astype(o_ref.dtype)
        off += width
    qcw_ref, qiw_ref, v1_ref, wi_ref = out_refs[n_seg:n_seg + 4]
    nt = (((1,), (1,)), ((), ()))
    tq = h.shape[0]
    row = 0
    for o_ref, heads, dim in ((qcw_ref, DSA_HEADS, DSA_HEAD_DIM), (qiw_ref, IDX_HEADS, IDX_DIM)):
        y = lax.dot_general(wt_ref[0, row:row + heads * dim, :], h, nt, preferred_element_type=F32)
        for hd in range(heads):
            o_ref[0, 0, :, hd * tq:(hd + 1) * tq] = y[hd * dim:(hd + 1) * dim, :].astype(o_ref.dtype)
        row += heads * dim
    vw = lax.dot_general(wt_ref[0, row:row + PROJ_T_ROWS[2], :], h, nt, preferred_element_type=F32)
    v1_ref[0, 0:DSA_HEAD_DIM, :] = vw[0:DSA_HEAD_DIM, :].astype(v1_ref.dtype)
    v1_ref[0, DSA_HEAD_DIM:, :] = jnp.ones((ONES_ROWS, tq), v1_ref.dtype)
    wi_ref[0] = vw[DSA_HEAD_DIM:, :]


def _proj_all(h3d, layer, w, wt, rope, qn, kvn, wq1t, wq2t, wk, wvt, tq):
    b, s, k = h3d.shape
    n_all = MLA_IN_WIDTH + sum(seg[1] for seg in PROJ_SEGMENTS)
    hq = MLA_HEADS * MLA_QK_PAD
    v1_rows = DSA_HEAD_DIM + ONES_ROWS
    out_shapes = [jax.ShapeDtypeStruct((b, s, width), dt) for _, width, _, dt in PROJ_SEGMENTS]
    out_specs = [pl.BlockSpec((1, tq, width), lambda bi, i: (bi, i, 0)) for _, width, _, _ in PROJ_SEGMENTS]
    for heads, dim in ((DSA_HEADS, DSA_HEAD_DIM), (IDX_HEADS, IDX_DIM)):
        out_shapes.append(jax.ShapeDtypeStruct((b, s // tq, dim, heads * tq), BF16))
        out_specs.append(pl.BlockSpec((1, 1, dim, heads * tq), lambda bi, i: (bi, i, 0, 0)))
    for rows, dt in ((v1_rows, BF16), (IDX_HEADS, F32)):
        out_shapes.append(jax.ShapeDtypeStruct((b, rows, s), dt))
        out_specs.append(pl.BlockSpec((1, rows, tq), lambda bi, i: (bi, 0, i)))
    for rows in (hq, None, MLA_HEADS * MLA_V1_ROWS):
        if rows is None:
            out_shapes.append(jax.ShapeDtypeStruct((b, s, hq), BF16))
            out_specs.append(pl.BlockSpec((1, tq, hq), lambda bi, i: (bi, i, 0)))
        else:
            out_shapes.append(jax.ShapeDtypeStruct((b, rows, s), BF16))
            out_specs.append(pl.BlockSpec((1, rows, tq), lambda bi, i: (bi, 0, i)))
    resident = pl.Buffered(1)
    weights = (w, wt, qn, kvn, wq1t, wq2t, wk, wvt)
    layer_spec = lambda a: pl.BlockSpec((1,) + a.shape[1:], lambda bi, i: (layer,) + (0,) * (a.ndim - 1), pipeline_mode=resident)
    m1, m2, m1t, m2t = rope
    vmem = (sum(_nbytes(a.shape[1:], a.dtype) for a in weights) + 2 * _nbytes((tq, k), BF16) + 8 * _nbytes((tq, LANES), F32)
            + 2 * sum(_nbytes((tq, width), dt) for _, width, _, dt in PROJ_SEGMENTS)
            + 2 * (2 * _nbytes((DSA_HEAD_DIM, DSA_HEADS * tq), BF16) + _nbytes((v1_rows + IDX_HEADS, tq), F32))
            + 2 * (2 * _nbytes((tq, hq), BF16) + _nbytes((MLA_HEADS * MLA_V1_ROWS, tq), BF16))
            + 8 * _nbytes((tq, PROJ_CHUNK), F32) + 6 * _nbytes((tq, hq), F32))
    outs = pl.pallas_call(
        _proj_all_kernel,
        grid=(b, s // tq),
        in_specs=[pl.BlockSpec((1, tq, k), lambda bi, i: (bi, i, 0)), layer_spec(w), layer_spec(wt),
                  pl.BlockSpec((1, tq, LANES), lambda bi, i: (bi, i, 0)), pl.BlockSpec((1, tq, LANES), lambda bi, i: (bi, i, 0)),
                  pl.BlockSpec((1, LANES, tq), lambda bi, i: (bi, 0, i)), pl.BlockSpec((1, LANES, tq), lambda bi, i: (bi, 0, i))]
        + [layer_spec(a) for a in weights[2:]],
        out_specs=out_specs,
        out_shape=out_shapes,
        compiler_params=_compiler_params(("parallel", "parallel"), vmem),
        name="proj_all",
    )(h3d, w, wt, m1, m2, m1t, m2t, *weights[2:])
    names = [seg[0] for seg in PROJ_SEGMENTS] + ["q_c_wide", "q_idx_wide", "v_t1", "w_idx_t", "mla_q_t", "mla_k", "mla_v_t1"]
    return dict(zip(names, outs, strict=True))


def _mla_prep(a, m1_ref, m2_ref, m1t_ref, m2t_ref, qn_ref, kvn_ref, wq1t_ref, wq2t_ref, wk_ref, wvt_ref, qt_ref, k_ref, vt_ref):
    c_q = a[:, :MLA_Q_LORA]
    c_kv = a[:, MLA_Q_LORA:MLA_Q_LORA + MLA_KV_LORA]
    kr1 = a[:, MLA_Q_LORA + MLA_KV_LORA:MLA_Q_LORA + MLA_KV_LORA + LANES]
    kr2 = a[:, MLA_Q_LORA + MLA_KV_LORA + LANES:]
    cqn = _rms(c_q, qn_ref[0]).astype(BF16)
    ckvn = _rms(c_kv, kvn_ref[0]).astype(BF16)
    nt = (((1,), (1,)), ((), ()))
    qa_t = lax.dot_general(wq1t_ref[0], cqn, nt, preferred_element_type=F32)
    qb_t = lax.dot_general(wq2t_ref[0], cqn, nt, preferred_element_type=F32)
    kn = jnp.dot(ckvn, wk_ref[0], preferred_element_type=F32)
    kr = kr1 * m1_ref[0] + kr2 * m2_ref[0]
    rope = slice(MLA_NOPE, MLA_NOPE + MLA_ROPE)
    cos_t = m1t_ref[0, rope, :] * MLA_Q_SCALE
    sin_t = m2t_ref[0, rope, :] * MLA_Q_SCALE
    pad_rows = jnp.zeros((MLA_QK_PAD - MLA_NOPE - MLA_ROPE, qa_t.shape[1]), qt_ref.dtype)
    for h in range(MLA_HEADS):
        base = h * MLA_QK_PAD
        sl = slice(base, base + MLA_QK_PAD)
        qt_ref[0, base:base + MLA_NOPE, :] = (qa_t[base:base + MLA_NOPE, :] * MLA_Q_SCALE).astype(qt_ref.dtype)
        qt_ref[0, base + MLA_NOPE:base + MLA_NOPE + MLA_ROPE, :] = (
            qa_t[base + MLA_NOPE:base + MLA_NOPE + MLA_ROPE, :] * cos_t
            + qb_t[h * MLA_ROPE:(h + 1) * MLA_ROPE, :] * sin_t).astype(qt_ref.dtype)
        qt_ref[0, base + MLA_NOPE + MLA_ROPE:base + MLA_QK_PAD, :] = pad_rows
        k_ref[0, :, sl] = (kn[:, sl] + kr).astype(k_ref.dtype)
    v_t = lax.dot_general(wvt_ref[0], ckvn, nt, preferred_element_type=F32)
    ones = jnp.ones((ONES_ROWS, v_t.shape[1]), vt_ref.dtype)
    for h in range(MLA_HEADS):
        base = h * MLA_V1_ROWS
        vt_ref[0, base:base + MLA_V, :] = v_t[h * MLA_V:(h + 1) * MLA_V, :].astype(vt_ref.dtype)
        vt_ref[0, base + MLA_V:base + MLA_V1_ROWS, :] = ones


def _mla_attn_kernel(qt_ref, k_ref, vt_ref, g_ref, o_ref, m_ref, l_ref, acc_ref, s0_ref, s1_ref, *, tq, tk):
    assert tq == 2 * tk
    qi = pl.program_id(1)
    krow = lax.broadcasted_iota(I32, (tk, tq), 0)
    qcol = lax.broadcasted_iota(I32, (tk, tq), 1)
    tiles_per_q = tq // tk
    m_ref[...] = jnp.full(m_ref.shape, -jnp.inf, F32)
    l_ref[...] = jnp.zeros(l_ref.shape, F32)
    acc_ref[...] = jnp.zeros(acc_ref.shape, F32)

    staged = (s0_ref, s1_ref)
    n_tiles = (qi + 1) * tiles_per_q

    def head_lanes(h):
        return slice(h * tq, (h + 1) * tq)

    def stage(j, dst_ref):
        ks = pl.multiple_of(j * tk, tk)
        for h in range(MLA_HEADS):
            k = k_ref[0, pl.ds(ks, tk), h * MLA_QK_PAD:(h + 1) * MLA_QK_PAD]
            dst_ref[:, head_lanes(h)] = jnp.dot(k, qt_ref[0, h * MLA_QK_PAD:(h + 1) * MLA_QK_PAD, :],
                                                preferred_element_type=F32)

    def tile(j, cur_ref, next_ref, diagonal):
        ks = pl.multiple_of(j * tk, tk)
        stage(jnp.minimum(j + 1, n_tiles - 1), next_ref)
        m_all = m_ref[...]
        l_all = l_ref[...]
        if diagonal:
            causal = ks + krow <= qi * tq + qcol
        for h in range(MLA_HEADS):
            s = cur_ref[:, head_lanes(h)]
            if diagonal:
                s = jnp.where(causal, s, -jnp.inf)
            m_old = m_all[h:h + 1, :]
            m_new = jnp.maximum(m_old, jnp.max(s, axis=0, keepdims=True))
            alpha = jnp.exp2(m_old - m_new)
            p = jnp.exp2(s - m_new).astype(BF16)
            m_ref[h:h + 1, :] = m_new
            pv = jnp.dot(vt_ref[0, h * MLA_V1_ROWS:(h + 1) * MLA_V1_ROWS, pl.ds(ks, tk)], p, preferred_element_type=F32)
            hs = slice(h * MLA_V, (h + 1) * MLA_V)
            acc_ref[hs, :] = alpha * acc_ref[hs, :] + pv[0:MLA_V, :]
            l_ref[h:h + 1, :] = alpha * l_all[h:h + 1, :] + pv[MLA_V:MLA_V + 1, :]

    stage(0, staged[0])

    def full_pair(pi, carry):
        for half in range(2):
            tile(2 * pi + half, staged[half], staged[1 - half], diagonal=False)
        return carry

    lax.fori_loop(0, qi, full_pair, 0)
    for half in range(2):
        tile(2 * qi + half, staged[half], staged[1 - half], diagonal=True)
    for h in range(MLA_HEADS):
        hs = slice(h * MLA_V, (h + 1) * MLA_V)
        acc_ref[hs, :] = acc_ref[hs, :] / l_ref[h:h + 1, :]
    o_ref[0] = (acc_ref[...].T * g_ref[0]).astype(o_ref.dtype)


def _mla_attn(qt, k, vt, gate, tq=256, tk=128):
    b, s, hq = k.shape
    vmem = 2 * (_nbytes((hq, tq), BF16) + _nbytes((s, hq), BF16) + _nbytes((MLA_WIDTH, s), BF16)
                + _nbytes((tq, MLA_WIDTH), F32) + _nbytes((tq, MLA_WIDTH), BF16)) + _nbytes((MLA_WIDTH, tq), F32) + 16 * _nbytes((tk, tq), F32)
    return pl.pallas_call(
        functools.partial(_mla_attn_kernel, tq=tq, tk=tk),
        grid=(b, s // tq),
        in_specs=[pl.BlockSpec((1, hq, tq), lambda bi, i: (bi, 0, i)),
                  pl.BlockSpec((1, s, hq), lambda bi, i: (bi, 0, 0)),
                  pl.BlockSpec((1, MLA_HEADS * MLA_V1_ROWS, s), lambda bi, i: (bi, 0, 0)),
                  pl.BlockSpec((1, tq, MLA_WIDTH), lambda bi, i: (bi, i, 0))],
        out_specs=pl.BlockSpec((1, tq, MLA_WIDTH), lambda bi, i: (bi, i, 0)),
        out_shape=jax.ShapeDtypeStruct((b, s, MLA_WIDTH), BF16),
        scratch_shapes=[pltpu.VMEM((MLA_HEADS, tq), F32), pltpu.VMEM((MLA_HEADS, tq), F32),
                        pltpu.VMEM((MLA_WIDTH, tq), F32),
                        pltpu.VMEM((tk, MLA_HEADS * tq), F32), pltpu.VMEM((tk, MLA_HEADS * tq), F32)],
        compiler_params=_compiler_params(("parallel", "arbitrary"), vmem),
        name="mla_attn",
    )(qt, k, vt, gate)


def _ssd_kernel(xbc_ref, xprev_ref, zs_ref, sm_ref, cw_ref, cb_ref, dtb_ref, alog_ref, dskf_ref, nrm_ref, exp16_ref, exp32_ref,
                o_ref, state_ref, y_ref):
    c = pl.program_id(1)
    q = SSM_CHUNK

    @pl.when(c == 0)
    def _():
        state_ref[...] = jnp.zeros(state_ref.shape, F32)

    x_in = xbc_ref[0]
    tail = xprev_ref.shape[1]
    prev = jnp.where(c > 0, xprev_ref[0].astype(F32), 0.0).astype(x_in.dtype)
    win = jnp.concatenate([prev, x_in], axis=0)
    x = x_in.astype(F32)
    wrow = lax.broadcasted_iota(I32, (q, tail + q), 0)
    wcol = lax.broadcasted_iota(I32, (q, tail + q), 1)
    acc = cb_ref[...] + cw_ref[SSM_CONV - 1:SSM_CONV, :] * x
    for j in range(1, SSM_CONV):
        pick = jnp.where(wcol == wrow + (tail - j), 1.0, 0.0).astype(win.dtype)
        acc = acc + cw_ref[SSM_CONV - 1 - j:SSM_CONV - j, :] * jnp.dot(pick, win, preferred_element_type=F32)
    half = 0.5 * acc
    xc = half + half * jnp.tanh(half)

    xs = xc[:, :SSM_INNER]
    bm = xc[:, SSM_INNER:SSM_INNER + SSM_GROUPS * SSM_STATE]
    cm = xc[:, SSM_INNER + SSM_GROUPS * SSM_STATE:]

    pre = sm_ref[0] + dtb_ref[...]
    dt = jnp.maximum(pre, 0.0) + jnp.log1p(jnp.exp(-jnp.abs(pre)))
    a = -jnp.exp(alog_ref[...])
    row = lax.broadcasted_iota(I32, (q, q), 0)
    col = lax.broadcasted_iota(I32, (q, q), 1)
    lower = row >= col
    cum = jnp.dot(jnp.where(lower, 1.0, 0.0).astype(F32), dt * a, preferred_element_type=F32,
                  precision=lax.Precision.HIGHEST)
    cum_t = cum.T
    cum_last = cum[q - 1:q, :]
    per_head = jnp.concatenate([dt, jnp.exp(cum_last - cum), jnp.exp(cum)], axis=0).astype(BF16)
    spread = jnp.dot(per_head, exp16_ref[...], preferred_element_type=F32)
    dt_full, to_end_full, e_cum_full = spread[0:q], spread[q:2 * q], spread[2 * q:3 * q]
    chunk_decay_full = jnp.dot(jnp.broadcast_to(jnp.exp(cum_last), (SUBLANES, LANES)), exp32_ref[...],
                               preferred_element_type=F32, precision=lax.Precision.HIGHEST)[0:1]
    xdt = xs * dt_full
    xw = (xdt * to_end_full).astype(BF16)
    first_head = lax.broadcasted_iota(I32, (q, LANES), 1) < SSM_HEAD_DIM
    group_w = SSM_HEADS_PER_GROUP * SSM_HEAD_DIM
    heads_per_tile = LANES // SSM_HEAD_DIM

    for g in range(SSM_GROUPS):
        bg32 = bm[:, g * SSM_STATE:(g + 1) * SSM_STATE]
        bg = bg32.astype(BF16)
        cg = cm[:, g * SSM_STATE:(g + 1) * SSM_STATE].astype(BF16)
        cb = lax.dot_general(cg, bg, (((1,), (1,)), ((), ())), preferred_element_type=F32)
        gl = slice(g * group_w, (g + 1) * group_w)
        st = state_ref[g]
        y_off = jnp.dot(cg, st.astype(BF16), preferred_element_type=F32)
        state_ref[g] = st * chunk_decay_full[:, gl] + jnp.dot(bg32.T.astype(BF16), xw[:, gl], preferred_element_type=F32)
        for tile in range(group_w // LANES):
            h0 = g * SSM_HEADS_PER_GROUP + tile * heads_per_tile
            tl = slice(h0 * SSM_HEAD_DIM, h0 * SSM_HEAD_DIM + LANES)
            g_both = []
            for h in range(h0, h0 + heads_per_tile):
                diff = cum[:, h:h + 1] - cum_t[h:h + 1, :]
                decay_in = jnp.exp(jnp.where(lower, diff, -jnp.inf))
                g_both.append((cb * decay_in).astype(BF16))
            xdt_tile = xdt[:, tl]
            stacked = jnp.concatenate([jnp.where(first_head, xdt_tile, 0.0).astype(BF16),
                                       jnp.where(first_head, 0.0, xdt_tile).astype(BF16)], axis=0)
            y_diag = jnp.dot(jnp.concatenate(g_both, axis=1), stacked, preferred_element_type=F32)
            y = y_diag + y_off[:, tile * LANES:(tile + 1) * LANES] * e_cum_full[:, tl] + xs[:, tl] * dskf_ref[:, tl]
            y_ref[:, tl] = y * zs_ref[0, :, tl]

    o_ref[0] = _rms(y_ref[...], nrm_ref[...]).astype(o_ref.dtype)


def _ssd(xbc, zs, small, conv_w, conv_b, dt_bias, a_log, d_skip, ssm_norm):
    b, s, _ = xbc.shape
    q = SSM_CHUNK
    pad = lambda v: jnp.pad(v.astype(F32), (0, LANES - v.shape[0])).reshape(1, LANES)
    full = lambda shape: pl.BlockSpec(shape, lambda bi, c: (0,) * len(shape))
    blk = lambda width: pl.BlockSpec((1, q, width), lambda bi, c: (bi, c, 0))
    expand = (jnp.arange(LANES)[:, None] == jnp.arange(SSM_INNER)[None, :] // SSM_HEAD_DIM).astype(F32)
    d_skip_full = jnp.repeat(d_skip.astype(F32), SSM_HEAD_DIM).reshape(1, SSM_INNER)
    group_w = SSM_HEADS_PER_GROUP * SSM_HEAD_DIM
    vmem = (2 * (_nbytes((q, SSM_CONV_DIM), xbc.dtype) + _nbytes((q, SSM_INNER), zs.dtype) + _nbytes((q, LANES), F32)
                 + _nbytes((q, SSM_INNER), BF16) + _nbytes((LANES, SSM_INNER), BF16) + _nbytes((LANES, SSM_INNER), F32))
            + _nbytes((2 * q, SSM_CONV_DIM), xbc.dtype) + _nbytes((SSM_GROUPS, SSM_STATE, group_w), F32)
            + _nbytes((q, SSM_INNER), F32) + 10 * _nbytes((q, SSM_CONV_DIM), F32))
    return pl.pallas_call(
        _ssd_kernel,
        grid=(b, s // q),
        in_specs=[blk(SSM_CONV_DIM),
                  pl.BlockSpec((1, SSM_CONV_TAIL, SSM_CONV_DIM),
                               lambda bi, c: (bi, jnp.maximum(c * (q // SSM_CONV_TAIL) - 1, 0), 0)),
                  blk(SSM_INNER), blk(LANES), full((SSM_CONV, SSM_CONV_DIM)), full((1, SSM_CONV_DIM)),
                  full((1, LANES)), full((1, LANES)), full((1, SSM_INNER)), full((1, SSM_INNER)),
                  full((LANES, SSM_INNER)), full((LANES, SSM_INNER))],
        out_specs=blk(SSM_INNER),
        out_shape=jax.ShapeDtypeStruct((b, s, SSM_INNER), BF16),
        scratch_shapes=[pltpu.VMEM((SSM_GROUPS, SSM_STATE, group_w), F32),
                        pltpu.VMEM((q, SSM_INNER), F32)],
        compiler_params=_compiler_params(("parallel", "arbitrary"), vmem),
        name="ssd",
    )(xbc, xbc, zs, small, conv_w.astype(F32), conv_b.reshape(1, -1).astype(F32), pad(dt_bias), pad(a_log), d_skip_full,
      ssm_norm.reshape(1, -1).astype(F32), expand.astype(BF16), expand)


def _t5_bucket_of(n):
    steps_per_octave = (REL_BUCKETS - T5_EXACT) / math.log2(REL_MAX_DIST / T5_EXACT)
    x = jnp.maximum(n, T5_EXACT).astype(F32) * (1.0 / T5_EXACT)
    large = T5_EXACT + jnp.floor(jnp.log2(x) * steps_per_octave).astype(I32)
    return jnp.where(n <= T5_EXACT, n, jnp.minimum(large, REL_BUCKETS - 1))


def _dsa_kernel(qmin_ref, kmax_ref, tblt_ref, qcw_ref, qiw_ref, kc_ref, ki_ref, v1_ref, wi_ref, posk_ref, posq_ref, gate_ref, o_ref,
                key_ref, hi_ref, lo_ref, s0_ref, s1_ref, bias_ref, acc_ref, ot_ref, m_ref, l_ref, tie_ref, *, n_sel, tq):
    tk = DSA_KEY_TILE
    qi = pl.program_id(1)
    n_tiles = (qi + 1) * (tq // tk)
    krow = lax.broadcasted_iota(I32, (tk, tq), 0)
    qcol = lax.broadcasted_iota(I32, (tk, tq), 1)
    w = wi_ref[0]
    posq = posq_ref[0]

    def tile_start(kt):
        return pl.multiple_of(kt * tk, tk)

    def causal_of(ks):
        return (ks + krow) <= (qi * tq + qcol)

    def head_lanes(h):
        return slice(h * tq, (h + 1) * tq)

    staged = (s0_ref, s1_ref)

    def stage(k_ref, q_ref, kt, dst_ref):
        dst_ref[...] = jnp.dot(k_ref[0, pl.ds(tile_start(kt), tk), :], q_ref[0, 0], preferred_element_type=F32)

    def tile_pairs(tile_fn, peel_last):
        def pair(pi, last_pair):
            for half in range(2):
                tile_fn(2 * pi + half, staged[half], staged[1 - half], last_pair, last_pair and half == 1)

        def body(pi, carry):
            pair(pi, False)
            return carry

        if peel_last:
            lax.fori_loop(0, n_tiles // 2 - 1, body, 0)
            pair(n_tiles // 2 - 1, True)
        else:
            lax.fori_loop(0, n_tiles // 2, body, 0)

    stage(ki_ref, qiw_ref, 0, staged[0])

    def score_tile(kt, cur_ref, next_ref, last_pair, last_tile):
        ks = tile_start(kt)
        if last_tile:
            stage(kc_ref, qcw_ref, 0, next_ref)
        else:
            stage(ki_ref, qiw_ref, kt + 1, next_ref)
        sc = jnp.zeros((tk, tq), F32)
        for h in range(IDX_HEADS):
            sc = sc + w[h:h + 1, :] * jnp.maximum(cur_ref[:, head_lanes(h)], 0.0)
        sc = jnp.where(sc == 0.0, 0.0, sc)
        bits = lax.bitcast_convert_type(sc, I32)
        key = bits ^ ((bits >> 31) & 0x7FFFFFFF)
        if last_pair:
            key = jnp.where(causal_of(ks), key, INT_MIN)
        key_ref[pl.ds(ks, tk), :] = key
        hi_ref[pl.ds(ks, tk), :] = (key >> 16).astype(I16)
        lo_ref[pl.ds(ks, tk), :] = ((key & 0xFFFF) - HALF16).astype(I16)

    tile_pairs(score_tile, peel_last=True)

    def count16(half_ref, pred):
        def body(pi, cnt):
            for kt in (2 * pi, 2 * pi + 1):
                cnt = cnt + jnp.where(pred(half_ref[pl.ds(tile_start(kt), tk), :]), jnp.int16(1), jnp.int16(0))
            return cnt
        cnt = lax.fori_loop(0, n_tiles // 2, body, jnp.zeros((tk, tq), I16))
        rows = 2 * SUBLANES
        parts = [cnt[r:r + rows, :] for r in range(0, tk, rows)]
        while len(parts) > 1:
            parts = [a + b for a, b in zip(parts[0::2], parts[1::2], strict=True)]
        return jnp.sum(parts[0].astype(I32), axis=0, keepdims=True)

    def nth_largest16(half_ref, rank):
        def bit_step(i, cand):
            trial = cand | lax.shift_left(jnp.int32(1), 15 - i)
            thr = (trial - HALF16).astype(I16)
            tot = count16(half_ref, lambda v: v >= thr)
            return jnp.where(tot >= rank, trial, cand)
        return lax.fori_loop(0, 16, bit_step, jnp.zeros((1, tq), I32)) - HALF16

    tau_hi = nth_largest16(hi_ref, n_sel)
    tau_hi16 = tau_hi.astype(I16)
    above_hi = count16(hi_ref, lambda v: v > tau_hi16)

    def mask_low(kt, carry):
        rows = pl.ds(tile_start(kt), tk)
        lo_ref[rows, :] = jnp.where(hi_ref[rows, :] == tau_hi16, lo_ref[rows, :], jnp.int16(-HALF16))
        return carry

    lax.fori_loop(0, n_tiles, mask_low, 0)
    tau_lo = nth_largest16(lo_ref, n_sel - above_hi)
    tau_lo16 = tau_lo.astype(I16)
    tau = lax.shift_left(tau_hi, 16) | (tau_lo + HALF16)
    n_gt = above_hi + count16(lo_ref, lambda v: v > tau_lo16)
    n_ge = above_hi + count16(lo_ref, lambda v: v >= tau_lo16)
    need = n_sel - n_gt

    m_ref[...] = jnp.full(m_ref.shape, NEG_BIG, F32)
    l_ref[...] = jnp.zeros(l_ref.shape, F32)
    acc_ref[...] = jnp.zeros(acc_ref.shape, F32)
    tie_ref[...] = jnp.zeros(tie_ref.shape, I32)
    lower_incl = jnp.where(lax.broadcasted_iota(I32, (tk, tk), 0) >= lax.broadcasted_iota(I32, (tk, tk), 1),
                           1.0, 0.0).astype(BF16)

    splits =jnp.where((n_ge > n_sel) & (tau != INT_MIN), 1.0, 0.0)
    any_split = jnp.max(splits) > 0.0

    def attend_tile(kt, cur_ref, next_ref, general_bias, tie_split):
        ks = tile_start(kt)
        key = key_ref[pl.ds(ks, tk), :]
        if tie_split:
            eq = key == tau
            prefix = jnp.dot(lower_incl, jnp.where(eq, 1.0, 0.0).astype(BF16), preferred_element_type=F32)
            tie_before = tie_ref[0:1, :]
            tie_rank = tie_before + prefix.astype(I32)
            sel = ((key > tau) | (eq & (tie_rank <= need))) & causal_of(ks)
            tie_ref[0:1, :] = tie_before + prefix[tk - 1:tk, :].astype(I32)
        else:
            sel = (key >= tau) & causal_of(ks)

        if general_bias:
            posk = posk_ref[0, pl.ds(ks, tk), :]
            bucket = _t5_bucket_of(jnp.maximum(posq - posk, 0))
            for hp in range(DSA_HEADS // 2):
                table = jnp.broadcast_to(tblt_ref[hp:hp + 1, :], (tk, LANES))
                for c in range(tq // LANES):
                    both = jnp.take_along_axis(table, bucket[:, c * LANES:(c + 1) * LANES], axis=1)
                    lo, hi = 2 * hp * tq + c * LANES, (2 * hp + 1) * tq + c * LANES
                    bias_ref[:, lo:lo + LANES] = lax.bitcast_convert_type(both & jnp.int32(-65536), F32)
                    bias_ref[:, hi:hi + LANES] = lax.bitcast_convert_type(lax.shift_left(both, 16), F32)
        stage(kc_ref, qcw_ref, jnp.minimum(kt + 1, n_tiles - 1), next_ref)
        m_all = m_ref[0:1, :]
        l_all = l_ref[0:1, :]
        v1 = v1_ref[0, :, pl.ds(ks, tk)]
        for h in range(DSA_HEADS):
            hl = head_lanes(h)
            s = cur_ref[:, hl]
            if general_bias:
                s = s + bias_ref[:, hl]
            s = jnp.where(sel, s, NEG_BIG)
            m_old = m_all[:, hl]
            m_new = jnp.maximum(m_old, jnp.max(s, axis=0, keepdims=True))
            m_ref[0:1, hl] = m_new
            alpha = jnp.exp2(m_old - m_new)
            pv = jnp.dot(v1, jnp.exp2(s - m_new).astype(BF16), preferred_element_type=F32)
            acc_ref[:, hl] = alpha * acc_ref[:, hl] + pv[0:DSA_HEAD_DIM, :]
            l_ref[0:1, hl] = alpha * l_all[:, hl] + pv[DSA_HEAD_DIM:DSA_HEAD_DIM + 1, :]

    bi = pl.program_id(0)

    def attend(kt, cur_ref, next_ref, last_pair, last_tile):
        far = (qmin_ref[bi, qi] - kmax_ref[bi, kt]) >= T5_FAR
        fast = jnp.logical_and(far, jnp.logical_not(any_split))
        plain = jnp.logical_and(jnp.logical_not(far), jnp.logical_not(any_split))

        @pl.when(fast)
        def _():
            attend_tile(kt, cur_ref, next_ref, general_bias=False, tie_split=False)

        @pl.when(plain)
        def _():
            attend_tile(kt, cur_ref, next_ref, general_bias=True, tie_split=False)

        @pl.when(any_split)
        def _():
            attend_tile(kt, cur_ref, next_ref, general_bias=True, tie_split=True)

    tile_pairs(attend, peel_last=False)

    for h in range(DSA_HEADS):
        hl = head_lanes(h)
        ot_ref[h * DSA_HEAD_DIM:(h + 1) * DSA_HEAD_DIM, :] = acc_ref[:, hl] / l_ref[0:1, hl]
    o_ref[0] = (ot_ref[...].T * gate_ref[0]).astype(o_ref.dtype)


def _dsa(rel_bias, qcw, qiw, kc, ki, v1, wi, pos_col, pos_row, gate, tq):
    b, s, _ = kc.shape
    n_sel = min(TOPK_MAX, s // 4)
    vw_rows = v1.shape[1]
    hw = DSA_HEADS * tq
    tbl = rel_bias.astype(F32) * LOG2E
    tbl = tbl - tbl[REL_BUCKETS - 1:REL_BUCKETS, :]
    bits = lax.bitcast_convert_type(tbl.T.astype(jnp.bfloat16), jnp.uint16).astype(jnp.uint32)
    packed = lax.bitcast_convert_type((bits[0::2] << 16) | bits[1::2], I32)
    tbl_t = jnp.pad(packed, ((0, SUBLANES - DSA_HEADS // 2), (0, LANES - REL_BUCKETS)))
    q_min = jnp.min(pos_row.reshape(b, s // tq, tq), axis=-1)
    k_max = jnp.max(pos_row.reshape(b, s // DSA_KEY_TILE, DSA_KEY_TILE), axis=-1)
    tk = DSA_KEY_TILE
    vmem = (2 * (2 * _nbytes((DSA_HEAD_DIM, hw), BF16) + 2 * _nbytes((s, LANES), BF16) + _nbytes((vw_rows, s), F32)
                 + _nbytes((vw_rows, tq), F32) + _nbytes((s, LANES), I32) + _nbytes((tq, DSA_WIDTH), F32)
                 + _nbytes((tq, DSA_WIDTH), BF16)) + _nbytes((s, tq), I32) + 2 * _nbytes((tk, hw), F32)
            + _nbytes((tk, hw), BF16) + _nbytes((DSA_HEAD_DIM, hw), F32) + _nbytes((DSA_WIDTH, tq), F32)
            + 32 * _nbytes((tk, tq), F32))
    return pl.pallas_call(
        functools.partial(_dsa_kernel, n_sel=n_sel, tq=tq),
        grid=(b, s // tq),
        in_specs=[pl.BlockSpec(memory_space=pltpu.SMEM), pl.BlockSpec(memory_space=pltpu.SMEM),
                  pl.BlockSpec((SUBLANES, LANES), lambda bi, i: (0, 0)),
                  pl.BlockSpec((1, 1, DSA_HEAD_DIM, hw), lambda bi, i: (bi, i, 0, 0)),
                  pl.BlockSpec((1, 1, IDX_DIM, hw), lambda bi, i: (bi, i, 0, 0)),
                  pl.BlockSpec((1, s, DSA_HEAD_DIM), lambda bi, i: (bi, 0, 0)),
                  pl.BlockSpec((1, s, IDX_DIM), lambda bi, i: (bi, 0, 0)),
                  pl.BlockSpec((1, vw_rows, s), lambda bi, i: (bi, 0, 0)),
                  pl.BlockSpec((1, IDX_HEADS, tq), lambda bi, i: (bi, 0, i)),
                  pl.BlockSpec((1, s, 1), lambda bi, i: (bi, 0, 0)),
                  pl.BlockSpec((1, 1, tq), lambda bi, i: (bi, 0, i)),
                  pl.BlockSpec((1, tq, DSA_WIDTH), lambda bi, i: (bi, i, 0))],
        out_specs=pl.BlockSpec((1, tq, DSA_WIDTH), lambda bi, i: (bi, i, 0)),
        out_shape=jax.ShapeDtypeStruct((b, s, DSA_WIDTH), BF16),
        scratch_shapes=[pltpu.VMEM((s, tq), I32),
                        pltpu.VMEM((s, tq), I16), pltpu.VMEM((s, tq), I16),
                        pltpu.VMEM((tk, hw), F32), pltpu.VMEM((tk, hw), F32),
                        pltpu.VMEM((tk, hw), F32),
                        pltpu.VMEM((DSA_HEAD_DIM, hw), F32),
                        pltpu.VMEM((DSA_WIDTH, tq), F32),
                        pltpu.VMEM((SUBLANES, hw), F32), pltpu.VMEM((SUBLANES, hw), F32),
                        pltpu.VMEM((SUBLANES, tq), I32)],
        compiler_params=_compiler_params(("parallel", "arbitrary"), vmem),
        name="dsa",
    )(q_min, k_max, tbl_t, qcw, qiw, kc, ki, v1, wi, pos_col, pos_row, gate)


def _merge_kernel(oa_ref, ob_ref, oc_ref, g_ref, x_ref, p_ref, wa_ref, wb_ref, wc_ref, wo_ref, wpg_ref, wple_ref,
                  gn_ref, xo_ref, ho_ref):
    d = D_MODEL
    ya = jnp.dot(oa_ref[0], wa_ref[0], preferred_element_type=F32)
    yb = jnp.dot(ob_ref[0], wb_ref[0], preferred_element_type=F32)
    yc = jnp.dot(oc_ref[0], wc_ref[0], preferred_element_type=F32)
    merged = g_ref[0, :, 0:d] * ya + g_ref[0, :, d:2 * d] * yb + g_ref[0, :, 2 * d:3 * d] * yc
    x1 = x_ref[0] + jnp.dot(merged.astype(BF16), wo_ref[0], preferred_element_type=F32)
    ple_gate = _sigmoid(jnp.dot(x1.astype(BF16), wpg_ref[0], preferred_element_type=F32))
    x2 = x1 + ple_gate * jnp.dot(p_ref[0, 0].astype(BF16), wple_ref[0], preferred_element_type=F32)
    xo_ref[0] = x2
    ho_ref[0] = _rms(x2, gn_ref[...]).astype(ho_ref.dtype)


def _merge(oa, ob, oc, gates, x, p, layer, wa, wb, wc, wo, wpg, wple, g_next, h_dtype, tm=512):
    b, s, d = x.shape
    full = lambda a: pl.BlockSpec((1,) + a.shape[1:], lambda bi, i: (layer, 0, 0), pipeline_mode=pl.Buffered(1))
    row = lambda a: pl.BlockSpec((1, tm, a.shape[2]), lambda bi, i: (bi, i, 0))
    weights = (wa, wb, wc, wo, wpg, wple)
    acts = (oa, ob, oc, gates, x)
    vmem = (sum(_nbytes(a.shape[1:], a.dtype) for a in weights)
            + 2 * (sum(_nbytes((tm, a.shape[2]), a.dtype) for a in acts) + _nbytes((tm, p.shape[3]), p.dtype)
                   + 2 * _nbytes((tm, d), F32)) + 8 * _nbytes((tm, d), F32))
    return pl.pallas_call(
        _merge_kernel,
        grid=(b, s // tm),
        in_specs=[row(a) for a in acts] + [pl.BlockSpec((1, 1, tm, p.shape[3]), lambda bi, i: (layer, bi, i, 0))]
        + [full(a) for a in weights] + [pl.BlockSpec((1, d), lambda bi, i: (0, 0))],
        out_specs=[pl.BlockSpec((1, tm, d), lambda bi, i: (bi, i, 0)), pl.BlockSpec((1, tm, d), lambda bi, i: (bi, i, 0))],
        out_shape=[jax.ShapeDtypeStruct((b, s, d), F32), jax.ShapeDtypeStruct((b, s, d), h_dtype)],
        compiler_params=_compiler_params(("parallel", "parallel"), vmem),
        name="merge",
    )(*acts, p, *weights, g_next.reshape(1, d))


def _rotate_half_cols(w):
    half = w.shape[-1] // 2
    return jnp.concatenate([-w[..., half:], w[..., :half]], axis=-1)


def _prep_weights(w_in, w_uq, w_ukv):
    depth = w_in.shape[0]
    seg = lambda name: w_in[:, :, SPLIT[name][0]:SPLIT[name][1]]
    z = lambda *shape: jnp.zeros((depth,) + shape, w_in.dtype)
    d = D_MODEL
    kr = seg("k_rope")
    rope_lo = MLA_NOPE
    rope_pad = MLA_QK_PAD - MLA_NOPE - MLA_ROPE
    w = {}
    w["mla_in"] = jnp.concatenate(
        [seg("c_q"), seg("c_kv"), z(d, rope_lo), kr, z(d, rope_pad), z(d, rope_lo), _rotate_half_cols(kr), z(d, rope_pad)], axis=-1)
    w["gate_a"] = seg("gate_a")
    w["z"] = seg("z")
    w["xbc"] = seg("xbc")
    w["small"] = jnp.concatenate([seg("dt"), z(d, LANES - SSM_HEADS)], axis=-1)
    w["k_c"] = seg("k_c")
    w["k_idx"] = seg("k_idx")
    w["gate_c"] = seg("gate_c")
    w["merge"] = seg("merge")
    w["q_c_t"] = jnp.swapaxes(seg("q_c") * (DSA_HEAD_DIM ** -0.5 * LOG2E), 1, 2)
    w["q_idx_t"] = jnp.swapaxes(seg("q_idx"), 1, 2)
    w["vw_t"] = jnp.swapaxes(jnp.concatenate([seg("v_c"), seg("w_idx")], axis=-1), 1, 2)
    w = {
        "proj": jnp.concatenate([w["mla_in"]] + [w[seg[0]] for seg in PROJ_SEGMENTS], axis=-1).astype(BF16),
        "proj_t": jnp.concatenate([w["q_c_t"], w["q_idx_t"], w["vw_t"]], axis=1).astype(BF16),
    }

    uq =w_uq.reshape(depth, MLA_Q_LORA, MLA_HEADS, MLA_NOPE + MLA_ROPE)
    nope, rope = uq[..., :MLA_NOPE], uq[..., MLA_NOPE:]
    zq = lambda width: jnp.zeros((depth, MLA_Q_LORA, MLA_HEADS, width), w_uq.dtype)
    hq = MLA_HEADS * MLA_QK_PAD
    uq1 = jnp.concatenate([nope, rope, zq(rope_pad)], axis=-1).reshape(depth, MLA_Q_LORA, hq)
    uq2 = _rotate_half_cols(rope).reshape(depth, MLA_Q_LORA, MLA_HEADS * MLA_ROPE)
    w["uq1_t"] = jnp.swapaxes(uq1, 1, 2).astype(BF16)
    w["uq2_t"] = jnp.swapaxes(uq2, 1, 2).astype(BF16)
    ukv = w_ukv.reshape(depth, MLA_KV_LORA, MLA_HEADS, MLA_NOPE + MLA_V)
    zk = jnp.zeros((depth, MLA_KV_LORA, MLA_HEADS, MLA_QK_PAD - MLA_NOPE), w_ukv.dtype)
    w["uk"] = jnp.concatenate([ukv[..., :MLA_NOPE], zk], axis=-1).reshape(depth, MLA_KV_LORA, hq).astype(BF16)
    w["uv_t"] = jnp.swapaxes(ukv[..., MLA_NOPE:].reshape(depth, MLA_KV_LORA, MLA_WIDTH), 1, 2).astype(BF16)
    return w


def _rope_multipliers(positions):
    b, s = positions.shape
    inv_freq = 1.0 / (ROPE_THETA ** (jnp.arange(0, MLA_ROPE, 2, dtype=F32) / MLA_ROPE))
    ang = positions.astype(F32)[..., None] * inv_freq
    cos, sin = jnp.cos(ang), jnp.sin(ang)
    pad = MLA_QK_PAD - MLA_NOPE - MLA_ROPE
    m1 = jnp.concatenate([jnp.ones((b, s, MLA_NOPE), F32), cos, cos, jnp.zeros((b, s, pad), F32)], axis=-1)
    m2 = jnp.concatenate([jnp.zeros((b, s, MLA_NOPE), F32), sin, sin, jnp.zeros((b, s, pad), F32)], axis=-1)
    return m1, m2, jnp.swapaxes(m1, 1, 2), jnp.swapaxes(m2, 1, 2)


def kernel(x, p, positions, norm_g, w_in, mla_q_norm, w_uq, mla_kv_norm, w_ukv, conv_w, conv_b, dt_bias, a_log, d_skip, ssm_norm, w_br_a, w_br_b, w_br_c, w_out, rel_bias, w_ple, w_ple_gate, final_norm):
    b, s, d = x.shape
    depth = w_in.shape[0]
    w = _prep_weights(w_in, w_uq, w_ukv)
    m1, m2, m1t, m2t = _rope_multipliers(positions)
    pos_col = positions.astype(I32).reshape(b, s, 1)
    pos_row = positions.astype(I32).reshape(b, 1, s)
    wbr_a, wbr_b, wbr_c = w_br_a.astype(BF16), w_br_b.astype(BF16), w_br_c.astype(BF16)
    wo, wpg, wple = w_out.astype(BF16), w_ple_gate.astype(BF16), w_ple.astype(BF16)

    q_norm = mla_q_norm.astype(F32).reshape(depth, 1, -1)
    kv_norm = mla_kv_norm.astype(F32).reshape(depth, 1, -1)

    h = _norm(x, norm_g[0], BF16)
    for i in range(depth):
        pr = _proj_all(h, i, w["proj"], w["proj_t"], (m1, m2, m1t, m2t), q_norm, kv_norm,
                       w["uq1_t"], w["uq2_t"], w["uk"], w["uv_t"], DSA_Q_BLOCK)
        o_a = _mla_attn(pr["mla_q_t"], pr["mla_k"], pr["mla_v_t1"], pr["gate_a"])
        o_b = _ssd(pr["xbc"], pr["z"], pr["small"], conv_w[i], conv_b[i], dt_bias[i], a_log[i], d_skip[i], ssm_norm[i])
        o_c = _dsa(rel_bias, pr["q_c_wide"], pr["q_idx_wide"], pr["k_c"], pr["k_idx"], pr["v_t1"], pr["w_idx_t"], pos_col, pos_row,
                   pr["gate_c"], DSA_Q_BLOCK)

        last = i == depth - 1
        g_next = final_norm if last else norm_g[i + 1]
        x, h = _merge(o_a, o_b, o_c, pr["merge"], x, p, i, wbr_a, wbr_b, wbr_c, wo, wpg, wple, g_next,
                      F32 if last else BF16)
    return h
```

```python
import functools
import math

import jax
import jax.numpy as jnp
from jax import lax
from jax.experimental import pallas as pl
from jax.experimental.pallas import tpu as pltpu

F32 = jnp.float32
BF16 = jnp.bfloat16
I32 = jnp.int32
I16 = jnp.int16
HALF16 = 1 << 15

D_MODEL = 1024
NORM_EPS = 1e-6

MLA_HEADS = 8
MLA_NOPE = 64
MLA_ROPE = 32
MLA_V = 64
MLA_Q_LORA = 384
MLA_KV_LORA = 256
MLA_WIDTH = MLA_HEADS * MLA_V
ROPE_THETA = 10000.0
MLA_QK_PAD = 128
LOG2E = math.log2(math.e)
MLA_Q_SCALE = (MLA_NOPE + MLA_ROPE) ** -0.5 * LOG2E

SSM_HEADS = 16
SSM_HEAD_DIM = 64
SSM_INNER = SSM_HEADS * SSM_HEAD_DIM
SSM_GROUPS = 2
SSM_STATE = 128
SSM_CONV = 4
SSM_CHUNK = 128
SSM_CONV_DIM = SSM_INNER + 2 * SSM_GROUPS * SSM_STATE
SSM_HEADS_PER_GROUP = SSM_HEADS // SSM_GROUPS
ONES_ROWS = 16
MLA_V1_ROWS = MLA_V + ONES_ROWS
SSM_CONV_TAIL = 16

DSA_HEADS = 8
DSA_HEAD_DIM = 64
DSA_WIDTH = DSA_HEADS * DSA_HEAD_DIM
IDX_HEADS = 8
IDX_DIM = 64
TOPK_MAX = 256
DSA_KEY_TILE = 128
DSA_Q_BLOCK = 256

REL_BUCKETS = 32
REL_MAX_DIST = 128
N_BRANCHES = 3

LANES = 128
SUBLANES = 8
VMEM_LIMIT_CAP = 56 * 1024 * 1024
VMEM_LIMIT_FLOOR = 32 * 1024 * 1024
INT_MIN = -(2 ** 31)
NEG_BIG = -1e30

SPLIT_SIZES = (
    MLA_Q_LORA, MLA_KV_LORA, MLA_ROPE, MLA_WIDTH, SSM_INNER, SSM_CONV_DIM, SSM_HEADS, DSA_WIDTH,
    DSA_HEAD_DIM, DSA_HEAD_DIM, IDX_HEADS * IDX_DIM, IDX_DIM, IDX_HEADS, DSA_WIDTH,
    N_BRANCHES * D_MODEL,
)
SPLIT_NAMES = ("c_q", "c_kv", "k_rope", "gate_a", "z", "xbc", "dt", "q_c", "k_c", "v_c", "q_idx",
               "k_idx", "w_idx", "gate_c", "merge")


def _split_bounds():
    out, off = {}, 0
    for name, size in zip(SPLIT_NAMES, SPLIT_SIZES, strict=True):
        out[name] = (off, off + size)
        off += size
    return out


SPLIT = _split_bounds()


def _t5_large_thresholds():
    exact = REL_BUCKETS // 2
    thr = []
    for j in range(1, REL_BUCKETS - exact):
        thr.append(int(math.ceil(exact * (REL_MAX_DIST / exact) ** (j / (REL_BUCKETS - exact)) - 1e-9)))
    return tuple(thr)


T5_EXACT = REL_BUCKETS // 2
T5_LARGE_THR = _t5_large_thresholds()
T5_FAR = T5_LARGE_THR[-1]


def _compiler_params(semantics, vmem_bytes):
    limit = int(min(VMEM_LIMIT_CAP, max(VMEM_LIMIT_FLOOR, vmem_bytes)))
    return pltpu.CompilerParams(dimension_semantics=semantics, vmem_limit_bytes=limit)


def _nbytes(shape, dtype):
    return math.prod(shape) * jnp.dtype(dtype).itemsize


def _sigmoid(x):
    return 1.0 / (1.0 + jnp.exp(-x))


def _rms(x, g):
    return x * lax.rsqrt(jnp.mean(x * x, axis=-1, keepdims=True) + NORM_EPS) * g


def _norm_kernel(x_ref, g_ref, o_ref):
    o_ref[0] = _rms(x_ref[0], g_ref[...]).astype(o_ref.dtype)


def _norm(x, g, out_dtype, tm=512):
    b, s, d = x.shape
    return pl.pallas_call(
        _norm_kernel,
        grid=(b, s // tm),
        in_specs=[pl.BlockSpec((1, tm, d), lambda bi, i: (bi, i, 0)), pl.BlockSpec((1, d), lambda bi, i: (0, 0))],
        out_specs=pl.BlockSpec((1, tm, d), lambda bi, i: (bi, i, 0)),
        out_shape=jax.ShapeDtypeStruct((b, s, d), out_dtype),
        compiler_params=_compiler_params(("parallel", "parallel"), 4 * _nbytes((tm, d), F32)),
        name="rms_norm",
    )(x, g.reshape(1, d))


MLA_IN_WIDTH = MLA_Q_LORA + MLA_KV_LORA + 2 * LANES
PROJ_SEGMENTS = (
    ("small", LANES, None, F32),
    ("gate_a", MLA_WIDTH, "silu", BF16),
    ("gate_c", DSA_WIDTH, "silu", BF16),
    ("z", SSM_INNER, "silu", BF16),
    ("xbc", SSM_CONV_DIM, None, BF16),
    ("merge", N_BRANCHES * D_MODEL, "sigmoid", BF16),
    ("k_c", DSA_HEAD_DIM, None, BF16),
    ("k_idx", IDX_DIM, None, BF16),
)
PROJ_CHUNK = 512
PROJ_T_ROWS = (DSA_WIDTH, IDX_HEADS * IDX_DIM, DSA_HEAD_DIM + IDX_HEADS)


def _proj_all_kernel(h_ref, w_ref, wt_ref, m1_ref, m2_ref, m1t_ref, m2t_ref, qn_ref, kvn_ref, wq1t_ref, wq2t_ref, wk_ref,
                     wvt_ref, *out_refs):
    h = h_ref[0]
    n_seg = len(PROJ_SEGMENTS)
    a = jnp.dot(h, w_ref[0, :, 0:MLA_IN_WIDTH], preferred_element_type=F32)
    _mla_prep(a, m1_ref, m2_ref, m1t_ref, m2t_ref, qn_ref, kvn_ref, wq1t_ref, wq2t_ref, wk_ref, wvt_ref, *out_refs[n_seg + 4:])
    off = MLA_IN_WIDTH
    for (_, width, act, _), o_ref in zip(PROJ_SEGMENTS, out_refs[:n_seg], strict=True):
        for c in range(0, width, PROJ_CHUNK):
            cw = min(PROJ_CHUNK, width - c)
            y = jnp.dot(h, w_ref[0, :, off + c:off + c + cw], preferred_element_type=F32)
            if act == "silu":
                y = y * _sigmoid(y)
            elif act == "sigmoid":
                y = _sigmoid(y)
            o_ref[0, :, c:c + cw] = y.astype(o_ref.dtype)
        off += width
    qcw_ref, qiw_ref, v1_ref, wi_ref = out_refs[n_seg:n_seg + 4]
    nt = (((1,), (1,)), ((), ()))
    tq = h.shape[0]
    row = 0
    for o_ref, heads, dim in ((qcw_ref, DSA_HEADS, DSA_HEAD_DIM), (qiw_ref, IDX_HEADS, IDX_DIM)):
        y = lax.dot_general(wt_ref[0, row:row + heads * dim, :], h, nt, preferred_element_type=F32)
        for hd in range(heads):
            o_ref[0, 0, :, hd * tq:(hd + 1) * tq] = y[hd * dim:(hd + 1) * dim, :].astype(o_ref.dtype)
        row += heads * dim
    vw = lax.dot_general(wt_ref[0, row:row + PROJ_T_ROWS[2], :], h, nt, preferred_element_type=F32)
    v1_ref[0, 0:DSA_HEAD_DIM, :] = vw[0:DSA_HEAD_DIM, :].astype(v1_ref.dtype)
    v1_ref[0, DSA_HEAD_DIM:, :] = jnp.ones((ONES_ROWS, tq), v1_ref.dtype)
    wi_ref[0] = vw[DSA_HEAD_DIM:, :]


def _proj_all(h3d, layer, w, wt, rope, qn, kvn, wq1t, wq2t, wk, wvt, tq):
    b, s, k = h3d.shape
    n_all = MLA_IN_WIDTH + sum(seg[1] for seg in PROJ_SEGMENTS)
    hq = MLA_HEADS * MLA_QK_PAD
    v1_rows = DSA_HEAD_DIM + ONES_ROWS
    out_shapes = [jax.ShapeDtypeStruct((b, s, width), dt) for _, width, _, dt in PROJ_SEGMENTS]
    out_specs = [pl.BlockSpec((1, tq, width), lambda bi, i: (bi, i, 0)) for _, width, _, _ in PROJ_SEGMENTS]
    for heads, dim in ((DSA_HEADS, DSA_HEAD_DIM), (IDX_HEADS, IDX_DIM)):
        out_shapes.append(jax.ShapeDtypeStruct((b, s // tq, dim, heads * tq), BF16))
        out_specs.append(pl.BlockSpec((1, 1, dim, heads * tq), lambda bi, i: (bi, i, 0, 0)))
    for rows, dt in ((v1_rows, BF16), (IDX_HEADS, F32)):
        out_shapes.append(jax.ShapeDtypeStruct((b, rows, s), dt))
        out_specs.append(pl.BlockSpec((1, rows, tq), lambda bi, i: (bi, 0, i)))
    for rows in (hq, None, MLA_HEADS * MLA_V1_ROWS):
        if rows is None:
            out_shapes.append(jax.ShapeDtypeStruct((b, s, hq), BF16))
            out_specs.append(pl.BlockSpec((1, tq, hq), lambda bi, i: (bi, i, 0)))
        else:
            out_shapes.append(jax.ShapeDtypeStruct((b, rows, s), BF16))
            out_specs.append(pl.BlockSpec((1, rows, tq), lambda bi, i: (bi, 0, i)))
    resident = pl.Buffered(1)
    weights = (w, wt, qn, kvn, wq1t, wq2t, wk, wvt)
    layer_spec = lambda a: pl.BlockSpec((1,) + a.shape[1:], lambda bi, i: (layer,) + (0,) * (a.ndim - 1), pipeline_mode=resident)
    m1, m2, m1t, m2t = rope
    vmem = (sum(_nbytes(a.shape[1:], a.dtype) for a in weights) + 2 * _nbytes((tq, k), BF16) + 8 * _nbytes((tq, LANES), F32)
            + 2 * sum(_nbytes((tq, width), dt) for _, width, _, dt in PROJ_SEGMENTS)
            + 2 * (2 * _nbytes((DSA_HEAD_DIM, DSA_HEADS * tq), BF16) + _nbytes((v1_rows + IDX_HEADS, tq), F32))
            + 2 * (2 * _nbytes((tq, hq), BF16) + _nbytes((MLA_HEADS * MLA_V1_ROWS, tq), BF16))
            + 8 * _nbytes((tq, PROJ_CHUNK), F32) + 6 * _nbytes((tq, hq), F32))
    outs = pl.pallas_call(
        _proj_all_kernel,
        grid=(b, s // tq),
        in_specs=[pl.BlockSpec((1, tq, k), lambda bi, i: (bi, i, 0)), layer_spec(w), layer_spec(wt),
                  pl.BlockSpec((1, tq, LANES), lambda bi, i: (bi, i, 0)), pl.BlockSpec((1, tq, LANES), lambda bi, i: (bi, i, 0)),
                  pl.BlockSpec((1, LANES, tq), lambda bi, i: (bi, 0, i)), pl.BlockSpec((1, LANES, tq), lambda bi, i: (bi, 0, i))]
        + [layer_spec(a) for a in weights[2:]],
        out_specs=out_specs,
        out_shape=out_shapes,
        compiler_params=_compiler_params(("parallel", "parallel"), vmem),
        name="proj_all",
    )(h3d, w, wt, m1, m2, m1t, m2t, *weights[2:])
    names = [seg[0] for seg in PROJ_SEGMENTS] + ["q_c_wide", "q_idx_wide", "v_t1", "w_idx_t", "mla_q_t", "mla_k", "mla_v_t1"]
    return dict(zip(names, outs, strict=True))


def _mla_prep(a, m1_ref, m2_ref, m1t_ref, m2t_ref, qn_ref, kvn_ref, wq1t_ref, wq2t_ref, wk_ref, wvt_ref, qt_ref, k_ref, vt_ref):
    c_q = a[:, :MLA_Q_LORA]
    c_kv = a[:, MLA_Q_LORA:MLA_Q_LORA + MLA_KV_LORA]
    kr1 = a[:, MLA_Q_LORA + MLA_KV_LORA:MLA_Q_LORA + MLA_KV_LORA + LANES]
    kr2 = a[:, MLA_Q_LORA + MLA_KV_LORA + LANES:]
    cqn = _rms(c_q, qn_ref[0]).astype(BF16)
    ckvn = _rms(c_kv, kvn_ref[0]).astype(BF16)
    nt = (((1,), (1,)), ((), ()))
    qa_t = lax.dot_general(wq1t_ref[0], cqn, nt, preferred_element_type=F32)
    qb_t = lax.dot_general(wq2t_ref[0], cqn, nt, preferred_element_type=F32)
    kn = jnp.dot(ckvn, wk_ref[0], preferred_element_type=F32)
    kr = kr1 * m1_ref[0] + kr2 * m2_ref[0]
    rope = slice(MLA_NOPE, MLA_NOPE + MLA_ROPE)
    cos_t = m1t_ref[0, rope, :] * MLA_Q_SCALE
    sin_t = m2t_ref[0, rope, :] * MLA_Q_SCALE
    pad_rows = jnp.zeros((MLA_QK_PAD - MLA_NOPE - MLA_ROPE, qa_t.shape[1]), qt_ref.dtype)
    for h in range(MLA_HEADS):
        base = h * MLA_QK_PAD
        sl = slice(base, base + MLA_QK_PAD)
        qt_ref[0, base:base + MLA_NOPE, :] = (qa_t[base:base + MLA_NOPE, :] * MLA_Q_SCALE).astype(qt_ref.dtype)
        qt_ref[0, base + MLA_NOPE:base + MLA_NOPE + MLA_ROPE, :] = (
            qa_t[base + MLA_NOPE:base + MLA_NOPE + MLA_ROPE, :] * cos_t
            + qb_t[h * MLA_ROPE:(h + 1) * MLA_ROPE, :] * sin_t).astype(qt_ref.dtype)
        qt_ref[0, base + MLA_NOPE + MLA_ROPE:base + MLA_QK_PAD, :] = pad_rows
        k_ref[0, :, sl] = (kn[:, sl] + kr).astype(k_ref.dtype)
    v_t = lax.dot_general(wvt_ref[0], ckvn, nt, preferred_element_type=F32)
    ones = jnp.ones((ONES_ROWS, v_t.shape[1]), vt_ref.dtype)
    for h in range(MLA_HEADS):
        base = h * MLA_V1_ROWS
        vt_ref[0, base:base + MLA_V, :] = v_t[h * MLA_V:(h + 1) * MLA_V, :].astype(vt_ref.dtype)
        vt_ref[0, base + MLA_V:base + MLA_V1_ROWS, :] = ones


def _mla_attn_kernel(qt_ref, k_ref, vt_ref, g_ref, o_ref, m_ref, l_ref, acc_ref, s0_ref, s1_ref, *, tq, tk):
    assert tq == 2 * tk
    qi = pl.program_id(1)
    krow = lax.broadcasted_iota(I32, (tk, tq), 0)
    qcol = lax.broadcasted_iota(I32, (tk, tq), 1)
    tiles_per_q = tq // tk
    m_ref[...] = jnp.full(m_ref.shape, -jnp.inf, F32)
    l_ref[...] = jnp.zeros(l_ref.shape, F32)
    acc_ref[...] = jnp.zeros(acc_ref.shape, F32)

    staged = (s0_ref, s1_ref)
    n_tiles = (qi + 1) * tiles_per_q

    def head_lanes(h):
        return slice(h * tq, (h + 1) * tq)

    def stage(j, dst_ref):
        ks = pl.multiple_of(j * tk, tk)
        for h in range(MLA_HEADS):
            k = k_ref[0, pl.ds(ks, tk), h * MLA_QK_PAD:(h + 1) * MLA_QK_PAD]
            dst_ref[:, head_lanes(h)] = jnp.dot(k, qt_ref[0, h * MLA_QK_PAD:(h + 1) * MLA_QK_PAD, :],
                                                preferred_element_type=F32)

    def tile(j, cur_ref, next_ref, diagonal):
        ks = pl.multiple_of(j * tk, tk)
        stage(jnp.minimum(j + 1, n_tiles - 1), next_ref)
        m_all = m_ref[...]
        l_all = l_ref[...]
        if diagonal:
            causal = ks + krow <= qi * tq + qcol
        for h in range(MLA_HEADS):
            s = cur_ref[:, head_lanes(h)]
            if diagonal:
                s = jnp.where(causal, s, -jnp.inf)
            m_old = m_all[h:h + 1, :]
            m_new = jnp.maximum(m_old, jnp.max(s, axis=0, keepdims=True))
            alpha = jnp.exp2(m_old - m_new)
            p = jnp.exp2(s - m_new).astype(BF16)
            m_ref[h:h + 1, :] = m_new
            pv = jnp.dot(vt_ref[0, h * MLA_V1_ROWS:(h + 1) * MLA_V1_ROWS, pl.ds(ks, tk)], p, preferred_element_type=F32)
            hs = slice(h * MLA_V, (h + 1) * MLA_V)
            acc_ref[hs, :] = alpha * acc_ref[hs, :] + pv[0:MLA_V, :]
            l_ref[h:h + 1, :] = alpha * l_all[h:h + 1, :] + pv[MLA_V:MLA_V + 1, :]

    stage(0, staged[0])

    def full_pair(pi, carry):
        for half in range(2):
            tile(2 * pi + half, staged[half], staged[1 - half], diagonal=False)
        return carry

    lax.fori_loop(0, qi, full_pair, 0)
    for half in range(2):
        tile(2 * qi + half, staged[half], staged[1 - half], diagonal=True)
    for h in range(MLA_HEADS):
        hs = slice(h * MLA_V, (h + 1) * MLA_V)
        acc_ref[hs, :] = acc_ref[hs, :] / l_ref[h:h + 1, :]
    o_ref[0] = (acc_ref[...].T * g_ref[0]).astype(o_ref.dtype)


def _mla_attn(qt, k, vt, gate, tq=256, tk=128):
    b, s, hq = k.shape
    vmem = 2 * (_nbytes((hq, tq), BF16) + _nbytes((s, hq), BF16) + _nbytes((MLA_WIDTH, s), BF16)
                + _nbytes((tq, MLA_WIDTH), F32) + _nbytes((tq, MLA_WIDTH), BF16)) + _nbytes((MLA_WIDTH, tq), F32) + 16 * _nbytes((tk, tq), F32)
    return pl.pallas_call(
        functools.partial(_mla_attn_kernel, tq=tq, tk=tk),
        grid=(b, s // tq),
        in_specs=[pl.BlockSpec((1, hq, tq), lambda bi, i: (bi, 0, i)),
                  pl.BlockSpec((1, s, hq), lambda bi, i: (bi, 0, 0)),
                  pl.BlockSpec((1, MLA_HEADS * MLA_V1_ROWS, s), lambda bi, i: (bi, 0, 0)),
                  pl.BlockSpec((1, tq, MLA_WIDTH), lambda bi, i: (bi, i, 0))],
        out_specs=pl.BlockSpec((1, tq, MLA_WIDTH), lambda bi, i: (bi, i, 0)),
        out_shape=jax.ShapeDtypeStruct((b, s, MLA_WIDTH), BF16),
        scratch_shapes=[pltpu.VMEM((MLA_HEADS, tq), F32), pltpu.VMEM((MLA_HEADS, tq), F32),
                        pltpu.VMEM((MLA_WIDTH, tq), F32),
                        pltpu.VMEM((tk, MLA_HEADS * tq), F32), pltpu.VMEM((tk, MLA_HEADS * tq), F32)],
        compiler_params=_compiler_params(("parallel", "arbitrary"), vmem),
        name="mla_attn",
    )(qt, k, vt, gate)


def _ssd_kernel(xbc_ref, xprev_ref, zs_ref, sm_ref, cw_ref, cb_ref, dtb_ref, alog_ref, dskf_ref, nrm_ref, exp16_ref, exp32_ref,
                o_ref, state_ref, y_ref):
    c = pl.program_id(1)
    q = SSM_CHUNK

    @pl.when(c == 0)
    def _():
        state_ref[...] = jnp.zeros(state_ref.shape, F32)

    x_in = xbc_ref[0]
    tail = xprev_ref.shape[1]
    prev = jnp.where(c > 0, xprev_ref[0].astype(F32), 0.0).astype(x_in.dtype)
    win = jnp.concatenate([prev, x_in], axis=0)
    x = x_in.astype(F32)
    wrow = lax.broadcasted_iota(I32, (q, tail + q), 0)
    wcol = lax.broadcasted_iota(I32, (q, tail + q), 1)
    acc = cb_ref[...] + cw_ref[SSM_CONV - 1:SSM_CONV, :] * x
    for j in range(1, SSM_CONV):
        pick = jnp.where(wcol == wrow + (tail - j), 1.0, 0.0).astype(win.dtype)
        acc = acc + cw_ref[SSM_CONV - 1 - j:SSM_CONV - j, :] * jnp.dot(pick, win, preferred_element_type=F32)
    half = 0.5 * acc
    xc = half + half * jnp.tanh(half)

    xs = xc[:, :SSM_INNER]
    bm = xc[:, SSM_INNER:SSM_INNER + SSM_GROUPS * SSM_STATE]
    cm = xc[:, SSM_INNER + SSM_GROUPS * SSM_STATE:]

    pre = sm_ref[0] + dtb_ref[...]
    dt = jnp.maximum(pre, 0.0) + jnp.log1p(jnp.exp(-jnp.abs(pre)))
    a = -jnp.exp(alog_ref[...])
    row = lax.broadcasted_iota(I32, (q, q), 0)
    col = lax.broadcasted_iota(I32, (q, q), 1)
    lower = row >= col
    cum = jnp.dot(jnp.where(lower, 1.0, 0.0).astype(F32), dt * a, preferred_element_type=F32,
                  precision=lax.Precision.HIGHEST)
    cum_t = cum.T
    cum_last = cum[q - 1:q, :]
    per_head = jnp.concatenate([dt, jnp.exp(cum_last - cum), jnp.exp(cum)], axis=0).astype(BF16)
    spread = jnp.dot(per_head, exp16_ref[...], preferred_element_type=F32)
    dt_full, to_end_full, e_cum_full = spread[0:q], spread[q:2 * q], spread[2 * q:3 * q]
    chunk_decay_full = jnp.dot(jnp.broadcast_to(jnp.exp(cum_last), (SUBLANES, LANES)), exp32_ref[...],
                               preferred_element_type=F32, precision=lax.Precision.HIGHEST)[0:1]
    xdt = xs * dt_full
    xw = (xdt * to_end_full).astype(BF16)
    first_head = lax.broadcasted_iota(I32, (q, LANES), 1) < SSM_HEAD_DIM
    group_w = SSM_HEADS_PER_GROUP * SSM_HEAD_DIM
    heads_per_tile = LANES // SSM_HEAD_DIM

    for g in range(SSM_GROUPS):
        bg32 = bm[:, g * SSM_STATE:(g + 1) * SSM_STATE]
        bg = bg32.astype(BF16)
        cg = cm[:, g * SSM_STATE:(g + 1) * SSM_STATE].astype(BF16)
        cb = lax.dot_general(cg, bg, (((1,), (1,)), ((), ())), preferred_element_type=F32)
        gl = slice(g * group_w, (g + 1) * group_w)
        st = state_ref[g]
        y_off = jnp.dot(cg, st.astype(BF16), preferred_element_type=F32)
        state_ref[g] = st * chunk_decay_full[:, gl] + jnp.dot(bg32.T.astype(BF16), xw[:, gl], preferred_element_type=F32)
        for tile in range(group_w // LANES):
            h0 = g * SSM_HEADS_PER_GROUP + tile * heads_per_tile
            tl = slice(h0 * SSM_HEAD_DIM, h0 * SSM_HEAD_DIM + LANES)
            g_both = []
            for h in range(h0, h0 + heads_per_tile):
                diff = cum[:, h:h + 1] - cum_t[h:h + 1, :]
                decay_in = jnp.exp(jnp.where(lower, diff, -jnp.inf))
                g_both.append((cb * decay_in).astype(BF16))
            xdt_tile = xdt[:, tl]
            stacked = jnp.concatenate([jnp.where(first_head, xdt_tile, 0.0).astype(BF16),
                                       jnp.where(first_head, 0.0, xdt_tile).astype(BF16)], axis=0)
            y_diag = jnp.dot(jnp.concatenate(g_both, axis=1), stacked, preferred_element_type=F32)
            y = y_diag + y_off[:, tile * LANES:(tile + 1) * LANES] * e_cum_full[:, tl] + xs[:, tl] * dskf_ref[:, tl]
            y_ref[:, tl] = y * zs_ref[0, :, tl]

    o_ref[0] = _rms(y_ref[...], nrm_ref[...]).astype(o_ref.dtype)


def _ssd(xbc, zs, small, conv_w, conv_b, dt_bias, a_log, d_skip, ssm_norm):
    b, s, _ = xbc.shape
    q = SSM_CHUNK
    pad = lambda v: jnp.pad(v.astype(F32), (0, LANES - v.shape[0])).reshape(1, LANES)
    full = lambda shape: pl.BlockSpec(shape, lambda bi, c: (0,) * len(shape))
    blk = lambda width: pl.BlockSpec((1, q, width), lambda bi, c: (bi, c, 0))
    expand = (jnp.arange(LANES)[:, None] == jnp.arange(SSM_INNER)[None, :] // SSM_HEAD_DIM).astype(F32)
    d_skip_full = jnp.repeat(d_skip.astype(F32), SSM_HEAD_DIM).reshape(1, SSM_INNER)
    group_w = SSM_HEADS_PER_GROUP * SSM_HEAD_DIM
    vmem = (2 * (_nbytes((q, SSM_CONV_DIM), xbc.dtype) + _nbytes((q, SSM_INNER), zs.dtype) + _nbytes((q, LANES), F32)
                 + _nbytes((q, SSM_INNER), BF16) + _nbytes((LANES, SSM_INNER), BF16) + _nbytes((LANES, SSM_INNER), F32))
            + _nbytes((2 * q, SSM_CONV_DIM), xbc.dtype) + _nbytes((SSM_GROUPS, SSM_STATE, group_w), F32)
            + _nbytes((q, SSM_INNER), F32) + 10 * _nbytes((q, SSM_CONV_DIM), F32))
    return pl.pallas_call(
        _ssd_kernel,
        grid=(b, s // q),
        in_specs=[blk(SSM_CONV_DIM),
                  pl.BlockSpec((1, SSM_CONV_TAIL, SSM_CONV_DIM),
                               lambda bi, c: (bi, jnp.maximum(c * (q // SSM_CONV_TAIL) - 1, 0), 0)),
                  blk(SSM_INNER), blk(LANES), full((SSM_CONV, SSM_CONV_DIM)), full((1, SSM_CONV_DIM)),
                  full((1, LANES)), full((1, LANES)), full((1, SSM_INNER)), full((1, SSM_INNER)),
                  full((LANES, SSM_INNER)), full((LANES, SSM_INNER))],
        out_specs=blk(SSM_INNER),
        out_shape=jax.ShapeDtypeStruct((b, s, SSM_INNER), BF16),
        scratch_shapes=[pltpu.VMEM((SSM_GROUPS, SSM_STATE, group_w), F32),
                        pltpu.VMEM((q, SSM_INNER), F32)],
        compiler_params=_compiler_params(("parallel", "arbitrary"), vmem),
        name="ssd",
    )(xbc, xbc, zs, small, conv_w.astype(F32), conv_b.reshape(1, -1).astype(F32), pad(dt_bias), pad(a_log), d_skip_full,
      ssm_norm.reshape(1, -1).astype(F32), expand.astype(BF16), expand)


def _t5_bucket_of(n):
    steps_per_octave = (REL_BUCKETS - T5_EXACT) / math.log2(REL_MAX_DIST / T5_EXACT)
    x = jnp.maximum(n, T5_EXACT).astype(F32) * (1.0 / T5_EXACT)
    large = T5_EXACT + jnp.floor(jnp.log2(x) * steps_per_octave).astype(I32)
    return jnp.where(n <= T5_EXACT, n, jnp.minimum(large, REL_BUCKETS - 1))


def _dsa_kernel(qmin_ref, kmax_ref, consec_ref, tblt_ref, toe_ref, qcw_ref, qiw_ref, kc_ref, ki_ref, v1_ref, wi_ref, posk_ref, posq_ref, gate_ref, o_ref,
                key_ref, hi_ref, lo_ref, s0_ref, s1_ref, bias_ref, acc_ref, ot_ref, m_ref, l_ref, tie_ref, *, n_sel, tq):
    tk = DSA_KEY_TILE
    qi = pl.program_id(1)
    n_tiles = (qi + 1) * (tq // tk)
    krow = lax.broadcasted_iota(I32, (tk, tq), 0)
    qcol = lax.broadcasted_iota(I32, (tk, tq), 1)
    w = wi_ref[0]
    posq = posq_ref[0]

    def tile_start(kt):
        return pl.multiple_of(kt * tk, tk)

    def causal_of(ks):
        return (ks + krow) <= (qi * tq + qcol)

    def head_lanes(h):
        return slice(h * tq, (h + 1) * tq)

    staged = (s0_ref, s1_ref)

    def stage(k_ref, q_ref, kt, dst_ref):
        dst_ref[...] = jnp.dot(k_ref[0, pl.ds(tile_start(kt), tk), :], q_ref[0, 0], preferred_element_type=F32)

    def tile_pairs(tile_fn, peel_last):
        def pair(pi, last_pair):
            for half in range(2):
                tile_fn(2 * pi + half, staged[half], staged[1 - half], last_pair, last_pair and half == 1)

        def body(pi, carry):
            pair(pi, False)
            return carry

        if peel_last:
            lax.fori_loop(0, n_tiles // 2 - 1, body, 0)
            pair(n_tiles // 2 - 1, True)
        else:
            lax.fori_loop(0, n_tiles // 2, body, 0)

    stage(ki_ref, qiw_ref, 0, staged[0])

    def score_tile(kt, cur_ref, next_ref, last_pair, last_tile):
        ks = tile_start(kt)
        if last_tile:
            stage(kc_ref, qcw_ref, 0, next_ref)
        else:
            stage(ki_ref, qiw_ref, kt + 1, next_ref)
        sc = jnp.zeros((tk, tq), F32)
        for h in range(IDX_HEADS):
            sc = sc + w[h:h + 1, :] * jnp.maximum(cur_ref[:, head_lanes(h)], 0.0)
        sc = jnp.where(sc == 0.0, 0.0, sc)
        bits = lax.bitcast_convert_type(sc, I32)
        key = bits ^ ((bits >> 31) & 0x7FFFFFFF)
        if last_pair:
            key = jnp.where(causal_of(ks), key, INT_MIN)
        key_ref[pl.ds(ks, tk), :] = key
        hi_ref[pl.ds(ks, tk), :] = (key >> 16).astype(I16)
        lo_ref[pl.ds(ks, tk), :] = ((key & 0xFFFF) - HALF16).astype(I16)

    tile_pairs(score_tile, peel_last=True)

    def count16(half_ref, pred):
        def body(pi, cnt):
            for kt in (2 * pi, 2 * pi + 1):
                cnt = cnt + jnp.where(pred(half_ref[pl.ds(tile_start(kt), tk), :]), jnp.int16(1), jnp.int16(0))
            return cnt
        cnt = lax.fori_loop(0, n_tiles // 2, body, jnp.zeros((tk, tq), I16))
        rows = 2 * SUBLANES
        parts = [cnt[r:r + rows, :] for r in range(0, tk, rows)]
        while len(parts) > 1:
            parts = [a + b for a, b in zip(parts[0::2], parts[1::2], strict=True)]
        return jnp.sum(parts[0].astype(I32), axis=0, keepdims=True)

    def nth_largest16(half_ref, rank):
        def bit_step(i, cand):
            trial = cand | lax.shift_left(jnp.int32(1), 15 - i)
            thr = (trial - HALF16).astype(I16)
            tot = count16(half_ref, lambda v: v >= thr)
            return jnp.where(tot >= rank, trial, cand)
        return lax.fori_loop(0, 16, bit_step, jnp.zeros((1, tq), I32)) - HALF16

    tau_hi = nth_largest16(hi_ref, n_sel)
    tau_hi16 = tau_hi.astype(I16)
    above_hi = count16(hi_ref, lambda v: v > tau_hi16)

    def mask_low(kt, carry):
        rows = pl.ds(tile_start(kt), tk)
        lo_ref[rows, :] = jnp.where(hi_ref[rows, :] == tau_hi16, lo_ref[rows, :], jnp.int16(-HALF16))
        return carry

    lax.fori_loop(0, n_tiles, mask_low, 0)
    tau_lo = nth_largest16(lo_ref, n_sel - above_hi)
    tau_lo16 = tau_lo.astype(I16)
    tau = lax.shift_left(tau_hi, 16) | (tau_lo + HALF16)
    n_gt = above_hi + count16(lo_ref, lambda v: v > tau_lo16)
    n_ge = above_hi + count16(lo_ref, lambda v: v >= tau_lo16)
    need = n_sel - n_gt

    m_ref[...] = jnp.full(m_ref.shape, NEG_BIG, F32)
    l_ref[...] = jnp.zeros(l_ref.shape, F32)
    acc_ref[...] = jnp.zeros(acc_ref.shape, F32)
    tie_ref[...] = jnp.zeros(tie_ref.shape, I32)
    lower_incl = jnp.where(lax.broadcasted_iota(I32, (tk, tk), 0) >= lax.broadcasted_iota(I32, (tk, tk), 1),
                           1.0, 0.0).astype(BF16)

    splits =jnp.where((n_ge > n_sel) & (tau != INT_MIN), 1.0, 0.0)
    any_split = jnp.max(splits) > 0.0

    def attend_tile(kt, cur_ref, next_ref, general_bias, tie_split):
        ks = tile_start(kt)
        key = key_ref[pl.ds(ks, tk), :]
        if tie_split:
            eq = key == tau
            prefix = jnp.dot(lower_incl, jnp.where(eq, 1.0, 0.0).astype(BF16), preferred_element_type=F32)
            tie_before = tie_ref[0:1, :]
            tie_rank = tie_before + prefix.astype(I32)
            sel = ((key > tau) | (eq & (tie_rank <= need))) & causal_of(ks)
            tie_ref[0:1, :] = tie_before + prefix[tk - 1:tk, :].astype(I32)
        else:
            sel = (key >= tau) & causal_of(ks)

        if general_bias == "toeplitz":
            offset = [(qmin_ref[bi, qi] + c * LANES) - (kmax_ref[bi, kt] - (tk - 1)) for c in range(tq // LANES)]
            which = [jnp.where(off == 0, 0, jnp.where(off == tk, 1, 2)) for off in offset]
        elif general_bias == "gather":
            posk = posk_ref[0, pl.ds(ks, tk), :]
            bucket = _t5_bucket_of(jnp.maximum(posq - posk, 0))
            for hp in range(DSA_HEADS // 2):
                table = jnp.broadcast_to(tblt_ref[hp:hp + 1, :], (tk, LANES))
                for c in range(tq // LANES):
                    both = jnp.take_along_axis(table, bucket[:, c * LANES:(c + 1) * LANES], axis=1)
                    lo, hi = 2 * hp * tq + c * LANES, (2 * hp + 1) * tq + c * LANES
                    bias_ref[:, lo:lo + LANES] = lax.bitcast_convert_type(both & jnp.int32(-65536), F32)
                    bias_ref[:, hi:hi + LANES] = lax.bitcast_convert_type(lax.shift_left(both, 16), F32)
        stage(kc_ref, qcw_ref, jnp.minimum(kt + 1, n_tiles - 1), next_ref)
        m_all = m_ref[0:1, :]
        l_all = l_ref[0:1, :]
        v1 = v1_ref[0, :, pl.ds(ks, tk)]
        for h in range(DSA_HEADS):
            hl = head_lanes(h)
            s = cur_ref[:, hl]
            if general_bias == "toeplitz":
                s = s + jnp.concatenate([toe_ref[which[c], h] for c in range(tq // LANES)], axis=1)
            elif general_bias == "gather":
                s = s + bias_ref[:, hl]
            s = jnp.where(sel, s, NEG_BIG)
            m_old = m_all[:, hl]
            m_new = jnp.maximum(m_old, jnp.max(s, axis=0, keepdims=True))
            m_ref[0:1, hl] = m_new
            alpha = jnp.exp2(m_old - m_new)
            pv = jnp.dot(v1, jnp.exp2(s - m_new).astype(BF16), preferred_element_type=F32)
            acc_ref[:, hl] = alpha * acc_ref[:, hl] + pv[0:DSA_HEAD_DIM, :]
            l_ref[0:1, hl] = alpha * l_all[:, hl] + pv[DSA_HEAD_DIM:DSA_HEAD_DIM + 1, :]

    bi = pl.program_id(0)

    def attend(kt, cur_ref, next_ref, last_pair, last_tile):
        far = (qmin_ref[bi, qi] - kmax_ref[bi, kt]) >= T5_FAR
        no_split = jnp.logical_not(any_split)
        near = jnp.logical_and(jnp.logical_not(far), no_split)
        consecutive = consec_ref[bi] == 1

        @pl.when(jnp.logical_and(far, no_split))
        def _():
            attend_tile(kt, cur_ref, next_ref, general_bias=None, tie_split=False)

        @pl.when(jnp.logical_and(near, consecutive))
        def _():
            attend_tile(kt, cur_ref, next_ref, general_bias="toeplitz", tie_split=False)

        @pl.when(jnp.logical_and(near, jnp.logical_not(consecutive)))
        def _():
            attend_tile(kt, cur_ref, next_ref, general_bias="gather", tie_split=False)

        @pl.when(any_split)
        def _():
            attend_tile(kt, cur_ref, next_ref, general_bias="gather", tie_split=True)

    tile_pairs(attend, peel_last=False)

    for h in range(DSA_HEADS):
        hl = head_lanes(h)
        ot_ref[h * DSA_HEAD_DIM:(h + 1) * DSA_HEAD_DIM, :] = acc_ref[:, hl] / l_ref[0:1, hl]
    o_ref[0] = (ot_ref[...].T * gate_ref[0]).astype(o_ref.dtype)


DSA_TOEPLITZ_OFFSETS = (0, DSA_KEY_TILE)


def _bias_toeplitz_kernel(tbl_ref, o_ref):
    j = pl.program_id(0)
    tk = DSA_KEY_TILE
    diff = lax.broadcasted_iota(I32, (tk, LANES), 1) - lax.broadcasted_iota(I32, (tk, LANES), 0)
    for slot, offset in enumerate(DSA_TOEPLITZ_OFFSETS):
        @pl.when(j == slot)
        def _(offset=offset):
            bucket = _t5_bucket_of(jnp.maximum(diff + offset, 0))
            for h in range(DSA_HEADS):
                table = jnp.broadcast_to(tbl_ref[h:h + 1, :], (tk, LANES))
                o_ref[0, h] = jnp.take_along_axis(table, bucket, axis=1)

    @pl.when(j == len(DSA_TOEPLITZ_OFFSETS))
    def _():
        o_ref[...] = jnp.zeros(o_ref.shape, F32)


def _bias_toeplitz(tbl_rows):
    n = len(DSA_TOEPLITZ_OFFSETS) + 1
    return pl.pallas_call(
        _bias_toeplitz_kernel,
        grid=(n,),
        in_specs=[pl.BlockSpec((DSA_HEADS, LANES), lambda j: (0, 0))],
        out_specs=pl.BlockSpec((1, DSA_HEADS, DSA_KEY_TILE, LANES), lambda j: (j, 0, 0, 0)),
        out_shape=jax.ShapeDtypeStruct((n, DSA_HEADS, DSA_KEY_TILE, LANES), F32),
        compiler_params=_compiler_params(("parallel",), 0),
        name="bias_toeplitz",
    )(tbl_rows)


def _dsa_bias_tables(rel_bias):
    tbl = rel_bias.astype(F32) * LOG2E
    tbl = tbl - tbl[REL_BUCKETS - 1:REL_BUCKETS, :]
    rows = jnp.pad(tbl.T, ((0, 0), (0, LANES - REL_BUCKETS)))
    bits = lax.bitcast_convert_type(tbl.T.astype(jnp.bfloat16), jnp.uint16).astype(jnp.uint32)
    packed = lax.bitcast_convert_type((bits[0::2] << 16) | bits[1::2], I32)
    packed = jnp.pad(packed, ((0, SUBLANES - DSA_HEADS // 2), (0, LANES - REL_BUCKETS)))
    return packed, _bias_toeplitz(rows)


def _dsa(bias_tables, qcw, qiw, kc, ki, v1, wi, pos_col, pos_row, gate, tq):
    b, s, _ = kc.shape
    n_sel = min(TOPK_MAX, s // 4)
    vw_rows = v1.shape[1]
    hw = DSA_HEADS * tq
    tbl_t, toeplitz = bias_tables
    q_min = jnp.min(pos_row.reshape(b, s // tq, tq), axis=-1)
    k_max = jnp.max(pos_row.reshape(b, s // DSA_KEY_TILE, DSA_KEY_TILE), axis=-1)
    consecutive = jnp.all(pos_row[:, 0, 1:] - pos_row[:, 0, :-1] == 1, axis=-1).astype(I32)
    tk = DSA_KEY_TILE
    vmem = (2 * (2 * _nbytes((DSA_HEAD_DIM, hw), BF16) + 2 * _nbytes((s, LANES), BF16) + _nbytes((vw_rows, s), F32)
                 + _nbytes((vw_rows, tq), F32) + _nbytes((s, LANES), I32) + _nbytes((tq, DSA_WIDTH), F32)
                 + _nbytes((tq, DSA_WIDTH), BF16)) + _nbytes((s, tq), I32) + 2 * _nbytes((tk, hw), F32)
            + _nbytes((tk, hw), BF16) + _nbytes((DSA_HEAD_DIM, hw), F32) + _nbytes((DSA_WIDTH, tq), F32)
            + 32 * _nbytes((tk, tq), F32))
    return pl.pallas_call(
        functools.partial(_dsa_kernel, n_sel=n_sel, tq=tq),
        grid=(b, s // tq),
        in_specs=[pl.BlockSpec(memory_space=pltpu.SMEM), pl.BlockSpec(memory_space=pltpu.SMEM),
                  pl.BlockSpec(memory_space=pltpu.SMEM),
                  pl.BlockSpec((SUBLANES, LANES), lambda bi, i: (0, 0)),
                  pl.BlockSpec(toeplitz.shape, lambda bi, i: (0, 0, 0, 0)),
                  pl.BlockSpec((1, 1, DSA_HEAD_DIM, hw), lambda bi, i: (bi, i, 0, 0)),
                  pl.BlockSpec((1, 1, IDX_DIM, hw), lambda bi, i: (bi, i, 0, 0)),
                  pl.BlockSpec((1, s, DSA_HEAD_DIM), lambda bi, i: (bi, 0, 0)),
                  pl.BlockSpec((1, s, IDX_DIM), lambda bi, i: (bi, 0, 0)),
                  pl.BlockSpec((1, vw_rows, s), lambda bi, i: (bi, 0, 0)),
                  pl.BlockSpec((1, IDX_HEADS, tq), lambda bi, i: (bi, 0, i)),
                  pl.BlockSpec((1, s, 1), lambda bi, i: (bi, 0, 0)),
                  pl.BlockSpec((1, 1, tq), lambda bi, i: (bi, 0, i)),
                  pl.BlockSpec((1, tq, DSA_WIDTH), lambda bi, i: (bi, i, 0))],
        out_specs=pl.BlockSpec((1, tq, DSA_WIDTH), lambda bi, i: (bi, i, 0)),
        out_shape=jax.ShapeDtypeStruct((b, s, DSA_WIDTH), BF16),
        scratch_shapes=[pltpu.VMEM((s, tq), I32),
                        pltpu.VMEM((s, tq), I16), pltpu.VMEM((s, tq), I16),
                        pltpu.VMEM((tk, hw), F32), pltpu.VMEM((tk, hw), F32),
                        pltpu.VMEM((tk, hw), F32),
                        pltpu.VMEM((DSA_HEAD_DIM, hw), F32),
                        pltpu.VMEM((DSA_WIDTH, tq), F32),
                        pltpu.VMEM((SUBLANES, hw), F32), pltpu.VMEM((SUBLANES, hw), F32),
                        pltpu.VMEM((SUBLANES, tq), I32)],
        compiler_params=_compiler_params(("parallel", "arbitrary"), vmem),
        name="dsa",
    )(q_min, k_max, consecutive, tbl_t, toeplitz, qcw, qiw, kc, ki, v1, wi, pos_col, pos_row, gate)


def _merge_kernel(oa_ref, ob_ref, oc_ref, g_ref, x_ref, p_ref, wa_ref, wb_ref, wc_ref, wo_ref, wpg_ref, wple_ref,
                  gn_ref, xo_ref, ho_ref):
    d = D_MODEL
    ya = jnp.dot(oa_ref[0], wa_ref[0], preferred_element_type=F32)
    yb = jnp.dot(ob_ref[0], wb_ref[0], preferred_element_type=F32)
    yc = jnp.dot(oc_ref[0], wc_ref[0], preferred_element_type=F32)
    merged = g_ref[0, :, 0:d] * ya + g_ref[0, :, d:2 * d] * yb + g_ref[0, :, 2 * d:3 * d] * yc
    x1 = x_ref[0] + jnp.dot(merged.astype(BF16), wo_ref[0], preferred_element_type=F32)
    ple_gate = _sigmoid(jnp.dot(x1.astype(BF16), wpg_ref[0], preferred_element_type=F32))
    x2 = x1 + ple_gate * jnp.dot(p_ref[0, 0].astype(BF16), wple_ref[0], preferred_element_type=F32)
    xo_ref[0] = x2
    ho_ref[0] = _rms(x2, gn_ref[...]).astype(ho_ref.dtype)


def _merge(oa, ob, oc, gates, x, p, layer, wa, wb, wc, wo, wpg, wple, g_next, h_dtype, tm=512):
    b, s, d = x.shape
    full = lambda a: pl.BlockSpec((1,) + a.shape[1:], lambda bi, i: (layer, 0, 0), pipeline_mode=pl.Buffered(1))
    row = lambda a: pl.BlockSpec((1, tm, a.shape[2]), lambda bi, i: (bi, i, 0))
    weights = (wa, wb, wc, wo, wpg, wple)
    acts = (oa, ob, oc, gates, x)
    vmem = (sum(_nbytes(a.shape[1:], a.dtype) for a in weights)
            + 2 * (sum(_nbytes((tm, a.shape[2]), a.dtype) for a in acts) + _nbytes((tm, p.shape[3]), p.dtype)
                   + 2 * _nbytes((tm, d), F32)) + 8 * _nbytes((tm, d), F32))
    return pl.pallas_call(
        _merge_kernel,
        grid=(b, s // tm),
        in_specs=[row(a) for a in acts] + [pl.BlockSpec((1, 1, tm, p.shape[3]), lambda bi, i: (layer, bi, i, 0))]
        + [full(a) for a in weights] + [pl.BlockSpec((1, d), lambda bi, i: (0, 0))],
        out_specs=[pl.BlockSpec((1, tm, d), lambda bi, i: (bi, i, 0)), pl.BlockSpec((1, tm, d), lambda bi, i: (bi, i, 0))],
        out_shape=[jax.ShapeDtypeStruct((b, s, d), F32), jax.ShapeDtypeStruct((b, s, d), h_dtype)],
        compiler_params=_compiler_params(("parallel", "parallel"), vmem),
        name="merge",
    )(*acts, p, *weights, g_next.reshape(1, d))


def _rotate_half_cols(w):
    half = w.shape[-1] // 2
    return jnp.concatenate([-w[..., half:], w[..., :half]], axis=-1)


def _prep_weights(w_in, w_uq, w_ukv):
    depth = w_in.shape[0]
    seg = lambda name: w_in[:, :, SPLIT[name][0]:SPLIT[name][1]]
    z = lambda *shape: jnp.zeros((depth,) + shape, w_in.dtype)
    d = D_MODEL
    kr = seg("k_rope")
    rope_lo = MLA_NOPE
    rope_pad = MLA_QK_PAD - MLA_NOPE - MLA_ROPE
    w = {}
    w["mla_in"] = jnp.concatenate(
        [seg("c_q"), seg("c_kv"), z(d, rope_lo), kr, z(d, rope_pad), z(d, rope_lo), _rotate_half_cols(kr), z(d, rope_pad)], axis=-1)
    w["gate_a"] = seg("gate_a")
    w["z"] = seg("z")
    w["xbc"] = seg("xbc")
    w["small"] = jnp.concatenate([seg("dt"), z(d, LANES - SSM_HEADS)], axis=-1)
    w["k_c"] = seg("k_c")
    w["k_idx"] = seg("k_idx")
    w["gate_c"] = seg("gate_c")
    w["merge"] = seg("merge")
    w["q_c_t"] = jnp.swapaxes(seg("q_c") * (DSA_HEAD_DIM ** -0.5 * LOG2E), 1, 2)
    w["q_idx_t"] = jnp.swapaxes(seg("q_idx"), 1, 2)
    w["vw_t"] = jnp.swapaxes(jnp.concatenate([seg("v_c"), seg("w_idx")], axis=-1), 1, 2)
    w = {
        "proj": jnp.concatenate([w["mla_in"]] + [w[seg[0]] for seg in PROJ_SEGMENTS], axis=-1).astype(BF16),
        "proj_t": jnp.concatenate([w["q_c_t"], w["q_idx_t"], w["vw_t"]], axis=1).astype(BF16),
    }

    uq =w_uq.reshape(depth, MLA_Q_LORA, MLA_HEADS, MLA_NOPE + MLA_ROPE)
    nope, rope = uq[..., :MLA_NOPE], uq[..., MLA_NOPE:]
    zq = lambda width: jnp.zeros((depth, MLA_Q_LORA, MLA_HEADS, width), w_uq.dtype)
    hq = MLA_HEADS * MLA_QK_PAD
    uq1 = jnp.concatenate([nope, rope, zq(rope_pad)], axis=-1).reshape(depth, MLA_Q_LORA, hq)
    uq2 = _rotate_half_cols(rope).reshape(depth, MLA_Q_LORA, MLA_HEADS * MLA_ROPE)
    w["uq1_t"] = jnp.swapaxes(uq1, 1, 2).astype(BF16)
    w["uq2_t"] = jnp.swapaxes(uq2, 1, 2).astype(BF16)
    ukv = w_ukv.reshape(depth, MLA_KV_LORA, MLA_HEADS, MLA_NOPE + MLA_V)
    zk = jnp.zeros((depth, MLA_KV_LORA, MLA_HEADS, MLA_QK_PAD - MLA_NOPE), w_ukv.dtype)
    w["uk"] = jnp.concatenate([ukv[..., :MLA_NOPE], zk], axis=-1).reshape(depth, MLA_KV_LORA, hq).astype(BF16)
    w["uv_t"] = jnp.swapaxes(ukv[..., MLA_NOPE:].reshape(depth, MLA_KV_LORA, MLA_WIDTH), 1, 2).astype(BF16)
    return w


def _rope_multipliers(positions):
    b, s = positions.shape
    inv_freq = 1.0 / (ROPE_THETA ** (jnp.arange(0, MLA_ROPE, 2, dtype=F32) / MLA_ROPE))
    ang = positions.astype(F32)[..., None] * inv_freq
    cos, sin = jnp.cos(ang), jnp.sin(ang)
    pad = MLA_QK_PAD - MLA_NOPE - MLA_ROPE
    m1 = jnp.concatenate([jnp.ones((b, s, MLA_NOPE), F32), cos, cos, jnp.zeros((b, s, pad), F32)], axis=-1)
    m2 = jnp.concatenate([jnp.zeros((b, s, MLA_NOPE), F32), sin, sin, jnp.zeros((b, s, pad), F32)], axis=-1)
    return m1, m2, jnp.swapaxes(m1, 1, 2), jnp.swapaxes(m2, 1, 2)


def kernel(x, p, positions, norm_g, w_in, mla_q_norm, w_uq, mla_kv_norm, w_ukv, conv_w, conv_b, dt_bias, a_log, d_skip, ssm_norm, w_br_a, w_br_b, w_br_c, w_out, rel_bias, w_ple, w_ple_gate, final_norm):
    b, s, d = x.shape
    depth = w_in.shape[0]
    w = _prep_weights(w_in, w_uq, w_ukv)
    m1, m2, m1t, m2t = _rope_multipliers(positions)
    pos_col = positions.astype(I32).reshape(b, s, 1)
    pos_row = positions.astype(I32).reshape(b, 1, s)
    wbr_a, wbr_b, wbr_c = w_br_a.astype(BF16), w_br_b.astype(BF16), w_br_c.astype(BF16)
    wo, wpg, wple = w_out.astype(BF16), w_ple_gate.astype(BF16), w_ple.astype(BF16)

    q_norm = mla_q_norm.astype(F32).reshape(depth, 1, -1)
    kv_norm = mla_kv_norm.astype(F32).reshape(depth, 1, -1)

    bias_tables = _dsa_bias_tables(rel_bias)
    h = _norm(x, norm_g[0], BF16)
    for i in range(depth):
        pr = _proj_all(h, i, w["proj"], w["proj_t"], (m1, m2, m1t, m2t), q_norm, kv_norm,
                       w["uq1_t"], w["uq2_t"], w["uk"], w["uv_t"], DSA_Q_BLOCK)
        o_a = _mla_attn(pr["mla_q_t"], pr["mla_k"], pr["mla_v_t1"], pr["gate_a"])
        o_b = _ssd(pr["xbc"], pr["z"], pr["small"], conv_w[i], conv_b[i], dt_bias[i], a_log[i], d_skip[i], ssm_norm[i])
        o_c = _dsa(bias_tables, pr["q_c_wide"], pr["q_idx_wide"], pr["k_c"], pr["k_idx"], pr["v_t1"], pr["w_idx_t"], pos_col, pos_row,
                   pr["gate_c"], DSA_Q_BLOCK)

        last = i == depth - 1
        g_next = final_norm if last else norm_g[i + 1]
        x, h = _merge(o_a, o_b, o_c, pr["merge"], x, p, i, wbr_a, wbr_b, wbr_c, wo, wpg, wple, g_next,
                      F32 if last else BF16)
    return h
```

```python
import functools
import math

import jax
import jax.numpy as jnp
from jax import lax
from jax.experimental import pallas as pl
from jax.experimental.pallas import tpu as pltpu

F32 = jnp.float32
BF16 = jnp.bfloat16
I32 = jnp.int32
I16 = jnp.int16
HALF16 = 1 << 15

D_MODEL = 1024
NORM_EPS = 1e-6

MLA_HEADS = 8
MLA_NOPE = 64
MLA_ROPE = 32
MLA_V = 64
MLA_Q_LORA = 384
MLA_KV_LORA = 256
MLA_WIDTH = MLA_HEADS * MLA_V
ROPE_THETA = 10000.0
MLA_QK_PAD = 128
LOG2E = math.log2(math.e)
MLA_Q_SCALE = (MLA_NOPE + MLA_ROPE) ** -0.5 * LOG2E

SSM_HEADS = 16
SSM_HEAD_DIM = 64
SSM_INNER = SSM_HEADS * SSM_HEAD_DIM
SSM_GROUPS = 2
SSM_STATE = 128
SSM_CONV = 4
SSM_CHUNK = 128
SSM_CONV_DIM = SSM_INNER + 2 * SSM_GROUPS * SSM_STATE
SSM_HEADS_PER_GROUP = SSM_HEADS // SSM_GROUPS
ONES_ROWS = 16
MLA_V1_ROWS = MLA_V + ONES_ROWS
SSM_CONV_TAIL = 16

DSA_HEADS = 8
DSA_HEAD_DIM = 64
DSA_WIDTH = DSA_HEADS * DSA_HEAD_DIM
IDX_HEADS = 8
IDX_DIM = 64
TOPK_MAX = 256
DSA_KEY_TILE = 128
DSA_Q_BLOCK = 256

REL_BUCKETS = 32
REL_MAX_DIST = 128
N_BRANCHES = 3

LANES = 128
SUBLANES = 8
VMEM_LIMIT_CAP = 56 * 1024 * 1024
VMEM_LIMIT_FLOOR = 32 * 1024 * 1024
INT_MIN = -(2 ** 31)
NEG_BIG = -1e30

SPLIT_SIZES = (
    MLA_Q_LORA, MLA_KV_LORA, MLA_ROPE, MLA_WIDTH, SSM_INNER, SSM_CONV_DIM, SSM_HEADS, DSA_WIDTH,
    DSA_HEAD_DIM, DSA_HEAD_DIM, IDX_HEADS * IDX_DIM, IDX_DIM, IDX_HEADS, DSA_WIDTH,
    N_BRANCHES * D_MODEL,
)
SPLIT_NAMES = ("c_q", "c_kv", "k_rope", "gate_a", "z", "xbc", "dt", "q_c", "k_c", "v_c", "q_idx",
               "k_idx", "w_idx", "gate_c", "merge")


def _split_bounds():
    out, off = {}, 0
    for name, size in zip(SPLIT_NAMES, SPLIT_SIZES, strict=True):
        out[name] = (off, off + size)
        off += size
    return out


SPLIT = _split_bounds()


def _t5_large_thresholds():
    exact = REL_BUCKETS // 2
    thr = []
    for j in range(1, REL_BUCKETS - exact):
        thr.append(int(math.ceil(exact * (REL_MAX_DIST / exact) ** (j / (REL_BUCKETS - exact)) - 1e-9)))
    return tuple(thr)


T5_EXACT = REL_BUCKETS // 2
T5_LARGE_THR = _t5_large_thresholds()
T5_FAR = T5_LARGE_THR[-1]


def _compiler_params(semantics, vmem_bytes):
    limit = int(min(VMEM_LIMIT_CAP, max(VMEM_LIMIT_FLOOR, vmem_bytes)))
    return pltpu.CompilerParams(dimension_semantics=semantics, vmem_limit_bytes=limit)


def _nbytes(shape, dtype):
    return math.prod(shape) * jnp.dtype(dtype).itemsize


def _sigmoid(x):
    return 1.0 / (1.0 + jnp.exp(-x))


def _rms(x, g):
    return x * lax.rsqrt(jnp.mean(x * x, axis=-1, keepdims=True) + NORM_EPS) * g


MLA_IN_WIDTH = MLA_Q_LORA + MLA_KV_LORA + 2 * LANES
PROJ_SEGMENTS = (
    ("small", LANES, None, F32),
    ("gate_a", MLA_WIDTH, "silu", BF16),
    ("gate_c", DSA_WIDTH, "silu", BF16),
    ("z", SSM_INNER, "silu", BF16),
    ("xbc", SSM_CONV_DIM, None, BF16),
    ("merge", N_BRANCHES * D_MODEL, "sigmoid", BF16),
    ("k_c", DSA_HEAD_DIM, None, BF16),
    ("k_idx", IDX_DIM, None, BF16),
)
PROJ_CHUNK = 512
PROJ_T_ROWS = (DSA_WIDTH, IDX_HEADS * IDX_DIM, DSA_HEAD_DIM + IDX_HEADS)


def _proj_all_kernel(x_ref, g_ref, w_ref, wt_ref, m1_ref, m2_ref, m1t_ref, m2t_ref, qn_ref, kvn_ref, wq1t_ref, wq2t_ref,
                     wk_ref, wvt_ref, *out_refs):
    h = _rms(x_ref[0], g_ref[0]).astype(BF16)
    n_seg = len(PROJ_SEGMENTS)
    a = jnp.dot(h, w_ref[0, :, 0:MLA_IN_WIDTH], preferred_element_type=F32)
    _mla_prep(a, m1_ref, m2_ref, m1t_ref, m2t_ref, qn_ref, kvn_ref, wq1t_ref, wq2t_ref, wk_ref, wvt_ref, *out_refs[n_seg + 4:])
    off = MLA_IN_WIDTH
    for (_, width, act, _), o_ref in zip(PROJ_SEGMENTS, out_refs[:n_seg], strict=True):
        for c in range(0, width, PROJ_CHUNK):
            cw = min(PROJ_CHUNK, width - c)
            y = jnp.dot(h, w_ref[0, :, off + c:off + c + cw], preferred_element_type=F32)
            if act == "silu":
                y = y * _sigmoid(y)
            elif act == "sigmoid":
                y = _sigmoid(y)
            o_ref[0, :, c:c + cw] = y.astype(o_ref.dtype)
        off += width
    qcw_ref, qiw_ref, v1_ref, wi_ref = out_refs[n_seg:n_seg + 4]
    nt = (((1,), (1,)), ((), ()))
    tq = h.shape[0]
    row = 0
    for o_ref, heads, dim in ((qcw_ref, DSA_HEADS, DSA_HEAD_DIM), (qiw_ref, IDX_HEADS, IDX_DIM)):
        y = lax.dot_general(wt_ref[0, row:row + heads * dim, :], h, nt, preferred_element_type=F32)
        for hd in range(heads):
            o_ref[0, 0, :, hd * tq:(hd + 1) * tq] = y[hd * dim:(hd + 1) * dim, :].astype(o_ref.dtype)
        row += heads * dim
    vw = lax.dot_general(wt_ref[0, row:row + PROJ_T_ROWS[2], :], h, nt, preferred_element_type=F32)
    v1_ref[0, 0:DSA_HEAD_DIM, :] = vw[0:DSA_HEAD_DIM, :].astype(v1_ref.dtype)
    v1_ref[0, DSA_HEAD_DIM:, :] = jnp.ones((ONES_ROWS, tq), v1_ref.dtype)
    wi_ref[0] = vw[DSA_HEAD_DIM:, :]


def _proj_all(x, layer, gains, w, wt, rope, qn, kvn, wq1t, wq2t, wk, wvt, tq):
    b, s, k = x.shape
    n_all = MLA_IN_WIDTH + sum(seg[1] for seg in PROJ_SEGMENTS)
    hq = MLA_HEADS * MLA_QK_PAD
    v1_rows = DSA_HEAD_DIM + ONES_ROWS
    out_shapes = [jax.ShapeDtypeStruct((b, s, width), dt) for _, width, _, dt in PROJ_SEGMENTS]
    out_specs = [pl.BlockSpec((1, tq, width), lambda bi, i: (bi, i, 0)) for _, width, _, _ in PROJ_SEGMENTS]
    for heads, dim in ((DSA_HEADS, DSA_HEAD_DIM), (IDX_HEADS, IDX_DIM)):
        out_shapes.append(jax.ShapeDtypeStruct((b, s // tq, dim, heads * tq), BF16))
        out_specs.append(pl.BlockSpec((1, 1, dim, heads * tq), lambda bi, i: (bi, i, 0, 0)))
    for rows, dt in ((v1_rows, BF16), (IDX_HEADS, F32)):
        out_shapes.append(jax.ShapeDtypeStruct((b, rows, s), dt))
        out_specs.append(pl.BlockSpec((1, rows, tq), lambda bi, i: (bi, 0, i)))
    for rows in (hq, None, MLA_HEADS * MLA_V1_ROWS):
        if rows is None:
            out_shapes.append(jax.ShapeDtypeStruct((b, s, hq), BF16))
            out_specs.append(pl.BlockSpec((1, tq, hq), lambda bi, i: (bi, i, 0)))
        else:
            out_shapes.append(jax.ShapeDtypeStruct((b, rows, s), BF16))
            out_specs.append(pl.BlockSpec((1, rows, tq), lambda bi, i: (bi, 0, i)))
    resident = pl.Buffered(1)
    weights = (w, wt, qn, kvn, wq1t, wq2t, wk, wvt)
    layer_spec = lambda a: pl.BlockSpec((1,) + a.shape[1:], lambda bi, i: (layer,) + (0,) * (a.ndim - 1), pipeline_mode=resident)
    m1, m2, m1t, m2t = rope
    vmem = (sum(_nbytes(a.shape[1:], a.dtype) for a in weights) + 4 * _nbytes((tq, k), F32) + 8 * _nbytes((tq, LANES), F32)
            + 2 * sum(_nbytes((tq, width), dt) for _, width, _, dt in PROJ_SEGMENTS)
            + 2 * (2 * _nbytes((DSA_HEAD_DIM, DSA_HEADS * tq), BF16) + _nbytes((v1_rows + IDX_HEADS, tq), F32))
            + 2 * (2 * _nbytes((tq, hq), BF16) + _nbytes((MLA_HEADS * MLA_V1_ROWS, tq), BF16))
            + 8 * _nbytes((tq, PROJ_CHUNK), F32) + 6 * _nbytes((tq, hq), F32))
    outs = pl.pallas_call(
        _proj_all_kernel,
        grid=(b, s // tq),
        in_specs=[pl.BlockSpec((1, tq, k), lambda bi, i: (bi, i, 0)), layer_spec(gains), layer_spec(w), layer_spec(wt),
                  pl.BlockSpec((1, tq, LANES), lambda bi, i: (bi, i, 0)), pl.BlockSpec((1, tq, LANES), lambda bi, i: (bi, i, 0)),
                  pl.BlockSpec((1, LANES, tq), lambda bi, i: (bi, 0, i)), pl.BlockSpec((1, LANES, tq), lambda bi, i: (bi, 0, i))]
        + [layer_spec(a) for a in weights[2:]],
        out_specs=out_specs,
        out_shape=out_shapes,
        compiler_params=_compiler_params(("parallel", "parallel"), vmem),
        name="proj_all",
    )(x, gains, w, wt, m1, m2, m1t, m2t, *weights[2:])
    names = [seg[0] for seg in PROJ_SEGMENTS] + ["q_c_wide", "q_idx_wide", "v_t1", "w_idx_t", "mla_q_t", "mla_k", "mla_v_t1"]
    return dict(zip(names, outs, strict=True))


def _mla_prep(a, m1_ref, m2_ref, m1t_ref, m2t_ref, qn_ref, kvn_ref, wq1t_ref, wq2t_ref, wk_ref, wvt_ref, qt_ref, k_ref, vt_ref):
    c_q = a[:, :MLA_Q_LORA]
    c_kv = a[:, MLA_Q_LORA:MLA_Q_LORA + MLA_KV_LORA]
    kr1 = a[:, MLA_Q_LORA + MLA_KV_LORA:MLA_Q_LORA + MLA_KV_LORA + LANES]
    kr2 = a[:, MLA_Q_LORA + MLA_KV_LORA + LANES:]
    cqn = _rms(c_q, qn_ref[0]).astype(BF16)
    ckvn = _rms(c_kv, kvn_ref[0]).astype(BF16)
    nt = (((1,), (1,)), ((), ()))
    qa_t = lax.dot_general(wq1t_ref[0], cqn, nt, preferred_element_type=F32)
    qb_t = lax.dot_general(wq2t_ref[0], cqn, nt, preferred_element_type=F32)
    kn = jnp.dot(ckvn, wk_ref[0], preferred_element_type=F32)
    kr = kr1 * m1_ref[0] + kr2 * m2_ref[0]
    rope = slice(MLA_NOPE, MLA_NOPE + MLA_ROPE)
    cos_t = m1t_ref[0, rope, :] * MLA_Q_SCALE
    sin_t = m2t_ref[0, rope, :] * MLA_Q_SCALE
    pad_rows = jnp.zeros((MLA_QK_PAD - MLA_NOPE - MLA_ROPE, qa_t.shape[1]), qt_ref.dtype)
    for h in range(MLA_HEADS):
        base = h * MLA_QK_PAD
        sl = slice(base, base + MLA_QK_PAD)
        qt_ref[0, base:base + MLA_NOPE, :] = (qa_t[base:base + MLA_NOPE, :] * MLA_Q_SCALE).astype(qt_ref.dtype)
        qt_ref[0, base + MLA_NOPE:base + MLA_NOPE + MLA_ROPE, :] = (
            qa_t[base + MLA_NOPE:base + MLA_NOPE + MLA_ROPE, :] * cos_t
            + qb_t[h * MLA_ROPE:(h + 1) * MLA_ROPE, :] * sin_t).astype(qt_ref.dtype)
        qt_ref[0, base + MLA_NOPE + MLA_ROPE:base + MLA_QK_PAD, :] = pad_rows
        k_ref[0, :, sl] = (kn[:, sl] + kr).astype(k_ref.dtype)
    v_t = lax.dot_general(wvt_ref[0], ckvn, nt, preferred_element_type=F32)
    ones = jnp.ones((ONES_ROWS, v_t.shape[1]), vt_ref.dtype)
    for h in range(MLA_HEADS):
        base = h * MLA_V1_ROWS
        vt_ref[0, base:base + MLA_V, :] = v_t[h * MLA_V:(h + 1) * MLA_V, :].astype(vt_ref.dtype)
        vt_ref[0, base + MLA_V:base + MLA_V1_ROWS, :] = ones


def _mla_attn_kernel(qt_ref, k_ref, vt_ref, g_ref, o_ref, m_ref, l_ref, acc_ref, s0_ref, s1_ref, *, tq, tk):
    assert tq == 2 * tk
    qi = pl.program_id(1)
    krow = lax.broadcasted_iota(I32, (tk, tq), 0)
    qcol = lax.broadcasted_iota(I32, (tk, tq), 1)
    tiles_per_q = tq // tk
    m_ref[...] = jnp.full(m_ref.shape, -jnp.inf, F32)
    l_ref[...] = jnp.zeros(l_ref.shape, F32)
    acc_ref[...] = jnp.zeros(acc_ref.shape, F32)

    staged = (s0_ref, s1_ref)
    n_tiles = (qi + 1) * tiles_per_q

    def head_lanes(h):
        return slice(h * tq, (h + 1) * tq)

    def stage(j, dst_ref):
        ks = pl.multiple_of(j * tk, tk)
        for h in range(MLA_HEADS):
            k = k_ref[0, pl.ds(ks, tk), h * MLA_QK_PAD:(h + 1) * MLA_QK_PAD]
            dst_ref[:, head_lanes(h)] = jnp.dot(k, qt_ref[0, h * MLA_QK_PAD:(h + 1) * MLA_QK_PAD, :],
                                                preferred_element_type=F32)

    def tile(j, cur_ref, next_ref, diagonal):
        ks = pl.multiple_of(j * tk, tk)
        stage(jnp.minimum(j + 1, n_tiles - 1), next_ref)
        m_all = m_ref[...]
        l_all = l_ref[...]
        if diagonal:
            causal = ks + krow <= qi * tq + qcol
        for h in range(MLA_HEADS):
            s = cur_ref[:, head_lanes(h)]
            if diagonal:
                s = jnp.where(causal, s, -jnp.inf)
            m_old = m_all[h:h + 1, :]
            m_new = jnp.maximum(m_old, jnp.max(s, axis=0, keepdims=True))
            alpha = jnp.exp2(m_old - m_new)
            p = jnp.exp2(s - m_new).astype(BF16)
            m_ref[h:h + 1, :] = m_new
            pv = jnp.dot(vt_ref[0, h * MLA_V1_ROWS:(h + 1) * MLA_V1_ROWS, pl.ds(ks, tk)], p, preferred_element_type=F32)
            hs = slice(h * MLA_V, (h + 1) * MLA_V)
            acc_ref[hs, :] = alpha * acc_ref[hs, :] + pv[0:MLA_V, :]
            l_ref[h:h + 1, :] = alpha * l_all[h:h + 1, :] + pv[MLA_V:MLA_V + 1, :]

    stage(0, staged[0])

    def full_pair(pi, carry):
        for half in range(2):
            tile(2 * pi + half, staged[half], staged[1 - half], diagonal=False)
        return carry

    lax.fori_loop(0, qi, full_pair, 0)
    for half in range(2):
        tile(2 * qi + half, staged[half], staged[1 - half], diagonal=True)
    for h in range(MLA_HEADS):
        hs = slice(h * MLA_V, (h + 1) * MLA_V)
        acc_ref[hs, :] = acc_ref[hs, :] / l_ref[h:h + 1, :]
    o_ref[0] = (acc_ref[...].T * g_ref[0]).astype(o_ref.dtype)


def _mla_attn(qt, k, vt, gate, tq=256, tk=128):
    b, s, hq = k.shape
    vmem = 2 * (_nbytes((hq, tq), BF16) + _nbytes((s, hq), BF16) + _nbytes((MLA_WIDTH, s), BF16)
                + _nbytes((tq, MLA_WIDTH), F32) + _nbytes((tq, MLA_WIDTH), BF16)) + _nbytes((MLA_WIDTH, tq), F32) + 16 * _nbytes((tk, tq), F32)
    return pl.pallas_call(
        functools.partial(_mla_attn_kernel, tq=tq, tk=tk),
        grid=(b, s // tq),
        in_specs=[pl.BlockSpec((1, hq, tq), lambda bi, i: (bi, 0, i)),
                  pl.BlockSpec((1, s, hq), lambda bi, i: (bi, 0, 0)),
                  pl.BlockSpec((1, MLA_HEADS * MLA_V1_ROWS, s), lambda bi, i: (bi, 0, 0)),
                  pl.BlockSpec((1, tq, MLA_WIDTH), lambda bi, i: (bi, i, 0))],
        out_specs=pl.BlockSpec((1, tq, MLA_WIDTH), lambda bi, i: (bi, i, 0)),
        out_shape=jax.ShapeDtypeStruct((b, s, MLA_WIDTH), BF16),
        scratch_shapes=[pltpu.VMEM((MLA_HEADS, tq), F32), pltpu.VMEM((MLA_HEADS, tq), F32),
                        pltpu.VMEM((MLA_WIDTH, tq), F32),
                        pltpu.VMEM((tk, MLA_HEADS * tq), F32), pltpu.VMEM((tk, MLA_HEADS * tq), F32)],
        compiler_params=_compiler_params(("parallel", "arbitrary"), vmem),
        name="mla_attn",
    )(qt, k, vt, gate)


def _ssd_kernel(xbc_ref, xprev_ref, zs_ref, sm_ref, cw_ref, cb_ref, dtb_ref, alog_ref, dskf_ref, nrm_ref, exp16_ref, exp32_ref,
                o_ref, state_ref, y_ref):
    c = pl.program_id(1)
    q = SSM_CHUNK

    @pl.when(c == 0)
    def _():
        state_ref[...] = jnp.zeros(state_ref.shape, F32)

    x_in = xbc_ref[0]
    tail = xprev_ref.shape[1]
    prev = jnp.where(c > 0, xprev_ref[0].astype(F32), 0.0).astype(x_in.dtype)
    win = jnp.concatenate([prev, x_in], axis=0)
    x = x_in.astype(F32)
    wrow = lax.broadcasted_iota(I32, (q, tail + q), 0)
    wcol = lax.broadcasted_iota(I32, (q, tail + q), 1)
    acc = cb_ref[0] + cw_ref[0, SSM_CONV - 1:SSM_CONV, :] * x
    for j in range(1, SSM_CONV):
        pick = jnp.where(wcol == wrow + (tail - j), 1.0, 0.0).astype(win.dtype)
        acc = acc + cw_ref[0, SSM_CONV - 1 - j:SSM_CONV - j, :] * jnp.dot(pick, win, preferred_element_type=F32)
    half = 0.5 * acc
    xc = half + half * jnp.tanh(half)

    xs = xc[:, :SSM_INNER]
    bm = xc[:, SSM_INNER:SSM_INNER + SSM_GROUPS * SSM_STATE]
    cm = xc[:, SSM_INNER + SSM_GROUPS * SSM_STATE:]

    pre = sm_ref[0] + dtb_ref[0]
    dt = jnp.maximum(pre, 0.0) + jnp.log1p(jnp.exp(-jnp.abs(pre)))
    a = -jnp.exp(alog_ref[0])
    row = lax.broadcasted_iota(I32, (q, q), 0)
    col = lax.broadcasted_iota(I32, (q, q), 1)
    lower = row >= col
    cum = jnp.dot(jnp.where(lower, 1.0, 0.0).astype(F32), dt * a, preferred_element_type=F32,
                  precision=lax.Precision.HIGHEST)
    cum_t = cum.T
    cum_last = cum[q - 1:q, :]
    per_head = jnp.concatenate([dt, jnp.exp(cum_last - cum), jnp.exp(cum)], axis=0).astype(BF16)
    spread = jnp.dot(per_head, exp16_ref[...], preferred_element_type=F32)
    dt_full, to_end_full, e_cum_full = spread[0:q], spread[q:2 * q], spread[2 * q:3 * q]
    chunk_decay_full = jnp.dot(jnp.broadcast_to(jnp.exp(cum_last), (SUBLANES, LANES)), exp32_ref[...],
                               preferred_element_type=F32, precision=lax.Precision.HIGHEST)[0:1]
    xdt = xs * dt_full
    xw = (xdt * to_end_full).astype(BF16)
    first_head = lax.broadcasted_iota(I32, (q, LANES), 1) < SSM_HEAD_DIM
    group_w = SSM_HEADS_PER_GROUP * SSM_HEAD_DIM
    heads_per_tile = LANES // SSM_HEAD_DIM

    for g in range(SSM_GROUPS):
        bg32 = bm[:, g * SSM_STATE:(g + 1) * SSM_STATE]
        bg = bg32.astype(BF16)
        cg = cm[:, g * SSM_STATE:(g + 1) * SSM_STATE].astype(BF16)
        cb = lax.dot_general(cg, bg, (((1,), (1,)), ((), ())), preferred_element_type=F32)
        gl = slice(g * group_w, (g + 1) * group_w)
        st = state_ref[g]
        y_off = jnp.dot(cg, st.astype(BF16), preferred_element_type=F32)
        state_ref[g] = st * chunk_decay_full[:, gl] + jnp.dot(bg32.T.astype(BF16), xw[:, gl], preferred_element_type=F32)
        for tile in range(group_w // LANES):
            h0 = g * SSM_HEADS_PER_GROUP + tile * heads_per_tile
            tl = slice(h0 * SSM_HEAD_DIM, h0 * SSM_HEAD_DIM + LANES)
            g_both = []
            for h in range(h0, h0 + heads_per_tile):
                diff = cum[:, h:h + 1] - cum_t[h:h + 1, :]
                decay_in = jnp.exp(jnp.where(lower, diff, -jnp.inf))
                g_both.append((cb * decay_in).astype(BF16))
            xdt_tile = xdt[:, tl]
            stacked = jnp.concatenate([jnp.where(first_head, xdt_tile, 0.0).astype(BF16),
                                       jnp.where(first_head, 0.0, xdt_tile).astype(BF16)], axis=0)
            y_diag = jnp.dot(jnp.concatenate(g_both, axis=1), stacked, preferred_element_type=F32)
            y = y_diag + y_off[:, tile * LANES:(tile + 1) * LANES] * e_cum_full[:, tl] + xs[:, tl] * dskf_ref[0, :, tl]
            y_ref[:, tl] = y * zs_ref[0, :, tl]

    o_ref[0] = _rms(y_ref[...], nrm_ref[0]).astype(o_ref.dtype)


def _ssd_params(conv_w, conv_b, dt_bias, a_log, d_skip, ssm_norm):
    depth = conv_w.shape[0]
    pad = lambda v: jnp.pad(v.astype(F32), ((0, 0), (0, LANES - v.shape[1]))).reshape(depth, 1, LANES)
    expand = (jnp.arange(LANES)[:, None] == jnp.arange(SSM_INNER)[None, :] // SSM_HEAD_DIM).astype(F32)
    d_skip_full = jnp.repeat(d_skip.astype(F32), SSM_HEAD_DIM, axis=1).reshape(depth, 1, SSM_INNER)
    return (conv_w.astype(F32), conv_b.astype(F32).reshape(depth, 1, -1), pad(dt_bias), pad(a_log), d_skip_full,
            ssm_norm.astype(F32).reshape(depth, 1, -1)), (expand.astype(BF16), expand)


def _ssd(xbc, zs, small, layer, params, expands):
    b, s, _ = xbc.shape
    q = SSM_CHUNK
    full = lambda shape: pl.BlockSpec(shape, lambda bi, c: (0,) * len(shape))
    of_layer = lambda a: pl.BlockSpec((1,) + a.shape[1:], lambda bi, c: (layer, 0, 0))
    blk = lambda width: pl.BlockSpec((1, q, width), lambda bi, c: (bi, c, 0))
    group_w = SSM_HEADS_PER_GROUP * SSM_HEAD_DIM
    vmem = (2 * (_nbytes((q, SSM_CONV_DIM), xbc.dtype) + _nbytes((q, SSM_INNER), zs.dtype) + _nbytes((q, LANES), F32)
                 + _nbytes((q, SSM_INNER), BF16) + _nbytes((LANES, SSM_INNER), BF16) + _nbytes((LANES, SSM_INNER), F32))
            + _nbytes((2 * q, SSM_CONV_DIM), xbc.dtype) + _nbytes((SSM_GROUPS, SSM_STATE, group_w), F32)
            + _nbytes((q, SSM_INNER), F32) + 10 * _nbytes((q, SSM_CONV_DIM), F32))
    return pl.pallas_call(
        _ssd_kernel,
        grid=(b, s // q),
        in_specs=[blk(SSM_CONV_DIM),
                  pl.BlockSpec((1, SSM_CONV_TAIL, SSM_CONV_DIM),
                               lambda bi, c: (bi, jnp.maximum(c * (q // SSM_CONV_TAIL) - 1, 0), 0)),
                  blk(SSM_INNER), blk(LANES)] + [of_layer(a) for a in params]
        + [full((LANES, SSM_INNER)), full((LANES, SSM_INNER))],
        out_specs=blk(SSM_INNER),
        out_shape=jax.ShapeDtypeStruct((b, s, SSM_INNER), BF16),
        scratch_shapes=[pltpu.VMEM((SSM_GROUPS, SSM_STATE, group_w), F32),
                        pltpu.VMEM((q, SSM_INNER), F32)],
        compiler_params=_compiler_params(("parallel", "arbitrary"), vmem),
        name="ssd",
    )(xbc, xbc, zs, small, *params, *expands)


def _t5_bucket_of(n):
    steps_per_octave = (REL_BUCKETS - T5_EXACT) / math.log2(REL_MAX_DIST / T5_EXACT)
    x = jnp.maximum(n, T5_EXACT).astype(F32) * (1.0 / T5_EXACT)
    large = T5_EXACT + jnp.floor(jnp.log2(x) * steps_per_octave).astype(I32)
    return jnp.where(n <= T5_EXACT, n, jnp.minimum(large, REL_BUCKETS - 1))


def _dsa_kernel(qmin_ref, kmax_ref, consec_ref, tblt_ref, toe_ref, qcw_ref, qiw_ref, kc_ref, ki_ref, v1_ref, wi_ref, posk_ref, posq_ref, gate_ref, o_ref,
                key_ref, hi_ref, lo_ref, s0_ref, s1_ref, bias_ref, acc_ref, ot_ref, m_ref, l_ref, tie_ref, *, n_sel, tq):
    tk = DSA_KEY_TILE
    qi = pl.program_id(1)
    n_tiles = (qi + 1) * (tq // tk)
    krow = lax.broadcasted_iota(I32, (tk, tq), 0)
    qcol = lax.broadcasted_iota(I32, (tk, tq), 1)
    w = wi_ref[0]
    posq = posq_ref[0]

    def tile_start(kt):
        return pl.multiple_of(kt * tk, tk)

    def causal_of(ks):
        return (ks + krow) <= (qi * tq + qcol)

    def head_lanes(h):
        return slice(h * tq, (h + 1) * tq)

    staged = (s0_ref, s1_ref)

    def stage(k_ref, q_ref, kt, dst_ref):
        dst_ref[...] = jnp.dot(k_ref[0, pl.ds(tile_start(kt), tk), :], q_ref[0, 0], preferred_element_type=F32)

    def tile_pairs(tile_fn, peel_last):
        def pair(pi, last_pair):
            for half in range(2):
                tile_fn(2 * pi + half, staged[half], staged[1 - half], last_pair, last_pair and half == 1)

        def body(pi, carry):
            pair(pi, False)
            return carry

        if peel_last:
            lax.fori_loop(0, n_tiles // 2 - 1, body, 0)
            pair(n_tiles // 2 - 1, True)
        else:
            lax.fori_loop(0, n_tiles // 2, body, 0)

    stage(ki_ref, qiw_ref, 0, staged[0])

    def score_tile(kt, cur_ref, next_ref, last_pair, last_tile):
        ks = tile_start(kt)
        if last_tile:
            stage(kc_ref, qcw_ref, 0, next_ref)
        else:
            stage(ki_ref, qiw_ref, kt + 1, next_ref)
        sc = jnp.zeros((tk, tq), F32)
        for h in range(IDX_HEADS):
            sc = sc + w[h:h + 1, :] * jnp.maximum(cur_ref[:, head_lanes(h)], 0.0)
        sc = jnp.where(sc == 0.0, 0.0, sc)
        bits = lax.bitcast_convert_type(sc, I32)
        key = bits ^ ((bits >> 31) & 0x7FFFFFFF)
        if last_pair:
            key = jnp.where(causal_of(ks), key, INT_MIN)
        key_ref[pl.ds(ks, tk), :] = key
        hi_ref[pl.ds(ks, tk), :] = (key >> 16).astype(I16)
        lo_ref[pl.ds(ks, tk), :] = ((key & 0xFFFF) - HALF16).astype(I16)

    tile_pairs(score_tile, peel_last=True)

    def count16(half_ref, pred):
        def body(pi, cnt):
            for kt in (2 * pi, 2 * pi + 1):
                cnt = cnt + jnp.where(pred(half_ref[pl.ds(tile_start(kt), tk), :]), jnp.int16(1), jnp.int16(0))
            return cnt
        cnt = lax.fori_loop(0, n_tiles // 2, body, jnp.zeros((tk, tq), I16))
        rows = 2 * SUBLANES
        parts = [cnt[r:r + rows, :] for r in range(0, tk, rows)]
        while len(parts) > 1:
            parts = [a + b for a, b in zip(parts[0::2], parts[1::2], strict=True)]
        return jnp.sum(parts[0].astype(I32), axis=0, keepdims=True)

    def nth_largest16(half_ref, rank):
        def bit_step(i, cand):
            trial = cand | lax.shift_left(jnp.int32(1), 15 - i)
            thr = (trial - HALF16).astype(I16)
            tot = count16(half_ref, lambda v: v >= thr)
            return jnp.where(tot >= rank, trial, cand)
        return lax.fori_loop(0, 16, bit_step, jnp.zeros((1, tq), I32)) - HALF16

    tau_hi = nth_largest16(hi_ref, n_sel)
    tau_hi16 = tau_hi.astype(I16)
    above_hi = count16(hi_ref, lambda v: v > tau_hi16)

    def mask_low(kt, carry):
        rows = pl.ds(tile_start(kt), tk)
        lo_ref[rows, :] = jnp.where(hi_ref[rows, :] == tau_hi16, lo_ref[rows, :], jnp.int16(-HALF16))
        return carry

    lax.fori_loop(0, n_tiles, mask_low, 0)
    tau_lo = nth_largest16(lo_ref, n_sel - above_hi)
    tau_lo16 = tau_lo.astype(I16)
    tau = lax.shift_left(tau_hi, 16) | (tau_lo + HALF16)
    n_gt = above_hi + count16(lo_ref, lambda v: v > tau_lo16)
    n_ge = above_hi + count16(lo_ref, lambda v: v >= tau_lo16)
    need = n_sel - n_gt

    m_ref[...] = jnp.full(m_ref.shape, NEG_BIG, F32)
    l_ref[...] = jnp.zeros(l_ref.shape, F32)
    acc_ref[...] = jnp.zeros(acc_ref.shape, F32)
    tie_ref[...] = jnp.zeros(tie_ref.shape, I32)
    lower_incl = jnp.where(lax.broadcasted_iota(I32, (tk, tk), 0) >= lax.broadcasted_iota(I32, (tk, tk), 1),
                           1.0, 0.0).astype(BF16)

    splits =jnp.where((n_ge > n_sel) & (tau != INT_MIN), 1.0, 0.0)
    any_split = jnp.max(splits) > 0.0

    def attend_tile(kt, cur_ref, next_ref, general_bias, tie_split):
        ks = tile_start(kt)
        key = key_ref[pl.ds(ks, tk), :]
        if tie_split:
            eq = key == tau
            prefix = jnp.dot(lower_incl, jnp.where(eq, 1.0, 0.0).astype(BF16), preferred_element_type=F32)
            tie_before = tie_ref[0:1, :]
            tie_rank = tie_before + prefix.astype(I32)
            sel = ((key > tau) | (eq & (tie_rank <= need))) & causal_of(ks)
            tie_ref[0:1, :] = tie_before + prefix[tk - 1:tk, :].astype(I32)
        else:
            sel = (key >= tau) & causal_of(ks)

        if general_bias == "toeplitz":
            offset = [(qmin_ref[bi, qi] + c * LANES) - (kmax_ref[bi, kt] - (tk - 1)) for c in range(tq // LANES)]
            which = [jnp.where(off == 0, 0, jnp.where(off == tk, 1, 2)) for off in offset]
        elif general_bias == "gather":
            posk = posk_ref[0, pl.ds(ks, tk), :]
            bucket = _t5_bucket_of(jnp.maximum(posq - posk, 0))
            for hp in range(DSA_HEADS // 2):
                table = jnp.broadcast_to(tblt_ref[hp:hp + 1, :], (tk, LANES))
                for c in range(tq // LANES):
                    both = jnp.take_along_axis(table, bucket[:, c * LANES:(c + 1) * LANES], axis=1)
                    lo, hi = 2 * hp * tq + c * LANES, (2 * hp + 1) * tq + c * LANES
                    bias_ref[:, lo:lo + LANES] = lax.bitcast_convert_type(both & jnp.int32(-65536), F32)
                    bias_ref[:, hi:hi + LANES] = lax.bitcast_convert_type(lax.shift_left(both, 16), F32)
        stage(kc_ref, qcw_ref, jnp.minimum(kt + 1, n_tiles - 1), next_ref)
        m_all = m_ref[0:1, :]
        l_all = l_ref[0:1, :]
        v1 = v1_ref[0, :, pl.ds(ks, tk)]
        for h in range(DSA_HEADS):
            hl = head_lanes(h)
            s = cur_ref[:, hl]
            if general_bias == "toeplitz":
                s = s + jnp.concatenate([toe_ref[which[c], h] for c in range(tq // LANES)], axis=1)
            elif general_bias == "gather":
                s = s + bias_ref[:, hl]
            s = jnp.where(sel, s, NEG_BIG)
            m_old = m_all[:, hl]
            m_new = jnp.maximum(m_old, jnp.max(s, axis=0, keepdims=True))
            m_ref[0:1, hl] = m_new
            alpha = jnp.exp2(m_old - m_new)
            pv = jnp.dot(v1, jnp.exp2(s - m_new).astype(BF16), preferred_element_type=F32)
            acc_ref[:, hl] = alpha * acc_ref[:, hl] + pv[0:DSA_HEAD_DIM, :]
            l_ref[0:1, hl] = alpha * l_all[:, hl] + pv[DSA_HEAD_DIM:DSA_HEAD_DIM + 1, :]

    bi = pl.program_id(0)

    def attend(kt, cur_ref, next_ref, last_pair, last_tile):
        far = (qmin_ref[bi, qi] - kmax_ref[bi, kt]) >= T5_FAR
        no_split = jnp.logical_not(any_split)
        near = jnp.logical_and(jnp.logical_not(far), no_split)
        consecutive = consec_ref[bi] == 1

        @pl.when(jnp.logical_and(far, no_split))
        def _():
            attend_tile(kt, cur_ref, next_ref, general_bias=None, tie_split=False)

        @pl.when(jnp.logical_and(near, consecutive))
        def _():
            attend_tile(kt, cur_ref, next_ref, general_bias="toeplitz", tie_split=False)

        @pl.when(jnp.logical_and(near, jnp.logical_not(consecutive)))
        def _():
            attend_tile(kt, cur_ref, next_ref, general_bias="gather", tie_split=False)

        @pl.when(any_split)
        def _():
            attend_tile(kt, cur_ref, next_ref, general_bias="gather", tie_split=True)

    tile_pairs(attend, peel_last=False)

    for h in range(DSA_HEADS):
        hl = head_lanes(h)
        ot_ref[h * DSA_HEAD_DIM:(h + 1) * DSA_HEAD_DIM, :] = acc_ref[:, hl] / l_ref[0:1, hl]
    o_ref[0] = (ot_ref[...].T * gate_ref[0]).astype(o_ref.dtype)


DSA_TOEPLITZ_OFFSETS = (0, DSA_KEY_TILE)


def _bias_toeplitz_kernel(tbl_ref, o_ref):
    j = pl.program_id(0)
    tk = DSA_KEY_TILE
    diff = lax.broadcasted_iota(I32, (tk, LANES), 1) - lax.broadcasted_iota(I32, (tk, LANES), 0)
    for slot, offset in enumerate(DSA_TOEPLITZ_OFFSETS):
        @pl.when(j == slot)
        def _(offset=offset):
            bucket = _t5_bucket_of(jnp.maximum(diff + offset, 0))
            for h in range(DSA_HEADS):
                table = jnp.broadcast_to(tbl_ref[h:h + 1, :], (tk, LANES))
                o_ref[0, h] = jnp.take_along_axis(table, bucket, axis=1)

    @pl.when(j == len(DSA_TOEPLITZ_OFFSETS))
    def _():
        o_ref[...] = jnp.zeros(o_ref.shape, F32)


def _bias_toeplitz(tbl_rows):
    n = len(DSA_TOEPLITZ_OFFSETS) + 1
    return pl.pallas_call(
        _bias_toeplitz_kernel,
        grid=(n,),
        in_specs=[pl.BlockSpec((DSA_HEADS, LANES), lambda j: (0, 0))],
        out_specs=pl.BlockSpec((1, DSA_HEADS, DSA_KEY_TILE, LANES), lambda j: (j, 0, 0, 0)),
        out_shape=jax.ShapeDtypeStruct((n, DSA_HEADS, DSA_KEY_TILE, LANES), F32),
        compiler_params=_compiler_params(("parallel",), 0),
        name="bias_toeplitz",
    )(tbl_rows)


def _dsa_bias_tables(rel_bias):
    tbl = rel_bias.astype(F32) * LOG2E
    tbl = tbl - tbl[REL_BUCKETS - 1:REL_BUCKETS, :]
    rows = jnp.pad(tbl.T, ((0, 0), (0, LANES - REL_BUCKETS)))
    bits = lax.bitcast_convert_type(tbl.T.astype(jnp.bfloat16), jnp.uint16).astype(jnp.uint32)
    packed = lax.bitcast_convert_type((bits[0::2] << 16) | bits[1::2], I32)
    packed = jnp.pad(packed, ((0, SUBLANES - DSA_HEADS // 2), (0, LANES - REL_BUCKETS)))
    return packed, _bias_toeplitz(rows)


def _dsa(bias_tables, qcw, qiw, kc, ki, v1, wi, pos_col, pos_row, gate, tq):
    b, s, _ = kc.shape
    n_sel = min(TOPK_MAX, s // 4)
    vw_rows = v1.shape[1]
    hw = DSA_HEADS * tq
    tbl_t, toeplitz = bias_tables
    q_min = jnp.min(pos_row.reshape(b, s // tq, tq), axis=-1)
    k_max = jnp.max(pos_row.reshape(b, s // DSA_KEY_TILE, DSA_KEY_TILE), axis=-1)
    consecutive = jnp.all(pos_row[:, 0, 1:] - pos_row[:, 0, :-1] == 1, axis=-1).astype(I32)
    tk = DSA_KEY_TILE
    vmem = (2 * (2 * _nbytes((DSA_HEAD_DIM, hw), BF16) + 2 * _nbytes((s, LANES), BF16) + _nbytes((vw_rows, s), F32)
                 + _nbytes((vw_rows, tq), F32) + _nbytes((s, LANES), I32) + _nbytes((tq, DSA_WIDTH), F32)
                 + _nbytes((tq, DSA_WIDTH), BF16)) + _nbytes((s, tq), I32) + 2 * _nbytes((tk, hw), F32)
            + _nbytes((tk, hw), BF16) + _nbytes((DSA_HEAD_DIM, hw), F32) + _nbytes((DSA_WIDTH, tq), F32)
            + 32 * _nbytes((tk, tq), F32))
    return pl.pallas_call(
        functools.partial(_dsa_kernel, n_sel=n_sel, tq=tq),
        grid=(b, s // tq),
        in_specs=[pl.BlockSpec(memory_space=pltpu.SMEM), pl.BlockSpec(memory_space=pltpu.SMEM),
                  pl.BlockSpec(memory_space=pltpu.SMEM),
                  pl.BlockSpec((SUBLANES, LANES), lambda bi, i: (0, 0)),
                  pl.BlockSpec(toeplitz.shape, lambda bi, i: (0, 0, 0, 0)),
                  pl.BlockSpec((1, 1, DSA_HEAD_DIM, hw), lambda bi, i: (bi, i, 0, 0)),
                  pl.BlockSpec((1, 1, IDX_DIM, hw), lambda bi, i: (bi, i, 0, 0)),
                  pl.BlockSpec((1, s, DSA_HEAD_DIM), lambda bi, i: (bi, 0, 0)),
                  pl.BlockSpec((1, s, IDX_DIM), lambda bi, i: (bi, 0, 0)),
                  pl.BlockSpec((1, vw_rows, s), lambda bi, i: (bi, 0, 0)),
                  pl.BlockSpec((1, IDX_HEADS, tq), lambda bi, i: (bi, 0, i)),
                  pl.BlockSpec((1, s, 1), lambda bi, i: (bi, 0, 0)),
                  pl.BlockSpec((1, 1, tq), lambda bi, i: (bi, 0, i)),
                  pl.BlockSpec((1, tq, DSA_WIDTH), lambda bi, i: (bi, i, 0))],
        out_specs=pl.BlockSpec((1, tq, DSA_WIDTH), lambda bi, i: (bi, i, 0)),
        out_shape=jax.ShapeDtypeStruct((b, s, DSA_WIDTH), BF16),
        scratch_shapes=[pltpu.VMEM((s, tq), I32),
                        pltpu.VMEM((s, tq), I16), pltpu.VMEM((s, tq), I16),
                        pltpu.VMEM((tk, hw), F32), pltpu.VMEM((tk, hw), F32),
                        pltpu.VMEM((tk, hw), F32),
                        pltpu.VMEM((DSA_HEAD_DIM, hw), F32),
                        pltpu.VMEM((DSA_WIDTH, tq), F32),
                        pltpu.VMEM((SUBLANES, hw), F32), pltpu.VMEM((SUBLANES, hw), F32),
                        pltpu.VMEM((SUBLANES, tq), I32)],
        compiler_params=_compiler_params(("parallel", "arbitrary"), vmem),
        name="dsa",
    )(q_min, k_max, consecutive, tbl_t, toeplitz, qcw, qiw, kc, ki, v1, wi, pos_col, pos_row, gate)


def _merge_kernel(oa_ref, ob_ref, oc_ref, g_ref, x_ref, p_ref, wa_ref, wb_ref, wc_ref, wo_ref, wpg_ref, wple_ref,
                  gn_ref, o_ref, *, final):
    d = D_MODEL
    ya = jnp.dot(oa_ref[0], wa_ref[0], preferred_element_type=F32)
    yb = jnp.dot(ob_ref[0], wb_ref[0], preferred_element_type=F32)
    yc = jnp.dot(oc_ref[0], wc_ref[0], preferred_element_type=F32)
    merged = g_ref[0, :, 0:d] * ya + g_ref[0, :, d:2 * d] * yb + g_ref[0, :, 2 * d:3 * d] * yc
    x1 = x_ref[0] + jnp.dot(merged.astype(BF16), wo_ref[0], preferred_element_type=F32)
    ple_gate = _sigmoid(jnp.dot(x1.astype(BF16), wpg_ref[0], preferred_element_type=F32))
    x2 = x1 + ple_gate * jnp.dot(p_ref[0, 0].astype(BF16), wple_ref[0], preferred_element_type=F32)
    o_ref[0] = _rms(x2, gn_ref[...]) if final else x2


def _merge(oa, ob, oc, gates, x, p, layer, wa, wb, wc, wo, wpg, wple, final_gain, final, tm=512):
    b, s, d = x.shape
    full = lambda a: pl.BlockSpec((1,) + a.shape[1:], lambda bi, i: (layer, 0, 0), pipeline_mode=pl.Buffered(1))
    row = lambda a: pl.BlockSpec((1, tm, a.shape[2]), lambda bi, i: (bi, i, 0))
    weights = (wa, wb, wc, wo, wpg, wple)
    acts = (oa, ob, oc, gates, x)
    vmem = (sum(_nbytes(a.shape[1:], a.dtype) for a in weights)
            + 2 * (sum(_nbytes((tm, a.shape[2]), a.dtype) for a in acts) + _nbytes((tm, p.shape[3]), p.dtype)
                   + 2 * _nbytes((tm, d), F32)) + 8 * _nbytes((tm, d), F32))
    return pl.pallas_call(
        functools.partial(_merge_kernel, final=final),
        grid=(b, s // tm),
        in_specs=[row(a) for a in acts] + [pl.BlockSpec((1, 1, tm, p.shape[3]), lambda bi, i: (layer, bi, i, 0))]
        + [full(a) for a in weights] + [pl.BlockSpec((1, d), lambda bi, i: (0, 0))],
        out_specs=pl.BlockSpec((1, tm, d), lambda bi, i: (bi, i, 0)),
        out_shape=jax.ShapeDtypeStruct((b, s, d), F32),
        compiler_params=_compiler_params(("parallel", "parallel"), vmem),
        name="merge",
    )(*acts, p, *weights, final_gain.astype(F32).reshape(1, d))


def _rotate_half_cols(w):
    half = w.shape[-1] // 2
    return jnp.concatenate([-w[..., half:], w[..., :half]], axis=-1)


def _prep_weights(w_in, w_uq, w_ukv):
    depth = w_in.shape[0]
    seg = lambda name: w_in[:, :, SPLIT[name][0]:SPLIT[name][1]]
    z = lambda *shape: jnp.zeros((depth,) + shape, w_in.dtype)
    d = D_MODEL
    kr = seg("k_rope")
    rope_lo = MLA_NOPE
    rope_pad = MLA_QK_PAD - MLA_NOPE - MLA_ROPE
    w = {}
    w["mla_in"] = jnp.concatenate(
        [seg("c_q"), seg("c_kv"), z(d, rope_lo), kr, z(d, rope_pad), z(d, rope_lo), _rotate_half_cols(kr), z(d, rope_pad)], axis=-1)
    w["gate_a"] = seg("gate_a")
    w["z"] = seg("z")
    w["xbc"] = seg("xbc")
    w["small"] = jnp.concatenate([seg("dt"), z(d, LANES - SSM_HEADS)], axis=-1)
    w["k_c"] = seg("k_c")
    w["k_idx"] = seg("k_idx")
    w["gate_c"] = seg("gate_c")
    w["merge"] = seg("merge")
    w["q_c_t"] = jnp.swapaxes(seg("q_c") * (DSA_HEAD_DIM ** -0.5 * LOG2E), 1, 2)
    w["q_idx_t"] = jnp.swapaxes(seg("q_idx"), 1, 2)
    w["vw_t"] = jnp.swapaxes(jnp.concatenate([seg("v_c"), seg("w_idx")], axis=-1), 1, 2)
    w = {
        "proj": jnp.concatenate([w["mla_in"]] + [w[seg[0]] for seg in PROJ_SEGMENTS], axis=-1).astype(BF16),
        "proj_t": jnp.concatenate([w["q_c_t"], w["q_idx_t"], w["vw_t"]], axis=1).astype(BF16),
    }

    uq =w_uq.reshape(depth, MLA_Q_LORA, MLA_HEADS, MLA_NOPE + MLA_ROPE)
    nope, rope = uq[..., :MLA_NOPE], uq[..., MLA_NOPE:]
    zq = lambda width: jnp.zeros((depth, MLA_Q_LORA, MLA_HEADS, width), w_uq.dtype)
    hq = MLA_HEADS * MLA_QK_PAD
    uq1 = jnp.concatenate([nope, rope, zq(rope_pad)], axis=-1).reshape(depth, MLA_Q_LORA, hq)
    uq2 = _rotate_half_cols(rope).reshape(depth, MLA_Q_LORA, MLA_HEADS * MLA_ROPE)
    w["uq1_t"] = jnp.swapaxes(uq1, 1, 2).astype(BF16)
    w["uq2_t"] = jnp.swapaxes(uq2, 1, 2).astype(BF16)
    ukv = w_ukv.reshape(depth, MLA_KV_LORA, MLA_HEADS, MLA_NOPE + MLA_V)
    zk = jnp.zeros((depth, MLA_KV_LORA, MLA_HEADS, MLA_QK_PAD - MLA_NOPE), w_ukv.dtype)
    w["uk"] = jnp.concatenate([ukv[..., :MLA_NOPE], zk], axis=-1).reshape(depth, MLA_KV_LORA, hq).astype(BF16)
    w["uv_t"] = jnp.swapaxes(ukv[..., MLA_NOPE:].reshape(depth, MLA_KV_LORA, MLA_WIDTH), 1, 2).astype(BF16)
    return w


def _rope_multipliers(positions):
    b, s = positions.shape
    inv_freq = 1.0 / (ROPE_THETA ** (jnp.arange(0, MLA_ROPE, 2, dtype=F32) / MLA_ROPE))
    ang = positions.astype(F32)[..., None] * inv_freq
    cos, sin = jnp.cos(ang), jnp.sin(ang)
    pad = MLA_QK_PAD - MLA_NOPE - MLA_ROPE
    m1 = jnp.concatenate([jnp.ones((b, s, MLA_NOPE), F32), cos, cos, jnp.zeros((b, s, pad), F32)], axis=-1)
    m2 = jnp.concatenate([jnp.zeros((b, s, MLA_NOPE), F32), sin, sin, jnp.zeros((b, s, pad), F32)], axis=-1)
    return m1, m2, jnp.swapaxes(m1, 1, 2), jnp.swapaxes(m2, 1, 2)


def kernel(x, p, positions, norm_g, w_in, mla_q_norm, w_uq, mla_kv_norm, w_ukv, conv_w, conv_b, dt_bias, a_log, d_skip, ssm_norm, w_br_a, w_br_b, w_br_c, w_out, rel_bias, w_ple, w_ple_gate, final_norm):
    b, s, d = x.shape
    depth = w_in.shape[0]
    w = _prep_weights(w_in, w_uq, w_ukv)
    m1, m2, m1t, m2t = _rope_multipliers(positions)
    pos_col = positions.astype(I32).reshape(b, s, 1)
    pos_row = positions.astype(I32).reshape(b, 1, s)
    wbr_a, wbr_b, wbr_c = w_br_a.astype(BF16), w_br_b.astype(BF16), w_br_c.astype(BF16)
    wo, wpg, wple = w_out.astype(BF16), w_ple_gate.astype(BF16), w_ple.astype(BF16)

    q_norm = mla_q_norm.astype(F32).reshape(depth, 1, -1)
    kv_norm = mla_kv_norm.astype(F32).reshape(depth, 1, -1)

    gains = norm_g.astype(F32).reshape(depth, 1, d)
    bias_tables = _dsa_bias_tables(rel_bias)
    ssd_params, ssd_expands = _ssd_params(conv_w, conv_b, dt_bias, a_log, d_skip, ssm_norm)
    x = x.astype(F32)
    for i in range(depth):
        pr = _proj_all(x, i, gains, w["proj"], w["proj_t"], (m1, m2, m1t, m2t), q_norm, kv_norm,
                       w["uq1_t"], w["uq2_t"], w["uk"], w["uv_t"], DSA_Q_BLOCK)
        o_a = _mla_attn(pr["mla_q_t"], pr["mla_k"], pr["mla_v_t1"], pr["gate_a"])
        o_b = _ssd(pr["xbc"], pr["z"], pr["small"], i, ssd_params, ssd_expands)
        o_c = _dsa(bias_tables, pr["q_c_wide"], pr["q_idx_wide"], pr["k_c"], pr["k_idx"], pr["v_t1"], pr["w_idx_t"], pos_col, pos_row,
                   pr["gate_c"], DSA_Q_BLOCK)
        x = _merge(o_a, o_b, o_c, pr["merge"], x, p, i, wbr_a, wbr_b, wbr_c, wo, wpg, wple, final_norm,
                   final=i == depth - 1)
    return x
```

```python
import functools
import math

import jax
import jax.numpy as jnp
from jax import lax
from jax.experimental import pallas as pl
from jax.experimental.pallas import tpu as pltpu

F32 = jnp.float32
BF16 = jnp.bfloat16
I32 = jnp.int32
I16 = jnp.int16
HALF16 = 1 << 15

D_MODEL = 1024
NORM_EPS = 1e-6

MLA_HEADS = 8
MLA_NOPE = 64
MLA_ROPE = 32
MLA_V = 64
MLA_Q_LORA = 384
MLA_KV_LORA = 256
MLA_WIDTH = MLA_HEADS * MLA_V
ROPE_THETA = 10000.0
MLA_QK_PAD = 128
LOG2E = math.log2(math.e)
MLA_Q_SCALE = (MLA_NOPE + MLA_ROPE) ** -0.5 * LOG2E

SSM_HEADS = 16
SSM_HEAD_DIM = 64
SSM_INNER = SSM_HEADS * SSM_HEAD_DIM
SSM_GROUPS = 2
SSM_STATE = 128
SSM_CONV = 4
SSM_CHUNK = 128
SSM_CONV_DIM = SSM_INNER + 2 * SSM_GROUPS * SSM_STATE
SSM_HEADS_PER_GROUP = SSM_HEADS // SSM_GROUPS
ONES_ROWS = 16
MLA_V1_ROWS = MLA_V + ONES_ROWS
SSM_CONV_TAIL = 16

DSA_HEADS = 8
DSA_HEAD_DIM = 64
DSA_WIDTH = DSA_HEADS * DSA_HEAD_DIM
IDX_HEADS = 8
IDX_DIM = 64
TOPK_MAX = 256
DSA_KEY_TILE = 128
DSA_Q_BLOCK = 256

REL_BUCKETS = 32
REL_MAX_DIST = 128
N_BRANCHES = 3

LANES = 128
SUBLANES = 8
VMEM_LIMIT_CAP = 56 * 1024 * 1024
VMEM_LIMIT_FLOOR = 32 * 1024 * 1024
INT_MIN = -(2 ** 31)
NEG_BIG = -1e30

SPLIT_SIZES = (
    MLA_Q_LORA, MLA_KV_LORA, MLA_ROPE, MLA_WIDTH, SSM_INNER, SSM_CONV_DIM, SSM_HEADS, DSA_WIDTH,
    DSA_HEAD_DIM, DSA_HEAD_DIM, IDX_HEADS * IDX_DIM, IDX_DIM, IDX_HEADS, DSA_WIDTH,
    N_BRANCHES * D_MODEL,
)
SPLIT_NAMES = ("c_q", "c_kv", "k_rope", "gate_a", "z", "xbc", "dt", "q_c", "k_c", "v_c", "q_idx",
               "k_idx", "w_idx", "gate_c", "merge")


def _split_bounds():
    out, off = {}, 0
    for name, size in zip(SPLIT_NAMES, SPLIT_SIZES, strict=True):
        out[name] = (off, off + size)
        off += size
    return out


SPLIT = _split_bounds()


def _t5_large_thresholds():
    exact = REL_BUCKETS // 2
    thr = []
    for j in range(1, REL_BUCKETS - exact):
        thr.append(int(math.ceil(exact * (REL_MAX_DIST / exact) ** (j / (REL_BUCKETS - exact)) - 1e-9)))
    return tuple(thr)


T5_EXACT = REL_BUCKETS // 2
T5_LARGE_THR = _t5_large_thresholds()
T5_FAR = T5_LARGE_THR[-1]


def _compiler_params(semantics, vmem_bytes):
    limit = int(min(VMEM_LIMIT_CAP, max(VMEM_LIMIT_FLOOR, vmem_bytes)))
    return pltpu.CompilerParams(dimension_semantics=semantics, vmem_limit_bytes=limit)


def _nbytes(shape, dtype):
    return math.prod(shape) * jnp.dtype(dtype).itemsize


def _sigmoid(x):
    return 1.0 / (1.0 + jnp.exp(-x))


def _rms(x, g):
    return x * lax.rsqrt(jnp.mean(x * x, axis=-1, keepdims=True) + NORM_EPS) * g


MLA_IN_WIDTH = MLA_Q_LORA + MLA_KV_LORA + 2 * LANES
PROJ_SEGMENTS = (
    ("small", LANES, None, F32),
    ("gate_a", MLA_WIDTH, "silu", BF16),
    ("gate_c", DSA_WIDTH, "silu", BF16),
    ("z", SSM_INNER, "silu", BF16),
    ("xbc", SSM_CONV_DIM, None, BF16),
    ("merge", N_BRANCHES * D_MODEL, "sigmoid", BF16),
    ("k_c", DSA_HEAD_DIM, None, BF16),
    ("k_idx", IDX_DIM, None, BF16),
)
PROJ_CHUNK = 512
PROJ_T_ROWS = (DSA_WIDTH, IDX_HEADS * IDX_DIM, DSA_HEAD_DIM + IDX_HEADS)


def _proj_all_kernel(x_ref, g_ref, w_ref, wt_ref, m1_ref, m2_ref, m1t_ref, m2t_ref, qn_ref, kvn_ref, wq1t_ref, wq2t_ref,
                     wk_ref, wvt_ref, *out_refs):
    h = _rms(x_ref[0], g_ref[0]).astype(BF16)
    n_seg = len(PROJ_SEGMENTS)
    a = jnp.dot(h, w_ref[0, :, 0:MLA_IN_WIDTH], preferred_element_type=F32)
    _mla_prep(a, m1_ref, m2_ref, m1t_ref, m2t_ref, qn_ref, kvn_ref, wq1t_ref, wq2t_ref, wk_ref, wvt_ref, *out_refs[n_seg + 4:])
    off = MLA_IN_WIDTH
    for (_, width, act, _), o_ref in zip(PROJ_SEGMENTS, out_refs[:n_seg], strict=True):
        for c in range(0, width, PROJ_CHUNK):
            cw = min(PROJ_CHUNK, width - c)
            y = jnp.dot(h, w_ref[0, :, off + c:off + c + cw], preferred_element_type=F32)
            if act == "silu":
                y = y * _sigmoid(y)
            elif act == "sigmoid":
                y = _sigmoid(y)
            o_ref[0, :, c:c + cw] = y.astype(o_ref.dtype)
        off += width
    qcw_ref, qiw_ref, v1_ref, wi_ref = out_refs[n_seg:n_seg + 4]
    nt = (((1,), (1,)), ((), ()))
    tq = h.shape[0]
    row = 0
    for o_ref, heads, dim in ((qcw_ref, DSA_HEADS, DSA_HEAD_DIM), (qiw_ref, IDX_HEADS, IDX_DIM)):
        y = lax.dot_general(wt_ref[0, row:row + heads * dim, :], h, nt, preferred_element_type=F32)
        for hd in range(heads):
            o_ref[0, 0, :, hd * tq:(hd + 1) * tq] = y[hd * dim:(hd + 1) * dim, :].astype(o_ref.dtype)
        row += heads * dim
    vw = lax.dot_general(wt_ref[0, row:row + PROJ_T_ROWS[2], :], h, nt, preferred_element_type=F32)
    v1_ref[0, 0:DSA_HEAD_DIM, :] = vw[0:DSA_HEAD_DIM, :].astype(v1_ref.dtype)
    v1_ref[0, DSA_HEAD_DIM:, :] = jnp.ones((ONES_ROWS, tq), v1_ref.dtype)
    wi_ref[0] = vw[DSA_HEAD_DIM:, :]


def _proj_all(x, layer, gains, w, wt, rope, qn, kvn, wq1t, wq2t, wk, wvt, tq):
    b, s, k = x.shape
    n_all = MLA_IN_WIDTH + sum(seg[1] for seg in PROJ_SEGMENTS)
    hq = MLA_HEADS * MLA_QK_PAD
    v1_rows = DSA_HEAD_DIM + ONES_ROWS
    out_shapes = [jax.ShapeDtypeStruct((b, s, width), dt) for _, width, _, dt in PROJ_SEGMENTS]
    out_specs = [pl.BlockSpec((1, tq, width), lambda bi, i: (bi, i, 0)) for _, width, _, _ in PROJ_SEGMENTS]
    for heads, dim in ((DSA_HEADS, DSA_HEAD_DIM), (IDX_HEADS, IDX_DIM)):
        out_shapes.append(jax.ShapeDtypeStruct((b, s // tq, dim, heads * tq), BF16))
        out_specs.append(pl.BlockSpec((1, 1, dim, heads * tq), lambda bi, i: (bi, i, 0, 0)))
    for rows, dt in ((v1_rows, BF16), (IDX_HEADS, F32)):
        out_shapes.append(jax.ShapeDtypeStruct((b, rows, s), dt))
        out_specs.append(pl.BlockSpec((1, rows, tq), lambda bi, i: (bi, 0, i)))
    for rows in (hq, None, MLA_HEADS * MLA_V1_ROWS):
        if rows is None:
            out_shapes.append(jax.ShapeDtypeStruct((b, s, hq), BF16))
            out_specs.append(pl.BlockSpec((1, tq, hq), lambda bi, i: (bi, i, 0)))
        else:
            out_shapes.append(jax.ShapeDtypeStruct((b, rows, s), BF16))
            out_specs.append(pl.BlockSpec((1, rows, tq), lambda bi, i: (bi, 0, i)))
    resident = pl.Buffered(1)
    weights = (w, wt, qn, kvn, wq1t, wq2t, wk, wvt)
    layer_spec = lambda a: pl.BlockSpec((1,) + a.shape[1:], lambda bi, i: (layer,) + (0,) * (a.ndim - 1), pipeline_mode=resident)
    m1, m2, m1t, m2t = rope
    vmem = (sum(_nbytes(a.shape[1:], a.dtype) for a in weights) + 4 * _nbytes((tq, k), F32) + 8 * _nbytes((tq, LANES), F32)
            + 2 * sum(_nbytes((tq, width), dt) for _, width, _, dt in PROJ_SEGMENTS)
            + 2 * (2 * _nbytes((DSA_HEAD_DIM, DSA_HEADS * tq), BF16) + _nbytes((v1_rows + IDX_HEADS, tq), F32))
            + 2 * (2 * _nbytes((tq, hq), BF16) + _nbytes((MLA_HEADS * MLA_V1_ROWS, tq), BF16))
            + 8 * _nbytes((tq, PROJ_CHUNK), F32) + 6 * _nbytes((tq, hq), F32))
    outs = pl.pallas_call(
        _proj_all_kernel,
        grid=(b, s // tq),
        in_specs=[pl.BlockSpec((1, tq, k), lambda bi, i: (bi, i, 0)), layer_spec(gains), layer_spec(w), layer_spec(wt),
                  pl.BlockSpec((1, tq, LANES), lambda bi, i: (bi, i, 0)), pl.BlockSpec((1, tq, LANES), lambda bi, i: (bi, i, 0)),
                  pl.BlockSpec((1, LANES, tq), lambda bi, i: (bi, 0, i)), pl.BlockSpec((1, LANES, tq), lambda bi, i: (bi, 0, i))]
        + [layer_spec(a) for a in weights[2:]],
        out_specs=out_specs,
        out_shape=out_shapes,
        compiler_params=_compiler_params(("parallel", "parallel"), vmem),
        name="proj_all",
    )(x, gains, w, wt, m1, m2, m1t, m2t, *weights[2:])
    names = [seg[0] for seg in PROJ_SEGMENTS] + ["q_c_wide", "q_idx_wide", "v_t1", "w_idx_t", "mla_q_t", "mla_k", "mla_v_t1"]
    return dict(zip(names, outs, strict=True))


def _mla_prep(a, m1_ref, m2_ref, m1t_ref, m2t_ref, qn_ref, kvn_ref, wq1t_ref, wq2t_ref, wk_ref, wvt_ref, qt_ref, k_ref, vt_ref):
    c_q = a[:, :MLA_Q_LORA]
    c_kv = a[:, MLA_Q_LORA:MLA_Q_LORA + MLA_KV_LORA]
    kr1 = a[:, MLA_Q_LORA + MLA_KV_LORA:MLA_Q_LORA + MLA_KV_LORA + LANES]
    kr2 = a[:, MLA_Q_LORA + MLA_KV_LORA + LANES:]
    cqn = _rms(c_q, qn_ref[0]).astype(BF16)
    ckvn = _rms(c_kv, kvn_ref[0]).astype(BF16)
    nt = (((1,), (1,)), ((), ()))
    qa_t = lax.dot_general(wq1t_ref[0], cqn, nt, preferred_element_type=F32)
    qb_t = lax.dot_general(wq2t_ref[0], cqn, nt, preferred_element_type=F32)
    kn = jnp.dot(ckvn, wk_ref[0], preferred_element_type=F32)
    kr = kr1 * m1_ref[0] + kr2 * m2_ref[0]
    rope = slice(MLA_NOPE, MLA_NOPE + MLA_ROPE)
    cos_t = m1t_ref[0, rope, :] * MLA_Q_SCALE
    sin_t = m2t_ref[0, rope, :] * MLA_Q_SCALE
    pad_rows = jnp.zeros((MLA_QK_PAD - MLA_NOPE - MLA_ROPE, qa_t.shape[1]), qt_ref.dtype)
    for h in range(MLA_HEADS):
        base = h * MLA_QK_PAD
        sl = slice(base, base + MLA_QK_PAD)
        qt_ref[0, base:base + MLA_NOPE, :] = (qa_t[base:base + MLA_NOPE, :] * MLA_Q_SCALE).astype(qt_ref.dtype)
        qt_ref[0, base + MLA_NOPE:base + MLA_NOPE + MLA_ROPE, :] = (
            qa_t[base + MLA_NOPE:base + MLA_NOPE + MLA_ROPE, :] * cos_t
            + qb_t[h * MLA_ROPE:(h + 1) * MLA_ROPE, :] * sin_t).astype(qt_ref.dtype)
        qt_ref[0, base + MLA_NOPE + MLA_ROPE:base + MLA_QK_PAD, :] = pad_rows
        k_ref[0, :, sl] = (kn[:, sl] + kr).astype(k_ref.dtype)
    v_t = lax.dot_general(wvt_ref[0], ckvn, nt, preferred_element_type=F32)
    ones = jnp.ones((ONES_ROWS, v_t.shape[1]), vt_ref.dtype)
    for h in range(MLA_HEADS):
        base = h * MLA_V1_ROWS
        vt_ref[0, base:base + MLA_V, :] = v_t[h * MLA_V:(h + 1) * MLA_V, :].astype(vt_ref.dtype)
        vt_ref[0, base + MLA_V:base + MLA_V1_ROWS, :] = ones


def _mla_attn_kernel(qt_ref, k_ref, vt_ref, g_ref, o_ref, m_ref, l_ref, acc_ref, s0_ref, s1_ref, *, tq, tk):
    assert tq == 2 * tk
    qi = pl.program_id(1)
    krow = lax.broadcasted_iota(I32, (tk, tq), 0)
    qcol = lax.broadcasted_iota(I32, (tk, tq), 1)
    tiles_per_q = tq // tk
    m_ref[...] = jnp.full(m_ref.shape, -jnp.inf, F32)
    l_ref[...] = jnp.zeros(l_ref.shape, F32)
    acc_ref[...] = jnp.zeros(acc_ref.shape, F32)

    staged = (s0_ref, s1_ref)
    n_tiles = (qi + 1) * tiles_per_q

    def head_lanes(h):
        return slice(h * tq, (h + 1) * tq)

    def stage(j, dst_ref):
        ks = pl.multiple_of(j * tk, tk)
        for h in range(MLA_HEADS):
            k = k_ref[0, pl.ds(ks, tk), h * MLA_QK_PAD:(h + 1) * MLA_QK_PAD]
            dst_ref[:, head_lanes(h)] = jnp.dot(k, qt_ref[0, h * MLA_QK_PAD:(h + 1) * MLA_QK_PAD, :],
                                                preferred_element_type=F32)

    def tile(j, cur_ref, next_ref, diagonal):
        ks = pl.multiple_of(j * tk, tk)
        stage(jnp.minimum(j + 1, n_tiles - 1), next_ref)
        m_all = m_ref[...]
        l_all = l_ref[...]
        if diagonal:
            causal = ks + krow <= qi * tq + qcol
        for h in range(MLA_HEADS):
            s = cur_ref[:, head_lanes(h)]
            if diagonal:
                s = jnp.where(causal, s, -jnp.inf)
            m_old = m_all[h:h + 1, :]
            m_new = jnp.maximum(m_old, jnp.max(s, axis=0, keepdims=True))
            alpha = jnp.exp2(m_old - m_new)
            p = jnp.exp2((s - m_new).astype(BF16))
            m_ref[h:h + 1, :] = m_new
            pv = jnp.dot(vt_ref[0, h * MLA_V1_ROWS:(h + 1) * MLA_V1_ROWS, pl.ds(ks, tk)], p, preferred_element_type=F32)
            hs = slice(h * MLA_V, (h + 1) * MLA_V)
            acc_ref[hs, :] = alpha * acc_ref[hs, :] + pv[0:MLA_V, :]
            l_ref[h:h + 1, :] = alpha * l_all[h:h + 1, :] + pv[MLA_V:MLA_V + 1, :]

    stage(0, staged[0])

    def full_pair(pi, carry):
        for half in range(2):
            tile(2 * pi + half, staged[half], staged[1 - half], diagonal=False)
        return carry

    lax.fori_loop(0, qi, full_pair, 0)
    for half in range(2):
        tile(2 * qi + half, staged[half], staged[1 - half], diagonal=True)
    for h in range(MLA_HEADS):
        hs = slice(h * MLA_V, (h + 1) * MLA_V)
        acc_ref[hs, :] = acc_ref[hs, :] / l_ref[h:h + 1, :]
    o_ref[0] = (acc_ref[...].T * g_ref[0]).astype(o_ref.dtype)


def _mla_attn(qt, k, vt, gate, tq=256, tk=128):
    b, s, hq = k.shape
    vmem = 2 * (_nbytes((hq, tq), BF16) + _nbytes((s, hq), BF16) + _nbytes((MLA_WIDTH, s), BF16)
                + _nbytes((tq, MLA_WIDTH), F32) + _nbytes((tq, MLA_WIDTH), BF16)) + _nbytes((MLA_WIDTH, tq), F32) + 16 * _nbytes((tk, tq), F32)
    return pl.pallas_call(
        functools.partial(_mla_attn_kernel, tq=tq, tk=tk),
        grid=(b, s // tq),
        in_specs=[pl.BlockSpec((1, hq, tq), lambda bi, i: (bi, 0, i)),
                  pl.BlockSpec((1, s, hq), lambda bi, i: (bi, 0, 0)),
                  pl.BlockSpec((1, MLA_HEADS * MLA_V1_ROWS, s), lambda bi, i: (bi, 0, 0)),
                  pl.BlockSpec((1, tq, MLA_WIDTH), lambda bi, i: (bi, i, 0))],
        out_specs=pl.BlockSpec((1, tq, MLA_WIDTH), lambda bi, i: (bi, i, 0)),
        out_shape=jax.ShapeDtypeStruct((b, s, MLA_WIDTH), BF16),
        scratch_shapes=[pltpu.VMEM((MLA_HEADS, tq), F32), pltpu.VMEM((MLA_HEADS, tq), F32),
                        pltpu.VMEM((MLA_WIDTH, tq), F32),
                        pltpu.VMEM((tk, MLA_HEADS * tq), F32), pltpu.VMEM((tk, MLA_HEADS * tq), F32)],
        compiler_params=_compiler_params(("parallel", "arbitrary"), vmem),
        name="mla_attn",
    )(qt, k, vt, gate)


def _ssd_kernel(xbc_ref, xprev_ref, zs_ref, sm_ref, cw_ref, cb_ref, dtb_ref, alog_ref, dskf_ref, nrm_ref, exp16_ref, exp32_ref,
                o_ref, state_ref, y_ref):
    c = pl.program_id(1)
    q = SSM_CHUNK

    @pl.when(c == 0)
    def _():
        state_ref[...] = jnp.zeros(state_ref.shape, F32)

    x_in = xbc_ref[0]
    tail = xprev_ref.shape[1]
    prev = jnp.where(c > 0, xprev_ref[0].astype(F32), 0.0).astype(x_in.dtype)
    win = jnp.concatenate([prev, x_in], axis=0)
    x = x_in.astype(F32)
    wrow = lax.broadcasted_iota(I32, (q, tail + q), 0)
    wcol = lax.broadcasted_iota(I32, (q, tail + q), 1)
    acc = cb_ref[0] + cw_ref[0, SSM_CONV - 1:SSM_CONV, :] * x
    for j in range(1, SSM_CONV):
        pick = jnp.where(wcol == wrow + (tail - j), 1.0, 0.0).astype(win.dtype)
        acc = acc + cw_ref[0, SSM_CONV - 1 - j:SSM_CONV - j, :] * jnp.dot(pick, win, preferred_element_type=F32)
    half = 0.5 * acc
    xc = half + half * jnp.tanh(half)

    xs = xc[:, :SSM_INNER]
    bm = xc[:, SSM_INNER:SSM_INNER + SSM_GROUPS * SSM_STATE]
    cm = xc[:, SSM_INNER + SSM_GROUPS * SSM_STATE:]

    pre = sm_ref[0] + dtb_ref[0]
    dt = jnp.maximum(pre, 0.0) + jnp.log1p(jnp.exp(-jnp.abs(pre)))
    a = -jnp.exp(alog_ref[0])
    row = lax.broadcasted_iota(I32, (q, q), 0)
    col = lax.broadcasted_iota(I32, (q, q), 1)
    lower = row >= col
    cum = jnp.dot(jnp.where(lower, 1.0, 0.0).astype(F32), dt * a, preferred_element_type=F32,
                  precision=lax.Precision.HIGHEST)
    cum_t = cum.T
    cum_last = cum[q - 1:q, :]
    per_head = jnp.concatenate([dt, jnp.exp(cum_last - cum), jnp.exp(cum)], axis=0).astype(BF16)
    spread = jnp.dot(per_head, exp16_ref[...], preferred_element_type=F32)
    dt_full, to_end_full, e_cum_full = spread[0:q], spread[q:2 * q], spread[2 * q:3 * q]
    chunk_decay_full = jnp.dot(jnp.broadcast_to(jnp.exp(cum_last), (SUBLANES, LANES)), exp32_ref[...],
                               preferred_element_type=F32, precision=lax.Precision.HIGHEST)[0:1]
    xdt = xs * dt_full
    xw = (xdt * to_end_full).astype(BF16)
    first_head = lax.broadcasted_iota(I32, (q, LANES), 1) < SSM_HEAD_DIM
    group_w = SSM_HEADS_PER_GROUP * SSM_HEAD_DIM
    heads_per_tile = LANES // SSM_HEAD_DIM

    for g in range(SSM_GROUPS):
        bg32 = bm[:, g * SSM_STATE:(g + 1) * SSM_STATE]
        bg = bg32.astype(BF16)
        cg = cm[:, g * SSM_STATE:(g + 1) * SSM_STATE].astype(BF16)
        cb = lax.dot_general(cg, bg, (((1,), (1,)), ((), ())), preferred_element_type=F32)
        gl = slice(g * group_w, (g + 1) * group_w)
        st = state_ref[g]
        y_off = jnp.dot(cg, st.astype(BF16), preferred_element_type=F32)
        state_ref[g] = st * chunk_decay_full[:, gl] + jnp.dot(bg32.T.astype(BF16), xw[:, gl], preferred_element_type=F32)
        for tile in range(group_w // LANES):
            h0 = g * SSM_HEADS_PER_GROUP + tile * heads_per_tile
            tl = slice(h0 * SSM_HEAD_DIM, h0 * SSM_HEAD_DIM + LANES)
            g_both = []
            for h in range(h0, h0 + heads_per_tile):
                diff = cum[:, h:h + 1] - cum_t[h:h + 1, :]
                decay_in = jnp.exp(jnp.where(lower, diff, -jnp.inf))
                g_both.append((cb * decay_in).astype(BF16))
            xdt_tile = xdt[:, tl]
            stacked = jnp.concatenate([jnp.where(first_head, xdt_tile, 0.0).astype(BF16),
                                       jnp.where(first_head, 0.0, xdt_tile).astype(BF16)], axis=0)
            y_diag = jnp.dot(jnp.concatenate(g_both, axis=1), stacked, preferred_element_type=F32)
            y = y_diag + y_off[:, tile * LANES:(tile + 1) * LANES] * e_cum_full[:, tl] + xs[:, tl] * dskf_ref[0, :, tl]
            y_ref[:, tl] = y * zs_ref[0, :, tl]

    o_ref[0] = _rms(y_ref[...], nrm_ref[0]).astype(o_ref.dtype)


def _ssd_params(conv_w, conv_b, dt_bias, a_log, d_skip, ssm_norm):
    depth = conv_w.shape[0]
    pad = lambda v: jnp.pad(v.astype(F32), ((0, 0), (0, LANES - v.shape[1]))).reshape(depth, 1, LANES)
    expand = (jnp.arange(LANES)[:, None] == jnp.arange(SSM_INNER)[None, :] // SSM_HEAD_DIM).astype(F32)
    d_skip_full = jnp.repeat(d_skip.astype(F32), SSM_HEAD_DIM, axis=1).reshape(depth, 1, SSM_INNER)
    return (conv_w.astype(F32), conv_b.astype(F32).reshape(depth, 1, -1), pad(dt_bias), pad(a_log), d_skip_full,
            ssm_norm.astype(F32).reshape(depth, 1, -1)), (expand.astype(BF16), expand)


def _ssd(xbc, zs, small, layer, params, expands):
    b, s, _ = xbc.shape
    q = SSM_CHUNK
    full = lambda shape: pl.BlockSpec(shape, lambda bi, c: (0,) * len(shape))
    of_layer = lambda a: pl.BlockSpec((1,) + a.shape[1:], lambda bi, c: (layer, 0, 0))
    blk = lambda width: pl.BlockSpec((1, q, width), lambda bi, c: (bi, c, 0))
    group_w = SSM_HEADS_PER_GROUP * SSM_HEAD_DIM
    vmem = (2 * (_nbytes((q, SSM_CONV_DIM), xbc.dtype) + _nbytes((q, SSM_INNER), zs.dtype) + _nbytes((q, LANES), F32)
                 + _nbytes((q, SSM_INNER), BF16) + _nbytes((LANES, SSM_INNER), BF16) + _nbytes((LANES, SSM_INNER), F32))
            + _nbytes((2 * q, SSM_CONV_DIM), xbc.dtype) + _nbytes((SSM_GROUPS, SSM_STATE, group_w), F32)
            + _nbytes((q, SSM_INNER), F32) + 10 * _nbytes((q, SSM_CONV_DIM), F32))
    return pl.pallas_call(
        _ssd_kernel,
        grid=(b, s // q),
        in_specs=[blk(SSM_CONV_DIM),
                  pl.BlockSpec((1, SSM_CONV_TAIL, SSM_CONV_DIM),
                               lambda bi, c: (bi, jnp.maximum(c * (q // SSM_CONV_TAIL) - 1, 0), 0)),
                  blk(SSM_INNER), blk(LANES)] + [of_layer(a) for a in params]
        + [full((LANES, SSM_INNER)), full((LANES, SSM_INNER))],
        out_specs=blk(SSM_INNER),
        out_shape=jax.ShapeDtypeStruct((b, s, SSM_INNER), BF16),
        scratch_shapes=[pltpu.VMEM((SSM_GROUPS, SSM_STATE, group_w), F32),
                        pltpu.VMEM((q, SSM_INNER), F32)],
        compiler_params=_compiler_params(("parallel", "arbitrary"), vmem),
        name="ssd",
    )(xbc, xbc, zs, small, *params, *expands)


def _t5_bucket_of(n):
    steps_per_octave = (REL_BUCKETS - T5_EXACT) / math.log2(REL_MAX_DIST / T5_EXACT)
    x = jnp.maximum(n, T5_EXACT).astype(F32) * (1.0 / T5_EXACT)
    large = T5_EXACT + jnp.floor(jnp.log2(x) * steps_per_octave).astype(I32)
    return jnp.where(n <= T5_EXACT, n, jnp.minimum(large, REL_BUCKETS - 1))


def _dsa_kernel(qmin_ref, kmax_ref, consec_ref, tblt_ref, toe_ref, qcw_ref, qiw_ref, kc_ref, ki_ref, v1_ref, wi_ref, posk_ref, posq_ref, gate_ref, o_ref,
                key_ref, hi_ref, lo_ref, s0_ref, s1_ref, bias_ref, acc_ref, ot_ref, m_ref, l_ref, tie_ref, *, n_sel, tq):
    tk = DSA_KEY_TILE
    qi = pl.program_id(1)
    n_tiles = (qi + 1) * (tq // tk)
    krow = lax.broadcasted_iota(I32, (tk, tq), 0)
    qcol = lax.broadcasted_iota(I32, (tk, tq), 1)
    w = wi_ref[0]
    posq = posq_ref[0]

    def tile_start(kt):
        return pl.multiple_of(kt * tk, tk)

    def causal_of(ks):
        return (ks + krow) <= (qi * tq + qcol)

    def head_lanes(h):
        return slice(h * tq, (h + 1) * tq)

    staged = (s0_ref, s1_ref)

    def stage(k_ref, q_ref, kt, dst_ref):
        dst_ref[...] = jnp.dot(k_ref[0, pl.ds(tile_start(kt), tk), :], q_ref[0, 0], preferred_element_type=F32)

    def tile_pairs(tile_fn, peel_last):
        def pair(pi, last_pair):
            for half in range(2):
                tile_fn(2 * pi + half, staged[half], staged[1 - half], last_pair, last_pair and half == 1)

        def body(pi, carry):
            pair(pi, False)
            return carry

        if peel_last:
            lax.fori_loop(0, n_tiles // 2 - 1, body, 0)
            pair(n_tiles // 2 - 1, True)
        else:
            lax.fori_loop(0, n_tiles // 2, body, 0)

    stage(ki_ref, qiw_ref, 0, staged[0])

    def score_tile(kt, cur_ref, next_ref, last_pair, last_tile):
        ks = tile_start(kt)
        if last_tile:
            stage(kc_ref, qcw_ref, 0, next_ref)
        else:
            stage(ki_ref, qiw_ref, kt + 1, next_ref)
        sc = jnp.zeros((tk, tq), F32)
        for h in range(IDX_HEADS):
            sc = sc + w[h:h + 1, :] * jnp.maximum(cur_ref[:, head_lanes(h)], 0.0)
        sc = jnp.where(sc == 0.0, 0.0, sc)
        bits = lax.bitcast_convert_type(sc, I32)
        key = bits ^ ((bits >> 31) & 0x7FFFFFFF)
        if last_pair:
            key = jnp.where(causal_of(ks), key, INT_MIN)
        key_ref[pl.ds(ks, tk), :] = key
        hi_ref[pl.ds(ks, tk), :] = (key >> 16).astype(I16)
        lo_ref[pl.ds(ks, tk), :] = ((key & 0xFFFF) - HALF16).astype(I16)

    tile_pairs(score_tile, peel_last=True)

    def count16(half_ref, pred):
        def body(pi, cnt):
            for kt in (2 * pi, 2 * pi + 1):
                cnt = cnt + jnp.where(pred(half_ref[pl.ds(tile_start(kt), tk), :]), jnp.int16(1), jnp.int16(0))
            return cnt
        cnt = lax.fori_loop(0, n_tiles // 2, body, jnp.zeros((tk, tq), I16))
        rows = 2 * SUBLANES
        parts = [cnt[r:r + rows, :] for r in range(0, tk, rows)]
        while len(parts) > 1:
            parts = [a + b for a, b in zip(parts[0::2], parts[1::2], strict=True)]
        return jnp.sum(parts[0].astype(I32), axis=0, keepdims=True)

    def nth_largest16(half_ref, rank):
        def bit_step(i, carry):
            cand, reached = carry
            trial = cand | lax.shift_left(jnp.int32(1), 15 - i)
            thr = (trial - HALF16).astype(I16)
            tot = count16(half_ref, lambda v: v >= thr)
            take = tot >= rank
            return jnp.where(take, trial, cand), jnp.where(take, tot, reached)
        init = (jnp.zeros((1, tq), I32), jnp.full((1, tq), n_tiles * tk, I32))
        cand, reached = lax.fori_loop(0, 16, bit_step, init)
        return cand - HALF16, reached

    tau_hi, _ = nth_largest16(hi_ref, n_sel)
    tau_hi16 = tau_hi.astype(I16)
    above_hi = count16(hi_ref, lambda v: v > tau_hi16)

    def mask_low(kt, carry):
        rows = pl.ds(tile_start(kt), tk)
        lo_ref[rows, :] = jnp.where(hi_ref[rows, :] == tau_hi16, lo_ref[rows, :], jnp.int16(-HALF16))
        return carry

    lax.fori_loop(0, n_tiles, mask_low, 0)
    tau_lo, reached_lo = nth_largest16(lo_ref, n_sel - above_hi)
    tau_lo16 = tau_lo.astype(I16)
    tau = lax.shift_left(tau_hi, 16) | (tau_lo + HALF16)
    n_gt = above_hi + count16(lo_ref, lambda v: v > tau_lo16)
    n_ge = above_hi + reached_lo
    need = n_sel - n_gt

    m_ref[...] = jnp.full(m_ref.shape, NEG_BIG, F32)
    l_ref[...] = jnp.zeros(l_ref.shape, F32)
    acc_ref[...] = jnp.zeros(acc_ref.shape, F32)
    tie_ref[...] = jnp.zeros(tie_ref.shape, I32)
    lower_incl = jnp.where(lax.broadcasted_iota(I32, (tk, tk), 0) >= lax.broadcasted_iota(I32, (tk, tk), 1),
                           1.0, 0.0).astype(BF16)

    splits =jnp.where((n_ge > n_sel) & (tau != INT_MIN), 1.0, 0.0)
    any_split = jnp.max(splits) > 0.0

    def attend_tile(kt, cur_ref, next_ref, general_bias, tie_split):
        ks = tile_start(kt)
        key = key_ref[pl.ds(ks, tk), :]
        if tie_split:
            eq = key == tau
            prefix = jnp.dot(lower_incl, jnp.where(eq, 1.0, 0.0).astype(BF16), preferred_element_type=F32)
            tie_before = tie_ref[0:1, :]
            tie_rank = tie_before + prefix.astype(I32)
            sel = ((key > tau) | (eq & (tie_rank <= need))) & causal_of(ks)
            tie_ref[0:1, :] = tie_before + prefix[tk - 1:tk, :].astype(I32)
        else:
            sel = (key >= tau) & causal_of(ks)

        if general_bias == "toeplitz":
            offset = [(qmin_ref[bi, qi] + c * LANES) - (kmax_ref[bi, kt] - (tk - 1)) for c in range(tq // LANES)]
            which = [jnp.where(off == 0, 0, jnp.where(off == tk, 1, 2)) for off in offset]
        elif general_bias == "gather":
            posk = posk_ref[0, pl.ds(ks, tk), :]
            bucket = _t5_bucket_of(jnp.maximum(posq - posk, 0))
            for hp in range(DSA_HEADS // 2):
                table = jnp.broadcast_to(tblt_ref[hp:hp + 1, :], (tk, LANES))
                for c in range(tq // LANES):
                    both = jnp.take_along_axis(table, bucket[:, c * LANES:(c + 1) * LANES], axis=1)
                    lo, hi = 2 * hp * tq + c * LANES, (2 * hp + 1) * tq + c * LANES
                    bias_ref[:, lo:lo + LANES] = lax.bitcast_convert_type(both & jnp.int32(-65536), F32)
                    bias_ref[:, hi:hi + LANES] = lax.bitcast_convert_type(lax.shift_left(both, 16), F32)
        stage(kc_ref, qcw_ref, jnp.minimum(kt + 1, n_tiles - 1), next_ref)
        m_all = m_ref[0:1, :]
        l_all = l_ref[0:1, :]
        v1 = v1_ref[0, :, pl.ds(ks, tk)]
        for h in range(DSA_HEADS):
            hl = head_lanes(h)
            s = cur_ref[:, hl]
            if general_bias == "toeplitz":
                s = s + jnp.concatenate([toe_ref[which[c], h] for c in range(tq // LANES)], axis=1)
            elif general_bias == "gather":
                s = s + bias_ref[:, hl]
            s = jnp.where(sel, s, NEG_BIG)
            m_old = m_all[:, hl]
            m_new = jnp.maximum(m_old, jnp.max(s, axis=0, keepdims=True))
            m_ref[0:1, hl] = m_new
            alpha = jnp.exp2(m_old - m_new)
            p = jnp.exp2((s - m_new).astype(BF16))
            pv = jnp.dot(v1, p, preferred_element_type=F32)
            acc_ref[:, hl] = alpha * acc_ref[:, hl] + pv[0:DSA_HEAD_DIM, :]
            l_ref[0:1, hl] = alpha * l_all[:, hl] + pv[DSA_HEAD_DIM:DSA_HEAD_DIM + 1, :]

    bi = pl.program_id(0)

    def attend(kt, cur_ref, next_ref, last_pair, last_tile):
        far = (qmin_ref[bi, qi] - kmax_ref[bi, kt]) >= T5_FAR
        no_split = jnp.logical_not(any_split)
        near = jnp.logical_and(jnp.logical_not(far), no_split)
        consecutive = consec_ref[bi] == 1

        @pl.when(jnp.logical_and(far, no_split))
        def _():
            attend_tile(kt, cur_ref, next_ref, general_bias=None, tie_split=False)

        @pl.when(jnp.logical_and(near, consecutive))
        def _():
            attend_tile(kt, cur_ref, next_ref, general_bias="toeplitz", tie_split=False)

        @pl.when(jnp.logical_and(near, jnp.logical_not(consecutive)))
        def _():
            attend_tile(kt, cur_ref, next_ref, general_bias="gather", tie_split=False)

        @pl.when(any_split)
        def _():
            attend_tile(kt, cur_ref, next_ref, general_bias="gather", tie_split=True)

    tile_pairs(attend, peel_last=False)

    for h in range(DSA_HEADS):
        hl = head_lanes(h)
        ot_ref[h * DSA_HEAD_DIM:(h + 1) * DSA_HEAD_DIM, :] = acc_ref[:, hl] / l_ref[0:1, hl]
    o_ref[0] = (ot_ref[...].T * gate_ref[0]).astype(o_ref.dtype)


DSA_TOEPLITZ_OFFSETS = (0, DSA_KEY_TILE)


def _bias_toeplitz_kernel(tbl_ref, o_ref):
    j = pl.program_id(0)
    tk = DSA_KEY_TILE
    diff = lax.broadcasted_iota(I32, (tk, LANES), 1) - lax.broadcasted_iota(I32, (tk, LANES), 0)
    for slot, offset in enumerate(DSA_TOEPLITZ_OFFSETS):
        @pl.when(j == slot)
        def _(offset=offset):
            bucket = _t5_bucket_of(jnp.maximum(diff + offset, 0))
            for h in range(DSA_HEADS):
                table = jnp.broadcast_to(tbl_ref[h:h + 1, :], (tk, LANES))
                o_ref[0, h] = jnp.take_along_axis(table, bucket, axis=1)

    @pl.when(j == len(DSA_TOEPLITZ_OFFSETS))
    def _():
        o_ref[...] = jnp.zeros(o_ref.shape, F32)


def _bias_toeplitz(tbl_rows):
    n = len(DSA_TOEPLITZ_OFFSETS) + 1
    return pl.pallas_call(
        _bias_toeplitz_kernel,
        grid=(n,),
        in_specs=[pl.BlockSpec((DSA_HEADS, LANES), lambda j: (0, 0))],
        out_specs=pl.BlockSpec((1, DSA_HEADS, DSA_KEY_TILE, LANES), lambda j: (j, 0, 0, 0)),
        out_shape=jax.ShapeDtypeStruct((n, DSA_HEADS, DSA_KEY_TILE, LANES), F32),
        compiler_params=_compiler_params(("parallel",), 0),
        name="bias_toeplitz",
    )(tbl_rows)


def _dsa_bias_tables(rel_bias):
    tbl = rel_bias.astype(F32) * LOG2E
    tbl = tbl - tbl[REL_BUCKETS - 1:REL_BUCKETS, :]
    rows = jnp.pad(tbl.T, ((0, 0), (0, LANES - REL_BUCKETS)))
    bits = lax.bitcast_convert_type(tbl.T.astype(jnp.bfloat16), jnp.uint16).astype(jnp.uint32)
    packed = lax.bitcast_convert_type((bits[0::2] << 16) | bits[1::2], I32)
    packed = jnp.pad(packed, ((0, SUBLANES - DSA_HEADS // 2), (0, LANES - REL_BUCKETS)))
    return packed, _bias_toeplitz(rows)


def _dsa(bias_tables, qcw, qiw, kc, ki, v1, wi, pos_col, pos_row, gate, tq):
    b, s, _ = kc.shape
    n_sel = min(TOPK_MAX, s // 4)
    vw_rows = v1.shape[1]
    hw = DSA_HEADS * tq
    tbl_t, toeplitz = bias_tables
    q_min = jnp.min(pos_row.reshape(b, s // tq, tq), axis=-1)
    k_max = jnp.max(pos_row.reshape(b, s // DSA_KEY_TILE, DSA_KEY_TILE), axis=-1)
    consecutive = jnp.all(pos_row[:, 0, 1:] - pos_row[:, 0, :-1] == 1, axis=-1).astype(I32)
    tk = DSA_KEY_TILE
    vmem = (2 * (2 * _nbytes((DSA_HEAD_DIM, hw), BF16) + 2 * _nbytes((s, LANES), BF16) + _nbytes((vw_rows, s), F32)
                 + _nbytes((vw_rows, tq), F32) + _nbytes((s, LANES), I32) + _nbytes((tq, DSA_WIDTH), F32)
                 + _nbytes((tq, DSA_WIDTH), BF16)) + _nbytes((s, tq), I32) + 2 * _nbytes((tk, hw), F32)
            + _nbytes((tk, hw), BF16) + _nbytes((DSA_HEAD_DIM, hw), F32) + _nbytes((DSA_WIDTH, tq), F32)
            + 32 * _nbytes((tk, tq), F32))
    return pl.pallas_call(
        functools.partial(_dsa_kernel, n_sel=n_sel, tq=tq),
        grid=(b, s // tq),
        in_specs=[pl.BlockSpec(memory_space=pltpu.SMEM), pl.BlockSpec(memory_space=pltpu.SMEM),
                  pl.BlockSpec(memory_space=pltpu.SMEM),
                  pl.BlockSpec((SUBLANES, LANES), lambda bi, i: (0, 0)),
                  pl.BlockSpec(toeplitz.shape, lambda bi, i: (0, 0, 0, 0)),
                  pl.BlockSpec((1, 1, DSA_HEAD_DIM, hw), lambda bi, i: (bi, i, 0, 0)),
                  pl.BlockSpec((1, 1, IDX_DIM, hw), lambda bi, i: (bi, i, 0, 0)),
                  pl.BlockSpec((1, s, DSA_HEAD_DIM), lambda bi, i: (bi, 0, 0)),
                  pl.BlockSpec((1, s, IDX_DIM), lambda bi, i: (bi, 0, 0)),
                  pl.BlockSpec((1, vw_rows, s), lambda bi, i: (bi, 0, 0)),
                  pl.BlockSpec((1, IDX_HEADS, tq), lambda bi, i: (bi, 0, i)),
                  pl.BlockSpec((1, s, 1), lambda bi, i: (bi, 0, 0)),
                  pl.BlockSpec((1, 1, tq), lambda bi, i: (bi, 0, i)),
                  pl.BlockSpec((1, tq, DSA_WIDTH), lambda bi, i: (bi, i, 0))],
        out_specs=pl.BlockSpec((1, tq, DSA_WIDTH), lambda bi, i: (bi, i, 0)),
        out_shape=jax.ShapeDtypeStruct((b, s, DSA_WIDTH), BF16),
        scratch_shapes=[pltpu.VMEM((s, tq), I32),
                        pltpu.VMEM((s, tq), I16), pltpu.VMEM((s, tq), I16),
                        pltpu.VMEM((tk, hw), F32), pltpu.VMEM((tk, hw), F32),
                        pltpu.VMEM((tk, hw), F32),
                        pltpu.VMEM((DSA_HEAD_DIM, hw), F32),
                        pltpu.VMEM((DSA_WIDTH, tq), F32),
                        pltpu.VMEM((SUBLANES, hw), F32), pltpu.VMEM((SUBLANES, hw), F32),
                        pltpu.VMEM((SUBLANES, tq), I32)],
        compiler_params=_compiler_params(("parallel", "arbitrary"), vmem),
        name="dsa",
    )(q_min, k_max, consecutive, tbl_t, toeplitz, qcw, qiw, kc, ki, v1, wi, pos_col, pos_row, gate)


def _merge_kernel(oa_ref, ob_ref, oc_ref, g_ref, x_ref, p_ref, wa_ref, wb_ref, wc_ref, wo_ref, wpg_ref, wple_ref,
                  gn_ref, o_ref, *, final):
    d = D_MODEL
    ya = jnp.dot(oa_ref[0], wa_ref[0], preferred_element_type=F32)
    yb = jnp.dot(ob_ref[0], wb_ref[0], preferred_element_type=F32)
    yc = jnp.dot(oc_ref[0], wc_ref[0], preferred_element_type=F32)
    merged = g_ref[0, :, 0:d] * ya + g_ref[0, :, d:2 * d] * yb + g_ref[0, :, 2 * d:3 * d] * yc
    x1 = x_ref[0] + jnp.dot(merged.astype(BF16), wo_ref[0], preferred_element_type=F32)
    ple_gate = _sigmoid(jnp.dot(x1.astype(BF16), wpg_ref[0], preferred_element_type=F32))
    x2 = x1 + ple_gate * jnp.dot(p_ref[0, 0].astype(BF16), wple_ref[0], preferred_element_type=F32)
    o_ref[0] = _rms(x2, gn_ref[...]) if final else x2


def _merge(oa, ob, oc, gates, x, p, layer, wa, wb, wc, wo, wpg, wple, final_gain, final, tm=512):
    b, s, d = x.shape
    full = lambda a: pl.BlockSpec((1,) + a.shape[1:], lambda bi, i: (layer, 0, 0), pipeline_mode=pl.Buffered(1))
    row = lambda a: pl.BlockSpec((1, tm, a.shape[2]), lambda bi, i: (bi, i, 0))
    weights = (wa, wb, wc, wo, wpg, wple)
    acts = (oa, ob, oc, gates, x)
    vmem = (sum(_nbytes(a.shape[1:], a.dtype) for a in weights)
            + 2 * (sum(_nbytes((tm, a.shape[2]), a.dtype) for a in acts) + _nbytes((tm, p.shape[3]), p.dtype)
                   + 2 * _nbytes((tm, d), F32)) + 8 * _nbytes((tm, d), F32))
    return pl.pallas_call(
        functools.partial(_merge_kernel, final=final),
        grid=(b, s // tm),
        in_specs=[row(a) for a in acts] + [pl.BlockSpec((1, 1, tm, p.shape[3]), lambda bi, i: (layer, bi, i, 0))]
        + [full(a) for a in weights] + [pl.BlockSpec((1, d), lambda bi, i: (0, 0))],
        out_specs=pl.BlockSpec((1, tm, d), lambda bi, i: (bi, i, 0)),
        out_shape=jax.ShapeDtypeStruct((b, s, d), F32),
        compiler_params=_compiler_params(("parallel", "parallel"), vmem),
        name="merge",
    )(*acts, p, *weights, final_gain.astype(F32).reshape(1, d))


def _rotate_half_cols(w):
    half = w.shape[-1] // 2
    return jnp.concatenate([-w[..., half:], w[..., :half]], axis=-1)


def _prep_weights(w_in, w_uq, w_ukv):
    depth = w_in.shape[0]
    seg = lambda name: w_in[:, :, SPLIT[name][0]:SPLIT[name][1]]
    z = lambda *shape: jnp.zeros((depth,) + shape, w_in.dtype)
    d = D_MODEL
    kr = seg("k_rope")
    rope_lo = MLA_NOPE
    rope_pad = MLA_QK_PAD - MLA_NOPE - MLA_ROPE
    w = {}
    w["mla_in"] = jnp.concatenate(
        [seg("c_q"), seg("c_kv"), z(d, rope_lo), kr, z(d, rope_pad), z(d, rope_lo), _rotate_half_cols(kr), z(d, rope_pad)], axis=-1)
    w["gate_a"] = seg("gate_a")
    w["z"] = seg("z")
    w["xbc"] = seg("xbc")
    w["small"] = jnp.concatenate([seg("dt"), z(d, LANES - SSM_HEADS)], axis=-1)
    w["k_c"] = seg("k_c")
    w["k_idx"] = seg("k_idx")
    w["gate_c"] = seg("gate_c")
    w["merge"] = seg("merge")
    w["q_c_t"] = jnp.swapaxes(seg("q_c") * (DSA_HEAD_DIM ** -0.5 * LOG2E), 1, 2)
    w["q_idx_t"] = jnp.swapaxes(seg("q_idx"), 1, 2)
    w["vw_t"] = jnp.swapaxes(jnp.concatenate([seg("v_c"), seg("w_idx")], axis=-1), 1, 2)
    w = {
        "proj": jnp.concatenate([w["mla_in"]] + [w[seg[0]] for seg in PROJ_SEGMENTS], axis=-1).astype(BF16),
        "proj_t": jnp.concatenate([w["q_c_t"], w["q_idx_t"], w["vw_t"]], axis=1).astype(BF16),
    }

    uq =w_uq.reshape(depth, MLA_Q_LORA, MLA_HEADS, MLA_NOPE + MLA_ROPE)
    nope, rope = uq[..., :MLA_NOPE], uq[..., MLA_NOPE:]
    zq = lambda width: jnp.zeros((depth, MLA_Q_LORA, MLA_HEADS, width), w_uq.dtype)
    hq = MLA_HEADS * MLA_QK_PAD
    uq1 = jnp.concatenate([nope, rope, zq(rope_pad)], axis=-1).reshape(depth, MLA_Q_LORA, hq)
    uq2 = _rotate_half_cols(rope).reshape(depth, MLA_Q_LORA, MLA_HEADS * MLA_ROPE)
    w["uq1_t"] = jnp.swapaxes(uq1, 1, 2).astype(BF16)
    w["uq2_t"] = jnp.swapaxes(uq2, 1, 2).astype(BF16)
    ukv = w_ukv.reshape(depth, MLA_KV_LORA, MLA_HEADS, MLA_NOPE + MLA_V)
    zk = jnp.zeros((depth, MLA_KV_LORA, MLA_HEADS, MLA_QK_PAD - MLA_NOPE), w_ukv.dtype)
    w["uk"] = jnp.concatenate([ukv[..., :MLA_NOPE], zk], axis=-1).reshape(depth, MLA_KV_LORA, hq).astype(BF16)
    w["uv_t"] = jnp.swapaxes(ukv[..., MLA_NOPE:].reshape(depth, MLA_KV_LORA, MLA_WIDTH), 1, 2).astype(BF16)
    return w


def _rope_multipliers(positions):
    b, s = positions.shape
    inv_freq = 1.0 / (ROPE_THETA ** (jnp.arange(0, MLA_ROPE, 2, dtype=F32) / MLA_ROPE))
    ang = positions.astype(F32)[..., None] * inv_freq
    cos, sin = jnp.cos(ang), jnp.sin(ang)
    pad = MLA_QK_PAD - MLA_NOPE - MLA_ROPE
    m1 = jnp.concatenate([jnp.ones((b, s, MLA_NOPE), F32), cos, cos, jnp.zeros((b, s, pad), F32)], axis=-1)
    m2 = jnp.concatenate([jnp.zeros((b, s, MLA_NOPE), F32), sin, sin, jnp.zeros((b, s, pad), F32)], axis=-1)
    return m1, m2, jnp.swapaxes(m1, 1, 2), jnp.swapaxes(m2, 1, 2)


def kernel(x, p, positions, norm_g, w_in, mla_q_norm, w_uq, mla_kv_norm, w_ukv, conv_w, conv_b, dt_bias, a_log, d_skip, ssm_norm, w_br_a, w_br_b, w_br_c, w_out, rel_bias, w_ple, w_ple_gate, final_norm):
    b, s, d = x.shape
    depth = w_in.shape[0]
    w = _prep_weights(w_in, w_uq, w_ukv)
    m1, m2, m1t, m2t = _rope_multipliers(positions)
    pos_col = positions.astype(I32).reshape(b, s, 1)
    pos_row = positions.astype(I32).reshape(b, 1, s)
    wbr_a, wbr_b, wbr_c = w_br_a.astype(BF16), w_br_b.astype(BF16), w_br_c.astype(BF16)
    wo, wpg, wple = w_out.astype(BF16), w_ple_gate.astype(BF16), w_ple.astype(BF16)

    q_norm = mla_q_norm.astype(F32).reshape(depth, 1, -1)
    kv_norm = mla_kv_norm.astype(F32).reshape(depth, 1, -1)

    gains = norm_g.astype(F32).reshape(depth, 1, d)
    bias_tables = _dsa_bias_tables(rel_bias)
    ssd_params, ssd_expands = _ssd_params(conv_w, conv_b, dt_bias, a_log, d_skip, ssm_norm)
    x = x.astype(F32)
    for i in range(depth):
        pr = _proj_all(x, i, gains, w["proj"], w["proj_t"], (m1, m2, m1t, m2t), q_norm, kv_norm,
                       w["uq1_t"], w["uq2_t"], w["uk"], w["uv_t"], DSA_Q_BLOCK)
        o_a = _mla_attn(pr["mla_q_t"], pr["mla_k"], pr["mla_v_t1"], pr["gate_a"])
        o_b = _ssd(pr["xbc"], pr["z"], pr["small"], i, ssd_params, ssd_expands)
        o_c = _dsa(bias_tables, pr["q_c_wide"], pr["q_idx_wide"], pr["k_c"], pr["k_idx"], pr["v_t1"], pr["w_idx_t"], pos_col, pos_row,
                   pr["gate_c"], DSA_Q_BLOCK)
        x = _merge(o_a, o_b, o_c, pr["merge"], x, p, i, wbr_a, wbr_b, wbr_c, wo, wpg, wple, final_norm,
                   final=i == depth - 1)
    return x
```

```python
import functools
import math

import jax
import jax.numpy as jnp
from jax import lax
from jax.experimental import pallas as pl
from jax.experimental.pallas import tpu as pltpu

F32 = jnp.float32
BF16 = jnp.bfloat16
I32 = jnp.int32
I16 = jnp.int16
HALF16 = 1 << 15

D_MODEL = 1024
NORM_EPS = 1e-6

MLA_HEADS = 8
MLA_NOPE = 64
MLA_ROPE = 32
MLA_V = 64
MLA_Q_LORA = 384
MLA_KV_LORA = 256
MLA_WIDTH = MLA_HEADS * MLA_V
ROPE_THETA = 10000.0
MLA_QK_PAD = 128
LOG2E = math.log2(math.e)
MLA_Q_SCALE = (MLA_NOPE + MLA_ROPE) ** -0.5 * LOG2E

SSM_HEADS = 16
SSM_HEAD_DIM = 64
SSM_INNER = SSM_HEADS * SSM_HEAD_DIM
SSM_GROUPS = 2
SSM_STATE = 128
SSM_CONV = 4
SSM_CHUNK = 128
SSM_CONV_DIM = SSM_INNER + 2 * SSM_GROUPS * SSM_STATE
SSM_HEADS_PER_GROUP = SSM_HEADS // SSM_GROUPS
ONES_ROWS = 16
MLA_V1_ROWS = MLA_V + ONES_ROWS
SSM_CONV_TAIL = 16

DSA_HEADS = 8
DSA_HEAD_DIM = 64
DSA_WIDTH = DSA_HEADS * DSA_HEAD_DIM
IDX_HEADS = 8
IDX_DIM = 64
TOPK_MAX = 256
DSA_KEY_TILE = 128
DSA_Q_BLOCK = 256

REL_BUCKETS = 32
REL_MAX_DIST = 128
N_BRANCHES = 3

LANES = 128
SUBLANES = 8
VMEM_LIMIT_CAP = 56 * 1024 * 1024
VMEM_LIMIT_FLOOR = 32 * 1024 * 1024
INT_MIN = -(2 ** 31)
NEG_BIG = -1e30

SPLIT_SIZES = (
    MLA_Q_LORA, MLA_KV_LORA, MLA_ROPE, MLA_WIDTH, SSM_INNER, SSM_CONV_DIM, SSM_HEADS, DSA_WIDTH,
    DSA_HEAD_DIM, DSA_HEAD_DIM, IDX_HEADS * IDX_DIM, IDX_DIM, IDX_HEADS, DSA_WIDTH,
    N_BRANCHES * D_MODEL,
)
SPLIT_NAMES = ("c_q", "c_kv", "k_rope", "gate_a", "z", "xbc", "dt", "q_c", "k_c", "v_c", "q_idx",
               "k_idx", "w_idx", "gate_c", "merge")


def _split_bounds():
    out, off = {}, 0
    for name, size in zip(SPLIT_NAMES, SPLIT_SIZES, strict=True):
        out[name] = (off, off + size)
        off += size
    return out


SPLIT = _split_bounds()


def _t5_large_thresholds():
    exact = REL_BUCKETS // 2
    thr = []
    for j in range(1, REL_BUCKETS - exact):
        thr.append(int(math.ceil(exact * (REL_MAX_DIST / exact) ** (j / (REL_BUCKETS - exact)) - 1e-9)))
    return tuple(thr)


T5_EXACT = REL_BUCKETS // 2
T5_LARGE_THR = _t5_large_thresholds()
T5_FAR = T5_LARGE_THR[-1]


def _compiler_params(semantics, vmem_bytes):
    limit = int(min(VMEM_LIMIT_CAP, max(VMEM_LIMIT_FLOOR, vmem_bytes)))
    return pltpu.CompilerParams(dimension_semantics=semantics, vmem_limit_bytes=limit)


def _nbytes(shape, dtype):
    return math.prod(shape) * jnp.dtype(dtype).itemsize


def _sigmoid(x):
    return 1.0 / (1.0 + jnp.exp(-x))


def _rms(x, g):
    return x * lax.rsqrt(jnp.mean(x * x, axis=-1, keepdims=True) + NORM_EPS) * g


MLA_IN_WIDTH = MLA_Q_LORA + MLA_KV_LORA + 2 * LANES
PROJ_SEGMENTS = (
    ("small", LANES, None, F32),
    ("gate_a", MLA_WIDTH, "silu", BF16),
    ("gate_c", DSA_WIDTH, "silu", BF16),
    ("z", SSM_INNER, "silu", BF16),
    ("xbc", SSM_CONV_DIM, None, BF16),
    ("merge", N_BRANCHES * D_MODEL, "sigmoid", BF16),
    ("k_c", DSA_HEAD_DIM, None, BF16),
    ("k_idx", IDX_DIM, None, BF16),
)
PROJ_CHUNK = 512
PROJ_T_ROWS = (DSA_WIDTH, IDX_HEADS * IDX_DIM, DSA_HEAD_DIM + IDX_HEADS)


def _proj_all_kernel(x_ref, g_ref, w_ref, wt_ref, m1_ref, m2_ref, m1t_ref, m2t_ref, qn_ref, kvn_ref, wq1t_ref, wq2t_ref,
                     wk_ref, wvt_ref, *out_refs):
    h = _rms(x_ref[0], g_ref[0]).astype(BF16)
    n_seg = len(PROJ_SEGMENTS)
    a = jnp.dot(h, w_ref[0, :, 0:MLA_IN_WIDTH], preferred_element_type=F32)
    _mla_prep(a, m1_ref, m2_ref, m1t_ref, m2t_ref, qn_ref, kvn_ref, wq1t_ref, wq2t_ref, wk_ref, wvt_ref, *out_refs[n_seg + 4:])
    off = MLA_IN_WIDTH
    for (_, width, act, _), o_ref in zip(PROJ_SEGMENTS, out_refs[:n_seg], strict=True):
        for c in range(0, width, PROJ_CHUNK):
            cw = min(PROJ_CHUNK, width - c)
            y = jnp.dot(h, w_ref[0, :, off + c:off + c + cw], preferred_element_type=F32)
            if act == "silu":
                y = y * _sigmoid(y)
            elif act == "sigmoid":
                y = _sigmoid(y)
            o_ref[0, :, c:c + cw] = y.astype(o_ref.dtype)
        off += width
    qcw_ref, qiw_ref, v1_ref, wi_ref = out_refs[n_seg:n_seg + 4]
    nt = (((1,), (1,)), ((), ()))
    tq = h.shape[0]
    row = 0
    for o_ref, heads, dim in ((qcw_ref, DSA_HEADS, DSA_HEAD_DIM), (qiw_ref, IDX_HEADS, IDX_DIM)):
        y = lax.dot_general(wt_ref[0, row:row + heads * dim, :], h, nt, preferred_element_type=F32)
        for hd in range(heads):
            o_ref[0, 0, :, hd * tq:(hd + 1) * tq] = y[hd * dim:(hd + 1) * dim, :].astype(o_ref.dtype)
        row += heads * dim
    vw = lax.dot_general(wt_ref[0, row:row + PROJ_T_ROWS[2], :], h, nt, preferred_element_type=F32)
    v1_ref[0, 0:DSA_HEAD_DIM, :] = vw[0:DSA_HEAD_DIM, :].astype(v1_ref.dtype)
    v1_ref[0, DSA_HEAD_DIM:, :] = jnp.ones((ONES_ROWS, tq), v1_ref.dtype)
    wi_ref[0] = vw[DSA_HEAD_DIM:, :]


def _proj_all(x, layer, gains, w, wt, rope, qn, kvn, wq1t, wq2t, wk, wvt, tq):
    b, s, k = x.shape
    n_all = MLA_IN_WIDTH + sum(seg[1] for seg in PROJ_SEGMENTS)
    hq = MLA_HEADS * MLA_QK_PAD
    v1_rows = DSA_HEAD_DIM + ONES_ROWS
    out_shapes = [jax.ShapeDtypeStruct((b, s, width), dt) for _, width, _, dt in PROJ_SEGMENTS]
    out_specs = [pl.BlockSpec((1, tq, width), lambda bi, i: (bi, i, 0)) for _, width, _, _ in PROJ_SEGMENTS]
    for heads, dim in ((DSA_HEADS, DSA_HEAD_DIM), (IDX_HEADS, IDX_DIM)):
        out_shapes.append(jax.ShapeDtypeStruct((b, s // tq, dim, heads * tq), BF16))
        out_specs.append(pl.BlockSpec((1, 1, dim, heads * tq), lambda bi, i: (bi, i, 0, 0)))
    for rows, dt in ((v1_rows, BF16), (IDX_HEADS, F32)):
        out_shapes.append(jax.ShapeDtypeStruct((b, rows, s), dt))
        out_specs.append(pl.BlockSpec((1, rows, tq), lambda bi, i: (bi, 0, i)))
    for rows in (hq, None, MLA_HEADS * MLA_V1_ROWS):
        if rows is None:
            out_shapes.append(jax.ShapeDtypeStruct((b, s, hq), BF16))
            out_specs.append(pl.BlockSpec((1, tq, hq), lambda bi, i: (bi, i, 0)))
        else:
            out_shapes.append(jax.ShapeDtypeStruct((b, rows, s), BF16))
            out_specs.append(pl.BlockSpec((1, rows, tq), lambda bi, i: (bi, 0, i)))
    resident = pl.Buffered(1)
    weights = (w, wt, qn, kvn, wq1t, wq2t, wk, wvt)
    layer_spec = lambda a: pl.BlockSpec((1,) + a.shape[1:], lambda bi, i: (layer,) + (0,) * (a.ndim - 1), pipeline_mode=resident)
    m1, m2, m1t, m2t = rope
    vmem = (sum(_nbytes(a.shape[1:], a.dtype) for a in weights) + 4 * _nbytes((tq, k), F32) + 8 * _nbytes((tq, LANES), F32)
            + 2 * sum(_nbytes((tq, width), dt) for _, width, _, dt in PROJ_SEGMENTS)
            + 2 * (2 * _nbytes((DSA_HEAD_DIM, DSA_HEADS * tq), BF16) + _nbytes((v1_rows + IDX_HEADS, tq), F32))
            + 2 * (2 * _nbytes((tq, hq), BF16) + _nbytes((MLA_HEADS * MLA_V1_ROWS, tq), BF16))
            + 8 * _nbytes((tq, PROJ_CHUNK), F32) + 6 * _nbytes((tq, hq), F32))
    outs = pl.pallas_call(
        _proj_all_kernel,
        grid=(b, s // tq),
        in_specs=[pl.BlockSpec((1, tq, k), lambda bi, i: (bi, i, 0)), layer_spec(gains), layer_spec(w), layer_spec(wt),
                  pl.BlockSpec((1, tq, LANES), lambda bi, i: (bi, i, 0)), pl.BlockSpec((1, tq, LANES), lambda bi, i: (bi, i, 0)),
                  pl.BlockSpec((1, LANES, tq), lambda bi, i: (bi, 0, i)), pl.BlockSpec((1, LANES, tq), lambda bi, i: (bi, 0, i))]
        + [layer_spec(a) for a in weights[2:]],
        out_specs=out_specs,
        out_shape=out_shapes,
        compiler_params=_compiler_params(("parallel", "parallel"), vmem),
        name="proj_all",
    )(x, gains, w, wt, m1, m2, m1t, m2t, *weights[2:])
    names = [seg[0] for seg in PROJ_SEGMENTS] + ["q_c_wide", "q_idx_wide", "v_t1", "w_idx_t", "mla_q_t", "mla_k", "mla_v_t1"]
    return dict(zip(names, outs, strict=True))


def _mla_prep(a, m1_ref, m2_ref, m1t_ref, m2t_ref, qn_ref, kvn_ref, wq1t_ref, wq2t_ref, wk_ref, wvt_ref, qt_ref, k_ref, vt_ref):
    c_q = a[:, :MLA_Q_LORA]
    c_kv = a[:, MLA_Q_LORA:MLA_Q_LORA + MLA_KV_LORA]
    kr1 = a[:, MLA_Q_LORA + MLA_KV_LORA:MLA_Q_LORA + MLA_KV_LORA + LANES]
    kr2 = a[:, MLA_Q_LORA + MLA_KV_LORA + LANES:]
    cqn = _rms(c_q, qn_ref[0]).astype(BF16)
    ckvn = _rms(c_kv, kvn_ref[0]).astype(BF16)
    nt = (((1,), (1,)), ((), ()))
    qa_t = lax.dot_general(wq1t_ref[0], cqn, nt, preferred_element_type=F32)
    qb_t = lax.dot_general(wq2t_ref[0], cqn, nt, preferred_element_type=F32)
    kn = jnp.dot(ckvn, wk_ref[0], preferred_element_type=F32)
    kr = kr1 * m1_ref[0] + kr2 * m2_ref[0]
    rope = slice(MLA_NOPE, MLA_NOPE + MLA_ROPE)
    cos_t = m1t_ref[0, rope, :] * MLA_Q_SCALE
    sin_t = m2t_ref[0, rope, :] * MLA_Q_SCALE
    pad_rows = jnp.zeros((MLA_QK_PAD - MLA_NOPE - MLA_ROPE, qa_t.shape[1]), qt_ref.dtype)
    for h in range(MLA_HEADS):
        base = h * MLA_QK_PAD
        sl = slice(base, base + MLA_QK_PAD)
        qt_ref[0, base:base + MLA_NOPE, :] = (qa_t[base:base + MLA_NOPE, :] * MLA_Q_SCALE).astype(qt_ref.dtype)
        qt_ref[0, base + MLA_NOPE:base + MLA_NOPE + MLA_ROPE, :] = (
            qa_t[base + MLA_NOPE:base + MLA_NOPE + MLA_ROPE, :] * cos_t
            + qb_t[h * MLA_ROPE:(h + 1) * MLA_ROPE, :] * sin_t).astype(qt_ref.dtype)
        qt_ref[0, base + MLA_NOPE + MLA_ROPE:base + MLA_QK_PAD, :] = pad_rows
        k_ref[0, :, sl] = (kn[:, sl] + kr).astype(k_ref.dtype)
    v_t = lax.dot_general(wvt_ref[0], ckvn, nt, preferred_element_type=F32)
    ones = jnp.ones((ONES_ROWS, v_t.shape[1]), vt_ref.dtype)
    for h in range(MLA_HEADS):
        base = h * MLA_V1_ROWS
        vt_ref[0, base:base + MLA_V, :] = v_t[h * MLA_V:(h + 1) * MLA_V, :].astype(vt_ref.dtype)
        vt_ref[0, base + MLA_V:base + MLA_V1_ROWS, :] = ones


def _mla_attn_kernel(qt_ref, k_ref, vt_ref, g_ref, o_ref, m_ref, l_ref, acc_ref, s0_ref, s1_ref, *, tq, tk):
    assert tq == 2 * tk
    qi = pl.program_id(1)
    krow = lax.broadcasted_iota(I32, (tk, tq), 0)
    qcol = lax.broadcasted_iota(I32, (tk, tq), 1)
    tiles_per_q = tq // tk
    m_ref[...] = jnp.full(m_ref.shape, -jnp.inf, F32)
    l_ref[...] = jnp.zeros(l_ref.shape, F32)
    acc_ref[...] = jnp.zeros(acc_ref.shape, F32)

    staged = (s0_ref, s1_ref)
    n_tiles = (qi + 1) * tiles_per_q

    def head_lanes(h):
        return slice(h * tq, (h + 1) * tq)

    def stage(j, dst_ref):
        ks = pl.multiple_of(j * tk, tk)
        for h in range(MLA_HEADS):
            k = k_ref[0, pl.ds(ks, tk), h * MLA_QK_PAD:(h + 1) * MLA_QK_PAD]
            dst_ref[:, head_lanes(h)] = jnp.dot(k, qt_ref[0, h * MLA_QK_PAD:(h + 1) * MLA_QK_PAD, :],
                                                preferred_element_type=F32)

    def tile(j, cur_ref, next_ref, diagonal):
        ks = pl.multiple_of(j * tk, tk)
        stage(jnp.minimum(j + 1, n_tiles - 1), next_ref)
        m_all = m_ref[...]
        l_all = l_ref[...]
        if diagonal:
            causal = ks + krow <= qi * tq + qcol
        for h in range(MLA_HEADS):
            s = cur_ref[:, head_lanes(h)]
            if diagonal:
                s = jnp.where(causal, s, -jnp.inf)
            m_old = m_all[h:h + 1, :]
            m_new = jnp.maximum(m_old, jnp.max(s, axis=0, keepdims=True))
            alpha = jnp.exp2(m_old - m_new)
            p = jnp.exp2((s - m_new).astype(BF16))
            m_ref[h:h + 1, :] = m_new
            pv = jnp.dot(vt_ref[0, h * MLA_V1_ROWS:(h + 1) * MLA_V1_ROWS, pl.ds(ks, tk)], p, preferred_element_type=F32)
            hs = slice(h * MLA_V, (h + 1) * MLA_V)
            acc_ref[hs, :] = alpha * acc_ref[hs, :] + pv[0:MLA_V, :]
            l_ref[h:h + 1, :] = alpha * l_all[h:h + 1, :] + pv[MLA_V:MLA_V + 1, :]

    stage(0, staged[0])

    def full_pair(pi, carry):
        for half in range(2):
            tile(2 * pi + half, staged[half], staged[1 - half], diagonal=False)
        return carry

    lax.fori_loop(0, qi, full_pair, 0)
    for half in range(2):
        tile(2 * qi + half, staged[half], staged[1 - half], diagonal=True)
    for h in range(MLA_HEADS):
        hs = slice(h * MLA_V, (h + 1) * MLA_V)
        acc_ref[hs, :] = acc_ref[hs, :] * (1.0 / l_ref[h:h + 1, :])
    o_ref[0] = (acc_ref[...].T * g_ref[0]).astype(o_ref.dtype)


def _mla_attn(qt, k, vt, gate, tq=256, tk=128):
    b, s, hq = k.shape
    vmem = 2 * (_nbytes((hq, tq), BF16) + _nbytes((s, hq), BF16) + _nbytes((MLA_WIDTH, s), BF16)
                + _nbytes((tq, MLA_WIDTH), F32) + _nbytes((tq, MLA_WIDTH), BF16)) + _nbytes((MLA_WIDTH, tq), F32) + 16 * _nbytes((tk, tq), F32)
    return pl.pallas_call(
        functools.partial(_mla_attn_kernel, tq=tq, tk=tk),
        grid=(b, s // tq),
        in_specs=[pl.BlockSpec((1, hq, tq), lambda bi, i: (bi, 0, i)),
                  pl.BlockSpec((1, s, hq), lambda bi, i: (bi, 0, 0)),
                  pl.BlockSpec((1, MLA_HEADS * MLA_V1_ROWS, s), lambda bi, i: (bi, 0, 0)),
                  pl.BlockSpec((1, tq, MLA_WIDTH), lambda bi, i: (bi, i, 0))],
        out_specs=pl.BlockSpec((1, tq, MLA_WIDTH), lambda bi, i: (bi, i, 0)),
        out_shape=jax.ShapeDtypeStruct((b, s, MLA_WIDTH), BF16),
        scratch_shapes=[pltpu.VMEM((MLA_HEADS, tq), F32), pltpu.VMEM((MLA_HEADS, tq), F32),
                        pltpu.VMEM((MLA_WIDTH, tq), F32),
                        pltpu.VMEM((tk, MLA_HEADS * tq), F32), pltpu.VMEM((tk, MLA_HEADS * tq), F32)],
        compiler_params=_compiler_params(("parallel", "arbitrary"), vmem),
        name="mla_attn",
    )(qt, k, vt, gate)


def _ssd_kernel(xbc_ref, xprev_ref, zs_ref, sm_ref, cw_ref, cb_ref, dtb_ref, alog_ref, dskf_ref, nrm_ref, exp16_ref, exp32_ref,
                o_ref, state_ref, y_ref):
    c = pl.program_id(1)
    q = SSM_CHUNK

    @pl.when(c == 0)
    def _():
        state_ref[...] = jnp.zeros(state_ref.shape, F32)

    x_in = xbc_ref[0]
    tail = xprev_ref.shape[1]
    prev = jnp.where(c > 0, xprev_ref[0].astype(F32), 0.0).astype(x_in.dtype)
    win = jnp.concatenate([prev, x_in], axis=0)
    x = x_in.astype(F32)
    wrow = lax.broadcasted_iota(I32, (q, tail + q), 0)
    wcol = lax.broadcasted_iota(I32, (q, tail + q), 1)
    acc = cb_ref[0] + cw_ref[0, SSM_CONV - 1:SSM_CONV, :] * x
    for j in range(1, SSM_CONV):
        pick = jnp.where(wcol == wrow + (tail - j), 1.0, 0.0).astype(win.dtype)
        acc = acc + cw_ref[0, SSM_CONV - 1 - j:SSM_CONV - j, :] * jnp.dot(pick, win, preferred_element_type=F32)
    half = 0.5 * acc
    xc = half + half * jnp.tanh(half)

    xs = xc[:, :SSM_INNER]
    bm = xc[:, SSM_INNER:SSM_INNER + SSM_GROUPS * SSM_STATE]
    cm = xc[:, SSM_INNER + SSM_GROUPS * SSM_STATE:]

    pre = sm_ref[0] + dtb_ref[0]
    dt = jnp.maximum(pre, 0.0) + jnp.log1p(jnp.exp(-jnp.abs(pre)))
    a = -jnp.exp(alog_ref[0])
    row = lax.broadcasted_iota(I32, (q, q), 0)
    col = lax.broadcasted_iota(I32, (q, q), 1)
    lower = row >= col
    cum = jnp.dot(jnp.where(lower, 1.0, 0.0).astype(F32), dt * a, preferred_element_type=F32,
                  precision=lax.Precision.HIGHEST)
    cum_t = cum.T
    cum_last = cum[q - 1:q, :]
    per_head = jnp.concatenate([dt, jnp.exp(cum_last - cum), jnp.exp(cum)], axis=0).astype(BF16)
    spread = jnp.dot(per_head, exp16_ref[...], preferred_element_type=F32)
    dt_full, to_end_full, e_cum_full = spread[0:q], spread[q:2 * q], spread[2 * q:3 * q]
    chunk_decay_full = jnp.dot(jnp.broadcast_to(jnp.exp(cum_last), (SUBLANES, LANES)), exp32_ref[...],
                               preferred_element_type=F32, precision=lax.Precision.HIGHEST)[0:1]
    xdt = xs * dt_full
    xw = (xdt * to_end_full).astype(BF16)
    first_head = lax.broadcasted_iota(I32, (q, LANES), 1) < SSM_HEAD_DIM
    group_w = SSM_HEADS_PER_GROUP * SSM_HEAD_DIM
    heads_per_tile = LANES // SSM_HEAD_DIM

    for g in range(SSM_GROUPS):
        bg32 = bm[:, g * SSM_STATE:(g + 1) * SSM_STATE]
        bg = bg32.astype(BF16)
        cg = cm[:, g * SSM_STATE:(g + 1) * SSM_STATE].astype(BF16)
        cb = lax.dot_general(cg, bg, (((1,), (1,)), ((), ())), preferred_element_type=F32)
        gl = slice(g * group_w, (g + 1) * group_w)
        st = state_ref[g]
        y_off = jnp.dot(cg, st.astype(BF16), preferred_element_type=F32)
        state_ref[g] = st * chunk_decay_full[:, gl] + jnp.dot(bg32.T.astype(BF16), xw[:, gl], preferred_element_type=F32)
        for tile in range(group_w // LANES):
            h0 = g * SSM_HEADS_PER_GROUP + tile * heads_per_tile
            tl = slice(h0 * SSM_HEAD_DIM, h0 * SSM_HEAD_DIM + LANES)
            g_both = []
            for h in range(h0, h0 + heads_per_tile):
                diff = cum[:, h:h + 1] - cum_t[h:h + 1, :]
                decay_in = jnp.exp(jnp.where(lower, diff, -jnp.inf))
                g_both.append((cb * decay_in).astype(BF16))
            xdt_tile = xdt[:, tl]
            stacked = jnp.concatenate([jnp.where(first_head, xdt_tile, 0.0).astype(BF16),
                                       jnp.where(first_head, 0.0, xdt_tile).astype(BF16)], axis=0)
            y_diag = jnp.dot(jnp.concatenate(g_both, axis=1), stacked, preferred_element_type=F32)
            y = y_diag + y_off[:, tile * LANES:(tile + 1) * LANES] * e_cum_full[:, tl] + xs[:, tl] * dskf_ref[0, :, tl]
            y_ref[:, tl] = y * zs_ref[0, :, tl]

    o_ref[0] = _rms(y_ref[...], nrm_ref[0]).astype(o_ref.dtype)


def _ssd_params(conv_w, conv_b, dt_bias, a_log, d_skip, ssm_norm):
    depth = conv_w.shape[0]
    pad = lambda v: jnp.pad(v.astype(F32), ((0, 0), (0, LANES - v.shape[1]))).reshape(depth, 1, LANES)
    expand = (jnp.arange(LANES)[:, None] == jnp.arange(SSM_INNER)[None, :] // SSM_HEAD_DIM).astype(F32)
    d_skip_full = jnp.repeat(d_skip.astype(F32), SSM_HEAD_DIM, axis=1).reshape(depth, 1, SSM_INNER)
    return (conv_w.astype(F32), conv_b.astype(F32).reshape(depth, 1, -1), pad(dt_bias), pad(a_log), d_skip_full,
            ssm_norm.astype(F32).reshape(depth, 1, -1)), (expand.astype(BF16), expand)


def _ssd(xbc, zs, small, layer, params, expands):
    b, s, _ = xbc.shape
    q = SSM_CHUNK
    full = lambda shape: pl.BlockSpec(shape, lambda bi, c: (0,) * len(shape))
    of_layer = lambda a: pl.BlockSpec((1,) + a.shape[1:], lambda bi, c: (layer, 0, 0))
    blk = lambda width: pl.BlockSpec((1, q, width), lambda bi, c: (bi, c, 0))
    group_w = SSM_HEADS_PER_GROUP * SSM_HEAD_DIM
    vmem = (2 * (_nbytes((q, SSM_CONV_DIM), xbc.dtype) + _nbytes((q, SSM_INNER), zs.dtype) + _nbytes((q, LANES), F32)
                 + _nbytes((q, SSM_INNER), BF16) + _nbytes((LANES, SSM_INNER), BF16) + _nbytes((LANES, SSM_INNER), F32))
            + _nbytes((2 * q, SSM_CONV_DIM), xbc.dtype) + _nbytes((SSM_GROUPS, SSM_STATE, group_w), F32)
            + _nbytes((q, SSM_INNER), F32) + 10 * _nbytes((q, SSM_CONV_DIM), F32))
    return pl.pallas_call(
        _ssd_kernel,
        grid=(b, s // q),
        in_specs=[blk(SSM_CONV_DIM),
                  pl.BlockSpec((1, SSM_CONV_TAIL, SSM_CONV_DIM),
                               lambda bi, c: (bi, jnp.maximum(c * (q // SSM_CONV_TAIL) - 1, 0), 0)),
                  blk(SSM_INNER), blk(LANES)] + [of_layer(a) for a in params]
        + [full((LANES, SSM_INNER)), full((LANES, SSM_INNER))],
        out_specs=blk(SSM_INNER),
        out_shape=jax.ShapeDtypeStruct((b, s, SSM_INNER), BF16),
        scratch_shapes=[pltpu.VMEM((SSM_GROUPS, SSM_STATE, group_w), F32),
                        pltpu.VMEM((q, SSM_INNER), F32)],
        compiler_params=_compiler_params(("parallel", "arbitrary"), vmem),
        name="ssd",
    )(xbc, xbc, zs, small, *params, *expands)


def _t5_bucket_of(n):
    steps_per_octave = (REL_BUCKETS - T5_EXACT) / math.log2(REL_MAX_DIST / T5_EXACT)
    x = jnp.maximum(n, T5_EXACT).astype(F32) * (1.0 / T5_EXACT)
    large = T5_EXACT + jnp.floor(jnp.log2(x) * steps_per_octave).astype(I32)
    return jnp.where(n <= T5_EXACT, n, jnp.minimum(large, REL_BUCKETS - 1))


def _dsa_kernel(qmin_ref, kmax_ref, consec_ref, tblt_ref, toe_ref, qcw_ref, qiw_ref, kc_ref, ki_ref, v1_ref, wi_ref, posk_ref, posq_ref, gate_ref, o_ref,
                key_ref, hi_ref, lo_ref, s0_ref, s1_ref, bias_ref, acc_ref, ot_ref, m_ref, l_ref, tie_ref, *, n_sel, tq):
    tk = DSA_KEY_TILE
    qi = pl.program_id(1)
    n_tiles = (qi + 1) * (tq // tk)
    krow = lax.broadcasted_iota(I32, (tk, tq), 0)
    qcol = lax.broadcasted_iota(I32, (tk, tq), 1)
    w = wi_ref[0]
    posq = posq_ref[0]

    def tile_start(kt):
        return pl.multiple_of(kt * tk, tk)

    def causal_of(ks):
        return (ks + krow) <= (qi * tq + qcol)

    def head_lanes(h):
        return slice(h * tq, (h + 1) * tq)

    staged = (s0_ref, s1_ref)

    def stage(k_ref, q_ref, kt, dst_ref):
        dst_ref[...] = jnp.dot(k_ref[0, pl.ds(tile_start(kt), tk), :], q_ref[0, 0], preferred_element_type=F32)

    def tile_pairs(tile_fn, peel_last):
        def pair(pi, last_pair):
            for half in range(2):
                tile_fn(2 * pi + half, staged[half], staged[1 - half], last_pair, last_pair and half == 1)

        def body(pi, carry):
            pair(pi, False)
            return carry

        if peel_last:
            lax.fori_loop(0, n_tiles // 2 - 1, body, 0)
            pair(n_tiles // 2 - 1, True)
        else:
            lax.fori_loop(0, n_tiles // 2, body, 0)

    stage(ki_ref, qiw_ref, 0, staged[0])

    def score_tile(kt, cur_ref, next_ref, last_pair, last_tile):
        ks = tile_start(kt)
        if last_tile:
            stage(kc_ref, qcw_ref, 0, next_ref)
        else:
            stage(ki_ref, qiw_ref, kt + 1, next_ref)
        sc = jnp.zeros((tk, tq), F32)
        for h in range(IDX_HEADS):
            sc = sc + w[h:h + 1, :] * jnp.maximum(cur_ref[:, head_lanes(h)], 0.0)
        sc = jnp.where(sc == 0.0, 0.0, sc)
        bits = lax.bitcast_convert_type(sc, I32)
        key = bits ^ ((bits >> 31) & 0x7FFFFFFF)
        if last_pair:
            key = jnp.where(causal_of(ks), key, INT_MIN)
        key_ref[pl.ds(ks, tk), :] = key
        hi_ref[pl.ds(ks, tk), :] = (key >> 16).astype(I16)
        lo_ref[pl.ds(ks, tk), :] = ((key & 0xFFFF) - HALF16).astype(I16)

    tile_pairs(score_tile, peel_last=True)

    def count16(half_ref, pred):
        def body(pi, cnt):
            for kt in (2 * pi, 2 * pi + 1):
                cnt = cnt + jnp.where(pred(half_ref[pl.ds(tile_start(kt), tk), :]), jnp.int16(1), jnp.int16(0))
            return cnt
        cnt = lax.fori_loop(0, n_tiles // 2, body, jnp.zeros((tk, tq), I16))
        rows = 2 * SUBLANES
        parts = [cnt[r:r + rows, :] for r in range(0, tk, rows)]
        while len(parts) > 1:
            parts = [a + b for a, b in zip(parts[0::2], parts[1::2], strict=True)]
        return jnp.sum(parts[0].astype(I32), axis=0, keepdims=True)

    def nth_largest16(half_ref, rank):
        def bit_step(i, carry):
            cand, reached = carry
            trial = cand | lax.shift_left(jnp.int32(1), 15 - i)
            thr = (trial - HALF16).astype(I16)
            tot = count16(half_ref, lambda v: v >= thr)
            take = tot >= rank
            return jnp.where(take, trial, cand), jnp.where(take, tot, reached)
        init = (jnp.zeros((1, tq), I32), jnp.full((1, tq), n_tiles * tk, I32))
        cand, reached = lax.fori_loop(0, 16, bit_step, init)
        return cand - HALF16, reached

    tau_hi, _ = nth_largest16(hi_ref, n_sel)
    tau_hi16 = tau_hi.astype(I16)
    above_hi = count16(hi_ref, lambda v: v > tau_hi16)

    def mask_low(kt, carry):
        rows = pl.ds(tile_start(kt), tk)
        lo_ref[rows, :] = jnp.where(hi_ref[rows, :] == tau_hi16, lo_ref[rows, :], jnp.int16(-HALF16))
        return carry

    lax.fori_loop(0, n_tiles, mask_low, 0)
    tau_lo, reached_lo = nth_largest16(lo_ref, n_sel - above_hi)
    tau_lo16 = tau_lo.astype(I16)
    tau = lax.shift_left(tau_hi, 16) | (tau_lo + HALF16)
    n_gt = above_hi + count16(lo_ref, lambda v: v > tau_lo16)
    n_ge = above_hi + reached_lo
    need = n_sel - n_gt
    tau_floor = jnp.maximum(tau, INT_MIN + 1)

    m_ref[...] = jnp.full(m_ref.shape, NEG_BIG, F32)
    l_ref[...] = jnp.zeros(l_ref.shape, F32)
    acc_ref[...] = jnp.zeros(acc_ref.shape, F32)
    tie_ref[...] = jnp.zeros(tie_ref.shape, I32)
    lower_incl = jnp.where(lax.broadcasted_iota(I32, (tk, tk), 0) >= lax.broadcasted_iota(I32, (tk, tk), 1),
                           1.0, 0.0).astype(BF16)

    splits =jnp.where((n_ge > n_sel) & (tau != INT_MIN), 1.0, 0.0)
    any_split = jnp.max(splits) > 0.0

    def attend_tile(kt, cur_ref, next_ref, general_bias, tie_split):
        ks = tile_start(kt)
        key = key_ref[pl.ds(ks, tk), :]
        if tie_split:
            eq = key == tau
            prefix = jnp.dot(lower_incl, jnp.where(eq, 1.0, 0.0).astype(BF16), preferred_element_type=F32)
            tie_before = tie_ref[0:1, :]
            tie_rank = tie_before + prefix.astype(I32)
            sel = ((key > tau) | (eq & (tie_rank <= need))) & causal_of(ks)
            tie_ref[0:1, :] = tie_before + prefix[tk - 1:tk, :].astype(I32)
        else:
            sel = key >= tau_floor

        if general_bias == "toeplitz":
            offset = [(qmin_ref[bi, qi] + c * LANES) - (kmax_ref[bi, kt] - (tk - 1)) for c in range(tq // LANES)]
            which = [jnp.where(off == 0, 0, jnp.where(off == tk, 1, 2)) for off in offset]
        elif general_bias == "gather":
            posk = posk_ref[0, pl.ds(ks, tk), :]
            bucket = _t5_bucket_of(jnp.maximum(posq - posk, 0))
            for hp in range(DSA_HEADS // 2):
                table = jnp.broadcast_to(tblt_ref[hp:hp + 1, :], (tk, LANES))
                for c in range(tq // LANES):
                    both = jnp.take_along_axis(table, bucket[:, c * LANES:(c + 1) * LANES], axis=1)
                    lo, hi = 2 * hp * tq + c * LANES, (2 * hp + 1) * tq + c * LANES
                    bias_ref[:, lo:lo + LANES] = lax.bitcast_convert_type(both & jnp.int32(-65536), F32)
                    bias_ref[:, hi:hi + LANES] = lax.bitcast_convert_type(lax.shift_left(both, 16), F32)
        stage(kc_ref, qcw_ref, jnp.minimum(kt + 1, n_tiles - 1), next_ref)
        m_all = m_ref[0:1, :]
        l_all = l_ref[0:1, :]
        v1 = v1_ref[0, :, pl.ds(ks, tk)]
        for h in range(DSA_HEADS):
            hl = head_lanes(h)
            s = cur_ref[:, hl]
            if general_bias == "toeplitz":
                s = s + jnp.concatenate([toe_ref[which[c], h] for c in range(tq // LANES)], axis=1)
            elif general_bias == "gather":
                s = s + bias_ref[:, hl]
            s = jnp.where(sel, s, NEG_BIG)
            m_old = m_all[:, hl]
            m_new = jnp.maximum(m_old, jnp.max(s, axis=0, keepdims=True))
            m_ref[0:1, hl] = m_new
            alpha = jnp.exp2(m_old - m_new)
            p = jnp.exp2((s - m_new).astype(BF16))
            pv = jnp.dot(v1, p, preferred_element_type=F32)
            acc_ref[:, hl] = alpha * acc_ref[:, hl] + pv[0:DSA_HEAD_DIM, :]
            l_ref[0:1, hl] = alpha * l_all[:, hl] + pv[DSA_HEAD_DIM:DSA_HEAD_DIM + 1, :]

    bi = pl.program_id(0)

    def attend(kt, cur_ref, next_ref, last_pair, last_tile):
        far = (qmin_ref[bi, qi] - kmax_ref[bi, kt]) >= T5_FAR
        no_split = jnp.logical_not(any_split)
        near = jnp.logical_and(jnp.logical_not(far), no_split)
        consecutive = consec_ref[bi] == 1

        @pl.when(jnp.logical_and(far, no_split))
        def _():
            attend_tile(kt, cur_ref, next_ref, general_bias=None, tie_split=False)

        @pl.when(jnp.logical_and(near, consecutive))
        def _():
            attend_tile(kt, cur_ref, next_ref, general_bias="toeplitz", tie_split=False)

        @pl.when(jnp.logical_and(near, jnp.logical_not(consecutive)))
        def _():
            attend_tile(kt, cur_ref, next_ref, general_bias="gather", tie_split=False)

        @pl.when(any_split)
        def _():
            attend_tile(kt, cur_ref, next_ref, general_bias="gather", tie_split=True)

    tile_pairs(attend, peel_last=False)

    for h in range(DSA_HEADS):
        hl = head_lanes(h)
        ot_ref[h * DSA_HEAD_DIM:(h + 1) * DSA_HEAD_DIM, :] = acc_ref[:, hl] * (1.0 / l_ref[0:1, hl])
    o_ref[0] = (ot_ref[...].T * gate_ref[0]).astype(o_ref.dtype)


DSA_TOEPLITZ_OFFSETS = (0, DSA_KEY_TILE)


def _bias_toeplitz_kernel(tbl_ref, o_ref):
    j = pl.program_id(0)
    tk = DSA_KEY_TILE
    diff = lax.broadcasted_iota(I32, (tk, LANES), 1) - lax.broadcasted_iota(I32, (tk, LANES), 0)
    for slot, offset in enumerate(DSA_TOEPLITZ_OFFSETS):
        @pl.when(j == slot)
        def _(offset=offset):
            bucket = _t5_bucket_of(jnp.maximum(diff + offset, 0))
            for h in range(DSA_HEADS):
                table = jnp.broadcast_to(tbl_ref[h:h + 1, :], (tk, LANES))
                o_ref[0, h] = jnp.take_along_axis(table, bucket, axis=1)

    @pl.when(j == len(DSA_TOEPLITZ_OFFSETS))
    def _():
        o_ref[...] = jnp.zeros(o_ref.shape, F32)


def _bias_toeplitz(tbl_rows):
    n = len(DSA_TOEPLITZ_OFFSETS) + 1
    return pl.pallas_call(
        _bias_toeplitz_kernel,
        grid=(n,),
        in_specs=[pl.BlockSpec((DSA_HEADS, LANES), lambda j: (0, 0))],
        out_specs=pl.BlockSpec((1, DSA_HEADS, DSA_KEY_TILE, LANES), lambda j: (j, 0, 0, 0)),
        out_shape=jax.ShapeDtypeStruct((n, DSA_HEADS, DSA_KEY_TILE, LANES), F32),
        compiler_params=_compiler_params(("parallel",), 0),
        name="bias_toeplitz",
    )(tbl_rows)


def _dsa_bias_tables(rel_bias):
    tbl = rel_bias.astype(F32) * LOG2E
    tbl = tbl - tbl[REL_BUCKETS - 1:REL_BUCKETS, :]
    rows = jnp.pad(tbl.T, ((0, 0), (0, LANES - REL_BUCKETS)))
    bits = lax.bitcast_convert_type(tbl.T.astype(jnp.bfloat16), jnp.uint16).astype(jnp.uint32)
    packed = lax.bitcast_convert_type((bits[0::2] << 16) | bits[1::2], I32)
    packed = jnp.pad(packed, ((0, SUBLANES - DSA_HEADS // 2), (0, LANES - REL_BUCKETS)))
    return packed, _bias_toeplitz(rows)


def _dsa(bias_tables, qcw, qiw, kc, ki, v1, wi, pos_col, pos_row, gate, tq):
    b, s, _ = kc.shape
    n_sel = min(TOPK_MAX, s // 4)
    vw_rows = v1.shape[1]
    hw = DSA_HEADS * tq
    tbl_t, toeplitz = bias_tables
    q_min = jnp.min(pos_row.reshape(b, s // tq, tq), axis=-1)
    k_max = jnp.max(pos_row.reshape(b, s // DSA_KEY_TILE, DSA_KEY_TILE), axis=-1)
    consecutive = jnp.all(pos_row[:, 0, 1:] - pos_row[:, 0, :-1] == 1, axis=-1).astype(I32)
    tk = DSA_KEY_TILE
    vmem = (2 * (2 * _nbytes((DSA_HEAD_DIM, hw), BF16) + 2 * _nbytes((s, LANES), BF16) + _nbytes((vw_rows, s), F32)
                 + _nbytes((vw_rows, tq), F32) + _nbytes((s, LANES), I32) + _nbytes((tq, DSA_WIDTH), F32)
                 + _nbytes((tq, DSA_WIDTH), BF16)) + _nbytes((s, tq), I32) + 2 * _nbytes((tk, hw), F32)
            + _nbytes((tk, hw), BF16) + _nbytes((DSA_HEAD_DIM, hw), F32) + _nbytes((DSA_WIDTH, tq), F32)
            + 32 * _nbytes((tk, tq), F32))
    return pl.pallas_call(
        functools.partial(_dsa_kernel, n_sel=n_sel, tq=tq),
        grid=(b, s // tq),
        in_specs=[pl.BlockSpec(memory_space=pltpu.SMEM), pl.BlockSpec(memory_space=pltpu.SMEM),
                  pl.BlockSpec(memory_space=pltpu.SMEM),
                  pl.BlockSpec((SUBLANES, LANES), lambda bi, i: (0, 0)),
                  pl.BlockSpec(toeplitz.shape, lambda bi, i: (0, 0, 0, 0)),
                  pl.BlockSpec((1, 1, DSA_HEAD_DIM, hw), lambda bi, i: (bi, i, 0, 0)),
                  pl.BlockSpec((1, 1, IDX_DIM, hw), lambda bi, i: (bi, i, 0, 0)),
                  pl.BlockSpec((1, s, DSA_HEAD_DIM), lambda bi, i: (bi, 0, 0)),
                  pl.BlockSpec((1, s, IDX_DIM), lambda bi, i: (bi, 0, 0)),
                  pl.BlockSpec((1, vw_rows, s), lambda bi, i: (bi, 0, 0)),
                  pl.BlockSpec((1, IDX_HEADS, tq), lambda bi, i: (bi, 0, i)),
                  pl.BlockSpec((1, s, 1), lambda bi, i: (bi, 0, 0)),
                  pl.BlockSpec((1, 1, tq), lambda bi, i: (bi, 0, i)),
                  pl.BlockSpec((1, tq, DSA_WIDTH), lambda bi, i: (bi, i, 0))],
        out_specs=pl.BlockSpec((1, tq, DSA_WIDTH), lambda bi, i: (bi, i, 0)),
        out_shape=jax.ShapeDtypeStruct((b, s, DSA_WIDTH), BF16),
        scratch_shapes=[pltpu.VMEM((s, tq), I32),
                        pltpu.VMEM((s, tq), I16), pltpu.VMEM((s, tq), I16),
                        pltpu.VMEM((tk, hw), F32), pltpu.VMEM((tk, hw), F32),
                        pltpu.VMEM((tk, hw), F32),
                        pltpu.VMEM((DSA_HEAD_DIM, hw), F32),
                        pltpu.VMEM((DSA_WIDTH, tq), F32),
                        pltpu.VMEM((SUBLANES, hw), F32), pltpu.VMEM((SUBLANES, hw), F32),
                        pltpu.VMEM((SUBLANES, tq), I32)],
        compiler_params=_compiler_params(("parallel", "arbitrary"), vmem),
        name="dsa",
    )(q_min, k_max, consecutive, tbl_t, toeplitz, qcw, qiw, kc, ki, v1, wi, pos_col, pos_row, gate)


def _merge_kernel(oa_ref, ob_ref, oc_ref, g_ref, x_ref, p_ref, wa_ref, wb_ref, wc_ref, wo_ref, wpg_ref, wple_ref,
                  gn_ref, o_ref, *, final):
    d = D_MODEL
    ya = jnp.dot(oa_ref[0], wa_ref[0], preferred_element_type=F32)
    yb = jnp.dot(ob_ref[0], wb_ref[0], preferred_element_type=F32)
    yc = jnp.dot(oc_ref[0], wc_ref[0], preferred_element_type=F32)
    merged = g_ref[0, :, 0:d] * ya + g_ref[0, :, d:2 * d] * yb + g_ref[0, :, 2 * d:3 * d] * yc
    x1 = x_ref[0] + jnp.dot(merged.astype(BF16), wo_ref[0], preferred_element_type=F32)
    ple_gate = _sigmoid(jnp.dot(x1.astype(BF16), wpg_ref[0], preferred_element_type=F32))
    x2 = x1 + ple_gate * jnp.dot(p_ref[0, 0].astype(BF16), wple_ref[0], preferred_element_type=F32)
    o_ref[0] = _rms(x2, gn_ref[...]) if final else x2


def _merge(oa, ob, oc, gates, x, p, layer, wa, wb, wc, wo, wpg, wple, final_gain, final, tm=512):
    b, s, d = x.shape
    full = lambda a: pl.BlockSpec((1,) + a.shape[1:], lambda bi, i: (layer, 0, 0), pipeline_mode=pl.Buffered(1))
    row = lambda a: pl.BlockSpec((1, tm, a.shape[2]), lambda bi, i: (bi, i, 0))
    weights = (wa, wb, wc, wo, wpg, wple)
    acts = (oa, ob, oc, gates, x)
    vmem = (sum(_nbytes(a.shape[1:], a.dtype) for a in weights)
            + 2 * (sum(_nbytes((tm, a.shape[2]), a.dtype) for a in acts) + _nbytes((tm, p.shape[3]), p.dtype)
                   + 2 * _nbytes((tm, d), F32)) + 8 * _nbytes((tm, d), F32))
    return pl.pallas_call(
        functools.partial(_merge_kernel, final=final),
        grid=(b, s // tm),
        in_specs=[row(a) for a in acts] + [pl.BlockSpec((1, 1, tm, p.shape[3]), lambda bi, i: (layer, bi, i, 0))]
        + [full(a) for a in weights] + [pl.BlockSpec((1, d), lambda bi, i: (0, 0))],
        out_specs=pl.BlockSpec((1, tm, d), lambda bi, i: (bi, i, 0)),
        out_shape=jax.ShapeDtypeStruct((b, s, d), F32),
        compiler_params=_compiler_params(("parallel", "parallel"), vmem),
        name="merge",
    )(*acts, p, *weights, final_gain.astype(F32).reshape(1, d))


def _rotate_half_cols(w):
    half = w.shape[-1] // 2
    return jnp.concatenate([-w[..., half:], w[..., :half]], axis=-1)


def _prep_weights(w_in, w_uq, w_ukv):
    depth = w_in.shape[0]
    seg = lambda name: w_in[:, :, SPLIT[name][0]:SPLIT[name][1]]
    z = lambda *shape: jnp.zeros((depth,) + shape, w_in.dtype)
    d = D_MODEL
    kr = seg("k_rope")
    rope_lo = MLA_NOPE
    rope_pad = MLA_QK_PAD - MLA_NOPE - MLA_ROPE
    w = {}
    w["mla_in"] = jnp.concatenate(
        [seg("c_q"), seg("c_kv"), z(d, rope_lo), kr, z(d, rope_pad), z(d, rope_lo), _rotate_half_cols(kr), z(d, rope_pad)], axis=-1)
    w["gate_a"] = seg("gate_a")
    w["z"] = seg("z")
    w["xbc"] = seg("xbc")
    w["small"] = jnp.concatenate([seg("dt"), z(d, LANES - SSM_HEADS)], axis=-1)
    w["k_c"] = seg("k_c")
    w["k_idx"] = seg("k_idx")
    w["gate_c"] = seg("gate_c")
    w["merge"] = seg("merge")
    w["q_c_t"] = jnp.swapaxes(seg("q_c") * (DSA_HEAD_DIM ** -0.5 * LOG2E), 1, 2)
    w["q_idx_t"] = jnp.swapaxes(seg("q_idx"), 1, 2)
    w["vw_t"] = jnp.swapaxes(jnp.concatenate([seg("v_c"), seg("w_idx")], axis=-1), 1, 2)
    w = {
        "proj": jnp.concatenate([w["mla_in"]] + [w[seg[0]] for seg in PROJ_SEGMENTS], axis=-1).astype(BF16),
        "proj_t": jnp.concatenate([w["q_c_t"], w["q_idx_t"], w["vw_t"]], axis=1).astype(BF16),
    }

    uq =w_uq.reshape(depth, MLA_Q_LORA, MLA_HEADS, MLA_NOPE + MLA_ROPE)
    nope, rope = uq[..., :MLA_NOPE], uq[..., MLA_NOPE:]
    zq = lambda width: jnp.zeros((depth, MLA_Q_LORA, MLA_HEADS, width), w_uq.dtype)
    hq = MLA_HEADS * MLA_QK_PAD
    uq1 = jnp.concatenate([nope, rope, zq(rope_pad)], axis=-1).reshape(depth, MLA_Q_LORA, hq)
    uq2 = _rotate_half_cols(rope).reshape(depth, MLA_Q_LORA, MLA_HEADS * MLA_ROPE)
    w["uq1_t"] = jnp.swapaxes(uq1, 1, 2).astype(BF16)
    w["uq2_t"] = jnp.swapaxes(uq2, 1, 2).astype(BF16)
    ukv = w_ukv.reshape(depth, MLA_KV_LORA, MLA_HEADS, MLA_NOPE + MLA_V)
    zk = jnp.zeros((depth, MLA_KV_LORA, MLA_HEADS, MLA_QK_PAD - MLA_NOPE), w_ukv.dtype)
    w["uk"] = jnp.concatenate([ukv[..., :MLA_NOPE], zk], axis=-1).reshape(depth, MLA_KV_LORA, hq).astype(BF16)
    w["uv_t"] = jnp.swapaxes(ukv[..., MLA_NOPE:].reshape(depth, MLA_KV_LORA, MLA_WIDTH), 1, 2).astype(BF16)
    return w


def _rope_multipliers(positions):
    b, s = positions.shape
    inv_freq = 1.0 / (ROPE_THETA ** (jnp.arange(0, MLA_ROPE, 2, dtype=F32) / MLA_ROPE))
    ang = positions.astype(F32)[..., None] * inv_freq
    cos, sin = jnp.cos(ang), jnp.sin(ang)
    pad = MLA_QK_PAD - MLA_NOPE - MLA_ROPE
    m1 = jnp.concatenate([jnp.ones((b, s, MLA_NOPE), F32), cos, cos, jnp.zeros((b, s, pad), F32)], axis=-1)
    m2 = jnp.concatenate([jnp.zeros((b, s, MLA_NOPE), F32), sin, sin, jnp.zeros((b, s, pad), F32)], axis=-1)
    return m1, m2, jnp.swapaxes(m1, 1, 2), jnp.swapaxes(m2, 1, 2)


def kernel(x, p, positions, norm_g, w_in, mla_q_norm, w_uq, mla_kv_norm, w_ukv, conv_w, conv_b, dt_bias, a_log, d_skip, ssm_norm, w_br_a, w_br_b, w_br_c, w_out, rel_bias, w_ple, w_ple_gate, final_norm):
    b, s, d = x.shape
    depth = w_in.shape[0]
    w = _prep_weights(w_in, w_uq, w_ukv)
    m1, m2, m1t, m2t = _rope_multipliers(positions)
    pos_col = positions.astype(I32).reshape(b, s, 1)
    pos_row = positions.astype(I32).reshape(b, 1, s)
    wbr_a, wbr_b, wbr_c = w_br_a.astype(BF16), w_br_b.astype(BF16), w_br_c.astype(BF16)
    wo, wpg, wple = w_out.astype(BF16), w_ple_gate.astype(BF16), w_ple.astype(BF16)

    q_norm = mla_q_norm.astype(F32).reshape(depth, 1, -1)
    kv_norm = mla_kv_norm.astype(F32).reshape(depth, 1, -1)

    gains = norm_g.astype(F32).reshape(depth, 1, d)
    bias_tables = _dsa_bias_tables(rel_bias)
    ssd_params, ssd_expands = _ssd_params(conv_w, conv_b, dt_bias, a_log, d_skip, ssm_norm)
    x = x.astype(F32)
    for i in range(depth):
        pr = _proj_all(x, i, gains, w["proj"], w["proj_t"], (m1, m2, m1t, m2t), q_norm, kv_norm,
                       w["uq1_t"], w["uq2_t"], w["uk"], w["uv_t"], DSA_Q_BLOCK)
        o_a = _mla_attn(pr["mla_q_t"], pr["mla_k"], pr["mla_v_t1"], pr["gate_a"])
        o_b = _ssd(pr["xbc"], pr["z"], pr["small"], i, ssd_params, ssd_expands)
        o_c = _dsa(bias_tables, pr["q_c_wide"], pr["q_idx_wide"], pr["k_c"], pr["k_idx"], pr["v_t1"], pr["w_idx_t"], pos_col, pos_row,
                   pr["gate_c"], DSA_Q_BLOCK)
        x = _merge(o_a, o_b, o_c, pr["merge"], x, p, i, wbr_a, wbr_b, wbr_c, wo, wpg, wple, final_norm,
                   final=i == depth - 1)
    return x
```

```python
import functools
import math

import jax
import jax.numpy as jnp
from jax import lax
from jax.experimental import pallas as pl
from jax.experimental.pallas import tpu as pltpu

F32 = jnp.float32
BF16 = jnp.bfloat16
I32 = jnp.int32
I16 = jnp.int16
HALF16 = 1 << 15

D_MODEL = 1024
NORM_EPS = 1e-6

MLA_HEADS = 8
MLA_NOPE = 64
MLA_ROPE = 32
MLA_V = 64
MLA_Q_LORA = 384
MLA_KV_LORA = 256
MLA_WIDTH = MLA_HEADS * MLA_V
ROPE_THETA = 10000.0
MLA_QK_PAD = 128
LOG2E = math.log2(math.e)
MLA_Q_SCALE = (MLA_NOPE + MLA_ROPE) ** -0.5 * LOG2E

SSM_HEADS = 16
SSM_HEAD_DIM = 64
SSM_INNER = SSM_HEADS * SSM_HEAD_DIM
SSM_GROUPS = 2
SSM_STATE = 128
SSM_CONV = 4
SSM_CHUNK = 128
SSM_CONV_DIM = SSM_INNER + 2 * SSM_GROUPS * SSM_STATE
SSM_HEADS_PER_GROUP = SSM_HEADS // SSM_GROUPS
ONES_ROWS = 16
MLA_V1_ROWS = MLA_V + ONES_ROWS
SSM_CONV_TAIL = 16

DSA_HEADS = 8
DSA_HEAD_DIM = 64
DSA_WIDTH = DSA_HEADS * DSA_HEAD_DIM
IDX_HEADS = 8
IDX_DIM = 64
TOPK_MAX = 256
DSA_KEY_TILE = 128
DSA_Q_BLOCK = 256

REL_BUCKETS = 32
REL_MAX_DIST = 128
N_BRANCHES = 3

LANES = 128
SUBLANES = 8
VMEM_LIMIT_CAP = 56 * 1024 * 1024
VMEM_LIMIT_FLOOR = 32 * 1024 * 1024
INT_MIN = -(2 ** 31)
NEG_BIG = -1e30

SPLIT_SIZES = (
    MLA_Q_LORA, MLA_KV_LORA, MLA_ROPE, MLA_WIDTH, SSM_INNER, SSM_CONV_DIM, SSM_HEADS, DSA_WIDTH,
    DSA_HEAD_DIM, DSA_HEAD_DIM, IDX_HEADS * IDX_DIM, IDX_DIM, IDX_HEADS, DSA_WIDTH,
    N_BRANCHES * D_MODEL,
)
SPLIT_NAMES = ("c_q", "c_kv", "k_rope", "gate_a", "z", "xbc", "dt", "q_c", "k_c", "v_c", "q_idx",
               "k_idx", "w_idx", "gate_c", "merge")


def _split_bounds():
    out, off = {}, 0
    for name, size in zip(SPLIT_NAMES, SPLIT_SIZES, strict=True):
        out[name] = (off, off + size)
        off += size
    return out


SPLIT = _split_bounds()


def _t5_large_thresholds():
    exact = REL_BUCKETS // 2
    thr = []
    for j in range(1, REL_BUCKETS - exact):
        thr.append(int(math.ceil(exact * (REL_MAX_DIST / exact) ** (j / (REL_BUCKETS - exact)) - 1e-9)))
    return tuple(thr)


T5_EXACT = REL_BUCKETS // 2
T5_LARGE_THR = _t5_large_thresholds()
T5_FAR = T5_LARGE_THR[-1]


def _compiler_params(semantics, vmem_bytes):
    limit = int(min(VMEM_LIMIT_CAP, max(VMEM_LIMIT_FLOOR, vmem_bytes)))
    return pltpu.CompilerParams(dimension_semantics=semantics, vmem_limit_bytes=limit)


def _nbytes(shape, dtype):
    return math.prod(shape) * jnp.dtype(dtype).itemsize


def _sigmoid(x):
    return 1.0 / (1.0 + jnp.exp(-x))


def _rms(x, g):
    return x * lax.rsqrt(jnp.mean(x * x, axis=-1, keepdims=True) + NORM_EPS) * g


MLA_IN_WIDTH = MLA_Q_LORA + MLA_KV_LORA + 2 * LANES
PROJ_SEGMENTS = (
    ("small", LANES, None, F32),
    ("gate_a", MLA_WIDTH, "silu", BF16),
    ("gate_c", DSA_WIDTH, "silu", BF16),
    ("z", SSM_INNER, "silu", BF16),
    ("xbc", SSM_CONV_DIM, None, BF16),
    ("merge", N_BRANCHES * D_MODEL, "sigmoid", BF16),
    ("k_c", DSA_HEAD_DIM, None, BF16),
    ("k_idx", IDX_DIM, None, BF16),
)
PROJ_CHUNK = 512
PROJ_T_ROWS = (DSA_WIDTH, IDX_HEADS * IDX_DIM, DSA_HEAD_DIM + IDX_HEADS)


def _proj_all_kernel(x_ref, g_ref, w_ref, wt_ref, m1_ref, m2_ref, m1t_ref, m2t_ref, qn_ref, kvn_ref, wq1t_ref, wq2t_ref,
                     wk_ref, wvt_ref, *out_refs):
    h = _rms(x_ref[0], g_ref[0]).astype(BF16)
    n_seg = len(PROJ_SEGMENTS)
    first_name, first_width, first_act, first_dtype = PROJ_SEGMENTS[0]
    assert first_act is None and first_dtype == F32 and (MLA_IN_WIDTH + first_width) % (2 * LANES) == 0
    a = jnp.dot(h, w_ref[0, :, 0:MLA_IN_WIDTH + first_width], preferred_element_type=F32)
    out_refs[0][0] = a[:, MLA_IN_WIDTH:]
    _mla_prep(a[:, :MLA_IN_WIDTH], m1_ref, m2_ref, m1t_ref, m2t_ref, qn_ref, kvn_ref, wq1t_ref, wq2t_ref, wk_ref, wvt_ref,
              *out_refs[n_seg + 4:])
    off = MLA_IN_WIDTH + first_width
    for (_, width, act, _), o_ref in zip(PROJ_SEGMENTS[1:], out_refs[1:n_seg], strict=True):
        for c in range(0, width, PROJ_CHUNK):
            cw = min(PROJ_CHUNK, width - c)
            y = jnp.dot(h, w_ref[0, :, off + c:off + c + cw], preferred_element_type=F32)
            if act == "silu":
                y = y * _sigmoid(y)
            elif act == "sigmoid":
                y = _sigmoid(y)
            o_ref[0, :, c:c + cw] = y.astype(o_ref.dtype)
        off += width
    qcw_ref, qiw_ref, v1_ref, wi_ref = out_refs[n_seg:n_seg + 4]
    nt = (((1,), (1,)), ((), ()))
    tq = h.shape[0]
    row = 0
    for o_ref, heads, dim in ((qcw_ref, DSA_HEADS, DSA_HEAD_DIM), (qiw_ref, IDX_HEADS, IDX_DIM)):
        y = lax.dot_general(wt_ref[0, row:row + heads * dim, :], h, nt, preferred_element_type=F32)
        for hd in range(heads):
            o_ref[0, 0, :, hd * tq:(hd + 1) * tq] = y[hd * dim:(hd + 1) * dim, :].astype(o_ref.dtype)
        row += heads * dim
    vw = lax.dot_general(wt_ref[0, row:row + PROJ_T_ROWS[2], :], h, nt, preferred_element_type=F32)
    v1_ref[0, 0:DSA_HEAD_DIM, :] = vw[0:DSA_HEAD_DIM, :].astype(v1_ref.dtype)
    v1_ref[0, DSA_HEAD_DIM:, :] = jnp.ones((ONES_ROWS, tq), v1_ref.dtype)
    wi_ref[0] = vw[DSA_HEAD_DIM:, :]


def _proj_all(x, layer, gains, w, wt, rope, qn, kvn, wq1t, wq2t, wk, wvt, tq):
    b, s, k = x.shape
    n_all = MLA_IN_WIDTH + sum(seg[1] for seg in PROJ_SEGMENTS)
    hq = MLA_HEADS * MLA_QK_PAD
    v1_rows = DSA_HEAD_DIM + ONES_ROWS
    out_shapes = [jax.ShapeDtypeStruct((b, s, width), dt) for _, width, _, dt in PROJ_SEGMENTS]
    out_specs = [pl.BlockSpec((1, tq, width), lambda bi, i: (bi, i, 0)) for _, width, _, _ in PROJ_SEGMENTS]
    for heads, dim in ((DSA_HEADS, DSA_HEAD_DIM), (IDX_HEADS, IDX_DIM)):
        out_shapes.append(jax.ShapeDtypeStruct((b, s // tq, dim, heads * tq), BF16))
        out_specs.append(pl.BlockSpec((1, 1, dim, heads * tq), lambda bi, i: (bi, i, 0, 0)))
    for rows, dt in ((v1_rows, BF16), (IDX_HEADS, F32)):
        out_shapes.append(jax.ShapeDtypeStruct((b, rows, s), dt))
        out_specs.append(pl.BlockSpec((1, rows, tq), lambda bi, i: (bi, 0, i)))
    for rows in (hq, None, MLA_HEADS * MLA_V1_ROWS):
        if rows is None:
            out_shapes.append(jax.ShapeDtypeStruct((b, s, hq), BF16))
            out_specs.append(pl.BlockSpec((1, tq, hq), lambda bi, i: (bi, i, 0)))
        else:
            out_shapes.append(jax.ShapeDtypeStruct((b, rows, s), BF16))
            out_specs.append(pl.BlockSpec((1, rows, tq), lambda bi, i: (bi, 0, i)))
    resident = pl.Buffered(1)
    weights = (w, wt, qn, kvn, wq1t, wq2t, wk, wvt)
    layer_spec = lambda a: pl.BlockSpec((1,) + a.shape[1:], lambda bi, i: (layer,) + (0,) * (a.ndim - 1), pipeline_mode=resident)
    m1, m2, m1t, m2t = rope
    vmem = (sum(_nbytes(a.shape[1:], a.dtype) for a in weights) + 4 * _nbytes((tq, k), F32) + 8 * _nbytes((tq, LANES), F32)
            + 2 * sum(_nbytes((tq, width), dt) for _, width, _, dt in PROJ_SEGMENTS)
            + 2 * (2 * _nbytes((DSA_HEAD_DIM, DSA_HEADS * tq), BF16) + _nbytes((v1_rows + IDX_HEADS, tq), F32))
            + 2 * (2 * _nbytes((tq, hq), BF16) + _nbytes((MLA_HEADS * MLA_V1_ROWS, tq), BF16))
            + 8 * _nbytes((tq, PROJ_CHUNK), F32) + 6 * _nbytes((tq, hq), F32))
    outs = pl.pallas_call(
        _proj_all_kernel,
        grid=(b, s // tq),
        in_specs=[pl.BlockSpec((1, tq, k), lambda bi, i: (bi, i, 0)), layer_spec(gains), layer_spec(w), layer_spec(wt),
                  pl.BlockSpec((1, tq, LANES), lambda bi, i: (bi, i, 0)), pl.BlockSpec((1, tq, LANES), lambda bi, i: (bi, i, 0)),
                  pl.BlockSpec((1, LANES, tq), lambda bi, i: (bi, 0, i)), pl.BlockSpec((1, LANES, tq), lambda bi, i: (bi, 0, i))]
        + [layer_spec(a) for a in weights[2:]],
        out_specs=out_specs,
        out_shape=out_shapes,
        compiler_params=_compiler_params(("parallel", "parallel"), vmem),
        name="proj_all",
    )(x, gains, w, wt, m1, m2, m1t, m2t, *weights[2:])
    names = [seg[0] for seg in PROJ_SEGMENTS] + ["q_c_wide", "q_idx_wide", "v_t1", "w_idx_t", "mla_q_t", "mla_k", "mla_v_t1"]
    return dict(zip(names, outs, strict=True))


def _mla_prep(a, m1_ref, m2_ref, m1t_ref, m2t_ref, qn_ref, kvn_ref, wq1t_ref, wq2t_ref, wk_ref, wvt_ref, qt_ref, k_ref, vt_ref):
    c_q = a[:, :MLA_Q_LORA]
    c_kv = a[:, MLA_Q_LORA:MLA_Q_LORA + MLA_KV_LORA]
    kr1 = a[:, MLA_Q_LORA + MLA_KV_LORA:MLA_Q_LORA + MLA_KV_LORA + LANES]
    kr2 = a[:, MLA_Q_LORA + MLA_KV_LORA + LANES:]
    cqn = _rms(c_q, qn_ref[0]).astype(BF16)
    ckvn = _rms(c_kv, kvn_ref[0]).astype(BF16)
    nt = (((1,), (1,)), ((), ()))
    qa_t = lax.dot_general(wq1t_ref[0], cqn, nt, preferred_element_type=F32)
    qb_t = lax.dot_general(wq2t_ref[0], cqn, nt, preferred_element_type=F32)
    kn = jnp.dot(ckvn, wk_ref[0], preferred_element_type=F32)
    kr = kr1 * m1_ref[0] + kr2 * m2_ref[0]
    rope = slice(MLA_NOPE, MLA_NOPE + MLA_ROPE)
    cos_t = m1t_ref[0, rope, :] * MLA_Q_SCALE
    sin_t = m2t_ref[0, rope, :] * MLA_Q_SCALE
    pad_rows = jnp.zeros((MLA_QK_PAD - MLA_NOPE - MLA_ROPE, qa_t.shape[1]), qt_ref.dtype)
    for h in range(MLA_HEADS):
        base = h * MLA_QK_PAD
        sl = slice(base, base + MLA_QK_PAD)
        qt_ref[0, base:base + MLA_NOPE, :] = (qa_t[base:base + MLA_NOPE, :] * MLA_Q_SCALE).astype(qt_ref.dtype)
        qt_ref[0, base + MLA_NOPE:base + MLA_NOPE + MLA_ROPE, :] = (
            qa_t[base + MLA_NOPE:base + MLA_NOPE + MLA_ROPE, :] * cos_t
            + qb_t[h * MLA_ROPE:(h + 1) * MLA_ROPE, :] * sin_t).astype(qt_ref.dtype)
        qt_ref[0, base + MLA_NOPE + MLA_ROPE:base + MLA_QK_PAD, :] = pad_rows
        k_ref[0, :, sl] = (kn[:, sl] + kr).astype(k_ref.dtype)
    v_t = lax.dot_general(wvt_ref[0], ckvn, nt, preferred_element_type=F32)
    ones = jnp.ones((ONES_ROWS, v_t.shape[1]), vt_ref.dtype)
    for h in range(MLA_HEADS):
        base = h * MLA_V1_ROWS
        vt_ref[0, base:base + MLA_V, :] = v_t[h * MLA_V:(h + 1) * MLA_V, :].astype(vt_ref.dtype)
        vt_ref[0, base + MLA_V:base + MLA_V1_ROWS, :] = ones


def _mla_attn_kernel(qt_ref, k_ref, vt_ref, g_ref, o_ref, m_ref, l_ref, acc_ref, s0_ref, s1_ref, *, tq, tk):
    assert tq == 2 * tk
    qi = pl.program_id(1)
    krow = lax.broadcasted_iota(I32, (tk, tq), 0)
    qcol = lax.broadcasted_iota(I32, (tk, tq), 1)
    tiles_per_q = tq // tk
    m_ref[...] = jnp.full(m_ref.shape, -jnp.inf, F32)
    l_ref[...] = jnp.zeros(l_ref.shape, F32)
    acc_ref[...] = jnp.zeros(acc_ref.shape, F32)

    staged = (s0_ref, s1_ref)
    n_tiles = (qi + 1) * tiles_per_q

    def head_lanes(h):
        return slice(h * tq, (h + 1) * tq)

    def stage(j, dst_ref):
        ks = pl.multiple_of(j * tk, tk)
        for h in range(MLA_HEADS):
            k = k_ref[0, pl.ds(ks, tk), h * MLA_QK_PAD:(h + 1) * MLA_QK_PAD]
            dst_ref[:, head_lanes(h)] = jnp.dot(k, qt_ref[0, h * MLA_QK_PAD:(h + 1) * MLA_QK_PAD, :],
                                                preferred_element_type=F32)

    def tile(j, cur_ref, next_ref, diagonal):
        ks = pl.multiple_of(j * tk, tk)
        stage(jnp.minimum(j + 1, n_tiles - 1), next_ref)
        m_all = m_ref[...]
        l_all = l_ref[...]
        if diagonal:
            causal = ks + krow <= qi * tq + qcol
        for h in range(MLA_HEADS):
            s = cur_ref[:, head_lanes(h)]
            if diagonal:
                s = jnp.where(causal, s, -jnp.inf)
            m_old = m_all[h:h + 1, :]
            m_new = jnp.maximum(m_old, jnp.max(s, axis=0, keepdims=True))
            alpha = jnp.exp2(m_old - m_new)
            p = jnp.exp2((s - m_new).astype(BF16))
            m_ref[h:h + 1, :] = m_new
            pv = jnp.dot(vt_ref[0, h * MLA_V1_ROWS:(h + 1) * MLA_V1_ROWS, pl.ds(ks, tk)], p, preferred_element_type=F32)
            hs = slice(h * MLA_V, (h + 1) * MLA_V)
            acc_ref[hs, :] = alpha * acc_ref[hs, :] + pv[0:MLA_V, :]
            l_ref[h:h + 1, :] = alpha * l_all[h:h + 1, :] + pv[MLA_V:MLA_V + 1, :]

    stage(0, staged[0])

    def full_pair(pi, carry):
        for half in range(2):
            tile(2 * pi + half, staged[half], staged[1 - half], diagonal=False)
        return carry

    lax.fori_loop(0, qi, full_pair, 0)
    for half in range(2):
        tile(2 * qi + half, staged[half], staged[1 - half], diagonal=True)
    for h in range(MLA_HEADS):
        hs = slice(h * MLA_V, (h + 1) * MLA_V)
        acc_ref[hs, :] = acc_ref[hs, :] * (1.0 / l_ref[h:h + 1, :])
    o_ref[0] = (acc_ref[...].T * g_ref[0]).astype(o_ref.dtype)


def _mla_attn(qt, k, vt, gate, tq=256, tk=128):
    b, s, hq = k.shape
    vmem = 2 * (_nbytes((hq, tq), BF16) + _nbytes((s, hq), BF16) + _nbytes((MLA_WIDTH, s), BF16)
                + _nbytes((tq, MLA_WIDTH), F32) + _nbytes((tq, MLA_WIDTH), BF16)) + _nbytes((MLA_WIDTH, tq), F32) + 16 * _nbytes((tk, tq), F32)
    return pl.pallas_call(
        functools.partial(_mla_attn_kernel, tq=tq, tk=tk),
        grid=(b, s // tq),
        in_specs=[pl.BlockSpec((1, hq, tq), lambda bi, i: (bi, 0, i)),
                  pl.BlockSpec((1, s, hq), lambda bi, i: (bi, 0, 0)),
                  pl.BlockSpec((1, MLA_HEADS * MLA_V1_ROWS, s), lambda bi, i: (bi, 0, 0)),
                  pl.BlockSpec((1, tq, MLA_WIDTH), lambda bi, i: (bi, i, 0))],
        out_specs=pl.BlockSpec((1, tq, MLA_WIDTH), lambda bi, i: (bi, i, 0)),
        out_shape=jax.ShapeDtypeStruct((b, s, MLA_WIDTH), BF16),
        scratch_shapes=[pltpu.VMEM((MLA_HEADS, tq), F32), pltpu.VMEM((MLA_HEADS, tq), F32),
                        pltpu.VMEM((MLA_WIDTH, tq), F32),
                        pltpu.VMEM((tk, MLA_HEADS * tq), F32), pltpu.VMEM((tk, MLA_HEADS * tq), F32)],
        compiler_params=_compiler_params(("parallel", "arbitrary"), vmem),
        name="mla_attn",
    )(qt, k, vt, gate)


def _ssd_kernel(xbc_ref, xprev_ref, zs_ref, sm_ref, cw_ref, cb_ref, dtb_ref, alog_ref, dskf_ref, nrm_ref, exp16_ref, exp32_ref,
                o_ref, state_ref, y_ref):
    c = pl.program_id(1)
    q = SSM_CHUNK

    @pl.when(c == 0)
    def _():
        state_ref[...] = jnp.zeros(state_ref.shape, F32)

    x_in = xbc_ref[0]
    tail = xprev_ref.shape[1]
    prev = jnp.where(c > 0, xprev_ref[0].astype(F32), 0.0).astype(x_in.dtype)
    win = jnp.concatenate([prev, x_in], axis=0)
    x = x_in.astype(F32)
    wrow = lax.broadcasted_iota(I32, (q, tail + q), 0)
    wcol = lax.broadcasted_iota(I32, (q, tail + q), 1)
    acc = cb_ref[0] + cw_ref[0, SSM_CONV - 1:SSM_CONV, :] * x
    for j in range(1, SSM_CONV):
        pick = jnp.where(wcol == wrow + (tail - j), 1.0, 0.0).astype(win.dtype)
        acc = acc + cw_ref[0, SSM_CONV - 1 - j:SSM_CONV - j, :] * jnp.dot(pick, win, preferred_element_type=F32)
    half = 0.5 * acc
    xc = half + half * jnp.tanh(half)

    xs = xc[:, :SSM_INNER]
    bm = xc[:, SSM_INNER:SSM_INNER + SSM_GROUPS * SSM_STATE]
    cm = xc[:, SSM_INNER + SSM_GROUPS * SSM_STATE:]

    pre = sm_ref[0] + dtb_ref[0]
    dt = jnp.maximum(pre, 0.0) + jnp.log1p(jnp.exp(-jnp.abs(pre)))
    a = -jnp.exp(alog_ref[0])
    row = lax.broadcasted_iota(I32, (q, q), 0)
    col = lax.broadcasted_iota(I32, (q, q), 1)
    lower = row >= col
    cum = jnp.dot(jnp.where(lower, 1.0, 0.0).astype(F32), dt * a, preferred_element_type=F32,
                  precision=lax.Precision.HIGHEST)
    cum_t = cum.T
    cum_last = cum[q - 1:q, :]
    per_head = jnp.concatenate([dt, jnp.exp(cum_last - cum), jnp.exp(cum)], axis=0).astype(BF16)
    spread = jnp.dot(per_head, exp16_ref[...], preferred_element_type=F32)
    dt_full, to_end_full, e_cum_full = spread[0:q], spread[q:2 * q], spread[2 * q:3 * q]
    chunk_decay_full = jnp.dot(jnp.broadcast_to(jnp.exp(cum_last), (SUBLANES, LANES)), exp32_ref[...],
                               preferred_element_type=F32, precision=lax.Precision.HIGHEST)[0:1]
    xdt = xs * dt_full
    xw = (xdt * to_end_full).astype(BF16)
    first_head = lax.broadcasted_iota(I32, (q, LANES), 1) < SSM_HEAD_DIM
    group_w = SSM_HEADS_PER_GROUP * SSM_HEAD_DIM
    heads_per_tile = LANES // SSM_HEAD_DIM

    for g in range(SSM_GROUPS):
        bg32 = bm[:, g * SSM_STATE:(g + 1) * SSM_STATE]
        bg = bg32.astype(BF16)
        cg = cm[:, g * SSM_STATE:(g + 1) * SSM_STATE].astype(BF16)
        cb = lax.dot_general(cg, bg, (((1,), (1,)), ((), ())), preferred_element_type=F32)
        gl = slice(g * group_w, (g + 1) * group_w)
        st = state_ref[g]
        y_off = jnp.dot(cg, st.astype(BF16), preferred_element_type=F32)
        state_ref[g] = st * chunk_decay_full[:, gl] + jnp.dot(bg32.T.astype(BF16), xw[:, gl], preferred_element_type=F32)
        for tile in range(group_w // LANES):
            h0 = g * SSM_HEADS_PER_GROUP + tile * heads_per_tile
            tl = slice(h0 * SSM_HEAD_DIM, h0 * SSM_HEAD_DIM + LANES)
            g_both = []
            for h in range(h0, h0 + heads_per_tile):
                diff = cum[:, h:h + 1] - cum_t[h:h + 1, :]
                decay_in = jnp.exp(jnp.where(lower, diff, -jnp.inf))
                g_both.append((cb * decay_in).astype(BF16))
            xdt_tile = xdt[:, tl]
            stacked = jnp.concatenate([jnp.where(first_head, xdt_tile, 0.0).astype(BF16),
                                       jnp.where(first_head, 0.0, xdt_tile).astype(BF16)], axis=0)
            y_diag = jnp.dot(jnp.concatenate(g_both, axis=1), stacked, preferred_element_type=F32)
            y = y_diag + y_off[:, tile * LANES:(tile + 1) * LANES] * e_cum_full[:, tl] + xs[:, tl] * dskf_ref[0, :, tl]
            y_ref[:, tl] = y * zs_ref[0, :, tl]

    o_ref[0] = _rms(y_ref[...], nrm_ref[0]).astype(o_ref.dtype)


def _ssd_params(conv_w, conv_b, dt_bias, a_log, d_skip, ssm_norm):
    depth = conv_w.shape[0]
    pad = lambda v: jnp.pad(v.astype(F32), ((0, 0), (0, LANES - v.shape[1]))).reshape(depth, 1, LANES)
    expand = (jnp.arange(LANES)[:, None] == jnp.arange(SSM_INNER)[None, :] // SSM_HEAD_DIM).astype(F32)
    d_skip_full = jnp.repeat(d_skip.astype(F32), SSM_HEAD_DIM, axis=1).reshape(depth, 1, SSM_INNER)
    return (conv_w.astype(F32), conv_b.astype(F32).reshape(depth, 1, -1), pad(dt_bias), pad(a_log), d_skip_full,
            ssm_norm.astype(F32).reshape(depth, 1, -1)), (expand.astype(BF16), expand)


def _ssd(xbc, zs, small, layer, params, expands):
    b, s, _ = xbc.shape
    q = SSM_CHUNK
    full = lambda shape: pl.BlockSpec(shape, lambda bi, c: (0,) * len(shape))
    of_layer = lambda a: pl.BlockSpec((1,) + a.shape[1:], lambda bi, c: (layer, 0, 0))
    blk = lambda width: pl.BlockSpec((1, q, width), lambda bi, c: (bi, c, 0))
    group_w = SSM_HEADS_PER_GROUP * SSM_HEAD_DIM
    vmem = (2 * (_nbytes((q, SSM_CONV_DIM), xbc.dtype) + _nbytes((q, SSM_INNER), zs.dtype) + _nbytes((q, LANES), F32)
                 + _nbytes((q, SSM_INNER), BF16) + _nbytes((LANES, SSM_INNER), BF16) + _nbytes((LANES, SSM_INNER), F32))
            + _nbytes((2 * q, SSM_CONV_DIM), xbc.dtype) + _nbytes((SSM_GROUPS, SSM_STATE, group_w), F32)
            + _nbytes((q, SSM_INNER), F32) + 10 * _nbytes((q, SSM_CONV_DIM), F32))
    return pl.pallas_call(
        _ssd_kernel,
        grid=(b, s // q),
        in_specs=[blk(SSM_CONV_DIM),
                  pl.BlockSpec((1, SSM_CONV_TAIL, SSM_CONV_DIM),
                               lambda bi, c: (bi, jnp.maximum(c * (q // SSM_CONV_TAIL) - 1, 0), 0)),
                  blk(SSM_INNER), blk(LANES)] + [of_layer(a) for a in params]
        + [full((LANES, SSM_INNER)), full((LANES, SSM_INNER))],
        out_specs=blk(SSM_INNER),
        out_shape=jax.ShapeDtypeStruct((b, s, SSM_INNER), BF16),
        scratch_shapes=[pltpu.VMEM((SSM_GROUPS, SSM_STATE, group_w), F32),
                        pltpu.VMEM((q, SSM_INNER), F32)],
        compiler_params=_compiler_params(("parallel", "arbitrary"), vmem),
        name="ssd",
    )(xbc, xbc, zs, small, *params, *expands)


def _t5_bucket_of(n):
    steps_per_octave = (REL_BUCKETS - T5_EXACT) / math.log2(REL_MAX_DIST / T5_EXACT)
    x = jnp.maximum(n, T5_EXACT).astype(F32) * (1.0 / T5_EXACT)
    large = T5_EXACT + jnp.floor(jnp.log2(x) * steps_per_octave).astype(I32)
    return jnp.where(n <= T5_EXACT, n, jnp.minimum(large, REL_BUCKETS - 1))


def _dsa_kernel(qmin_ref, kmax_ref, consec_ref, tblt_ref, toe_ref, qcw_ref, qiw_ref, kc_ref, ki_ref, v1_ref, wi_ref, posk_ref, posq_ref, gate_ref, o_ref,
                key_ref, hi_ref, lo_ref, s0_ref, s1_ref, bias_ref, acc_ref, ot_ref, m_ref, l_ref, tie_ref, *, n_sel, tq):
    tk = DSA_KEY_TILE
    qi = pl.program_id(1)
    n_tiles = (qi + 1) * (tq // tk)
    krow = lax.broadcasted_iota(I32, (tk, tq), 0)
    qcol = lax.broadcasted_iota(I32, (tk, tq), 1)
    w = wi_ref[0]
    posq = posq_ref[0]

    def tile_start(kt):
        return pl.multiple_of(kt * tk, tk)

    def causal_of(ks):
        return (ks + krow) <= (qi * tq + qcol)

    def head_lanes(h):
        return slice(h * tq, (h + 1) * tq)

    staged = (s0_ref, s1_ref)

    def stage(k_ref, q_ref, kt, dst_ref):
        dst_ref[...] = jnp.dot(k_ref[0, pl.ds(tile_start(kt), tk), :], q_ref[0, 0], preferred_element_type=F32)

    def tile_pairs(tile_fn, peel_last):
        def pair(pi, last_pair):
            for half in range(2):
                tile_fn(2 * pi + half, staged[half], staged[1 - half], last_pair, last_pair and half == 1)

        def body(pi, carry):
            pair(pi, False)
            return carry

        if peel_last:
            lax.fori_loop(0, n_tiles // 2 - 1, body, 0)
            pair(n_tiles // 2 - 1, True)
        else:
            lax.fori_loop(0, n_tiles // 2, body, 0)

    stage(ki_ref, qiw_ref, 0, staged[0])

    def score_tile(kt, cur_ref, next_ref, last_pair, last_tile):
        ks = tile_start(kt)
        if last_tile:
            stage(kc_ref, qcw_ref, 0, next_ref)
        else:
            stage(ki_ref, qiw_ref, kt + 1, next_ref)
        sc = jnp.zeros((tk, tq), F32)
        for h in range(IDX_HEADS):
            sc = sc + w[h:h + 1, :] * jnp.maximum(cur_ref[:, head_lanes(h)], 0.0)
        sc = jnp.where(sc == 0.0, 0.0, sc)
        bits = lax.bitcast_convert_type(sc, I32)
        key = bits ^ ((bits >> 31) & 0x7FFFFFFF)
        if last_pair:
            key = jnp.where(causal_of(ks), key, INT_MIN)
        key_ref[pl.ds(ks, tk), :] = key
        hi_ref[pl.ds(ks, tk), :] = (key >> 16).astype(I16)
        lo_ref[pl.ds(ks, tk), :] = ((key & 0xFFFF) - HALF16).astype(I16)

    tile_pairs(score_tile, peel_last=True)

    def count16(half_ref, pred):
        def body(pi, cnt):
            for kt in (2 * pi, 2 * pi + 1):
                cnt = cnt + jnp.where(pred(half_ref[pl.ds(tile_start(kt), tk), :]), jnp.int16(1), jnp.int16(0))
            return cnt
        cnt = lax.fori_loop(0, n_tiles // 2, body, jnp.zeros((tk, tq), I16))
        rows = 2 * SUBLANES
        parts = [cnt[r:r + rows, :] for r in range(0, tk, rows)]
        while len(parts) > 1:
            parts = [a + b for a, b in zip(parts[0::2], parts[1::2], strict=True)]
        return jnp.sum(parts[0].astype(I32), axis=0, keepdims=True)

    def nth_largest16(half_ref, rank):
        def bit_step(i, carry):
            cand, reached = carry
            trial = cand | lax.shift_left(jnp.int32(1), 15 - i)
            thr = (trial - HALF16).astype(I16)
            tot = count16(half_ref, lambda v: v >= thr)
            take = tot >= rank
            return jnp.where(take, trial, cand), jnp.where(take, tot, reached)
        init = (jnp.zeros((1, tq), I32), jnp.full((1, tq), n_tiles * tk, I32))
        cand, reached = lax.fori_loop(0, 16, bit_step, init)
        return cand - HALF16, reached

    tau_hi, _ = nth_largest16(hi_ref, n_sel)
    tau_hi16 = tau_hi.astype(I16)
    above_hi = count16(hi_ref, lambda v: v > tau_hi16)

    def mask_low(kt, carry):
        rows = pl.ds(tile_start(kt), tk)
        lo_ref[rows, :] = jnp.where(hi_ref[rows, :] == tau_hi16, lo_ref[rows, :], jnp.int16(-HALF16))
        return carry

    lax.fori_loop(0, n_tiles, mask_low, 0)
    tau_lo, reached_lo = nth_largest16(lo_ref, n_sel - above_hi)
    tau_lo16 = tau_lo.astype(I16)
    tau = lax.shift_left(tau_hi, 16) | (tau_lo + HALF16)
    n_gt = above_hi + count16(lo_ref, lambda v: v > tau_lo16)
    n_ge = above_hi + reached_lo
    need = n_sel - n_gt
    tau_floor = jnp.maximum(tau, INT_MIN + 1)

    m_ref[...] = jnp.full(m_ref.shape, NEG_BIG, F32)
    l_ref[...] = jnp.zeros(l_ref.shape, F32)
    acc_ref[...] = jnp.zeros(acc_ref.shape, F32)
    tie_ref[...] = jnp.zeros(tie_ref.shape, I32)
    lower_incl = jnp.where(lax.broadcasted_iota(I32, (tk, tk), 0) >= lax.broadcasted_iota(I32, (tk, tk), 1),
                           1.0, 0.0).astype(BF16)

    splits =jnp.where((n_ge > n_sel) & (tau != INT_MIN), 1.0, 0.0)
    any_split = jnp.max(splits) > 0.0

    def attend_tile(kt, cur_ref, next_ref, general_bias, tie_split):
        ks = tile_start(kt)
        key = key_ref[pl.ds(ks, tk), :]
        if tie_split:
            eq = key == tau
            prefix = jnp.dot(lower_incl, jnp.where(eq, 1.0, 0.0).astype(BF16), preferred_element_type=F32)
            tie_before = tie_ref[0:1, :]
            tie_rank = tie_before + prefix.astype(I32)
            sel = ((key > tau) | (eq & (tie_rank <= need))) & causal_of(ks)
            tie_ref[0:1, :] = tie_before + prefix[tk - 1:tk, :].astype(I32)
        else:
            sel = key >= tau_floor

        if general_bias == "toeplitz":
            offset = [(qmin_ref[bi, qi] + c * LANES) - (kmax_ref[bi, kt] - (tk - 1)) for c in range(tq // LANES)]
            which = [jnp.where(off == 0, 0, jnp.where(off == tk, 1, 2)) for off in offset]
        elif general_bias == "gather":
            posk = posk_ref[0, pl.ds(ks, tk), :]
            bucket = _t5_bucket_of(jnp.maximum(posq - posk, 0))
            for hp in range(DSA_HEADS // 2):
                table = jnp.broadcast_to(tblt_ref[hp:hp + 1, :], (tk, LANES))
                for c in range(tq // LANES):
                    both = jnp.take_along_axis(table, bucket[:, c * LANES:(c + 1) * LANES], axis=1)
                    lo, hi = 2 * hp * tq + c * LANES, (2 * hp + 1) * tq + c * LANES
                    bias_ref[:, lo:lo + LANES] = lax.bitcast_convert_type(both & jnp.int32(-65536), F32)
                    bias_ref[:, hi:hi + LANES] = lax.bitcast_convert_type(lax.shift_left(both, 16), F32)
        stage(kc_ref, qcw_ref, jnp.minimum(kt + 1, n_tiles - 1), next_ref)
        m_all = m_ref[0:1, :]
        l_all = l_ref[0:1, :]
        v1 = v1_ref[0, :, pl.ds(ks, tk)]
        for h in range(DSA_HEADS):
            hl = head_lanes(h)
            s = cur_ref[:, hl]
            if general_bias == "toeplitz":
                s = s + jnp.concatenate([toe_ref[which[c], h] for c in range(tq // LANES)], axis=1)
            elif general_bias == "gather":
                s = s + bias_ref[:, hl]
            s = jnp.where(sel, s, NEG_BIG)
            m_old = m_all[:, hl]
            m_new = jnp.maximum(m_old, jnp.max(s, axis=0, keepdims=True))
            m_ref[0:1, hl] = m_new
            alpha = jnp.exp2(m_old - m_new)
            p = jnp.exp2((s - m_new).astype(BF16))
            pv = jnp.dot(v1, p, preferred_element_type=F32)
            acc_ref[:, hl] = alpha * acc_ref[:, hl] + pv[0:DSA_HEAD_DIM, :]
            l_ref[0:1, hl] = alpha * l_all[:, hl] + pv[DSA_HEAD_DIM:DSA_HEAD_DIM + 1, :]

    bi = pl.program_id(0)

    def attend(kt, cur_ref, next_ref, last_pair, last_tile):
        far = (qmin_ref[bi, qi] - kmax_ref[bi, kt]) >= T5_FAR
        no_split = jnp.logical_not(any_split)
        near = jnp.logical_and(jnp.logical_not(far), no_split)
        consecutive = consec_ref[bi] == 1

        @pl.when(jnp.logical_and(far, no_split))
        def _():
            attend_tile(kt, cur_ref, next_ref, general_bias=None, tie_split=False)

        @pl.when(jnp.logical_and(near, consecutive))
        def _():
            attend_tile(kt, cur_ref, next_ref, general_bias="toeplitz", tie_split=False)

        @pl.when(jnp.logical_and(near, jnp.logical_not(consecutive)))
        def _():
            attend_tile(kt, cur_ref, next_ref, general_bias="gather", tie_split=False)

        @pl.when(any_split)
        def _():
            attend_tile(kt, cur_ref, next_ref, general_bias="gather", tie_split=True)

    tile_pairs(attend, peel_last=False)

    for h in range(DSA_HEADS):
        hl = head_lanes(h)
        ot_ref[h * DSA_HEAD_DIM:(h + 1) * DSA_HEAD_DIM, :] = acc_ref[:, hl] * (1.0 / l_ref[0:1, hl])
    o_ref[0] = (ot_ref[...].T * gate_ref[0]).astype(o_ref.dtype)


DSA_TOEPLITZ_OFFSETS = (0, DSA_KEY_TILE)


def _bias_toeplitz_kernel(tbl_ref, o_ref):
    j = pl.program_id(0)
    tk = DSA_KEY_TILE
    diff = lax.broadcasted_iota(I32, (tk, LANES), 1) - lax.broadcasted_iota(I32, (tk, LANES), 0)
    for slot, offset in enumerate(DSA_TOEPLITZ_OFFSETS):
        @pl.when(j == slot)
        def _(offset=offset):
            bucket = _t5_bucket_of(jnp.maximum(diff + offset, 0))
            for h in range(DSA_HEADS):
                table = jnp.broadcast_to(tbl_ref[h:h + 1, :], (tk, LANES))
                o_ref[0, h] = jnp.take_along_axis(table, bucket, axis=1)

    @pl.when(j == len(DSA_TOEPLITZ_OFFSETS))
    def _():
        o_ref[...] = jnp.zeros(o_ref.shape, F32)


def _bias_toeplitz(tbl_rows):
    n = len(DSA_TOEPLITZ_OFFSETS) + 1
    return pl.pallas_call(
        _bias_toeplitz_kernel,
        grid=(n,),
        in_specs=[pl.BlockSpec((DSA_HEADS, LANES), lambda j: (0, 0))],
        out_specs=pl.BlockSpec((1, DSA_HEADS, DSA_KEY_TILE, LANES), lambda j: (j, 0, 0, 0)),
        out_shape=jax.ShapeDtypeStruct((n, DSA_HEADS, DSA_KEY_TILE, LANES), F32),
        compiler_params=_compiler_params(("parallel",), 0),
        name="bias_toeplitz",
    )(tbl_rows)


def _dsa_bias_tables(rel_bias):
    tbl = rel_bias.astype(F32) * LOG2E
    tbl = tbl - tbl[REL_BUCKETS - 1:REL_BUCKETS, :]
    rows = jnp.pad(tbl.T, ((0, 0), (0, LANES - REL_BUCKETS)))
    bits = lax.bitcast_convert_type(tbl.T.astype(jnp.bfloat16), jnp.uint16).astype(jnp.uint32)
    packed = lax.bitcast_convert_type((bits[0::2] << 16) | bits[1::2], I32)
    packed = jnp.pad(packed, ((0, SUBLANES - DSA_HEADS // 2), (0, LANES - REL_BUCKETS)))
    return packed, _bias_toeplitz(rows)


def _dsa(bias_tables, qcw, qiw, kc, ki, v1, wi, pos_col, pos_row, gate, tq):
    b, s, _ = kc.shape
    n_sel = min(TOPK_MAX, s // 4)
    vw_rows = v1.shape[1]
    hw = DSA_HEADS * tq
    tbl_t, toeplitz = bias_tables
    q_min = jnp.min(pos_row.reshape(b, s // tq, tq), axis=-1)
    k_max = jnp.max(pos_row.reshape(b, s // DSA_KEY_TILE, DSA_KEY_TILE), axis=-1)
    consecutive = jnp.all(pos_row[:, 0, 1:] - pos_row[:, 0, :-1] == 1, axis=-1).astype(I32)
    tk = DSA_KEY_TILE
    vmem = (2 * (2 * _nbytes((DSA_HEAD_DIM, hw), BF16) + 2 * _nbytes((s, LANES), BF16) + _nbytes((vw_rows, s), F32)
                 + _nbytes((vw_rows, tq), F32) + _nbytes((s, LANES), I32) + _nbytes((tq, DSA_WIDTH), F32)
                 + _nbytes((tq, DSA_WIDTH), BF16)) + _nbytes((s, tq), I32) + 2 * _nbytes((tk, hw), F32)
            + _nbytes((tk, hw), BF16) + _nbytes((DSA_HEAD_DIM, hw), F32) + _nbytes((DSA_WIDTH, tq), F32)
            + 32 * _nbytes((tk, tq), F32))
    return pl.pallas_call(
        functools.partial(_dsa_kernel, n_sel=n_sel, tq=tq),
        grid=(b, s // tq),
        in_specs=[pl.BlockSpec(memory_space=pltpu.SMEM), pl.BlockSpec(memory_space=pltpu.SMEM),
                  pl.BlockSpec(memory_space=pltpu.SMEM),
                  pl.BlockSpec((SUBLANES, LANES), lambda bi, i: (0, 0)),
                  pl.BlockSpec(toeplitz.shape, lambda bi, i: (0, 0, 0, 0)),
                  pl.BlockSpec((1, 1, DSA_HEAD_DIM, hw), lambda bi, i: (bi, i, 0, 0)),
                  pl.BlockSpec((1, 1, IDX_DIM, hw), lambda bi, i: (bi, i, 0, 0)),
                  pl.BlockSpec((1, s, DSA_HEAD_DIM), lambda bi, i: (bi, 0, 0)),
                  pl.BlockSpec((1, s, IDX_DIM), lambda bi, i: (bi, 0, 0)),
                  pl.BlockSpec((1, vw_rows, s), lambda bi, i: (bi, 0, 0)),
                  pl.BlockSpec((1, IDX_HEADS, tq), lambda bi, i: (bi, 0, i)),
                  pl.BlockSpec((1, s, 1), lambda bi, i: (bi, 0, 0)),
                  pl.BlockSpec((1, 1, tq), lambda bi, i: (bi, 0, i)),
                  pl.BlockSpec((1, tq, DSA_WIDTH), lambda bi, i: (bi, i, 0))],
        out_specs=pl.BlockSpec((1, tq, DSA_WIDTH), lambda bi, i: (bi, i, 0)),
        out_shape=jax.ShapeDtypeStruct((b, s, DSA_WIDTH), BF16),
        scratch_shapes=[pltpu.VMEM((s, tq), I32),
                        pltpu.VMEM((s, tq), I16), pltpu.VMEM((s, tq), I16),
                        pltpu.VMEM((tk, hw), F32), pltpu.VMEM((tk, hw), F32),
                        pltpu.VMEM((tk, hw), F32),
                        pltpu.VMEM((DSA_HEAD_DIM, hw), F32),
                        pltpu.VMEM((DSA_WIDTH, tq), F32),
                        pltpu.VMEM((SUBLANES, hw), F32), pltpu.VMEM((SUBLANES, hw), F32),
                        pltpu.VMEM((SUBLANES, tq), I32)],
        compiler_params=_compiler_params(("parallel", "arbitrary"), vmem),
        name="dsa",
    )(q_min, k_max, consecutive, tbl_t, toeplitz, qcw, qiw, kc, ki, v1, wi, pos_col, pos_row, gate)


def _merge_kernel(oa_ref, ob_ref, oc_ref, g_ref, x_ref, p_ref, wa_ref, wb_ref, wc_ref, wo_ref, wpg_ref, wple_ref,
                  gn_ref, o_ref, *, final):
    d = D_MODEL
    ya = jnp.dot(oa_ref[0], wa_ref[0], preferred_element_type=F32)
    yb = jnp.dot(ob_ref[0], wb_ref[0], preferred_element_type=F32)
    yc = jnp.dot(oc_ref[0], wc_ref[0], preferred_element_type=F32)
    merged = g_ref[0, :, 0:d] * ya + g_ref[0, :, d:2 * d] * yb + g_ref[0, :, 2 * d:3 * d] * yc
    x1 = x_ref[0] + jnp.dot(merged.astype(BF16), wo_ref[0], preferred_element_type=F32)
    ple_gate = _sigmoid(jnp.dot(x1.astype(BF16), wpg_ref[0], preferred_element_type=F32))
    x2 = x1 + ple_gate * jnp.dot(p_ref[0, 0].astype(BF16), wple_ref[0], preferred_element_type=F32)
    o_ref[0] = _rms(x2, gn_ref[...]) if final else x2


def _merge(oa, ob, oc, gates, x, p, layer, wa, wb, wc, wo, wpg, wple, final_gain, final, tm=512):
    b, s, d = x.shape
    full = lambda a: pl.BlockSpec((1,) + a.shape[1:], lambda bi, i: (layer, 0, 0), pipeline_mode=pl.Buffered(1))
    row = lambda a: pl.BlockSpec((1, tm, a.shape[2]), lambda bi, i: (bi, i, 0))
    weights = (wa, wb, wc, wo, wpg, wple)
    acts = (oa, ob, oc, gates, x)
    vmem = (sum(_nbytes(a.shape[1:], a.dtype) for a in weights)
            + 2 * (sum(_nbytes((tm, a.shape[2]), a.dtype) for a in acts) + _nbytes((tm, p.shape[3]), p.dtype)
                   + 2 * _nbytes((tm, d), F32)) + 8 * _nbytes((tm, d), F32))
    return pl.pallas_call(
        functools.partial(_merge_kernel, final=final),
        grid=(b, s // tm),
        in_specs=[row(a) for a in acts] + [pl.BlockSpec((1, 1, tm, p.shape[3]), lambda bi, i: (layer, bi, i, 0))]
        + [full(a) for a in weights] + [pl.BlockSpec((1, d), lambda bi, i: (0, 0))],
        out_specs=pl.BlockSpec((1, tm, d), lambda bi, i: (bi, i, 0)),
        out_shape=jax.ShapeDtypeStruct((b, s, d), F32),
        compiler_params=_compiler_params(("parallel", "parallel"), vmem),
        name="merge",
    )(*acts, p, *weights, final_gain.astype(F32).reshape(1, d))


def _rotate_half_cols(w):
    half = w.shape[-1] // 2
    return jnp.concatenate([-w[..., half:], w[..., :half]], axis=-1)


def _prep_weights(w_in, w_uq, w_ukv):
    depth = w_in.shape[0]
    seg = lambda name: w_in[:, :, SPLIT[name][0]:SPLIT[name][1]]
    z = lambda *shape: jnp.zeros((depth,) + shape, w_in.dtype)
    d = D_MODEL
    kr = seg("k_rope")
    rope_lo = MLA_NOPE
    rope_pad = MLA_QK_PAD - MLA_NOPE - MLA_ROPE
    w = {}
    w["mla_in"] = jnp.concatenate(
        [seg("c_q"), seg("c_kv"), z(d, rope_lo), kr, z(d, rope_pad), z(d, rope_lo), _rotate_half_cols(kr), z(d, rope_pad)], axis=-1)
    w["gate_a"] = seg("gate_a")
    w["z"] = seg("z")
    w["xbc"] = seg("xbc")
    w["small"] = jnp.concatenate([seg("dt"), z(d, LANES - SSM_HEADS)], axis=-1)
    w["k_c"] = seg("k_c")
    w["k_idx"] = seg("k_idx")
    w["gate_c"] = seg("gate_c")
    w["merge"] = seg("merge")
    w["q_c_t"] = jnp.swapaxes(seg("q_c") * (DSA_HEAD_DIM ** -0.5 * LOG2E), 1, 2)
    w["q_idx_t"] = jnp.swapaxes(seg("q_idx"), 1, 2)
    w["vw_t"] = jnp.swapaxes(jnp.concatenate([seg("v_c"), seg("w_idx")], axis=-1), 1, 2)
    w = {
        "proj": jnp.concatenate([w["mla_in"]] + [w[seg[0]] for seg in PROJ_SEGMENTS], axis=-1).astype(BF16),
        "proj_t": jnp.concatenate([w["q_c_t"], w["q_idx_t"], w["vw_t"]], axis=1).astype(BF16),
    }

    uq =w_uq.reshape(depth, MLA_Q_LORA, MLA_HEADS, MLA_NOPE + MLA_ROPE)
    nope, rope = uq[..., :MLA_NOPE], uq[..., MLA_NOPE:]
    zq = lambda width: jnp.zeros((depth, MLA_Q_LORA, MLA_HEADS, width), w_uq.dtype)
    hq = MLA_HEADS * MLA_QK_PAD
    uq1 = jnp.concatenate([nope, rope, zq(rope_pad)], axis=-1).reshape(depth, MLA_Q_LORA, hq)
    uq2 = _rotate_half_cols(rope).reshape(depth, MLA_Q_LORA, MLA_HEADS * MLA_ROPE)
    w["uq1_t"] = jnp.swapaxes(uq1, 1, 2).astype(BF16)
    w["uq2_t"] = jnp.swapaxes(uq2, 1, 2).astype(BF16)
    ukv = w_ukv.reshape(depth, MLA_KV_LORA, MLA_HEADS, MLA_NOPE + MLA_V)
    zk = jnp.zeros((depth, MLA_KV_LORA, MLA_HEADS, MLA_QK_PAD - MLA_NOPE), w_ukv.dtype)
    w["uk"] = jnp.concatenate([ukv[..., :MLA_NOPE], zk], axis=-1).reshape(depth, MLA_KV_LORA, hq).astype(BF16)
    w["uv_t"] = jnp.swapaxes(ukv[..., MLA_NOPE:].reshape(depth, MLA_KV_LORA, MLA_WIDTH), 1, 2).astype(BF16)
    return w


def _rope_multipliers(positions):
    b, s = positions.shape
    inv_freq = 1.0 / (ROPE_THETA ** (jnp.arange(0, MLA_ROPE, 2, dtype=F32) / MLA_ROPE))
    ang = positions.astype(F32)[..., None] * inv_freq
    cos, sin = jnp.cos(ang), jnp.sin(ang)
    pad = MLA_QK_PAD - MLA_NOPE - MLA_ROPE
    m1 = jnp.concatenate([jnp.ones((b, s, MLA_NOPE), F32), cos, cos, jnp.zeros((b, s, pad), F32)], axis=-1)
    m2 = jnp.concatenate([jnp.zeros((b, s, MLA_NOPE), F32), sin, sin, jnp.zeros((b, s, pad), F32)], axis=-1)
    return m1, m2, jnp.swapaxes(m1, 1, 2), jnp.swapaxes(m2, 1, 2)


def kernel(x, p, positions, norm_g, w_in, mla_q_norm, w_uq, mla_kv_norm, w_ukv, conv_w, conv_b, dt_bias, a_log, d_skip, ssm_norm, w_br_a, w_br_b, w_br_c, w_out, rel_bias, w_ple, w_ple_gate, final_norm):
    b, s, d = x.shape
    depth = w_in.shape[0]
    w = _prep_weights(w_in, w_uq, w_ukv)
    m1, m2, m1t, m2t = _rope_multipliers(positions)
    pos_col = positions.astype(I32).reshape(b, s, 1)
    pos_row = positions.astype(I32).reshape(b, 1, s)
    wbr_a, wbr_b, wbr_c = w_br_a.astype(BF16), w_br_b.astype(BF16), w_br_c.astype(BF16)
    wo, wpg, wple = w_out.astype(BF16), w_ple_gate.astype(BF16), w_ple.astype(BF16)

    q_norm = mla_q_norm.astype(F32).reshape(depth, 1, -1)
    kv_norm = mla_kv_norm.astype(F32).reshape(depth, 1, -1)

    gains = norm_g.astype(F32).reshape(depth, 1, d)
    bias_tables = _dsa_bias_tables(rel_bias)
    ssd_params, ssd_expands = _ssd_params(conv_w, conv_b, dt_bias, a_log, d_skip, ssm_norm)
    x = x.astype(F32)
    for i in range(depth):
        pr = _proj_all(x, i, gains, w["proj"], w["proj_t"], (m1, m2, m1t, m2t), q_norm, kv_norm,
                       w["uq1_t"], w["uq2_t"], w["uk"], w["uv_t"], DSA_Q_BLOCK)
        o_a = _mla_attn(pr["mla_q_t"], pr["mla_k"], pr["mla_v_t1"], pr["gate_a"])
        o_b = _ssd(pr["xbc"], pr["z"], pr["small"], i, ssd_params, ssd_expands)
        o_c = _dsa(bias_tables, pr["q_c_wide"], pr["q_idx_wide"], pr["k_c"], pr["k_idx"], pr["v_t1"], pr["w_idx_t"], pos_col, pos_row,
                   pr["gate_c"], DSA_Q_BLOCK)
        x = _merge(o_a, o_b, o_c, pr["merge"], x, p, i, wbr_a, wbr_b, wbr_c, wo, wpg, wple, final_norm,
                   final=i == depth - 1)
    return x
```

```python
import functools
import math

import jax
import jax.numpy as jnp
from jax import lax
from jax.experimental import pallas as pl
from jax.experimental.pallas import tpu as pltpu

F32 = jnp.float32
BF16 = jnp.bfloat16
I32 = jnp.int32
I16 = jnp.int16
HALF16 = 1 << 15

D_MODEL = 1024
NORM_EPS = 1e-6

MLA_HEADS = 8
MLA_NOPE = 64
MLA_ROPE = 32
MLA_V = 64
MLA_Q_LORA = 384
MLA_KV_LORA = 256
MLA_WIDTH = MLA_HEADS * MLA_V
ROPE_THETA = 10000.0
MLA_QK_PAD = 128
LOG2E = math.log2(math.e)
MLA_Q_SCALE = (MLA_NOPE + MLA_ROPE) ** -0.5 * LOG2E

SSM_HEADS = 16
SSM_HEAD_DIM = 64
SSM_INNER = SSM_HEADS * SSM_HEAD_DIM
SSM_GROUPS = 2
SSM_STATE = 128
SSM_CONV = 4
SSM_CHUNK = 128
SSM_CONV_DIM = SSM_INNER + 2 * SSM_GROUPS * SSM_STATE
SSM_HEADS_PER_GROUP = SSM_HEADS // SSM_GROUPS
ONES_ROWS = 16
MLA_V1_ROWS = MLA_V + ONES_ROWS
SSM_CONV_TAIL = 16

DSA_HEADS = 8
DSA_HEAD_DIM = 64
DSA_WIDTH = DSA_HEADS * DSA_HEAD_DIM
IDX_HEADS = 8
IDX_DIM = 64
TOPK_MAX = 256
DSA_KEY_TILE = 128
DSA_Q_BLOCK = 256

REL_BUCKETS = 32
REL_MAX_DIST = 128
N_BRANCHES = 3

LANES = 128
SUBLANES = 8
VMEM_LIMIT_CAP = 56 * 1024 * 1024
VMEM_LIMIT_FLOOR = 32 * 1024 * 1024
INT_MIN = -(2 ** 31)
NEG_BIG = -1e30

SPLIT_SIZES = (
    MLA_Q_LORA, MLA_KV_LORA, MLA_ROPE, MLA_WIDTH, SSM_INNER, SSM_CONV_DIM, SSM_HEADS, DSA_WIDTH,
    DSA_HEAD_DIM, DSA_HEAD_DIM, IDX_HEADS * IDX_DIM, IDX_DIM, IDX_HEADS, DSA_WIDTH,
    N_BRANCHES * D_MODEL,
)
SPLIT_NAMES = ("c_q", "c_kv", "k_rope", "gate_a", "z", "xbc", "dt", "q_c", "k_c", "v_c", "q_idx",
               "k_idx", "w_idx", "gate_c", "merge")


def _split_bounds():
    out, off = {}, 0
    for name, size in zip(SPLIT_NAMES, SPLIT_SIZES, strict=True):
        out[name] = (off, off + size)
        off += size
    return out


SPLIT = _split_bounds()


def _t5_large_thresholds():
    exact = REL_BUCKETS // 2
    thr = []
    for j in range(1, REL_BUCKETS - exact):
        thr.append(int(math.ceil(exact * (REL_MAX_DIST / exact) ** (j / (REL_BUCKETS - exact)) - 1e-9)))
    return tuple(thr)


T5_EXACT = REL_BUCKETS // 2
T5_LARGE_THR = _t5_large_thresholds()
T5_FAR = T5_LARGE_THR[-1]


def _compiler_params(semantics, vmem_bytes):
    limit = int(min(VMEM_LIMIT_CAP, max(VMEM_LIMIT_FLOOR, vmem_bytes)))
    return pltpu.CompilerParams(dimension_semantics=semantics, vmem_limit_bytes=limit)


def _nbytes(shape, dtype):
    return math.prod(shape) * jnp.dtype(dtype).itemsize


def _sigmoid(x):
    return 1.0 / (1.0 + jnp.exp(-x))


def _rms(x, g):
    return x * lax.rsqrt(jnp.mean(x * x, axis=-1, keepdims=True) + NORM_EPS) * g


MLA_IN_WIDTH = MLA_Q_LORA + MLA_KV_LORA + 2 * LANES
PROJ_SEGMENTS = (
    ("small", LANES, None, F32),
    ("gate_a", MLA_WIDTH, "silu", BF16),
    ("gate_c", DSA_WIDTH, "silu", BF16),
    ("z", SSM_INNER, "silu", BF16),
    ("xbc", SSM_CONV_DIM, None, BF16),
    ("merge", N_BRANCHES * D_MODEL, "sigmoid", BF16),
    ("k_c", DSA_HEAD_DIM, None, BF16),
    ("k_idx", IDX_DIM, None, BF16),
)
PROJ_CHUNK = 512
PROJ_T_ROWS = (DSA_WIDTH, IDX_HEADS * IDX_DIM, DSA_HEAD_DIM + IDX_HEADS)


def _proj_all_kernel(x_ref, g_ref, w_ref, wt_ref, m1_ref, m2_ref, m1t_ref, m2t_ref, qn_ref, kvn_ref, wq1t_ref, wq2t_ref,
                     wk_ref, wvt_ref, *out_refs):
    h = _rms(x_ref[0], g_ref[0]).astype(BF16)
    n_seg = len(PROJ_SEGMENTS)
    first_name, first_width, first_act, first_dtype = PROJ_SEGMENTS[0]
    assert first_act is None and first_dtype == F32 and (MLA_IN_WIDTH + first_width) % (2 * LANES) == 0
    a = jnp.dot(h, w_ref[0, :, 0:MLA_IN_WIDTH + first_width], preferred_element_type=F32)
    out_refs[0][0] = a[:, MLA_IN_WIDTH:]
    _mla_prep(a[:, :MLA_IN_WIDTH], m1_ref, m2_ref, m1t_ref, m2t_ref, qn_ref, kvn_ref, wq1t_ref, wq2t_ref, wk_ref, wvt_ref,
              *out_refs[n_seg + 4:])
    off = MLA_IN_WIDTH + first_width
    (_, wa_, act_a, _), (_, wb_, act_b, _) = PROJ_SEGMENTS[-2:]
    assert act_a is None and act_b is None and wa_ + wb_ == LANES
    for (_, width, act, _), o_ref in zip(PROJ_SEGMENTS[1:-2], out_refs[1:n_seg - 2], strict=True):
        for c in range(0, width, PROJ_CHUNK):
            cw = min(PROJ_CHUNK, width - c)
            y = jnp.dot(h, w_ref[0, :, off + c:off + c + cw], preferred_element_type=F32)
            if act == "silu":
                y = y * _sigmoid(y)
            elif act == "sigmoid":
                y = _sigmoid(y)
            o_ref[0, :, c:c + cw] = y.astype(o_ref.dtype)
        off += width
    y = jnp.dot(h, w_ref[0, :, off:off + LANES], preferred_element_type=F32)
    out_refs[n_seg - 2][0] = y[:, :wa_].astype(out_refs[n_seg - 2].dtype)
    out_refs[n_seg - 1][0] = y[:, wa_:].astype(out_refs[n_seg - 1].dtype)
    qcw_ref, qiw_ref, v1_ref, wi_ref = out_refs[n_seg:n_seg + 4]
    nt = (((1,), (1,)), ((), ()))
    tq = h.shape[0]
    row = 0
    for o_ref, heads, dim in ((qcw_ref, DSA_HEADS, DSA_HEAD_DIM), (qiw_ref, IDX_HEADS, IDX_DIM)):
        y = lax.dot_general(wt_ref[0, row:row + heads * dim, :], h, nt, preferred_element_type=F32)
        for hd in range(heads):
            o_ref[0, 0, :, hd * tq:(hd + 1) * tq] = y[hd * dim:(hd + 1) * dim, :].astype(o_ref.dtype)
        row += heads * dim
    vw = lax.dot_general(wt_ref[0, row:row + PROJ_T_ROWS[2], :], h, nt, preferred_element_type=F32)
    v1_ref[0, 0:DSA_HEAD_DIM, :] = vw[0:DSA_HEAD_DIM, :].astype(v1_ref.dtype)
    v1_ref[0, DSA_HEAD_DIM:, :] = jnp.ones((ONES_ROWS, tq), v1_ref.dtype)
    wi_ref[0] = vw[DSA_HEAD_DIM:, :]


def _proj_all(x, layer, gains, w, wt, rope, qn, kvn, wq1t, wq2t, wk, wvt, tq):
    b, s, k = x.shape
    n_all = MLA_IN_WIDTH + sum(seg[1] for seg in PROJ_SEGMENTS)
    hq = MLA_HEADS * MLA_QK_PAD
    v1_rows = DSA_HEAD_DIM + ONES_ROWS
    out_shapes = [jax.ShapeDtypeStruct((b, s, width), dt) for _, width, _, dt in PROJ_SEGMENTS]
    out_specs = [pl.BlockSpec((1, tq, width), lambda bi, i: (bi, i, 0)) for _, width, _, _ in PROJ_SEGMENTS]
    for heads, dim in ((DSA_HEADS, DSA_HEAD_DIM), (IDX_HEADS, IDX_DIM)):
        out_shapes.append(jax.ShapeDtypeStruct((b, s // tq, dim, heads * tq), BF16))
        out_specs.append(pl.BlockSpec((1, 1, dim, heads * tq), lambda bi, i: (bi, i, 0, 0)))
    for rows, dt in ((v1_rows, BF16), (IDX_HEADS, F32)):
        out_shapes.append(jax.ShapeDtypeStruct((b, rows, s), dt))
        out_specs.append(pl.BlockSpec((1, rows, tq), lambda bi, i: (bi, 0, i)))
    for rows in (hq, None, MLA_HEADS * MLA_V1_ROWS):
        if rows is None:
            out_shapes.append(jax.ShapeDtypeStruct((b, s, hq), BF16))
            out_specs.append(pl.BlockSpec((1, tq, hq), lambda bi, i: (bi, i, 0)))
        else:
            out_shapes.append(jax.ShapeDtypeStruct((b, rows, s), BF16))
            out_specs.append(pl.BlockSpec((1, rows, tq), lambda bi, i: (bi, 0, i)))
    resident = pl.Buffered(1)
    weights = (w, wt, qn, kvn, wq1t, wq2t, wk, wvt)
    layer_spec = lambda a: pl.BlockSpec((1,) + a.shape[1:], lambda bi, i: (layer,) + (0,) * (a.ndim - 1), pipeline_mode=resident)
    m1, m2, m1t, m2t = rope
    vmem = (sum(_nbytes(a.shape[1:], a.dtype) for a in weights) + 4 * _nbytes((tq, k), F32) + 8 * _nbytes((tq, LANES), F32)
            + 2 * sum(_nbytes((tq, width), dt) for _, width, _, dt in PROJ_SEGMENTS)
            + 2 * (2 * _nbytes((DSA_HEAD_DIM, DSA_HEADS * tq), BF16) + _nbytes((v1_rows + IDX_HEADS, tq), F32))
            + 2 * (2 * _nbytes((tq, hq), BF16) + _nbytes((MLA_HEADS * MLA_V1_ROWS, tq), BF16))
            + 8 * _nbytes((tq, PROJ_CHUNK), F32) + 6 * _nbytes((tq, hq), F32))
    outs = pl.pallas_call(
        _proj_all_kernel,
        grid=(b, s // tq),
        in_specs=[pl.BlockSpec((1, tq, k), lambda bi, i: (bi, i, 0)), layer_spec(gains), layer_spec(w), layer_spec(wt),
                  pl.BlockSpec((1, tq, LANES), lambda bi, i: (bi, i, 0)), pl.BlockSpec((1, tq, LANES), lambda bi, i: (bi, i, 0)),
                  pl.BlockSpec((1, LANES, tq), lambda bi, i: (bi, 0, i)), pl.BlockSpec((1, LANES, tq), lambda bi, i: (bi, 0, i))]
        + [layer_spec(a) for a in weights[2:]],
        out_specs=out_specs,
        out_shape=out_shapes,
        compiler_params=_compiler_params(("parallel", "parallel"), vmem),
        name="proj_all",
    )(x, gains, w, wt, m1, m2, m1t, m2t, *weights[2:])
    names = [seg[0] for seg in PROJ_SEGMENTS] + ["q_c_wide", "q_idx_wide", "v_t1", "w_idx_t", "mla_q_t", "mla_k", "mla_v_t1"]
    return dict(zip(names, outs, strict=True))


def _mla_prep(a, m1_ref, m2_ref, m1t_ref, m2t_ref, qn_ref, kvn_ref, wq1t_ref, wq2t_ref, wk_ref, wvt_ref, qt_ref, k_ref, vt_ref):
    c_q = a[:, :MLA_Q_LORA]
    c_kv = a[:, MLA_Q_LORA:MLA_Q_LORA + MLA_KV_LORA]
    kr1 = a[:, MLA_Q_LORA + MLA_KV_LORA:MLA_Q_LORA + MLA_KV_LORA + LANES]
    kr2 = a[:, MLA_Q_LORA + MLA_KV_LORA + LANES:]
    cqn = _rms(c_q, qn_ref[0]).astype(BF16)
    ckvn = _rms(c_kv, kvn_ref[0]).astype(BF16)
    nt = (((1,), (1,)), ((), ()))
    qa_t = lax.dot_general(wq1t_ref[0], cqn, nt, preferred_element_type=F32)
    qb_t = lax.dot_general(wq2t_ref[0], cqn, nt, preferred_element_type=F32)
    kn = jnp.dot(ckvn, wk_ref[0], preferred_element_type=F32)
    kr = kr1 * m1_ref[0] + kr2 * m2_ref[0]
    rope = slice(MLA_NOPE, MLA_NOPE + MLA_ROPE)
    cos_t = m1t_ref[0, rope, :] * MLA_Q_SCALE
    sin_t = m2t_ref[0, rope, :] * MLA_Q_SCALE
    pad_rows = jnp.zeros((MLA_QK_PAD - MLA_NOPE - MLA_ROPE, qa_t.shape[1]), qt_ref.dtype)
    for h in range(MLA_HEADS):
        base = h * MLA_QK_PAD
        sl = slice(base, base + MLA_QK_PAD)
        qt_ref[0, base:base + MLA_NOPE, :] = (qa_t[base:base + MLA_NOPE, :] * MLA_Q_SCALE).astype(qt_ref.dtype)
        qt_ref[0, base + MLA_NOPE:base + MLA_NOPE + MLA_ROPE, :] = (
            qa_t[base + MLA_NOPE:base + MLA_NOPE + MLA_ROPE, :] * cos_t
            + qb_t[h * MLA_ROPE:(h + 1) * MLA_ROPE, :] * sin_t).astype(qt_ref.dtype)
        qt_ref[0, base + MLA_NOPE + MLA_ROPE:base + MLA_QK_PAD, :] = pad_rows
        k_ref[0, :, sl] = (kn[:, sl] + kr).astype(k_ref.dtype)
    v_t = lax.dot_general(wvt_ref[0], ckvn, nt, preferred_element_type=F32)
    ones = jnp.ones((ONES_ROWS, v_t.shape[1]), vt_ref.dtype)
    for h in range(MLA_HEADS):
        base = h * MLA_V1_ROWS
        vt_ref[0, base:base + MLA_V, :] = v_t[h * MLA_V:(h + 1) * MLA_V, :].astype(vt_ref.dtype)
        vt_ref[0, base + MLA_V:base + MLA_V1_ROWS, :] = ones


def _mla_attn_kernel(qt_ref, k_ref, vt_ref, g_ref, o_ref, m_ref, l_ref, acc_ref, s0_ref, s1_ref, *, tq, tk):
    assert tq == 2 * tk
    qi = pl.program_id(1)
    krow = lax.broadcasted_iota(I32, (tk, tq), 0)
    qcol = lax.broadcasted_iota(I32, (tk, tq), 1)
    tiles_per_q = tq // tk
    m_ref[...] = jnp.full(m_ref.shape, -jnp.inf, F32)
    l_ref[...] = jnp.zeros(l_ref.shape, F32)
    acc_ref[...] = jnp.zeros(acc_ref.shape, F32)

    staged = (s0_ref, s1_ref)
    n_tiles = (qi + 1) * tiles_per_q

    def head_lanes(h):
        return slice(h * tq, (h + 1) * tq)

    def stage(j, dst_ref):
        ks = pl.multiple_of(j * tk, tk)
        for h in range(MLA_HEADS):
            k = k_ref[0, pl.ds(ks, tk), h * MLA_QK_PAD:(h + 1) * MLA_QK_PAD]
            dst_ref[:, head_lanes(h)] = jnp.dot(k, qt_ref[0, h * MLA_QK_PAD:(h + 1) * MLA_QK_PAD, :],
                                                preferred_element_type=F32)

    def tile(j, cur_ref, next_ref, diagonal):
        ks = pl.multiple_of(j * tk, tk)
        stage(jnp.minimum(j + 1, n_tiles - 1), next_ref)
        m_all = m_ref[...]
        l_all = l_ref[...]
        if diagonal:
            causal = ks + krow <= qi * tq + qcol
        for h in range(MLA_HEADS):
            s = cur_ref[:, head_lanes(h)]
            if diagonal:
                s = jnp.where(causal, s, -jnp.inf)
            m_old = m_all[h:h + 1, :]
            m_new = jnp.maximum(m_old, jnp.max(s, axis=0, keepdims=True))
            alpha = jnp.exp2(m_old - m_new)
            p = jnp.exp2((s - m_new).astype(BF16))
            m_ref[h:h + 1, :] = m_new
            pv = jnp.dot(vt_ref[0, h * MLA_V1_ROWS:(h + 1) * MLA_V1_ROWS, pl.ds(ks, tk)], p, preferred_element_type=F32)
            hs = slice(h * MLA_V, (h + 1) * MLA_V)
            acc_ref[hs, :] = alpha * acc_ref[hs, :] + pv[0:MLA_V, :]
            l_ref[h:h + 1, :] = alpha * l_all[h:h + 1, :] + pv[MLA_V:MLA_V + 1, :]

    stage(0, staged[0])

    def full_pair(pi, carry):
        for half in range(2):
            tile(2 * pi + half, staged[half], staged[1 - half], diagonal=False)
        return carry

    lax.fori_loop(0, qi, full_pair, 0)
    for half in range(2):
        tile(2 * qi + half, staged[half], staged[1 - half], diagonal=True)
    for h in range(MLA_HEADS):
        hs = slice(h * MLA_V, (h + 1) * MLA_V)
        acc_ref[hs, :] = acc_ref[hs, :] * (1.0 / l_ref[h:h + 1, :])
    o_ref[0] = (acc_ref[...].T * g_ref[0]).astype(o_ref.dtype)


def _mla_attn(qt, k, vt, gate, tq=256, tk=128):
    b, s, hq = k.shape
    vmem = 2 * (_nbytes((hq, tq), BF16) + _nbytes((s, hq), BF16) + _nbytes((MLA_WIDTH, s), BF16)
                + _nbytes((tq, MLA_WIDTH), F32) + _nbytes((tq, MLA_WIDTH), BF16)) + _nbytes((MLA_WIDTH, tq), F32) + 16 * _nbytes((tk, tq), F32)
    return pl.pallas_call(
        functools.partial(_mla_attn_kernel, tq=tq, tk=tk),
        grid=(b, s // tq),
        in_specs=[pl.BlockSpec((1, hq, tq), lambda bi, i: (bi, 0, i)),
                  pl.BlockSpec((1, s, hq), lambda bi, i: (bi, 0, 0)),
                  pl.BlockSpec((1, MLA_HEADS * MLA_V1_ROWS, s), lambda bi, i: (bi, 0, 0)),
                  pl.BlockSpec((1, tq, MLA_WIDTH), lambda bi, i: (bi, i, 0))],
        out_specs=pl.BlockSpec((1, tq, MLA_WIDTH), lambda bi, i: (bi, i, 0)),
        out_shape=jax.ShapeDtypeStruct((b, s, MLA_WIDTH), BF16),
        scratch_shapes=[pltpu.VMEM((MLA_HEADS, tq), F32), pltpu.VMEM((MLA_HEADS, tq), F32),
                        pltpu.VMEM((MLA_WIDTH, tq), F32),
                        pltpu.VMEM((tk, MLA_HEADS * tq), F32), pltpu.VMEM((tk, MLA_HEADS * tq), F32)],
        compiler_params=_compiler_params(("parallel", "arbitrary"), vmem),
        name="mla_attn",
    )(qt, k, vt, gate)


def _ssd_kernel(xbc_ref, xprev_ref, zs_ref, sm_ref, cw_ref, cb_ref, dtb_ref, alog_ref, dskf_ref, nrm_ref, exp16_ref, exp32_ref,
                o_ref, state_ref, y_ref):
    c = pl.program_id(1)
    q = SSM_CHUNK

    @pl.when(c == 0)
    def _():
        state_ref[...] = jnp.zeros(state_ref.shape, F32)

    x_in = xbc_ref[0]
    tail = xprev_ref.shape[1]
    prev = jnp.where(c > 0, xprev_ref[0].astype(F32), 0.0).astype(x_in.dtype)
    win = jnp.concatenate([prev, x_in], axis=0)
    x = x_in.astype(F32)
    wrow = lax.broadcasted_iota(I32, (q, tail + q), 0)
    wcol = lax.broadcasted_iota(I32, (q, tail + q), 1)
    acc = cb_ref[0] + cw_ref[0, SSM_CONV - 1:SSM_CONV, :] * x
    for j in range(1, SSM_CONV):
        pick = jnp.where(wcol == wrow + (tail - j), 1.0, 0.0).astype(win.dtype)
        acc = acc + cw_ref[0, SSM_CONV - 1 - j:SSM_CONV - j, :] * jnp.dot(pick, win, preferred_element_type=F32)
    half = 0.5 * acc
    xc = half + half * jnp.tanh(half)

    xs = xc[:, :SSM_INNER]
    bm = xc[:, SSM_INNER:SSM_INNER + SSM_GROUPS * SSM_STATE]
    cm = xc[:, SSM_INNER + SSM_GROUPS * SSM_STATE:]

    pre = sm_ref[0] + dtb_ref[0]
    dt = jnp.maximum(pre, 0.0) + jnp.log1p(jnp.exp(-jnp.abs(pre)))
    a = -jnp.exp(alog_ref[0])
    row = lax.broadcasted_iota(I32, (q, q), 0)
    col = lax.broadcasted_iota(I32, (q, q), 1)
    lower = row >= col
    cum = jnp.dot(jnp.where(lower, 1.0, 0.0).astype(F32), dt * a, preferred_element_type=F32,
                  precision=lax.Precision.HIGHEST)
    cum_t = cum.T
    cum_last = cum[q - 1:q, :]
    per_head = jnp.concatenate([dt, jnp.exp(cum_last - cum), jnp.exp(cum)], axis=0).astype(BF16)
    spread = jnp.dot(per_head, exp16_ref[...], preferred_element_type=F32)
    dt_full, to_end_full, e_cum_full = spread[0:q], spread[q:2 * q], spread[2 * q:3 * q]
    chunk_decay_full = jnp.dot(jnp.broadcast_to(jnp.exp(cum_last), (SUBLANES, LANES)), exp32_ref[...],
                               preferred_element_type=F32, precision=lax.Precision.HIGHEST)[0:1]
    xdt = xs * dt_full
    xw = (xdt * to_end_full).astype(BF16)
    first_head = lax.broadcasted_iota(I32, (q, LANES), 1) < SSM_HEAD_DIM
    group_w = SSM_HEADS_PER_GROUP * SSM_HEAD_DIM
    heads_per_tile = LANES // SSM_HEAD_DIM

    for g in range(SSM_GROUPS):
        bg32 = bm[:, g * SSM_STATE:(g + 1) * SSM_STATE]
        bg = bg32.astype(BF16)
        cg = cm[:, g * SSM_STATE:(g + 1) * SSM_STATE].astype(BF16)
        cb = lax.dot_general(cg, bg, (((1,), (1,)), ((), ())), preferred_element_type=F32)
        gl = slice(g * group_w, (g + 1) * group_w)
        st = state_ref[g]
        y_off = jnp.dot(cg, st.astype(BF16), preferred_element_type=F32)
        state_ref[g] = st * chunk_decay_full[:, gl] + jnp.dot(bg32.T.astype(BF16), xw[:, gl], preferred_element_type=F32)
        for tile in range(group_w // LANES):
            h0 = g * SSM_HEADS_PER_GROUP + tile * heads_per_tile
            tl = slice(h0 * SSM_HEAD_DIM, h0 * SSM_HEAD_DIM + LANES)
            g_both = []
            for h in range(h0, h0 + heads_per_tile):
                diff = cum[:, h:h + 1] - cum_t[h:h + 1, :]
                decay_in = jnp.exp(jnp.where(lower, diff, -jnp.inf))
                g_both.append((cb * decay_in).astype(BF16))
            xdt_tile = xdt[:, tl]
            stacked = jnp.concatenate([jnp.where(first_head, xdt_tile, 0.0).astype(BF16),
                                       jnp.where(first_head, 0.0, xdt_tile).astype(BF16)], axis=0)
            y_diag = jnp.dot(jnp.concatenate(g_both, axis=1), stacked, preferred_element_type=F32)
            y = y_diag + y_off[:, tile * LANES:(tile + 1) * LANES] * e_cum_full[:, tl] + xs[:, tl] * dskf_ref[0, :, tl]
            y_ref[:, tl] = y * zs_ref[0, :, tl]

    o_ref[0] = _rms(y_ref[...], nrm_ref[0]).astype(o_ref.dtype)


def _ssd_params(conv_w, conv_b, dt_bias, a_log, d_skip, ssm_norm):
    depth = conv_w.shape[0]
    pad = lambda v: jnp.pad(v.astype(F32), ((0, 0), (0, LANES - v.shape[1]))).reshape(depth, 1, LANES)
    expand = (jnp.arange(LANES)[:, None] == jnp.arange(SSM_INNER)[None, :] // SSM_HEAD_DIM).astype(F32)
    d_skip_full = jnp.repeat(d_skip.astype(F32), SSM_HEAD_DIM, axis=1).reshape(depth, 1, SSM_INNER)
    return (conv_w.astype(F32), conv_b.astype(F32).reshape(depth, 1, -1), pad(dt_bias), pad(a_log), d_skip_full,
            ssm_norm.astype(F32).reshape(depth, 1, -1)), (expand.astype(BF16), expand)


def _ssd(xbc, zs, small, layer, params, expands):
    b, s, _ = xbc.shape
    q = SSM_CHUNK
    full = lambda shape: pl.BlockSpec(shape, lambda bi, c: (0,) * len(shape))
    of_layer = lambda a: pl.BlockSpec((1,) + a.shape[1:], lambda bi, c: (layer, 0, 0))
    blk = lambda width: pl.BlockSpec((1, q, width), lambda bi, c: (bi, c, 0))
    group_w = SSM_HEADS_PER_GROUP * SSM_HEAD_DIM
    vmem = (2 * (_nbytes((q, SSM_CONV_DIM), xbc.dtype) + _nbytes((q, SSM_INNER), zs.dtype) + _nbytes((q, LANES), F32)
                 + _nbytes((q, SSM_INNER), BF16) + _nbytes((LANES, SSM_INNER), BF16) + _nbytes((LANES, SSM_INNER), F32))
            + _nbytes((2 * q, SSM_CONV_DIM), xbc.dtype) + _nbytes((SSM_GROUPS, SSM_STATE, group_w), F32)
            + _nbytes((q, SSM_INNER), F32) + 10 * _nbytes((q, SSM_CONV_DIM), F32))
    return pl.pallas_call(
        _ssd_kernel,
        grid=(b, s // q),
        in_specs=[blk(SSM_CONV_DIM),
                  pl.BlockSpec((1, SSM_CONV_TAIL, SSM_CONV_DIM),
                               lambda bi, c: (bi, jnp.maximum(c * (q // SSM_CONV_TAIL) - 1, 0), 0)),
                  blk(SSM_INNER), blk(LANES)] + [of_layer(a) for a in params]
        + [full((LANES, SSM_INNER)), full((LANES, SSM_INNER))],
        out_specs=blk(SSM_INNER),
        out_shape=jax.ShapeDtypeStruct((b, s, SSM_INNER), BF16),
        scratch_shapes=[pltpu.VMEM((SSM_GROUPS, SSM_STATE, group_w), F32),
                        pltpu.VMEM((q, SSM_INNER), F32)],
        compiler_params=_compiler_params(("parallel", "arbitrary"), vmem),
        name="ssd",
    )(xbc, xbc, zs, small, *params, *expands)


def _t5_bucket_of(n):
    steps_per_octave = (REL_BUCKETS - T5_EXACT) / math.log2(REL_MAX_DIST / T5_EXACT)
    x = jnp.maximum(n, T5_EXACT).astype(F32) * (1.0 / T5_EXACT)
    large = T5_EXACT + jnp.floor(jnp.log2(x) * steps_per_octave).astype(I32)
    return jnp.where(n <= T5_EXACT, n, jnp.minimum(large, REL_BUCKETS - 1))


def _dsa_kernel(qmin_ref, kmax_ref, consec_ref, tblt_ref, toe_ref, qcw_ref, qiw_ref, kc_ref, ki_ref, v1_ref, wi_ref, posk_ref, posq_ref, gate_ref, o_ref,
                key_ref, hi_ref, lo_ref, s0_ref, s1_ref, bias_ref, acc_ref, ot_ref, m_ref, l_ref, tie_ref, *, n_sel, tq):
    tk = DSA_KEY_TILE
    qi = pl.program_id(1)
    n_tiles = (qi + 1) * (tq // tk)
    krow = lax.broadcasted_iota(I32, (tk, tq), 0)
    qcol = lax.broadcasted_iota(I32, (tk, tq), 1)
    w = wi_ref[0]
    posq = posq_ref[0]

    def tile_start(kt):
        return pl.multiple_of(kt * tk, tk)

    def causal_of(ks):
        return (ks + krow) <= (qi * tq + qcol)

    def head_lanes(h):
        return slice(h * tq, (h + 1) * tq)

    staged = (s0_ref, s1_ref)

    def stage(k_ref, q_ref, kt, dst_ref):
        dst_ref[...] = jnp.dot(k_ref[0, pl.ds(tile_start(kt), tk), :], q_ref[0, 0], preferred_element_type=F32)

    def tile_pairs(tile_fn, peel_last):
        def pair(pi, last_pair):
            for half in range(2):
                tile_fn(2 * pi + half, staged[half], staged[1 - half], last_pair, last_pair and half == 1)

        def body(pi, carry):
            pair(pi, False)
            return carry

        if peel_last:
            lax.fori_loop(0, n_tiles // 2 - 1, body, 0)
            pair(n_tiles // 2 - 1, True)
        else:
            lax.fori_loop(0, n_tiles // 2, body, 0)

    stage(ki_ref, qiw_ref, 0, staged[0])

    def score_tile(kt, cur_ref, next_ref, last_pair, last_tile):
        ks = tile_start(kt)
        if last_tile:
            stage(kc_ref, qcw_ref, 0, next_ref)
        else:
            stage(ki_ref, qiw_ref, kt + 1, next_ref)
        sc = jnp.zeros((tk, tq), F32)
        for h in range(IDX_HEADS):
            sc = sc + w[h:h + 1, :] * jnp.maximum(cur_ref[:, head_lanes(h)], 0.0)
        sc = jnp.where(sc == 0.0, 0.0, sc)
        bits = lax.bitcast_convert_type(sc, I32)
        key = bits ^ ((bits >> 31) & 0x7FFFFFFF)
        if last_pair:
            key = jnp.where(causal_of(ks), key, INT_MIN)
        key_ref[pl.ds(ks, tk), :] = key
        hi_ref[pl.ds(ks, tk), :] = (key >> 16).astype(I16)
        lo_ref[pl.ds(ks, tk), :] = ((key & 0xFFFF) - HALF16).astype(I16)

    tile_pairs(score_tile, peel_last=True)

    def count16(half_ref, pred):
        def body(pi, cnt):
            for kt in (2 * pi, 2 * pi + 1):
                cnt = cnt + jnp.where(pred(half_ref[pl.ds(tile_start(kt), tk), :]), jnp.int16(1), jnp.int16(0))
            return cnt
        cnt = lax.fori_loop(0, n_tiles // 2, body, jnp.zeros((tk, tq), I16))
        rows = 2 * SUBLANES
        parts = [cnt[r:r + rows, :] for r in range(0, tk, rows)]
        while len(parts) > 1:
            parts = [a + b for a, b in zip(parts[0::2], parts[1::2], strict=True)]
        return jnp.sum(parts[0].astype(I32), axis=0, keepdims=True)

    def nth_largest16(half_ref, rank):
        def bit_step(i, carry):
            cand, reached = carry
            trial = cand | lax.shift_left(jnp.int32(1), 15 - i)
            thr = (trial - HALF16).astype(I16)
            tot = count16(half_ref, lambda v: v >= thr)
            take = tot >= rank
            return jnp.where(take, trial, cand), jnp.where(take, tot, reached)
        init = (jnp.zeros((1, tq), I32), jnp.full((1, tq), n_tiles * tk, I32))
        cand, reached = lax.fori_loop(0, 16, bit_step, init)
        return cand - HALF16, reached

    tau_hi, _ = nth_largest16(hi_ref, n_sel)
    tau_hi16 = tau_hi.astype(I16)
    above_hi = count16(hi_ref, lambda v: v > tau_hi16)

    def mask_low(kt, carry):
        rows = pl.ds(tile_start(kt), tk)
        lo_ref[rows, :] = jnp.where(hi_ref[rows, :] == tau_hi16, lo_ref[rows, :], jnp.int16(-HALF16))
        return carry

    lax.fori_loop(0, n_tiles, mask_low, 0)
    tau_lo, reached_lo = nth_largest16(lo_ref, n_sel - above_hi)
    tau_lo16 = tau_lo.astype(I16)
    tau = lax.shift_left(tau_hi, 16) | (tau_lo + HALF16)
    n_gt = above_hi + count16(lo_ref, lambda v: v > tau_lo16)
    n_ge = above_hi + reached_lo
    need = n_sel - n_gt
    tau_floor = jnp.maximum(tau, INT_MIN + 1)

    m_ref[...] = jnp.full(m_ref.shape, NEG_BIG, F32)
    l_ref[...] = jnp.zeros(l_ref.shape, F32)
    acc_ref[...] = jnp.zeros(acc_ref.shape, F32)
    tie_ref[...] = jnp.zeros(tie_ref.shape, I32)
    lower_incl = jnp.where(lax.broadcasted_iota(I32, (tk, tk), 0) >= lax.broadcasted_iota(I32, (tk, tk), 1),
                           1.0, 0.0).astype(BF16)

    splits =jnp.where((n_ge > n_sel) & (tau != INT_MIN), 1.0, 0.0)
    any_split = jnp.max(splits) > 0.0

    def attend_tile(kt, cur_ref, next_ref, general_bias, tie_split):
        ks = tile_start(kt)
        key = key_ref[pl.ds(ks, tk), :]
        if tie_split:
            eq = key == tau
            prefix = jnp.dot(lower_incl, jnp.where(eq, 1.0, 0.0).astype(BF16), preferred_element_type=F32)
            tie_before = tie_ref[0:1, :]
            tie_rank = tie_before + prefix.astype(I32)
            sel = ((key > tau) | (eq & (tie_rank <= need))) & causal_of(ks)
            tie_ref[0:1, :] = tie_before + prefix[tk - 1:tk, :].astype(I32)
        else:
            sel = key >= tau_floor

        if general_bias == "toeplitz":
            offset = [(qmin_ref[bi, qi] + c * LANES) - (kmax_ref[bi, kt] - (tk - 1)) for c in range(tq // LANES)]
            which = [jnp.where(off == 0, 0, jnp.where(off == tk, 1, 2)) for off in offset]
        elif general_bias == "gather":
            posk = posk_ref[0, pl.ds(ks, tk), :]
            bucket = _t5_bucket_of(jnp.maximum(posq - posk, 0))
            for hp in range(DSA_HEADS // 2):
                table = jnp.broadcast_to(tblt_ref[hp:hp + 1, :], (tk, LANES))
                for c in range(tq // LANES):
                    both = jnp.take_along_axis(table, bucket[:, c * LANES:(c + 1) * LANES], axis=1)
                    lo, hi = 2 * hp * tq + c * LANES, (2 * hp + 1) * tq + c * LANES
                    bias_ref[:, lo:lo + LANES] = lax.bitcast_convert_type(both & jnp.int32(-65536), F32)
                    bias_ref[:, hi:hi + LANES] = lax.bitcast_convert_type(lax.shift_left(both, 16), F32)
        stage(kc_ref, qcw_ref, jnp.minimum(kt + 1, n_tiles - 1), next_ref)
        m_all = m_ref[0:1, :]
        l_all = l_ref[0:1, :]
        v1 = v1_ref[0, :, pl.ds(ks, tk)]
        for h in range(DSA_HEADS):
            hl = head_lanes(h)
            s = cur_ref[:, hl]
            if general_bias == "toeplitz":
                s = s + jnp.concatenate([toe_ref[which[c], h] for c in range(tq // LANES)], axis=1)
            elif general_bias == "gather":
                s = s + bias_ref[:, hl]
            s = jnp.where(sel, s, NEG_BIG)
            m_old = m_all[:, hl]
            m_new = jnp.maximum(m_old, jnp.max(s, axis=0, keepdims=True))
            m_ref[0:1, hl] = m_new
            alpha = jnp.exp2(m_old - m_new)
            p = jnp.exp2((s - m_new).astype(BF16))
            pv = jnp.dot(v1, p, preferred_element_type=F32)
            acc_ref[:, hl] = alpha * acc_ref[:, hl] + pv[0:DSA_HEAD_DIM, :]
            l_ref[0:1, hl] = alpha * l_all[:, hl] + pv[DSA_HEAD_DIM:DSA_HEAD_DIM + 1, :]

    bi = pl.program_id(0)

    def attend(kt, cur_ref, next_ref, last_pair, last_tile):
        far = (qmin_ref[bi, qi] - kmax_ref[bi, kt]) >= T5_FAR
        no_split = jnp.logical_not(any_split)
        near = jnp.logical_and(jnp.logical_not(far), no_split)
        consecutive = consec_ref[bi] == 1

        @pl.when(jnp.logical_and(far, no_split))
        def _():
            attend_tile(kt, cur_ref, next_ref, general_bias=None, tie_split=False)

        @pl.when(jnp.logical_and(near, consecutive))
        def _():
            attend_tile(kt, cur_ref, next_ref, general_bias="toeplitz", tie_split=False)

        @pl.when(jnp.logical_and(near, jnp.logical_not(consecutive)))
        def _():
            attend_tile(kt, cur_ref, next_ref, general_bias="gather", tie_split=False)

        @pl.when(any_split)
        def _():
            attend_tile(kt, cur_ref, next_ref, general_bias="gather", tie_split=True)

    tile_pairs(attend, peel_last=False)

    for h in range(DSA_HEADS):
        hl = head_lanes(h)
        ot_ref[h * DSA_HEAD_DIM:(h + 1) * DSA_HEAD_DIM, :] = acc_ref[:, hl] * (1.0 / l_ref[0:1, hl])
    o_ref[0] = (ot_ref[...].T * gate_ref[0]).astype(o_ref.dtype)


DSA_TOEPLITZ_OFFSETS = (0, DSA_KEY_TILE)


def _bias_toeplitz_kernel(tbl_ref, o_ref):
    j = pl.program_id(0)
    tk = DSA_KEY_TILE
    diff = lax.broadcasted_iota(I32, (tk, LANES), 1) - lax.broadcasted_iota(I32, (tk, LANES), 0)
    for slot, offset in enumerate(DSA_TOEPLITZ_OFFSETS):
        @pl.when(j == slot)
        def _(offset=offset):
            bucket = _t5_bucket_of(jnp.maximum(diff + offset, 0))
            for h in range(DSA_HEADS):
                table = jnp.broadcast_to(tbl_ref[h:h + 1, :], (tk, LANES))
                o_ref[0, h] = jnp.take_along_axis(table, bucket, axis=1)

    @pl.when(j == len(DSA_TOEPLITZ_OFFSETS))
    def _():
        o_ref[...] = jnp.zeros(o_ref.shape, F32)


def _bias_toeplitz(tbl_rows):
    n = len(DSA_TOEPLITZ_OFFSETS) + 1
    return pl.pallas_call(
        _bias_toeplitz_kernel,
        grid=(n,),
        in_specs=[pl.BlockSpec((DSA_HEADS, LANES), lambda j: (0, 0))],
        out_specs=pl.BlockSpec((1, DSA_HEADS, DSA_KEY_TILE, LANES), lambda j: (j, 0, 0, 0)),
        out_shape=jax.ShapeDtypeStruct((n, DSA_HEADS, DSA_KEY_TILE, LANES), F32),
        compiler_params=_compiler_params(("parallel",), 0),
        name="bias_toeplitz",
    )(tbl_rows)


def _dsa_bias_tables(rel_bias):
    tbl = rel_bias.astype(F32) * LOG2E
    tbl = tbl - tbl[REL_BUCKETS - 1:REL_BUCKETS, :]
    rows = jnp.pad(tbl.T, ((0, 0), (0, LANES - REL_BUCKETS)))
    bits = lax.bitcast_convert_type(tbl.T.astype(jnp.bfloat16), jnp.uint16).astype(jnp.uint32)
    packed = lax.bitcast_convert_type((bits[0::2] << 16) | bits[1::2], I32)
    packed = jnp.pad(packed, ((0, SUBLANES - DSA_HEADS // 2), (0, LANES - REL_BUCKETS)))
    return packed, _bias_toeplitz(rows)


def _dsa(bias_tables, qcw, qiw, kc, ki, v1, wi, pos_col, pos_row, gate, tq):
    b, s, _ = kc.shape
    n_sel = min(TOPK_MAX, s // 4)
    vw_rows = v1.shape[1]
    hw = DSA_HEADS * tq
    tbl_t, toeplitz = bias_tables
    q_min = jnp.min(pos_row.reshape(b, s // tq, tq), axis=-1)
    k_max = jnp.max(pos_row.reshape(b, s // DSA_KEY_TILE, DSA_KEY_TILE), axis=-1)
    consecutive = jnp.all(pos_row[:, 0, 1:] - pos_row[:, 0, :-1] == 1, axis=-1).astype(I32)
    tk = DSA_KEY_TILE
    vmem = (2 * (2 * _nbytes((DSA_HEAD_DIM, hw), BF16) + 2 * _nbytes((s, LANES), BF16) + _nbytes((vw_rows, s), F32)
                 + _nbytes((vw_rows, tq), F32) + _nbytes((s, LANES), I32) + _nbytes((tq, DSA_WIDTH), F32)
                 + _nbytes((tq, DSA_WIDTH), BF16)) + _nbytes((s, tq), I32) + 2 * _nbytes((tk, hw), F32)
            + _nbytes((tk, hw), BF16) + _nbytes((DSA_HEAD_DIM, hw), F32) + _nbytes((DSA_WIDTH, tq), F32)
            + 32 * _nbytes((tk, tq), F32))
    return pl.pallas_call(
        functools.partial(_dsa_kernel, n_sel=n_sel, tq=tq),
        grid=(b, s // tq),
        in_specs=[pl.BlockSpec(memory_space=pltpu.SMEM), pl.BlockSpec(memory_space=pltpu.SMEM),
                  pl.BlockSpec(memory_space=pltpu.SMEM),
                  pl.BlockSpec((SUBLANES, LANES), lambda bi, i: (0, 0)),
                  pl.BlockSpec(toeplitz.shape, lambda bi, i: (0, 0, 0, 0)),
                  pl.BlockSpec((1, 1, DSA_HEAD_DIM, hw), lambda bi, i: (bi, i, 0, 0)),
                  pl.BlockSpec((1, 1, IDX_DIM, hw), lambda bi, i: (bi, i, 0, 0)),
                  pl.BlockSpec((1, s, DSA_HEAD_DIM), lambda bi, i: (bi, 0, 0)),
                  pl.BlockSpec((1, s, IDX_DIM), lambda bi, i: (bi, 0, 0)),
                  pl.BlockSpec((1, vw_rows, s), lambda bi, i: (bi, 0, 0)),
                  pl.BlockSpec((1, IDX_HEADS, tq), lambda bi, i: (bi, 0, i)),
                  pl.BlockSpec((1, s, 1), lambda bi, i: (bi, 0, 0)),
                  pl.BlockSpec((1, 1, tq), lambda bi, i: (bi, 0, i)),
                  pl.BlockSpec((1, tq, DSA_WIDTH), lambda bi, i: (bi, i, 0))],
        out_specs=pl.BlockSpec((1, tq, DSA_WIDTH), lambda bi, i: (bi, i, 0)),
        out_shape=jax.ShapeDtypeStruct((b, s, DSA_WIDTH), BF16),
        scratch_shapes=[pltpu.VMEM((s, tq), I32),
                        pltpu.VMEM((s, tq), I16), pltpu.VMEM((s, tq), I16),
                        pltpu.VMEM((tk, hw), F32), pltpu.VMEM((tk, hw), F32),
                        pltpu.VMEM((tk, hw), F32),
                        pltpu.VMEM((DSA_HEAD_DIM, hw), F32),
                        pltpu.VMEM((DSA_WIDTH, tq), F32),
                        pltpu.VMEM((SUBLANES, hw), F32), pltpu.VMEM((SUBLANES, hw), F32),
                        pltpu.VMEM((SUBLANES, tq), I32)],
        compiler_params=_compiler_params(("parallel", "arbitrary"), vmem),
        name="dsa",
    )(q_min, k_max, consecutive, tbl_t, toeplitz, qcw, qiw, kc, ki, v1, wi, pos_col, pos_row, gate)


def _merge_kernel(oa_ref, ob_ref, oc_ref, g_ref, x_ref, p_ref, wa_ref, wb_ref, wc_ref, wo_ref, wpg_ref, wple_ref,
                  gn_ref, o_ref, *, final):
    d = D_MODEL
    ya = jnp.dot(oa_ref[0], wa_ref[0], preferred_element_type=F32)
    yb = jnp.dot(ob_ref[0], wb_ref[0], preferred_element_type=F32)
    yc = jnp.dot(oc_ref[0], wc_ref[0], preferred_element_type=F32)
    merged = g_ref[0, :, 0:d] * ya + g_ref[0, :, d:2 * d] * yb + g_ref[0, :, 2 * d:3 * d] * yc
    x1 = x_ref[0] + jnp.dot(merged.astype(BF16), wo_ref[0], preferred_element_type=F32)
    ple_gate = _sigmoid(jnp.dot(x1.astype(BF16), wpg_ref[0], preferred_element_type=F32))
    x2 = x1 + ple_gate * jnp.dot(p_ref[0, 0].astype(BF16), wple_ref[0], preferred_element_type=F32)
    o_ref[0] = _rms(x2, gn_ref[...]) if final else x2


def _merge(oa, ob, oc, gates, x, p, layer, wa, wb, wc, wo, wpg, wple, final_gain, final, tm=512):
    b, s, d = x.shape
    full = lambda a: pl.BlockSpec((1,) + a.shape[1:], lambda bi, i: (layer, 0, 0), pipeline_mode=pl.Buffered(1))
    row = lambda a: pl.BlockSpec((1, tm, a.shape[2]), lambda bi, i: (bi, i, 0))
    weights = (wa, wb, wc, wo, wpg, wple)
    acts = (oa, ob, oc, gates, x)
    vmem = (sum(_nbytes(a.shape[1:], a.dtype) for a in weights)
            + 2 * (sum(_nbytes((tm, a.shape[2]), a.dtype) for a in acts) + _nbytes((tm, p.shape[3]), p.dtype)
                   + 2 * _nbytes((tm, d), F32)) + 8 * _nbytes((tm, d), F32))
    return pl.pallas_call(
        functools.partial(_merge_kernel, final=final),
        grid=(b, s // tm),
        in_specs=[row(a) for a in acts] + [pl.BlockSpec((1, 1, tm, p.shape[3]), lambda bi, i: (layer, bi, i, 0))]
        + [full(a) for a in weights] + [pl.BlockSpec((1, d), lambda bi, i: (0, 0))],
        out_specs=pl.BlockSpec((1, tm, d), lambda bi, i: (bi, i, 0)),
        out_shape=jax.ShapeDtypeStruct((b, s, d), F32),
        compiler_params=_compiler_params(("parallel", "parallel"), vmem),
        name="merge",
    )(*acts, p, *weights, final_gain.astype(F32).reshape(1, d))


def _rotate_half_cols(w):
    half = w.shape[-1] // 2
    return jnp.concatenate([-w[..., half:], w[..., :half]], axis=-1)


def _prep_weights(w_in, w_uq, w_ukv):
    depth = w_in.shape[0]
    seg = lambda name: w_in[:, :, SPLIT[name][0]:SPLIT[name][1]]
    z = lambda *shape: jnp.zeros((depth,) + shape, w_in.dtype)
    d = D_MODEL
    kr = seg("k_rope")
    rope_lo = MLA_NOPE
    rope_pad = MLA_QK_PAD - MLA_NOPE - MLA_ROPE
    w = {}
    w["mla_in"] = jnp.concatenate(
        [seg("c_q"), seg("c_kv"), z(d, rope_lo), kr, z(d, rope_pad), z(d, rope_lo), _rotate_half_cols(kr), z(d, rope_pad)], axis=-1)
    w["gate_a"] = seg("gate_a")
    w["z"] = seg("z")
    w["xbc"] = seg("xbc")
    w["small"] = jnp.concatenate([seg("dt"), z(d, LANES - SSM_HEADS)], axis=-1)
    w["k_c"] = seg("k_c")
    w["k_idx"] = seg("k_idx")
    w["gate_c"] = seg("gate_c")
    w["merge"] = seg("merge")
    w["q_c_t"] = jnp.swapaxes(seg("q_c") * (DSA_HEAD_DIM ** -0.5 * LOG2E), 1, 2)
    w["q_idx_t"] = jnp.swapaxes(seg("q_idx"), 1, 2)
    w["vw_t"] = jnp.swapaxes(jnp.concatenate([seg("v_c"), seg("w_idx")], axis=-1), 1, 2)
    w = {
        "proj": jnp.concatenate([w["mla_in"]] + [w[seg[0]] for seg in PROJ_SEGMENTS], axis=-1).astype(BF16),
        "proj_t": jnp.concatenate([w["q_c_t"], w["q_idx_t"], w["vw_t"]], axis=1).astype(BF16),
    }

    uq =w_uq.reshape(depth, MLA_Q_LORA, MLA_HEADS, MLA_NOPE + MLA_ROPE)
    nope, rope = uq[..., :MLA_NOPE], uq[..., MLA_NOPE:]
    zq = lambda width: jnp.zeros((depth, MLA_Q_LORA, MLA_HEADS, width), w_uq.dtype)
    hq = MLA_HEADS * MLA_QK_PAD
    uq1 = jnp.concatenate([nope, rope, zq(rope_pad)], axis=-1).reshape(depth, MLA_Q_LORA, hq)
    uq2 = _rotate_half_cols(rope).reshape(depth, MLA_Q_LORA, MLA_HEADS * MLA_ROPE)
    w["uq1_t"] = jnp.swapaxes(uq1, 1, 2).astype(BF16)
    w["uq2_t"] = jnp.swapaxes(uq2, 1, 2).astype(BF16)
    ukv = w_ukv.reshape(depth, MLA_KV_LORA, MLA_HEADS, MLA_NOPE + MLA_V)
    zk = jnp.zeros((depth, MLA_KV_LORA, MLA_HEADS, MLA_QK_PAD - MLA_NOPE), w_ukv.dtype)
    w["uk"] = jnp.concatenate([ukv[..., :MLA_NOPE], zk], axis=-1).reshape(depth, MLA_KV_LORA, hq).astype(BF16)
    w["uv_t"] = jnp.swapaxes(ukv[..., MLA_NOPE:].reshape(depth, MLA_KV_LORA, MLA_WIDTH), 1, 2).astype(BF16)
    return w


def _rope_multipliers(positions):
    b, s = positions.shape
    inv_freq = 1.0 / (ROPE_THETA ** (jnp.arange(0, MLA_ROPE, 2, dtype=F32) / MLA_ROPE))
    ang = positions.astype(F32)[..., None] * inv_freq
    cos, sin = jnp.cos(ang), jnp.sin(ang)
    pad = MLA_QK_PAD - MLA_NOPE - MLA_ROPE
    m1 = jnp.concatenate([jnp.ones((b, s, MLA_NOPE), F32), cos, cos, jnp.zeros((b, s, pad), F32)], axis=-1)
    m2 = jnp.concatenate([jnp.zeros((b, s, MLA_NOPE), F32), sin, sin, jnp.zeros((b, s, pad), F32)], axis=-1)
    return m1, m2, jnp.swapaxes(m1, 1, 2), jnp.swapaxes(m2, 1, 2)


def kernel(x, p, positions, norm_g, w_in, mla_q_norm, w_uq, mla_kv_norm, w_ukv, conv_w, conv_b, dt_bias, a_log, d_skip, ssm_norm, w_br_a, w_br_b, w_br_c, w_out, rel_bias, w_ple, w_ple_gate, final_norm):
    b, s, d = x.shape
    depth = w_in.shape[0]
    w = _prep_weights(w_in, w_uq, w_ukv)
    m1, m2, m1t, m2t = _rope_multipliers(positions)
    pos_col = positions.astype(I32).reshape(b, s, 1)
    pos_row = positions.astype(I32).reshape(b, 1, s)
    wbr_a, wbr_b, wbr_c = w_br_a.astype(BF16), w_br_b.astype(BF16), w_br_c.astype(BF16)
    wo, wpg, wple = w_out.astype(BF16), w_ple_gate.astype(BF16), w_ple.astype(BF16)

    q_norm = mla_q_norm.astype(F32).reshape(depth, 1, -1)
    kv_norm = mla_kv_norm.astype(F32).reshape(depth, 1, -1)

    gains = norm_g.astype(F32).reshape(depth, 1, d)
    bias_tables = _dsa_bias_tables(rel_bias)
    ssd_params, ssd_expands = _ssd_params(conv_w, conv_b, dt_bias, a_log, d_skip, ssm_norm)
    x = x.astype(F32)
    for i in range(depth):
        pr = _proj_all(x, i, gains, w["proj"], w["proj_t"], (m1, m2, m1t, m2t), q_norm, kv_norm,
                       w["uq1_t"], w["uq2_t"], w["uk"], w["uv_t"], DSA_Q_BLOCK)
        o_a = _mla_attn(pr["mla_q_t"], pr["mla_k"], pr["mla_v_t1"], pr["gate_a"])
        o_b = _ssd(pr["xbc"], pr["z"], pr["small"], i, ssd_params, ssd_expands)
        o_c = _dsa(bias_tables, pr["q_c_wide"], pr["q_idx_wide"], pr["k_c"], pr["k_idx"], pr["v_t1"], pr["w_idx_t"], pos_col, pos_row,
                   pr["gate_c"], DSA_Q_BLOCK)
        x = _merge(o_a, o_b, o_c, pr["merge"], x, p, i, wbr_a, wbr_b, wbr_c, wo, wpg, wple, final_norm,
                   final=i == depth - 1)
    return x
```
